```python
import math, functools
import jax, jax.numpy as jnp
from jax import lax
import numpy as np

D_MODEL = 2048
BATCH = 8
SEQ = 4096
DEPTH = 1

CHUNK = 64
D_LRU = 1024
LRU_HEADS = 16
LRU_HEAD_DIM = D_LRU // LRU_HEADS
LRU_CONV = 4
LRU_C = 8.0
D_SC = 1024
SC_CONV = 3
D_FF = 5632
FFN_CONV = 3
N_BRANCH = 2
EPS = 1e-6
IN_COLS = 2 * D_LRU + 3 * D_SC + N_BRANCH * D_MODEL

kernel_name = "hybrid_rglru_shortconv_convffn_block"


def rmsnorm(x, g):
    xf = x.astype(jnp.float32)
    y = xf * lax.rsqrt(jnp.mean(xf * xf, axis=-1, keepdims=True) + EPS)
    return (y * g.astype(jnp.float32)).astype(x.dtype)


def causal_dwconv(x, w):
    k_w = w.shape[0]
    s = x.shape[1]
    xp = jnp.pad(x, ((0, 0), (k_w - 1, 0), (0, 0)))
    y = xp[:, 0:s] * w[0]
    for k in range(1, k_w):
        y = y + xp[:, k:k + s] * w[k]
    return y


def _lin_combine(left, right):
    a1, b1 = left
    a2, b2 = right
    return a1 * a2, a2 * b1 + b2


def rg_lru(x, w_a, b_a, w_x, b_x, lam):
    bsz, s, d = x.shape
    xf = x.astype(jnp.float32)
    xh = xf.reshape(bsz, s, LRU_HEADS, LRU_HEAD_DIM)
    r = jax.nn.sigmoid(jnp.einsum('bshi,hij->bshj', xh, w_a.astype(jnp.float32)).reshape(bsz, s, d) + b_a.astype(jnp.float32))
    i = jax.nn.sigmoid(jnp.einsum('bshi,hij->bshj', xh, w_x.astype(jnp.float32)).reshape(bsz, s, d) + b_x.astype(jnp.float32))
    log_a = -LRU_C * jax.nn.softplus(-lam.astype(jnp.float32)) * r
    a = jnp.exp(log_a)
    u = jnp.sqrt(-jnp.expm1(2.0 * log_a)) * (i * xf)
    n_chunks = s // CHUNK
    a_c = a.reshape(bsz, n_chunks, CHUNK, d).transpose(1, 0, 2, 3)
    u_c = u.reshape(bsz, n_chunks, CHUNK, d).transpose(1, 0, 2, 3)

    def step(h0, inp):
        ac, uc = inp
        a_cum, b_cum = lax.associative_scan(_lin_combine, (ac, uc), axis=1)
        h = a_cum * h0[:, None, :] + b_cum
        return h[:, -1], h

    h_init = jnp.zeros((bsz, d), jnp.float32)
    _, hs = lax.scan(step, h_init, (a_c, u_c))
    return hs.transpose(1, 0, 2, 3).reshape(bsz, s, d).astype(x.dtype)


def _fwd_setup_inputs(seed: int = 0) -> dict:
    key = jax.random.key(seed)
    ks = jax.random.split(key, 20)
    f32 = jnp.float32
    nrm = lambda k, shp, fan: jax.random.normal(k, shp, f32) * (fan ** -0.5)
    a0 = jax.random.uniform(ks[9], (D_LRU,), f32, 0.9, 0.999)
    return {
        "x": jax.random.normal(ks[0], (BATCH, SEQ, D_MODEL), f32),
        "g_mix": 1.0 + 0.02 * jax.random.normal(ks[1], (D_MODEL,), f32),
        "w_in": nrm(ks[2], (D_MODEL, IN_COLS), D_MODEL),
        "lru_conv_w": nrm(ks[3], (LRU_CONV, D_LRU), LRU_CONV),
        "lru_conv_b": 0.01 * jax.random.normal(ks[4], (D_LRU,), f32),
        "lru_wa": nrm(ks[5], (LRU_HEADS, LRU_HEAD_DIM, LRU_HEAD_DIM), LRU_HEAD_DIM),
        "lru_ba": 0.01 * jax.random.normal(ks[6], (D_LRU,), f32),
        "lru_wx": nrm(ks[7], (LRU_HEADS, LRU_HEAD_DIM, LRU_HEAD_DIM), LRU_HEAD_DIM),
        "lru_bx": 0.01 * jax.random.normal(ks[8], (D_LRU,), f32),
        "lru_lambda": jnp.log(a0) - jnp.log1p(-a0),
        "lru_w_out": nrm(ks[10], (D_LRU, D_MODEL), D_LRU),
        "sc_conv_w": nrm(ks[11], (SC_CONV, D_SC), SC_CONV),
        "sc_w_out": nrm(ks[12], (D_SC, D_MODEL), D_SC),
        "w_o": nrm(ks[13], (D_MODEL, D_MODEL), D_MODEL),
        "g_ffn": 1.0 + 0.02 * jax.random.normal(ks[14], (D_MODEL,), f32),
        "ffn_w_up": nrm(ks[15], (D_MODEL, 2 * D_FF), D_MODEL),
        "ffn_conv_w": nrm(ks[16], (FFN_CONV, 2 * D_FF), FFN_CONV),
        "ffn_w_down": nrm(ks[17], (D_FF, D_MODEL), D_FF),
        "g_final": 1.0 + 0.02 * jax.random.normal(ks[18], (D_MODEL,), f32),
    }


def _fwd_reference(x, g_mix, w_in, lru_conv_w, lru_conv_b, lru_wa, lru_ba, lru_wx, lru_bx,
              lru_lambda, lru_w_out, sc_conv_w, sc_w_out, w_o, g_ffn, ffn_w_up,
              ffn_conv_w, ffn_w_down, g_final):
    for _ in range(DEPTH):
        h = rmsnorm(x, g_mix)
        p = h @ w_in
        o = 0
        lru_x = p[..., o:o + D_LRU]; o += D_LRU
        lru_gate = p[..., o:o + D_LRU]; o += D_LRU
        sc_b = p[..., o:o + D_SC]; o += D_SC
        sc_c = p[..., o:o + D_SC]; o += D_SC
        sc_v = p[..., o:o + D_SC]; o += D_SC
        gate_lru = p[..., o:o + D_MODEL]; o += D_MODEL
        gate_sc = p[..., o:o + D_MODEL]

        xc = causal_dwconv(lru_x, lru_conv_w) + lru_conv_b
        y_lru = rg_lru(xc, lru_wa, lru_ba, lru_wx, lru_bx, lru_lambda)
        y_lru = (jax.nn.gelu(lru_gate) * y_lru) @ lru_w_out

        y_sc = (sc_b * causal_dwconv(sc_c * sc_v, sc_conv_w)) @ sc_w_out

        merged = jax.nn.sigmoid(gate_lru) * y_lru + jax.nn.sigmoid(gate_sc) * y_sc
        x = x + merged @ w_o

        h = rmsnorm(x, g_ffn)
        u = causal_dwconv(h @ ffn_w_up, ffn_conv_w)
        ff_gate, ff_val = u[..., :D_FF], u[..., D_FF:]
        x = x + (jax.nn.silu(ff_gate) * ff_val) @ ffn_w_down
    return rmsnorm(x, g_final)


import jax as _jax
import jax.numpy as _jnp

TWIN_FORMAT = 'train_step'
FWD_PARAMS = ['x', 'g_mix', 'w_in', 'lru_conv_w', 'lru_conv_b', 'lru_wa', 'lru_ba', 'lru_wx', 'lru_bx', 'lru_lambda', 'lru_w_out', 'sc_conv_w', 'sc_w_out', 'w_o', 'g_ffn', 'ffn_w_up', 'ffn_conv_w', 'ffn_w_down', 'g_final']
TWIN_WEIGHTS = ['g_mix', 'w_in', 'lru_conv_w', 'lru_conv_b', 'lru_wa', 'lru_ba', 'lru_wx', 'lru_bx', 'lru_lambda', 'lru_w_out', 'sc_conv_w', 'sc_w_out', 'w_o', 'g_ffn', 'ffn_w_up', 'ffn_conv_w', 'ffn_w_down', 'g_final']
TWIN_DIFF_INPUT = 'x'
TWIN_INPUTS = ['x', 'g_mix', 'w_in', 'lru_conv_w', 'lru_conv_b', 'lru_wa', 'lru_ba', 'lru_wx', 'lru_bx', 'lru_lambda', 'lru_w_out', 'sc_conv_w', 'sc_w_out', 'w_o', 'g_ffn', 'ffn_w_up', 'ffn_conv_w', 'ffn_w_down', 'g_final', 'loss_target', 'm_g_mix', 'm_w_in', 'm_lru_conv_w', 'm_lru_conv_b', 'm_lru_wa', 'm_lru_ba', 'm_lru_wx', 'm_lru_bx', 'm_lru_lambda', 'm_lru_w_out', 'm_sc_conv_w', 'm_sc_w_out', 'm_w_o', 'm_g_ffn', 'm_ffn_w_up', 'm_ffn_conv_w', 'm_ffn_w_down', 'm_g_final', 'v_g_mix', 'v_w_in', 'v_lru_conv_w', 'v_lru_conv_b', 'v_lru_wa', 'v_lru_ba', 'v_lru_wx', 'v_lru_bx', 'v_lru_lambda', 'v_lru_w_out', 'v_sc_conv_w', 'v_sc_w_out', 'v_w_o', 'v_g_ffn', 'v_ffn_w_up', 'v_ffn_conv_w', 'v_ffn_w_down', 'v_g_final']
TWIN_OUTPUTS = ['loss', 'grad_x', 'grad_g_mix', 'grad_w_in', 'grad_lru_conv_w', 'grad_lru_conv_b', 'grad_lru_wa', 'grad_lru_ba', 'grad_lru_wx', 'grad_lru_bx', 'grad_lru_lambda', 'grad_lru_w_out', 'grad_sc_conv_w', 'grad_sc_w_out', 'grad_w_o', 'grad_g_ffn', 'grad_ffn_w_up', 'grad_ffn_conv_w', 'grad_ffn_w_down', 'grad_g_final', 'delta_g_mix', 'delta_w_in', 'delta_lru_conv_w', 'delta_lru_conv_b', 'delta_lru_wa', 'delta_lru_ba', 'delta_lru_wx', 'delta_lru_bx', 'delta_lru_lambda', 'delta_lru_w_out', 'delta_sc_conv_w', 'delta_sc_w_out', 'delta_w_o', 'delta_g_ffn', 'delta_ffn_w_up', 'delta_ffn_conv_w', 'delta_ffn_w_down', 'delta_g_final', 'new_m_g_mix', 'new_m_w_in', 'new_m_lru_conv_w', 'new_m_lru_conv_b', 'new_m_lru_wa', 'new_m_lru_ba', 'new_m_lru_wx', 'new_m_lru_bx', 'new_m_lru_lambda', 'new_m_lru_w_out', 'new_m_sc_conv_w', 'new_m_sc_w_out', 'new_m_w_o', 'new_m_g_ffn', 'new_m_ffn_w_up', 'new_m_ffn_conv_w', 'new_m_ffn_w_down', 'new_m_g_final', 'new_v_g_mix', 'new_v_w_in', 'new_v_lru_conv_w', 'new_v_lru_conv_b', 'new_v_lru_wa', 'new_v_lru_ba', 'new_v_lru_wx', 'new_v_lru_bx', 'new_v_lru_lambda', 'new_v_lru_w_out', 'new_v_sc_conv_w', 'new_v_sc_w_out', 'new_v_w_o', 'new_v_g_ffn', 'new_v_ffn_w_up', 'new_v_ffn_conv_w', 'new_v_ffn_w_down', 'new_v_g_final']
TWIN_LEAF_KINDS = {'loss': 'loss', 'grad_x': 'grad_x', 'grad_g_mix': 'grad_w', 'grad_w_in': 'grad_w', 'grad_lru_conv_w': 'grad_w', 'grad_lru_conv_b': 'grad_w', 'grad_lru_wa': 'grad_w', 'grad_lru_ba': 'grad_w', 'grad_lru_wx': 'grad_w', 'grad_lru_bx': 'grad_w', 'grad_lru_lambda': 'grad_w', 'grad_lru_w_out': 'grad_w', 'grad_sc_conv_w': 'grad_w', 'grad_sc_w_out': 'grad_w', 'grad_w_o': 'grad_w', 'grad_g_ffn': 'grad_w', 'grad_ffn_w_up': 'grad_w', 'grad_ffn_conv_w': 'grad_w', 'grad_ffn_w_down': 'grad_w', 'grad_g_final': 'grad_w', 'delta_g_mix': 'delta_w', 'delta_w_in': 'delta_w', 'delta_lru_conv_w': 'delta_w', 'delta_lru_conv_b': 'delta_w', 'delta_lru_wa': 'delta_w', 'delta_lru_ba': 'delta_w', 'delta_lru_wx': 'delta_w', 'delta_lru_bx': 'delta_w', 'delta_lru_lambda': 'delta_w', 'delta_lru_w_out': 'delta_w', 'delta_sc_conv_w': 'delta_w', 'delta_sc_w_out': 'delta_w', 'delta_w_o': 'delta_w', 'delta_g_ffn': 'delta_w', 'delta_ffn_w_up': 'delta_w', 'delta_ffn_conv_w': 'delta_w', 'delta_ffn_w_down': 'delta_w', 'delta_g_final': 'delta_w', 'new_m_g_mix': 'new_m', 'new_m_w_in': 'new_m', 'new_m_lru_conv_w': 'new_m', 'new_m_lru_conv_b': 'new_m', 'new_m_lru_wa': 'new_m', 'new_m_lru_ba': 'new_m', 'new_m_lru_wx': 'new_m', 'new_m_lru_bx': 'new_m', 'new_m_lru_lambda': 'new_m', 'new_m_lru_w_out': 'new_m', 'new_m_sc_conv_w': 'new_m', 'new_m_sc_w_out': 'new_m', 'new_m_w_o': 'new_m', 'new_m_g_ffn': 'new_m', 'new_m_ffn_w_up': 'new_m', 'new_m_ffn_conv_w': 'new_m', 'new_m_ffn_w_down': 'new_m', 'new_m_g_final': 'new_m', 'new_v_g_mix': 'new_v', 'new_v_w_in': 'new_v', 'new_v_lru_conv_w': 'new_v', 'new_v_lru_conv_b': 'new_v', 'new_v_lru_wa': 'new_v', 'new_v_lru_ba': 'new_v', 'new_v_lru_wx': 'new_v', 'new_v_lru_bx': 'new_v', 'new_v_lru_lambda': 'new_v', 'new_v_lru_w_out': 'new_v', 'new_v_sc_conv_w': 'new_v', 'new_v_sc_w_out': 'new_v', 'new_v_w_o': 'new_v', 'new_v_g_ffn': 'new_v', 'new_v_ffn_w_up': 'new_v', 'new_v_ffn_conv_w': 'new_v', 'new_v_ffn_w_down': 'new_v', 'new_v_g_final': 'new_v'}


def _forward(args):
    return _fwd_reference(*[args[k] for k in FWD_PARAMS])


def _output_shape():
    def fwd():
        inp = _fwd_setup_inputs(0)
        return _fwd_reference(*[inp[k] for k in FWD_PARAMS])
    out = _jax.eval_shape(fwd)
    return out.shape, out.dtype

N_MICROBATCH = 1
ADAM_LR = 0.001
ADAM_B1 = 0.9
ADAM_B2 = 0.999
ADAM_EPS = 1e-08
ADAM_WD = 0.01
ADAM_STEP = 10
PER_EXAMPLE_BATCH_AXIS = {'x': 0, 'loss_target': 0}
SHARED_INPUTS = []
_WEIGHT_DTYPES = {'g_mix': _jnp.float32, 'w_in': _jnp.float32, 'lru_conv_w': _jnp.float32, 'lru_conv_b': _jnp.float32, 'lru_wa': _jnp.float32, 'lru_ba': _jnp.float32, 'lru_wx': _jnp.float32, 'lru_bx': _jnp.float32, 'lru_lambda': _jnp.float32, 'lru_w_out': _jnp.float32, 'sc_conv_w': _jnp.float32, 'sc_w_out': _jnp.float32, 'w_o': _jnp.float32, 'g_ffn': _jnp.float32, 'ffn_w_up': _jnp.float32, 'ffn_conv_w': _jnp.float32, 'ffn_w_down': _jnp.float32, 'g_final': _jnp.float32}
MOMENT_SCALE = {'g_mix': 8.807240e-02, 'w_in': 4.013606e-02, 'lru_conv_w': 2.663255e-02, 'lru_conv_b': 1.228471e-01, 'lru_wa': 6.239434e-03, 'lru_ba': 5.547240e-03, 'lru_wx': 1.077991e-02, 'lru_bx': 8.532089e-03, 'lru_lambda': 1.121268e-02, 'lru_w_out': 1.698484e-02, 'sc_conv_w': 6.656722e-02, 'sc_w_out': 4.570205e-02, 'w_o': 4.862959e-02, 'g_ffn': 6.034882e-02, 'ffn_w_up': 2.561453e-02, 'ffn_conv_w': 2.547617e-02, 'ffn_w_down': 4.177259e-02, 'g_final': 1.600678e+01}


def _to_microbatches(a, axis):
    t = _jnp.moveaxis(a, axis, 0)
    t = t.reshape((N_MICROBATCH, t.shape[0] // N_MICROBATCH) + t.shape[1:])
    return _jnp.moveaxis(t, 1, axis + 1)


def setup_inputs(seed: int = 0) -> dict:
    inp = _fwd_setup_inputs(seed)
    key = _jax.random.fold_in(_jax.random.key(seed), 7919)
    shape, _ = _output_shape()
    out = dict(inp)
    out["loss_target"] = _jax.random.normal(_jax.random.fold_in(key, 0), shape, _jnp.float32)
    for i, name in enumerate(TWIN_WEIGHTS):
        w = inp[name].astype(_jnp.float32)
        if MOMENT_SCALE is None:
            s = _jnp.sqrt(_jnp.mean(_jnp.square(w)) + 1e-30)
        else:
            s = MOMENT_SCALE[name]
        km, kv = _jax.random.split(_jax.random.fold_in(key, i + 1))
        out[name] = w
        out["m_" + name] = s * _jax.random.normal(km, w.shape, _jnp.float32)
        out["v_" + name] = (s * s) * _jax.random.uniform(kv, w.shape, _jnp.float32, 0.5, 1.5)
    if N_MICROBATCH > 1:
        for name, axis in PER_EXAMPLE_BATCH_AXIS.items():
            out[name] = _to_microbatches(out[name], axis)
    return {'x': out['x'], 'g_mix': out['g_mix'], 'w_in': out['w_in'], 'lru_conv_w': out['lru_conv_w'], 'lru_conv_b': out['lru_conv_b'], 'lru_wa': out['lru_wa'], 'lru_ba': out['lru_ba'], 'lru_wx': out['lru_wx'], 'lru_bx': out['lru_bx'], 'lru_lambda': out['lru_lambda'], 'lru_w_out': out['lru_w_out'], 'sc_conv_w': out['sc_conv_w'], 'sc_w_out': out['sc_w_out'], 'w_o': out['w_o'], 'g_ffn': out['g_ffn'], 'ffn_w_up': out['ffn_w_up'], 'ffn_conv_w': out['ffn_conv_w'], 'ffn_w_down': out['ffn_w_down'], 'g_final': out['g_final'], 'loss_target': out['loss_target'], 'm_g_mix': out['m_g_mix'], 'm_w_in': out['m_w_in'], 'm_lru_conv_w': out['m_lru_conv_w'], 'm_lru_conv_b': out['m_lru_conv_b'], 'm_lru_wa': out['m_lru_wa'], 'm_lru_ba': out['m_lru_ba'], 'm_lru_wx': out['m_lru_wx'], 'm_lru_bx': out['m_lru_bx'], 'm_lru_lambda': out['m_lru_lambda'], 'm_lru_w_out': out['m_lru_w_out'], 'm_sc_conv_w': out['m_sc_conv_w'], 'm_sc_w_out': out['m_sc_w_out'], 'm_w_o': out['m_w_o'], 'm_g_ffn': out['m_g_ffn'], 'm_ffn_w_up': out['m_ffn_w_up'], 'm_ffn_conv_w': out['m_ffn_conv_w'], 'm_ffn_w_down': out['m_ffn_w_down'], 'm_g_final': out['m_g_final'], 'v_g_mix': out['v_g_mix'], 'v_w_in': out['v_w_in'], 'v_lru_conv_w': out['v_lru_conv_w'], 'v_lru_conv_b': out['v_lru_conv_b'], 'v_lru_wa': out['v_lru_wa'], 'v_lru_ba': out['v_lru_ba'], 'v_lru_wx': out['v_lru_wx'], 'v_lru_bx': out['v_lru_bx'], 'v_lru_lambda': out['v_lru_lambda'], 'v_lru_w_out': out['v_lru_w_out'], 'v_sc_conv_w': out['v_sc_conv_w'], 'v_sc_w_out': out['v_sc_w_out'], 'v_w_o': out['v_w_o'], 'v_g_ffn': out['v_g_ffn'], 'v_ffn_w_up': out['v_ffn_w_up'], 'v_ffn_conv_w': out['v_ffn_conv_w'], 'v_ffn_w_down': out['v_ffn_w_down'], 'v_g_final': out['v_g_final']}


def _loss(weights, diff, rest, loss_target):
    with _jax.named_scope("forward"):
        args = {**rest, TWIN_DIFF_INPUT: diff, **{k: w.astype(_WEIGHT_DTYPES[k]) for k, w in weights.items()}}
        y = _forward(args)
    with _jax.named_scope("loss_head"):
        err = _jnp.square(y.astype(_jnp.float32) - loss_target)
        return 0.5 * _jnp.sum(_jnp.mean(err, axis=-1)) if err.ndim else 0.5 * err


def _adamw(w, g, m, v):
    m = ADAM_B1 * m + (1.0 - ADAM_B1) * g
    v = ADAM_B2 * v + (1.0 - ADAM_B2) * _jnp.square(g)
    m_hat = m / (1.0 - ADAM_B1 ** ADAM_STEP)
    v_hat = v / (1.0 - ADAM_B2 ** ADAM_STEP)
    delta = -ADAM_LR * (m_hat / (_jnp.sqrt(v_hat) + ADAM_EPS) + ADAM_WD * w)
    return delta, m, v


def reference(x, g_mix, w_in, lru_conv_w, lru_conv_b, lru_wa, lru_ba, lru_wx, lru_bx, lru_lambda, lru_w_out, sc_conv_w, sc_w_out, w_o, g_ffn, ffn_w_up, ffn_conv_w, ffn_w_down, g_final, loss_target, m_g_mix, m_w_in, m_lru_conv_w, m_lru_conv_b, m_lru_wa, m_lru_ba, m_lru_wx, m_lru_bx, m_lru_lambda, m_lru_w_out, m_sc_conv_w, m_sc_w_out, m_w_o, m_g_ffn, m_ffn_w_up, m_ffn_conv_w, m_ffn_w_down, m_g_final, v_g_mix, v_w_in, v_lru_conv_w, v_lru_conv_b, v_lru_wa, v_lru_ba, v_lru_wx, v_lru_bx, v_lru_lambda, v_lru_w_out, v_sc_conv_w, v_sc_w_out, v_w_o, v_g_ffn, v_ffn_w_up, v_ffn_conv_w, v_ffn_w_down, v_g_final):
    given = dict(x=x, g_mix=g_mix, w_in=w_in, lru_conv_w=lru_conv_w, lru_conv_b=lru_conv_b, lru_wa=lru_wa, lru_ba=lru_ba, lru_wx=lru_wx, lru_bx=lru_bx, lru_lambda=lru_lambda, lru_w_out=lru_w_out, sc_conv_w=sc_conv_w, sc_w_out=sc_w_out, w_o=w_o, g_ffn=g_ffn, ffn_w_up=ffn_w_up, ffn_conv_w=ffn_conv_w, ffn_w_down=ffn_w_down, g_final=g_final, loss_target=loss_target, m_g_mix=m_g_mix, m_w_in=m_w_in, m_lru_conv_w=m_lru_conv_w, m_lru_conv_b=m_lru_conv_b, m_lru_wa=m_lru_wa, m_lru_ba=m_lru_ba, m_lru_wx=m_lru_wx, m_lru_bx=m_lru_bx, m_lru_lambda=m_lru_lambda, m_lru_w_out=m_lru_w_out, m_sc_conv_w=m_sc_conv_w, m_sc_w_out=m_sc_w_out, m_w_o=m_w_o, m_g_ffn=m_g_ffn, m_ffn_w_up=m_ffn_w_up, m_ffn_conv_w=m_ffn_conv_w, m_ffn_w_down=m_ffn_w_down, m_g_final=m_g_final, v_g_mix=v_g_mix, v_w_in=v_w_in, v_lru_conv_w=v_lru_conv_w, v_lru_conv_b=v_lru_conv_b, v_lru_wa=v_lru_wa, v_lru_ba=v_lru_ba, v_lru_wx=v_lru_wx, v_lru_bx=v_lru_bx, v_lru_lambda=v_lru_lambda, v_lru_w_out=v_lru_w_out, v_sc_conv_w=v_sc_conv_w, v_sc_w_out=v_sc_w_out, v_w_o=v_w_o, v_g_ffn=v_g_ffn, v_ffn_w_up=v_ffn_w_up, v_ffn_conv_w=v_ffn_conv_w, v_ffn_w_down=v_ffn_w_down, v_g_final=v_g_final)
    weights = {n: given[n] for n in TWIN_WEIGHTS}
    shared = {n: given[n] for n in SHARED_INPUTS}
    per_example = {n: given[n] for n in ['x']}
    grad_fn = _jax.value_and_grad(_loss, argnums=(0, 1))

    def one_microbatch(ex, loss_target):
        ex = dict(ex)
        diff = ex.pop(TWIN_DIFF_INPUT)
        return grad_fn(weights, diff, {**shared, **ex}, loss_target)

    if N_MICROBATCH == 1:
        loss, (grad_w, grad_x) = one_microbatch(per_example, given["loss_target"])
    else:
        def body(carry, xs):
            loss_sum, grad_sum = carry
            l_k, (gw_k, gx_k) = one_microbatch(xs[0], xs[1])
            with _jax.named_scope("update"):
                return (loss_sum + l_k, _jax.tree.map(_jnp.add, grad_sum, gw_k)), gx_k

        init = (_jnp.zeros((), _jnp.float32), _jax.tree.map(_jnp.zeros_like, weights))
        (loss, grad_w), grad_x = _jax.lax.scan(body, init, (per_example, given["loss_target"]))
    with _jax.named_scope("update"):
        delta_w, new_m, new_v = {}, {}, {}
        for n in TWIN_WEIGHTS:
            delta_w[n], new_m[n], new_v[n] = _adamw(weights[n], grad_w[n], given["m_" + n], given["v_" + n])
    return (loss, grad_x, *[grad_w[n] for n in TWIN_WEIGHTS], *[delta_w[n] for n in TWIN_WEIGHTS],
            *[new_m[n] for n in TWIN_WEIGHTS], *[new_v[n] for n in TWIN_WEIGHTS])
```

```python
import functools
import math

import jax
import jax.numpy as jnp
from jax import lax
from jax.experimental import pallas as pl
from jax.experimental.pallas import tpu as pltpu

F32 = jnp.float32
BF16 = jnp.bfloat16

LANE = 128
SUBLANE = 8
BF16_ROWS = 16
VMEM_BYTES_V7X = 64 * 1024 * 1024
VMEM_BUDGET = VMEM_BYTES_V7X - 8 * 1024 * 1024

EPS = 1e-6
LRU_C = 8.0
ADAM_LR = 0.001
ADAM_B1 = 0.9
ADAM_B2 = 0.999
ADAM_EPS = 1e-08
ADAM_WD = 0.01
ADAM_STEP = 10

N_CHIPS = 4
N_DEV = 8
MESH = pl.DeviceIdType.MESH
ANY = pl.BlockSpec(memory_space=pl.ANY)
VMEM_SPEC = pl.BlockSpec(memory_space=pltpu.VMEM)


def _pick(n, cap, mult=LANE):
    best = None
    d = mult
    while d <= min(n, cap):
        if n % d == 0:
            best = d
        d += mult
    return n if best is None else best


def _cparams(semantics, block_bytes):
    limit = min(VMEM_BUDGET, max(32 * 1024 * 1024, int(block_bytes * 1.25) + (4 << 20)))
    return pltpu.CompilerParams(dimension_semantics=semantics, vmem_limit_bytes=limit)


def _nbytes(shape, dtype):
    return math.prod(shape) * jnp.dtype(dtype).itemsize


def _sigmoid(z):
    return 1.0 / (1.0 + jnp.exp(-z))


def _softplus(z):
    e = jnp.exp(-jnp.abs(z))
    u = 1.0 + e
    log1p = jnp.where(u == 1.0, e, jnp.log(u) * (e / (u - 1.0)))
    return jnp.maximum(z, 0.0) + log1p


def _neg_expm1(z):
    small = z * (1.0 + z * (0.5 + z * (1.0 / 6.0 + z * (1.0 / 24.0))))
    return -jnp.where(jnp.abs(z) < 0.03, small, jnp.exp(z) - 1.0)


_GELU_K = math.sqrt(2.0 / math.pi)
_GELU_C = 0.044715


def _gelu_and_grad(z):
    z2 = z * z
    th = jnp.tanh(_GELU_K * (z + _GELU_C * z2 * z))
    val = 0.5 * z * (1.0 + th)
    grad = 0.5 * (1.0 + th) + 0.5 * z * (1.0 - th * th) * (_GELU_K * (1.0 + 3.0 * _GELU_C * z2))
    return val, grad


def _rows_before(cat, k):
    if k == 0:
        return cat[SUBLANE:, :]
    return pltpu.roll(cat, k, 0)[SUBLANE:, :]


def _rows_after(cat, k):
    n = cat.shape[0]
    if k == 0:
        return cat[:n - SUBLANE, :]
    return pltpu.roll(cat, n - k, 0)[:n - SUBLANE, :]


def _conv_fwd(cat, w, width):
    y = _rows_before(cat, width - 1) * w[0:1, :]
    for k in range(1, width):
        y = y + _rows_before(cat, width - 1 - k) * w[k:k + 1, :]
    return y


def _conv_bwd_input(cat, w, width):
    dx = _rows_after(cat, width - 1) * w[0:1, :]
    for k in range(1, width):
        dx = dx + _rows_after(cat, width - 1 - k) * w[k:k + 1, :]
    return dx


def _conv_bwd_weight(dw_ref, dy, catx, width):
    for k in range(width):
        dw_ref[k:k + 1, :] += jnp.sum(dy * _rows_before(catx, width - 1 - k), axis=0, keepdims=True)


def _scan_tiles(a_ref, b_ref, out_ref, carry0, n_rows, reverse):
    cols = a_ref.shape[1]
    row = lax.broadcasted_iota(jnp.int32, (SUBLANE, cols), 0)
    n_tiles = n_rows // SUBLANE

    def step(j, carry):
        tile = (n_tiles - 1 - j) if reverse else j
        off = pl.multiple_of(tile * SUBLANE, SUBLANE)
        a = a_ref[pl.ds(off, SUBLANE), :]
        b = b_ref[pl.ds(off, SUBLANE), :]
        for s in (1, 2, 4):
            if reverse:
                keep = row < SUBLANE - s
                shift = SUBLANE - s
            else:
                keep = row >= s
                shift = s
            a_sh = jnp.where(keep, pltpu.roll(a, shift, 0), 1.0)
            b_sh = jnp.where(keep, pltpu.roll(b, shift, 0), 0.0)
            b = a * b_sh + b
            a = a * a_sh
        out = a * carry + b
        out_ref[pl.ds(off, SUBLANE), :] = out
        return out[0:1, :] if reverse else out[SUBLANE - 1:SUBLANE, :]

    return lax.fori_loop(0, n_tiles, step, carry0)


def _mm(a, b, mode, out_dtype, res=None, name=None):
    if mode == "nn":
        (m, k), (k2, n) = a.shape, b.shape
        dims = (((1,), (0,)), ((), ()))
    elif mode == "nt":
        (m, k), (n, k2) = a.shape, b.shape
        dims = (((1,), (1,)), ((), ()))
    else:
        (k, m), (k2, n) = a.shape, b.shape
        dims = (((0,), (0,)), ((), ()))
    assert k == k2 and a.dtype == BF16 and b.dtype == BF16
    bm = _pick(m, 1024)
    bn = _pick(n, 1024)
    bk = _pick(k, 2048)
    nk = k // bk
    out_bytes = jnp.dtype(out_dtype).itemsize

    def est(bk_):
        e = 2 * (bm * bk_ + bk_ * bn) * 2 + 2 * bm * bn * out_bytes
        if nk > 1 or k // bk_ > 1:
            e += bm * bn * 4
        if res is not None:
            e += 2 * bm * bn * 4
        return e

    while est(bk) > 40 * 1024 * 1024 and bk % (2 * LANE) == 0 and k % (bk // 2) == 0:
        bk //= 2
        nk = k // bk

    if mode == "tn":
        a_spec = pl.BlockSpec((bk, bm), lambda i, j, kk: (kk, i))
    else:
        a_spec = pl.BlockSpec((bm, bk), lambda i, j, kk: (i, kk))
    if mode == "nt":
        b_spec = pl.BlockSpec((bn, bk), lambda i, j, kk: (j, kk))
    else:
        b_spec = pl.BlockSpec((bk, bn), lambda i, j, kk: (kk, j))
    o_spec = pl.BlockSpec((bm, bn), lambda i, j, kk: (i, j))
    in_specs = [a_spec, b_spec]
    operands = [a, b]
    if res is not None:
        in_specs.append(o_spec)
        operands.append(res)
    has_res = res is not None

    def body(*refs):
        a_ref, b_ref = refs[0], refs[1]
        res_ref = refs[2] if has_res else None
        o_ref = refs[3] if has_res else refs[2]
        part = lax.dot_general(a_ref[...], b_ref[...], dims, preferred_element_type=F32)
        if nk == 1:
            if has_res:
                part = part + res_ref[...]
            o_ref[...] = part.astype(o_ref.dtype)
            return
        acc_ref = refs[-1]
        kk = pl.program_id(2)

        @pl.when(kk == 0)
        def _():
            acc_ref[...] = part

        @pl.when(kk > 0)
        def _():
            acc_ref[...] += part

        @pl.when(kk == nk - 1)
        def _():
            total = acc_ref[...]
            if has_res:
                total = total + res_ref[...]
            o_ref[...] = total.astype(o_ref.dtype)

    scratch = [pltpu.VMEM((bm, bn), F32)] if nk > 1 else []
    return pl.pallas_call(
        body,
        out_shape=jax.ShapeDtypeStruct((m, n), out_dtype),
        grid=(m // bm, n // bn, nk),
        in_specs=in_specs,
        out_specs=o_spec,
        scratch_shapes=scratch,
        compiler_params=_cparams(("parallel", "parallel", "arbitrary"), est(bk)),
        name=name,
    )(*operands)


def _rms_fwd(x, g, name):
    t, d = x.shape
    tb = _pick(t, 512, SUBLANE)

    def body(x_ref, g_ref, h_ref):
        xv = x_ref[...]
        r = lax.rsqrt(jnp.mean(xv * xv, axis=-1, keepdims=True) + EPS)
        h_ref[...] = ((xv * r) * g_ref[...]).astype(BF16)

    blk = pl.BlockSpec((tb, d), lambda i: (i, 0))
    return pl.pallas_call(
        body,
        out_shape=jax.ShapeDtypeStruct((t, d), BF16),
        grid=(t // tb,),
        in_specs=[blk, pl.BlockSpec((1, d), lambda i: (0, 0))],
        out_specs=blk,
        compiler_params=_cparams(("parallel",), 2 * tb * d * 6),
        name=name,
    )(x, g.reshape(1, d))


def _rms_bwd(x, g, dh, dres, name, want_bf16):
    t, d = x.shape
    tb = _pick(t, 256, SUBLANE)

    def body(x_ref, g_ref, dh_ref, dres_ref, *outs):
        dx_ref, dg_ref = outs[0], outs[-1]
        xv = x_ref[...]
        r = lax.rsqrt(jnp.mean(xv * xv, axis=-1, keepdims=True) + EPS)
        xhat = xv * r
        dhv = dh_ref[...]
        dxhat = dhv * g_ref[...]
        dx = dres_ref[...] + r * (dxhat - xhat * jnp.mean(dxhat * xhat, axis=-1, keepdims=True))
        dx_ref[...] = dx
        if want_bf16:
            outs[1][...] = dx.astype(BF16)

        @pl.when(pl.program_id(0) == 0)
        def _():
            dg_ref[...] = jnp.zeros_like(dg_ref)

        dg_ref[...] += jnp.sum(dhv * xhat, axis=0, keepdims=True)

    blk = pl.BlockSpec((tb, d), lambda i: (i, 0))
    row = pl.BlockSpec((1, d), lambda i: (0, 0))
    out_shape = [jax.ShapeDtypeStruct((t, d), F32)]
    out_specs = [blk]
    if want_bf16:
        out_shape.append(jax.ShapeDtypeStruct((t, d), BF16))
        out_specs.append(blk)
    out_shape.append(jax.ShapeDtypeStruct((1, d), F32))
    out_specs.append(row)
    return pl.pallas_call(
        body,
        out_shape=out_shape,
        grid=(t // tb,),
        in_specs=[blk, row, blk, blk],
        out_specs=out_specs,
        compiler_params=_cparams(("arbitrary",), 2 * tb * d * 18),
        name=name,
    )(x, g.reshape(1, d), dh, dres)


def _loss_head(x3, g, target):
    t, d = x3.shape
    tb = _pick(t, 256, SUBLANE)

    def body(x_ref, g_ref, t_ref, loss_ref, dx_ref, dxb_ref, dg_ref):
        xv = x_ref[...]
        gv = g_ref[...]
        r = lax.rsqrt(jnp.mean(xv * xv, axis=-1, keepdims=True) + EPS)
        xhat = xv * r
        err = xhat * gv - t_ref[...]
        dy = err * (1.0 / d)
        dxhat = dy * gv
        dx = r * (dxhat - xhat * jnp.mean(dxhat * xhat, axis=-1, keepdims=True))
        dx_ref[...] = dx
        dxb_ref[...] = dx.astype(BF16)

        @pl.when(pl.program_id(0) == 0)
        def _():
            dg_ref[...] = jnp.zeros_like(dg_ref)
            loss_ref[...] = jnp.zeros_like(loss_ref)

        dg_ref[...] += jnp.sum(dy * xhat, axis=0, keepdims=True)
        per_token = jnp.mean(err * err, axis=-1, keepdims=True)
        loss_ref[...] += 0.5 * jnp.sum(per_token, axis=0, keepdims=True)

    blk = pl.BlockSpec((tb, d), lambda i: (i, 0))
    row = pl.BlockSpec((1, d), lambda i: (0, 0))
    return pl.pallas_call(
        body,
        out_shape=[jax.ShapeDtypeStruct((1, 1), F32), jax.ShapeDtypeStruct((t, d), F32),
                   jax.ShapeDtypeStruct((t, d), BF16), jax.ShapeDtypeStruct((1, d), F32)],
        grid=(t // tb,),
        in_specs=[blk, row, blk],
        out_specs=[pl.BlockSpec((1, 1), lambda i: (0, 0)), blk, blk, row],
        compiler_params=_cparams(("arbitrary",), 2 * tb * d * 14),
        name="loss_head",
    )(x3, g.reshape(1, d), target)


def _lru_gates(xc, wa, ba, wx, bx, lam):
    nn = (((1,), (0,)), ((), ()))
    xcb = xc.astype(BF16)
    r = _sigmoid(lax.dot_general(xcb, wa, nn, preferred_element_type=F32) + ba)
    i = _sigmoid(lax.dot_general(xcb, wx, nn, preferred_element_type=F32) + bx)
    cl = -LRU_C * _softplus(-lam)
    log_a = cl * r
    a = jnp.exp(log_a)
    one_minus_a2 = _neg_expm1(2.0 * log_a)
    return xcb, r, i, a, one_minus_a2, cl


def _lru_fwd(p, conv_w, conv_b, wa_bd, ba, wx_bd, bx, lam, d_lru, gc, tc):
    t = p.shape[0]
    ng = d_lru // gc
    nt = t // tc
    width = conv_w.shape[0]

    def body(lx_ref, gate_ref, cw_ref, cb_ref, wa_ref, ba_ref, wx_ref, bx_ref, lam_ref,
             y_ref, h_ref, halo, hcar, a_s, u_s):
        @pl.when(pl.program_id(1) == 0)
        def _():
            halo[...] = jnp.zeros_like(halo)
            hcar[...] = jnp.zeros_like(hcar)

        x = lx_ref[...]
        cat = jnp.concatenate([halo[...], x], axis=0)
        halo[...] = x[tc - SUBLANE:, :]
        xc = _conv_fwd(cat, cw_ref[...], width) + cb_ref[...]
        _, r, i, a, om, _ = _lru_gates(xc, wa_ref[...], ba_ref[...], wx_ref[...], bx_ref[...], lam_ref[...])
        a_s[...] = a
        u_s[...] = jnp.sqrt(om) * (i * xc)
        hcar[0:1, :] = _scan_tiles(a_s, u_s, h_ref, hcar[0:1, :], tc, reverse=False)
        gl, _ = _gelu_and_grad(gate_ref[...])
        y_ref[...] = (gl * h_ref[...]).astype(BF16)

    blk = lambda off: pl.BlockSpec((tc, gc), lambda g, s, off=off: (s, off + g))
    rowv = lambda rows: pl.BlockSpec((rows, gc), lambda g, s: (0, g))
    wspec = pl.BlockSpec((None, gc, gc), lambda g, s: (g, 0, 0))
    out_blk = pl.BlockSpec((tc, gc), lambda g, s: (s, g))
    return pl.pallas_call(
        body,
        out_shape=[jax.ShapeDtypeStruct((t, d_lru), BF16), jax.ShapeDtypeStruct((t, d_lru), F32)],
        grid=(ng, nt),
        in_specs=[blk(0), blk(ng), rowv(width), rowv(1), wspec, rowv(1), wspec, rowv(1), rowv(1)],
        out_specs=[out_blk, out_blk],
        scratch_shapes=[pltpu.VMEM((SUBLANE, gc), F32), pltpu.VMEM((SUBLANE, gc), F32),
                        pltpu.VMEM((tc, gc), F32), pltpu.VMEM((tc, gc), F32)],
        compiler_params=_cparams(("parallel", "arbitrary"), 40 * tc * gc * 4),
        name="lru_fwd",
    )(p, p, conv_w, conv_b.reshape(1, -1), wa_bd, ba.reshape(1, -1), wx_bd, bx.reshape(1, -1),
      lam.reshape(1, -1))


def _lru_bwd(p, hseq, dyp, conv_w, conv_b, wa_bd, ba, wx_bd, bx, lam, d_lru, gc, tc):
    t = p.shape[0]
    ng = d_lru // gc
    nt = t // tc
    width = conv_w.shape[0]
    halo_blocks = tc // SUBLANE
    nn = (((1,), (0,)), ((), ()))
    nt_dims = (((1,), (1,)), ((), ()))
    tn_dims = (((0,), (0,)), ((), ()))

    def body(lx_ref, lxh_ref, gate_ref, h_ref, hh_ref, dyp_ref,
             cw_ref, cb_ref, wa_ref, ba_ref, wx_ref, bx_ref, lam_ref,
             dlx_ref, dgate_ref, dcw_ref, dcb_ref, dwa_ref, dba_ref, dwx_ref, dbx_ref, dlam_ref,
             nxt_dxc, nxt_a, nxt_g, al_s, b_s, g_s):
        s = pl.program_id(1)
        first_chunk = s == nt - 1

        @pl.when(s == 0)
        def _():
            nxt_dxc[...] = jnp.zeros_like(nxt_dxc)
            nxt_a[...] = jnp.zeros_like(nxt_a)
            nxt_g[...] = jnp.zeros_like(nxt_g)
            for ref in (dcw_ref, dcb_ref, dwa_ref, dba_ref, dwx_ref, dbx_ref, dlam_ref):
                ref[...] = jnp.zeros_like(ref)

        keep = jnp.where(first_chunk, 0.0, 1.0)
        x = lx_ref[...]
        catx = jnp.concatenate([lxh_ref[...] * keep, x], axis=0)
        cw = cw_ref[...]
        xc = _conv_fwd(catx, cw, width) + cb_ref[...]
        wa = wa_ref[...]
        wx = wx_ref[...]
        lam_v = lam_ref[...]
        xcb, r, i, a, om, cl = _lru_gates(xc, wa, ba_ref[...], wx, bx_ref[...], lam_v)
        mult = jnp.sqrt(om)

        h = h_ref[...]
        hprev = _rows_before(jnp.concatenate([hh_ref[...] * keep, h], axis=0), 1)
        gl, dgl = _gelu_and_grad(gate_ref[...])
        dyp_v = dyp_ref[...]
        dgate_ref[...] = (dyp_v * h * dgl).astype(BF16)

        al_s[...] = _rows_after(jnp.concatenate([a, nxt_a[...]], axis=0), 1)
        b_s[...] = dyp_v * gl
        nxt_g[0:1, :] = _scan_tiles(al_s, b_s, g_s, nxt_g[0:1, :], tc, reverse=True)
        nxt_a[...] = a[0:SUBLANE, :]
        du = g_s[...]

        da = du * hprev
        dmult = du * (i * xc)
        di = du * mult * xc
        dxc = du * mult * i
        dlog_a = da * a - dmult * (a * a / mult)
        dlam_ref[...] += jnp.sum(dlog_a * r, axis=0, keepdims=True) * (LRU_C * _sigmoid(-lam_v))
        dza = (dlog_a * cl) * r * (1.0 - r)
        dzx = di * i * (1.0 - i)
        dba_ref[...] += jnp.sum(dza, axis=0, keepdims=True)
        dbx_ref[...] += jnp.sum(dzx, axis=0, keepdims=True)
        dzab = dza.astype(BF16)
        dzxb = dzx.astype(BF16)
        dwa_ref[...] += lax.dot_general(xcb, dzab, tn_dims, preferred_element_type=F32)
        dwx_ref[...] += lax.dot_general(xcb, dzxb, tn_dims, preferred_element_type=F32)
        dxc = dxc + lax.dot_general(dzab, wa, nt_dims, preferred_element_type=F32)
        dxc = dxc + lax.dot_general(dzxb, wx, nt_dims, preferred_element_type=F32)
        dcb_ref[...] += jnp.sum(dxc, axis=0, keepdims=True)
        _conv_bwd_weight(dcw_ref, dxc, catx, width)
        catd = jnp.concatenate([dxc, nxt_dxc[...]], axis=0)
        dlx_ref[...] = _conv_bwd_input(catd, cw, width).astype(BF16)
        nxt_dxc[...] = dxc[0:SUBLANE, :]

    rev = lambda s: nt - 1 - s
    blk = lambda off: pl.BlockSpec((tc, gc), lambda g, s, off=off: (rev(s), off + g))
    halo = lambda off: pl.BlockSpec(
        (SUBLANE, gc), lambda g, s, off=off: (jnp.maximum(rev(s) * halo_blocks - 1, 0), off + g))
    rowv = lambda rows: pl.BlockSpec((rows, gc), lambda g, s: (0, g))
    wspec = pl.BlockSpec((None, gc, gc), lambda g, s: (g, 0, 0))
    out_blk = pl.BlockSpec((tc, gc), lambda g, s: (rev(s), g))
    vec = lambda rows: jax.ShapeDtypeStruct((rows, d_lru), F32)
    wshape = jax.ShapeDtypeStruct((ng, gc, gc), F32)
    return pl.pallas_call(
        body,
        out_shape=[jax.ShapeDtypeStruct((t, d_lru), BF16), jax.ShapeDtypeStruct((t, d_lru), BF16),
                   vec(width), vec(1), wshape, vec(1), wshape, vec(1), vec(1)],
        grid=(ng, nt),
        in_specs=[blk(0), halo(0), blk(ng), blk(0), halo(0), blk(0),
                  rowv(width), rowv(1), wspec, rowv(1), wspec, rowv(1), rowv(1)],
        out_specs=[out_blk, out_blk, rowv(width), rowv(1), wspec, rowv(1), wspec, rowv(1), rowv(1)],
        scratch_shapes=[pltpu.VMEM((SUBLANE, gc), F32), pltpu.VMEM((SUBLANE, gc), F32),
                        pltpu.VMEM((SUBLANE, gc), F32),
                        pltpu.VMEM((tc, gc), F32), pltpu.VMEM((tc, gc), F32), pltpu.VMEM((tc, gc), F32)],
        compiler_params=_cparams(("parallel", "arbitrary"), 80 * tc * gc * 4),
        name="lru_bwd",
    )(p, p, p, hseq, hseq, dyp, conv_w, conv_b.reshape(1, -1), wa_bd, ba.reshape(1, -1), wx_bd,
      bx.reshape(1, -1), lam.reshape(1, -1))


def _sc_fwd(p, conv_w, col0, d_sc, cb, tc):
    t = p.shape[0]
    nc = d_sc // cb
    nt = t // tc
    width = conv_w.shape[0]
    base = col0 // cb

    def body(b_ref, c_ref, v_ref, w_ref, y_ref, halo):
        @pl.when(pl.program_id(1) == 0)
        def _():
            halo[...] = jnp.zeros_like(halo)

        cv = c_ref[...] * v_ref[...]
        cat = jnp.concatenate([halo[...], cv], axis=0)
        halo[...] = cv[tc - SUBLANE:, :]
        y_ref[...] = (b_ref[...] * _conv_fwd(cat, w_ref[...], width)).astype(BF16)

    blk = lambda slab: pl.BlockSpec((tc, cb), lambda j, s, slab=slab: (s, base + slab * nc + j))
    return pl.pallas_call(
        body,
        out_shape=jax.ShapeDtypeStruct((t, d_sc), BF16),
        grid=(nc, nt),
        in_specs=[blk(0), blk(1), blk(2), pl.BlockSpec((width, cb), lambda j, s: (0, j))],
        out_specs=pl.BlockSpec((tc, cb), lambda j, s: (s, j)),
        scratch_shapes=[pltpu.VMEM((SUBLANE, cb), F32)],
        compiler_params=_cparams(("parallel", "arbitrary"), 20 * tc * cb * 4),
        name="sc_fwd",
    )(p, p, p, conv_w)


def _sc_bwd(p, dyp, conv_w, col0, d_sc, cb, tc):
    t = p.shape[0]
    nc = d_sc // cb
    nt = t // tc
    width = conv_w.shape[0]
    base = col0 // cb
    halo_blocks = tc // SUBLANE

    def body(b_ref, c_ref, ch_ref, v_ref, vh_ref, dyp_ref, w_ref,
             db_ref, dc_ref, dv_ref, dw_ref, nxt_dq):
        s = pl.program_id(1)

        @pl.when(s == 0)
        def _():
            nxt_dq[...] = jnp.zeros_like(nxt_dq)
            dw_ref[...] = jnp.zeros_like(dw_ref)

        keep = jnp.where(s == nt - 1, 0.0, 1.0)
        cvals = c_ref[...]
        vvals = v_ref[...]
        w = w_ref[...]
        catcv = jnp.concatenate([ch_ref[...] * vh_ref[...] * keep, cvals * vvals], axis=0)
        q = _conv_fwd(catcv, w, width)
        dyp_v = dyp_ref[...]
        db_ref[...] = (dyp_v * q).astype(BF16)
        dq = dyp_v * b_ref[...]
        _conv_bwd_weight(dw_ref, dq, catcv, width)
        dcv = _conv_bwd_input(jnp.concatenate([dq, nxt_dq[...]], axis=0), w, width)
        nxt_dq[...] = dq[0:SUBLANE, :]
        dc_ref[...] = (dcv * vvals).astype(BF16)
        dv_ref[...] = (dcv * cvals).astype(BF16)

    rev = lambda s: nt - 1 - s
    blk = lambda slab: pl.BlockSpec((tc, cb), lambda j, s, slab=slab: (rev(s), base + slab * nc + j))
    halo = lambda slab: pl.BlockSpec(
        (SUBLANE, cb),
        lambda j, s, slab=slab: (jnp.maximum(rev(s) * halo_blocks - 1, 0), base + slab * nc + j))
    out_blk = pl.BlockSpec((tc, cb), lambda j, s: (rev(s), j))
    wblk = pl.BlockSpec((width, cb), lambda j, s: (0, j))
    act = jax.ShapeDtypeStruct((t, d_sc), BF16)
    return pl.pallas_call(
        body,
        out_shape=[act, act, act, jax.ShapeDtypeStruct((width, d_sc), F32)],
        grid=(nc, nt),
        in_specs=[blk(0), blk(1), halo(1), blk(2), halo(2), out_blk, wblk],
        out_specs=[out_blk, out_blk, out_blk, wblk],
        scratch_shapes=[pltpu.VMEM((SUBLANE, cb), F32)],
        compiler_params=_cparams(("parallel", "arbitrary"), 30 * tc * cb * 4),
        name="sc_bwd",
    )(p, p, p, p, p, dyp, conv_w)


def _merge_fwd(p, y_lru, y_sc, col0, tc):
    t, d = y_lru.shape
    cb = _pick(math.gcd(d, col0), 1024)
    nc = d // cb
    base = col0 // cb

    def body(gl_ref, gs_ref, yl_ref, ys_ref, o_ref):
        o_ref[...] = (_sigmoid(gl_ref[...]) * yl_ref[...] + _sigmoid(gs_ref[...]) * ys_ref[...]).astype(BF16)

    gate = lambda slab: pl.BlockSpec((tc, cb), lambda s, j, slab=slab: (s, base + slab * nc + j))
    blk = pl.BlockSpec((tc, cb), lambda s, j: (s, j))
    return pl.pallas_call(
        body,
        out_shape=jax.ShapeDtypeStruct((t, d), BF16),
        grid=(t // tc, nc),
        in_specs=[gate(0), gate(1), blk, blk],
        out_specs=blk,
        compiler_params=_cparams(("parallel", "parallel"), 2 * tc * cb * 20),
        name="merge_fwd",
    )(p, p, y_lru, y_sc)


def _merge_bwd(p, y_lru, y_sc, dmerged, col0, tc):
    t, d = y_lru.shape
    cb = _pick(math.gcd(d, col0), 1024)
    nc = d // cb
    base = col0 // cb

    def body(gl_ref, gs_ref, yl_ref, ys_ref, dm_ref, dgl_ref, dgs_ref, dyl_ref, dys_ref):
        dm = dm_ref[...]
        sl = _sigmoid(gl_ref[...])
        ss = _sigmoid(gs_ref[...])
        dgl_ref[...] = (dm * yl_ref[...] * (sl * (1.0 - sl))).astype(BF16)
        dgs_ref[...] = (dm * ys_ref[...] * (ss * (1.0 - ss))).astype(BF16)
        dyl_ref[...] = (dm * sl).astype(BF16)
        dys_ref[...] = (dm * ss).astype(BF16)

    gate = lambda slab: pl.BlockSpec((tc, cb), lambda s, j, slab=slab: (s, base + slab * nc + j))
    blk = pl.BlockSpec((tc, cb), lambda s, j: (s, j))
    act = jax.ShapeDtypeStruct((t, d), BF16)
    return pl.pallas_call(
        body,
        out_shape=[act, act, act, act],
        grid=(t // tc, nc),
        in_specs=[gate(0), gate(1), blk, blk, blk],
        out_specs=[blk, blk, blk, blk],
        compiler_params=_cparams(("parallel", "parallel"), 2 * tc * cb * 28),
        name="merge_bwd",
    )(p, p, y_lru, y_sc, dmerged)


def _ffn_act_fwd(up, conv_w, d_ff, cb, tc):
    t = up.shape[0]
    nc = d_ff // cb
    nt = t // tc
    width = conv_w.shape[0]

    def body(g_ref, v_ref, wg_ref, wv_ref, o_ref, halo_g, halo_v):
        @pl.when(pl.program_id(1) == 0)
        def _():
            halo_g[...] = jnp.zeros_like(halo_g)
            halo_v[...] = jnp.zeros_like(halo_v)

        g = g_ref[...]
        v = v_ref[...]
        ug = _conv_fwd(jnp.concatenate([halo_g[...], g], axis=0), wg_ref[...], width)
        uv = _conv_fwd(jnp.concatenate([halo_v[...], v], axis=0), wv_ref[...], width)
        halo_g[...] = g[tc - SUBLANE:, :]
        halo_v[...] = v[tc - SUBLANE:, :]
        o_ref[...] = (ug * _sigmoid(ug) * uv).astype(BF16)

    blk = lambda half: pl.BlockSpec((tc, cb), lambda j, s, half=half: (s, half * nc + j))
    wblk = lambda half: pl.BlockSpec((width, cb), lambda j, s, half=half: (0, half * nc + j))
    return pl.pallas_call(
        body,
        out_shape=jax.ShapeDtypeStruct((t, d_ff), BF16),
        grid=(nc, nt),
        in_specs=[blk(0), blk(1), wblk(0), wblk(1)],
        out_specs=pl.BlockSpec((tc, cb), lambda j, s: (s, j)),
        scratch_shapes=[pltpu.VMEM((SUBLANE, cb), F32), pltpu.VMEM((SUBLANE, cb), F32)],
        compiler_params=_cparams(("parallel", "arbitrary"), 24 * tc * cb * 4),
        name="ffn_act_fwd",
    )(up, up, conv_w, conv_w)


def _ffn_act_bwd(up, dact, conv_w, d_ff, cb, tc):
    t = up.shape[0]
    nc = d_ff // cb
    nt = t // tc
    width = conv_w.shape[0]
    halo_blocks = tc // SUBLANE

    def body(g_ref, gh_ref, v_ref, vh_ref, da_ref, wg_ref, wv_ref,
             dg_ref, dv_ref, dwg_ref, dwv_ref, nxt_g, nxt_v):
        s = pl.program_id(1)

        @pl.when(s == 0)
        def _():
            nxt_g[...] = jnp.zeros_like(nxt_g)
            nxt_v[...] = jnp.zeros_like(nxt_v)
            dwg_ref[...] = jnp.zeros_like(dwg_ref)
            dwv_ref[...] = jnp.zeros_like(dwv_ref)

        keep = jnp.where(s == nt - 1, 0.0, 1.0)
        wg = wg_ref[...]
        wv = wv_ref[...]
        catg = jnp.concatenate([gh_ref[...] * keep, g_ref[...]], axis=0)
        catv = jnp.concatenate([vh_ref[...] * keep, v_ref[...]], axis=0)
        ug = _conv_fwd(catg, wg, width)
        uv = _conv_fwd(catv, wv, width)
        sg = _sigmoid(ug)
        da = da_ref[...]
        dug = da * uv * (sg * (1.0 + ug * (1.0 - sg)))
        duv = da * (ug * sg)
        _conv_bwd_weight(dwg_ref, dug, catg, width)
        _conv_bwd_weight(dwv_ref, duv, catv, width)
        dg_ref[...] = _conv_bwd_input(jnp.concatenate([dug, nxt_g[...]], axis=0), wg, width).astype(BF16)
        dv_ref[...] = _conv_bwd_input(jnp.concatenate([duv, nxt_v[...]], axis=0), wv, width).astype(BF16)
        nxt_g[...] = dug[0:SUBLANE, :]
        nxt_v[...] = duv[0:SUBLANE, :]

    rev = lambda s: nt - 1 - s
    blk = lambda half: pl.BlockSpec((tc, cb), lambda j, s, half=half: (rev(s), half * nc + j))
    halo = lambda half: pl.BlockSpec(
        (SUBLANE, cb), lambda j, s, half=half: (jnp.maximum(rev(s) * halo_blocks - 1, 0), half * nc + j))
    wblk = lambda half: pl.BlockSpec((width, cb), lambda j, s, half=half: (0, half * nc + j))
    out_blk = pl.BlockSpec((tc, cb), lambda j, s: (rev(s), j))
    wout = pl.BlockSpec((width, cb), lambda j, s: (0, j))
    act = jax.ShapeDtypeStruct((t, d_ff), BF16)
    wshape = jax.ShapeDtypeStruct((width, d_ff), F32)
    return pl.pallas_call(
        body,
        out_shape=[act, act, wshape, wshape],
        grid=(nc, nt),
        in_specs=[blk(0), halo(0), blk(1), halo(1), out_blk, wblk(0), wblk(1)],
        out_specs=[out_blk, out_blk, wout, wout],
        scratch_shapes=[pltpu.VMEM((SUBLANE, cb), F32), pltpu.VMEM((SUBLANE, cb), F32)],
        compiler_params=_cparams(("parallel", "arbitrary"), 40 * tc * cb * 4),
        name="ffn_act_bwd",
    )(up, up, up, up, dact, conv_w, conv_w)


def _mesh_pos():
    x, y, c = lax.axis_index("x"), lax.axis_index("y"), lax.axis_index("c")
    return x, y, c


def _other_chips(x, y):
    return [(1 - x, y), (x, 1 - y), (1 - x, 1 - y)]


def _gather_weights(shards, col_sharded, small):
    n = len(shards)
    fulls = []
    for w, cs in zip(shards, col_sharded):
        r, cdim = w.shape
        fulls.append(jax.ShapeDtypeStruct((r, cdim * N_CHIPS) if cs else (r * N_CHIPS, cdim), w.dtype))
    fulls.append(jax.ShapeDtypeStruct((N_CHIPS,) + small.shape, small.dtype))

    def body(*refs):
        ins, small_ref = refs[:n], refs[n]
        outs, small_out = refs[n + 1:2 * n + 1], refs[2 * n + 1]
        send_sems, recv_sems, local_sems, small_send, small_recv = refs[2 * n + 2:]
        x, y, c = _mesh_pos()
        me = 2 * x + y
        chips = _other_chips(x, y)
        sibling = (x, y, 1 - c)

        def half_of(k, ref, chip, half):
            r, cdim = shards[k].shape
            h = r // 2
            if col_sharded[k]:
                return ref.at[pl.ds(pl.multiple_of(half * h, BF16_ROWS), h),
                              pl.ds(pl.multiple_of(chip * cdim, LANE), cdim)]
            return ref.at[pl.ds(pl.multiple_of(chip * r + half * h, BF16_ROWS), h), :]

        def whole_of(k, ref, chip):
            r, cdim = shards[k].shape
            if col_sharded[k]:
                return ref.at[:, pl.ds(pl.multiple_of(chip * cdim, LANE), cdim)]
            return ref.at[pl.ds(pl.multiple_of(chip * r, BF16_ROWS), r), :]

        def remote(k, slot, src, dst, to):
            return pltpu.make_async_remote_copy(
                src_ref=src, dst_ref=dst, send_sem=send_sems.at[k, slot], recv_sem=recv_sems.at[k, slot],
                device_id=to, device_id_type=MESH)

        small_out[me] = small_ref[...]
        small_copies = []
        for j, (px, py) in enumerate(chips):
            cp = pltpu.make_async_remote_copy(
                src_ref=small_ref, dst_ref=small_out.at[me], send_sem=small_send.at[j],
                recv_sem=small_recv.at[j], device_id=(px, py, c), device_id_type=MESH)
            cp.start()
            small_copies.append(cp)

        local = [pltpu.make_async_copy(ins[k], whole_of(k, outs[k], me), local_sems.at[k]) for k in range(n)]
        for cp in local:
            cp.start()
        sends = []
        for k in range(n):
            h = shards[k].shape[0] // 2
            src = ins[k].at[pl.ds(pl.multiple_of(c * h, BF16_ROWS), h), :]
            for j, (px, py) in enumerate(chips):
                cp = remote(k, j, src, half_of(k, outs[k], me, c), (px, py, c))
                cp.start()
                sends.append(cp)
        for k in range(n):
            for j, (px, py) in enumerate(chips):
                landed = half_of(k, outs[k], 2 * px + py, c)
                remote(k, j, landed, landed, (px, py, c)).wait_recv()
                cp = remote(k, 3 + j, landed, landed, sibling)
                cp.start()
                sends.append(cp)
        for k in range(n):
            for j, (px, py) in enumerate(chips):
                from_sibling = half_of(k, outs[k], 2 * px + py, 1 - c)
                remote(k, 3 + j, from_sibling, from_sibling, sibling).wait_recv()
        for j, (px, py) in enumerate(chips):
            pltpu.make_async_remote_copy(
                src_ref=small_ref, dst_ref=small_out.at[2 * px + py], send_sem=small_send.at[j],
                recv_sem=small_recv.at[j], device_id=(px, py, c), device_id_type=MESH).wait_recv()
        for cp in small_copies + sends:
            cp.wait_send()
        for cp in local:
            cp.wait()

    return pl.pallas_call(
        body,
        out_shape=fulls,
        in_specs=[ANY] * n + [VMEM_SPEC],
        out_specs=[ANY] * n + [VMEM_SPEC],
        scratch_shapes=[pltpu.SemaphoreType.DMA((n, 6)), pltpu.SemaphoreType.DMA((n, 6)),
                        pltpu.SemaphoreType.DMA((n,)),
                        pltpu.SemaphoreType.DMA((N_CHIPS - 1,)), pltpu.SemaphoreType.DMA((N_CHIPS - 1,))],
        name="gather_weights",
    )(*shards, small)


def _as3d(g, col_sharded):
    r, cdim = g.shape
    return g.reshape(1, r, cdim) if col_sharded else g.reshape(N_CHIPS, r // N_CHIPS, cdim)


def _pair_exchange(grads3):
    n = len(grads3)
    outs = [jax.ShapeDtypeStruct((g.shape[0], g.shape[1] // 2, g.shape[2]), g.dtype) for g in grads3]

    def body(*refs):
        ins, lands = refs[:n], refs[n:2 * n]
        send_sems, recv_sems = refs[2 * n:]
        x, y, c = _mesh_pos()
        sibling = (x, y, 1 - c)
        copies = []
        for k in range(n):
            h = grads3[k].shape[1] // 2
            src = ins[k].at[:, pl.ds(pl.multiple_of((1 - c) * h, SUBLANE), h), :]
            cp = pltpu.make_async_remote_copy(
                src_ref=src, dst_ref=lands[k], send_sem=send_sems.at[k], recv_sem=recv_sems.at[k],
                device_id=sibling, device_id_type=MESH)
            cp.start()
            copies.append(cp)
        for cp in copies:
            cp.wait()

    return pl.pallas_call(
        body,
        out_shape=outs,
        in_specs=[ANY] * n,
        out_specs=[ANY] * n,
        scratch_shapes=[pltpu.SemaphoreType.DMA((n,)), pltpu.SemaphoreType.DMA((n,))],
        name="grad_pair_exchange",
    )(*grads3)


def _pair_add(g3, other, core):
    a, r, cdim = g3.shape
    h = r // 2
    rb = _pick(h, max(BF16_ROWS, (512 * 1024) // cdim), BF16_ROWS)
    nb = h // rb

    def body(core_ref, g_ref, o_ref, out_ref):
        out_ref[...] = (g_ref[...] + o_ref[...]).astype(BF16)

    grid_spec = pltpu.PrefetchScalarGridSpec(
        num_scalar_prefetch=1,
        grid=(a, nb),
        in_specs=[pl.BlockSpec((None, rb, cdim), lambda i, j, core_ref: (i, core_ref[0] * nb + j, 0)),
                  pl.BlockSpec((None, rb, cdim), lambda i, j, core_ref: (i, j, 0))],
        out_specs=pl.BlockSpec((None, rb, cdim), lambda i, j, core_ref: (i, j, 0)),
    )
    return pl.pallas_call(
        body,
        out_shape=jax.ShapeDtypeStruct((a, h, cdim), BF16),
        grid_spec=grid_spec,
        compiler_params=_cparams(("parallel", "parallel"), 2 * rb * cdim * 10),
        name="grad_pair_add",
    )(core, g3, other)


def _chip_exchange(partials, col_sharded, small):
    n = len(partials)
    blocks = []
    for pz, cs in zip(partials, col_sharded):
        a, h, cdim = pz.shape
        blocks.append((h, cdim // N_CHIPS) if cs else (h, cdim))
    outs = [jax.ShapeDtypeStruct((N_CHIPS,) + b, BF16) for b in blocks]
    outs.append(jax.ShapeDtypeStruct(small.shape, F32))

    def body(*refs):
        ins, small_ref = refs[:n], refs[n]
        lands, small_out = refs[n + 1:2 * n + 1], refs[2 * n + 1]
        slots, send_sems, recv_sems, local_sems, small_send, small_recv = refs[2 * n + 2:]
        x, y, c = _mesh_pos()
        me = 2 * x + y
        my_dev = 4 * x + 2 * y + c
        chips = _other_chips(x, y)

        def block_of(k, chip):
            if col_sharded[k]:
                w = blocks[k][1]
                return ins[k].at[0, :, pl.ds(pl.multiple_of(chip * w, LANE), w)]
            return ins[k].at[chip]

        slots[my_dev] = small_ref[...]
        small_copies = []
        for m in range(1, N_DEV):
            peer = (x ^ ((m >> 2) & 1), y ^ ((m >> 1) & 1), c ^ (m & 1))
            cp = pltpu.make_async_remote_copy(
                src_ref=small_ref, dst_ref=slots.at[my_dev], send_sem=small_send.at[m - 1],
                recv_sem=small_recv.at[m - 1], device_id=peer, device_id_type=MESH)
            cp.start()
            small_copies.append(cp)

        local = [pltpu.make_async_copy(block_of(k, me), lands[k].at[me], local_sems.at[k]) for k in range(n)]
        for cp in local:
            cp.start()
        copies = []
        for k in range(n):
            for j, (px, py) in enumerate(chips):
                cp = pltpu.make_async_remote_copy(
                    src_ref=block_of(k, 2 * px + py), dst_ref=lands[k].at[me],
                    send_sem=send_sems.at[k, j], recv_sem=recv_sems.at[k, j],
                    device_id=(px, py, c), device_id_type=MESH)
                cp.start()
                copies.append(cp)

        for m in range(1, N_DEV):
            peer_dev = my_dev ^ m
            pltpu.make_async_remote_copy(
                src_ref=small_ref, dst_ref=slots.at[peer_dev], send_sem=small_send.at[m - 1],
                recv_sem=small_recv.at[m - 1], device_id=(x, y, c), device_id_type=MESH).wait_recv()
        total = slots[0]
        for d in range(1, N_DEV):
            total = total + slots[d]
        small_out[...] = total
        for cp in small_copies:
            cp.wait_send()

        for k in range(n):
            for j, (px, py) in enumerate(chips):
                pltpu.make_async_remote_copy(
                    src_ref=block_of(k, me), dst_ref=lands[k].at[2 * px + py],
                    send_sem=send_sems.at[k, j], recv_sem=recv_sems.at[k, j],
                    device_id=(px, py, c), device_id_type=MESH).wait_recv()
        for cp in copies:
            cp.wait_send()
        for cp in local:
            cp.wait()

    return pl.pallas_call(
        body,
        out_shape=outs,
        in_specs=[ANY] * n + [VMEM_SPEC],
        out_specs=[ANY] * n + [VMEM_SPEC],
        scratch_shapes=[pltpu.VMEM((N_DEV,) + small.shape, F32),
                        pltpu.SemaphoreType.DMA((n, 3)), pltpu.SemaphoreType.DMA((n, 3)),
                        pltpu.SemaphoreType.DMA((n,)),
                        pltpu.SemaphoreType.DMA((N_DEV - 1,)), pltpu.SemaphoreType.DMA((N_DEV - 1,))],
        compiler_params=pltpu.CompilerParams(
            vmem_limit_bytes=min(VMEM_BUDGET, (N_DEV + 4) * _nbytes(small.shape, F32) + (8 << 20))),
        name="grad_chip_exchange",
    )(*partials, small)


def _chip_sum(land):
    _, h, cdim = land.shape
    rb = _pick(h, max(BF16_ROWS, (512 * 1024) // cdim), BF16_ROWS)

    def body(l_ref, o_ref):
        total = l_ref[0].astype(F32)
        for q in range(1, N_CHIPS):
            total = total + l_ref[q].astype(F32)
        o_ref[...] = total

    return pl.pallas_call(
        body,
        out_shape=jax.ShapeDtypeStruct((h, cdim), F32),
        grid=(h // rb,),
        in_specs=[pl.BlockSpec((N_CHIPS, rb, cdim), lambda i: (0, i, 0))],
        out_specs=pl.BlockSpec((rb, cdim), lambda i: (i, 0)),
        compiler_params=_cparams(("parallel",), 2 * rb * cdim * 12),
        name="grad_chip_sum",
    )(land)


def _pair_share(halves):
    n = len(halves)
    outs = [jax.ShapeDtypeStruct((2 * hv.shape[0], hv.shape[1]), hv.dtype) for hv in halves]

    def body(*refs):
        ins, fulls = refs[:n], refs[n:2 * n]
        send_sems, recv_sems, local_sems = refs[2 * n:]
        x, y, c = _mesh_pos()
        sibling = (x, y, 1 - c)
        started = []
        for k in range(n):
            h = halves[k].shape[0]
            mine = fulls[k].at[pl.ds(pl.multiple_of(c * h, SUBLANE), h), :]
            lc = pltpu.make_async_copy(ins[k], mine, local_sems.at[k])
            lc.start()
            rc = pltpu.make_async_remote_copy(
                src_ref=ins[k], dst_ref=mine, send_sem=send_sems.at[k], recv_sem=recv_sems.at[k],
                device_id=sibling, device_id_type=MESH)
            rc.start()
            started.append((lc, rc))
        for k, (lc, rc) in enumerate(started):
            h = halves[k].shape[0]
            theirs = fulls[k].at[pl.ds(pl.multiple_of((1 - c) * h, SUBLANE), h), :]
            pltpu.make_async_remote_copy(
                src_ref=ins[k], dst_ref=theirs, send_sem=send_sems.at[k], recv_sem=recv_sems.at[k],
                device_id=sibling, device_id_type=MESH).wait_recv()
            rc.wait_send()
            lc.wait()

    return pl.pallas_call(
        body,
        out_shape=outs,
        in_specs=[ANY] * n,
        out_specs=[ANY] * n,
        scratch_shapes=[pltpu.SemaphoreType.DMA((n,)), pltpu.SemaphoreType.DMA((n,)),
                        pltpu.SemaphoreType.DMA((n,))],
        name="grad_pair_share",
    )(*halves)


def _adamw(w, g, m, v, name):
    r, cdim = w.shape
    rb = _pick(r, max(SUBLANE, (256 * 1024) // cdim), SUBLANE)
    c1 = 1.0 - ADAM_B1 ** ADAM_STEP
    c2 = 1.0 - ADAM_B2 ** ADAM_STEP

    def body(w_ref, g_ref, m_ref, v_ref, go_ref, d_ref, mo_ref, vo_ref):
        gv = g_ref[...]
        mn = ADAM_B1 * m_ref[...] + (1.0 - ADAM_B1) * gv
        vn = ADAM_B2 * v_ref[...] + (1.0 - ADAM_B2) * (gv * gv)
        m_hat = mn / c1
        v_hat = vn / c2
        d_ref[...] = -ADAM_LR * (m_hat / (jnp.sqrt(v_hat) + ADAM_EPS) + ADAM_WD * w_ref[...])
        go_ref[...] = gv
        mo_ref[...] = mn
        vo_ref[...] = vn

    blk = pl.BlockSpec((rb, cdim), lambda i: (i, 0))
    shape = jax.ShapeDtypeStruct((r, cdim), F32)
    return pl.pallas_call(
        body,
        out_shape=[shape] * 4,
        grid=(r // rb,),
        in_specs=[blk] * 4,
        out_specs=[blk] * 4,
        compiler_params=_cparams(("parallel",), 2 * rb * cdim * 4 * 8),
        name=name,
    )(w, g, m, v)


def _pack(arrays):
    tile = SUBLANE * LANE
    pieces = []
    for arr in arrays:
        flat = arr.reshape(-1)
        pad = (-flat.shape[0]) % tile
        if pad:
            flat = jnp.concatenate([flat, jnp.zeros((pad,), flat.dtype)])
        pieces.append(flat)
    return jnp.concatenate(pieces).reshape(-1, LANE)


def _unpack(packed, shapes):
    tile = SUBLANE * LANE
    flat = packed.reshape(-1)
    out, off = [], 0
    for shp in shapes:
        size = math.prod(shp)
        out.append(flat[off:off + size].reshape(shp))
        off += size + ((-size) % tile)
    return out


def _block_diag_groups(w, per_group):
    hcount, hd, _ = w.shape
    ng = hcount // per_group
    w4 = w.reshape(ng, per_group, hd, hd)
    eye = jnp.eye(per_group, dtype=w.dtype)
    bd = w4[:, :, :, None, :] * eye[None, :, None, :, None]
    return bd.reshape(ng, per_group * hd, per_group * hd).astype(BF16)


def _diag_blocks(wbd, per_group, hd):
    ng = wbd.shape[0]
    w5 = wbd.reshape(ng, per_group, hd, per_group, hd)
    blocks = [w5[:, i, :, i, :] for i in range(per_group)]
    return jnp.stack(blocks, axis=1).reshape(ng * per_group, hd, hd)


def kernel(x, g_mix, w_in, lru_conv_w, lru_conv_b, lru_wa, lru_ba, lru_wx, lru_bx, lru_lambda, lru_w_out, sc_conv_w, sc_w_out, w_o, g_ffn, ffn_w_up, ffn_conv_w, ffn_w_down, g_final, loss_target, m_g_mix, m_w_in, m_lru_conv_w, m_lru_conv_b, m_lru_wa, m_lru_ba, m_lru_wx, m_lru_bx, m_lru_lambda, m_lru_w_out, m_sc_conv_w, m_sc_w_out, m_w_o, m_g_ffn, m_ffn_w_up, m_ffn_conv_w, m_ffn_w_down, m_g_final, v_g_mix, v_w_in, v_lru_conv_w, v_lru_conv_b, v_lru_wa, v_lru_ba, v_lru_wx, v_lru_bx, v_lru_lambda, v_lru_w_out, v_sc_conv_w, v_sc_w_out, v_w_o, v_g_ffn, v_ffn_w_up, v_ffn_conv_w, v_ffn_w_down, v_g_final):
    seq, d_model = x.shape[1], x.shape[2]
    heads, head_dim, _ = lru_wa.shape
    d_lru = heads * head_dim
    d_sc = sc_w_out.shape[0]
    d_ff = ffn_w_down.shape[0] * N_CHIPS
    assert x.shape[0] == 1 and w_in.shape[1] * N_CHIPS == 2 * d_lru + 3 * d_sc + 2 * d_model
    xs = x.reshape(seq, d_model)
    target = loss_target.reshape(seq, d_model)

    chip = 2 * lax.axis_index("x") + lax.axis_index("y")
    core = lax.axis_index("c").astype(jnp.int32).reshape(1)

    big_w = [w_in, lru_w_out, sc_w_out, w_o, ffn_w_up, ffn_w_down]
    big_m = [m_w_in, m_lru_w_out, m_sc_w_out, m_w_o, m_ffn_w_up, m_ffn_w_down]
    big_v = [v_w_in, v_lru_w_out, v_sc_w_out, v_w_o, v_ffn_w_up, v_ffn_w_down]
    col_sharded = [True, True, True, False, True, False]
    conv_shards = [lru_conv_w, sc_conv_w, ffn_conv_w]
    conv_pack = jnp.concatenate(
        [jnp.pad(w, ((0, SUBLANE - w.shape[0]), (0, 0))) for w in conv_shards], axis=1)
    *big_full, conv_all = _gather_weights([w.astype(BF16) for w in big_w], col_sharded, conv_pack)
    win_b, wlo_b, wso_b, wo_b, wup_b, wdn_b = big_full
    conv_full, off = [], 0
    for w in conv_shards:
        kw, nq = w.shape
        piece = conv_all[:, :kw, off:off + nq]
        conv_full.append(piece.transpose(1, 0, 2).reshape(kw, N_CHIPS * nq))
        off += nq
    lcw, scw, fcw = conv_full

    per_group = max(1, min(heads, 256 // head_dim))
    gc = per_group * head_dim
    wa_bd = _block_diag_groups(lru_wa, per_group)
    wx_bd = _block_diag_groups(lru_wx, per_group)
    tc = _pick(seq, 256, SUBLANE)
    cb_sc = _pick(d_sc, 512)
    cb_ff = _pick(d_ff, 512)
    col_sc = 2 * d_lru
    col_gates = 2 * d_lru + 3 * d_sc

    h1 = _rms_fwd(xs, g_mix, "rms_mix")
    p = _mm(h1, win_b, "nn", F32, name="mm_in")
    y_lru_pre, hseq = _lru_fwd(p, lcw, lru_conv_b, wa_bd, lru_ba, wx_bd, lru_bx, lru_lambda, d_lru, gc, tc)
    y_sc_pre = _sc_fwd(p, scw, col_sc, d_sc, cb_sc, tc)
    y_lru = _mm(y_lru_pre, wlo_b, "nn", F32, name="mm_lru_out")
    y_sc = _mm(y_sc_pre, wso_b, "nn", F32, name="mm_sc_out")
    merged = _merge_fwd(p, y_lru, y_sc, col_gates, tc)
    x2 = _mm(merged, wo_b, "nn", F32, res=xs, name="mm_o")
    h2 = _rms_fwd(x2, g_ffn, "rms_ffn")
    up = _mm(h2, wup_b, "nn", F32, name="mm_up")
    act = _ffn_act_fwd(up, fcw, d_ff, cb_ff, tc)
    x3 = _mm(act, wdn_b, "nn", F32, res=x2, name="mm_down")
    loss_part, dx3, dx3b, dg_final = _loss_head(x3, g_final, target)

    dact = _mm(dx3b, wdn_b, "nt", F32, name="mm_down_dx")
    g_wdn = _mm(act, dx3b, "tn", F32, name="mm_down_dw")
    dupg, dupv, dfcw_g, dfcw_v = _ffn_act_bwd(up, dact, fcw, d_ff, cb_ff, tc)
    dup = jnp.concatenate([dupg, dupv], axis=1)
    dh2 = _mm(dup, wup_b, "nt", F32, name="mm_up_dx")
    g_wup = _mm(h2, dup, "tn", F32, name="mm_up_dw")
    dx2, dx2b, dg_ffn = _rms_bwd(x2, g_ffn, dh2, dx3, "rms_ffn_bwd", True)
    dmerged = _mm(dx2b, wo_b, "nt", F32, name="mm_o_dx")
    g_wo = _mm(merged, dx2b, "tn", F32, name="mm_o_dw")
    dgl, dgs, dyl, dys = _merge_bwd(p, y_lru, y_sc, dmerged, col_gates, tc)
    dylp = _mm(dyl, wlo_b, "nt", F32, name="mm_lru_out_dx")
    g_wlo = _mm(y_lru_pre, dyl, "tn", F32, name="mm_lru_out_dw")
    dysp = _mm(dys, wso_b, "nt", F32, name="mm_sc_out_dx")
    g_wso = _mm(y_sc_pre, dys, "tn", F32, name="mm_sc_out_dw")
    dlx, dlgate, dlcw, dlcb, dwa_bd, dba, dwx_bd, dbx, dlam = _lru_bwd(
        p, hseq, dylp, lcw, lru_conv_b, wa_bd, lru_ba, wx_bd, lru_bx, lru_lambda, d_lru, gc, tc)
    dsb, dsc, dsv, dscw = _sc_bwd(p, dysp, scw, col_sc, d_sc, cb_sc, tc)
    dp = jnp.concatenate([dlx, dlgate, dsb, dsc, dsv, dgl, dgs], axis=1)
    dh1 = _mm(dp, win_b, "nt", F32, name="mm_in_dx")
    g_win = _mm(h1, dp, "tn", F32, name="mm_in_dw")
    grad_x, dg_mix = _rms_bwd(xs, g_mix, dh1, dx2, "rms_mix_bwd", False)

    big_g = [g_win, g_wlo, g_wso, g_wo, g_wup, g_wdn]
    grads3 = [_as3d(g, cs) for g, cs in zip(big_g, col_sharded)]
    from_sibling = _pair_exchange(grads3)
    partials = [_pair_add(g3, o, core) for g3, o in zip(grads3, from_sibling)]
    small_g = [dg_mix, dlcw, dlcb, _diag_blocks(dwa_bd, per_group, head_dim), dba,
               _diag_blocks(dwx_bd, per_group, head_dim), dbx, dlam, dscw, dg_ffn,
               jnp.concatenate([dfcw_g, dfcw_v], axis=1), dg_final]
    small_shapes = [a.shape for a in small_g]
    *lands, small_sum = _chip_exchange(partials, col_sharded, _pack(small_g))
    shard_g = _pair_share([_chip_sum(land) for land in lands])

    big_names = ["w_in", "lru_w_out", "sc_w_out", "w_o", "ffn_w_up", "ffn_w_down"]
    big_out = [_adamw(w, g, m, v, "adamw_" + nm)
               for nm, w, g, m, v in zip(big_names, big_w, shard_g, big_m, big_v)]
    sg = _unpack(small_sum, small_shapes)
    for idx in (1, 8, 10):
        nq = sg[idx].shape[1] // N_CHIPS
        sg[idx] = lax.dynamic_slice_in_dim(sg[idx], chip * nq, nq, axis=1)
    small_w = [g_mix, lru_conv_w, lru_conv_b, lru_wa, lru_ba, lru_wx, lru_bx, lru_lambda, sc_conv_w,
               g_ffn, ffn_conv_w, g_final]
    small_m = [m_g_mix, m_lru_conv_w, m_lru_conv_b, m_lru_wa, m_lru_ba, m_lru_wx, m_lru_bx, m_lru_lambda,
               m_sc_conv_w, m_g_ffn, m_ffn_conv_w, m_g_final]
    small_v = [v_g_mix, v_lru_conv_w, v_lru_conv_b, v_lru_wa, v_lru_ba, v_lru_wx, v_lru_bx, v_lru_lambda,
               v_sc_conv_w, v_g_ffn, v_ffn_conv_w, v_g_final]
    sg = [g.reshape(w.shape) for g, w in zip(sg, small_w)]
    w_shapes = [w.shape for w in small_w]
    packed = _adamw(_pack(small_w), _pack(sg), _pack(small_m), _pack(small_v), "adamw_small")
    small_out = [_unpack(pk, w_shapes) for pk in packed]

    order = [(0, 0), (1, 0), (0, 1), (0, 2), (0, 3), (0, 4), (0, 5), (0, 6), (0, 7), (1, 1), (0, 8), (1, 2),
             (1, 3), (0, 9), (1, 4), (0, 10), (1, 5), (0, 11)]
    by_kind = []
    for kind in range(4):
        by_kind.append([big_out[i][kind] if is_big else small_out[kind][i] for is_big, i in order])
    loss = lax.psum(loss_part[0, 0], ("x", "y", "c"))
    return (loss, grad_x.reshape(x.shape), *by_kind[0], *by_kind[1], *by_kind[2], *by_kind[3])
```

```python
import functools
import math

import jax
import jax.numpy as jnp
from jax import lax
from jax.experimental import pallas as pl
from jax.experimental.pallas import tpu as pltpu

F32 = jnp.float32
BF16 = jnp.bfloat16

LANE = 128
SUBLANE = 8
BF16_ROWS = 16
VMEM_BYTES_V7X = 64 * 1024 * 1024
VMEM_BUDGET = VMEM_BYTES_V7X - 8 * 1024 * 1024

EPS = 1e-6
LRU_C = 8.0
ADAM_LR = 0.001
ADAM_B1 = 0.9
ADAM_B2 = 0.999
ADAM_EPS = 1e-08
ADAM_WD = 0.01
ADAM_STEP = 10

N_CHIPS = 4
N_DEV = 8
MESH = pl.DeviceIdType.MESH
ANY = pl.BlockSpec(memory_space=pl.ANY)
VMEM_SPEC = pl.BlockSpec(memory_space=pltpu.VMEM)


def _pick(n, cap, mult=LANE):
    best = None
    d = mult
    while d <= min(n, cap):
        if n % d == 0:
            best = d
        d += mult
    return n if best is None else best


def _cparams(semantics, block_bytes):
    limit = min(VMEM_BUDGET, max(32 * 1024 * 1024, int(block_bytes * 1.25) + (4 << 20)))
    return pltpu.CompilerParams(dimension_semantics=semantics, vmem_limit_bytes=limit)


def _nbytes(shape, dtype):
    return math.prod(shape) * jnp.dtype(dtype).itemsize


def _sigmoid(z):
    return 1.0 / (1.0 + jnp.exp(-z))


def _softplus(z):
    e = jnp.exp(-jnp.abs(z))
    u = 1.0 + e
    log1p = jnp.where(u == 1.0, e, jnp.log(u) * (e / (u - 1.0)))
    return jnp.maximum(z, 0.0) + log1p


def _neg_expm1(z):
    small = z * (1.0 + z * (0.5 + z * (1.0 / 6.0 + z * (1.0 / 24.0))))
    return -jnp.where(jnp.abs(z) < 0.03, small, jnp.exp(z) - 1.0)


_GELU_K = math.sqrt(2.0 / math.pi)
_GELU_C = 0.044715


def _gelu_and_grad(z):
    z2 = z * z
    th = jnp.tanh(_GELU_K * (z + _GELU_C * z2 * z))
    val = 0.5 * z * (1.0 + th)
    grad = 0.5 * (1.0 + th) + 0.5 * z * (1.0 - th * th) * (_GELU_K * (1.0 + 3.0 * _GELU_C * z2))
    return val, grad


def _rows_before(cat, k):
    if k == 0:
        return cat[SUBLANE:, :]
    return pltpu.roll(cat, k, 0)[SUBLANE:, :]


def _rows_after(cat, k):
    n = cat.shape[0]
    if k == 0:
        return cat[:n - SUBLANE, :]
    return pltpu.roll(cat, n - k, 0)[:n - SUBLANE, :]


def _conv_fwd(cat, w, width):
    y = _rows_before(cat, width - 1) * w[0:1, :]
    for k in range(1, width):
        y = y + _rows_before(cat, width - 1 - k) * w[k:k + 1, :]
    return y


def _conv_bwd_input(cat, w, width):
    dx = _rows_after(cat, width - 1) * w[0:1, :]
    for k in range(1, width):
        dx = dx + _rows_after(cat, width - 1 - k) * w[k:k + 1, :]
    return dx


def _conv_bwd_weight(dw_ref, dy, catx, width):
    for k in range(width):
        dw_ref[k:k + 1, :] += jnp.sum(dy * _rows_before(catx, width - 1 - k), axis=0, keepdims=True)


def _scan_tiles(a_ref, b_ref, out_ref, carry0, n_rows, reverse):
    cols = a_ref.shape[1]
    row = lax.broadcasted_iota(jnp.int32, (SUBLANE, cols), 0)
    n_tiles = n_rows // SUBLANE

    def step(j, carry):
        tile = (n_tiles - 1 - j) if reverse else j
        off = pl.multiple_of(tile * SUBLANE, SUBLANE)
        a = a_ref[pl.ds(off, SUBLANE), :]
        b = b_ref[pl.ds(off, SUBLANE), :]
        for s in (1, 2, 4):
            if reverse:
                keep = row < SUBLANE - s
                shift = SUBLANE - s
            else:
                keep = row >= s
                shift = s
            a_sh = jnp.where(keep, pltpu.roll(a, shift, 0), 1.0)
            b_sh = jnp.where(keep, pltpu.roll(b, shift, 0), 0.0)
            b = a * b_sh + b
            a = a * a_sh
        out = a * carry + b
        out_ref[pl.ds(off, SUBLANE), :] = out
        return out[0:1, :] if reverse else out[SUBLANE - 1:SUBLANE, :]

    return lax.fori_loop(0, n_tiles, step, carry0)


def _mm(a, b, mode, out_dtype, res=None, name=None):
    if mode == "nn":
        (m, k), (k2, n) = a.shape, b.shape
        dims = (((1,), (0,)), ((), ()))
    elif mode == "nt":
        (m, k), (n, k2) = a.shape, b.shape
        dims = (((1,), (1,)), ((), ()))
    else:
        (k, m), (k2, n) = a.shape, b.shape
        dims = (((0,), (0,)), ((), ()))
    assert k == k2 and a.dtype == BF16 and b.dtype == BF16
    bm = _pick(m, 1024)
    bn = _pick(n, 1024)
    bk = _pick(k, 2048)
    nk = k // bk
    out_bytes = jnp.dtype(out_dtype).itemsize

    def est(bk_):
        e = 2 * (bm * bk_ + bk_ * bn) * 2 + 2 * bm * bn * out_bytes
        if nk > 1 or k // bk_ > 1:
            e += bm * bn * 4
        if res is not None:
            e += 2 * bm * bn * 4
        return e

    while est(bk) > 40 * 1024 * 1024 and bk % (2 * LANE) == 0 and k % (bk // 2) == 0:
        bk //= 2
        nk = k // bk

    if mode == "tn":
        a_spec = pl.BlockSpec((bk, bm), lambda i, j, kk: (kk, i))
    else:
        a_spec = pl.BlockSpec((bm, bk), lambda i, j, kk: (i, kk))
    if mode == "nt":
        b_spec = pl.BlockSpec((bn, bk), lambda i, j, kk: (j, kk))
    else:
        b_spec = pl.BlockSpec((bk, bn), lambda i, j, kk: (kk, j))
    o_spec = pl.BlockSpec((bm, bn), lambda i, j, kk: (i, j))
    in_specs = [a_spec, b_spec]
    operands = [a, b]
    if res is not None:
        in_specs.append(o_spec)
        operands.append(res)
    has_res = res is not None

    def body(*refs):
        a_ref, b_ref = refs[0], refs[1]
        res_ref = refs[2] if has_res else None
        o_ref = refs[3] if has_res else refs[2]
        part = lax.dot_general(a_ref[...], b_ref[...], dims, preferred_element_type=F32)
        if nk == 1:
            if has_res:
                part = part + res_ref[...]
            o_ref[...] = part.astype(o_ref.dtype)
            return
        acc_ref = refs[-1]
        kk = pl.program_id(2)

        @pl.when(kk == 0)
        def _():
            acc_ref[...] = part

        @pl.when(kk > 0)
        def _():
            acc_ref[...] += part

        @pl.when(kk == nk - 1)
        def _():
            total = acc_ref[...]
            if has_res:
                total = total + res_ref[...]
            o_ref[...] = total.astype(o_ref.dtype)

    scratch = [pltpu.VMEM((bm, bn), F32)] if nk > 1 else []
    return pl.pallas_call(
        body,
        out_shape=jax.ShapeDtypeStruct((m, n), out_dtype),
        grid=(m // bm, n // bn, nk),
        in_specs=in_specs,
        out_specs=o_spec,
        scratch_shapes=scratch,
        compiler_params=_cparams(("parallel", "parallel", "arbitrary"), est(bk)),
        name=name,
    )(*operands)


def _rms_fwd(x, g, name):
    t, d = x.shape
    tb = _pick(t, 512, SUBLANE)

    def body(x_ref, g_ref, h_ref):
        xv = x_ref[...]
        r = lax.rsqrt(jnp.mean(xv * xv, axis=-1, keepdims=True) + EPS)
        h_ref[...] = ((xv * r) * g_ref[...]).astype(BF16)

    blk = pl.BlockSpec((tb, d), lambda i: (i, 0))
    return pl.pallas_call(
        body,
        out_shape=jax.ShapeDtypeStruct((t, d), BF16),
        grid=(t // tb,),
        in_specs=[blk, pl.BlockSpec((1, d), lambda i: (0, 0))],
        out_specs=blk,
        compiler_params=_cparams(("parallel",), 2 * tb * d * 6),
        name=name,
    )(x, g.reshape(1, d))


def _rms_bwd(x, g, dh, dres, name, want_bf16):
    t, d = x.shape
    tb = _pick(t, 256, SUBLANE)

    def body(x_ref, g_ref, dh_ref, dres_ref, *outs):
        dx_ref, dg_ref = outs[0], outs[-1]
        xv = x_ref[...]
        r = lax.rsqrt(jnp.mean(xv * xv, axis=-1, keepdims=True) + EPS)
        xhat = xv * r
        dhv = dh_ref[...]
        dxhat = dhv * g_ref[...]
        dx = dres_ref[...] + r * (dxhat - xhat * jnp.mean(dxhat * xhat, axis=-1, keepdims=True))
        dx_ref[...] = dx
        if want_bf16:
            outs[1][...] = dx.astype(BF16)

        @pl.when(pl.program_id(0) == 0)
        def _():
            dg_ref[...] = jnp.zeros_like(dg_ref)

        dg_ref[...] += jnp.sum(dhv * xhat, axis=0, keepdims=True)

    blk = pl.BlockSpec((tb, d), lambda i: (i, 0))
    row = pl.BlockSpec((1, d), lambda i: (0, 0))
    out_shape = [jax.ShapeDtypeStruct((t, d), F32)]
    out_specs = [blk]
    if want_bf16:
        out_shape.append(jax.ShapeDtypeStruct((t, d), BF16))
        out_specs.append(blk)
    out_shape.append(jax.ShapeDtypeStruct((1, d), F32))
    out_specs.append(row)
    return pl.pallas_call(
        body,
        out_shape=out_shape,
        grid=(t // tb,),
        in_specs=[blk, row, blk, blk],
        out_specs=out_specs,
        compiler_params=_cparams(("arbitrary",), 2 * tb * d * 18),
        name=name,
    )(x, g.reshape(1, d), dh, dres)


def _loss_head(x3, g, target):
    t, d = x3.shape
    tb = _pick(t, 256, SUBLANE)

    def body(x_ref, g_ref, t_ref, loss_ref, dx_ref, dxb_ref, dg_ref):
        xv = x_ref[...]
        gv = g_ref[...]
        r = lax.rsqrt(jnp.mean(xv * xv, axis=-1, keepdims=True) + EPS)
        xhat = xv * r
        err = xhat * gv - t_ref[...]
        dy = err * (1.0 / d)
        dxhat = dy * gv
        dx = r * (dxhat - xhat * jnp.mean(dxhat * xhat, axis=-1, keepdims=True))
        dx_ref[...] = dx
        dxb_ref[...] = dx.astype(BF16)

        @pl.when(pl.program_id(0) == 0)
        def _():
            dg_ref[...] = jnp.zeros_like(dg_ref)
            loss_ref[...] = jnp.zeros_like(loss_ref)

        dg_ref[...] += jnp.sum(dy * xhat, axis=0, keepdims=True)
        per_token = jnp.mean(err * err, axis=-1, keepdims=True)
        loss_ref[...] += 0.5 * jnp.sum(per_token, axis=0, keepdims=True)

    blk = pl.BlockSpec((tb, d), lambda i: (i, 0))
    row = pl.BlockSpec((1, d), lambda i: (0, 0))
    return pl.pallas_call(
        body,
        out_shape=[jax.ShapeDtypeStruct((1, 1), F32), jax.ShapeDtypeStruct((t, d), F32),
                   jax.ShapeDtypeStruct((t, d), BF16), jax.ShapeDtypeStruct((1, d), F32)],
        grid=(t // tb,),
        in_specs=[blk, row, blk],
        out_specs=[pl.BlockSpec((1, 1), lambda i: (0, 0)), blk, blk, row],
        compiler_params=_cparams(("arbitrary",), 2 * tb * d * 14),
        name="loss_head",
    )(x3, g.reshape(1, d), target)


def _lru_gates(xc, wa, ba, wx, bx, lam):
    nn = (((1,), (0,)), ((), ()))
    xcb = xc.astype(BF16)
    r = _sigmoid(lax.dot_general(xcb, wa, nn, preferred_element_type=F32) + ba)
    i = _sigmoid(lax.dot_general(xcb, wx, nn, preferred_element_type=F32) + bx)
    cl = -LRU_C * _softplus(-lam)
    log_a = cl * r
    a = jnp.exp(log_a)
    one_minus_a2 = _neg_expm1(2.0 * log_a)
    return xcb, r, i, a, one_minus_a2, cl


def _lru_fwd(p, conv_w, conv_b, wa_bd, ba, wx_bd, bx, lam, d_lru, gc, tc):
    t = p.shape[0]
    ng = d_lru // gc
    nt = t // tc
    width = conv_w.shape[0]

    def body(lx_ref, gate_ref, cw_ref, cb_ref, wa_ref, ba_ref, wx_ref, bx_ref, lam_ref,
             y_ref, h_ref, halo, hcar, a_s, u_s):
        @pl.when(pl.program_id(1) == 0)
        def _():
            halo[...] = jnp.zeros_like(halo)
            hcar[...] = jnp.zeros_like(hcar)

        x = lx_ref[...]
        cat = jnp.concatenate([halo[...], x], axis=0)
        halo[...] = x[tc - SUBLANE:, :]
        xc = _conv_fwd(cat, cw_ref[...], width) + cb_ref[...]
        _, r, i, a, om, _ = _lru_gates(xc, wa_ref[...], ba_ref[...], wx_ref[...], bx_ref[...], lam_ref[...])
        a_s[...] = a
        u_s[...] = jnp.sqrt(om) * (i * xc)
        hcar[0:1, :] = _scan_tiles(a_s, u_s, h_ref, hcar[0:1, :], tc, reverse=False)
        gl, _ = _gelu_and_grad(gate_ref[...])
        y_ref[...] = (gl * h_ref[...]).astype(BF16)

    blk = lambda off: pl.BlockSpec((tc, gc), lambda g, s, off=off: (s, off + g))
    rowv = lambda rows: pl.BlockSpec((rows, gc), lambda g, s: (0, g))
    wspec = pl.BlockSpec((None, gc, gc), lambda g, s: (g, 0, 0))
    out_blk = pl.BlockSpec((tc, gc), lambda g, s: (s, g))
    return pl.pallas_call(
        body,
        out_shape=[jax.ShapeDtypeStruct((t, d_lru), BF16), jax.ShapeDtypeStruct((t, d_lru), F32)],
        grid=(ng, nt),
        in_specs=[blk(0), blk(ng), rowv(width), rowv(1), wspec, rowv(1), wspec, rowv(1), rowv(1)],
        out_specs=[out_blk, out_blk],
        scratch_shapes=[pltpu.VMEM((SUBLANE, gc), F32), pltpu.VMEM((SUBLANE, gc), F32),
                        pltpu.VMEM((tc, gc), F32), pltpu.VMEM((tc, gc), F32)],
        compiler_params=_cparams(("parallel", "arbitrary"), 40 * tc * gc * 4),
        name="lru_fwd",
    )(p, p, conv_w, conv_b.reshape(1, -1), wa_bd, ba.reshape(1, -1), wx_bd, bx.reshape(1, -1),
      lam.reshape(1, -1))


def _lru_bwd(p, hseq, dyp, conv_w, conv_b, wa_bd, ba, wx_bd, bx, lam, d_lru, gc, tc):
    t = p.shape[0]
    ng = d_lru // gc
    nt = t // tc
    width = conv_w.shape[0]
    halo_blocks = tc // SUBLANE
    nn = (((1,), (0,)), ((), ()))
    nt_dims = (((1,), (1,)), ((), ()))
    tn_dims = (((0,), (0,)), ((), ()))

    def body(lx_ref, lxh_ref, gate_ref, h_ref, hh_ref, dyp_ref,
             cw_ref, cb_ref, wa_ref, ba_ref, wx_ref, bx_ref, lam_ref,
             dlx_ref, dgate_ref, dcw_ref, dcb_ref, dwa_ref, dba_ref, dwx_ref, dbx_ref, dlam_ref,
             nxt_dxc, nxt_a, nxt_g, al_s, b_s, g_s):
        s = pl.program_id(1)
        first_chunk = s == nt - 1

        @pl.when(s == 0)
        def _():
            nxt_dxc[...] = jnp.zeros_like(nxt_dxc)
            nxt_a[...] = jnp.zeros_like(nxt_a)
            nxt_g[...] = jnp.zeros_like(nxt_g)
            for ref in (dcw_ref, dcb_ref, dwa_ref, dba_ref, dwx_ref, dbx_ref, dlam_ref):
                ref[...] = jnp.zeros_like(ref)

        keep = jnp.where(first_chunk, 0.0, 1.0)
        x = lx_ref[...]
        catx = jnp.concatenate([lxh_ref[...] * keep, x], axis=0)
        cw = cw_ref[...]
        xc = _conv_fwd(catx, cw, width) + cb_ref[...]
        wa = wa_ref[...]
        wx = wx_ref[...]
        lam_v = lam_ref[...]
        xcb, r, i, a, om, cl = _lru_gates(xc, wa, ba_ref[...], wx, bx_ref[...], lam_v)
        mult = jnp.sqrt(om)

        h = h_ref[...]
        hprev = _rows_before(jnp.concatenate([hh_ref[...] * keep, h], axis=0), 1)
        gl, dgl = _gelu_and_grad(gate_ref[...])
        dyp_v = dyp_ref[...]
        dgate_ref[...] = (dyp_v * h * dgl).astype(BF16)

        al_s[...] = _rows_after(jnp.concatenate([a, nxt_a[...]], axis=0), 1)
        b_s[...] = dyp_v * gl
        nxt_g[0:1, :] = _scan_tiles(al_s, b_s, g_s, nxt_g[0:1, :], tc, reverse=True)
        nxt_a[...] = a[0:SUBLANE, :]
        du = g_s[...]

        da = du * hprev
        dmult = du * (i * xc)
        di = du * mult * xc
        dxc = du * mult * i
        dlog_a = da * a - dmult * (a * a / mult)
        dlam_ref[...] += jnp.sum(dlog_a * r, axis=0, keepdims=True) * (LRU_C * _sigmoid(-lam_v))
        dza = (dlog_a * cl) * r * (1.0 - r)
        dzx = di * i * (1.0 - i)
        dba_ref[...] += jnp.sum(dza, axis=0, keepdims=True)
        dbx_ref[...] += jnp.sum(dzx, axis=0, keepdims=True)
        dzab = dza.astype(BF16)
        dzxb = dzx.astype(BF16)
        dwa_ref[...] += lax.dot_general(xcb, dzab, tn_dims, preferred_element_type=F32)
        dwx_ref[...] += lax.dot_general(xcb, dzxb, tn_dims, preferred_element_type=F32)
        dxc = dxc + lax.dot_general(dzab, wa, nt_dims, preferred_element_type=F32)
        dxc = dxc + lax.dot_general(dzxb, wx, nt_dims, preferred_element_type=F32)
        dcb_ref[...] += jnp.sum(dxc, axis=0, keepdims=True)
        _conv_bwd_weight(dcw_ref, dxc, catx, width)
        catd = jnp.concatenate([dxc, nxt_dxc[...]], axis=0)
        dlx_ref[...] = _conv_bwd_input(catd, cw, width).astype(BF16)
        nxt_dxc[...] = dxc[0:SUBLANE, :]

    rev = lambda s: nt - 1 - s
    blk = lambda off: pl.BlockSpec((tc, gc), lambda g, s, off=off: (rev(s), off + g))
    halo = lambda off: pl.BlockSpec(
        (SUBLANE, gc), lambda g, s, off=off: (jnp.maximum(rev(s) * halo_blocks - 1, 0), off + g))
    rowv = lambda rows: pl.BlockSpec((rows, gc), lambda g, s: (0, g))
    wspec = pl.BlockSpec((None, gc, gc), lambda g, s: (g, 0, 0))
    out_blk = pl.BlockSpec((tc, gc), lambda g, s: (rev(s), g))
    vec = lambda rows: jax.ShapeDtypeStruct((rows, d_lru), F32)
    wshape = jax.ShapeDtypeStruct((ng, gc, gc), F32)
    return pl.pallas_call(
        body,
        out_shape=[jax.ShapeDtypeStruct((t, d_lru), BF16), jax.ShapeDtypeStruct((t, d_lru), BF16),
                   vec(width), vec(1), wshape, vec(1), wshape, vec(1), vec(1)],
        grid=(ng, nt),
        in_specs=[blk(0), halo(0), blk(ng), blk(0), halo(0), blk(0),
                  rowv(width), rowv(1), wspec, rowv(1), wspec, rowv(1), rowv(1)],
        out_specs=[out_blk, out_blk, rowv(width), rowv(1), wspec, rowv(1), wspec, rowv(1), rowv(1)],
        scratch_shapes=[pltpu.VMEM((SUBLANE, gc), F32), pltpu.VMEM((SUBLANE, gc), F32),
                        pltpu.VMEM((SUBLANE, gc), F32),
                        pltpu.VMEM((tc, gc), F32), pltpu.VMEM((tc, gc), F32), pltpu.VMEM((tc, gc), F32)],
        compiler_params=_cparams(("parallel", "arbitrary"), 80 * tc * gc * 4),
        name="lru_bwd",
    )(p, p, p, hseq, hseq, dyp, conv_w, conv_b.reshape(1, -1), wa_bd, ba.reshape(1, -1), wx_bd,
      bx.reshape(1, -1), lam.reshape(1, -1))


def _sc_fwd(p, conv_w, col0, d_sc, cb, tc):
    t = p.shape[0]
    nc = d_sc // cb
    nt = t // tc
    width = conv_w.shape[0]
    base = col0 // cb

    def body(b_ref, c_ref, v_ref, w_ref, y_ref, halo):
        @pl.when(pl.program_id(1) == 0)
        def _():
            halo[...] = jnp.zeros_like(halo)

        cv = c_ref[...] * v_ref[...]
        cat = jnp.concatenate([halo[...], cv], axis=0)
        halo[...] = cv[tc - SUBLANE:, :]
        y_ref[...] = (b_ref[...] * _conv_fwd(cat, w_ref[...], width)).astype(BF16)

    blk = lambda slab: pl.BlockSpec((tc, cb), lambda j, s, slab=slab: (s, base + slab * nc + j))
    return pl.pallas_call(
        body,
        out_shape=jax.ShapeDtypeStruct((t, d_sc), BF16),
        grid=(nc, nt),
        in_specs=[blk(0), blk(1), blk(2), pl.BlockSpec((width, cb), lambda j, s: (0, j))],
        out_specs=pl.BlockSpec((tc, cb), lambda j, s: (s, j)),
        scratch_shapes=[pltpu.VMEM((SUBLANE, cb), F32)],
        compiler_params=_cparams(("parallel", "arbitrary"), 20 * tc * cb * 4),
        name="sc_fwd",
    )(p, p, p, conv_w)


def _sc_bwd(p, dyp, conv_w, col0, d_sc, cb, tc):
    t = p.shape[0]
    nc = d_sc // cb
    nt = t // tc
    width = conv_w.shape[0]
    base = col0 // cb
    halo_blocks = tc // SUBLANE

    def body(b_ref, c_ref, ch_ref, v_ref, vh_ref, dyp_ref, w_ref,
             db_ref, dc_ref, dv_ref, dw_ref, nxt_dq):
        s = pl.program_id(1)

        @pl.when(s == 0)
        def _():
            nxt_dq[...] = jnp.zeros_like(nxt_dq)
            dw_ref[...] = jnp.zeros_like(dw_ref)

        keep = jnp.where(s == nt - 1, 0.0, 1.0)
        cvals = c_ref[...]
        vvals = v_ref[...]
        w = w_ref[...]
        catcv = jnp.concatenate([ch_ref[...] * vh_ref[...] * keep, cvals * vvals], axis=0)
        q = _conv_fwd(catcv, w, width)
        dyp_v = dyp_ref[...]
        db_ref[...] = (dyp_v * q).astype(BF16)
        dq = dyp_v * b_ref[...]
        _conv_bwd_weight(dw_ref, dq, catcv, width)
        dcv = _conv_bwd_input(jnp.concatenate([dq, nxt_dq[...]], axis=0), w, width)
        nxt_dq[...] = dq[0:SUBLANE, :]
        dc_ref[...] = (dcv * vvals).astype(BF16)
        dv_ref[...] = (dcv * cvals).astype(BF16)

    rev = lambda s: nt - 1 - s
    blk = lambda slab: pl.BlockSpec((tc, cb), lambda j, s, slab=slab: (rev(s), base + slab * nc + j))
    halo = lambda slab: pl.BlockSpec(
        (SUBLANE, cb),
        lambda j, s, slab=slab: (jnp.maximum(rev(s) * halo_blocks - 1, 0), base + slab * nc + j))
    out_blk = pl.BlockSpec((tc, cb), lambda j, s: (rev(s), j))
    wblk = pl.BlockSpec((width, cb), lambda j, s: (0, j))
    act = jax.ShapeDtypeStruct((t, d_sc), BF16)
    return pl.pallas_call(
        body,
        out_shape=[act, act, act, jax.ShapeDtypeStruct((width, d_sc), F32)],
        grid=(nc, nt),
        in_specs=[blk(0), blk(1), halo(1), blk(2), halo(2), out_blk, wblk],
        out_specs=[out_blk, out_blk, out_blk, wblk],
        scratch_shapes=[pltpu.VMEM((SUBLANE, cb), F32)],
        compiler_params=_cparams(("parallel", "arbitrary"), 30 * tc * cb * 4),
        name="sc_bwd",
    )(p, p, p, p, p, dyp, conv_w)


def _merge_fwd(p, y_lru, y_sc, col0, tc):
    t, d = y_lru.shape
    cb = _pick(math.gcd(d, col0), 1024)
    nc = d // cb
    base = col0 // cb

    def body(gl_ref, gs_ref, yl_ref, ys_ref, o_ref):
        o_ref[...] = (_sigmoid(gl_ref[...]) * yl_ref[...] + _sigmoid(gs_ref[...]) * ys_ref[...]).astype(BF16)

    gate = lambda slab: pl.BlockSpec((tc, cb), lambda s, j, slab=slab: (s, base + slab * nc + j))
    blk = pl.BlockSpec((tc, cb), lambda s, j: (s, j))
    return pl.pallas_call(
        body,
        out_shape=jax.ShapeDtypeStruct((t, d), BF16),
        grid=(t // tc, nc),
        in_specs=[gate(0), gate(1), blk, blk],
        out_specs=blk,
        compiler_params=_cparams(("parallel", "parallel"), 2 * tc * cb * 20),
        name="merge_fwd",
    )(p, p, y_lru, y_sc)


def _merge_bwd(p, y_lru, y_sc, dmerged, col0, tc):
    t, d = y_lru.shape
    cb = _pick(math.gcd(d, col0), 1024)
    nc = d // cb
    base = col0 // cb

    def body(gl_ref, gs_ref, yl_ref, ys_ref, dm_ref, dgl_ref, dgs_ref, dyl_ref, dys_ref):
        dm = dm_ref[...]
        sl = _sigmoid(gl_ref[...])
        ss = _sigmoid(gs_ref[...])
        dgl_ref[...] = (dm * yl_ref[...] * (sl * (1.0 - sl))).astype(BF16)
        dgs_ref[...] = (dm * ys_ref[...] * (ss * (1.0 - ss))).astype(BF16)
        dyl_ref[...] = (dm * sl).astype(BF16)
        dys_ref[...] = (dm * ss).astype(BF16)

    gate = lambda slab: pl.BlockSpec((tc, cb), lambda s, j, slab=slab: (s, base + slab * nc + j))
    blk = pl.BlockSpec((tc, cb), lambda s, j: (s, j))
    act = jax.ShapeDtypeStruct((t, d), BF16)
    return pl.pallas_call(
        body,
        out_shape=[act, act, act, act],
        grid=(t // tc, nc),
        in_specs=[gate(0), gate(1), blk, blk, blk],
        out_specs=[blk, blk, blk, blk],
        compiler_params=_cparams(("parallel", "parallel"), 2 * tc * cb * 28),
        name="merge_bwd",
    )(p, p, y_lru, y_sc, dmerged)


def _ffn_act_fwd(up, conv_w, d_ff, cb, tc):
    t = up.shape[0]
    nc = d_ff // cb
    nt = t // tc
    width = conv_w.shape[0]

    def body(g_ref, v_ref, wg_ref, wv_ref, o_ref, halo_g, halo_v):
        @pl.when(pl.program_id(1) == 0)
        def _():
            halo_g[...] = jnp.zeros_like(halo_g)
            halo_v[...] = jnp.zeros_like(halo_v)

        g = g_ref[...]
        v = v_ref[...]
        ug = _conv_fwd(jnp.concatenate([halo_g[...], g], axis=0), wg_ref[...], width)
        uv = _conv_fwd(jnp.concatenate([halo_v[...], v], axis=0), wv_ref[...], width)
        halo_g[...] = g[tc - SUBLANE:, :]
        halo_v[...] = v[tc - SUBLANE:, :]
        o_ref[...] = (ug * _sigmoid(ug) * uv).astype(BF16)

    blk = lambda half: pl.BlockSpec((tc, cb), lambda j, s, half=half: (s, half * nc + j))
    wblk = lambda half: pl.BlockSpec((width, cb), lambda j, s, half=half: (0, half * nc + j))
    return pl.pallas_call(
        body,
        out_shape=jax.ShapeDtypeStruct((t, d_ff), BF16),
        grid=(nc, nt),
        in_specs=[blk(0), blk(1), wblk(0), wblk(1)],
        out_specs=pl.BlockSpec((tc, cb), lambda j, s: (s, j)),
        scratch_shapes=[pltpu.VMEM((SUBLANE, cb), F32), pltpu.VMEM((SUBLANE, cb), F32)],
        compiler_params=_cparams(("parallel", "arbitrary"), 24 * tc * cb * 4),
        name="ffn_act_fwd",
    )(up, up, conv_w, conv_w)


def _ffn_act_bwd(up, dact, conv_w, d_ff, cb, tc):
    t = up.shape[0]
    nc = d_ff // cb
    nt = t // tc
    width = conv_w.shape[0]
    halo_blocks = tc // SUBLANE

    def body(g_ref, gh_ref, v_ref, vh_ref, da_ref, wg_ref, wv_ref,
             dg_ref, dv_ref, dwg_ref, dwv_ref, nxt_g, nxt_v):
        s = pl.program_id(1)

        @pl.when(s == 0)
        def _():
            nxt_g[...] = jnp.zeros_like(nxt_g)
            nxt_v[...] = jnp.zeros_like(nxt_v)
            dwg_ref[...] = jnp.zeros_like(dwg_ref)
            dwv_ref[...] = jnp.zeros_like(dwv_ref)

        keep = jnp.where(s == nt - 1, 0.0, 1.0)
        wg = wg_ref[...]
        wv = wv_ref[...]
        catg = jnp.concatenate([gh_ref[...] * keep, g_ref[...]], axis=0)
        catv = jnp.concatenate([vh_ref[...] * keep, v_ref[...]], axis=0)
        ug = _conv_fwd(catg, wg, width)
        uv = _conv_fwd(catv, wv, width)
        sg = _sigmoid(ug)
        da = da_ref[...]
        dug = da * uv * (sg * (1.0 + ug * (1.0 - sg)))
        duv = da * (ug * sg)
        _conv_bwd_weight(dwg_ref, dug, catg, width)
        _conv_bwd_weight(dwv_ref, duv, catv, width)
        dg_ref[...] = _conv_bwd_input(jnp.concatenate([dug, nxt_g[...]], axis=0), wg, width).astype(BF16)
        dv_ref[...] = _conv_bwd_input(jnp.concatenate([duv, nxt_v[...]], axis=0), wv, width).astype(BF16)
        nxt_g[...] = dug[0:SUBLANE, :]
        nxt_v[...] = duv[0:SUBLANE, :]

    rev = lambda s: nt - 1 - s
    blk = lambda half: pl.BlockSpec((tc, cb), lambda j, s, half=half: (rev(s), half * nc + j))
    halo = lambda half: pl.BlockSpec(
        (SUBLANE, cb), lambda j, s, half=half: (jnp.maximum(rev(s) * halo_blocks - 1, 0), half * nc + j))
    wblk = lambda half: pl.BlockSpec((width, cb), lambda j, s, half=half: (0, half * nc + j))
    out_blk = pl.BlockSpec((tc, cb), lambda j, s: (rev(s), j))
    wout = pl.BlockSpec((width, cb), lambda j, s: (0, j))
    act = jax.ShapeDtypeStruct((t, d_ff), BF16)
    wshape = jax.ShapeDtypeStruct((width, d_ff), F32)
    return pl.pallas_call(
        body,
        out_shape=[act, act, wshape, wshape],
        grid=(nc, nt),
        in_specs=[blk(0), halo(0), blk(1), halo(1), out_blk, wblk(0), wblk(1)],
        out_specs=[out_blk, out_blk, wout, wout],
        scratch_shapes=[pltpu.VMEM((SUBLANE, cb), F32), pltpu.VMEM((SUBLANE, cb), F32)],
        compiler_params=_cparams(("parallel", "arbitrary"), 40 * tc * cb * 4),
        name="ffn_act_bwd",
    )(up, up, up, up, dact, conv_w, conv_w)


def _mesh_pos():
    x, y, c = lax.axis_index("x"), lax.axis_index("y"), lax.axis_index("c")
    return x, y, c


def _other_chips(x, y):
    return [(1 - x, y), (x, 1 - y), (1 - x, 1 - y)]


def _cast_place(w, chip, col_sharded, name):
    r, cdim = w.shape
    full = (r, cdim * N_CHIPS) if col_sharded else (r * N_CHIPS, cdim)
    rb = _pick(r, max(BF16_ROWS, (512 * 1024) // cdim), BF16_ROWS)
    nb = r // rb

    def body(chip_ref, w_ref, o_ref):
        o_ref[...] = w_ref[...].astype(BF16)

    if col_sharded:
        out_map = lambda i, chip_ref: (i, chip_ref[0])
    else:
        out_map = lambda i, chip_ref: (chip_ref[0] * nb + i, 0)
    grid_spec = pltpu.PrefetchScalarGridSpec(
        num_scalar_prefetch=1,
        grid=(nb,),
        in_specs=[pl.BlockSpec((rb, cdim), lambda i, chip_ref: (i, 0))],
        out_specs=pl.BlockSpec((rb, cdim), out_map),
    )
    return pl.pallas_call(
        body,
        out_shape=jax.ShapeDtypeStruct(full, BF16),
        grid_spec=grid_spec,
        compiler_params=_cparams(("parallel",), 2 * rb * cdim * 6),
        name=name,
    )(chip, w)


def _gather_weights(fulls, shard_shapes, col_sharded, small):
    n = len(fulls)
    shards = [jax.ShapeDtypeStruct(s, BF16) for s in shard_shapes]
    out_shape = [jax.ShapeDtypeStruct(f.shape, f.dtype) for f in fulls]
    out_shape.append(jax.ShapeDtypeStruct((N_CHIPS,) + small.shape, small.dtype))

    def body(*refs):
        small_ref = refs[n]
        outs, small_out = refs[n + 1:2 * n + 1], refs[2 * n + 1]
        send_sems, recv_sems, small_send, small_recv = refs[2 * n + 2:]
        x, y, c = _mesh_pos()
        me = 2 * x + y
        chips = _other_chips(x, y)
        sibling = (x, y, 1 - c)

        def half_of(k, ref, chip, half):
            r, cdim = shards[k].shape
            h = r // 2
            if col_sharded[k]:
                return ref.at[pl.ds(pl.multiple_of(half * h, BF16_ROWS), h),
                              pl.ds(pl.multiple_of(chip * cdim, LANE), cdim)]
            return ref.at[pl.ds(pl.multiple_of(chip * r + half * h, BF16_ROWS), h), :]

        def remote(k, slot, src, dst, to):
            return pltpu.make_async_remote_copy(
                src_ref=src, dst_ref=dst, send_sem=send_sems.at[k, slot], recv_sem=recv_sems.at[k, slot],
                device_id=to, device_id_type=MESH)

        small_out[me] = small_ref[...]
        small_copies = []
        for j, (px, py) in enumerate(chips):
            cp = pltpu.make_async_remote_copy(
                src_ref=small_ref, dst_ref=small_out.at[me], send_sem=small_send.at[j],
                recv_sem=small_recv.at[j], device_id=(px, py, c), device_id_type=MESH)
            cp.start()
            small_copies.append(cp)

        sends = []
        for k in range(n):
            mine = half_of(k, outs[k], me, c)
            for j, (px, py) in enumerate(chips):
                cp = remote(k, j, mine, mine, (px, py, c))
                cp.start()
                sends.append(cp)
        for k in range(n):
            for j, (px, py) in enumerate(chips):
                landed = half_of(k, outs[k], 2 * px + py, c)
                remote(k, j, landed, landed, (px, py, c)).wait_recv()
                cp = remote(k, 3 + j, landed, landed, sibling)
                cp.start()
                sends.append(cp)
        for k in range(n):
            for j, (px, py) in enumerate(chips):
                from_sibling = half_of(k, outs[k], 2 * px + py, 1 - c)
                remote(k, 3 + j, from_sibling, from_sibling, sibling).wait_recv()
        for j, (px, py) in enumerate(chips):
            pltpu.make_async_remote_copy(
                src_ref=small_ref, dst_ref=small_out.at[2 * px + py], send_sem=small_send.at[j],
                recv_sem=small_recv.at[j], device_id=(px, py, c), device_id_type=MESH).wait_recv()
        for cp in small_copies + sends:
            cp.wait_send()

    return pl.pallas_call(
        body,
        out_shape=out_shape,
        in_specs=[ANY] * n + [VMEM_SPEC],
        out_specs=[ANY] * n + [VMEM_SPEC],
        input_output_aliases={k: k for k in range(n)},
        scratch_shapes=[pltpu.SemaphoreType.DMA((n, 6)), pltpu.SemaphoreType.DMA((n, 6)),
                        pltpu.SemaphoreType.DMA((N_CHIPS - 1,)), pltpu.SemaphoreType.DMA((N_CHIPS - 1,))],
        name="gather_weights",
    )(*fulls, small)


def _as3d(g, col_sharded):
    r, cdim = g.shape
    return g.reshape(1, r, cdim) if col_sharded else g.reshape(N_CHIPS, r // N_CHIPS, cdim)


def _pair_exchange(grads3):
    n = len(grads3)
    outs = [jax.ShapeDtypeStruct((g.shape[0], g.shape[1] // 2, g.shape[2]), g.dtype) for g in grads3]

    def body(*refs):
        ins, lands = refs[:n], refs[n:2 * n]
        send_sems, recv_sems = refs[2 * n:]
        x, y, c = _mesh_pos()
        sibling = (x, y, 1 - c)
        copies = []
        for k in range(n):
            h = grads3[k].shape[1] // 2
            src = ins[k].at[:, pl.ds(pl.multiple_of((1 - c) * h, SUBLANE), h), :]
            cp = pltpu.make_async_remote_copy(
                src_ref=src, dst_ref=lands[k], send_sem=send_sems.at[k], recv_sem=recv_sems.at[k],
                device_id=sibling, device_id_type=MESH)
            cp.start()
            copies.append(cp)
        for cp in copies:
            cp.wait()

    return pl.pallas_call(
        body,
        out_shape=outs,
        in_specs=[ANY] * n,
        out_specs=[ANY] * n,
        scratch_shapes=[pltpu.SemaphoreType.DMA((n,)), pltpu.SemaphoreType.DMA((n,))],
        name="grad_pair_exchange",
    )(*grads3)


def _pair_add(g3, other, core):
    a, r, cdim = g3.shape
    h = r // 2
    rb = _pick(h, max(BF16_ROWS, (512 * 1024) // cdim), BF16_ROWS)
    nb = h // rb

    def body(core_ref, g_ref, o_ref, out_ref):
        out_ref[...] = (g_ref[...] + o_ref[...]).astype(BF16)

    grid_spec = pltpu.PrefetchScalarGridSpec(
        num_scalar_prefetch=1,
        grid=(a, nb),
        in_specs=[pl.BlockSpec((None, rb, cdim), lambda i, j, core_ref: (i, core_ref[0] * nb + j, 0)),
                  pl.BlockSpec((None, rb, cdim), lambda i, j, core_ref: (i, j, 0))],
        out_specs=pl.BlockSpec((None, rb, cdim), lambda i, j, core_ref: (i, j, 0)),
    )
    return pl.pallas_call(
        body,
        out_shape=jax.ShapeDtypeStruct((a, h, cdim), BF16),
        grid_spec=grid_spec,
        compiler_params=_cparams(("parallel", "parallel"), 2 * rb * cdim * 10),
        name="grad_pair_add",
    )(core, g3, other)


def _chip_exchange(partials, col_sharded, small):
    n = len(partials)
    blocks = []
    for pz, cs in zip(partials, col_sharded):
        a, h, cdim = pz.shape
        blocks.append((h, cdim // N_CHIPS) if cs else (h, cdim))
    outs = [jax.ShapeDtypeStruct((N_CHIPS - 1,) + b, BF16) for b in blocks]
    outs.append(jax.ShapeDtypeStruct(small.shape, F32))

    def body(*refs):
        ins, small_ref = refs[:n], refs[n]
        lands, small_out = refs[n + 1:2 * n + 1], refs[2 * n + 1]
        slots, send_sems, recv_sems, small_send, small_recv = refs[2 * n + 2:]
        x, y, c = _mesh_pos()
        me = 2 * x + y
        my_dev = 4 * x + 2 * y + c
        chips = _other_chips(x, y)

        def block_of(k, chip):
            if col_sharded[k]:
                w = blocks[k][1]
                return ins[k].at[0, :, pl.ds(pl.multiple_of(chip * w, LANE), w)]
            return ins[k].at[chip]

        slots[my_dev] = small_ref[...]
        small_copies = []
        for m in range(1, N_DEV):
            peer = (x ^ ((m >> 2) & 1), y ^ ((m >> 1) & 1), c ^ (m & 1))
            cp = pltpu.make_async_remote_copy(
                src_ref=small_ref, dst_ref=slots.at[my_dev], send_sem=small_send.at[m - 1],
                recv_sem=small_recv.at[m - 1], device_id=peer, device_id_type=MESH)
            cp.start()
            small_copies.append(cp)

        copies = []
        for k in range(n):
            for j, (px, py) in enumerate(chips):
                cp = pltpu.make_async_remote_copy(
                    src_ref=block_of(k, 2 * px + py), dst_ref=lands[k].at[j],
                    send_sem=send_sems.at[k, j], recv_sem=recv_sems.at[k, j],
                    device_id=(px, py, c), device_id_type=MESH)
                cp.start()
                copies.append(cp)

        for m in range(1, N_DEV):
            peer_dev = my_dev ^ m
            pltpu.make_async_remote_copy(
                src_ref=small_ref, dst_ref=slots.at[peer_dev], send_sem=small_send.at[m - 1],
                recv_sem=small_recv.at[m - 1], device_id=(x, y, c), device_id_type=MESH).wait_recv()
        total = slots[0]
        for d in range(1, N_DEV):
            total = total + slots[d]
        small_out[...] = total
        for cp in small_copies:
            cp.wait_send()

        for k in range(n):
            for j, (px, py) in enumerate(chips):
                pltpu.make_async_remote_copy(
                    src_ref=block_of(k, me), dst_ref=lands[k].at[j],
                    send_sem=send_sems.at[k, j], recv_sem=recv_sems.at[k, j],
                    device_id=(px, py, c), device_id_type=MESH).wait_recv()
        for cp in copies:
            cp.wait_send()

    return pl.pallas_call(
        body,
        out_shape=outs,
        in_specs=[ANY] * n + [VMEM_SPEC],
        out_specs=[ANY] * n + [VMEM_SPEC],
        scratch_shapes=[pltpu.VMEM((N_DEV,) + small.shape, F32),
                        pltpu.SemaphoreType.DMA((n, 3)), pltpu.SemaphoreType.DMA((n, 3)),
                        pltpu.SemaphoreType.DMA((N_DEV - 1,)), pltpu.SemaphoreType.DMA((N_DEV - 1,))],
        compiler_params=pltpu.CompilerParams(
            vmem_limit_bytes=min(VMEM_BUDGET, (N_DEV + 4) * _nbytes(small.shape, F32) + (8 << 20))),
        name="grad_chip_exchange",
    )(*partials, small)


def _chip_sum(partial, land, where, col_sharded):
    _, h, cdim = land.shape
    rb = _pick(h, max(BF16_ROWS, (512 * 1024) // cdim), BF16_ROWS)
    nb = h // rb

    def body(where_ref, own_ref, l_ref, o_ref):
        total = own_ref[...].astype(F32)
        for j in range(N_CHIPS - 1):
            total = total + l_ref[j].astype(F32)
        o_ref[...] = total

    if col_sharded:
        own_map = lambda i, w: (0, i, w[0])
    else:
        own_map = lambda i, w: (w[0], i, 0)
    grid_spec = pltpu.PrefetchScalarGridSpec(
        num_scalar_prefetch=1,
        grid=(nb,),
        in_specs=[pl.BlockSpec((None, rb, cdim), own_map),
                  pl.BlockSpec((N_CHIPS - 1, rb, cdim), lambda i, w: (0, i, 0))],
        out_specs=pl.BlockSpec((rb, cdim), lambda i, w: (w[1] * nb + i, 0)),
    )
    return pl.pallas_call(
        body,
        out_shape=jax.ShapeDtypeStruct((2 * h, cdim), F32),
        grid_spec=grid_spec,
        compiler_params=_cparams(("parallel",), 2 * rb * cdim * 12),
        name="grad_chip_sum",
    )(where, partial, land)


def _pair_share(shards):
    n = len(shards)

    def body(*refs):
        fulls = refs[n:2 * n]
        send_sems, recv_sems = refs[2 * n:]
        x, y, c = _mesh_pos()
        sibling = (x, y, 1 - c)
        started = []
        for k in range(n):
            h = shards[k].shape[0] // 2
            mine = fulls[k].at[pl.ds(pl.multiple_of(c * h, SUBLANE), h), :]
            rc = pltpu.make_async_remote_copy(
                src_ref=mine, dst_ref=mine, send_sem=send_sems.at[k], recv_sem=recv_sems.at[k],
                device_id=sibling, device_id_type=MESH)
            rc.start()
            started.append(rc)
        for k, rc in enumerate(started):
            h = shards[k].shape[0] // 2
            theirs = fulls[k].at[pl.ds(pl.multiple_of((1 - c) * h, SUBLANE), h), :]
            pltpu.make_async_remote_copy(
                src_ref=theirs, dst_ref=theirs, send_sem=send_sems.at[k], recv_sem=recv_sems.at[k],
                device_id=sibling, device_id_type=MESH).wait_recv()
            rc.wait_send()

    return pl.pallas_call(
        body,
        out_shape=[jax.ShapeDtypeStruct(s.shape, s.dtype) for s in shards],
        in_specs=[ANY] * n,
        out_specs=[ANY] * n,
        input_output_aliases={k: k for k in range(n)},
        scratch_shapes=[pltpu.SemaphoreType.DMA((n,)), pltpu.SemaphoreType.DMA((n,))],
        name="grad_pair_share",
    )(*shards)


def _adamw(w, g, m, v, name):
    r, cdim = w.shape
    rb = _pick(r, max(SUBLANE, (256 * 1024) // cdim), SUBLANE)
    c1 = 1.0 - ADAM_B1 ** ADAM_STEP
    c2 = 1.0 - ADAM_B2 ** ADAM_STEP

    def body(w_ref, g_ref, m_ref, v_ref, go_ref, d_ref, mo_ref, vo_ref):
        gv = g_ref[...]
        mn = ADAM_B1 * m_ref[...] + (1.0 - ADAM_B1) * gv
        vn = ADAM_B2 * v_ref[...] + (1.0 - ADAM_B2) * (gv * gv)
        m_hat = mn / c1
        v_hat = vn / c2
        d_ref[...] = -ADAM_LR * (m_hat / (jnp.sqrt(v_hat) + ADAM_EPS) + ADAM_WD * w_ref[...])
        go_ref[...] = gv
        mo_ref[...] = mn
        vo_ref[...] = vn

    blk = pl.BlockSpec((rb, cdim), lambda i: (i, 0))
    shape = jax.ShapeDtypeStruct((r, cdim), F32)
    return pl.pallas_call(
        body,
        out_shape=[shape] * 4,
        grid=(r // rb,),
        in_specs=[blk] * 4,
        out_specs=[blk] * 4,
        compiler_params=_cparams(("parallel",), 2 * rb * cdim * 4 * 8),
        name=name,
    )(w, g, m, v)


def _pack(arrays):
    tile = SUBLANE * LANE
    pieces = []
    for arr in arrays:
        flat = arr.reshape(-1)
        pad = (-flat.shape[0]) % tile
        if pad:
            flat = jnp.concatenate([flat, jnp.zeros((pad,), flat.dtype)])
        pieces.append(flat)
    return jnp.concatenate(pieces).reshape(-1, LANE)


def _unpack(packed, shapes):
    tile = SUBLANE * LANE
    flat = packed.reshape(-1)
    out, off = [], 0
    for shp in shapes:
        size = math.prod(shp)
        out.append(flat[off:off + size].reshape(shp))
        off += size + ((-size) % tile)
    return out


def _block_diag_groups(w, per_group):
    hcount, hd, _ = w.shape
    ng = hcount // per_group
    w4 = w.reshape(ng, per_group, hd, hd)
    eye = jnp.eye(per_group, dtype=w.dtype)
    bd = w4[:, :, :, None, :] * eye[None, :, None, :, None]
    return bd.reshape(ng, per_group * hd, per_group * hd).astype(BF16)


def _diag_blocks(wbd, per_group, hd):
    ng = wbd.shape[0]
    w5 = wbd.reshape(ng, per_group, hd, per_group, hd)
    blocks = [w5[:, i, :, i, :] for i in range(per_group)]
    return jnp.stack(blocks, axis=1).reshape(ng * per_group, hd, hd)


def kernel(x, g_mix, w_in, lru_conv_w, lru_conv_b, lru_wa, lru_ba, lru_wx, lru_bx, lru_lambda, lru_w_out, sc_conv_w, sc_w_out, w_o, g_ffn, ffn_w_up, ffn_conv_w, ffn_w_down, g_final, loss_target, m_g_mix, m_w_in, m_lru_conv_w, m_lru_conv_b, m_lru_wa, m_lru_ba, m_lru_wx, m_lru_bx, m_lru_lambda, m_lru_w_out, m_sc_conv_w, m_sc_w_out, m_w_o, m_g_ffn, m_ffn_w_up, m_ffn_conv_w, m_ffn_w_down, m_g_final, v_g_mix, v_w_in, v_lru_conv_w, v_lru_conv_b, v_lru_wa, v_lru_ba, v_lru_wx, v_lru_bx, v_lru_lambda, v_lru_w_out, v_sc_conv_w, v_sc_w_out, v_w_o, v_g_ffn, v_ffn_w_up, v_ffn_conv_w, v_ffn_w_down, v_g_final):
    seq, d_model = x.shape[1], x.shape[2]
    heads, head_dim, _ = lru_wa.shape
    d_lru = heads * head_dim
    d_sc = sc_w_out.shape[0]
    d_ff = ffn_w_down.shape[0] * N_CHIPS
    assert x.shape[0] == 1 and w_in.shape[1] * N_CHIPS == 2 * d_lru + 3 * d_sc + 2 * d_model
    xs = x.reshape(seq, d_model)
    target = loss_target.reshape(seq, d_model)

    chip = 2 * lax.axis_index("x") + lax.axis_index("y")
    core = lax.axis_index("c").astype(jnp.int32).reshape(1)

    big_w = [w_in, lru_w_out, sc_w_out, w_o, ffn_w_up, ffn_w_down]
    big_m = [m_w_in, m_lru_w_out, m_sc_w_out, m_w_o, m_ffn_w_up, m_ffn_w_down]
    big_v = [v_w_in, v_lru_w_out, v_sc_w_out, v_w_o, v_ffn_w_up, v_ffn_w_down]
    col_sharded = [True, True, True, False, True, False]
    conv_shards = [lru_conv_w, sc_conv_w, ffn_conv_w]
    conv_pack = jnp.concatenate(
        [jnp.pad(w, ((0, SUBLANE - w.shape[0]), (0, 0))) for w in conv_shards], axis=1)
    big_names = ["w_in", "lru_w_out", "sc_w_out", "w_o", "ffn_w_up", "ffn_w_down"]
    chip_arr = chip.astype(jnp.int32).reshape(1)
    placed = [_cast_place(w, chip_arr, cs, "cast_" + nm) for w, cs, nm in zip(big_w, col_sharded, big_names)]
    *big_full, conv_all = _gather_weights(placed, [w.shape for w in big_w], col_sharded, conv_pack)
    win_b, wlo_b, wso_b, wo_b, wup_b, wdn_b = big_full
    conv_full, off = [], 0
    for w in conv_shards:
        kw, nq = w.shape
        piece = conv_all[:, :kw, off:off + nq]
        conv_full.append(piece.transpose(1, 0, 2).reshape(kw, N_CHIPS * nq))
        off += nq
    lcw, scw, fcw = conv_full

    per_group = max(1, min(heads, 256 // head_dim))
    gc = per_group * head_dim
    wa_bd = _block_diag_groups(lru_wa, per_group)
    wx_bd = _block_diag_groups(lru_wx, per_group)
    tc = _pick(seq, 256, SUBLANE)
    cb_sc = _pick(d_sc, 512)
    cb_ff = _pick(d_ff, 512)
    col_sc = 2 * d_lru
    col_gates = 2 * d_lru + 3 * d_sc

    h1 = _rms_fwd(xs, g_mix, "rms_mix")
    p = _mm(h1, win_b, "nn", F32, name="mm_in")
    y_lru_pre, hseq = _lru_fwd(p, lcw, lru_conv_b, wa_bd, lru_ba, wx_bd, lru_bx, lru_lambda, d_lru, gc, tc)
    y_sc_pre = _sc_fwd(p, scw, col_sc, d_sc, cb_sc, tc)
    y_lru = _mm(y_lru_pre, wlo_b, "nn", F32, name="mm_lru_out")
    y_sc = _mm(y_sc_pre, wso_b, "nn", F32, name="mm_sc_out")
    merged = _merge_fwd(p, y_lru, y_sc, col_gates, tc)
    x2 = _mm(merged, wo_b, "nn", F32, res=xs, name="mm_o")
    h2 = _rms_fwd(x2, g_ffn, "rms_ffn")
    up = _mm(h2, wup_b, "nn", F32, name="mm_up")
    act = _ffn_act_fwd(up, fcw, d_ff, cb_ff, tc)
    x3 = _mm(act, wdn_b, "nn", F32, res=x2, name="mm_down")
    loss_part, dx3, dx3b, dg_final = _loss_head(x3, g_final, target)

    dact = _mm(dx3b, wdn_b, "nt", F32, name="mm_down_dx")
    g_wdn = _mm(act, dx3b, "tn", F32, name="mm_down_dw")
    dupg, dupv, dfcw_g, dfcw_v = _ffn_act_bwd(up, dact, fcw, d_ff, cb_ff, tc)
    dup = jnp.concatenate([dupg, dupv], axis=1)
    dh2 = _mm(dup, wup_b, "nt", F32, name="mm_up_dx")
    g_wup = _mm(h2, dup, "tn", F32, name="mm_up_dw")
    dx2, dx2b, dg_ffn = _rms_bwd(x2, g_ffn, dh2, dx3, "rms_ffn_bwd", True)
    dmerged = _mm(dx2b, wo_b, "nt", F32, name="mm_o_dx")
    g_wo = _mm(merged, dx2b, "tn", F32, name="mm_o_dw")
    dgl, dgs, dyl, dys = _merge_bwd(p, y_lru, y_sc, dmerged, col_gates, tc)
    dylp = _mm(dyl, wlo_b, "nt", F32, name="mm_lru_out_dx")
    g_wlo = _mm(y_lru_pre, dyl, "tn", F32, name="mm_lru_out_dw")
    dysp = _mm(dys, wso_b, "nt", F32, name="mm_sc_out_dx")
    g_wso = _mm(y_sc_pre, dys, "tn", F32, name="mm_sc_out_dw")
    dlx, dlgate, dlcw, dlcb, dwa_bd, dba, dwx_bd, dbx, dlam = _lru_bwd(
        p, hseq, dylp, lcw, lru_conv_b, wa_bd, lru_ba, wx_bd, lru_bx, lru_lambda, d_lru, gc, tc)
    dsb, dsc, dsv, dscw = _sc_bwd(p, dysp, scw, col_sc, d_sc, cb_sc, tc)
    dp = jnp.concatenate([dlx, dlgate, dsb, dsc, dsv, dgl, dgs], axis=1)
    dh1 = _mm(dp, win_b, "nt", F32, name="mm_in_dx")
    g_win = _mm(h1, dp, "tn", F32, name="mm_in_dw")
    grad_x, dg_mix = _rms_bwd(xs, g_mix, dh1, dx2, "rms_mix_bwd", False)

    big_g = [g_win, g_wlo, g_wso, g_wo, g_wup, g_wdn]
    grads3 = [_as3d(g, cs) for g, cs in zip(big_g, col_sharded)]
    from_sibling = _pair_exchange(grads3)
    partials = [_pair_add(g3, o, core) for g3, o in zip(grads3, from_sibling)]
    small_g = [dg_mix, dlcw, dlcb, _diag_blocks(dwa_bd, per_group, head_dim), dba,
               _diag_blocks(dwx_bd, per_group, head_dim), dbx, dlam, dscw, dg_ffn,
               jnp.concatenate([dfcw_g, dfcw_v], axis=1), dg_final]
    small_shapes = [a.shape for a in small_g]
    *lands, small_sum = _chip_exchange(partials, col_sharded, _pack(small_g))
    where = jnp.concatenate([chip_arr, core])
    shard_g = _pair_share([_chip_sum(pz, land, where, cs) for pz, land, cs in zip(partials, lands, col_sharded)])

    big_out = [_adamw(w, g, m, v, "adamw_" + nm)
               for nm, w, g, m, v in zip(big_names, big_w, shard_g, big_m, big_v)]
    sg = _unpack(small_sum, small_shapes)
    for idx in (1, 8, 10):
        nq = sg[idx].shape[1] // N_CHIPS
        sg[idx] = lax.dynamic_slice_in_dim(sg[idx], chip * nq, nq, axis=1)
    small_w = [g_mix, lru_conv_w, lru_conv_b, lru_wa, lru_ba, lru_wx, lru_bx, lru_lambda, sc_conv_w,
               g_ffn, ffn_conv_w, g_final]
    small_m = [m_g_mix, m_lru_conv_w, m_lru_conv_b, m_lru_wa, m_lru_ba, m_lru_wx, m_lru_bx, m_lru_lambda,
               m_sc_conv_w, m_g_ffn, m_ffn_conv_w, m_g_final]
    small_v = [v_g_mix, v_lru_conv_w, v_lru_conv_b, v_lru_wa, v_lru_ba, v_lru_wx, v_lru_bx, v_lru_lambda,
               v_sc_conv_w, v_g_ffn, v_ffn_conv_w, v_g_final]
    sg = [g.reshape(w.shape) for g, w in zip(sg, small_w)]
    w_shapes = [w.shape for w in small_w]
    packed = _adamw(_pack(small_w), _pack(sg), _pack(small_m), _pack(small_v), "adamw_small")
    small_out = [_unpack(pk, w_shapes) for pk in packed]

    order = [(0, 0), (1, 0), (0, 1), (0, 2), (0, 3), (0, 4), (0, 5), (0, 6), (0, 7), (1, 1), (0, 8), (1, 2),
             (1, 3), (0, 9), (1, 4), (0, 10), (1, 5), (0, 11)]
    by_kind = []
    for kind in range(4):
        by_kind.append([big_out[i][kind] if is_big else small_out[kind][i] for is_big, i in order])
    loss = lax.psum(loss_part[0, 0], ("x", "y", "c"))
    return (loss, grad_x.reshape(x.shape), *by_kind[0], *by_kind[1], *by_kind[2], *by_kind[3])
```

```python
import functools
import math

import jax
import jax.numpy as jnp
from jax import lax
from jax.experimental import pallas as pl
from jax.experimental.pallas import tpu as pltpu

F32 = jnp.float32
BF16 = jnp.bfloat16

LANE = 128
SUBLANE = 8
BF16_ROWS = 16
VMEM_BYTES_V7X = 64 * 1024 * 1024
VMEM_BUDGET = VMEM_BYTES_V7X - 8 * 1024 * 1024

EPS = 1e-6
LRU_C = 8.0
ADAM_LR = 0.001
ADAM_B1 = 0.9
ADAM_B2 = 0.999
ADAM_EPS = 1e-08
ADAM_WD = 0.01
ADAM_STEP = 10

N_CHIPS = 4
N_DEV = 8
MESH = pl.DeviceIdType.MESH
ANY = pl.BlockSpec(memory_space=pl.ANY)
VMEM_SPEC = pl.BlockSpec(memory_space=pltpu.VMEM)
HBM_SPEC = pl.BlockSpec(memory_space=pltpu.HBM)
SEM_SPEC = pl.BlockSpec(memory_space=pltpu.SEMAPHORE)
DATAFLOW_EFFECT = pltpu.SideEffectType.DATAFLOW_SIDE_EFFECTING


def _pick(n, cap, mult=LANE):
    best = None
    d = mult
    while d <= min(n, cap):
        if n % d == 0:
            best = d
        d += mult
    return n if best is None else best


def _cparams(semantics, block_bytes):
    limit = min(VMEM_BUDGET, max(32 * 1024 * 1024, int(block_bytes * 1.25) + (4 << 20)))
    return pltpu.CompilerParams(dimension_semantics=semantics, vmem_limit_bytes=limit)


def _nbytes(shape, dtype):
    return math.prod(shape) * jnp.dtype(dtype).itemsize


def _sigmoid(z):
    return 1.0 / (1.0 + jnp.exp(-z))


def _softplus(z):
    e = jnp.exp(-jnp.abs(z))
    u = 1.0 + e
    log1p = jnp.where(u == 1.0, e, jnp.log(u) * (e / (u - 1.0)))
    return jnp.maximum(z, 0.0) + log1p


def _neg_expm1(z):
    small = z * (1.0 + z * (0.5 + z * (1.0 / 6.0 + z * (1.0 / 24.0))))
    return -jnp.where(jnp.abs(z) < 0.03, small, jnp.exp(z) - 1.0)


_GELU_K = math.sqrt(2.0 / math.pi)
_GELU_C = 0.044715


def _gelu_and_grad(z):
    z2 = z * z
    th = jnp.tanh(_GELU_K * (z + _GELU_C * z2 * z))
    val = 0.5 * z * (1.0 + th)
    grad = 0.5 * (1.0 + th) + 0.5 * z * (1.0 - th * th) * (_GELU_K * (1.0 + 3.0 * _GELU_C * z2))
    return val, grad


def _rows_before(cat, k):
    if k == 0:
        return cat[SUBLANE:, :]
    return pltpu.roll(cat, k, 0)[SUBLANE:, :]


def _rows_after(cat, k):
    n = cat.shape[0]
    if k == 0:
        return cat[:n - SUBLANE, :]
    return pltpu.roll(cat, n - k, 0)[:n - SUBLANE, :]


def _conv_fwd(cat, w, width):
    y = _rows_before(cat, width - 1) * w[0:1, :]
    for k in range(1, width):
        y = y + _rows_before(cat, width - 1 - k) * w[k:k + 1, :]
    return y


def _conv_bwd_input(cat, w, width):
    dx = _rows_after(cat, width - 1) * w[0:1, :]
    for k in range(1, width):
        dx = dx + _rows_after(cat, width - 1 - k) * w[k:k + 1, :]
    return dx


def _conv_bwd_weight(dw_ref, dy, catx, width):
    for k in range(width):
        dw_ref[k:k + 1, :] += jnp.sum(dy * _rows_before(catx, width - 1 - k), axis=0, keepdims=True)


def _scan_tiles(a_ref, b_ref, out_ref, carry0, n_rows, reverse):
    cols = a_ref.shape[1]
    row = lax.broadcasted_iota(jnp.int32, (SUBLANE, cols), 0)
    n_tiles = n_rows // SUBLANE

    def step(j, carry):
        tile = (n_tiles - 1 - j) if reverse else j
        off = pl.multiple_of(tile * SUBLANE, SUBLANE)
        a = a_ref[pl.ds(off, SUBLANE), :]
        b = b_ref[pl.ds(off, SUBLANE), :]
        for s in (1, 2, 4):
            if reverse:
                keep = row < SUBLANE - s
                shift = SUBLANE - s
            else:
                keep = row >= s
                shift = s
            a_sh = jnp.where(keep, pltpu.roll(a, shift, 0), 1.0)
            b_sh = jnp.where(keep, pltpu.roll(b, shift, 0), 0.0)
            b = a * b_sh + b
            a = a * a_sh
        out = a * carry + b
        out_ref[pl.ds(off, SUBLANE), :] = out
        return out[0:1, :] if reverse else out[SUBLANE - 1:SUBLANE, :]

    return lax.fori_loop(0, n_tiles, step, carry0)


def _mm(a, b, mode, out_dtype, res=None, name=None):
    if mode == "nn":
        (m, k), (k2, n) = a.shape, b.shape
        dims = (((1,), (0,)), ((), ()))
    elif mode == "nt":
        (m, k), (n, k2) = a.shape, b.shape
        dims = (((1,), (1,)), ((), ()))
    else:
        (k, m), (k2, n) = a.shape, b.shape
        dims = (((0,), (0,)), ((), ()))
    assert k == k2 and a.dtype == BF16 and b.dtype == BF16
    bm = _pick(m, 1024)
    bn = _pick(n, 1024)
    bk = _pick(k, 2048)
    nk = k // bk
    out_bytes = jnp.dtype(out_dtype).itemsize

    def est(bk_):
        e = 2 * (bm * bk_ + bk_ * bn) * 2 + 2 * bm * bn * out_bytes
        if nk > 1 or k // bk_ > 1:
            e += bm * bn * 4
        if res is not None:
            e += 2 * bm * bn * 4
        return e

    while est(bk) > 40 * 1024 * 1024 and bk % (2 * LANE) == 0 and k % (bk // 2) == 0:
        bk //= 2
        nk = k // bk

    if mode == "tn":
        a_spec = pl.BlockSpec((bk, bm), lambda i, j, kk: (kk, i))
    else:
        a_spec = pl.BlockSpec((bm, bk), lambda i, j, kk: (i, kk))
    if mode == "nt":
        b_spec = pl.BlockSpec((bn, bk), lambda i, j, kk: (j, kk))
    else:
        b_spec = pl.BlockSpec((bk, bn), lambda i, j, kk: (kk, j))
    o_spec = pl.BlockSpec((bm, bn), lambda i, j, kk: (i, j))
    in_specs = [a_spec, b_spec]
    operands = [a, b]
    if res is not None:
        in_specs.append(o_spec)
        operands.append(res)
    has_res = res is not None

    def body(*refs):
        a_ref, b_ref = refs[0], refs[1]
        res_ref = refs[2] if has_res else None
        o_ref = refs[3] if has_res else refs[2]
        part = lax.dot_general(a_ref[...], b_ref[...], dims, preferred_element_type=F32)
        if nk == 1:
            if has_res:
                part = part + res_ref[...]
            o_ref[...] = part.astype(o_ref.dtype)
            return
        acc_ref = refs[-1]
        kk = pl.program_id(2)

        @pl.when(kk == 0)
        def _():
            acc_ref[...] = part

        @pl.when(kk > 0)
        def _():
            acc_ref[...] += part

        @pl.when(kk == nk - 1)
        def _():
            total = acc_ref[...]
            if has_res:
                total = total + res_ref[...]
            o_ref[...] = total.astype(o_ref.dtype)

    scratch = [pltpu.VMEM((bm, bn), F32)] if nk > 1 else []
    return pl.pallas_call(
        body,
        out_shape=jax.ShapeDtypeStruct((m, n), out_dtype),
        grid=(m // bm, n // bn, nk),
        in_specs=in_specs,
        out_specs=o_spec,
        scratch_shapes=scratch,
        compiler_params=_cparams(("parallel", "parallel", "arbitrary"), est(bk)),
        name=name,
    )(*operands)


def _rms_fwd(x, g, name):
    t, d = x.shape
    tb = _pick(t, 512, SUBLANE)

    def body(x_ref, g_ref, h_ref):
        xv = x_ref[...]
        r = lax.rsqrt(jnp.mean(xv * xv, axis=-1, keepdims=True) + EPS)
        h_ref[...] = ((xv * r) * g_ref[...]).astype(BF16)

    blk = pl.BlockSpec((tb, d), lambda i: (i, 0))
    return pl.pallas_call(
        body,
        out_shape=jax.ShapeDtypeStruct((t, d), BF16),
        grid=(t // tb,),
        in_specs=[blk, pl.BlockSpec((1, d), lambda i: (0, 0))],
        out_specs=blk,
        compiler_params=_cparams(("parallel",), 2 * tb * d * 6),
        name=name,
    )(x, g.reshape(1, d))


def _rms_bwd(x, g, dh, dres, name, want_bf16):
    t, d = x.shape
    tb = _pick(t, 256, SUBLANE)

    def body(x_ref, g_ref, dh_ref, dres_ref, *outs):
        dx_ref, dg_ref = outs[0], outs[-1]
        xv = x_ref[...]
        r = lax.rsqrt(jnp.mean(xv * xv, axis=-1, keepdims=True) + EPS)
        xhat = xv * r
        dhv = dh_ref[...]
        dxhat = dhv * g_ref[...]
        dx = dres_ref[...] + r * (dxhat - xhat * jnp.mean(dxhat * xhat, axis=-1, keepdims=True))
        dx_ref[...] = dx
        if want_bf16:
            outs[1][...] = dx.astype(BF16)

        @pl.when(pl.program_id(0) == 0)
        def _():
            dg_ref[...] = jnp.zeros_like(dg_ref)

        dg_ref[...] += jnp.sum(dhv * xhat, axis=0, keepdims=True)

    blk = pl.BlockSpec((tb, d), lambda i: (i, 0))
    row = pl.BlockSpec((1, d), lambda i: (0, 0))
    out_shape = [jax.ShapeDtypeStruct((t, d), F32)]
    out_specs = [blk]
    if want_bf16:
        out_shape.append(jax.ShapeDtypeStruct((t, d), BF16))
        out_specs.append(blk)
    out_shape.append(jax.ShapeDtypeStruct((1, d), F32))
    out_specs.append(row)
    return pl.pallas_call(
        body,
        out_shape=out_shape,
        grid=(t // tb,),
        in_specs=[blk, row, blk, blk],
        out_specs=out_specs,
        compiler_params=_cparams(("arbitrary",), 2 * tb * d * 18),
        name=name,
    )(x, g.reshape(1, d), dh, dres)


def _loss_head(x3, g, target):
    t, d = x3.shape
    tb = _pick(t, 256, SUBLANE)

    def body(x_ref, g_ref, t_ref, loss_ref, dx_ref, dxb_ref, dg_ref):
        xv = x_ref[...]
        gv = g_ref[...]
        r = lax.rsqrt(jnp.mean(xv * xv, axis=-1, keepdims=True) + EPS)
        xhat = xv * r
        err = xhat * gv - t_ref[...]
        dy = err * (1.0 / d)
        dxhat = dy * gv
        dx = r * (dxhat - xhat * jnp.mean(dxhat * xhat, axis=-1, keepdims=True))
        dx_ref[...] = dx
        dxb_ref[...] = dx.astype(BF16)

        @pl.when(pl.program_id(0) == 0)
        def _():
            dg_ref[...] = jnp.zeros_like(dg_ref)
            loss_ref[...] = jnp.zeros_like(loss_ref)

        dg_ref[...] += jnp.sum(dy * xhat, axis=0, keepdims=True)
        per_token = jnp.mean(err * err, axis=-1, keepdims=True)
        loss_ref[...] += 0.5 * jnp.sum(per_token, axis=0, keepdims=True)

    blk = pl.BlockSpec((tb, d), lambda i: (i, 0))
    row = pl.BlockSpec((1, d), lambda i: (0, 0))
    return pl.pallas_call(
        body,
        out_shape=[jax.ShapeDtypeStruct((1, 1), F32), jax.ShapeDtypeStruct((t, d), F32),
                   jax.ShapeDtypeStruct((t, d), BF16), jax.ShapeDtypeStruct((1, d), F32)],
        grid=(t // tb,),
        in_specs=[blk, row, blk],
        out_specs=[pl.BlockSpec((1, 1), lambda i: (0, 0)), blk, blk, row],
        compiler_params=_cparams(("arbitrary",), 2 * tb * d * 14),
        name="loss_head",
    )(x3, g.reshape(1, d), target)


def _lru_gates(xc, wa, ba, wx, bx, lam):
    nn = (((1,), (0,)), ((), ()))
    xcb = xc.astype(BF16)
    r = _sigmoid(lax.dot_general(xcb, wa, nn, preferred_element_type=F32) + ba)
    i = _sigmoid(lax.dot_general(xcb, wx, nn, preferred_element_type=F32) + bx)
    cl = -LRU_C * _softplus(-lam)
    log_a = cl * r
    a = jnp.exp(log_a)
    one_minus_a2 = _neg_expm1(2.0 * log_a)
    return xcb, r, i, a, one_minus_a2, cl


def _lru_fwd(p, conv_w, conv_b, wa_bd, ba, wx_bd, bx, lam, d_lru, gc, tc):
    t = p.shape[0]
    ng = d_lru // gc
    nt = t // tc
    width = conv_w.shape[0]

    def body(lx_ref, gate_ref, cw_ref, cb_ref, wa_ref, ba_ref, wx_ref, bx_ref, lam_ref,
             y_ref, h_ref, halo, hcar, a_s, u_s):
        @pl.when(pl.program_id(1) == 0)
        def _():
            halo[...] = jnp.zeros_like(halo)
            hcar[...] = jnp.zeros_like(hcar)

        x = lx_ref[...]
        cat = jnp.concatenate([halo[...], x], axis=0)
        halo[...] = x[tc - SUBLANE:, :]
        xc = _conv_fwd(cat, cw_ref[...], width) + cb_ref[...]
        _, r, i, a, om, _ = _lru_gates(xc, wa_ref[...], ba_ref[...], wx_ref[...], bx_ref[...], lam_ref[...])
        a_s[...] = a
        u_s[...] = jnp.sqrt(om) * (i * xc)
        hcar[0:1, :] = _scan_tiles(a_s, u_s, h_ref, hcar[0:1, :], tc, reverse=False)
        gl, _ = _gelu_and_grad(gate_ref[...])
        y_ref[...] = (gl * h_ref[...]).astype(BF16)

    blk = lambda off: pl.BlockSpec((tc, gc), lambda g, s, off=off: (s, off + g))
    rowv = lambda rows: pl.BlockSpec((rows, gc), lambda g, s: (0, g))
    wspec = pl.BlockSpec((None, gc, gc), lambda g, s: (g, 0, 0))
    out_blk = pl.BlockSpec((tc, gc), lambda g, s: (s, g))
    return pl.pallas_call(
        body,
        out_shape=[jax.ShapeDtypeStruct((t, d_lru), BF16), jax.ShapeDtypeStruct((t, d_lru), F32)],
        grid=(ng, nt),
        in_specs=[blk(0), blk(ng), rowv(width), rowv(1), wspec, rowv(1), wspec, rowv(1), rowv(1)],
        out_specs=[out_blk, out_blk],
        scratch_shapes=[pltpu.VMEM((SUBLANE, gc), F32), pltpu.VMEM((SUBLANE, gc), F32),
                        pltpu.VMEM((tc, gc), F32), pltpu.VMEM((tc, gc), F32)],
        compiler_params=_cparams(("parallel", "arbitrary"), 40 * tc * gc * 4),
        name="lru_fwd",
    )(p, p, conv_w, conv_b.reshape(1, -1), wa_bd, ba.reshape(1, -1), wx_bd, bx.reshape(1, -1),
      lam.reshape(1, -1))


def _lru_bwd(p, hseq, dyp, conv_w, conv_b, wa_bd, ba, wx_bd, bx, lam, d_lru, gc, tc):
    t = p.shape[0]
    ng = d_lru // gc
    nt = t // tc
    width = conv_w.shape[0]
    halo_blocks = tc // SUBLANE
    nn = (((1,), (0,)), ((), ()))
    nt_dims = (((1,), (1,)), ((), ()))
    tn_dims = (((0,), (0,)), ((), ()))

    def body(lx_ref, lxh_ref, gate_ref, h_ref, hh_ref, dyp_ref,
             cw_ref, cb_ref, wa_ref, ba_ref, wx_ref, bx_ref, lam_ref,
             dlx_ref, dgate_ref, dcw_ref, dcb_ref, dwa_ref, dba_ref, dwx_ref, dbx_ref, dlam_ref,
             nxt_dxc, nxt_a, nxt_g, al_s, b_s, g_s):
        s = pl.program_id(1)
        first_chunk = s == nt - 1

        @pl.when(s == 0)
        def _():
            nxt_dxc[...] = jnp.zeros_like(nxt_dxc)
            nxt_a[...] = jnp.zeros_like(nxt_a)
            nxt_g[...] = jnp.zeros_like(nxt_g)
            for ref in (dcw_ref, dcb_ref, dwa_ref, dba_ref, dwx_ref, dbx_ref, dlam_ref):
                ref[...] = jnp.zeros_like(ref)

        keep = jnp.where(first_chunk, 0.0, 1.0)
        x = lx_ref[...]
        catx = jnp.concatenate([lxh_ref[...] * keep, x], axis=0)
        cw = cw_ref[...]
        xc = _conv_fwd(catx, cw, width) + cb_ref[...]
        wa = wa_ref[...]
        wx = wx_ref[...]
        lam_v = lam_ref[...]
        xcb, r, i, a, om, cl = _lru_gates(xc, wa, ba_ref[...], wx, bx_ref[...], lam_v)
        mult = jnp.sqrt(om)

        h = h_ref[...]
        hprev = _rows_before(jnp.concatenate([hh_ref[...] * keep, h], axis=0), 1)
        gl, dgl = _gelu_and_grad(gate_ref[...])
        dyp_v = dyp_ref[...]
        dgate_ref[...] = (dyp_v * h * dgl).astype(BF16)

        al_s[...] = _rows_after(jnp.concatenate([a, nxt_a[...]], axis=0), 1)
        b_s[...] = dyp_v * gl
        nxt_g[0:1, :] = _scan_tiles(al_s, b_s, g_s, nxt_g[0:1, :], tc, reverse=True)
        nxt_a[...] = a[0:SUBLANE, :]
        du = g_s[...]

        da = du * hprev
        dmult = du * (i * xc)
        di = du * mult * xc
        dxc = du * mult * i
        dlog_a = da * a - dmult * (a * a / mult)
        dlam_ref[...] += jnp.sum(dlog_a * r, axis=0, keepdims=True) * (LRU_C * _sigmoid(-lam_v))
        dza = (dlog_a * cl) * r * (1.0 - r)
        dzx = di * i * (1.0 - i)
        dba_ref[...] += jnp.sum(dza, axis=0, keepdims=True)
        dbx_ref[...] += jnp.sum(dzx, axis=0, keepdims=True)
        dzab = dza.astype(BF16)
        dzxb = dzx.astype(BF16)
        dwa_ref[...] += lax.dot_general(xcb, dzab, tn_dims, preferred_element_type=F32)
        dwx_ref[...] += lax.dot_general(xcb, dzxb, tn_dims, preferred_element_type=F32)
        dxc = dxc + lax.dot_general(dzab, wa, nt_dims, preferred_element_type=F32)
        dxc = dxc + lax.dot_general(dzxb, wx, nt_dims, preferred_element_type=F32)
        dcb_ref[...] += jnp.sum(dxc, axis=0, keepdims=True)
        _conv_bwd_weight(dcw_ref, dxc, catx, width)
        catd = jnp.concatenate([dxc, nxt_dxc[...]], axis=0)
        dlx_ref[...] = _conv_bwd_input(catd, cw, width).astype(BF16)
        nxt_dxc[...] = dxc[0:SUBLANE, :]

    rev = lambda s: nt - 1 - s
    blk = lambda off: pl.BlockSpec((tc, gc), lambda g, s, off=off: (rev(s), off + g))
    halo = lambda off: pl.BlockSpec(
        (SUBLANE, gc), lambda g, s, off=off: (jnp.maximum(rev(s) * halo_blocks - 1, 0), off + g))
    rowv = lambda rows: pl.BlockSpec((rows, gc), lambda g, s: (0, g))
    wspec = pl.BlockSpec((None, gc, gc), lambda g, s: (g, 0, 0))
    out_blk = pl.BlockSpec((tc, gc), lambda g, s: (rev(s), g))
    vec = lambda rows: jax.ShapeDtypeStruct((rows, d_lru), F32)
    wshape = jax.ShapeDtypeStruct((ng, gc, gc), F32)
    return pl.pallas_call(
        body,
        out_shape=[jax.ShapeDtypeStruct((t, d_lru), BF16), jax.ShapeDtypeStruct((t, d_lru), BF16),
                   vec(width), vec(1), wshape, vec(1), wshape, vec(1), vec(1)],
        grid=(ng, nt),
        in_specs=[blk(0), halo(0), blk(ng), blk(0), halo(0), blk(0),
                  rowv(width), rowv(1), wspec, rowv(1), wspec, rowv(1), rowv(1)],
        out_specs=[out_blk, out_blk, rowv(width), rowv(1), wspec, rowv(1), wspec, rowv(1), rowv(1)],
        scratch_shapes=[pltpu.VMEM((SUBLANE, gc), F32), pltpu.VMEM((SUBLANE, gc), F32),
                        pltpu.VMEM((SUBLANE, gc), F32),
                        pltpu.VMEM((tc, gc), F32), pltpu.VMEM((tc, gc), F32), pltpu.VMEM((tc, gc), F32)],
        compiler_params=_cparams(("parallel", "arbitrary"), 80 * tc * gc * 4),
        name="lru_bwd",
    )(p, p, p, hseq, hseq, dyp, conv_w, conv_b.reshape(1, -1), wa_bd, ba.reshape(1, -1), wx_bd,
      bx.reshape(1, -1), lam.reshape(1, -1))


def _sc_fwd(p, conv_w, col0, d_sc, cb, tc):
    t = p.shape[0]
    nc = d_sc // cb
    nt = t // tc
    width = conv_w.shape[0]
    base = col0 // cb

    def body(b_ref, c_ref, v_ref, w_ref, y_ref, halo):
        @pl.when(pl.program_id(1) == 0)
        def _():
            halo[...] = jnp.zeros_like(halo)

        cv = c_ref[...] * v_ref[...]
        cat = jnp.concatenate([halo[...], cv], axis=0)
        halo[...] = cv[tc - SUBLANE:, :]
        y_ref[...] = (b_ref[...] * _conv_fwd(cat, w_ref[...], width)).astype(BF16)

    blk = lambda slab: pl.BlockSpec((tc, cb), lambda j, s, slab=slab: (s, base + slab * nc + j))
    return pl.pallas_call(
        body,
        out_shape=jax.ShapeDtypeStruct((t, d_sc), BF16),
        grid=(nc, nt),
        in_specs=[blk(0), blk(1), blk(2), pl.BlockSpec((width, cb), lambda j, s: (0, j))],
        out_specs=pl.BlockSpec((tc, cb), lambda j, s: (s, j)),
        scratch_shapes=[pltpu.VMEM((SUBLANE, cb), F32)],
        compiler_params=_cparams(("parallel", "arbitrary"), 20 * tc * cb * 4),
        name="sc_fwd",
    )(p, p, p, conv_w)


def _sc_bwd(p, dyp, conv_w, col0, d_sc, cb, tc):
    t = p.shape[0]
    nc = d_sc // cb
    nt = t // tc
    width = conv_w.shape[0]
    base = col0 // cb
    halo_blocks = tc // SUBLANE

    def body(b_ref, c_ref, ch_ref, v_ref, vh_ref, dyp_ref, w_ref,
             db_ref, dc_ref, dv_ref, dw_ref, nxt_dq):
        s = pl.program_id(1)

        @pl.when(s == 0)
        def _():
            nxt_dq[...] = jnp.zeros_like(nxt_dq)
            dw_ref[...] = jnp.zeros_like(dw_ref)

        keep = jnp.where(s == nt - 1, 0.0, 1.0)
        cvals = c_ref[...]
        vvals = v_ref[...]
        w = w_ref[...]
        catcv = jnp.concatenate([ch_ref[...] * vh_ref[...] * keep, cvals * vvals], axis=0)
        q = _conv_fwd(catcv, w, width)
        dyp_v = dyp_ref[...]
        db_ref[...] = (dyp_v * q).astype(BF16)
        dq = dyp_v * b_ref[...]
        _conv_bwd_weight(dw_ref, dq, catcv, width)
        dcv = _conv_bwd_input(jnp.concatenate([dq, nxt_dq[...]], axis=0), w, width)
        nxt_dq[...] = dq[0:SUBLANE, :]
        dc_ref[...] = (dcv * vvals).astype(BF16)
        dv_ref[...] = (dcv * cvals).astype(BF16)

    rev = lambda s: nt - 1 - s
    blk = lambda slab: pl.BlockSpec((tc, cb), lambda j, s, slab=slab: (rev(s), base + slab * nc + j))
    halo = lambda slab: pl.BlockSpec(
        (SUBLANE, cb),
        lambda j, s, slab=slab: (jnp.maximum(rev(s) * halo_blocks - 1, 0), base + slab * nc + j))
    out_blk = pl.BlockSpec((tc, cb), lambda j, s: (rev(s), j))
    wblk = pl.BlockSpec((width, cb), lambda j, s: (0, j))
    act = jax.ShapeDtypeStruct((t, d_sc), BF16)
    return pl.pallas_call(
        body,
        out_shape=[act, act, act, jax.ShapeDtypeStruct((width, d_sc), F32)],
        grid=(nc, nt),
        in_specs=[blk(0), blk(1), halo(1), blk(2), halo(2), out_blk, wblk],
        out_specs=[out_blk, out_blk, out_blk, wblk],
        scratch_shapes=[pltpu.VMEM((SUBLANE, cb), F32)],
        compiler_params=_cparams(("parallel", "arbitrary"), 30 * tc * cb * 4),
        name="sc_bwd",
    )(p, p, p, p, p, dyp, conv_w)


def _merge_fwd(p, y_lru, y_sc, col0, tc):
    t, d = y_lru.shape
    cb = _pick(math.gcd(d, col0), 1024)
    nc = d // cb
    base = col0 // cb

    def body(gl_ref, gs_ref, yl_ref, ys_ref, o_ref):
        o_ref[...] = (_sigmoid(gl_ref[...]) * yl_ref[...] + _sigmoid(gs_ref[...]) * ys_ref[...]).astype(BF16)

    gate = lambda slab: pl.BlockSpec((tc, cb), lambda s, j, slab=slab: (s, base + slab * nc + j))
    blk = pl.BlockSpec((tc, cb), lambda s, j: (s, j))
    return pl.pallas_call(
        body,
        out_shape=jax.ShapeDtypeStruct((t, d), BF16),
        grid=(t // tc, nc),
        in_specs=[gate(0), gate(1), blk, blk],
        out_specs=blk,
        compiler_params=_cparams(("parallel", "parallel"), 2 * tc * cb * 20),
        name="merge_fwd",
    )(p, p, y_lru, y_sc)


def _merge_bwd(p, y_lru, y_sc, dmerged, col0, tc):
    t, d = y_lru.shape
    cb = _pick(math.gcd(d, col0), 1024)
    nc = d // cb
    base = col0 // cb

    def body(gl_ref, gs_ref, yl_ref, ys_ref, dm_ref, dgl_ref, dgs_ref, dyl_ref, dys_ref):
        dm = dm_ref[...]
        sl = _sigmoid(gl_ref[...])
        ss = _sigmoid(gs_ref[...])
        dgl_ref[...] = (dm * yl_ref[...] * (sl * (1.0 - sl))).astype(BF16)
        dgs_ref[...] = (dm * ys_ref[...] * (ss * (1.0 - ss))).astype(BF16)
        dyl_ref[...] = (dm * sl).astype(BF16)
        dys_ref[...] = (dm * ss).astype(BF16)

    gate = lambda slab: pl.BlockSpec((tc, cb), lambda s, j, slab=slab: (s, base + slab * nc + j))
    blk = pl.BlockSpec((tc, cb), lambda s, j: (s, j))
    act = jax.ShapeDtypeStruct((t, d), BF16)
    return pl.pallas_call(
        body,
        out_shape=[act, act, act, act],
        grid=(t // tc, nc),
        in_specs=[gate(0), gate(1), blk, blk, blk],
        out_specs=[blk, blk, blk, blk],
        compiler_params=_cparams(("parallel", "parallel"), 2 * tc * cb * 28),
        name="merge_bwd",
    )(p, p, y_lru, y_sc, dmerged)


def _ffn_act_fwd(up, conv_w, d_ff, cb, tc):
    t = up.shape[0]
    nc = d_ff // cb
    nt = t // tc
    width = conv_w.shape[0]

    def body(g_ref, v_ref, wg_ref, wv_ref, o_ref, halo_g, halo_v):
        @pl.when(pl.program_id(1) == 0)
        def _():
            halo_g[...] = jnp.zeros_like(halo_g)
            halo_v[...] = jnp.zeros_like(halo_v)

        g = g_ref[...]
        v = v_ref[...]
        ug = _conv_fwd(jnp.concatenate([halo_g[...], g], axis=0), wg_ref[...], width)
        uv = _conv_fwd(jnp.concatenate([halo_v[...], v], axis=0), wv_ref[...], width)
        halo_g[...] = g[tc - SUBLANE:, :]
        halo_v[...] = v[tc - SUBLANE:, :]
        o_ref[...] = (ug * _sigmoid(ug) * uv).astype(BF16)

    blk = lambda half: pl.BlockSpec((tc, cb), lambda j, s, half=half: (s, half * nc + j))
    wblk = lambda half: pl.BlockSpec((width, cb), lambda j, s, half=half: (0, half * nc + j))
    return pl.pallas_call(
        body,
        out_shape=jax.ShapeDtypeStruct((t, d_ff), BF16),
        grid=(nc, nt),
        in_specs=[blk(0), blk(1), wblk(0), wblk(1)],
        out_specs=pl.BlockSpec((tc, cb), lambda j, s: (s, j)),
        scratch_shapes=[pltpu.VMEM((SUBLANE, cb), F32), pltpu.VMEM((SUBLANE, cb), F32)],
        compiler_params=_cparams(("parallel", "arbitrary"), 24 * tc * cb * 4),
        name="ffn_act_fwd",
    )(up, up, conv_w, conv_w)


def _ffn_act_bwd(up, dact, conv_w, d_ff, cb, tc):
    t = up.shape[0]
    nc = d_ff // cb
    nt = t // tc
    width = conv_w.shape[0]
    halo_blocks = tc // SUBLANE

    def body(g_ref, gh_ref, v_ref, vh_ref, da_ref, wg_ref, wv_ref,
             dg_ref, dv_ref, dwg_ref, dwv_ref, nxt_g, nxt_v):
        s = pl.program_id(1)

        @pl.when(s == 0)
        def _():
            nxt_g[...] = jnp.zeros_like(nxt_g)
            nxt_v[...] = jnp.zeros_like(nxt_v)
            dwg_ref[...] = jnp.zeros_like(dwg_ref)
            dwv_ref[...] = jnp.zeros_like(dwv_ref)

        keep = jnp.where(s == nt - 1, 0.0, 1.0)
        wg = wg_ref[...]
        wv = wv_ref[...]
        catg = jnp.concatenate([gh_ref[...] * keep, g_ref[...]], axis=0)
        catv = jnp.concatenate([vh_ref[...] * keep, v_ref[...]], axis=0)
        ug = _conv_fwd(catg, wg, width)
        uv = _conv_fwd(catv, wv, width)
        sg = _sigmoid(ug)
        da = da_ref[...]
        dug = da * uv * (sg * (1.0 + ug * (1.0 - sg)))
        duv = da * (ug * sg)
        _conv_bwd_weight(dwg_ref, dug, catg, width)
        _conv_bwd_weight(dwv_ref, duv, catv, width)
        dg_ref[...] = _conv_bwd_input(jnp.concatenate([dug, nxt_g[...]], axis=0), wg, width).astype(BF16)
        dv_ref[...] = _conv_bwd_input(jnp.concatenate([duv, nxt_v[...]], axis=0), wv, width).astype(BF16)
        nxt_g[...] = dug[0:SUBLANE, :]
        nxt_v[...] = duv[0:SUBLANE, :]

    rev = lambda s: nt - 1 - s
    blk = lambda half: pl.BlockSpec((tc, cb), lambda j, s, half=half: (rev(s), half * nc + j))
    halo = lambda half: pl.BlockSpec(
        (SUBLANE, cb), lambda j, s, half=half: (jnp.maximum(rev(s) * halo_blocks - 1, 0), half * nc + j))
    wblk = lambda half: pl.BlockSpec((width, cb), lambda j, s, half=half: (0, half * nc + j))
    out_blk = pl.BlockSpec((tc, cb), lambda j, s: (rev(s), j))
    wout = pl.BlockSpec((width, cb), lambda j, s: (0, j))
    act = jax.ShapeDtypeStruct((t, d_ff), BF16)
    wshape = jax.ShapeDtypeStruct((width, d_ff), F32)
    return pl.pallas_call(
        body,
        out_shape=[act, act, wshape, wshape],
        grid=(nc, nt),
        in_specs=[blk(0), halo(0), blk(1), halo(1), out_blk, wblk(0), wblk(1)],
        out_specs=[out_blk, out_blk, wout, wout],
        scratch_shapes=[pltpu.VMEM((SUBLANE, cb), F32), pltpu.VMEM((SUBLANE, cb), F32)],
        compiler_params=_cparams(("parallel", "arbitrary"), 40 * tc * cb * 4),
        name="ffn_act_bwd",
    )(up, up, up, up, dact, conv_w, conv_w)


def _mesh_pos():
    x, y, c = lax.axis_index("x"), lax.axis_index("y"), lax.axis_index("c")
    return x, y, c


def _other_chips(x, y):
    return [(1 - x, y), (x, 1 - y), (1 - x, 1 - y)]


def _cast_place(w, chip, col_sharded, name):
    r, cdim = w.shape
    full = (r, cdim * N_CHIPS) if col_sharded else (r * N_CHIPS, cdim)
    rb = _pick(r, max(BF16_ROWS, (512 * 1024) // cdim), BF16_ROWS)
    nb = r // rb

    def body(chip_ref, w_ref, o_ref):
        o_ref[...] = w_ref[...].astype(BF16)

    if col_sharded:
        out_map = lambda i, chip_ref: (i, chip_ref[0])
    else:
        out_map = lambda i, chip_ref: (chip_ref[0] * nb + i, 0)
    grid_spec = pltpu.PrefetchScalarGridSpec(
        num_scalar_prefetch=1,
        grid=(nb,),
        in_specs=[pl.BlockSpec((rb, cdim), lambda i, chip_ref: (i, 0))],
        out_specs=pl.BlockSpec((rb, cdim), out_map),
    )
    return pl.pallas_call(
        body,
        out_shape=jax.ShapeDtypeStruct(full, BF16),
        grid_spec=grid_spec,
        compiler_params=_cparams(("parallel",), 2 * rb * cdim * 6),
        name=name,
    )(chip, w)


def _remote(src, dst, send_sems, recv_sems, idx, to):
    return pltpu.make_async_remote_copy(
        src_ref=src, dst_ref=dst, send_sem=send_sems.at[idx], recv_sem=recv_sems.at[idx],
        device_id=to, device_id_type=MESH)


def _exchange(name, arrays, n_sems, plan):
    n = len(arrays)

    def body(*refs):
        bufs = refs[n:2 * n]
        send_sems, recv_sems = refs[2 * n:]
        sends, arrivals = plan(bufs, send_sems, recv_sems)
        for cp in sends:
            cp.start()
        for cp in arrivals:
            cp.wait_recv()
        for cp in sends:
            cp.wait_send()

    outs = pl.pallas_call(
        body,
        out_shape=[jax.ShapeDtypeStruct(a.shape, a.dtype) for a in arrays],
        in_specs=[ANY] * n,
        out_specs=[ANY] * n,
        input_output_aliases={k: k for k in range(n)},
        scratch_shapes=[pltpu.SemaphoreType.DMA((n_sems,)), pltpu.SemaphoreType.DMA((n_sems,))],
        name=name,
    )(*arrays)
    return list(outs)


def _exchange_start(name, arrays, n_sems, plan):
    n = len(arrays)

    def body(*refs):
        bufs = refs[:n]
        send_sems, recv_sems = refs[n], refs[n + 1]
        token = refs[-1]
        sends, _ = plan(bufs, send_sems, recv_sems)
        for cp in sends:
            cp.start()
        token[...] = jnp.zeros_like(token)

    out = pl.pallas_call(
        body,
        out_shape=(pltpu.SemaphoreType.DMA((n_sems,)), pltpu.SemaphoreType.DMA((n_sems,)),
                   *[pltpu.HBM(a.shape, a.dtype) for a in arrays],
                   jax.ShapeDtypeStruct((SUBLANE, LANE), F32)),
        in_specs=[HBM_SPEC] * n,
        out_specs=(SEM_SPEC, SEM_SPEC, *[HBM_SPEC] * n, VMEM_SPEC),
        input_output_aliases={k: 2 + k for k in range(n)},
        compiler_params=pltpu.CompilerParams(has_side_effects=DATAFLOW_EFFECT),
        name=name,
    )(*[pltpu.with_memory_space_constraint(a, pltpu.HBM) for a in arrays])
    return out[0], out[1], list(out[2:2 + n]), out[-1]


def _exchange_wait(name, arrays, send_sems, recv_sems, after, plan):
    n = len(arrays)

    def body(*refs):
        bufs = refs[:n]
        sends, arrivals = plan(bufs, refs[n], refs[n + 1])
        for cp in arrivals:
            cp.wait_recv()
        for cp in sends:
            cp.wait_send()

    outs = pl.pallas_call(
        body,
        out_shape=[pltpu.HBM(a.shape, a.dtype) for a in arrays],
        in_specs=[HBM_SPEC] * n + [SEM_SPEC, SEM_SPEC, ANY],
        out_specs=[HBM_SPEC] * n,
        input_output_aliases={k: k for k in range(n)},
        compiler_params=pltpu.CompilerParams(has_side_effects=DATAFLOW_EFFECT),
        name=name,
    )(*arrays, send_sems, recv_sems, after)
    return list(outs)


def _tie(value, token):
    return lax.optimization_barrier((value, token))[0]


def _half_block(ref, shard_shape, col_sharded, chip, half):
    r, cdim = shard_shape
    h = r // 2
    if col_sharded:
        return ref.at[pl.ds(pl.multiple_of(half * h, BF16_ROWS), h),
                      pl.ds(pl.multiple_of(chip * cdim, LANE), cdim)]
    return ref.at[pl.ds(pl.multiple_of(chip * r + half * h, BF16_ROWS), h), :]


def _gather_plan(shard_shapes, col_sharded, ks):
    def plan(bufs, send_sems, recv_sems):
        x, y, c = _mesh_pos()
        sends, arrivals = [], []
        for ref, k in zip(bufs, ks):
            mine = _half_block(ref, shard_shapes[k], col_sharded[k], 2 * x + y, c)
            for j, (px, py) in enumerate(_other_chips(x, y)):
                landed = _half_block(ref, shard_shapes[k], col_sharded[k], 2 * px + py, c)
                sends.append(_remote(mine, mine, send_sems, recv_sems, 3 * k + j, (px, py, c)))
                arrivals.append(_remote(landed, landed, send_sems, recv_sems, 3 * k + j, (px, py, c)))
        return sends, arrivals
    return plan


def _forward_plan(shard_shapes, col_sharded, ks):
    def plan(bufs, send_sems, recv_sems):
        x, y, c = _mesh_pos()
        sends, arrivals = [], []
        for i, (ref, k) in enumerate(zip(bufs, ks)):
            for j, (px, py) in enumerate(_other_chips(x, y)):
                landed = _half_block(ref, shard_shapes[k], col_sharded[k], 2 * px + py, c)
                theirs = _half_block(ref, shard_shapes[k], col_sharded[k], 2 * px + py, 1 - c)
                sends.append(_remote(landed, landed, send_sems, recv_sems, 3 * i + j, (x, y, 1 - c)))
                arrivals.append(_remote(theirs, theirs, send_sems, recv_sems, 3 * i + j, (x, y, 1 - c)))
        return sends, arrivals
    return plan


def _small_gather(small):
    def body(small_ref, out_ref, send_sems, recv_sems):
        x, y, c = _mesh_pos()
        me = 2 * x + y
        out_ref[me] = small_ref[...]
        copies = []
        for j, (px, py) in enumerate(_other_chips(x, y)):
            cp = _remote(small_ref, out_ref.at[me], send_sems, recv_sems, j, (px, py, c))
            cp.start()
            copies.append(cp)
        for j, (px, py) in enumerate(_other_chips(x, y)):
            _remote(small_ref, out_ref.at[2 * px + py], send_sems, recv_sems, j, (px, py, c)).wait_recv()
        for cp in copies:
            cp.wait_send()

    return pl.pallas_call(
        body,
        out_shape=jax.ShapeDtypeStruct((N_CHIPS,) + small.shape, small.dtype),
        in_specs=[VMEM_SPEC],
        out_specs=VMEM_SPEC,
        scratch_shapes=[pltpu.SemaphoreType.DMA((N_CHIPS - 1,)), pltpu.SemaphoreType.DMA((N_CHIPS - 1,))],
        name="gather_small",
    )(small)


def _as3d(g, col_sharded):
    r, cdim = g.shape
    return g.reshape(1, r, cdim) if col_sharded else g.reshape(N_CHIPS, r // N_CHIPS, cdim)


def _pair_plan(m):
    def plan(bufs, send_sems, recv_sems):
        x, y, c = _mesh_pos()
        copies = []
        for i in range(m):
            h = bufs[i].shape[1] // 2
            src = bufs[i].at[:, pl.ds(pl.multiple_of((1 - c) * h, SUBLANE), h), :]
            copies.append(_remote(src, bufs[m + i], send_sems, recv_sems, i, (x, y, 1 - c)))
        return copies, copies
    return plan


def _chip_plan(col_flags):
    m = len(col_flags)

    def plan(bufs, send_sems, recv_sems):
        x, y, c = _mesh_pos()
        copies = []
        for i in range(m):
            land = bufs[m + i]
            width = land.shape[2]
            for j, (px, py) in enumerate(_other_chips(x, y)):
                q = 2 * px + py
                if col_flags[i]:
                    src = bufs[i].at[0, :, pl.ds(pl.multiple_of(q * width, LANE), width)]
                else:
                    src = bufs[i].at[q]
                copies.append(_remote(src, land.at[j], send_sems, recv_sems, 3 * i + j, (px, py, c)))
        return copies, copies
    return plan


def _share_plan(m):
    def plan(bufs, send_sems, recv_sems):
        x, y, c = _mesh_pos()
        sends, arrivals = [], []
        for i in range(m):
            h = bufs[i].shape[0] // 2
            mine = bufs[i].at[pl.ds(pl.multiple_of(c * h, SUBLANE), h), :]
            theirs = bufs[i].at[pl.ds(pl.multiple_of((1 - c) * h, SUBLANE), h), :]
            sends.append(_remote(mine, mine, send_sems, recv_sems, i, (x, y, 1 - c)))
            arrivals.append(_remote(theirs, theirs, send_sems, recv_sems, i, (x, y, 1 - c)))
        return sends, arrivals
    return plan


def _pair_add(g3, other, core):
    a, r, cdim = g3.shape
    h = r // 2
    rb = _pick(h, max(BF16_ROWS, (512 * 1024) // cdim), BF16_ROWS)
    nb = h // rb

    def body(core_ref, g_ref, o_ref, out_ref):
        out_ref[...] = (g_ref[...] + o_ref[...]).astype(BF16)

    grid_spec = pltpu.PrefetchScalarGridSpec(
        num_scalar_prefetch=1,
        grid=(a, nb),
        in_specs=[pl.BlockSpec((None, rb, cdim), lambda i, j, core_ref: (i, core_ref[0] * nb + j, 0)),
                  pl.BlockSpec((None, rb, cdim), lambda i, j, core_ref: (i, j, 0))],
        out_specs=pl.BlockSpec((None, rb, cdim), lambda i, j, core_ref: (i, j, 0)),
    )
    return pl.pallas_call(
        body,
        out_shape=jax.ShapeDtypeStruct((a, h, cdim), BF16),
        grid_spec=grid_spec,
        compiler_params=_cparams(("parallel", "parallel"), 2 * rb * cdim * 10),
        name="grad_pair_add",
    )(core, g3, other)


def _small_allreduce(small):
    def body(small_ref, out_ref, slots, send_sems, recv_sems):
        x, y, c = _mesh_pos()
        my_dev = 4 * x + 2 * y + c
        slots[my_dev] = small_ref[...]
        copies = []
        for m in range(1, N_DEV):
            peer = (x ^ ((m >> 2) & 1), y ^ ((m >> 1) & 1), c ^ (m & 1))
            cp = _remote(small_ref, slots.at[my_dev], send_sems, recv_sems, m - 1, peer)
            cp.start()
            copies.append(cp)
        for m in range(1, N_DEV):
            _remote(small_ref, slots.at[my_dev ^ m], send_sems, recv_sems, m - 1, (x, y, c)).wait_recv()
        total = slots[0]
        for d in range(1, N_DEV):
            total = total + slots[d]
        out_ref[...] = total
        for cp in copies:
            cp.wait_send()

    return pl.pallas_call(
        body,
        out_shape=jax.ShapeDtypeStruct(small.shape, F32),
        in_specs=[VMEM_SPEC],
        out_specs=VMEM_SPEC,
        scratch_shapes=[pltpu.VMEM((N_DEV,) + small.shape, F32),
                        pltpu.SemaphoreType.DMA((N_DEV - 1,)), pltpu.SemaphoreType.DMA((N_DEV - 1,))],
        compiler_params=pltpu.CompilerParams(
            vmem_limit_bytes=min(VMEM_BUDGET, (N_DEV + 4) * _nbytes(small.shape, F32) + (8 << 20))),
        name="grad_small_allreduce",
    )(small)


def _chip_sum(partial, land, where, col_sharded):
    _, h, cdim = land.shape
    rb = _pick(h, max(BF16_ROWS, (512 * 1024) // cdim), BF16_ROWS)
    nb = h // rb

    def body(where_ref, own_ref, l_ref, o_ref):
        total = own_ref[...].astype(F32)
        for j in range(N_CHIPS - 1):
            total = total + l_ref[j].astype(F32)
        o_ref[...] = total

    if col_sharded:
        own_map = lambda i, w: (0, i, w[0])
    else:
        own_map = lambda i, w: (w[0], i, 0)
    grid_spec = pltpu.PrefetchScalarGridSpec(
        num_scalar_prefetch=1,
        grid=(nb,),
        in_specs=[pl.BlockSpec((None, rb, cdim), own_map),
                  pl.BlockSpec((N_CHIPS - 1, rb, cdim), lambda i, w: (0, i, 0))],
        out_specs=pl.BlockSpec((rb, cdim), lambda i, w: (w[1] * nb + i, 0)),
    )
    return pl.pallas_call(
        body,
        out_shape=jax.ShapeDtypeStruct((2 * h, cdim), F32),
        grid_spec=grid_spec,
        compiler_params=_cparams(("parallel",), 2 * rb * cdim * 12),
        name="grad_chip_sum",
    )(where, partial, land)


def _adamw(w, g, m, v, name):
    r, cdim = w.shape
    rb = _pick(r, max(SUBLANE, (256 * 1024) // cdim), SUBLANE)
    c1 = 1.0 - ADAM_B1 ** ADAM_STEP
    c2 = 1.0 - ADAM_B2 ** ADAM_STEP

    def body(w_ref, g_ref, m_ref, v_ref, go_ref, d_ref, mo_ref, vo_ref):
        gv = g_ref[...]
        mn = ADAM_B1 * m_ref[...] + (1.0 - ADAM_B1) * gv
        vn = ADAM_B2 * v_ref[...] + (1.0 - ADAM_B2) * (gv * gv)
        m_hat = mn / c1
        v_hat = vn / c2
        d_ref[...] = -ADAM_LR * (m_hat / (jnp.sqrt(v_hat) + ADAM_EPS) + ADAM_WD * w_ref[...])
        go_ref[...] = gv
        mo_ref[...] = mn
        vo_ref[...] = vn

    blk = pl.BlockSpec((rb, cdim), lambda i: (i, 0))
    shape = jax.ShapeDtypeStruct((r, cdim), F32)
    return pl.pallas_call(
        body,
        out_shape=[shape] * 4,
        grid=(r // rb,),
        in_specs=[blk] * 4,
        out_specs=[blk] * 4,
        compiler_params=_cparams(("parallel",), 2 * rb * cdim * 4 * 8),
        name=name,
    )(w, g, m, v)


def _pack(arrays):
    tile = SUBLANE * LANE
    pieces = []
    for arr in arrays:
        flat = arr.reshape(-1)
        pad = (-flat.shape[0]) % tile
        if pad:
            flat = jnp.concatenate([flat, jnp.zeros((pad,), flat.dtype)])
        pieces.append(flat)
    return jnp.concatenate(pieces).reshape(-1, LANE)


def _unpack(packed, shapes):
    tile = SUBLANE * LANE
    flat = packed.reshape(-1)
    out, off = [], 0
    for shp in shapes:
        size = math.prod(shp)
        out.append(flat[off:off + size].reshape(shp))
        off += size + ((-size) % tile)
    return out


def _block_diag_groups(w, per_group):
    hcount, hd, _ = w.shape
    ng = hcount // per_group
    w4 = w.reshape(ng, per_group, hd, hd)
    eye = jnp.eye(per_group, dtype=w.dtype)
    bd = w4[:, :, :, None, :] * eye[None, :, None, :, None]
    return bd.reshape(ng, per_group * hd, per_group * hd).astype(BF16)


def _diag_blocks(wbd, per_group, hd):
    ng = wbd.shape[0]
    w5 = wbd.reshape(ng, per_group, hd, per_group, hd)
    blocks = [w5[:, i, :, i, :] for i in range(per_group)]
    return jnp.stack(blocks, axis=1).reshape(ng * per_group, hd, hd)


def kernel(x, g_mix, w_in, lru_conv_w, lru_conv_b, lru_wa, lru_ba, lru_wx, lru_bx, lru_lambda, lru_w_out, sc_conv_w, sc_w_out, w_o, g_ffn, ffn_w_up, ffn_conv_w, ffn_w_down, g_final, loss_target, m_g_mix, m_w_in, m_lru_conv_w, m_lru_conv_b, m_lru_wa, m_lru_ba, m_lru_wx, m_lru_bx, m_lru_lambda, m_lru_w_out, m_sc_conv_w, m_sc_w_out, m_w_o, m_g_ffn, m_ffn_w_up, m_ffn_conv_w, m_ffn_w_down, m_g_final, v_g_mix, v_w_in, v_lru_conv_w, v_lru_conv_b, v_lru_wa, v_lru_ba, v_lru_wx, v_lru_bx, v_lru_lambda, v_lru_w_out, v_sc_conv_w, v_sc_w_out, v_w_o, v_g_ffn, v_ffn_w_up, v_ffn_conv_w, v_ffn_w_down, v_g_final):
    seq, d_model = x.shape[1], x.shape[2]
    heads, head_dim, _ = lru_wa.shape
    d_lru = heads * head_dim
    d_sc = sc_w_out.shape[0]
    d_ff = ffn_w_down.shape[0] * N_CHIPS
    assert x.shape[0] == 1 and w_in.shape[1] * N_CHIPS == 2 * d_lru + 3 * d_sc + 2 * d_model
    xs = x.reshape(seq, d_model)
    target = loss_target.reshape(seq, d_model)

    chip = 2 * lax.axis_index("x") + lax.axis_index("y")
    core = lax.axis_index("c").astype(jnp.int32).reshape(1)

    big_w = [w_in, lru_w_out, sc_w_out, w_o, ffn_w_up, ffn_w_down]
    big_m = [m_w_in, m_lru_w_out, m_sc_w_out, m_w_o, m_ffn_w_up, m_ffn_w_down]
    big_v = [v_w_in, v_lru_w_out, v_sc_w_out, v_w_o, v_ffn_w_up, v_ffn_w_down]
    col_sharded = [True, True, True, False, True, False]
    conv_shards = [lru_conv_w, sc_conv_w, ffn_conv_w]
    conv_pack = jnp.concatenate(
        [jnp.pad(w, ((0, SUBLANE - w.shape[0]), (0, 0))) for w in conv_shards], axis=1)
    big_names = ["w_in", "lru_w_out", "sc_w_out", "w_o", "ffn_w_up", "ffn_w_down"]
    chip_arr = chip.astype(jnp.int32).reshape(1)
    placed = [_cast_place(w, chip_arr, cs, "cast_" + nm) for w, cs, nm in zip(big_w, col_sharded, big_names)]
    conv_all = _small_gather(conv_pack)
    shard_shapes = [w.shape for w in big_w]
    n_big = len(big_w)
    gather_send, gather_recv, in_flight, gather_token = _exchange_start(
        "gather_start", placed, 3 * n_big, _gather_plan(shard_shapes, col_sharded, list(range(n_big))))

    def arrived(ks, after, tag):
        got = _exchange_wait("gather_wait_" + tag, [in_flight[k] for k in ks], gather_send, gather_recv, after,
                             _gather_plan(shard_shapes, col_sharded, ks))
        return _exchange("gather_forward_" + tag, got, 3 * len(ks), _forward_plan(shard_shapes, col_sharded, ks))

    conv_full, off = [], 0
    for w in conv_shards:
        kw, nq = w.shape
        piece = conv_all[:, :kw, off:off + nq]
        conv_full.append(piece.transpose(1, 0, 2).reshape(kw, N_CHIPS * nq))
        off += nq
    lcw, scw, fcw = conv_full

    per_group = max(1, min(heads, 256 // head_dim))
    gc = per_group * head_dim
    wa_bd = _block_diag_groups(lru_wa, per_group)
    wx_bd = _block_diag_groups(lru_wx, per_group)
    tc = _pick(seq, 256, SUBLANE)
    cb_sc = _pick(d_sc, 512)
    cb_ff = _pick(d_ff, 512)
    col_sc = 2 * d_lru
    col_gates = 2 * d_lru + 3 * d_sc

    h1 = _rms_fwd(_tie(xs, gather_token), g_mix, "rms_mix")
    (win_b,) = arrived([0], h1, "in")
    p = _mm(h1, win_b, "nn", F32, name="mm_in")
    wlo_b, wso_b, wo_b = arrived([1, 2, 3], p, "mix")
    y_lru_pre, hseq = _lru_fwd(p, lcw, lru_conv_b, wa_bd, lru_ba, wx_bd, lru_bx, lru_lambda, d_lru, gc, tc)
    y_sc_pre = _sc_fwd(p, scw, col_sc, d_sc, cb_sc, tc)
    y_lru = _mm(y_lru_pre, wlo_b, "nn", F32, name="mm_lru_out")
    y_sc = _mm(y_sc_pre, wso_b, "nn", F32, name="mm_sc_out")
    merged = _merge_fwd(p, y_lru, y_sc, col_gates, tc)
    x2 = _mm(merged, wo_b, "nn", F32, res=xs, name="mm_o")
    (wup_b,) = arrived([4], x2, "up")
    h2 = _rms_fwd(x2, g_ffn, "rms_ffn")
    up = _mm(h2, wup_b, "nn", F32, name="mm_up")
    (wdn_b,) = arrived([5], up, "down")
    act = _ffn_act_fwd(up, fcw, d_ff, cb_ff, tc)
    x3 = _mm(act, wdn_b, "nn", F32, res=x2, name="mm_down")
    loss_part, dx3, dx3b, dg_final = _loss_head(x3, g_final, target)

    where = jnp.concatenate([chip_arr, core])

    def reduce_start(grads, flags, tag):
        views = [_as3d(g, cs) for g, cs in zip(grads, flags)]
        lands = [lax.empty((v.shape[0], v.shape[1] // 2, v.shape[2]), F32) for v in views]
        send, recv, bufs, token = _exchange_start("grad_pair_start_" + tag, views + lands, len(views),
                                                  _pair_plan(len(views)))
        return (send, recv, bufs, flags, tag), token

    def reduce_mid(state, after):
        send, recv, bufs, flags, tag = state
        m = len(flags)
        bufs = _exchange_wait("grad_pair_wait_" + tag, bufs, send, recv, after, _pair_plan(m))
        partials = [_pair_add(bufs[i], bufs[m + i], core) for i in range(m)]
        lands = []
        for pz, cs in zip(partials, flags):
            _, h, cdim = pz.shape
            lands.append(lax.empty((N_CHIPS - 1, h, cdim // N_CHIPS if cs else cdim), BF16))
        send, recv, bufs, token = _exchange_start("grad_chip_start_" + tag, partials + lands, 3 * m,
                                                  _chip_plan(flags))
        return (send, recv, bufs, flags, tag), token

    def reduce_end(state, after):
        send, recv, bufs, flags, tag = state
        m = len(flags)
        bufs = _exchange_wait("grad_chip_wait_" + tag, bufs, send, recv, after, _chip_plan(flags))
        return [_chip_sum(bufs[i], bufs[m + i], where, flags[i]) for i in range(m)]

    g_wdn = _mm(act, dx3b, "tn", F32, name="mm_down_dw")
    red_down, token = reduce_start([g_wdn], [False], "down")
    dact = _mm(_tie(dx3b, token), wdn_b, "nt", F32, name="mm_down_dx")
    red_down, token = reduce_mid(red_down, dact)
    dupg, dupv, dfcw_g, dfcw_v = _ffn_act_bwd(up, _tie(dact, token), fcw, d_ff, cb_ff, tc)
    dup = jnp.concatenate([dupg, dupv], axis=1)
    g_wup = _mm(h2, dup, "tn", F32, name="mm_up_dw")
    red_up, token = reduce_start([g_wup], [True], "up")
    dh2 = _mm(_tie(dup, token), wup_b, "nt", F32, name="mm_up_dx")
    red_up, token = reduce_mid(red_up, dh2)
    dx2, dx2b, dg_ffn = _rms_bwd(x2, g_ffn, _tie(dh2, token), dx3, "rms_ffn_bwd", True)
    g_wo = _mm(merged, dx2b, "tn", F32, name="mm_o_dw")
    dmerged = _mm(dx2b, wo_b, "nt", F32, name="mm_o_dx")
    dgl, dgs, dyl, dys = _merge_bwd(p, y_lru, y_sc, dmerged, col_gates, tc)
    g_wlo = _mm(y_lru_pre, dyl, "tn", F32, name="mm_lru_out_dw")
    g_wso = _mm(y_sc_pre, dys, "tn", F32, name="mm_sc_out_dw")
    red_mix, token = reduce_start([g_wlo, g_wso, g_wo], [True, True, False], "mix")
    dylp = _mm(_tie(dyl, token), wlo_b, "nt", F32, name="mm_lru_out_dx")
    dysp = _mm(dys, wso_b, "nt", F32, name="mm_sc_out_dx")
    red_mix, token = reduce_mid(red_mix, dysp)
    dlx, dlgate, dlcw, dlcb, dwa_bd, dba, dwx_bd, dbx, dlam = _lru_bwd(
        p, hseq, _tie(dylp, token), lcw, lru_conv_b, wa_bd, lru_ba, wx_bd, lru_bx, lru_lambda, d_lru, gc, tc)
    dsb, dsc, dsv, dscw = _sc_bwd(p, dysp, scw, col_sc, d_sc, cb_sc, tc)
    dp = jnp.concatenate([dlx, dlgate, dsb, dsc, dsv, dgl, dgs], axis=1)
    g_win = _mm(h1, dp, "tn", F32, name="mm_in_dw")
    red_in, token = reduce_start([g_win], [True], "in")
    dh1 = _mm(_tie(dp, token), win_b, "nt", F32, name="mm_in_dx")
    red_in, token = reduce_mid(red_in, dh1)
    grad_x, dg_mix = _rms_bwd(xs, g_mix, _tie(dh1, token), dx2, "rms_mix_bwd", False)

    small_g = [dg_mix, dlcw, dlcb, _diag_blocks(dwa_bd, per_group, head_dim), dba,
               _diag_blocks(dwx_bd, per_group, head_dim), dbx, dlam, dscw, dg_ffn,
               jnp.concatenate([dfcw_g, dfcw_v], axis=1), dg_final]
    small_shapes = [a.shape for a in small_g]
    small_sum = _small_allreduce(_pack(small_g))
    (h_wdn,) = reduce_end(red_down, grad_x)
    (h_wup,) = reduce_end(red_up, grad_x)
    h_wlo, h_wso, h_wo = reduce_end(red_mix, grad_x)
    s_wlo, s_wso, s_wo, s_wup, s_wdn = _exchange("grad_share_a", [h_wlo, h_wso, h_wo, h_wup, h_wdn], 5,
                                                 _share_plan(5))
    early = {1: s_wlo, 2: s_wso, 3: s_wo, 4: s_wup, 5: s_wdn}
    big_out = [None] * n_big
    for k, g in early.items():
        big_out[k] = _adamw(big_w[k], g, big_m[k], big_v[k], "adamw_" + big_names[k])
    (h_win,) = reduce_end(red_in, big_out[4][1])
    (s_win,) = _exchange("grad_share_b", [h_win], 1, _share_plan(1))
    big_out[0] = _adamw(big_w[0], s_win, big_m[0], big_v[0], "adamw_" + big_names[0])
    sg = _unpack(small_sum, small_shapes)
    for idx in (1, 8, 10):
        nq = sg[idx].shape[1] // N_CHIPS
        sg[idx] = lax.dynamic_slice_in_dim(sg[idx], chip * nq, nq, axis=1)
    small_w = [g_mix, lru_conv_w, lru_conv_b, lru_wa, lru_ba, lru_wx, lru_bx, lru_lambda, sc_conv_w,
               g_ffn, ffn_conv_w, g_final]
    small_m = [m_g_mix, m_lru_conv_w, m_lru_conv_b, m_lru_wa, m_lru_ba, m_lru_wx, m_lru_bx, m_lru_lambda,
               m_sc_conv_w, m_g_ffn, m_ffn_conv_w, m_g_final]
    small_v = [v_g_mix, v_lru_conv_w, v_lru_conv_b, v_lru_wa, v_lru_ba, v_lru_wx, v_lru_bx, v_lru_lambda,
               v_sc_conv_w, v_g_ffn, v_ffn_conv_w, v_g_final]
    sg = [g.reshape(w.shape) for g, w in zip(sg, small_w)]
    w_shapes = [w.shape for w in small_w]
    packed = _adamw(_pack(small_w), _pack(sg), _pack(small_m), _pack(small_v), "adamw_small")
    small_out = [_unpack(pk, w_shapes) for pk in packed]

    order = [(0, 0), (1, 0), (0, 1), (0, 2), (0, 3), (0, 4), (0, 5), (0, 6), (0, 7), (1, 1), (0, 8), (1, 2),
             (1, 3), (0, 9), (1, 4), (0, 10), (1, 5), (0, 11)]
    by_kind = []
    for kind in range(4):
        by_kind.append([big_out[i][kind] if is_big else small_out[kind][i] for is_big, i in order])
    loss = lax.psum(loss_part[0, 0], ("x", "y", "c"))
    return (loss, grad_x.reshape(x.shape), *by_kind[0], *by_kind[1], *by_kind[2], *by_kind[3])
```

```python
import functools
import math

import jax
import jax.numpy as jnp
from jax import lax
from jax.experimental import pallas as pl
from jax.experimental.pallas import tpu as pltpu

F32 = jnp.float32
BF16 = jnp.bfloat16

LANE = 128
SUBLANE = 8
BF16_ROWS = 16
VMEM_BYTES_V7X = 64 * 1024 * 1024
VMEM_BUDGET = VMEM_BYTES_V7X - 8 * 1024 * 1024

EPS = 1e-6
LRU_C = 8.0
ADAM_LR = 0.001
ADAM_B1 = 0.9
ADAM_B2 = 0.999
ADAM_EPS = 1e-08
ADAM_WD = 0.01
ADAM_STEP = 10

N_CHIPS = 4
N_DEV = 8
MESH = pl.DeviceIdType.MESH
ANY = pl.BlockSpec(memory_space=pl.ANY)
VMEM_SPEC = pl.BlockSpec(memory_space=pltpu.VMEM)
HBM_SPEC = pl.BlockSpec(memory_space=pltpu.HBM)
SEM_SPEC = pl.BlockSpec(memory_space=pltpu.SEMAPHORE)
DATAFLOW_EFFECT = pltpu.SideEffectType.DATAFLOW_SIDE_EFFECTING


def _pick(n, cap, mult=LANE):
    best = None
    d = mult
    while d <= min(n, cap):
        if n % d == 0:
            best = d
        d += mult
    return n if best is None else best


def _cparams(semantics, block_bytes):
    limit = min(VMEM_BUDGET, max(32 * 1024 * 1024, int(block_bytes * 1.25) + (4 << 20)))
    return pltpu.CompilerParams(dimension_semantics=semantics, vmem_limit_bytes=limit)


def _nbytes(shape, dtype):
    return math.prod(shape) * jnp.dtype(dtype).itemsize


def _sigmoid(z):
    return 1.0 / (1.0 + jnp.exp(-z))


def _softplus(z):
    e = jnp.exp(-jnp.abs(z))
    u = 1.0 + e
    log1p = jnp.where(u == 1.0, e, jnp.log(u) * (e / (u - 1.0)))
    return jnp.maximum(z, 0.0) + log1p


def _neg_expm1(z):
    small = z * (1.0 + z * (0.5 + z * (1.0 / 6.0 + z * (1.0 / 24.0))))
    return -jnp.where(jnp.abs(z) < 0.03, small, jnp.exp(z) - 1.0)


_GELU_K = math.sqrt(2.0 / math.pi)
_GELU_C = 0.044715


def _gelu_and_grad(z):
    z2 = z * z
    th = jnp.tanh(_GELU_K * (z + _GELU_C * z2 * z))
    val = 0.5 * z * (1.0 + th)
    grad = 0.5 * (1.0 + th) + 0.5 * z * (1.0 - th * th) * (_GELU_K * (1.0 + 3.0 * _GELU_C * z2))
    return val, grad


def _rows_before(cat, k):
    if k == 0:
        return cat[SUBLANE:, :]
    return pltpu.roll(cat, k, 0)[SUBLANE:, :]


def _rows_after(cat, k):
    n = cat.shape[0]
    if k == 0:
        return cat[:n - SUBLANE, :]
    return pltpu.roll(cat, n - k, 0)[:n - SUBLANE, :]


def _conv_fwd(cat, w, width):
    y = _rows_before(cat, width - 1) * w[0:1, :]
    for k in range(1, width):
        y = y + _rows_before(cat, width - 1 - k) * w[k:k + 1, :]
    return y


def _conv_bwd_input(cat, w, width):
    dx = _rows_after(cat, width - 1) * w[0:1, :]
    for k in range(1, width):
        dx = dx + _rows_after(cat, width - 1 - k) * w[k:k + 1, :]
    return dx


def _conv_bwd_weight(dw_ref, dy, catx, width):
    for k in range(width):
        dw_ref[k:k + 1, :] += jnp.sum(dy * _rows_before(catx, width - 1 - k), axis=0, keepdims=True)


def _scan_tiles(a_ref, b_ref, out_ref, carry0, n_rows, reverse):
    cols = a_ref.shape[1]
    row = lax.broadcasted_iota(jnp.int32, (SUBLANE, cols), 0)
    n_tiles = n_rows // SUBLANE

    def step(j, carry):
        tile = (n_tiles - 1 - j) if reverse else j
        off = pl.multiple_of(tile * SUBLANE, SUBLANE)
        a = a_ref[pl.ds(off, SUBLANE), :]
        b = b_ref[pl.ds(off, SUBLANE), :]
        for s in (1, 2, 4):
            if reverse:
                keep = row < SUBLANE - s
                shift = SUBLANE - s
            else:
                keep = row >= s
                shift = s
            a_sh = jnp.where(keep, pltpu.roll(a, shift, 0), 1.0)
            b_sh = jnp.where(keep, pltpu.roll(b, shift, 0), 0.0)
            b = a * b_sh + b
            a = a * a_sh
        out = a * carry + b
        out_ref[pl.ds(off, SUBLANE), :] = out
        return out[0:1, :] if reverse else out[SUBLANE - 1:SUBLANE, :]

    return lax.fori_loop(0, n_tiles, step, carry0)


def _dep_args(body, in_specs, operands, dep):
    if dep is None:
        return body, in_specs, operands
    n = len(operands)

    def wrapped(*refs):
        return body(*refs[:n], *refs[n + 1:])

    return wrapped, list(in_specs) + [ANY], list(operands) + [dep]


def _mm(a, b, mode, out_dtype, res=None, name=None, dep=None):
    if mode == "nn":
        (m, k), (k2, n) = a.shape, b.shape
        dims = (((1,), (0,)), ((), ()))
    elif mode == "nt":
        (m, k), (n, k2) = a.shape, b.shape
        dims = (((1,), (1,)), ((), ()))
    else:
        (k, m), (k2, n) = a.shape, b.shape
        dims = (((0,), (0,)), ((), ()))
    assert k == k2 and a.dtype == BF16 and b.dtype == BF16
    bm = _pick(m, 1024)
    bn = _pick(n, 1024)
    bk = _pick(k, 2048)
    nk = k // bk
    out_bytes = jnp.dtype(out_dtype).itemsize

    def est(bk_):
        e = 2 * (bm * bk_ + bk_ * bn) * 2 + 2 * bm * bn * out_bytes
        if nk > 1 or k // bk_ > 1:
            e += bm * bn * 4
        if res is not None:
            e += 2 * bm * bn * 4
        return e

    while est(bk) > 40 * 1024 * 1024 and bk % (2 * LANE) == 0 and k % (bk // 2) == 0:
        bk //= 2
        nk = k // bk

    if mode == "tn":
        a_spec = pl.BlockSpec((bk, bm), lambda i, j, kk: (kk, i))
    else:
        a_spec = pl.BlockSpec((bm, bk), lambda i, j, kk: (i, kk))
    if mode == "nt":
        b_spec = pl.BlockSpec((bn, bk), lambda i, j, kk: (j, kk))
    else:
        b_spec = pl.BlockSpec((bk, bn), lambda i, j, kk: (kk, j))
    o_spec = pl.BlockSpec((bm, bn), lambda i, j, kk: (i, j))
    in_specs = [a_spec, b_spec]
    operands = [a, b]
    if res is not None:
        in_specs.append(o_spec)
        operands.append(res)
    has_res = res is not None

    def body(*refs):
        a_ref, b_ref = refs[0], refs[1]
        res_ref = refs[2] if has_res else None
        o_ref = refs[3] if has_res else refs[2]
        part = lax.dot_general(a_ref[...], b_ref[...], dims, preferred_element_type=F32)
        if nk == 1:
            if has_res:
                part = part + res_ref[...]
            o_ref[...] = part.astype(o_ref.dtype)
            return
        acc_ref = refs[-1]
        kk = pl.program_id(2)

        @pl.when(kk == 0)
        def _():
            acc_ref[...] = part

        @pl.when(kk > 0)
        def _():
            acc_ref[...] += part

        @pl.when(kk == nk - 1)
        def _():
            total = acc_ref[...]
            if has_res:
                total = total + res_ref[...]
            o_ref[...] = total.astype(o_ref.dtype)

    scratch = [pltpu.VMEM((bm, bn), F32)] if nk > 1 else []
    body, in_specs, operands = _dep_args(body, in_specs, operands, dep)
    return pl.pallas_call(
        body,
        out_shape=jax.ShapeDtypeStruct((m, n), out_dtype),
        grid=(m // bm, n // bn, nk),
        in_specs=in_specs,
        out_specs=o_spec,
        scratch_shapes=scratch,
        compiler_params=_cparams(("parallel", "parallel", "arbitrary"), est(bk)),
        name=name,
    )(*operands)


def _rms_fwd(x, g, name, dep=None):
    t, d = x.shape
    tb = _pick(t, 512, SUBLANE)

    def body(x_ref, g_ref, h_ref):
        xv = x_ref[...]
        r = lax.rsqrt(jnp.mean(xv * xv, axis=-1, keepdims=True) + EPS)
        h_ref[...] = ((xv * r) * g_ref[...]).astype(BF16)

    blk = pl.BlockSpec((tb, d), lambda i: (i, 0))
    body, in_specs, operands = _dep_args(
        body, [blk, pl.BlockSpec((1, d), lambda i: (0, 0))], [x, g.reshape(1, d)], dep)
    return pl.pallas_call(
        body,
        out_shape=jax.ShapeDtypeStruct((t, d), BF16),
        grid=(t // tb,),
        in_specs=in_specs,
        out_specs=blk,
        compiler_params=_cparams(("parallel",), 2 * tb * d * 6),
        name=name,
    )(*operands)


def _rms_bwd(x, g, dh, dres, name, want_bf16, dep=None):
    t, d = x.shape
    tb = _pick(t, 256, SUBLANE)

    def body(x_ref, g_ref, dh_ref, dres_ref, *outs):
        dx_ref, dg_ref = outs[0], outs[-1]
        xv = x_ref[...]
        r = lax.rsqrt(jnp.mean(xv * xv, axis=-1, keepdims=True) + EPS)
        xhat = xv * r
        dhv = dh_ref[...]
        dxhat = dhv * g_ref[...]
        dx = dres_ref[...] + r * (dxhat - xhat * jnp.mean(dxhat * xhat, axis=-1, keepdims=True))
        dx_ref[...] = dx
        if want_bf16:
            outs[1][...] = dx.astype(BF16)

        @pl.when(pl.program_id(0) == 0)
        def _():
            dg_ref[...] = jnp.zeros_like(dg_ref)

        dg_ref[...] += jnp.sum(dhv * xhat, axis=0, keepdims=True)

    blk = pl.BlockSpec((tb, d), lambda i: (i, 0))
    row = pl.BlockSpec((1, d), lambda i: (0, 0))
    out_shape = [jax.ShapeDtypeStruct((t, d), F32)]
    out_specs = [blk]
    if want_bf16:
        out_shape.append(jax.ShapeDtypeStruct((t, d), BF16))
        out_specs.append(blk)
    out_shape.append(jax.ShapeDtypeStruct((1, d), F32))
    out_specs.append(row)
    body, in_specs, operands = _dep_args(body, [blk, row, blk, blk], [x, g.reshape(1, d), dh, dres], dep)
    return pl.pallas_call(
        body,
        out_shape=out_shape,
        grid=(t // tb,),
        in_specs=in_specs,
        out_specs=out_specs,
        compiler_params=_cparams(("arbitrary",), 2 * tb * d * 18),
        name=name,
    )(*operands)


def _loss_head(x3, g, target):
    t, d = x3.shape
    tb = _pick(t, 256, SUBLANE)

    def body(x_ref, g_ref, t_ref, loss_ref, dx_ref, dxb_ref, dg_ref):
        xv = x_ref[...]
        gv = g_ref[...]
        r = lax.rsqrt(jnp.mean(xv * xv, axis=-1, keepdims=True) + EPS)
        xhat = xv * r
        err = xhat * gv - t_ref[...]
        dy = err * (1.0 / d)
        dxhat = dy * gv
        dx = r * (dxhat - xhat * jnp.mean(dxhat * xhat, axis=-1, keepdims=True))
        dx_ref[...] = dx
        dxb_ref[...] = dx.astype(BF16)

        @pl.when(pl.program_id(0) == 0)
        def _():
            dg_ref[...] = jnp.zeros_like(dg_ref)
            loss_ref[...] = jnp.zeros_like(loss_ref)

        dg_ref[...] += jnp.sum(dy * xhat, axis=0, keepdims=True)
        per_token = jnp.mean(err * err, axis=-1, keepdims=True)
        loss_ref[...] += 0.5 * jnp.sum(per_token, axis=0, keepdims=True)

    blk = pl.BlockSpec((tb, d), lambda i: (i, 0))
    row = pl.BlockSpec((1, d), lambda i: (0, 0))
    return pl.pallas_call(
        body,
        out_shape=[jax.ShapeDtypeStruct((1, 1), F32), jax.ShapeDtypeStruct((t, d), F32),
                   jax.ShapeDtypeStruct((t, d), BF16), jax.ShapeDtypeStruct((1, d), F32)],
        grid=(t // tb,),
        in_specs=[blk, row, blk],
        out_specs=[pl.BlockSpec((1, 1), lambda i: (0, 0)), blk, blk, row],
        compiler_params=_cparams(("arbitrary",), 2 * tb * d * 14),
        name="loss_head",
    )(x3, g.reshape(1, d), target)


def _lru_gates(xc, wa, ba, wx, bx, lam):
    nn = (((1,), (0,)), ((), ()))
    xcb = xc.astype(BF16)
    r = _sigmoid(lax.dot_general(xcb, wa, nn, preferred_element_type=F32) + ba)
    i = _sigmoid(lax.dot_general(xcb, wx, nn, preferred_element_type=F32) + bx)
    cl = -LRU_C * _softplus(-lam)
    log_a = cl * r
    a = jnp.exp(log_a)
    one_minus_a2 = _neg_expm1(2.0 * log_a)
    return xcb, r, i, a, one_minus_a2, cl


def _lru_fwd(p, conv_w, conv_b, wa_bd, ba, wx_bd, bx, lam, d_lru, gc, tc):
    t = p.shape[0]
    ng = d_lru // gc
    nt = t // tc
    width = conv_w.shape[0]

    def body(lx_ref, gate_ref, cw_ref, cb_ref, wa_ref, ba_ref, wx_ref, bx_ref, lam_ref,
             y_ref, h_ref, halo, hcar, a_s, u_s):
        @pl.when(pl.program_id(1) == 0)
        def _():
            halo[...] = jnp.zeros_like(halo)
            hcar[...] = jnp.zeros_like(hcar)

        x = lx_ref[...]
        cat = jnp.concatenate([halo[...], x], axis=0)
        halo[...] = x[tc - SUBLANE:, :]
        xc = _conv_fwd(cat, cw_ref[...], width) + cb_ref[...]
        _, r, i, a, om, _ = _lru_gates(xc, wa_ref[...], ba_ref[...], wx_ref[...], bx_ref[...], lam_ref[...])
        a_s[...] = a
        u_s[...] = jnp.sqrt(om) * (i * xc)
        hcar[0:1, :] = _scan_tiles(a_s, u_s, h_ref, hcar[0:1, :], tc, reverse=False)
        gl, _ = _gelu_and_grad(gate_ref[...])
        y_ref[...] = (gl * h_ref[...]).astype(BF16)

    blk = lambda off: pl.BlockSpec((tc, gc), lambda g, s, off=off: (s, off + g))
    rowv = lambda rows: pl.BlockSpec((rows, gc), lambda g, s: (0, g))
    wspec = pl.BlockSpec((None, gc, gc), lambda g, s: (g, 0, 0))
    out_blk = pl.BlockSpec((tc, gc), lambda g, s: (s, g))
    return pl.pallas_call(
        body,
        out_shape=[jax.ShapeDtypeStruct((t, d_lru), BF16), jax.ShapeDtypeStruct((t, d_lru), F32)],
        grid=(ng, nt),
        in_specs=[blk(0), blk(ng), rowv(width), rowv(1), wspec, rowv(1), wspec, rowv(1), rowv(1)],
        out_specs=[out_blk, out_blk],
        scratch_shapes=[pltpu.VMEM((SUBLANE, gc), F32), pltpu.VMEM((SUBLANE, gc), F32),
                        pltpu.VMEM((tc, gc), F32), pltpu.VMEM((tc, gc), F32)],
        compiler_params=_cparams(("parallel", "arbitrary"), 40 * tc * gc * 4),
        name="lru_fwd",
    )(p, p, conv_w, conv_b.reshape(1, -1), wa_bd, ba.reshape(1, -1), wx_bd, bx.reshape(1, -1),
      lam.reshape(1, -1))


def _lru_bwd(p, hseq, dyp, conv_w, conv_b, wa_bd, ba, wx_bd, bx, lam, d_lru, gc, tc, dep=None):
    t = p.shape[0]
    ng = d_lru // gc
    nt = t // tc
    width = conv_w.shape[0]
    halo_blocks = tc // SUBLANE
    nn = (((1,), (0,)), ((), ()))
    nt_dims = (((1,), (1,)), ((), ()))
    tn_dims = (((0,), (0,)), ((), ()))

    def body(lx_ref, lxh_ref, gate_ref, h_ref, hh_ref, dyp_ref,
             cw_ref, cb_ref, wa_ref, ba_ref, wx_ref, bx_ref, lam_ref,
             dlx_ref, dgate_ref, dcw_ref, dcb_ref, dwa_ref, dba_ref, dwx_ref, dbx_ref, dlam_ref,
             nxt_dxc, nxt_a, nxt_g, al_s, b_s, g_s):
        s = pl.program_id(1)
        first_chunk = s == nt - 1

        @pl.when(s == 0)
        def _():
            nxt_dxc[...] = jnp.zeros_like(nxt_dxc)
            nxt_a[...] = jnp.zeros_like(nxt_a)
            nxt_g[...] = jnp.zeros_like(nxt_g)
            for ref in (dcw_ref, dcb_ref, dwa_ref, dba_ref, dwx_ref, dbx_ref, dlam_ref):
                ref[...] = jnp.zeros_like(ref)

        keep = jnp.where(first_chunk, 0.0, 1.0)
        x = lx_ref[...]
        catx = jnp.concatenate([lxh_ref[...] * keep, x], axis=0)
        cw = cw_ref[...]
        xc = _conv_fwd(catx, cw, width) + cb_ref[...]
        wa = wa_ref[...]
        wx = wx_ref[...]
        lam_v = lam_ref[...]
        xcb, r, i, a, om, cl = _lru_gates(xc, wa, ba_ref[...], wx, bx_ref[...], lam_v)
        mult = jnp.sqrt(om)

        h = h_ref[...]
        hprev = _rows_before(jnp.concatenate([hh_ref[...] * keep, h], axis=0), 1)
        gl, dgl = _gelu_and_grad(gate_ref[...])
        dyp_v = dyp_ref[...]
        dgate_ref[...] = (dyp_v * h * dgl).astype(BF16)

        al_s[...] = _rows_after(jnp.concatenate([a, nxt_a[...]], axis=0), 1)
        b_s[...] = dyp_v * gl
        nxt_g[0:1, :] = _scan_tiles(al_s, b_s, g_s, nxt_g[0:1, :], tc, reverse=True)
        nxt_a[...] = a[0:SUBLANE, :]
        du = g_s[...]

        da = du * hprev
        dmult = du * (i * xc)
        di = du * mult * xc
        dxc = du * mult * i
        dlog_a = da * a - dmult * (a * a / mult)
        dlam_ref[...] += jnp.sum(dlog_a * r, axis=0, keepdims=True) * (LRU_C * _sigmoid(-lam_v))
        dza = (dlog_a * cl) * r * (1.0 - r)
        dzx = di * i * (1.0 - i)
        dba_ref[...] += jnp.sum(dza, axis=0, keepdims=True)
        dbx_ref[...] += jnp.sum(dzx, axis=0, keepdims=True)
        dzab = dza.astype(BF16)
        dzxb = dzx.astype(BF16)
        dwa_ref[...] += lax.dot_general(xcb, dzab, tn_dims, preferred_element_type=F32)
        dwx_ref[...] += lax.dot_general(xcb, dzxb, tn_dims, preferred_element_type=F32)
        dxc = dxc + lax.dot_general(dzab, wa, nt_dims, preferred_element_type=F32)
        dxc = dxc + lax.dot_general(dzxb, wx, nt_dims, preferred_element_type=F32)
        dcb_ref[...] += jnp.sum(dxc, axis=0, keepdims=True)
        _conv_bwd_weight(dcw_ref, dxc, catx, width)
        catd = jnp.concatenate([dxc, nxt_dxc[...]], axis=0)
        dlx_ref[...] = _conv_bwd_input(catd, cw, width).astype(BF16)
        nxt_dxc[...] = dxc[0:SUBLANE, :]

    rev = lambda s: nt - 1 - s
    blk = lambda off: pl.BlockSpec((tc, gc), lambda g, s, off=off: (rev(s), off + g))
    halo = lambda off: pl.BlockSpec(
        (SUBLANE, gc), lambda g, s, off=off: (jnp.maximum(rev(s) * halo_blocks - 1, 0), off + g))
    rowv = lambda rows: pl.BlockSpec((rows, gc), lambda g, s: (0, g))
    wspec = pl.BlockSpec((None, gc, gc), lambda g, s: (g, 0, 0))
    out_blk = pl.BlockSpec((tc, gc), lambda g, s: (rev(s), g))
    vec = lambda rows: jax.ShapeDtypeStruct((rows, d_lru), F32)
    wshape = jax.ShapeDtypeStruct((ng, gc, gc), F32)
    body, in_specs, operands = _dep_args(
        body,
        [blk(0), halo(0), blk(ng), blk(0), halo(0), blk(0),
         rowv(width), rowv(1), wspec, rowv(1), wspec, rowv(1), rowv(1)],
        [p, p, p, hseq, hseq, dyp, conv_w, conv_b.reshape(1, -1), wa_bd, ba.reshape(1, -1), wx_bd,
         bx.reshape(1, -1), lam.reshape(1, -1)], dep)
    return pl.pallas_call(
        body,
        out_shape=[jax.ShapeDtypeStruct((t, d_lru), BF16), jax.ShapeDtypeStruct((t, d_lru), BF16),
                   vec(width), vec(1), wshape, vec(1), wshape, vec(1), vec(1)],
        grid=(ng, nt),
        in_specs=in_specs,
        out_specs=[out_blk, out_blk, rowv(width), rowv(1), wspec, rowv(1), wspec, rowv(1), rowv(1)],
        scratch_shapes=[pltpu.VMEM((SUBLANE, gc), F32), pltpu.VMEM((SUBLANE, gc), F32),
                        pltpu.VMEM((SUBLANE, gc), F32),
                        pltpu.VMEM((tc, gc), F32), pltpu.VMEM((tc, gc), F32), pltpu.VMEM((tc, gc), F32)],
        compiler_params=_cparams(("parallel", "arbitrary"), 80 * tc * gc * 4),
        name="lru_bwd",
    )(*operands)


def _sc_fwd(p, conv_w, col0, d_sc, cb, tc):
    t = p.shape[0]
    nc = d_sc // cb
    nt = t // tc
    width = conv_w.shape[0]
    base = col0 // cb

    def body(b_ref, c_ref, v_ref, w_ref, y_ref, halo):
        @pl.when(pl.program_id(1) == 0)
        def _():
            halo[...] = jnp.zeros_like(halo)

        cv = c_ref[...] * v_ref[...]
        cat = jnp.concatenate([halo[...], cv], axis=0)
        halo[...] = cv[tc - SUBLANE:, :]
        y_ref[...] = (b_ref[...] * _conv_fwd(cat, w_ref[...], width)).astype(BF16)

    blk = lambda slab: pl.BlockSpec((tc, cb), lambda j, s, slab=slab: (s, base + slab * nc + j))
    return pl.pallas_call(
        body,
        out_shape=jax.ShapeDtypeStruct((t, d_sc), BF16),
        grid=(nc, nt),
        in_specs=[blk(0), blk(1), blk(2), pl.BlockSpec((width, cb), lambda j, s: (0, j))],
        out_specs=pl.BlockSpec((tc, cb), lambda j, s: (s, j)),
        scratch_shapes=[pltpu.VMEM((SUBLANE, cb), F32)],
        compiler_params=_cparams(("parallel", "arbitrary"), 20 * tc * cb * 4),
        name="sc_fwd",
    )(p, p, p, conv_w)


def _sc_bwd(p, dyp, conv_w, col0, d_sc, cb, tc):
    t = p.shape[0]
    nc = d_sc // cb
    nt = t // tc
    width = conv_w.shape[0]
    base = col0 // cb
    halo_blocks = tc // SUBLANE

    def body(b_ref, c_ref, ch_ref, v_ref, vh_ref, dyp_ref, w_ref,
             db_ref, dc_ref, dv_ref, dw_ref, nxt_dq):
        s = pl.program_id(1)

        @pl.when(s == 0)
        def _():
            nxt_dq[...] = jnp.zeros_like(nxt_dq)
            dw_ref[...] = jnp.zeros_like(dw_ref)

        keep = jnp.where(s == nt - 1, 0.0, 1.0)
        cvals = c_ref[...]
        vvals = v_ref[...]
        w = w_ref[...]
        catcv = jnp.concatenate([ch_ref[...] * vh_ref[...] * keep, cvals * vvals], axis=0)
        q = _conv_fwd(catcv, w, width)
        dyp_v = dyp_ref[...]
        db_ref[...] = (dyp_v * q).astype(BF16)
        dq = dyp_v * b_ref[...]
        _conv_bwd_weight(dw_ref, dq, catcv, width)
        dcv = _conv_bwd_input(jnp.concatenate([dq, nxt_dq[...]], axis=0), w, width)
        nxt_dq[...] = dq[0:SUBLANE, :]
        dc_ref[...] = (dcv * vvals).astype(BF16)
        dv_ref[...] = (dcv * cvals).astype(BF16)

    rev = lambda s: nt - 1 - s
    blk = lambda slab: pl.BlockSpec((tc, cb), lambda j, s, slab=slab: (rev(s), base + slab * nc + j))
    halo = lambda slab: pl.BlockSpec(
        (SUBLANE, cb),
        lambda j, s, slab=slab: (jnp.maximum(rev(s) * halo_blocks - 1, 0), base + slab * nc + j))
    out_blk = pl.BlockSpec((tc, cb), lambda j, s: (rev(s), j))
    wblk = pl.BlockSpec((width, cb), lambda j, s: (0, j))
    act = jax.ShapeDtypeStruct((t, d_sc), BF16)
    return pl.pallas_call(
        body,
        out_shape=[act, act, act, jax.ShapeDtypeStruct((width, d_sc), F32)],
        grid=(nc, nt),
        in_specs=[blk(0), blk(1), halo(1), blk(2), halo(2), out_blk, wblk],
        out_specs=[out_blk, out_blk, out_blk, wblk],
        scratch_shapes=[pltpu.VMEM((SUBLANE, cb), F32)],
        compiler_params=_cparams(("parallel", "arbitrary"), 30 * tc * cb * 4),
        name="sc_bwd",
    )(p, p, p, p, p, dyp, conv_w)


def _merge_fwd(p, y_lru, y_sc, col0, tc):
    t, d = y_lru.shape
    cb = _pick(math.gcd(d, col0), 1024)
    nc = d // cb
    base = col0 // cb

    def body(gl_ref, gs_ref, yl_ref, ys_ref, o_ref):
        o_ref[...] = (_sigmoid(gl_ref[...]) * yl_ref[...] + _sigmoid(gs_ref[...]) * ys_ref[...]).astype(BF16)

    gate = lambda slab: pl.BlockSpec((tc, cb), lambda s, j, slab=slab: (s, base + slab * nc + j))
    blk = pl.BlockSpec((tc, cb), lambda s, j: (s, j))
    return pl.pallas_call(
        body,
        out_shape=jax.ShapeDtypeStruct((t, d), BF16),
        grid=(t // tc, nc),
        in_specs=[gate(0), gate(1), blk, blk],
        out_specs=blk,
        compiler_params=_cparams(("parallel", "parallel"), 2 * tc * cb * 20),
        name="merge_fwd",
    )(p, p, y_lru, y_sc)


def _merge_bwd(p, y_lru, y_sc, dmerged, col0, tc):
    t, d = y_lru.shape
    cb = _pick(math.gcd(d, col0), 1024)
    nc = d // cb
    base = col0 // cb

    def body(gl_ref, gs_ref, yl_ref, ys_ref, dm_ref, dgl_ref, dgs_ref, dyl_ref, dys_ref):
        dm = dm_ref[...]
        sl = _sigmoid(gl_ref[...])
        ss = _sigmoid(gs_ref[...])
        dgl_ref[...] = (dm * yl_ref[...] * (sl * (1.0 - sl))).astype(BF16)
        dgs_ref[...] = (dm * ys_ref[...] * (ss * (1.0 - ss))).astype(BF16)
        dyl_ref[...] = (dm * sl).astype(BF16)
        dys_ref[...] = (dm * ss).astype(BF16)

    gate = lambda slab: pl.BlockSpec((tc, cb), lambda s, j, slab=slab: (s, base + slab * nc + j))
    blk = pl.BlockSpec((tc, cb), lambda s, j: (s, j))
    act = jax.ShapeDtypeStruct((t, d), BF16)
    return pl.pallas_call(
        body,
        out_shape=[act, act, act, act],
        grid=(t // tc, nc),
        in_specs=[gate(0), gate(1), blk, blk, blk],
        out_specs=[blk, blk, blk, blk],
        compiler_params=_cparams(("parallel", "parallel"), 2 * tc * cb * 28),
        name="merge_bwd",
    )(p, p, y_lru, y_sc, dmerged)


def _ffn_act_fwd(up, conv_w, d_ff, cb, tc):
    t = up.shape[0]
    nc = d_ff // cb
    nt = t // tc
    width = conv_w.shape[0]

    def body(g_ref, v_ref, wg_ref, wv_ref, o_ref, halo_g, halo_v):
        @pl.when(pl.program_id(1) == 0)
        def _():
            halo_g[...] = jnp.zeros_like(halo_g)
            halo_v[...] = jnp.zeros_like(halo_v)

        g = g_ref[...]
        v = v_ref[...]
        ug = _conv_fwd(jnp.concatenate([halo_g[...], g], axis=0), wg_ref[...], width)
        uv = _conv_fwd(jnp.concatenate([halo_v[...], v], axis=0), wv_ref[...], width)
        halo_g[...] = g[tc - SUBLANE:, :]
        halo_v[...] = v[tc - SUBLANE:, :]
        o_ref[...] = (ug * _sigmoid(ug) * uv).astype(BF16)

    blk = lambda half: pl.BlockSpec((tc, cb), lambda j, s, half=half: (s, half * nc + j))
    wblk = lambda half: pl.BlockSpec((width, cb), lambda j, s, half=half: (0, half * nc + j))
    return pl.pallas_call(
        body,
        out_shape=jax.ShapeDtypeStruct((t, d_ff), BF16),
        grid=(nc, nt),
        in_specs=[blk(0), blk(1), wblk(0), wblk(1)],
        out_specs=pl.BlockSpec((tc, cb), lambda j, s: (s, j)),
        scratch_shapes=[pltpu.VMEM((SUBLANE, cb), F32), pltpu.VMEM((SUBLANE, cb), F32)],
        compiler_params=_cparams(("parallel", "arbitrary"), 24 * tc * cb * 4),
        name="ffn_act_fwd",
    )(up, up, conv_w, conv_w)


def _ffn_act_bwd(up, dact, conv_w, d_ff, cb, tc, dep=None):
    t = up.shape[0]
    nc = d_ff // cb
    nt = t // tc
    width = conv_w.shape[0]
    halo_blocks = tc // SUBLANE

    def body(g_ref, gh_ref, v_ref, vh_ref, da_ref, wg_ref, wv_ref,
             dg_ref, dv_ref, dwg_ref, dwv_ref, nxt_g, nxt_v):
        s = pl.program_id(1)

        @pl.when(s == 0)
        def _():
            nxt_g[...] = jnp.zeros_like(nxt_g)
            nxt_v[...] = jnp.zeros_like(nxt_v)
            dwg_ref[...] = jnp.zeros_like(dwg_ref)
            dwv_ref[...] = jnp.zeros_like(dwv_ref)

        keep = jnp.where(s == nt - 1, 0.0, 1.0)
        wg = wg_ref[...]
        wv = wv_ref[...]
        catg = jnp.concatenate([gh_ref[...] * keep, g_ref[...]], axis=0)
        catv = jnp.concatenate([vh_ref[...] * keep, v_ref[...]], axis=0)
        ug = _conv_fwd(catg, wg, width)
        uv = _conv_fwd(catv, wv, width)
        sg = _sigmoid(ug)
        da = da_ref[...]
        dug = da * uv * (sg * (1.0 + ug * (1.0 - sg)))
        duv = da * (ug * sg)
        _conv_bwd_weight(dwg_ref, dug, catg, width)
        _conv_bwd_weight(dwv_ref, duv, catv, width)
        dg_ref[...] = _conv_bwd_input(jnp.concatenate([dug, nxt_g[...]], axis=0), wg, width).astype(BF16)
        dv_ref[...] = _conv_bwd_input(jnp.concatenate([duv, nxt_v[...]], axis=0), wv, width).astype(BF16)
        nxt_g[...] = dug[0:SUBLANE, :]
        nxt_v[...] = duv[0:SUBLANE, :]

    rev = lambda s: nt - 1 - s
    blk = lambda half: pl.BlockSpec((tc, cb), lambda j, s, half=half: (rev(s), half * nc + j))
    halo = lambda half: pl.BlockSpec(
        (SUBLANE, cb), lambda j, s, half=half: (jnp.maximum(rev(s) * halo_blocks - 1, 0), half * nc + j))
    wblk = lambda half: pl.BlockSpec((width, cb), lambda j, s, half=half: (0, half * nc + j))
    out_blk = pl.BlockSpec((tc, cb), lambda j, s: (rev(s), j))
    wout = pl.BlockSpec((width, cb), lambda j, s: (0, j))
    act = jax.ShapeDtypeStruct((t, d_ff), BF16)
    wshape = jax.ShapeDtypeStruct((width, d_ff), F32)
    body, in_specs, operands = _dep_args(
        body, [blk(0), halo(0), blk(1), halo(1), out_blk, wblk(0), wblk(1)],
        [up, up, up, up, dact, conv_w, conv_w], dep)
    return pl.pallas_call(
        body,
        out_shape=[act, act, wshape, wshape],
        grid=(nc, nt),
        in_specs=in_specs,
        out_specs=[out_blk, out_blk, wout, wout],
        scratch_shapes=[pltpu.VMEM((SUBLANE, cb), F32), pltpu.VMEM((SUBLANE, cb), F32)],
        compiler_params=_cparams(("parallel", "arbitrary"), 40 * tc * cb * 4),
        name="ffn_act_bwd",
    )(*operands)


def _mesh_pos():
    x, y, c = lax.axis_index("x"), lax.axis_index("y"), lax.axis_index("c")
    return x, y, c


def _other_chips(x, y):
    return [(1 - x, y), (x, 1 - y), (1 - x, 1 - y)]


def _cast_place(w, chip, col_sharded, name):
    r, cdim = w.shape
    full = (r, cdim * N_CHIPS) if col_sharded else (r * N_CHIPS, cdim)
    rb = _pick(r, max(BF16_ROWS, (512 * 1024) // cdim), BF16_ROWS)
    nb = r // rb

    def body(chip_ref, w_ref, o_ref):
        o_ref[...] = w_ref[...].astype(BF16)

    if col_sharded:
        out_map = lambda i, chip_ref: (i, chip_ref[0])
    else:
        out_map = lambda i, chip_ref: (chip_ref[0] * nb + i, 0)
    grid_spec = pltpu.PrefetchScalarGridSpec(
        num_scalar_prefetch=1,
        grid=(nb,),
        in_specs=[pl.BlockSpec((rb, cdim), lambda i, chip_ref: (i, 0))],
        out_specs=pl.BlockSpec((rb, cdim), out_map),
    )
    return pl.pallas_call(
        body,
        out_shape=jax.ShapeDtypeStruct(full, BF16),
        grid_spec=grid_spec,
        compiler_params=_cparams(("parallel",), 2 * rb * cdim * 6),
        name=name,
    )(chip, w)


def _remote(src, dst, send_sems, recv_sems, idx, to):
    return pltpu.make_async_remote_copy(
        src_ref=src, dst_ref=dst, send_sem=send_sems.at[idx], recv_sem=recv_sems.at[idx],
        device_id=to, device_id_type=MESH)


def _exchange(name, arrays, n_sems, plan):
    n = len(arrays)

    def body(*refs):
        bufs = refs[n:2 * n]
        send_sems, recv_sems = refs[2 * n:]
        sends, arrivals = plan(bufs, send_sems, recv_sems)
        for cp in sends:
            cp.start()
        for cp in arrivals:
            cp.wait_recv()
        for cp in sends:
            cp.wait_send()

    outs = pl.pallas_call(
        body,
        out_shape=[jax.ShapeDtypeStruct(a.shape, a.dtype) for a in arrays],
        in_specs=[ANY] * n,
        out_specs=[ANY] * n,
        input_output_aliases={k: k for k in range(n)},
        scratch_shapes=[pltpu.SemaphoreType.DMA((n_sems,)), pltpu.SemaphoreType.DMA((n_sems,))],
        name=name,
    )(*arrays)
    return list(outs)


def _exchange_start(name, arrays, n_sems, plan, after=None):
    n = len(arrays)
    n_in = n + (after is not None)

    def body(*refs):
        bufs = refs[:n]
        send_sems, recv_sems = refs[n_in], refs[n_in + 1]
        token = refs[-1]
        sends, _ = plan(bufs, send_sems, recv_sems)
        for cp in sends:
            cp.start()
        token[...] = jnp.zeros_like(token)

    out = pl.pallas_call(
        body,
        out_shape=(pltpu.SemaphoreType.DMA((n_sems,)), pltpu.SemaphoreType.DMA((n_sems,)),
                   *[pltpu.HBM(a.shape, a.dtype) for a in arrays],
                   jax.ShapeDtypeStruct((SUBLANE, LANE), F32)),
        in_specs=[HBM_SPEC] * n + [ANY] * (n_in - n),
        out_specs=(SEM_SPEC, SEM_SPEC, *[HBM_SPEC] * n, VMEM_SPEC),
        input_output_aliases={k: 2 + k for k in range(n)},
        compiler_params=pltpu.CompilerParams(has_side_effects=DATAFLOW_EFFECT),
        name=name,
    )(*[pltpu.with_memory_space_constraint(a, pltpu.HBM) for a in arrays], *([after] if after is not None else []))
    return out[0], out[1], list(out[2:2 + n]), out[-1]


def _exchange_wait(name, arrays, send_sems, recv_sems, after, plan):
    n = len(arrays)

    def body(*refs):
        bufs = refs[:n]
        sends, arrivals = plan(bufs, refs[n], refs[n + 1])
        for cp in arrivals:
            cp.wait_recv()
        for cp in sends:
            cp.wait_send()

    outs = pl.pallas_call(
        body,
        out_shape=[pltpu.HBM(a.shape, a.dtype) for a in arrays],
        in_specs=[HBM_SPEC] * n + [SEM_SPEC, SEM_SPEC, ANY],
        out_specs=[HBM_SPEC] * n,
        input_output_aliases={k: k for k in range(n)},
        compiler_params=pltpu.CompilerParams(has_side_effects=DATAFLOW_EFFECT),
        name=name,
    )(*arrays, send_sems, recv_sems, after)
    return list(outs)


def _half_block(ref, shard_shape, col_sharded, chip, half):
    r, cdim = shard_shape
    h = r // 2
    if col_sharded:
        return ref.at[pl.ds(pl.multiple_of(half * h, BF16_ROWS), h),
                      pl.ds(pl.multiple_of(chip * cdim, LANE), cdim)]
    return ref.at[pl.ds(pl.multiple_of(chip * r + half * h, BF16_ROWS), h), :]


def _gather_plan(shard_shapes, col_sharded, ks):
    def plan(bufs, send_sems, recv_sems):
        x, y, c = _mesh_pos()
        sends, arrivals = [], []
        for ref, k in zip(bufs, ks):
            mine = _half_block(ref, shard_shapes[k], col_sharded[k], 2 * x + y, c)
            for j, (px, py) in enumerate(_other_chips(x, y)):
                landed = _half_block(ref, shard_shapes[k], col_sharded[k], 2 * px + py, c)
                sends.append(_remote(mine, mine, send_sems, recv_sems, 3 * k + j, (px, py, c)))
                arrivals.append(_remote(landed, landed, send_sems, recv_sems, 3 * k + j, (px, py, c)))
        return sends, arrivals
    return plan


def _forward_plan(shard_shapes, col_sharded, ks):
    def plan(bufs, send_sems, recv_sems):
        x, y, c = _mesh_pos()
        sends, arrivals = [], []
        for i, (ref, k) in enumerate(zip(bufs, ks)):
            for j, (px, py) in enumerate(_other_chips(x, y)):
                landed = _half_block(ref, shard_shapes[k], col_sharded[k], 2 * px + py, c)
                theirs = _half_block(ref, shard_shapes[k], col_sharded[k], 2 * px + py, 1 - c)
                sends.append(_remote(landed, landed, send_sems, recv_sems, 3 * i + j, (x, y, 1 - c)))
                arrivals.append(_remote(theirs, theirs, send_sems, recv_sems, 3 * i + j, (x, y, 1 - c)))
        return sends, arrivals
    return plan


def _small_gather(small):
    def body(small_ref, out_ref, send_sems, recv_sems):
        x, y, c = _mesh_pos()
        me = 2 * x + y
        out_ref[me] = small_ref[...]
        copies = []
        for j, (px, py) in enumerate(_other_chips(x, y)):
            cp = _remote(small_ref, out_ref.at[me], send_sems, recv_sems, j, (px, py, c))
            cp.start()
            copies.append(cp)
        for j, (px, py) in enumerate(_other_chips(x, y)):
            _remote(small_ref, out_ref.at[2 * px + py], send_sems, recv_sems, j, (px, py, c)).wait_recv()
        for cp in copies:
            cp.wait_send()

    return pl.pallas_call(
        body,
        out_shape=jax.ShapeDtypeStruct((N_CHIPS,) + small.shape, small.dtype),
        in_specs=[VMEM_SPEC],
        out_specs=VMEM_SPEC,
        scratch_shapes=[pltpu.SemaphoreType.DMA((N_CHIPS - 1,)), pltpu.SemaphoreType.DMA((N_CHIPS - 1,))],
        name="gather_small",
    )(small)


def _as3d(g, col_sharded):
    r, cdim = g.shape
    return g.reshape(1, r, cdim) if col_sharded else g.reshape(N_CHIPS, r // N_CHIPS, cdim)


def _pair_plan(m):
    def plan(bufs, send_sems, recv_sems):
        x, y, c = _mesh_pos()
        copies = []
        for i in range(m):
            h = bufs[i].shape[1] // 2
            src = bufs[i].at[:, pl.ds(pl.multiple_of((1 - c) * h, SUBLANE), h), :]
            copies.append(_remote(src, bufs[m + i], send_sems, recv_sems, i, (x, y, 1 - c)))
        return copies, copies
    return plan


def _chip_plan(col_flags):
    m = len(col_flags)

    def plan(bufs, send_sems, recv_sems):
        x, y, c = _mesh_pos()
        copies = []
        for i in range(m):
            land = bufs[m + i]
            width = land.shape[2]
            for j, (px, py) in enumerate(_other_chips(x, y)):
                q = 2 * px + py
                if col_flags[i]:
                    src = bufs[i].at[0, :, pl.ds(pl.multiple_of(q * width, LANE), width)]
                else:
                    src = bufs[i].at[q]
                copies.append(_remote(src, land.at[j], send_sems, recv_sems, 3 * i + j, (px, py, c)))
        return copies, copies
    return plan


def _share_plan(m):
    def plan(bufs, send_sems, recv_sems):
        x, y, c = _mesh_pos()
        sends, arrivals = [], []
        for i in range(m):
            h = bufs[i].shape[0] // 2
            mine = bufs[i].at[pl.ds(pl.multiple_of(c * h, SUBLANE), h), :]
            theirs = bufs[i].at[pl.ds(pl.multiple_of((1 - c) * h, SUBLANE), h), :]
            sends.append(_remote(mine, mine, send_sems, recv_sems, i, (x, y, 1 - c)))
            arrivals.append(_remote(theirs, theirs, send_sems, recv_sems, i, (x, y, 1 - c)))
        return sends, arrivals
    return plan


def _pair_add(g3, other, core):
    a, r, cdim = g3.shape
    h = r // 2
    rb = _pick(h, max(BF16_ROWS, (512 * 1024) // cdim), BF16_ROWS)
    nb = h // rb

    def body(core_ref, g_ref, o_ref, out_ref):
        out_ref[...] = (g_ref[...] + o_ref[...]).astype(BF16)

    grid_spec = pltpu.PrefetchScalarGridSpec(
        num_scalar_prefetch=1,
        grid=(a, nb),
        in_specs=[pl.BlockSpec((None, rb, cdim), lambda i, j, core_ref: (i, core_ref[0] * nb + j, 0)),
                  pl.BlockSpec((None, rb, cdim), lambda i, j, core_ref: (i, j, 0))],
        out_specs=pl.BlockSpec((None, rb, cdim), lambda i, j, core_ref: (i, j, 0)),
    )
    return pl.pallas_call(
        body,
        out_shape=jax.ShapeDtypeStruct((a, h, cdim), BF16),
        grid_spec=grid_spec,
        compiler_params=_cparams(("parallel", "parallel"), 2 * rb * cdim * 10),
        name="grad_pair_add",
    )(core, g3, other)


def _small_allreduce(small):
    def body(small_ref, out_ref, slots, send_sems, recv_sems):
        x, y, c = _mesh_pos()
        my_dev = 4 * x + 2 * y + c
        slots[my_dev] = small_ref[...]
        copies = []
        for m in range(1, N_DEV):
            peer = (x ^ ((m >> 2) & 1), y ^ ((m >> 1) & 1), c ^ (m & 1))
            cp = _remote(small_ref, slots.at[my_dev], send_sems, recv_sems, m - 1, peer)
            cp.start()
            copies.append(cp)
        for m in range(1, N_DEV):
            _remote(small_ref, slots.at[my_dev ^ m], send_sems, recv_sems, m - 1, (x, y, c)).wait_recv()
        total = slots[0]
        for d in range(1, N_DEV):
            total = total + slots[d]
        out_ref[...] = total
        for cp in copies:
            cp.wait_send()

    return pl.pallas_call(
        body,
        out_shape=jax.ShapeDtypeStruct(small.shape, F32),
        in_specs=[VMEM_SPEC],
        out_specs=VMEM_SPEC,
        scratch_shapes=[pltpu.VMEM((N_DEV,) + small.shape, F32),
                        pltpu.SemaphoreType.DMA((N_DEV - 1,)), pltpu.SemaphoreType.DMA((N_DEV - 1,))],
        compiler_params=pltpu.CompilerParams(
            vmem_limit_bytes=min(VMEM_BUDGET, (N_DEV + 4) * _nbytes(small.shape, F32) + (8 << 20))),
        name="grad_small_allreduce",
    )(small)


def _chip_sum(partial, land, where, col_sharded):
    _, h, cdim = land.shape
    rb = _pick(h, max(BF16_ROWS, (512 * 1024) // cdim), BF16_ROWS)
    nb = h // rb

    def body(where_ref, own_ref, l_ref, o_ref):
        total = own_ref[...].astype(F32)
        for j in range(N_CHIPS - 1):
            total = total + l_ref[j].astype(F32)
        o_ref[...] = total

    if col_sharded:
        own_map = lambda i, w: (0, i, w[0])
    else:
        own_map = lambda i, w: (w[0], i, 0)
    grid_spec = pltpu.PrefetchScalarGridSpec(
        num_scalar_prefetch=1,
        grid=(nb,),
        in_specs=[pl.BlockSpec((None, rb, cdim), own_map),
                  pl.BlockSpec((N_CHIPS - 1, rb, cdim), lambda i, w: (0, i, 0))],
        out_specs=pl.BlockSpec((rb, cdim), lambda i, w: (w[1] * nb + i, 0)),
    )
    return pl.pallas_call(
        body,
        out_shape=jax.ShapeDtypeStruct((2 * h, cdim), F32),
        grid_spec=grid_spec,
        compiler_params=_cparams(("parallel",), 2 * rb * cdim * 12),
        name="grad_chip_sum",
    )(where, partial, land)


def _adamw(w, g, m, v, name, dep=None):
    r, cdim = w.shape
    rb = _pick(r, max(SUBLANE, (256 * 1024) // cdim), SUBLANE)
    c1 = 1.0 - ADAM_B1 ** ADAM_STEP
    c2 = 1.0 - ADAM_B2 ** ADAM_STEP

    def body(w_ref, g_ref, m_ref, v_ref, go_ref, d_ref, mo_ref, vo_ref):
        gv = g_ref[...]
        mn = ADAM_B1 * m_ref[...] + (1.0 - ADAM_B1) * gv
        vn = ADAM_B2 * v_ref[...] + (1.0 - ADAM_B2) * (gv * gv)
        m_hat = mn / c1
        v_hat = vn / c2
        d_ref[...] = -ADAM_LR * (m_hat / (jnp.sqrt(v_hat) + ADAM_EPS) + ADAM_WD * w_ref[...])
        go_ref[...] = gv
        mo_ref[...] = mn
        vo_ref[...] = vn

    blk = pl.BlockSpec((rb, cdim), lambda i: (i, 0))
    shape = jax.ShapeDtypeStruct((r, cdim), F32)
    body, in_specs, operands = _dep_args(body, [blk] * 4, [w, g, m, v], dep)
    return pl.pallas_call(
        body,
        out_shape=[shape] * 4,
        grid=(r // rb,),
        in_specs=in_specs,
        out_specs=[blk] * 4,
        compiler_params=_cparams(("parallel",), 2 * rb * cdim * 4 * 8),
        name=name,
    )(*operands)


def _pack(arrays):
    tile = SUBLANE * LANE
    pieces = []
    for arr in arrays:
        flat = arr.reshape(-1)
        pad = (-flat.shape[0]) % tile
        if pad:
            flat = jnp.concatenate([flat, jnp.zeros((pad,), flat.dtype)])
        pieces.append(flat)
    return jnp.concatenate(pieces).reshape(-1, LANE)


def _unpack(packed, shapes):
    tile = SUBLANE * LANE
    flat = packed.reshape(-1)
    out, off = [], 0
    for shp in shapes:
        size = math.prod(shp)
        out.append(flat[off:off + size].reshape(shp))
        off += size + ((-size) % tile)
    return out


def _block_diag_groups(w, per_group):
    hcount, hd, _ = w.shape
    ng = hcount // per_group
    w4 = w.reshape(ng, per_group, hd, hd)
    eye = jnp.eye(per_group, dtype=w.dtype)
    bd = w4[:, :, :, None, :] * eye[None, :, None, :, None]
    return bd.reshape(ng, per_group * hd, per_group * hd).astype(BF16)


def _diag_blocks(wbd, per_group, hd):
    ng = wbd.shape[0]
    w5 = wbd.reshape(ng, per_group, hd, per_group, hd)
    blocks = [w5[:, i, :, i, :] for i in range(per_group)]
    return jnp.stack(blocks, axis=1).reshape(ng * per_group, hd, hd)


def kernel(x, g_mix, w_in, lru_conv_w, lru_conv_b, lru_wa, lru_ba, lru_wx, lru_bx, lru_lambda, lru_w_out, sc_conv_w, sc_w_out, w_o, g_ffn, ffn_w_up, ffn_conv_w, ffn_w_down, g_final, loss_target, m_g_mix, m_w_in, m_lru_conv_w, m_lru_conv_b, m_lru_wa, m_lru_ba, m_lru_wx, m_lru_bx, m_lru_lambda, m_lru_w_out, m_sc_conv_w, m_sc_w_out, m_w_o, m_g_ffn, m_ffn_w_up, m_ffn_conv_w, m_ffn_w_down, m_g_final, v_g_mix, v_w_in, v_lru_conv_w, v_lru_conv_b, v_lru_wa, v_lru_ba, v_lru_wx, v_lru_bx, v_lru_lambda, v_lru_w_out, v_sc_conv_w, v_sc_w_out, v_w_o, v_g_ffn, v_ffn_w_up, v_ffn_conv_w, v_ffn_w_down, v_g_final):
    seq, d_model = x.shape[1], x.shape[2]
    heads, head_dim, _ = lru_wa.shape
    d_lru = heads * head_dim
    d_sc = sc_w_out.shape[0]
    d_ff = ffn_w_down.shape[0] * N_CHIPS
    assert x.shape[0] == 1 and w_in.shape[1] * N_CHIPS == 2 * d_lru + 3 * d_sc + 2 * d_model
    xs = x.reshape(seq, d_model)
    target = loss_target.reshape(seq, d_model)

    chip = 2 * lax.axis_index("x") + lax.axis_index("y")
    core = lax.axis_index("c").astype(jnp.int32).reshape(1)

    big_w = [w_in, lru_w_out, sc_w_out, w_o, ffn_w_up, ffn_w_down]
    big_m = [m_w_in, m_lru_w_out, m_sc_w_out, m_w_o, m_ffn_w_up, m_ffn_w_down]
    big_v = [v_w_in, v_lru_w_out, v_sc_w_out, v_w_o, v_ffn_w_up, v_ffn_w_down]
    col_sharded = [True, True, True, False, True, False]
    conv_shards = [lru_conv_w, sc_conv_w, ffn_conv_w]
    conv_pack = jnp.concatenate(
        [jnp.pad(w, ((0, SUBLANE - w.shape[0]), (0, 0))) for w in conv_shards], axis=1)
    big_names = ["w_in", "lru_w_out", "sc_w_out", "w_o", "ffn_w_up", "ffn_w_down"]
    chip_arr = chip.astype(jnp.int32).reshape(1)
    placed = [_cast_place(w, chip_arr, cs, "cast_" + nm) for w, cs, nm in zip(big_w, col_sharded, big_names)]
    conv_all = _small_gather(conv_pack)
    shard_shapes = [w.shape for w in big_w]
    n_big = len(big_w)

    def gather_start(ks, after, tag):
        send, recv, bufs, token = _exchange_start(
            "gather_start_" + tag, [placed[k] for k in ks], 3 * n_big,
            _gather_plan(shard_shapes, col_sharded, ks), after=after)
        return (send, recv, dict(zip(ks, bufs))), token

    def arrived(state, ks, after, tag):
        send, recv, bufs = state
        got = _exchange_wait("gather_wait_" + tag, [bufs[k] for k in ks], send, recv, after,
                             _gather_plan(shard_shapes, col_sharded, ks))
        return _exchange("gather_forward_" + tag, got, 3 * len(ks), _forward_plan(shard_shapes, col_sharded, ks))

    conv_full, off = [], 0
    for w in conv_shards:
        kw, nq = w.shape
        piece = conv_all[:, :kw, off:off + nq]
        conv_full.append(piece.transpose(1, 0, 2).reshape(kw, N_CHIPS * nq))
        off += nq
    lcw, scw, fcw = conv_full

    per_group = max(1, min(heads, 256 // head_dim))
    gc = per_group * head_dim
    wa_bd = _block_diag_groups(lru_wa, per_group)
    wx_bd = _block_diag_groups(lru_wx, per_group)
    tc = _pick(seq, 256, SUBLANE)
    cb_sc = _pick(d_sc, 512)
    cb_ff = _pick(d_ff, 512)
    col_sc = 2 * d_lru
    col_gates = 2 * d_lru + 3 * d_sc

    first, token = gather_start([0], conv_all, "in")
    h1 = _rms_fwd(xs, g_mix, "rms_mix", dep=token)
    (win_b,) = arrived(first, [0], h1, "in")
    rest, token = gather_start([1, 2, 3, 4, 5], win_b, "rest")
    p = _mm(h1, win_b, "nn", F32, name="mm_in", dep=token)
    wlo_b, wso_b, wo_b = arrived(rest, [1, 2, 3], p, "mix")
    y_lru_pre, hseq = _lru_fwd(p, lcw, lru_conv_b, wa_bd, lru_ba, wx_bd, lru_bx, lru_lambda, d_lru, gc, tc)
    y_sc_pre = _sc_fwd(p, scw, col_sc, d_sc, cb_sc, tc)
    y_lru = _mm(y_lru_pre, wlo_b, "nn", F32, name="mm_lru_out")
    y_sc = _mm(y_sc_pre, wso_b, "nn", F32, name="mm_sc_out")
    merged = _merge_fwd(p, y_lru, y_sc, col_gates, tc)
    x2 = _mm(merged, wo_b, "nn", F32, res=xs, name="mm_o")
    (wup_b,) = arrived(rest, [4], x2, "up")
    h2 = _rms_fwd(x2, g_ffn, "rms_ffn")
    up = _mm(h2, wup_b, "nn", F32, name="mm_up")
    (wdn_b,) = arrived(rest, [5], up, "down")
    act = _ffn_act_fwd(up, fcw, d_ff, cb_ff, tc)
    x3 = _mm(act, wdn_b, "nn", F32, res=x2, name="mm_down")
    loss_part, dx3, dx3b, dg_final = _loss_head(x3, g_final, target)

    where = jnp.concatenate([chip_arr, core])

    def reduce_start(grads, flags, tag):
        views = [_as3d(g, cs) for g, cs in zip(grads, flags)]
        lands = [lax.empty((v.shape[0], v.shape[1] // 2, v.shape[2]), F32) for v in views]
        send, recv, bufs, token = _exchange_start("grad_pair_start_" + tag, views + lands, len(views),
                                                  _pair_plan(len(views)))
        return (send, recv, bufs, flags, tag), token

    def reduce_mid(state, after):
        send, recv, bufs, flags, tag = state
        m = len(flags)
        bufs = _exchange_wait("grad_pair_wait_" + tag, bufs, send, recv, after, _pair_plan(m))
        partials = [_pair_add(bufs[i], bufs[m + i], core) for i in range(m)]
        lands = []
        for pz, cs in zip(partials, flags):
            _, h, cdim = pz.shape
            lands.append(lax.empty((N_CHIPS - 1, h, cdim // N_CHIPS if cs else cdim), BF16))
        send, recv, bufs, token = _exchange_start("grad_chip_start_" + tag, partials + lands, 3 * m,
                                                  _chip_plan(flags))
        return (send, recv, bufs, flags, tag), token

    def reduce_end(state, after):
        send, recv, bufs, flags, tag = state
        m = len(flags)
        bufs = _exchange_wait("grad_chip_wait_" + tag, bufs, send, recv, after, _chip_plan(flags))
        return [_chip_sum(bufs[i], bufs[m + i], where, flags[i]) for i in range(m)]

    g_wdn = _mm(act, dx3b, "tn", F32, name="mm_down_dw")
    red_down, token = reduce_start([g_wdn], [False], "down")
    dact = _mm(dx3b, wdn_b, "nt", F32, name="mm_down_dx", dep=token)
    red_down, token = reduce_mid(red_down, dact)
    dupg, dupv, dfcw_g, dfcw_v = _ffn_act_bwd(up, dact, fcw, d_ff, cb_ff, tc, dep=token)
    dup = jnp.concatenate([dupg, dupv], axis=1)
    g_wup = _mm(h2, dup, "tn", F32, name="mm_up_dw")
    red_up, token = reduce_start([g_wup], [True], "up")
    dh2 = _mm(dup, wup_b, "nt", F32, name="mm_up_dx", dep=token)
    red_up, token = reduce_mid(red_up, dh2)
    dx2, dx2b, dg_ffn = _rms_bwd(x2, g_ffn, dh2, dx3, "rms_ffn_bwd", True, dep=token)
    g_wo = _mm(merged, dx2b, "tn", F32, name="mm_o_dw")
    dmerged = _mm(dx2b, wo_b, "nt", F32, name="mm_o_dx")
    dgl, dgs, dyl, dys = _merge_bwd(p, y_lru, y_sc, dmerged, col_gates, tc)
    g_wlo = _mm(y_lru_pre, dyl, "tn", F32, name="mm_lru_out_dw")
    g_wso = _mm(y_sc_pre, dys, "tn", F32, name="mm_sc_out_dw")
    red_mix, token = reduce_start([g_wlo, g_wso, g_wo], [True, True, False], "mix")
    dylp = _mm(dyl, wlo_b, "nt", F32, name="mm_lru_out_dx", dep=token)
    dysp = _mm(dys, wso_b, "nt", F32, name="mm_sc_out_dx")
    red_mix, token = reduce_mid(red_mix, dysp)
    dlx, dlgate, dlcw, dlcb, dwa_bd, dba, dwx_bd, dbx, dlam = _lru_bwd(
        p, hseq, dylp, lcw, lru_conv_b, wa_bd, lru_ba, wx_bd, lru_bx, lru_lambda, d_lru, gc, tc, dep=token)
    dsb, dsc, dsv, dscw = _sc_bwd(p, dysp, scw, col_sc, d_sc, cb_sc, tc)
    dp = jnp.concatenate([dlx, dlgate, dsb, dsc, dsv, dgl, dgs], axis=1)
    g_win = _mm(h1, dp, "tn", F32, name="mm_in_dw")
    red_in, token = reduce_start([g_win], [True], "in")
    dh1 = _mm(dp, win_b, "nt", F32, name="mm_in_dx", dep=token)
    grad_x, dg_mix = _rms_bwd(xs, g_mix, dh1, dx2, "rms_mix_bwd", False)

    small_g = [dg_mix, dlcw, dlcb, _diag_blocks(dwa_bd, per_group, head_dim), dba,
               _diag_blocks(dwx_bd, per_group, head_dim), dbx, dlam, dscw, dg_ffn,
               jnp.concatenate([dfcw_g, dfcw_v], axis=1), dg_final]
    small_shapes = [a.shape for a in small_g]
    small_sum = _small_allreduce(_pack(small_g))
    red_in, token = reduce_mid(red_in, small_sum)
    (h_wdn,) = reduce_end(red_down, token)
    (h_wup,) = reduce_end(red_up, token)
    h_wlo, h_wso, h_wo = reduce_end(red_mix, token)
    s_wlo, s_wso, s_wo, s_wup, s_wdn = _exchange("grad_share_a", [h_wlo, h_wso, h_wo, h_wup, h_wdn], 5,
                                                 _share_plan(5))
    early = {1: s_wlo, 2: s_wso, 3: s_wo, 4: s_wup, 5: s_wdn}
    big_out = [None] * n_big
    last = None
    for k, g in early.items():
        big_out[k] = _adamw(big_w[k], g, big_m[k], big_v[k], "adamw_" + big_names[k], dep=last)
        last = big_out[k][1]
    (h_win,) = reduce_end(red_in, last)
    (s_win,) = _exchange("grad_share_b", [h_win], 1, _share_plan(1))
    big_out[0] = _adamw(big_w[0], s_win, big_m[0], big_v[0], "adamw_" + big_names[0])
    sg = _unpack(small_sum, small_shapes)
    for idx in (1, 8, 10):
        nq = sg[idx].shape[1] // N_CHIPS
        sg[idx] = lax.dynamic_slice_in_dim(sg[idx], chip * nq, nq, axis=1)
    small_w = [g_mix, lru_conv_w, lru_conv_b, lru_wa, lru_ba, lru_wx, lru_bx, lru_lambda, sc_conv_w,
               g_ffn, ffn_conv_w, g_final]
    small_m = [m_g_mix, m_lru_conv_w, m_lru_conv_b, m_lru_wa, m_lru_ba, m_lru_wx, m_lru_bx, m_lru_lambda,
               m_sc_conv_w, m_g_ffn, m_ffn_conv_w, m_g_final]
    small_v = [v_g_mix, v_lru_conv_w, v_lru_conv_b, v_lru_wa, v_lru_ba, v_lru_wx, v_lru_bx, v_lru_lambda,
               v_sc_conv_w, v_g_ffn, v_ffn_conv_w, v_g_final]
    sg = [g.reshape(w.shape) for g, w in zip(sg, small_w)]
    w_shapes = [w.shape for w in small_w]
    packed = _adamw(_pack(small_w), _pack(sg), _pack(small_m), _pack(small_v), "adamw_small")
    small_out = [_unpack(pk, w_shapes) for pk in packed]

    order = [(0, 0), (1, 0), (0, 1), (0, 2), (0, 3), (0, 4), (0, 5), (0, 6), (0, 7), (1, 1), (0, 8), (1, 2),
             (1, 3), (0, 9), (1, 4), (0, 10), (1, 5), (0, 11)]
    by_kind = []
    for kind in range(4):
        by_kind.append([big_out[i][kind] if is_big else small_out[kind][i] for is_big, i in order])
    loss = lax.psum(loss_part[0, 0], ("x", "y", "c"))
    return (loss, grad_x.reshape(x.shape), *by_kind[0], *by_kind[1], *by_kind[2], *by_kind[3])
```

```python
import functools
import math

import jax
import jax.numpy as jnp
from jax import lax
from jax.experimental import pallas as pl
from jax.experimental.pallas import tpu as pltpu

F32 = jnp.float32
BF16 = jnp.bfloat16

LANE = 128
SUBLANE = 8
BF16_ROWS = 16
VMEM_BYTES_V7X = 64 * 1024 * 1024
VMEM_BUDGET = VMEM_BYTES_V7X - 8 * 1024 * 1024

EPS = 1e-6
LRU_C = 8.0
ADAM_LR = 0.001
ADAM_B1 = 0.9
ADAM_B2 = 0.999
ADAM_EPS = 1e-08
ADAM_WD = 0.01
ADAM_STEP = 10

N_CHIPS = 4
N_DEV = 8
MESH = pl.DeviceIdType.MESH
ANY = pl.BlockSpec(memory_space=pl.ANY)
VMEM_SPEC = pl.BlockSpec(memory_space=pltpu.VMEM)
HBM_SPEC = pl.BlockSpec(memory_space=pltpu.HBM)
SEM_SPEC = pl.BlockSpec(memory_space=pltpu.SEMAPHORE)
DATAFLOW_EFFECT = pltpu.SideEffectType.DATAFLOW_SIDE_EFFECTING


def _pick(n, cap, mult=LANE):
    best = None
    d = mult
    while d <= min(n, cap):
        if n % d == 0:
            best = d
        d += mult
    return n if best is None else best


def _cparams(semantics, block_bytes):
    limit = min(VMEM_BUDGET, max(32 * 1024 * 1024, int(block_bytes * 1.25) + (4 << 20)))
    return pltpu.CompilerParams(dimension_semantics=semantics, vmem_limit_bytes=limit)


def _nbytes(shape, dtype):
    return math.prod(shape) * jnp.dtype(dtype).itemsize


def _sigmoid(z):
    return 1.0 / (1.0 + jnp.exp(-z))


def _softplus(z):
    e = jnp.exp(-jnp.abs(z))
    u = 1.0 + e
    log1p = jnp.where(u == 1.0, e, jnp.log(u) * (e / (u - 1.0)))
    return jnp.maximum(z, 0.0) + log1p


def _neg_expm1(z):
    small = z * (1.0 + z * (0.5 + z * (1.0 / 6.0 + z * (1.0 / 24.0))))
    return -jnp.where(jnp.abs(z) < 0.03, small, jnp.exp(z) - 1.0)


_GELU_K = math.sqrt(2.0 / math.pi)
_GELU_C = 0.044715


def _gelu_and_grad(z):
    z2 = z * z
    th = jnp.tanh(_GELU_K * (z + _GELU_C * z2 * z))
    val = 0.5 * z * (1.0 + th)
    grad = 0.5 * (1.0 + th) + 0.5 * z * (1.0 - th * th) * (_GELU_K * (1.0 + 3.0 * _GELU_C * z2))
    return val, grad


def _rows_before(cat, k):
    if k == 0:
        return cat[SUBLANE:, :]
    return pltpu.roll(cat, k, 0)[SUBLANE:, :]


def _rows_after(cat, k):
    n = cat.shape[0]
    if k == 0:
        return cat[:n - SUBLANE, :]
    return pltpu.roll(cat, n - k, 0)[:n - SUBLANE, :]


def _conv_fwd(cat, w, width):
    y = _rows_before(cat, width - 1) * w[0:1, :]
    for k in range(1, width):
        y = y + _rows_before(cat, width - 1 - k) * w[k:k + 1, :]
    return y


def _conv_bwd_input(cat, w, width):
    dx = _rows_after(cat, width - 1) * w[0:1, :]
    for k in range(1, width):
        dx = dx + _rows_after(cat, width - 1 - k) * w[k:k + 1, :]
    return dx


def _conv_bwd_weight(dw_ref, dy, catx, width):
    for k in range(width):
        dw_ref[k:k + 1, :] += jnp.sum(dy * _rows_before(catx, width - 1 - k), axis=0, keepdims=True)


def _scan_tiles(a_ref, b_ref, out_ref, carry0, n_rows, reverse):
    cols = a_ref.shape[1]
    row = lax.broadcasted_iota(jnp.int32, (SUBLANE, cols), 0)
    n_tiles = n_rows // SUBLANE

    def step(j, carry):
        tile = (n_tiles - 1 - j) if reverse else j
        off = pl.multiple_of(tile * SUBLANE, SUBLANE)
        a = a_ref[pl.ds(off, SUBLANE), :]
        b = b_ref[pl.ds(off, SUBLANE), :]
        for s in (1, 2, 4):
            if reverse:
                keep = row < SUBLANE - s
                shift = SUBLANE - s
            else:
                keep = row >= s
                shift = s
            a_sh = jnp.where(keep, pltpu.roll(a, shift, 0), 1.0)
            b_sh = jnp.where(keep, pltpu.roll(b, shift, 0), 0.0)
            b = a * b_sh + b
            a = a * a_sh
        out = a * carry + b
        out_ref[pl.ds(off, SUBLANE), :] = out
        return out[0:1, :] if reverse else out[SUBLANE - 1:SUBLANE, :]

    return lax.fori_loop(0, n_tiles, step, carry0)


def _out(shape, dtype):
    if dtype == BF16:
        return pltpu.HBM(shape, dtype)
    return jax.ShapeDtypeStruct(shape, dtype)


def _hbm(x):
    return pltpu.with_memory_space_constraint(x, pltpu.HBM)


def _dep_args(body, in_specs, operands, *deps):
    deps = [d for d in deps if d is not None]
    if not deps:
        return body, in_specs, operands
    n = len(operands)

    def wrapped(*refs):
        return body(*refs[:n], *refs[n + len(deps):])

    return wrapped, list(in_specs) + [ANY] * len(deps), list(operands) + deps


def _mm(a, b, mode, out_dtype, res=None, name=None, dep=None):
    a_pieces = list(a) if isinstance(a, (list, tuple)) else [a]
    b_pieces = list(b) if isinstance(b, (list, tuple)) else [b]
    assert len(a_pieces) == 1 or (mode == "nt" and len(b_pieces) == 1)
    assert len(b_pieces) == 1 or (mode == "tn" and len(a_pieces) == 1)
    assert all(t.dtype == BF16 for t in a_pieces + b_pieces)
    if mode == "nn":
        (m, k), (k2, n) = a_pieces[0].shape, b_pieces[0].shape
        dims = (((1,), (0,)), ((), ()))
    elif mode == "nt":
        m, k = a_pieces[0].shape[0], sum(t.shape[1] for t in a_pieces)
        n, k2 = b_pieces[0].shape
        dims = (((1,), (1,)), ((), ()))
    else:
        k, m = a_pieces[0].shape
        k2, n = b_pieces[0].shape[0], sum(t.shape[1] for t in b_pieces)
        dims = (((0,), (0,)), ((), ()))
    assert k == k2
    k_unit = math.gcd(*[t.shape[1] for t in a_pieces]) if len(a_pieces) > 1 else k
    n_unit = math.gcd(*[t.shape[1] for t in b_pieces]) if len(b_pieces) > 1 else n
    bm = _pick(m, 1024)
    bn = _pick(n_unit, 1024 if len(b_pieces) == 1 else 1536)
    bk = _pick(k_unit, 2048)
    nk = k // bk
    out_bytes = jnp.dtype(out_dtype).itemsize

    def est(bk_):
        e = 2 * (len(a_pieces) * bm * bk_ + len(b_pieces) * bk_ * bn) * 2 + 2 * bm * bn * out_bytes
        if nk > 1 or k // bk_ > 1:
            e += bm * bn * 4
        if res is not None:
            e += 2 * bm * bn * 4
        return e

    while est(bk) > 36 * 1024 * 1024 and bk % (2 * LANE) == 0 and k % (bk // 2) == 0:
        bk //= 2
        nk = k // bk

    a_ranges, b_ranges, off = [], [], 0
    for t in a_pieces:
        cnt = (t.shape[0] if mode == "tn" else t.shape[1]) // bk
        a_ranges.append((off, cnt))
        off += cnt
    off = 0
    for t in b_pieces:
        cnt = (t.shape[0] if mode == "nt" else t.shape[1]) // bn
        b_ranges.append((off, cnt))
        off += cnt

    def walk(pos, rng, pieces):
        return pos if pieces == 1 else jnp.clip(pos - rng[0], 0, rng[1] - 1)

    in_specs = []
    for rng in a_ranges:
        if mode == "tn":
            in_specs.append(pl.BlockSpec((bk, bm), lambda i, j, kk: (kk, i)))
        else:
            in_specs.append(pl.BlockSpec(
                (bm, bk), lambda i, j, kk, rng=rng: (i, walk(kk, rng, len(a_pieces)))))
    for rng in b_ranges:
        if mode == "nt":
            in_specs.append(pl.BlockSpec((bn, bk), lambda i, j, kk: (j, kk)))
        else:
            def b_map(i, j, kk, rng=rng):
                if len(b_pieces) == 1:
                    return kk, j
                active = (j >= rng[0]) & (j < rng[0] + rng[1])
                return jnp.where(active, kk, 0), walk(j, rng, len(b_pieces))
            in_specs.append(pl.BlockSpec((bk, bn), b_map))
    o_spec = pl.BlockSpec((bm, bn), lambda i, j, kk: (i, j))
    operands = [_hbm(t) for t in a_pieces + b_pieces]
    if res is not None:
        in_specs.append(o_spec)
        operands.append(_hbm(res))
    has_res = res is not None
    na, nb = len(a_pieces), len(b_pieces)

    def body(*refs):
        a_refs, b_refs = refs[:na], refs[na:na + nb]
        res_ref = refs[na + nb] if has_res else None
        o_ref = refs[na + nb + has_res]
        acc_ref = refs[-1] if nk > 1 else None
        kk = pl.program_id(2)

        def accumulate(part, first_possible=True):
            if nk == 1:
                if has_res:
                    part = part + res_ref[...]
                o_ref[...] = part.astype(o_ref.dtype)
                return
            if first_possible:
                @pl.when(kk == 0)
                def _():
                    acc_ref[...] = part

                @pl.when(kk > 0)
                def _():
                    acc_ref[...] += part
            else:
                acc_ref[...] += part

        def dot(a_ref, b_ref):
            return lax.dot_general(a_ref[...], b_ref[...], dims, preferred_element_type=F32)

        if na > 1:
            for a_ref, (start, cnt) in zip(a_refs, a_ranges):
                @pl.when((kk >= start) & (kk < start + cnt))
                def _(a_ref=a_ref, start=start):
                    accumulate(dot(a_ref, b_refs[0]), first_possible=start == 0)
        elif nb > 1:
            j = pl.program_id(1)
            for b_ref, (start, cnt) in zip(b_refs, b_ranges):
                @pl.when((j >= start) & (j < start + cnt))
                def _(b_ref=b_ref):
                    accumulate(dot(a_refs[0], b_ref))
        else:
            accumulate(dot(a_refs[0], b_refs[0]))

        if nk > 1:
            @pl.when(kk == nk - 1)
            def _():
                total = acc_ref[...]
                if has_res:
                    total = total + res_ref[...]
                o_ref[...] = total.astype(o_ref.dtype)

    scratch = [pltpu.VMEM((bm, bn), F32)] if nk > 1 else []
    body, in_specs, operands = _dep_args(body, in_specs, operands, dep)
    return pl.pallas_call(
        body,
        out_shape=_out((m, n), out_dtype),
        grid=(m // bm, n // bn, nk),
        in_specs=in_specs,
        out_specs=o_spec,
        scratch_shapes=scratch,
        compiler_params=_cparams(("parallel", "parallel", "arbitrary"), est(bk)),
        name=name,
    )(*operands)


def _rms_fwd(x, g, name, dep=None):
    t, d = x.shape
    tb = _pick(t, 512, SUBLANE)

    def body(x_ref, g_ref, h_ref):
        xv = x_ref[...]
        r = lax.rsqrt(jnp.mean(xv * xv, axis=-1, keepdims=True) + EPS)
        h_ref[...] = ((xv * r) * g_ref[...]).astype(BF16)

    blk = pl.BlockSpec((tb, d), lambda i: (i, 0))
    body, in_specs, operands = _dep_args(
        body, [blk, pl.BlockSpec((1, d), lambda i: (0, 0))], [_hbm(x), g.reshape(1, d)], dep)
    return pl.pallas_call(
        body,
        out_shape=_out((t, d), BF16),
        grid=(t // tb,),
        in_specs=in_specs,
        out_specs=blk,
        compiler_params=_cparams(("parallel",), 2 * tb * d * 6),
        name=name,
    )(*operands)


def _rms_bwd(x, g, dh, dres, name, want_bf16, dep=None):
    t, d = x.shape
    tb = _pick(t, 256, SUBLANE)

    def body(x_ref, g_ref, dh_ref, dres_ref, *outs):
        dx_ref, dg_ref = outs[0], outs[-1]
        xv = x_ref[...]
        r = lax.rsqrt(jnp.mean(xv * xv, axis=-1, keepdims=True) + EPS)
        xhat = xv * r
        dhv = dh_ref[...]
        dxhat = dhv * g_ref[...]
        dx = dres_ref[...] + r * (dxhat - xhat * jnp.mean(dxhat * xhat, axis=-1, keepdims=True))
        dx_ref[...] = dx
        if want_bf16:
            outs[1][...] = dx.astype(BF16)

        @pl.when(pl.program_id(0) == 0)
        def _():
            dg_ref[...] = jnp.zeros_like(dg_ref)

        dg_ref[...] += jnp.sum(dhv * xhat, axis=0, keepdims=True)

    blk = pl.BlockSpec((tb, d), lambda i: (i, 0))
    row = pl.BlockSpec((1, d), lambda i: (0, 0))
    out_shape = [_out((t, d), F32)]
    out_specs = [blk]
    if want_bf16:
        out_shape.append(_out((t, d), BF16))
        out_specs.append(blk)
    out_shape.append(jax.ShapeDtypeStruct((1, d), F32))
    out_specs.append(row)
    body, in_specs, operands = _dep_args(
        body, [blk, row, blk, blk], [_hbm(x), g.reshape(1, d), _hbm(dh), _hbm(dres)], dep)
    return pl.pallas_call(
        body,
        out_shape=out_shape,
        grid=(t // tb,),
        in_specs=in_specs,
        out_specs=out_specs,
        compiler_params=_cparams(("arbitrary",), 2 * tb * d * 18),
        name=name,
    )(*operands)


def _loss_head(x3, g, target):
    t, d = x3.shape
    tb = _pick(t, 256, SUBLANE)

    def body(x_ref, g_ref, t_ref, loss_ref, dx_ref, dxb_ref, dg_ref):
        xv = x_ref[...]
        gv = g_ref[...]
        r = lax.rsqrt(jnp.mean(xv * xv, axis=-1, keepdims=True) + EPS)
        xhat = xv * r
        err = xhat * gv - t_ref[...]
        dy = err * (1.0 / d)
        dxhat = dy * gv
        dx = r * (dxhat - xhat * jnp.mean(dxhat * xhat, axis=-1, keepdims=True))
        dx_ref[...] = dx
        dxb_ref[...] = dx.astype(BF16)

        @pl.when(pl.program_id(0) == 0)
        def _():
            dg_ref[...] = jnp.zeros_like(dg_ref)
            loss_ref[...] = jnp.zeros_like(loss_ref)

        dg_ref[...] += jnp.sum(dy * xhat, axis=0, keepdims=True)
        per_token = jnp.mean(err * err, axis=-1, keepdims=True)
        loss_ref[...] += 0.5 * jnp.sum(per_token, axis=0, keepdims=True)

    blk = pl.BlockSpec((tb, d), lambda i: (i, 0))
    row = pl.BlockSpec((1, d), lambda i: (0, 0))
    return pl.pallas_call(
        body,
        out_shape=[jax.ShapeDtypeStruct((1, 1), F32), _out((t, d), F32),
                   _out((t, d), BF16), jax.ShapeDtypeStruct((1, d), F32)],
        grid=(t // tb,),
        in_specs=[blk, row, blk],
        out_specs=[pl.BlockSpec((1, 1), lambda i: (0, 0)), blk, blk, row],
        compiler_params=_cparams(("arbitrary",), 2 * tb * d * 14),
        name="loss_head",
    )(_hbm(x3), g.reshape(1, d), _hbm(target))


def _lru_gates(xc, wa, ba, wx, bx, lam):
    nn = (((1,), (0,)), ((), ()))
    xcb = xc.astype(BF16)
    r = _sigmoid(lax.dot_general(xcb, wa, nn, preferred_element_type=F32) + ba)
    i = _sigmoid(lax.dot_general(xcb, wx, nn, preferred_element_type=F32) + bx)
    cl = -LRU_C * _softplus(-lam)
    log_a = cl * r
    a = jnp.exp(log_a)
    one_minus_a2 = _neg_expm1(2.0 * log_a)
    return xcb, r, i, a, one_minus_a2, cl


def _lru_fwd(p, conv_w, conv_b, wa_bd, ba, wx_bd, bx, lam, d_lru, gc, tc):
    t = p.shape[0]
    ng = d_lru // gc
    nt = t // tc
    width = conv_w.shape[0]

    def body(lx_ref, gate_ref, cw_ref, cb_ref, wa_ref, ba_ref, wx_ref, bx_ref, lam_ref,
             y_ref, h_ref, halo, hcar, a_s, u_s):
        @pl.when(pl.program_id(1) == 0)
        def _():
            halo[...] = jnp.zeros_like(halo)
            hcar[...] = jnp.zeros_like(hcar)

        x = lx_ref[...]
        cat = jnp.concatenate([halo[...], x], axis=0)
        halo[...] = x[tc - SUBLANE:, :]
        xc = _conv_fwd(cat, cw_ref[...], width) + cb_ref[...]
        _, r, i, a, om, _ = _lru_gates(xc, wa_ref[...], ba_ref[...], wx_ref[...], bx_ref[...], lam_ref[...])
        a_s[...] = a
        u_s[...] = jnp.sqrt(om) * (i * xc)
        hcar[0:1, :] = _scan_tiles(a_s, u_s, h_ref, hcar[0:1, :], tc, reverse=False)
        gl, _ = _gelu_and_grad(gate_ref[...])
        y_ref[...] = (gl * h_ref[...]).astype(BF16)

    blk = lambda off: pl.BlockSpec((tc, gc), lambda g, s, off=off: (s, off + g))
    rowv = lambda rows: pl.BlockSpec((rows, gc), lambda g, s: (0, g))
    wspec = pl.BlockSpec((None, gc, gc), lambda g, s: (g, 0, 0))
    out_blk = pl.BlockSpec((tc, gc), lambda g, s: (s, g))
    return pl.pallas_call(
        body,
        out_shape=[_out((t, d_lru), BF16), _out((t, d_lru), F32)],
        grid=(ng, nt),
        in_specs=[blk(0), blk(ng), rowv(width), rowv(1), wspec, rowv(1), wspec, rowv(1), rowv(1)],
        out_specs=[out_blk, out_blk],
        scratch_shapes=[pltpu.VMEM((SUBLANE, gc), F32), pltpu.VMEM((SUBLANE, gc), F32),
                        pltpu.VMEM((tc, gc), F32), pltpu.VMEM((tc, gc), F32)],
        compiler_params=_cparams(("parallel", "arbitrary"), 40 * tc * gc * 4),
        name="lru_fwd",
    )(_hbm(p), _hbm(p), conv_w, conv_b.reshape(1, -1), wa_bd, ba.reshape(1, -1), wx_bd, bx.reshape(1, -1),
      lam.reshape(1, -1))


def _lru_bwd(p, hseq, dyp, conv_w, conv_b, wa_bd, ba, wx_bd, bx, lam, d_lru, gc, tc, dep=None):
    t = p.shape[0]
    ng = d_lru // gc
    nt = t // tc
    width = conv_w.shape[0]
    halo_blocks = tc // SUBLANE
    nn = (((1,), (0,)), ((), ()))
    nt_dims = (((1,), (1,)), ((), ()))
    tn_dims = (((0,), (0,)), ((), ()))

    def body(lx_ref, lxh_ref, gate_ref, h_ref, hh_ref, dyp_ref,
             cw_ref, cb_ref, wa_ref, ba_ref, wx_ref, bx_ref, lam_ref,
             dlx_ref, dgate_ref, dcw_ref, dcb_ref, dwa_ref, dba_ref, dwx_ref, dbx_ref, dlam_ref,
             nxt_dxc, nxt_a, nxt_g, al_s, b_s, g_s):
        s = pl.program_id(1)
        first_chunk = s == nt - 1

        @pl.when(s == 0)
        def _():
            nxt_dxc[...] = jnp.zeros_like(nxt_dxc)
            nxt_a[...] = jnp.zeros_like(nxt_a)
            nxt_g[...] = jnp.zeros_like(nxt_g)
            for ref in (dcw_ref, dcb_ref, dwa_ref, dba_ref, dwx_ref, dbx_ref, dlam_ref):
                ref[...] = jnp.zeros_like(ref)

        keep = jnp.where(first_chunk, 0.0, 1.0)
        x = lx_ref[...]
        catx = jnp.concatenate([lxh_ref[...] * keep, x], axis=0)
        cw = cw_ref[...]
        xc = _conv_fwd(catx, cw, width) + cb_ref[...]
        wa = wa_ref[...]
        wx = wx_ref[...]
        lam_v = lam_ref[...]
        xcb, r, i, a, om, cl = _lru_gates(xc, wa, ba_ref[...], wx, bx_ref[...], lam_v)
        mult = jnp.sqrt(om)

        h = h_ref[...]
        hprev = _rows_before(jnp.concatenate([hh_ref[...] * keep, h], axis=0), 1)
        gl, dgl = _gelu_and_grad(gate_ref[...])
        dyp_v = dyp_ref[...]
        dgate_ref[...] = (dyp_v * h * dgl).astype(BF16)

        al_s[...] = _rows_after(jnp.concatenate([a, nxt_a[...]], axis=0), 1)
        b_s[...] = dyp_v * gl
        nxt_g[0:1, :] = _scan_tiles(al_s, b_s, g_s, nxt_g[0:1, :], tc, reverse=True)
        nxt_a[...] = a[0:SUBLANE, :]
        du = g_s[...]

        da = du * hprev
        dmult = du * (i * xc)
        di = du * mult * xc
        dxc = du * mult * i
        dlog_a = da * a - dmult * (a * a / mult)
        dlam_ref[...] += jnp.sum(dlog_a * r, axis=0, keepdims=True) * (LRU_C * _sigmoid(-lam_v))
        dza = (dlog_a * cl) * r * (1.0 - r)
        dzx = di * i * (1.0 - i)
        dba_ref[...] += jnp.sum(dza, axis=0, keepdims=True)
        dbx_ref[...] += jnp.sum(dzx, axis=0, keepdims=True)
        dzab = dza.astype(BF16)
        dzxb = dzx.astype(BF16)
        dwa_ref[...] += lax.dot_general(xcb, dzab, tn_dims, preferred_element_type=F32)
        dwx_ref[...] += lax.dot_general(xcb, dzxb, tn_dims, preferred_element_type=F32)
        dxc = dxc + lax.dot_general(dzab, wa, nt_dims, preferred_element_type=F32)
        dxc = dxc + lax.dot_general(dzxb, wx, nt_dims, preferred_element_type=F32)
        dcb_ref[...] += jnp.sum(dxc, axis=0, keepdims=True)
        _conv_bwd_weight(dcw_ref, dxc, catx, width)
        catd = jnp.concatenate([dxc, nxt_dxc[...]], axis=0)
        dlx_ref[...] = _conv_bwd_input(catd, cw, width).astype(BF16)
        nxt_dxc[...] = dxc[0:SUBLANE, :]

    rev = lambda s: nt - 1 - s
    blk = lambda off: pl.BlockSpec((tc, gc), lambda g, s, off=off: (rev(s), off + g))
    halo = lambda off: pl.BlockSpec(
        (SUBLANE, gc), lambda g, s, off=off: (jnp.maximum(rev(s) * halo_blocks - 1, 0), off + g))
    rowv = lambda rows: pl.BlockSpec((rows, gc), lambda g, s: (0, g))
    wspec = pl.BlockSpec((None, gc, gc), lambda g, s: (g, 0, 0))
    out_blk = pl.BlockSpec((tc, gc), lambda g, s: (rev(s), g))
    vec = lambda rows: jax.ShapeDtypeStruct((rows, d_lru), F32)
    wshape = jax.ShapeDtypeStruct((ng, gc, gc), F32)
    body, in_specs, operands = _dep_args(
        body,
        [blk(0), halo(0), blk(ng), blk(0), halo(0), blk(0),
         rowv(width), rowv(1), wspec, rowv(1), wspec, rowv(1), rowv(1)],
        [_hbm(p), _hbm(p), _hbm(p), _hbm(hseq), _hbm(hseq), _hbm(dyp),
         conv_w, conv_b.reshape(1, -1), wa_bd, ba.reshape(1, -1), wx_bd,
         bx.reshape(1, -1), lam.reshape(1, -1)], dep)
    return pl.pallas_call(
        body,
        out_shape=[_out((t, d_lru), BF16), _out((t, d_lru), BF16),
                   vec(width), vec(1), wshape, vec(1), wshape, vec(1), vec(1)],
        grid=(ng, nt),
        in_specs=in_specs,
        out_specs=[out_blk, out_blk, rowv(width), rowv(1), wspec, rowv(1), wspec, rowv(1), rowv(1)],
        scratch_shapes=[pltpu.VMEM((SUBLANE, gc), F32), pltpu.VMEM((SUBLANE, gc), F32),
                        pltpu.VMEM((SUBLANE, gc), F32),
                        pltpu.VMEM((tc, gc), F32), pltpu.VMEM((tc, gc), F32), pltpu.VMEM((tc, gc), F32)],
        compiler_params=_cparams(("parallel", "arbitrary"), 80 * tc * gc * 4),
        name="lru_bwd",
    )(*operands)


def _sc_fwd(p, conv_w, col0, d_sc, cb, tc):
    t = p.shape[0]
    nc = d_sc // cb
    nt = t // tc
    width = conv_w.shape[0]
    base = col0 // cb

    def body(b_ref, c_ref, v_ref, w_ref, y_ref, halo):
        @pl.when(pl.program_id(1) == 0)
        def _():
            halo[...] = jnp.zeros_like(halo)

        cv = c_ref[...] * v_ref[...]
        cat = jnp.concatenate([halo[...], cv], axis=0)
        halo[...] = cv[tc - SUBLANE:, :]
        y_ref[...] = (b_ref[...] * _conv_fwd(cat, w_ref[...], width)).astype(BF16)

    blk = lambda slab: pl.BlockSpec((tc, cb), lambda j, s, slab=slab: (s, base + slab * nc + j))
    return pl.pallas_call(
        body,
        out_shape=_out((t, d_sc), BF16),
        grid=(nc, nt),
        in_specs=[blk(0), blk(1), blk(2), pl.BlockSpec((width, cb), lambda j, s: (0, j))],
        out_specs=pl.BlockSpec((tc, cb), lambda j, s: (s, j)),
        scratch_shapes=[pltpu.VMEM((SUBLANE, cb), F32)],
        compiler_params=_cparams(("parallel", "arbitrary"), 20 * tc * cb * 4),
        name="sc_fwd",
    )(_hbm(p), _hbm(p), _hbm(p), conv_w)


def _sc_bwd(p, dyp, conv_w, col0, d_sc, cb, tc):
    t = p.shape[0]
    nc = d_sc // cb
    nt = t // tc
    width = conv_w.shape[0]
    base = col0 // cb
    halo_blocks = tc // SUBLANE

    def body(b_ref, c_ref, ch_ref, v_ref, vh_ref, dyp_ref, w_ref,
             db_ref, dc_ref, dv_ref, dw_ref, nxt_dq):
        s = pl.program_id(1)

        @pl.when(s == 0)
        def _():
            nxt_dq[...] = jnp.zeros_like(nxt_dq)
            dw_ref[...] = jnp.zeros_like(dw_ref)

        keep = jnp.where(s == nt - 1, 0.0, 1.0)
        cvals = c_ref[...]
        vvals = v_ref[...]
        w = w_ref[...]
        catcv = jnp.concatenate([ch_ref[...] * vh_ref[...] * keep, cvals * vvals], axis=0)
        q = _conv_fwd(catcv, w, width)
        dyp_v = dyp_ref[...]
        db_ref[...] = (dyp_v * q).astype(BF16)
        dq = dyp_v * b_ref[...]
        _conv_bwd_weight(dw_ref, dq, catcv, width)
        dcv = _conv_bwd_input(jnp.concatenate([dq, nxt_dq[...]], axis=0), w, width)
        nxt_dq[...] = dq[0:SUBLANE, :]
        dc_ref[...] = (dcv * vvals).astype(BF16)
        dv_ref[...] = (dcv * cvals).astype(BF16)

    rev = lambda s: nt - 1 - s
    blk = lambda slab: pl.BlockSpec((tc, cb), lambda j, s, slab=slab: (rev(s), base + slab * nc + j))
    halo = lambda slab: pl.BlockSpec(
        (SUBLANE, cb),
        lambda j, s, slab=slab: (jnp.maximum(rev(s) * halo_blocks - 1, 0), base + slab * nc + j))
    out_blk = pl.BlockSpec((tc, cb), lambda j, s: (rev(s), j))
    wblk = pl.BlockSpec((width, cb), lambda j, s: (0, j))
    act = _out((t, d_sc), BF16)
    return pl.pallas_call(
        body,
        out_shape=[act, act, act, jax.ShapeDtypeStruct((width, d_sc), F32)],
        grid=(nc, nt),
        in_specs=[blk(0), blk(1), halo(1), blk(2), halo(2), out_blk, wblk],
        out_specs=[out_blk, out_blk, out_blk, wblk],
        scratch_shapes=[pltpu.VMEM((SUBLANE, cb), F32)],
        compiler_params=_cparams(("parallel", "arbitrary"), 30 * tc * cb * 4),
        name="sc_bwd",
    )(_hbm(p), _hbm(p), _hbm(p), _hbm(p), _hbm(p), _hbm(dyp), conv_w)


def _merge_fwd(p, y_lru, y_sc, col0, tc):
    t, d = y_lru.shape
    cb = _pick(math.gcd(d, col0), 1024)
    nc = d // cb
    base = col0 // cb

    def body(gl_ref, gs_ref, yl_ref, ys_ref, o_ref):
        o_ref[...] = (_sigmoid(gl_ref[...]) * yl_ref[...] + _sigmoid(gs_ref[...]) * ys_ref[...]).astype(BF16)

    gate = lambda slab: pl.BlockSpec((tc, cb), lambda s, j, slab=slab: (s, base + slab * nc + j))
    blk = pl.BlockSpec((tc, cb), lambda s, j: (s, j))
    return pl.pallas_call(
        body,
        out_shape=_out((t, d), BF16),
        grid=(t // tc, nc),
        in_specs=[gate(0), gate(1), blk, blk],
        out_specs=blk,
        compiler_params=_cparams(("parallel", "parallel"), 2 * tc * cb * 20),
        name="merge_fwd",
    )(_hbm(p), _hbm(p), _hbm(y_lru), _hbm(y_sc))


def _merge_bwd(p, y_lru, y_sc, dmerged, col0, tc):
    t, d = y_lru.shape
    cb = _pick(math.gcd(d, col0), 1024)
    nc = d // cb
    base = col0 // cb

    def body(gl_ref, gs_ref, yl_ref, ys_ref, dm_ref, dgl_ref, dgs_ref, dyl_ref, dys_ref):
        dm = dm_ref[...]
        sl = _sigmoid(gl_ref[...])
        ss = _sigmoid(gs_ref[...])
        dgl_ref[...] = (dm * yl_ref[...] * (sl * (1.0 - sl))).astype(BF16)
        dgs_ref[...] = (dm * ys_ref[...] * (ss * (1.0 - ss))).astype(BF16)
        dyl_ref[...] = (dm * sl).astype(BF16)
        dys_ref[...] = (dm * ss).astype(BF16)

    gate = lambda slab: pl.BlockSpec((tc, cb), lambda s, j, slab=slab: (s, base + slab * nc + j))
    blk = pl.BlockSpec((tc, cb), lambda s, j: (s, j))
    act = _out((t, d), BF16)
    return pl.pallas_call(
        body,
        out_shape=[act, act, act, act],
        grid=(t // tc, nc),
        in_specs=[gate(0), gate(1), blk, blk, blk],
        out_specs=[blk, blk, blk, blk],
        compiler_params=_cparams(("parallel", "parallel"), 2 * tc * cb * 28),
        name="merge_bwd",
    )(_hbm(p), _hbm(p), _hbm(y_lru), _hbm(y_sc), _hbm(dmerged))


def _ffn_act_fwd(up, conv_w, d_ff, cb, tc):
    t = up.shape[0]
    nc = d_ff // cb
    nt = t // tc
    width = conv_w.shape[0]

    def body(g_ref, v_ref, wg_ref, wv_ref, o_ref, halo_g, halo_v):
        @pl.when(pl.program_id(1) == 0)
        def _():
            halo_g[...] = jnp.zeros_like(halo_g)
            halo_v[...] = jnp.zeros_like(halo_v)

        g = g_ref[...]
        v = v_ref[...]
        ug = _conv_fwd(jnp.concatenate([halo_g[...], g], axis=0), wg_ref[...], width)
        uv = _conv_fwd(jnp.concatenate([halo_v[...], v], axis=0), wv_ref[...], width)
        halo_g[...] = g[tc - SUBLANE:, :]
        halo_v[...] = v[tc - SUBLANE:, :]
        o_ref[...] = (ug * _sigmoid(ug) * uv).astype(BF16)

    blk = lambda half: pl.BlockSpec((tc, cb), lambda j, s, half=half: (s, half * nc + j))
    wblk = lambda half: pl.BlockSpec((width, cb), lambda j, s, half=half: (0, half * nc + j))
    return pl.pallas_call(
        body,
        out_shape=_out((t, d_ff), BF16),
        grid=(nc, nt),
        in_specs=[blk(0), blk(1), wblk(0), wblk(1)],
        out_specs=pl.BlockSpec((tc, cb), lambda j, s: (s, j)),
        scratch_shapes=[pltpu.VMEM((SUBLANE, cb), F32), pltpu.VMEM((SUBLANE, cb), F32)],
        compiler_params=_cparams(("parallel", "arbitrary"), 24 * tc * cb * 4),
        name="ffn_act_fwd",
    )(_hbm(up), _hbm(up), conv_w, conv_w)


def _ffn_act_bwd(up, dact, conv_w, d_ff, cb, tc, dep=None):
    t = up.shape[0]
    nc = d_ff // cb
    nt = t // tc
    width = conv_w.shape[0]
    halo_blocks = tc // SUBLANE

    def body(g_ref, gh_ref, v_ref, vh_ref, da_ref, wg_ref, wv_ref,
             dg_ref, dv_ref, dwg_ref, dwv_ref, nxt_g, nxt_v):
        s = pl.program_id(1)

        @pl.when(s == 0)
        def _():
            nxt_g[...] = jnp.zeros_like(nxt_g)
            nxt_v[...] = jnp.zeros_like(nxt_v)
            dwg_ref[...] = jnp.zeros_like(dwg_ref)
            dwv_ref[...] = jnp.zeros_like(dwv_ref)

        keep = jnp.where(s == nt - 1, 0.0, 1.0)
        wg = wg_ref[...]
        wv = wv_ref[...]
        catg = jnp.concatenate([gh_ref[...] * keep, g_ref[...]], axis=0)
        catv = jnp.concatenate([vh_ref[...] * keep, v_ref[...]], axis=0)
        ug = _conv_fwd(catg, wg, width)
        uv = _conv_fwd(catv, wv, width)
        sg = _sigmoid(ug)
        da = da_ref[...]
        dug = da * uv * (sg * (1.0 + ug * (1.0 - sg)))
        duv = da * (ug * sg)
        _conv_bwd_weight(dwg_ref, dug, catg, width)
        _conv_bwd_weight(dwv_ref, duv, catv, width)
        dg_ref[...] = _conv_bwd_input(jnp.concatenate([dug, nxt_g[...]], axis=0), wg, width).astype(BF16)
        dv_ref[...] = _conv_bwd_input(jnp.concatenate([duv, nxt_v[...]], axis=0), wv, width).astype(BF16)
        nxt_g[...] = dug[0:SUBLANE, :]
        nxt_v[...] = duv[0:SUBLANE, :]

    rev = lambda s: nt - 1 - s
    blk = lambda half: pl.BlockSpec((tc, cb), lambda j, s, half=half: (rev(s), half * nc + j))
    halo = lambda half: pl.BlockSpec(
        (SUBLANE, cb), lambda j, s, half=half: (jnp.maximum(rev(s) * halo_blocks - 1, 0), half * nc + j))
    wblk = lambda half: pl.BlockSpec((width, cb), lambda j, s, half=half: (0, half * nc + j))
    out_blk = pl.BlockSpec((tc, cb), lambda j, s: (rev(s), j))
    wout = pl.BlockSpec((width, cb), lambda j, s: (0, j))
    act = _out((t, d_ff), BF16)
    wshape = jax.ShapeDtypeStruct((width, d_ff), F32)
    body, in_specs, operands = _dep_args(
        body, [blk(0), halo(0), blk(1), halo(1), out_blk, wblk(0), wblk(1)],
        [_hbm(up), _hbm(up), _hbm(up), _hbm(up), _hbm(dact), conv_w, conv_w], dep)
    return pl.pallas_call(
        body,
        out_shape=[act, act, wshape, wshape],
        grid=(nc, nt),
        in_specs=in_specs,
        out_specs=[out_blk, out_blk, wout, wout],
        scratch_shapes=[pltpu.VMEM((SUBLANE, cb), F32), pltpu.VMEM((SUBLANE, cb), F32)],
        compiler_params=_cparams(("parallel", "arbitrary"), 40 * tc * cb * 4),
        name="ffn_act_bwd",
    )(*operands)


def _mesh_pos():
    x, y, c = lax.axis_index("x"), lax.axis_index("y"), lax.axis_index("c")
    return x, y, c


def _other_chips(x, y):
    return [(1 - x, y), (x, 1 - y), (1 - x, 1 - y)]


def _cast_place(w, chip, col_sharded, name):
    r, cdim = w.shape
    full = (r, cdim * N_CHIPS) if col_sharded else (r * N_CHIPS, cdim)
    rb = _pick(r, max(BF16_ROWS, (512 * 1024) // cdim), BF16_ROWS)
    nb = r // rb

    def body(chip_ref, w_ref, o_ref):
        o_ref[...] = w_ref[...].astype(BF16)

    if col_sharded:
        out_map = lambda i, chip_ref: (i, chip_ref[0])
    else:
        out_map = lambda i, chip_ref: (chip_ref[0] * nb + i, 0)
    grid_spec = pltpu.PrefetchScalarGridSpec(
        num_scalar_prefetch=1,
        grid=(nb,),
        in_specs=[pl.BlockSpec((rb, cdim), lambda i, chip_ref: (i, 0))],
        out_specs=pl.BlockSpec((rb, cdim), out_map),
    )
    return pl.pallas_call(
        body,
        out_shape=jax.ShapeDtypeStruct(full, BF16),
        grid_spec=grid_spec,
        compiler_params=_cparams(("parallel",), 2 * rb * cdim * 6),
        name=name,
    )(chip, _hbm(w))


def _remote(src, dst, send_sems, recv_sems, idx, to):
    return pltpu.make_async_remote_copy(
        src_ref=src, dst_ref=dst, send_sem=send_sems.at[idx], recv_sem=recv_sems.at[idx],
        device_id=to, device_id_type=MESH)


def _exchange(name, arrays, n_sems, plan):
    n = len(arrays)

    def body(*refs):
        bufs = refs[n:2 * n]
        send_sems, recv_sems = refs[2 * n:]
        sends, arrivals = plan(bufs, send_sems, recv_sems)
        for cp in sends:
            cp.start()
        for cp in arrivals:
            cp.wait_recv()
        for cp in sends:
            cp.wait_send()

    outs = pl.pallas_call(
        body,
        out_shape=[jax.ShapeDtypeStruct(a.shape, a.dtype) for a in arrays],
        in_specs=[ANY] * n,
        out_specs=[ANY] * n,
        input_output_aliases={k: k for k in range(n)},
        scratch_shapes=[pltpu.SemaphoreType.DMA((n_sems,)), pltpu.SemaphoreType.DMA((n_sems,))],
        name=name,
    )(*arrays)
    return list(outs)


def _exchange_start(name, arrays, n_sems, plan, after=None):
    n = len(arrays)
    n_in = n + (after is not None)

    def body(*refs):
        bufs = refs[:n]
        send_sems, recv_sems = refs[n_in], refs[n_in + 1]
        token = refs[-1]
        sends, _ = plan(bufs, send_sems, recv_sems)
        for cp in sends:
            cp.start()
        token[...] = jnp.zeros_like(token)

    out = pl.pallas_call(
        body,
        out_shape=(pltpu.SemaphoreType.DMA((n_sems,)), pltpu.SemaphoreType.DMA((n_sems,)),
                   *[pltpu.HBM(a.shape, a.dtype) for a in arrays],
                   jax.ShapeDtypeStruct((SUBLANE, LANE), F32)),
        in_specs=[HBM_SPEC] * n + [ANY] * (n_in - n),
        out_specs=(SEM_SPEC, SEM_SPEC, *[HBM_SPEC] * n, VMEM_SPEC),
        input_output_aliases={k: 2 + k for k in range(n)},
        compiler_params=pltpu.CompilerParams(has_side_effects=DATAFLOW_EFFECT),
        name=name,
    )(*[pltpu.with_memory_space_constraint(a, pltpu.HBM) for a in arrays], *([after] if after is not None else []))
    return out[0], out[1], list(out[2:2 + n]), out[-1]


def _exchange_wait(name, arrays, send_sems, recv_sems, after, plan):
    n = len(arrays)

    def body(*refs):
        bufs = refs[:n]
        sends, arrivals = plan(bufs, refs[n], refs[n + 1])
        for cp in arrivals:
            cp.wait_recv()
        for cp in sends:
            cp.wait_send()

    outs = pl.pallas_call(
        body,
        out_shape=[pltpu.HBM(a.shape, a.dtype) for a in arrays],
        in_specs=[HBM_SPEC] * n + [SEM_SPEC, SEM_SPEC, ANY],
        out_specs=[HBM_SPEC] * n,
        input_output_aliases={k: k for k in range(n)},
        compiler_params=pltpu.CompilerParams(has_side_effects=DATAFLOW_EFFECT),
        name=name,
    )(*arrays, send_sems, recv_sems, after)
    return list(outs)


def _half_block(ref, shard_shape, col_sharded, chip, half):
    r, cdim = shard_shape
    h = r // 2
    if col_sharded:
        return ref.at[pl.ds(pl.multiple_of(half * h, BF16_ROWS), h),
                      pl.ds(pl.multiple_of(chip * cdim, LANE), cdim)]
    return ref.at[pl.ds(pl.multiple_of(chip * r + half * h, BF16_ROWS), h), :]


def _gather_plan(shard_shapes, col_sharded, ks):
    def plan(bufs, send_sems, recv_sems):
        x, y, c = _mesh_pos()
        sends, arrivals = [], []
        for ref, k in zip(bufs, ks):
            mine = _half_block(ref, shard_shapes[k], col_sharded[k], 2 * x + y, c)
            for j, (px, py) in enumerate(_other_chips(x, y)):
                landed = _half_block(ref, shard_shapes[k], col_sharded[k], 2 * px + py, c)
                sends.append(_remote(mine, mine, send_sems, recv_sems, 3 * k + j, (px, py, c)))
                arrivals.append(_remote(landed, landed, send_sems, recv_sems, 3 * k + j, (px, py, c)))
        return sends, arrivals
    return plan


def _forward_plan(shard_shapes, col_sharded, ks):
    def plan(bufs, send_sems, recv_sems):
        x, y, c = _mesh_pos()
        sends, arrivals = [], []
        for i, (ref, k) in enumerate(zip(bufs, ks)):
            for j, (px, py) in enumerate(_other_chips(x, y)):
                landed = _half_block(ref, shard_shapes[k], col_sharded[k], 2 * px + py, c)
                theirs = _half_block(ref, shard_shapes[k], col_sharded[k], 2 * px + py, 1 - c)
                sends.append(_remote(landed, landed, send_sems, recv_sems, 3 * i + j, (x, y, 1 - c)))
                arrivals.append(_remote(theirs, theirs, send_sems, recv_sems, 3 * i + j, (x, y, 1 - c)))
        return sends, arrivals
    return plan


def _small_gather(small):
    def body(small_ref, out_ref, send_sems, recv_sems):
        x, y, c = _mesh_pos()
        me = 2 * x + y
        out_ref[me] = small_ref[...]
        copies = []
        for j, (px, py) in enumerate(_other_chips(x, y)):
            cp = _remote(small_ref, out_ref.at[me], send_sems, recv_sems, j, (px, py, c))
            cp.start()
            copies.append(cp)
        for j, (px, py) in enumerate(_other_chips(x, y)):
            _remote(small_ref, out_ref.at[2 * px + py], send_sems, recv_sems, j, (px, py, c)).wait_recv()
        for cp in copies:
            cp.wait_send()

    return pl.pallas_call(
        body,
        out_shape=jax.ShapeDtypeStruct((N_CHIPS,) + small.shape, small.dtype),
        in_specs=[VMEM_SPEC],
        out_specs=VMEM_SPEC,
        scratch_shapes=[pltpu.SemaphoreType.DMA((N_CHIPS - 1,)), pltpu.SemaphoreType.DMA((N_CHIPS - 1,))],
        name="gather_small",
    )(small)


def _as3d(g, col_sharded):
    r, cdim = g.shape
    return g.reshape(1, r, cdim) if col_sharded else g.reshape(N_CHIPS, r // N_CHIPS, cdim)


def _pair_plan(m):
    def plan(bufs, send_sems, recv_sems):
        x, y, c = _mesh_pos()
        copies = []
        for i in range(m):
            h = bufs[i].shape[1] // 2
            src = bufs[i].at[:, pl.ds(pl.multiple_of((1 - c) * h, SUBLANE), h), :]
            copies.append(_remote(src, bufs[m + i], send_sems, recv_sems, i, (x, y, 1 - c)))
        return copies, copies
    return plan


def _chip_plan(col_flags):
    m = len(col_flags)

    def plan(bufs, send_sems, recv_sems):
        x, y, c = _mesh_pos()
        copies = []
        for i in range(m):
            land = bufs[m + i]
            width = land.shape[2]
            for j, (px, py) in enumerate(_other_chips(x, y)):
                q = 2 * px + py
                if col_flags[i]:
                    src = bufs[i].at[0, :, pl.ds(pl.multiple_of(q * width, LANE), width)]
                else:
                    src = bufs[i].at[q]
                copies.append(_remote(src, land.at[j], send_sems, recv_sems, 3 * i + j, (px, py, c)))
        return copies, copies
    return plan


def _share_plan(m):
    def plan(bufs, send_sems, recv_sems):
        x, y, c = _mesh_pos()
        sends, arrivals = [], []
        for i in range(m):
            h = bufs[i].shape[0] // 2
            mine = bufs[i].at[pl.ds(pl.multiple_of(c * h, SUBLANE), h), :]
            theirs = bufs[i].at[pl.ds(pl.multiple_of((1 - c) * h, SUBLANE), h), :]
            sends.append(_remote(mine, mine, send_sems, recv_sems, i, (x, y, 1 - c)))
            arrivals.append(_remote(theirs, theirs, send_sems, recv_sems, i, (x, y, 1 - c)))
        return sends, arrivals
    return plan


def _pair_add(g3, other, core):
    a, r, cdim = g3.shape
    h = r // 2
    rb = _pick(h, max(BF16_ROWS, (512 * 1024) // cdim), BF16_ROWS)
    nb = h // rb

    def body(core_ref, g_ref, o_ref, out_ref):
        out_ref[...] = (g_ref[...] + o_ref[...]).astype(BF16)

    grid_spec = pltpu.PrefetchScalarGridSpec(
        num_scalar_prefetch=1,
        grid=(a, nb),
        in_specs=[pl.BlockSpec((None, rb, cdim), lambda i, j, core_ref: (i, core_ref[0] * nb + j, 0)),
                  pl.BlockSpec((None, rb, cdim), lambda i, j, core_ref: (i, j, 0))],
        out_specs=pl.BlockSpec((None, rb, cdim), lambda i, j, core_ref: (i, j, 0)),
    )
    return pl.pallas_call(
        body,
        out_shape=_out((a, h, cdim), BF16),
        grid_spec=grid_spec,
        compiler_params=_cparams(("parallel", "parallel"), 2 * rb * cdim * 10),
        name="grad_pair_add",
    )(core, _hbm(g3), _hbm(other))


def _small_allreduce(small):
    def body(small_ref, out_ref, slots, send_sems, recv_sems):
        x, y, c = _mesh_pos()
        my_dev = 4 * x + 2 * y + c
        slots[my_dev] = small_ref[...]
        copies = []
        for m in range(1, N_DEV):
            peer = (x ^ ((m >> 2) & 1), y ^ ((m >> 1) & 1), c ^ (m & 1))
            cp = _remote(small_ref, slots.at[my_dev], send_sems, recv_sems, m - 1, peer)
            cp.start()
            copies.append(cp)
        for m in range(1, N_DEV):
            _remote(small_ref, slots.at[my_dev ^ m], send_sems, recv_sems, m - 1, (x, y, c)).wait_recv()
        total = slots[0]
        for d in range(1, N_DEV):
            total = total + slots[d]
        out_ref[...] = total
        for cp in copies:
            cp.wait_send()

    return pl.pallas_call(
        body,
        out_shape=jax.ShapeDtypeStruct(small.shape, F32),
        in_specs=[VMEM_SPEC],
        out_specs=VMEM_SPEC,
        scratch_shapes=[pltpu.VMEM((N_DEV,) + small.shape, F32),
                        pltpu.SemaphoreType.DMA((N_DEV - 1,)), pltpu.SemaphoreType.DMA((N_DEV - 1,))],
        compiler_params=pltpu.CompilerParams(
            vmem_limit_bytes=min(VMEM_BUDGET, (N_DEV + 4) * _nbytes(small.shape, F32) + (8 << 20))),
        name="grad_small_allreduce",
    )(small)


def _chip_sum(partial, land, where, col_sharded):
    _, h, cdim = land.shape
    rb = _pick(h, max(BF16_ROWS, (512 * 1024) // cdim), BF16_ROWS)
    nb = h // rb

    def body(where_ref, own_ref, l_ref, o_ref):
        total = own_ref[...].astype(F32)
        for j in range(N_CHIPS - 1):
            total = total + l_ref[j].astype(F32)
        o_ref[...] = total

    if col_sharded:
        own_map = lambda i, w: (0, i, w[0])
    else:
        own_map = lambda i, w: (w[0], i, 0)
    grid_spec = pltpu.PrefetchScalarGridSpec(
        num_scalar_prefetch=1,
        grid=(nb,),
        in_specs=[pl.BlockSpec((None, rb, cdim), own_map),
                  pl.BlockSpec((N_CHIPS - 1, rb, cdim), lambda i, w: (0, i, 0))],
        out_specs=pl.BlockSpec((rb, cdim), lambda i, w: (w[1] * nb + i, 0)),
    )
    return pl.pallas_call(
        body,
        out_shape=_out((2 * h, cdim), F32),
        grid_spec=grid_spec,
        compiler_params=_cparams(("parallel",), 2 * rb * cdim * 12),
        name="grad_chip_sum",
    )(where, _hbm(partial), _hbm(land))


def _adamw(w, g, m, v, name, dep=None):
    r, cdim = w.shape
    rb = _pick(r, max(SUBLANE, (256 * 1024) // cdim), SUBLANE)
    c1 = 1.0 - ADAM_B1 ** ADAM_STEP
    c2 = 1.0 - ADAM_B2 ** ADAM_STEP

    def body(w_ref, g_ref, m_ref, v_ref, go_ref, d_ref, mo_ref, vo_ref):
        gv = g_ref[...]
        mn = ADAM_B1 * m_ref[...] + (1.0 - ADAM_B1) * gv
        vn = ADAM_B2 * v_ref[...] + (1.0 - ADAM_B2) * (gv * gv)
        m_hat = mn / c1
        v_hat = vn / c2
        d_ref[...] = -ADAM_LR * (m_hat / (jnp.sqrt(v_hat) + ADAM_EPS) + ADAM_WD * w_ref[...])
        go_ref[...] = gv
        mo_ref[...] = mn
        vo_ref[...] = vn

    blk = pl.BlockSpec((rb, cdim), lambda i: (i, 0))
    shape = jax.ShapeDtypeStruct((r, cdim), F32)
    body, in_specs, operands = _dep_args(body, [blk] * 4, [_hbm(w), _hbm(g), _hbm(m), _hbm(v)], dep)
    return pl.pallas_call(
        body,
        out_shape=[shape] * 4,
        grid=(r // rb,),
        in_specs=in_specs,
        out_specs=[blk] * 4,
        compiler_params=_cparams(("parallel",), 2 * rb * cdim * 4 * 8),
        name=name,
    )(*operands)


def _pack(arrays):
    tile = SUBLANE * LANE
    pieces = []
    for arr in arrays:
        flat = arr.reshape(-1)
        pad = (-flat.shape[0]) % tile
        if pad:
            flat = jnp.concatenate([flat, jnp.zeros((pad,), flat.dtype)])
        pieces.append(flat)
    return jnp.concatenate(pieces).reshape(-1, LANE)


def _unpack(packed, shapes):
    tile = SUBLANE * LANE
    flat = packed.reshape(-1)
    out, off = [], 0
    for shp in shapes:
        size = math.prod(shp)
        out.append(flat[off:off + size].reshape(shp))
        off += size + ((-size) % tile)
    return out


def _block_diag_groups(w, per_group):
    hcount, hd, _ = w.shape
    ng = hcount // per_group
    w4 = w.reshape(ng, per_group, hd, hd)
    eye = jnp.eye(per_group, dtype=w.dtype)
    bd = w4[:, :, :, None, :] * eye[None, :, None, :, None]
    return bd.reshape(ng, per_group * hd, per_group * hd).astype(BF16)


def _diag_blocks(wbd, per_group, hd):
    ng = wbd.shape[0]
    w5 = wbd.reshape(ng, per_group, hd, per_group, hd)
    blocks = [w5[:, i, :, i, :] for i in range(per_group)]
    return jnp.stack(blocks, axis=1).reshape(ng * per_group, hd, hd)


def kernel(x, g_mix, w_in, lru_conv_w, lru_conv_b, lru_wa, lru_ba, lru_wx, lru_bx, lru_lambda, lru_w_out, sc_conv_w, sc_w_out, w_o, g_ffn, ffn_w_up, ffn_conv_w, ffn_w_down, g_final, loss_target, m_g_mix, m_w_in, m_lru_conv_w, m_lru_conv_b, m_lru_wa, m_lru_ba, m_lru_wx, m_lru_bx, m_lru_lambda, m_lru_w_out, m_sc_conv_w, m_sc_w_out, m_w_o, m_g_ffn, m_ffn_w_up, m_ffn_conv_w, m_ffn_w_down, m_g_final, v_g_mix, v_w_in, v_lru_conv_w, v_lru_conv_b, v_lru_wa, v_lru_ba, v_lru_wx, v_lru_bx, v_lru_lambda, v_lru_w_out, v_sc_conv_w, v_sc_w_out, v_w_o, v_g_ffn, v_ffn_w_up, v_ffn_conv_w, v_ffn_w_down, v_g_final):
    seq, d_model = x.shape[1], x.shape[2]
    heads, head_dim, _ = lru_wa.shape
    d_lru = heads * head_dim
    d_sc = sc_w_out.shape[0]
    d_ff = ffn_w_down.shape[0] * N_CHIPS
    assert x.shape[0] == 1 and w_in.shape[1] * N_CHIPS == 2 * d_lru + 3 * d_sc + 2 * d_model
    xs = x.reshape(seq, d_model)
    target = loss_target.reshape(seq, d_model)

    chip = 2 * lax.axis_index("x") + lax.axis_index("y")
    core = lax.axis_index("c").astype(jnp.int32).reshape(1)

    big_w = [w_in, lru_w_out, sc_w_out, w_o, ffn_w_up, ffn_w_down]
    big_m = [m_w_in, m_lru_w_out, m_sc_w_out, m_w_o, m_ffn_w_up, m_ffn_w_down]
    big_v = [v_w_in, v_lru_w_out, v_sc_w_out, v_w_o, v_ffn_w_up, v_ffn_w_down]
    col_sharded = [True, True, True, False, True, False]
    conv_shards = [lru_conv_w, sc_conv_w, ffn_conv_w]
    conv_pack = jnp.concatenate(
        [jnp.pad(w, ((0, SUBLANE - w.shape[0]), (0, 0))) for w in conv_shards], axis=1)
    big_names = ["w_in", "lru_w_out", "sc_w_out", "w_o", "ffn_w_up", "ffn_w_down"]
    chip_arr = chip.astype(jnp.int32).reshape(1)
    placed = [_cast_place(w, chip_arr, cs, "cast_" + nm) for w, cs, nm in zip(big_w, col_sharded, big_names)]
    conv_all = _small_gather(conv_pack)
    shard_shapes = [w.shape for w in big_w]
    n_big = len(big_w)

    def gather_start(ks, after, tag):
        send, recv, bufs, token = _exchange_start(
            "gather_start_" + tag, [placed[k] for k in ks], 3 * n_big,
            _gather_plan(shard_shapes, col_sharded, ks), after=after)
        return (send, recv, dict(zip(ks, bufs))), token

    def arrived(state, ks, after, tag):
        send, recv, bufs = state
        got = _exchange_wait("gather_wait_" + tag, [bufs[k] for k in ks], send, recv, after,
                             _gather_plan(shard_shapes, col_sharded, ks))
        return _exchange("gather_forward_" + tag, got, 3 * len(ks), _forward_plan(shard_shapes, col_sharded, ks))

    conv_full, off = [], 0
    for w in conv_shards:
        kw, nq = w.shape
        piece = conv_all[:, :kw, off:off + nq]
        conv_full.append(piece.transpose(1, 0, 2).reshape(kw, N_CHIPS * nq))
        off += nq
    lcw, scw, fcw = conv_full

    per_group = max(1, min(heads, 256 // head_dim))
    gc = per_group * head_dim
    wa_bd = _block_diag_groups(lru_wa, per_group)
    wx_bd = _block_diag_groups(lru_wx, per_group)
    tc = _pick(seq, 256, SUBLANE)
    cb_sc = _pick(d_sc, 512)
    cb_ff = _pick(d_ff, 512)
    col_sc = 2 * d_lru
    col_gates = 2 * d_lru + 3 * d_sc

    first, token = gather_start([0], conv_all, "in")
    h1 = _rms_fwd(xs, g_mix, "rms_mix", dep=token)
    (win_b,) = arrived(first, [0], h1, "in")
    rest, token = gather_start([1, 2, 3, 4, 5], win_b, "rest")
    p = _mm(h1, win_b, "nn", F32, name="mm_in", dep=token)
    wlo_b, wso_b, wo_b = arrived(rest, [1, 2, 3], p, "mix")
    y_lru_pre, hseq = _lru_fwd(p, lcw, lru_conv_b, wa_bd, lru_ba, wx_bd, lru_bx, lru_lambda, d_lru, gc, tc)
    y_sc_pre = _sc_fwd(p, scw, col_sc, d_sc, cb_sc, tc)
    y_lru = _mm(y_lru_pre, wlo_b, "nn", F32, name="mm_lru_out")
    y_sc = _mm(y_sc_pre, wso_b, "nn", F32, name="mm_sc_out")
    merged = _merge_fwd(p, y_lru, y_sc, col_gates, tc)
    x2 = _mm(merged, wo_b, "nn", F32, res=xs, name="mm_o")
    (wup_b,) = arrived(rest, [4], x2, "up")
    h2 = _rms_fwd(x2, g_ffn, "rms_ffn")
    up = _mm(h2, wup_b, "nn", F32, name="mm_up")
    (wdn_b,) = arrived(rest, [5], up, "down")
    act = _ffn_act_fwd(up, fcw, d_ff, cb_ff, tc)
    x3 = _mm(act, wdn_b, "nn", F32, res=x2, name="mm_down")
    loss_part, dx3, dx3b, dg_final = _loss_head(x3, g_final, target)

    where = jnp.concatenate([chip_arr, core])

    def reduce_start(grads, flags, tag):
        views = [_as3d(g, cs) for g, cs in zip(grads, flags)]
        lands = [lax.empty((v.shape[0], v.shape[1] // 2, v.shape[2]), F32) for v in views]
        send, recv, bufs, token = _exchange_start("grad_pair_start_" + tag, views + lands, len(views),
                                                  _pair_plan(len(views)))
        return (send, recv, bufs, flags, tag), token

    def reduce_mid(state, after):
        send, recv, bufs, flags, tag = state
        m = len(flags)
        bufs = _exchange_wait("grad_pair_wait_" + tag, bufs, send, recv, after, _pair_plan(m))
        partials = [_pair_add(bufs[i], bufs[m + i], core) for i in range(m)]
        lands = []
        for pz, cs in zip(partials, flags):
            _, h, cdim = pz.shape
            lands.append(lax.empty((N_CHIPS - 1, h, cdim // N_CHIPS if cs else cdim), BF16))
        send, recv, bufs, token = _exchange_start("grad_chip_start_" + tag, partials + lands, 3 * m,
                                                  _chip_plan(flags))
        return (send, recv, bufs, flags, tag), token

    def reduce_end(state, after):
        send, recv, bufs, flags, tag = state
        m = len(flags)
        bufs = _exchange_wait("grad_chip_wait_" + tag, bufs, send, recv, after, _chip_plan(flags))
        return [_chip_sum(bufs[i], bufs[m + i], where, flags[i]) for i in range(m)]

    g_wdn = _mm(act, dx3b, "tn", F32, name="mm_down_dw")
    red_down, token = reduce_start([g_wdn], [False], "down")
    dact = _mm(dx3b, wdn_b, "nt", F32, name="mm_down_dx", dep=token)
    red_down, token = reduce_mid(red_down, dact)
    dupg, dupv, dfcw_g, dfcw_v = _ffn_act_bwd(up, dact, fcw, d_ff, cb_ff, tc, dep=token)
    dup = [dupg, dupv]
    g_wup = _mm(h2, dup, "tn", F32, name="mm_up_dw")
    red_up, token = reduce_start([g_wup], [True], "up")
    dh2 = _mm(dup, wup_b, "nt", F32, name="mm_up_dx", dep=token)
    red_up, token = reduce_mid(red_up, dh2)
    dx2, dx2b, dg_ffn = _rms_bwd(x2, g_ffn, dh2, dx3, "rms_ffn_bwd", True, dep=token)
    g_wo = _mm(merged, dx2b, "tn", F32, name="mm_o_dw")
    dmerged = _mm(dx2b, wo_b, "nt", F32, name="mm_o_dx")
    dgl, dgs, dyl, dys = _merge_bwd(p, y_lru, y_sc, dmerged, col_gates, tc)
    g_wlo = _mm(y_lru_pre, dyl, "tn", F32, name="mm_lru_out_dw")
    g_wso = _mm(y_sc_pre, dys, "tn", F32, name="mm_sc_out_dw")
    red_mix, token = reduce_start([g_wlo, g_wso, g_wo], [True, True, False], "mix")
    dylp = _mm(dyl, wlo_b, "nt", F32, name="mm_lru_out_dx", dep=token)
    dysp = _mm(dys, wso_b, "nt", F32, name="mm_sc_out_dx")
    red_mix, token = reduce_mid(red_mix, dysp)
    dlx, dlgate, dlcw, dlcb, dwa_bd, dba, dwx_bd, dbx, dlam = _lru_bwd(
        p, hseq, dylp, lcw, lru_conv_b, wa_bd, lru_ba, wx_bd, lru_bx, lru_lambda, d_lru, gc, tc, dep=token)
    dsb, dsc, dsv, dscw = _sc_bwd(p, dysp, scw, col_sc, d_sc, cb_sc, tc)
    dp = [dlx, dlgate, dsb, dsc, dsv, dgl, dgs]
    g_win = _mm(h1, dp, "tn", F32, name="mm_in_dw")
    red_in, token = reduce_start([g_win], [True], "in")
    dh1 = _mm(dp, win_b, "nt", F32, name="mm_in_dx", dep=token)
    grad_x, dg_mix = _rms_bwd(xs, g_mix, dh1, dx2, "rms_mix_bwd", False)

    small_g = [dg_mix, dlcw, dlcb, _diag_blocks(dwa_bd, per_group, head_dim), dba,
               _diag_blocks(dwx_bd, per_group, head_dim), dbx, dlam, dscw, dg_ffn,
               jnp.concatenate([dfcw_g, dfcw_v], axis=1), dg_final]
    small_shapes = [a.shape for a in small_g]
    small_sum = _small_allreduce(_pack(small_g))
    red_in, token = reduce_mid(red_in, small_sum)
    (h_wdn,) = reduce_end(red_down, token)
    (h_wup,) = reduce_end(red_up, token)
    h_wlo, h_wso, h_wo = reduce_end(red_mix, token)
    s_wlo, s_wso, s_wo, s_wup, s_wdn = _exchange("grad_share_a", [h_wlo, h_wso, h_wo, h_wup, h_wdn], 5,
                                                 _share_plan(5))
    early = {1: s_wlo, 2: s_wso, 3: s_wo, 4: s_wup, 5: s_wdn}
    big_out = [None] * n_big
    last = None
    for k, g in early.items():
        big_out[k] = _adamw(big_w[k], g, big_m[k], big_v[k], "adamw_" + big_names[k], dep=last)
        last = big_out[k][1]
    (h_win,) = reduce_end(red_in, last)
    (s_win,) = _exchange("grad_share_b", [h_win], 1, _share_plan(1))
    big_out[0] = _adamw(big_w[0], s_win, big_m[0], big_v[0], "adamw_" + big_names[0])
    sg = _unpack(small_sum, small_shapes)
    for idx in (1, 8, 10):
        nq = sg[idx].shape[1] // N_CHIPS
        sg[idx] = lax.dynamic_slice_in_dim(sg[idx], chip * nq, nq, axis=1)
    small_w = [g_mix, lru_conv_w, lru_conv_b, lru_wa, lru_ba, lru_wx, lru_bx, lru_lambda, sc_conv_w,
               g_ffn, ffn_conv_w, g_final]
    small_m = [m_g_mix, m_lru_conv_w, m_lru_conv_b, m_lru_wa, m_lru_ba, m_lru_wx, m_lru_bx, m_lru_lambda,
               m_sc_conv_w, m_g_ffn, m_ffn_conv_w, m_g_final]
    small_v = [v_g_mix, v_lru_conv_w, v_lru_conv_b, v_lru_wa, v_lru_ba, v_lru_wx, v_lru_bx, v_lru_lambda,
               v_sc_conv_w, v_g_ffn, v_ffn_conv_w, v_g_final]
    sg = [g.reshape(w.shape) for g, w in zip(sg, small_w)]
    w_shapes = [w.shape for w in small_w]
    packed = _adamw(_pack(small_w), _pack(sg), _pack(small_m), _pack(small_v), "adamw_small")
    small_out = [_unpack(pk, w_shapes) for pk in packed]

    order = [(0, 0), (1, 0), (0, 1), (0, 2), (0, 3), (0, 4), (0, 5), (0, 6), (0, 7), (1, 1), (0, 8), (1, 2),
             (1, 3), (0, 9), (1, 4), (0, 10), (1, 5), (0, 11)]
    by_kind = []
    for kind in range(4):
        by_kind.append([big_out[i][kind] if is_big else small_out[kind][i] for is_big, i in order])
    loss = lax.psum(loss_part[0, 0], ("x", "y", "c"))
    return (loss, grad_x.reshape(x.shape), *by_kind[0], *by_kind[1], *by_kind[2], *by_kind[3])
```

```python
import functools
import math

import jax
import jax.numpy as jnp
from jax import lax
from jax.experimental import pallas as pl
from jax.experimental.pallas import tpu as pltpu

F32 = jnp.float32
BF16 = jnp.bfloat16

LANE = 128
SUBLANE = 8
BF16_ROWS = 16
VMEM_BYTES_V7X = 64 * 1024 * 1024
VMEM_BUDGET = VMEM_BYTES_V7X - 8 * 1024 * 1024
MM_VMEM_BUDGET = 42 * 1024 * 1024

EPS = 1e-6
LRU_C = 8.0
ADAM_LR = 0.001
ADAM_B1 = 0.9
ADAM_B2 = 0.999
ADAM_EPS = 1e-08
ADAM_WD = 0.01
ADAM_STEP = 10

N_CHIPS = 4
N_DEV = 8
MESH = pl.DeviceIdType.MESH
ANY = pl.BlockSpec(memory_space=pl.ANY)
VMEM_SPEC = pl.BlockSpec(memory_space=pltpu.VMEM)
HBM_SPEC = pl.BlockSpec(memory_space=pltpu.HBM)
SEM_SPEC = pl.BlockSpec(memory_space=pltpu.SEMAPHORE)
DATAFLOW_EFFECT = pltpu.SideEffectType.DATAFLOW_SIDE_EFFECTING


def _pick(n, cap, mult=LANE):
    best = None
    d = mult
    while d <= min(n, cap):
        if n % d == 0:
            best = d
        d += mult
    return n if best is None else best


def _cparams(semantics, block_bytes):
    limit = min(VMEM_BUDGET, max(32 * 1024 * 1024, int(block_bytes * 1.25) + (4 << 20)))
    return pltpu.CompilerParams(dimension_semantics=semantics, vmem_limit_bytes=limit)


def _nbytes(shape, dtype):
    return math.prod(shape) * jnp.dtype(dtype).itemsize


def _sigmoid(z):
    return 1.0 / (1.0 + jnp.exp(-z))


def _softplus(z):
    e = jnp.exp(-jnp.abs(z))
    u = 1.0 + e
    log1p = jnp.where(u == 1.0, e, jnp.log(u) * (e / (u - 1.0)))
    return jnp.maximum(z, 0.0) + log1p


def _neg_expm1(z):
    small = z * (1.0 + z * (0.5 + z * (1.0 / 6.0 + z * (1.0 / 24.0))))
    return -jnp.where(jnp.abs(z) < 0.03, small, jnp.exp(z) - 1.0)


_GELU_K = math.sqrt(2.0 / math.pi)
_GELU_C = 0.044715


def _gelu_and_grad(z):
    z2 = z * z
    th = jnp.tanh(_GELU_K * (z + _GELU_C * z2 * z))
    val = 0.5 * z * (1.0 + th)
    grad = 0.5 * (1.0 + th) + 0.5 * z * (1.0 - th * th) * (_GELU_K * (1.0 + 3.0 * _GELU_C * z2))
    return val, grad


def _rows_before(cat, k):
    if k == 0:
        return cat[SUBLANE:, :]
    return pltpu.roll(cat, k, 0)[SUBLANE:, :]


def _rows_after(cat, k):
    n = cat.shape[0]
    if k == 0:
        return cat[:n - SUBLANE, :]
    return pltpu.roll(cat, n - k, 0)[:n - SUBLANE, :]


def _conv_fwd(cat, w, width):
    y = _rows_before(cat, width - 1) * w[0:1, :]
    for k in range(1, width):
        y = y + _rows_before(cat, width - 1 - k) * w[k:k + 1, :]
    return y


def _conv_bwd_input(cat, w, width):
    dx = _rows_after(cat, width - 1) * w[0:1, :]
    for k in range(1, width):
        dx = dx + _rows_after(cat, width - 1 - k) * w[k:k + 1, :]
    return dx


def _conv_bwd_weight(dw_ref, dy, catx, width):
    for k in range(width):
        dw_ref[k:k + 1, :] += jnp.sum(dy * _rows_before(catx, width - 1 - k), axis=0, keepdims=True)


def _scan_tiles(a_ref, b_ref, out_ref, carry0, n_rows, reverse):
    cols = a_ref.shape[1]
    row = lax.broadcasted_iota(jnp.int32, (SUBLANE, cols), 0)
    n_tiles = n_rows // SUBLANE

    def step(j, carry):
        tile = (n_tiles - 1 - j) if reverse else j
        off = pl.multiple_of(tile * SUBLANE, SUBLANE)
        a = a_ref[pl.ds(off, SUBLANE), :]
        b = b_ref[pl.ds(off, SUBLANE), :]
        for s in (1, 2, 4):
            if reverse:
                keep = row < SUBLANE - s
                shift = SUBLANE - s
            else:
                keep = row >= s
                shift = s
            a_sh = jnp.where(keep, pltpu.roll(a, shift, 0), 1.0)
            b_sh = jnp.where(keep, pltpu.roll(b, shift, 0), 0.0)
            b = a * b_sh + b
            a = a * a_sh
        out = a * carry + b
        out_ref[pl.ds(off, SUBLANE), :] = out
        return out[0:1, :] if reverse else out[SUBLANE - 1:SUBLANE, :]

    return lax.fori_loop(0, n_tiles, step, carry0)


def _out(shape, dtype):
    return jax.ShapeDtypeStruct(shape, dtype)


def _hbm(x):
    return x


def _dep_args(body, in_specs, operands, *deps):
    deps = [d for d in deps if d is not None]
    if not deps:
        return body, in_specs, operands
    n = len(operands)

    def wrapped(*refs):
        return body(*refs[:n], *refs[n + len(deps):])

    return wrapped, list(in_specs) + [ANY] * len(deps), list(operands) + deps


def _mm(a, b, mode, out_dtype, res=None, name=None, dep=None):
    a_pieces = list(a) if isinstance(a, (list, tuple)) else [a]
    b_pieces = list(b) if isinstance(b, (list, tuple)) else [b]
    assert len(a_pieces) == 1 or (mode == "nt" and len(b_pieces) == 1)
    assert len(b_pieces) == 1 or (mode == "tn" and len(a_pieces) == 1)
    assert all(t.dtype == BF16 for t in a_pieces + b_pieces)
    if mode == "nn":
        (m, k), (k2, n) = a_pieces[0].shape, b_pieces[0].shape
        dims = (((1,), (0,)), ((), ()))
    elif mode == "nt":
        m, k = a_pieces[0].shape[0], sum(t.shape[1] for t in a_pieces)
        n, k2 = b_pieces[0].shape
        dims = (((1,), (1,)), ((), ()))
    else:
        k, m = a_pieces[0].shape
        k2, n = b_pieces[0].shape[0], sum(t.shape[1] for t in b_pieces)
        dims = (((0,), (0,)), ((), ()))
    assert k == k2
    k_unit = math.gcd(*[t.shape[1] for t in a_pieces]) if len(a_pieces) > 1 else k
    n_unit = math.gcd(*[t.shape[1] for t in b_pieces]) if len(b_pieces) > 1 else n
    out_bytes = jnp.dtype(out_dtype).itemsize
    pieces = len(a_pieces) > 1 or len(b_pieces) > 1
    bm = _pick(m, 1024)
    bn = _pick(n_unit, 1024)
    bk = _pick(k_unit, 2048) if pieces else k

    def est(bm_, bn_, bk_):
        e = 2 * (len(a_pieces) * bm_ * bk_ + len(b_pieces) * bk_ * bn_) * 2 + 2 * bm_ * bn_ * out_bytes
        if k // bk_ > 1:
            e += bm_ * bn_ * 4
        if res is not None:
            e += 2 * bm_ * bn_ * 4
        return e

    for shrink_n, floor in ((True, 512), (False, 512), (True, 256), (False, 256)):
        while est(bm, bn, bk) > MM_VMEM_BUDGET:
            if shrink_n and bn > floor and bn % 2 == 0 and n_unit % (bn // 2) == 0:
                bn //= 2
            elif not shrink_n and bm > floor and bm % 2 == 0 and m % (bm // 2) == 0:
                bm //= 2
            else:
                break
    while est(bm, bn, bk) > MM_VMEM_BUDGET and bk % (2 * LANE) == 0 and k_unit % (bk // 2) == 0:
        bk //= 2
    nk = k // bk

    a_ranges, b_ranges, off = [], [], 0
    for t in a_pieces:
        cnt = (t.shape[0] if mode == "tn" else t.shape[1]) // bk
        a_ranges.append((off, cnt))
        off += cnt
    off = 0
    for t in b_pieces:
        cnt = (t.shape[0] if mode == "nt" else t.shape[1]) // bn
        b_ranges.append((off, cnt))
        off += cnt

    def walk(pos, rng, pieces):
        return pos if pieces == 1 else jnp.clip(pos - rng[0], 0, rng[1] - 1)

    in_specs = []
    for rng in a_ranges:
        if mode == "tn":
            in_specs.append(pl.BlockSpec((bk, bm), lambda i, j, kk: (kk, i)))
        else:
            in_specs.append(pl.BlockSpec(
                (bm, bk), lambda i, j, kk, rng=rng: (i, walk(kk, rng, len(a_pieces)))))
    for rng in b_ranges:
        if mode == "nt":
            in_specs.append(pl.BlockSpec((bn, bk), lambda i, j, kk: (j, kk)))
        else:
            def b_map(i, j, kk, rng=rng):
                if len(b_pieces) == 1:
                    return kk, j
                active = (j >= rng[0]) & (j < rng[0] + rng[1])
                return jnp.where(active, kk, 0), walk(j, rng, len(b_pieces))
            in_specs.append(pl.BlockSpec((bk, bn), b_map))
    o_spec = pl.BlockSpec((bm, bn), lambda i, j, kk: (i, j))
    operands = [_hbm(t) for t in a_pieces + b_pieces]
    if res is not None:
        in_specs.append(o_spec)
        operands.append(_hbm(res))
    has_res = res is not None
    na, nb = len(a_pieces), len(b_pieces)

    def body(*refs):
        a_refs, b_refs = refs[:na], refs[na:na + nb]
        res_ref = refs[na + nb] if has_res else None
        o_ref = refs[na + nb + has_res]
        acc_ref = refs[-1] if nk > 1 else None
        kk = pl.program_id(2)

        def accumulate(part, first_possible=True):
            if nk == 1:
                if has_res:
                    part = part + res_ref[...]
                o_ref[...] = part.astype(o_ref.dtype)
                return
            if first_possible:
                @pl.when(kk == 0)
                def _():
                    acc_ref[...] = part

                @pl.when(kk > 0)
                def _():
                    acc_ref[...] += part
            else:
                acc_ref[...] += part

        def dot(a_ref, b_ref):
            return lax.dot_general(a_ref[...], b_ref[...], dims, preferred_element_type=F32)

        if na > 1:
            for a_ref, (start, cnt) in zip(a_refs, a_ranges):
                @pl.when((kk >= start) & (kk < start + cnt))
                def _(a_ref=a_ref, start=start):
                    accumulate(dot(a_ref, b_refs[0]), first_possible=start == 0)
        elif nb > 1:
            j = pl.program_id(1)
            for b_ref, (start, cnt) in zip(b_refs, b_ranges):
                @pl.when((j >= start) & (j < start + cnt))
                def _(b_ref=b_ref):
                    accumulate(dot(a_refs[0], b_ref))
        else:
            accumulate(dot(a_refs[0], b_refs[0]))

        if nk > 1:
            @pl.when(kk == nk - 1)
            def _():
                total = acc_ref[...]
                if has_res:
                    total = total + res_ref[...]
                o_ref[...] = total.astype(o_ref.dtype)

    scratch = [pltpu.VMEM((bm, bn), F32)] if nk > 1 else []
    body, in_specs, operands = _dep_args(body, in_specs, operands, dep)
    return pl.pallas_call(
        body,
        out_shape=_out((m, n), out_dtype),
        grid=(m // bm, n // bn, nk),
        in_specs=in_specs,
        out_specs=o_spec,
        scratch_shapes=scratch,
        compiler_params=_cparams(("parallel", "parallel", "arbitrary"), est(bm, bn, bk)),
        name=name,
    )(*operands)


def _rms_fwd(x, g, name, dep=None):
    t, d = x.shape
    tb = _pick(t, 512, SUBLANE)

    def body(x_ref, g_ref, h_ref):
        xv = x_ref[...]
        r = lax.rsqrt(jnp.mean(xv * xv, axis=-1, keepdims=True) + EPS)
        h_ref[...] = ((xv * r) * g_ref[...]).astype(BF16)

    blk = pl.BlockSpec((tb, d), lambda i: (i, 0))
    body, in_specs, operands = _dep_args(
        body, [blk, pl.BlockSpec((1, d), lambda i: (0, 0))], [_hbm(x), g.reshape(1, d)], dep)
    return pl.pallas_call(
        body,
        out_shape=_out((t, d), BF16),
        grid=(t // tb,),
        in_specs=in_specs,
        out_specs=blk,
        compiler_params=_cparams(("parallel",), 2 * tb * d * 6),
        name=name,
    )(*operands)


def _rms_bwd(x, g, dh, dres, name, want_bf16, dep=None):
    t, d = x.shape
    tb = _pick(t, 256, SUBLANE)

    def body(x_ref, g_ref, dh_ref, dres_ref, *outs):
        dx_ref, dg_ref = outs[0], outs[-1]
        xv = x_ref[...]
        r = lax.rsqrt(jnp.mean(xv * xv, axis=-1, keepdims=True) + EPS)
        xhat = xv * r
        dhv = dh_ref[...]
        dxhat = dhv * g_ref[...]
        dx = dres_ref[...] + r * (dxhat - xhat * jnp.mean(dxhat * xhat, axis=-1, keepdims=True))
        dx_ref[...] = dx
        if want_bf16:
            outs[1][...] = dx.astype(BF16)

        @pl.when(pl.program_id(0) == 0)
        def _():
            dg_ref[...] = jnp.zeros_like(dg_ref)

        dg_ref[...] += jnp.sum(dhv * xhat, axis=0, keepdims=True)

    blk = pl.BlockSpec((tb, d), lambda i: (i, 0))
    row = pl.BlockSpec((1, d), lambda i: (0, 0))
    out_shape = [_out((t, d), F32)]
    out_specs = [blk]
    if want_bf16:
        out_shape.append(_out((t, d), BF16))
        out_specs.append(blk)
    out_shape.append(jax.ShapeDtypeStruct((1, d), F32))
    out_specs.append(row)
    body, in_specs, operands = _dep_args(
        body, [blk, row, blk, blk], [_hbm(x), g.reshape(1, d), _hbm(dh), _hbm(dres)], dep)
    return pl.pallas_call(
        body,
        out_shape=out_shape,
        grid=(t // tb,),
        in_specs=in_specs,
        out_specs=out_specs,
        compiler_params=_cparams(("arbitrary",), 2 * tb * d * 18),
        name=name,
    )(*operands)


def _loss_head(x3, g, target):
    t, d = x3.shape
    tb = _pick(t, 256, SUBLANE)

    def body(x_ref, g_ref, t_ref, loss_ref, dx_ref, dxb_ref, dg_ref):
        xv = x_ref[...]
        gv = g_ref[...]
        r = lax.rsqrt(jnp.mean(xv * xv, axis=-1, keepdims=True) + EPS)
        xhat = xv * r
        err = xhat * gv - t_ref[...]
        dy = err * (1.0 / d)
        dxhat = dy * gv
        dx = r * (dxhat - xhat * jnp.mean(dxhat * xhat, axis=-1, keepdims=True))
        dx_ref[...] = dx
        dxb_ref[...] = dx.astype(BF16)

        @pl.when(pl.program_id(0) == 0)
        def _():
            dg_ref[...] = jnp.zeros_like(dg_ref)
            loss_ref[...] = jnp.zeros_like(loss_ref)

        dg_ref[...] += jnp.sum(dy * xhat, axis=0, keepdims=True)
        per_token = jnp.mean(err * err, axis=-1, keepdims=True)
        loss_ref[...] += 0.5 * jnp.sum(per_token, axis=0, keepdims=True)

    blk = pl.BlockSpec((tb, d), lambda i: (i, 0))
    row = pl.BlockSpec((1, d), lambda i: (0, 0))
    return pl.pallas_call(
        body,
        out_shape=[jax.ShapeDtypeStruct((1, 1), F32), _out((t, d), F32),
                   _out((t, d), BF16), jax.ShapeDtypeStruct((1, d), F32)],
        grid=(t // tb,),
        in_specs=[blk, row, blk],
        out_specs=[pl.BlockSpec((1, 1), lambda i: (0, 0)), blk, blk, row],
        compiler_params=_cparams(("arbitrary",), 2 * tb * d * 14),
        name="loss_head",
    )(_hbm(x3), g.reshape(1, d), _hbm(target))


def _lru_gates(xc, wa, ba, wx, bx, lam):
    nn = (((1,), (0,)), ((), ()))
    xcb = xc.astype(BF16)
    r = _sigmoid(lax.dot_general(xcb, wa, nn, preferred_element_type=F32) + ba)
    i = _sigmoid(lax.dot_general(xcb, wx, nn, preferred_element_type=F32) + bx)
    cl = -LRU_C * _softplus(-lam)
    log_a = cl * r
    a = jnp.exp(log_a)
    one_minus_a2 = _neg_expm1(2.0 * log_a)
    return xcb, r, i, a, one_minus_a2, cl


def _lru_fwd(p, conv_w, conv_b, wa_bd, ba, wx_bd, bx, lam, d_lru, gc, tc):
    t = p.shape[0]
    ng = d_lru // gc
    nt = t // tc
    width = conv_w.shape[0]

    def body(lx_ref, gate_ref, cw_ref, cb_ref, wa_ref, ba_ref, wx_ref, bx_ref, lam_ref,
             y_ref, h_ref, halo, hcar, a_s, u_s):
        @pl.when(pl.program_id(1) == 0)
        def _():
            halo[...] = jnp.zeros_like(halo)
            hcar[...] = jnp.zeros_like(hcar)

        x = lx_ref[...]
        cat = jnp.concatenate([halo[...], x], axis=0)
        halo[...] = x[tc - SUBLANE:, :]
        xc = _conv_fwd(cat, cw_ref[...], width) + cb_ref[...]
        _, r, i, a, om, _ = _lru_gates(xc, wa_ref[...], ba_ref[...], wx_ref[...], bx_ref[...], lam_ref[...])
        a_s[...] = a
        u_s[...] = jnp.sqrt(om) * (i * xc)
        hcar[0:1, :] = _scan_tiles(a_s, u_s, h_ref, hcar[0:1, :], tc, reverse=False)
        gl, _ = _gelu_and_grad(gate_ref[...])
        y_ref[...] = (gl * h_ref[...]).astype(BF16)

    blk = lambda off: pl.BlockSpec((tc, gc), lambda g, s, off=off: (s, off + g))
    rowv = lambda rows: pl.BlockSpec((rows, gc), lambda g, s: (0, g))
    wspec = pl.BlockSpec((None, gc, gc), lambda g, s: (g, 0, 0))
    out_blk = pl.BlockSpec((tc, gc), lambda g, s: (s, g))
    return pl.pallas_call(
        body,
        out_shape=[_out((t, d_lru), BF16), _out((t, d_lru), F32)],
        grid=(ng, nt),
        in_specs=[blk(0), blk(ng), rowv(width), rowv(1), wspec, rowv(1), wspec, rowv(1), rowv(1)],
        out_specs=[out_blk, out_blk],
        scratch_shapes=[pltpu.VMEM((SUBLANE, gc), F32), pltpu.VMEM((SUBLANE, gc), F32),
                        pltpu.VMEM((tc, gc), F32), pltpu.VMEM((tc, gc), F32)],
        compiler_params=_cparams(("parallel", "arbitrary"), 40 * tc * gc * 4),
        name="lru_fwd",
    )(_hbm(p), _hbm(p), conv_w, conv_b.reshape(1, -1), wa_bd, ba.reshape(1, -1), wx_bd, bx.reshape(1, -1),
      lam.reshape(1, -1))


def _lru_bwd(p, hseq, dyp, conv_w, conv_b, wa_bd, ba, wx_bd, bx, lam, d_lru, gc, tc, dep=None):
    t = p.shape[0]
    ng = d_lru // gc
    nt = t // tc
    width = conv_w.shape[0]
    halo_blocks = tc // SUBLANE
    nn = (((1,), (0,)), ((), ()))
    nt_dims = (((1,), (1,)), ((), ()))
    tn_dims = (((0,), (0,)), ((), ()))

    def body(lx_ref, lxh_ref, gate_ref, h_ref, hh_ref, dyp_ref,
             cw_ref, cb_ref, wa_ref, ba_ref, wx_ref, bx_ref, lam_ref,
             dlx_ref, dgate_ref, dcw_ref, dcb_ref, dwa_ref, dba_ref, dwx_ref, dbx_ref, dlam_ref,
             nxt_dxc, nxt_a, nxt_g, al_s, b_s, g_s):
        s = pl.program_id(1)
        first_chunk = s == nt - 1

        @pl.when(s == 0)
        def _():
            nxt_dxc[...] = jnp.zeros_like(nxt_dxc)
            nxt_a[...] = jnp.zeros_like(nxt_a)
            nxt_g[...] = jnp.zeros_like(nxt_g)
            for ref in (dcw_ref, dcb_ref, dwa_ref, dba_ref, dwx_ref, dbx_ref, dlam_ref):
                ref[...] = jnp.zeros_like(ref)

        keep = jnp.where(first_chunk, 0.0, 1.0)
        x = lx_ref[...]
        catx = jnp.concatenate([lxh_ref[...] * keep, x], axis=0)
        cw = cw_ref[...]
        xc = _conv_fwd(catx, cw, width) + cb_ref[...]
        wa = wa_ref[...]
        wx = wx_ref[...]
        lam_v = lam_ref[...]
        xcb, r, i, a, om, cl = _lru_gates(xc, wa, ba_ref[...], wx, bx_ref[...], lam_v)
        mult = jnp.sqrt(om)

        h = h_ref[...]
        hprev = _rows_before(jnp.concatenate([hh_ref[...] * keep, h], axis=0), 1)
        gl, dgl = _gelu_and_grad(gate_ref[...])
        dyp_v = dyp_ref[...]
        dgate_ref[...] = (dyp_v * h * dgl).astype(BF16)

        al_s[...] = _rows_after(jnp.concatenate([a, nxt_a[...]], axis=0), 1)
        b_s[...] = dyp_v * gl
        nxt_g[0:1, :] = _scan_tiles(al_s, b_s, g_s, nxt_g[0:1, :], tc, reverse=True)
        nxt_a[...] = a[0:SUBLANE, :]
        du = g_s[...]

        da = du * hprev
        dmult = du * (i * xc)
        di = du * mult * xc
        dxc = du * mult * i
        dlog_a = da * a - dmult * (a * a / mult)
        dlam_ref[...] += jnp.sum(dlog_a * r, axis=0, keepdims=True) * (LRU_C * _sigmoid(-lam_v))
        dza = (dlog_a * cl) * r * (1.0 - r)
        dzx = di * i * (1.0 - i)
        dba_ref[...] += jnp.sum(dza, axis=0, keepdims=True)
        dbx_ref[...] += jnp.sum(dzx, axis=0, keepdims=True)
        dzab = dza.astype(BF16)
        dzxb = dzx.astype(BF16)
        dwa_ref[...] += lax.dot_general(xcb, dzab, tn_dims, preferred_element_type=F32)
        dwx_ref[...] += lax.dot_general(xcb, dzxb, tn_dims, preferred_element_type=F32)
        dxc = dxc + lax.dot_general(dzab, wa, nt_dims, preferred_element_type=F32)
        dxc = dxc + lax.dot_general(dzxb, wx, nt_dims, preferred_element_type=F32)
        dcb_ref[...] += jnp.sum(dxc, axis=0, keepdims=True)
        _conv_bwd_weight(dcw_ref, dxc, catx, width)
        catd = jnp.concatenate([dxc, nxt_dxc[...]], axis=0)
        dlx_ref[...] = _conv_bwd_input(catd, cw, width).astype(BF16)
        nxt_dxc[...] = dxc[0:SUBLANE, :]

    rev = lambda s: nt - 1 - s
    blk = lambda off: pl.BlockSpec((tc, gc), lambda g, s, off=off: (rev(s), off + g))
    halo = lambda off: pl.BlockSpec(
        (SUBLANE, gc), lambda g, s, off=off: (jnp.maximum(rev(s) * halo_blocks - 1, 0), off + g))
    rowv = lambda rows: pl.BlockSpec((rows, gc), lambda g, s: (0, g))
    wspec = pl.BlockSpec((None, gc, gc), lambda g, s: (g, 0, 0))
    out_blk = pl.BlockSpec((tc, gc), lambda g, s: (rev(s), g))
    vec = lambda rows: jax.ShapeDtypeStruct((rows, d_lru), F32)
    wshape = jax.ShapeDtypeStruct((ng, gc, gc), F32)
    body, in_specs, operands = _dep_args(
        body,
        [blk(0), halo(0), blk(ng), blk(0), halo(0), blk(0),
         rowv(width), rowv(1), wspec, rowv(1), wspec, rowv(1), rowv(1)],
        [_hbm(p), _hbm(p), _hbm(p), _hbm(hseq), _hbm(hseq), _hbm(dyp),
         conv_w, conv_b.reshape(1, -1), wa_bd, ba.reshape(1, -1), wx_bd,
         bx.reshape(1, -1), lam.reshape(1, -1)], dep)
    return pl.pallas_call(
        body,
        out_shape=[_out((t, d_lru), BF16), _out((t, d_lru), BF16),
                   vec(width), vec(1), wshape, vec(1), wshape, vec(1), vec(1)],
        grid=(ng, nt),
        in_specs=in_specs,
        out_specs=[out_blk, out_blk, rowv(width), rowv(1), wspec, rowv(1), wspec, rowv(1), rowv(1)],
        scratch_shapes=[pltpu.VMEM((SUBLANE, gc), F32), pltpu.VMEM((SUBLANE, gc), F32),
                        pltpu.VMEM((SUBLANE, gc), F32),
                        pltpu.VMEM((tc, gc), F32), pltpu.VMEM((tc, gc), F32), pltpu.VMEM((tc, gc), F32)],
        compiler_params=_cparams(("parallel", "arbitrary"), 80 * tc * gc * 4),
        name="lru_bwd",
    )(*operands)


def _sc_fwd(p, conv_w, col0, d_sc, cb, tc):
    t = p.shape[0]
    nc = d_sc // cb
    nt = t // tc
    width = conv_w.shape[0]
    base = col0 // cb

    def body(b_ref, c_ref, v_ref, w_ref, y_ref, halo):
        @pl.when(pl.program_id(1) == 0)
        def _():
            halo[...] = jnp.zeros_like(halo)

        cv = c_ref[...] * v_ref[...]
        cat = jnp.concatenate([halo[...], cv], axis=0)
        halo[...] = cv[tc - SUBLANE:, :]
        y_ref[...] = (b_ref[...] * _conv_fwd(cat, w_ref[...], width)).astype(BF16)

    blk = lambda slab: pl.BlockSpec((tc, cb), lambda j, s, slab=slab: (s, base + slab * nc + j))
    return pl.pallas_call(
        body,
        out_shape=_out((t, d_sc), BF16),
        grid=(nc, nt),
        in_specs=[blk(0), blk(1), blk(2), pl.BlockSpec((width, cb), lambda j, s: (0, j))],
        out_specs=pl.BlockSpec((tc, cb), lambda j, s: (s, j)),
        scratch_shapes=[pltpu.VMEM((SUBLANE, cb), F32)],
        compiler_params=_cparams(("parallel", "arbitrary"), 20 * tc * cb * 4),
        name="sc_fwd",
    )(_hbm(p), _hbm(p), _hbm(p), conv_w)


def _sc_bwd(p, dyp, conv_w, col0, d_sc, cb, tc):
    t = p.shape[0]
    nc = d_sc // cb
    nt = t // tc
    width = conv_w.shape[0]
    base = col0 // cb
    halo_blocks = tc // SUBLANE

    def body(b_ref, c_ref, ch_ref, v_ref, vh_ref, dyp_ref, w_ref,
             db_ref, dc_ref, dv_ref, dw_ref, nxt_dq):
        s = pl.program_id(1)

        @pl.when(s == 0)
        def _():
            nxt_dq[...] = jnp.zeros_like(nxt_dq)
            dw_ref[...] = jnp.zeros_like(dw_ref)

        keep = jnp.where(s == nt - 1, 0.0, 1.0)
        cvals = c_ref[...]
        vvals = v_ref[...]
        w = w_ref[...]
        catcv = jnp.concatenate([ch_ref[...] * vh_ref[...] * keep, cvals * vvals], axis=0)
        q = _conv_fwd(catcv, w, width)
        dyp_v = dyp_ref[...]
        db_ref[...] = (dyp_v * q).astype(BF16)
        dq = dyp_v * b_ref[...]
        _conv_bwd_weight(dw_ref, dq, catcv, width)
        dcv = _conv_bwd_input(jnp.concatenate([dq, nxt_dq[...]], axis=0), w, width)
        nxt_dq[...] = dq[0:SUBLANE, :]
        dc_ref[...] = (dcv * vvals).astype(BF16)
        dv_ref[...] = (dcv * cvals).astype(BF16)

    rev = lambda s: nt - 1 - s
    blk = lambda slab: pl.BlockSpec((tc, cb), lambda j, s, slab=slab: (rev(s), base + slab * nc + j))
    halo = lambda slab: pl.BlockSpec(
        (SUBLANE, cb),
        lambda j, s, slab=slab: (jnp.maximum(rev(s) * halo_blocks - 1, 0), base + slab * nc + j))
    out_blk = pl.BlockSpec((tc, cb), lambda j, s: (rev(s), j))
    wblk = pl.BlockSpec((width, cb), lambda j, s: (0, j))
    act = _out((t, d_sc), BF16)
    return pl.pallas_call(
        body,
        out_shape=[act, act, act, jax.ShapeDtypeStruct((width, d_sc), F32)],
        grid=(nc, nt),
        in_specs=[blk(0), blk(1), halo(1), blk(2), halo(2), out_blk, wblk],
        out_specs=[out_blk, out_blk, out_blk, wblk],
        scratch_shapes=[pltpu.VMEM((SUBLANE, cb), F32)],
        compiler_params=_cparams(("parallel", "arbitrary"), 30 * tc * cb * 4),
        name="sc_bwd",
    )(_hbm(p), _hbm(p), _hbm(p), _hbm(p), _hbm(p), _hbm(dyp), conv_w)


def _merge_fwd(p, y_lru, y_sc, col0, tc):
    t, d = y_lru.shape
    cb = _pick(math.gcd(d, col0), 1024)
    nc = d // cb
    base = col0 // cb

    def body(gl_ref, gs_ref, yl_ref, ys_ref, o_ref):
        o_ref[...] = (_sigmoid(gl_ref[...]) * yl_ref[...] + _sigmoid(gs_ref[...]) * ys_ref[...]).astype(BF16)

    gate = lambda slab: pl.BlockSpec((tc, cb), lambda s, j, slab=slab: (s, base + slab * nc + j))
    blk = pl.BlockSpec((tc, cb), lambda s, j: (s, j))
    return pl.pallas_call(
        body,
        out_shape=_out((t, d), BF16),
        grid=(t // tc, nc),
        in_specs=[gate(0), gate(1), blk, blk],
        out_specs=blk,
        compiler_params=_cparams(("parallel", "parallel"), 2 * tc * cb * 20),
        name="merge_fwd",
    )(_hbm(p), _hbm(p), _hbm(y_lru), _hbm(y_sc))


def _merge_bwd(p, y_lru, y_sc, dmerged, col0, tc):
    t, d = y_lru.shape
    cb = _pick(math.gcd(d, col0), 1024)
    nc = d // cb
    base = col0 // cb

    def body(gl_ref, gs_ref, yl_ref, ys_ref, dm_ref, dgl_ref, dgs_ref, dyl_ref, dys_ref):
        dm = dm_ref[...]
        sl = _sigmoid(gl_ref[...])
        ss = _sigmoid(gs_ref[...])
        dgl_ref[...] = (dm * yl_ref[...] * (sl * (1.0 - sl))).astype(BF16)
        dgs_ref[...] = (dm * ys_ref[...] * (ss * (1.0 - ss))).astype(BF16)
        dyl_ref[...] = (dm * sl).astype(BF16)
        dys_ref[...] = (dm * ss).astype(BF16)

    gate = lambda slab: pl.BlockSpec((tc, cb), lambda s, j, slab=slab: (s, base + slab * nc + j))
    blk = pl.BlockSpec((tc, cb), lambda s, j: (s, j))
    act = _out((t, d), BF16)
    return pl.pallas_call(
        body,
        out_shape=[act, act, act, act],
        grid=(t // tc, nc),
        in_specs=[gate(0), gate(1), blk, blk, blk],
        out_specs=[blk, blk, blk, blk],
        compiler_params=_cparams(("parallel", "parallel"), 2 * tc * cb * 28),
        name="merge_bwd",
    )(_hbm(p), _hbm(p), _hbm(y_lru), _hbm(y_sc), _hbm(dmerged))


def _ffn_act_fwd(up, conv_w, d_ff, cb, tc):
    t = up.shape[0]
    nc = d_ff // cb
    nt = t // tc
    width = conv_w.shape[0]

    def body(g_ref, v_ref, wg_ref, wv_ref, o_ref, halo_g, halo_v):
        @pl.when(pl.program_id(1) == 0)
        def _():
            halo_g[...] = jnp.zeros_like(halo_g)
            halo_v[...] = jnp.zeros_like(halo_v)

        g = g_ref[...]
        v = v_ref[...]
        ug = _conv_fwd(jnp.concatenate([halo_g[...], g], axis=0), wg_ref[...], width)
        uv = _conv_fwd(jnp.concatenate([halo_v[...], v], axis=0), wv_ref[...], width)
        halo_g[...] = g[tc - SUBLANE:, :]
        halo_v[...] = v[tc - SUBLANE:, :]
        o_ref[...] = (ug * _sigmoid(ug) * uv).astype(BF16)

    blk = lambda half: pl.BlockSpec((tc, cb), lambda j, s, half=half: (s, half * nc + j))
    wblk = lambda half: pl.BlockSpec((width, cb), lambda j, s, half=half: (0, half * nc + j))
    return pl.pallas_call(
        body,
        out_shape=_out((t, d_ff), BF16),
        grid=(nc, nt),
        in_specs=[blk(0), blk(1), wblk(0), wblk(1)],
        out_specs=pl.BlockSpec((tc, cb), lambda j, s: (s, j)),
        scratch_shapes=[pltpu.VMEM((SUBLANE, cb), F32), pltpu.VMEM((SUBLANE, cb), F32)],
        compiler_params=_cparams(("parallel", "arbitrary"), 24 * tc * cb * 4),
        name="ffn_act_fwd",
    )(_hbm(up), _hbm(up), conv_w, conv_w)


def _ffn_act_bwd(up, dact, conv_w, d_ff, cb, tc, dep=None):
    t = up.shape[0]
    nc = d_ff // cb
    nt = t // tc
    width = conv_w.shape[0]
    halo_blocks = tc // SUBLANE

    def body(g_ref, gh_ref, v_ref, vh_ref, da_ref, wg_ref, wv_ref,
             dg_ref, dv_ref, dwg_ref, dwv_ref, nxt_g, nxt_v):
        s = pl.program_id(1)

        @pl.when(s == 0)
        def _():
            nxt_g[...] = jnp.zeros_like(nxt_g)
            nxt_v[...] = jnp.zeros_like(nxt_v)
            dwg_ref[...] = jnp.zeros_like(dwg_ref)
            dwv_ref[...] = jnp.zeros_like(dwv_ref)

        keep = jnp.where(s == nt - 1, 0.0, 1.0)
        wg = wg_ref[...]
        wv = wv_ref[...]
        catg = jnp.concatenate([gh_ref[...] * keep, g_ref[...]], axis=0)
        catv = jnp.concatenate([vh_ref[...] * keep, v_ref[...]], axis=0)
        ug = _conv_fwd(catg, wg, width)
        uv = _conv_fwd(catv, wv, width)
        sg = _sigmoid(ug)
        da = da_ref[...]
        dug = da * uv * (sg * (1.0 + ug * (1.0 - sg)))
        duv = da * (ug * sg)
        _conv_bwd_weight(dwg_ref, dug, catg, width)
        _conv_bwd_weight(dwv_ref, duv, catv, width)
        dg_ref[...] = _conv_bwd_input(jnp.concatenate([dug, nxt_g[...]], axis=0), wg, width).astype(BF16)
        dv_ref[...] = _conv_bwd_input(jnp.concatenate([duv, nxt_v[...]], axis=0), wv, width).astype(BF16)
        nxt_g[...] = dug[0:SUBLANE, :]
        nxt_v[...] = duv[0:SUBLANE, :]

    rev = lambda s: nt - 1 - s
    blk = lambda half: pl.BlockSpec((tc, cb), lambda j, s, half=half: (rev(s), half * nc + j))
    halo = lambda half: pl.BlockSpec(
        (SUBLANE, cb), lambda j, s, half=half: (jnp.maximum(rev(s) * halo_blocks - 1, 0), half * nc + j))
    wblk = lambda half: pl.BlockSpec((width, cb), lambda j, s, half=half: (0, half * nc + j))
    out_blk = pl.BlockSpec((tc, cb), lambda j, s: (rev(s), j))
    wout = pl.BlockSpec((width, cb), lambda j, s: (0, j))
    act = _out((t, d_ff), BF16)
    wshape = jax.ShapeDtypeStruct((width, d_ff), F32)
    body, in_specs, operands = _dep_args(
        body, [blk(0), halo(0), blk(1), halo(1), out_blk, wblk(0), wblk(1)],
        [_hbm(up), _hbm(up), _hbm(up), _hbm(up), _hbm(dact), conv_w, conv_w], dep)
    return pl.pallas_call(
        body,
        out_shape=[act, act, wshape, wshape],
        grid=(nc, nt),
        in_specs=in_specs,
        out_specs=[out_blk, out_blk, wout, wout],
        scratch_shapes=[pltpu.VMEM((SUBLANE, cb), F32), pltpu.VMEM((SUBLANE, cb), F32)],
        compiler_params=_cparams(("parallel", "arbitrary"), 40 * tc * cb * 4),
        name="ffn_act_bwd",
    )(*operands)


def _mesh_pos():
    x, y, c = lax.axis_index("x"), lax.axis_index("y"), lax.axis_index("c")
    return x, y, c


def _other_chips(x, y):
    return [(1 - x, y), (x, 1 - y), (1 - x, 1 - y)]


def _cast_place(w, chip, col_sharded, name):
    r, cdim = w.shape
    full = (r, cdim * N_CHIPS) if col_sharded else (r * N_CHIPS, cdim)
    rb = _pick(r, max(BF16_ROWS, (512 * 1024) // cdim), BF16_ROWS)
    nb = r // rb

    def body(chip_ref, w_ref, o_ref):
        o_ref[...] = w_ref[...].astype(BF16)

    if col_sharded:
        out_map = lambda i, chip_ref: (i, chip_ref[0])
    else:
        out_map = lambda i, chip_ref: (chip_ref[0] * nb + i, 0)
    grid_spec = pltpu.PrefetchScalarGridSpec(
        num_scalar_prefetch=1,
        grid=(nb,),
        in_specs=[pl.BlockSpec((rb, cdim), lambda i, chip_ref: (i, 0))],
        out_specs=pl.BlockSpec((rb, cdim), out_map),
    )
    return pl.pallas_call(
        body,
        out_shape=jax.ShapeDtypeStruct(full, BF16),
        grid_spec=grid_spec,
        compiler_params=_cparams(("parallel",), 2 * rb * cdim * 6),
        name=name,
    )(chip, _hbm(w))


def _remote(src, dst, send_sems, recv_sems, idx, to):
    return pltpu.make_async_remote_copy(
        src_ref=src, dst_ref=dst, send_sem=send_sems.at[idx], recv_sem=recv_sems.at[idx],
        device_id=to, device_id_type=MESH)


def _exchange(name, arrays, n_sems, plan):
    n = len(arrays)

    def body(*refs):
        bufs = refs[n:2 * n]
        send_sems, recv_sems = refs[2 * n:]
        sends, arrivals = plan(bufs, send_sems, recv_sems)
        for cp in sends:
            cp.start()
        for cp in arrivals:
            cp.wait_recv()
        for cp in sends:
            cp.wait_send()

    outs = pl.pallas_call(
        body,
        out_shape=[jax.ShapeDtypeStruct(a.shape, a.dtype) for a in arrays],
        in_specs=[ANY] * n,
        out_specs=[ANY] * n,
        input_output_aliases={k: k for k in range(n)},
        scratch_shapes=[pltpu.SemaphoreType.DMA((n_sems,)), pltpu.SemaphoreType.DMA((n_sems,))],
        name=name,
    )(*arrays)
    return list(outs)


def _exchange_start(name, arrays, n_sems, plan, after=None):
    n = len(arrays)
    n_in = n + (after is not None)

    def body(*refs):
        bufs = refs[:n]
        send_sems, recv_sems = refs[n_in], refs[n_in + 1]
        token = refs[-1]
        sends, _ = plan(bufs, send_sems, recv_sems)
        for cp in sends:
            cp.start()
        token[...] = jnp.zeros_like(token)

    out = pl.pallas_call(
        body,
        out_shape=(pltpu.SemaphoreType.DMA((n_sems,)), pltpu.SemaphoreType.DMA((n_sems,)),
                   *[pltpu.HBM(a.shape, a.dtype) for a in arrays],
                   jax.ShapeDtypeStruct((SUBLANE, LANE), F32)),
        in_specs=[HBM_SPEC] * n + [ANY] * (n_in - n),
        out_specs=(SEM_SPEC, SEM_SPEC, *[HBM_SPEC] * n, VMEM_SPEC),
        input_output_aliases={k: 2 + k for k in range(n)},
        compiler_params=pltpu.CompilerParams(has_side_effects=DATAFLOW_EFFECT),
        name=name,
    )(*[pltpu.with_memory_space_constraint(a, pltpu.HBM) for a in arrays], *([after] if after is not None else []))
    return out[0], out[1], list(out[2:2 + n]), out[-1]


def _exchange_wait(name, arrays, send_sems, recv_sems, after, plan):
    n = len(arrays)

    def body(*refs):
        bufs = refs[:n]
        sends, arrivals = plan(bufs, refs[n], refs[n + 1])
        for cp in arrivals:
            cp.wait_recv()
        for cp in sends:
            cp.wait_send()

    outs = pl.pallas_call(
        body,
        out_shape=[pltpu.HBM(a.shape, a.dtype) for a in arrays],
        in_specs=[HBM_SPEC] * n + [SEM_SPEC, SEM_SPEC, ANY],
        out_specs=[HBM_SPEC] * n,
        input_output_aliases={k: k for k in range(n)},
        compiler_params=pltpu.CompilerParams(has_side_effects=DATAFLOW_EFFECT),
        name=name,
    )(*arrays, send_sems, recv_sems, after)
    return list(outs)


def _half_block(ref, shard_shape, col_sharded, chip, half):
    r, cdim = shard_shape
    h = r // 2
    if col_sharded:
        return ref.at[pl.ds(pl.multiple_of(half * h, BF16_ROWS), h),
                      pl.ds(pl.multiple_of(chip * cdim, LANE), cdim)]
    return ref.at[pl.ds(pl.multiple_of(chip * r + half * h, BF16_ROWS), h), :]


def _gather_plan(shard_shapes, col_sharded, ks):
    def plan(bufs, send_sems, recv_sems):
        x, y, c = _mesh_pos()
        sends, arrivals = [], []
        for ref, k in zip(bufs, ks):
            mine = _half_block(ref, shard_shapes[k], col_sharded[k], 2 * x + y, c)
            for j, (px, py) in enumerate(_other_chips(x, y)):
                landed = _half_block(ref, shard_shapes[k], col_sharded[k], 2 * px + py, c)
                sends.append(_remote(mine, mine, send_sems, recv_sems, 3 * k + j, (px, py, c)))
                arrivals.append(_remote(landed, landed, send_sems, recv_sems, 3 * k + j, (px, py, c)))
        return sends, arrivals
    return plan


def _forward_plan(shard_shapes, col_sharded, ks):
    def plan(bufs, send_sems, recv_sems):
        x, y, c = _mesh_pos()
        sends, arrivals = [], []
        for i, (ref, k) in enumerate(zip(bufs, ks)):
            for j, (px, py) in enumerate(_other_chips(x, y)):
                landed = _half_block(ref, shard_shapes[k], col_sharded[k], 2 * px + py, c)
                theirs = _half_block(ref, shard_shapes[k], col_sharded[k], 2 * px + py, 1 - c)
                sends.append(_remote(landed, landed, send_sems, recv_sems, 3 * i + j, (x, y, 1 - c)))
                arrivals.append(_remote(theirs, theirs, send_sems, recv_sems, 3 * i + j, (x, y, 1 - c)))
        return sends, arrivals
    return plan


def _small_gather(small):
    def body(small_ref, out_ref, send_sems, recv_sems):
        x, y, c = _mesh_pos()
        me = 2 * x + y
        out_ref[me] = small_ref[...]
        copies = []
        for j, (px, py) in enumerate(_other_chips(x, y)):
            cp = _remote(small_ref, out_ref.at[me], send_sems, recv_sems, j, (px, py, c))
            cp.start()
            copies.append(cp)
        for j, (px, py) in enumerate(_other_chips(x, y)):
            _remote(small_ref, out_ref.at[2 * px + py], send_sems, recv_sems, j, (px, py, c)).wait_recv()
        for cp in copies:
            cp.wait_send()

    return pl.pallas_call(
        body,
        out_shape=jax.ShapeDtypeStruct((N_CHIPS,) + small.shape, small.dtype),
        in_specs=[VMEM_SPEC],
        out_specs=VMEM_SPEC,
        scratch_shapes=[pltpu.SemaphoreType.DMA((N_CHIPS - 1,)), pltpu.SemaphoreType.DMA((N_CHIPS - 1,))],
        name="gather_small",
    )(small)


def _as3d(g, col_sharded):
    r, cdim = g.shape
    return g.reshape(1, r, cdim) if col_sharded else g.reshape(N_CHIPS, r // N_CHIPS, cdim)


def _pair_plan(m):
    def plan(bufs, send_sems, recv_sems):
        x, y, c = _mesh_pos()
        copies = []
        for i in range(m):
            h = bufs[i].shape[1] // 2
            src = bufs[i].at[:, pl.ds(pl.multiple_of((1 - c) * h, SUBLANE), h), :]
            copies.append(_remote(src, bufs[m + i], send_sems, recv_sems, i, (x, y, 1 - c)))
        return copies, copies
    return plan


def _chip_plan(col_flags):
    m = len(col_flags)

    def plan(bufs, send_sems, recv_sems):
        x, y, c = _mesh_pos()
        copies = []
        for i in range(m):
            land = bufs[m + i]
            width = land.shape[2]
            for j, (px, py) in enumerate(_other_chips(x, y)):
                q = 2 * px + py
                if col_flags[i]:
                    src = bufs[i].at[0, :, pl.ds(pl.multiple_of(q * width, LANE), width)]
                else:
                    src = bufs[i].at[q]
                copies.append(_remote(src, land.at[j], send_sems, recv_sems, 3 * i + j, (px, py, c)))
        return copies, copies
    return plan


def _share_plan(m):
    def plan(bufs, send_sems, recv_sems):
        x, y, c = _mesh_pos()
        sends, arrivals = [], []
        for i in range(m):
            h = bufs[i].shape[0] // 2
            mine = bufs[i].at[pl.ds(pl.multiple_of(c * h, SUBLANE), h), :]
            theirs = bufs[i].at[pl.ds(pl.multiple_of((1 - c) * h, SUBLANE), h), :]
            sends.append(_remote(mine, mine, send_sems, recv_sems, i, (x, y, 1 - c)))
            arrivals.append(_remote(theirs, theirs, send_sems, recv_sems, i, (x, y, 1 - c)))
        return sends, arrivals
    return plan


def _pair_add(g3, other, core):
    a, r, cdim = g3.shape
    h = r // 2
    rb = _pick(h, max(BF16_ROWS, (512 * 1024) // cdim), BF16_ROWS)
    nb = h // rb

    def body(core_ref, g_ref, o_ref, out_ref):
        out_ref[...] = (g_ref[...] + o_ref[...]).astype(BF16)

    grid_spec = pltpu.PrefetchScalarGridSpec(
        num_scalar_prefetch=1,
        grid=(a, nb),
        in_specs=[pl.BlockSpec((None, rb, cdim), lambda i, j, core_ref: (i, core_ref[0] * nb + j, 0)),
                  pl.BlockSpec((None, rb, cdim), lambda i, j, core_ref: (i, j, 0))],
        out_specs=pl.BlockSpec((None, rb, cdim), lambda i, j, core_ref: (i, j, 0)),
    )
    return pl.pallas_call(
        body,
        out_shape=_out((a, h, cdim), BF16),
        grid_spec=grid_spec,
        compiler_params=_cparams(("parallel", "parallel"), 2 * rb * cdim * 10),
        name="grad_pair_add",
    )(core, _hbm(g3), _hbm(other))


def _small_allreduce(small):
    def body(small_ref, out_ref, slots, send_sems, recv_sems):
        x, y, c = _mesh_pos()
        my_dev = 4 * x + 2 * y + c
        slots[my_dev] = small_ref[...]
        copies = []
        for m in range(1, N_DEV):
            peer = (x ^ ((m >> 2) & 1), y ^ ((m >> 1) & 1), c ^ (m & 1))
            cp = _remote(small_ref, slots.at[my_dev], send_sems, recv_sems, m - 1, peer)
            cp.start()
            copies.append(cp)
        for m in range(1, N_DEV):
            _remote(small_ref, slots.at[my_dev ^ m], send_sems, recv_sems, m - 1, (x, y, c)).wait_recv()
        total = slots[0]
        for d in range(1, N_DEV):
            total = total + slots[d]
        out_ref[...] = total
        for cp in copies:
            cp.wait_send()

    return pl.pallas_call(
        body,
        out_shape=jax.ShapeDtypeStruct(small.shape, F32),
        in_specs=[VMEM_SPEC],
        out_specs=VMEM_SPEC,
        scratch_shapes=[pltpu.VMEM((N_DEV,) + small.shape, F32),
                        pltpu.SemaphoreType.DMA((N_DEV - 1,)), pltpu.SemaphoreType.DMA((N_DEV - 1,))],
        compiler_params=pltpu.CompilerParams(
            vmem_limit_bytes=min(VMEM_BUDGET, (N_DEV + 4) * _nbytes(small.shape, F32) + (8 << 20))),
        name="grad_small_allreduce",
    )(small)


def _chip_sum(partial, land, where, col_sharded):
    _, h, cdim = land.shape
    rb = _pick(h, max(BF16_ROWS, (512 * 1024) // cdim), BF16_ROWS)
    nb = h // rb

    def body(where_ref, own_ref, l_ref, o_ref):
        total = own_ref[...].astype(F32)
        for j in range(N_CHIPS - 1):
            total = total + l_ref[j].astype(F32)
        o_ref[...] = total

    if col_sharded:
        own_map = lambda i, w: (0, i, w[0])
    else:
        own_map = lambda i, w: (w[0], i, 0)
    grid_spec = pltpu.PrefetchScalarGridSpec(
        num_scalar_prefetch=1,
        grid=(nb,),
        in_specs=[pl.BlockSpec((None, rb, cdim), own_map),
                  pl.BlockSpec((N_CHIPS - 1, rb, cdim), lambda i, w: (0, i, 0))],
        out_specs=pl.BlockSpec((rb, cdim), lambda i, w: (w[1] * nb + i, 0)),
    )
    return pl.pallas_call(
        body,
        out_shape=_out((2 * h, cdim), F32),
        grid_spec=grid_spec,
        compiler_params=_cparams(("parallel",), 2 * rb * cdim * 12),
        name="grad_chip_sum",
    )(where, _hbm(partial), _hbm(land))


def _adamw(w, g, m, v, name, dep=None):
    r, cdim = w.shape
    rb = _pick(r, max(SUBLANE, (256 * 1024) // cdim), SUBLANE)
    c1 = 1.0 - ADAM_B1 ** ADAM_STEP
    c2 = 1.0 - ADAM_B2 ** ADAM_STEP

    def body(w_ref, g_ref, m_ref, v_ref, go_ref, d_ref, mo_ref, vo_ref):
        gv = g_ref[...]
        mn = ADAM_B1 * m_ref[...] + (1.0 - ADAM_B1) * gv
        vn = ADAM_B2 * v_ref[...] + (1.0 - ADAM_B2) * (gv * gv)
        m_hat = mn / c1
        v_hat = vn / c2
        d_ref[...] = -ADAM_LR * (m_hat / (jnp.sqrt(v_hat) + ADAM_EPS) + ADAM_WD * w_ref[...])
        go_ref[...] = gv
        mo_ref[...] = mn
        vo_ref[...] = vn

    blk = pl.BlockSpec((rb, cdim), lambda i: (i, 0))
    shape = jax.ShapeDtypeStruct((r, cdim), F32)
    body, in_specs, operands = _dep_args(body, [blk] * 4, [_hbm(w), _hbm(g), _hbm(m), _hbm(v)], dep)
    return pl.pallas_call(
        body,
        out_shape=[shape] * 4,
        grid=(r // rb,),
        in_specs=in_specs,
        out_specs=[blk] * 4,
        compiler_params=_cparams(("parallel",), 2 * rb * cdim * 4 * 8),
        name=name,
    )(*operands)


def _pack(arrays):
    tile = SUBLANE * LANE
    pieces = []
    for arr in arrays:
        flat = arr.reshape(-1)
        pad = (-flat.shape[0]) % tile
        if pad:
            flat = jnp.concatenate([flat, jnp.zeros((pad,), flat.dtype)])
        pieces.append(flat)
    return jnp.concatenate(pieces).reshape(-1, LANE)


def _unpack(packed, shapes):
    tile = SUBLANE * LANE
    flat = packed.reshape(-1)
    out, off = [], 0
    for shp in shapes:
        size = math.prod(shp)
        out.append(flat[off:off + size].reshape(shp))
        off += size + ((-size) % tile)
    return out


def _block_diag_groups(w, per_group):
    hcount, hd, _ = w.shape
    ng = hcount // per_group
    w4 = w.reshape(ng, per_group, hd, hd)
    eye = jnp.eye(per_group, dtype=w.dtype)
    bd = w4[:, :, :, None, :] * eye[None, :, None, :, None]
    return bd.reshape(ng, per_group * hd, per_group * hd).astype(BF16)


def _diag_blocks(wbd, per_group, hd):
    ng = wbd.shape[0]
    w5 = wbd.reshape(ng, per_group, hd, per_group, hd)
    blocks = [w5[:, i, :, i, :] for i in range(per_group)]
    return jnp.stack(blocks, axis=1).reshape(ng * per_group, hd, hd)


def kernel(x, g_mix, w_in, lru_conv_w, lru_conv_b, lru_wa, lru_ba, lru_wx, lru_bx, lru_lambda, lru_w_out, sc_conv_w, sc_w_out, w_o, g_ffn, ffn_w_up, ffn_conv_w, ffn_w_down, g_final, loss_target, m_g_mix, m_w_in, m_lru_conv_w, m_lru_conv_b, m_lru_wa, m_lru_ba, m_lru_wx, m_lru_bx, m_lru_lambda, m_lru_w_out, m_sc_conv_w, m_sc_w_out, m_w_o, m_g_ffn, m_ffn_w_up, m_ffn_conv_w, m_ffn_w_down, m_g_final, v_g_mix, v_w_in, v_lru_conv_w, v_lru_conv_b, v_lru_wa, v_lru_ba, v_lru_wx, v_lru_bx, v_lru_lambda, v_lru_w_out, v_sc_conv_w, v_sc_w_out, v_w_o, v_g_ffn, v_ffn_w_up, v_ffn_conv_w, v_ffn_w_down, v_g_final):
    seq, d_model = x.shape[1], x.shape[2]
    heads, head_dim, _ = lru_wa.shape
    d_lru = heads * head_dim
    d_sc = sc_w_out.shape[0]
    d_ff = ffn_w_down.shape[0] * N_CHIPS
    assert x.shape[0] == 1 and w_in.shape[1] * N_CHIPS == 2 * d_lru + 3 * d_sc + 2 * d_model
    xs = x.reshape(seq, d_model)
    target = loss_target.reshape(seq, d_model)

    chip = 2 * lax.axis_index("x") + lax.axis_index("y")
    core = lax.axis_index("c").astype(jnp.int32).reshape(1)

    big_w = [w_in, lru_w_out, sc_w_out, w_o, ffn_w_up, ffn_w_down]
    big_m = [m_w_in, m_lru_w_out, m_sc_w_out, m_w_o, m_ffn_w_up, m_ffn_w_down]
    big_v = [v_w_in, v_lru_w_out, v_sc_w_out, v_w_o, v_ffn_w_up, v_ffn_w_down]
    col_sharded = [True, True, True, False, True, False]
    conv_shards = [lru_conv_w, sc_conv_w, ffn_conv_w]
    conv_pack = jnp.concatenate(
        [jnp.pad(w, ((0, SUBLANE - w.shape[0]), (0, 0))) for w in conv_shards], axis=1)
    big_names = ["w_in", "lru_w_out", "sc_w_out", "w_o", "ffn_w_up", "ffn_w_down"]
    chip_arr = chip.astype(jnp.int32).reshape(1)
    placed = [_cast_place(w, chip_arr, cs, "cast_" + nm) for w, cs, nm in zip(big_w, col_sharded, big_names)]
    conv_all = _small_gather(conv_pack)
    shard_shapes = [w.shape for w in big_w]
    n_big = len(big_w)

    def gather_start(ks, after, tag):
        send, recv, bufs, token = _exchange_start(
            "gather_start_" + tag, [placed[k] for k in ks], 3 * n_big,
            _gather_plan(shard_shapes, col_sharded, ks), after=after)
        return (send, recv, dict(zip(ks, bufs))), token

    def arrived(state, ks, after, tag):
        send, recv, bufs = state
        got = _exchange_wait("gather_wait_" + tag, [bufs[k] for k in ks], send, recv, after,
                             _gather_plan(shard_shapes, col_sharded, ks))
        return _exchange("gather_forward_" + tag, got, 3 * len(ks), _forward_plan(shard_shapes, col_sharded, ks))

    conv_full, off = [], 0
    for w in conv_shards:
        kw, nq = w.shape
        piece = conv_all[:, :kw, off:off + nq]
        conv_full.append(piece.transpose(1, 0, 2).reshape(kw, N_CHIPS * nq))
        off += nq
    lcw, scw, fcw = conv_full

    per_group = max(1, min(heads, 256 // head_dim))
    gc = per_group * head_dim
    wa_bd = _block_diag_groups(lru_wa, per_group)
    wx_bd = _block_diag_groups(lru_wx, per_group)
    tc = _pick(seq, 256, SUBLANE)
    cb_sc = _pick(d_sc, 512)
    cb_ff = _pick(d_ff, 512)
    col_sc = 2 * d_lru
    col_gates = 2 * d_lru + 3 * d_sc

    first, token = gather_start([0], conv_all, "in")
    h1 = _rms_fwd(xs, g_mix, "rms_mix", dep=token)
    (win_b,) = arrived(first, [0], h1, "in")
    rest, token = gather_start([1, 2, 3, 4, 5], win_b, "rest")
    p = _mm(h1, win_b, "nn", F32, name="mm_in", dep=token)
    wlo_b, wso_b, wo_b = arrived(rest, [1, 2, 3], p, "mix")
    y_lru_pre, hseq = _lru_fwd(p, lcw, lru_conv_b, wa_bd, lru_ba, wx_bd, lru_bx, lru_lambda, d_lru, gc, tc)
    y_sc_pre = _sc_fwd(p, scw, col_sc, d_sc, cb_sc, tc)
    y_lru = _mm(y_lru_pre, wlo_b, "nn", F32, name="mm_lru_out")
    y_sc = _mm(y_sc_pre, wso_b, "nn", F32, name="mm_sc_out")
    merged = _merge_fwd(p, y_lru, y_sc, col_gates, tc)
    x2 = _mm(merged, wo_b, "nn", F32, res=xs, name="mm_o")
    (wup_b,) = arrived(rest, [4], x2, "up")
    h2 = _rms_fwd(x2, g_ffn, "rms_ffn")
    up = _mm(h2, wup_b, "nn", F32, name="mm_up")
    (wdn_b,) = arrived(rest, [5], up, "down")
    act = _ffn_act_fwd(up, fcw, d_ff, cb_ff, tc)
    x3 = _mm(act, wdn_b, "nn", F32, res=x2, name="mm_down")
    loss_part, dx3, dx3b, dg_final = _loss_head(x3, g_final, target)

    where = jnp.concatenate([chip_arr, core])

    def reduce_start(grads, flags, tag):
        views = [_as3d(g, cs) for g, cs in zip(grads, flags)]
        lands = [lax.empty((v.shape[0], v.shape[1] // 2, v.shape[2]), F32) for v in views]
        send, recv, bufs, token = _exchange_start("grad_pair_start_" + tag, views + lands, len(views),
                                                  _pair_plan(len(views)))
        return (send, recv, bufs, flags, tag), token

    def reduce_mid(state, after):
        send, recv, bufs, flags, tag = state
        m = len(flags)
        bufs = _exchange_wait("grad_pair_wait_" + tag, bufs, send, recv, after, _pair_plan(m))
        partials = [_pair_add(bufs[i], bufs[m + i], core) for i in range(m)]
        lands = []
        for pz, cs in zip(partials, flags):
            _, h, cdim = pz.shape
            lands.append(lax.empty((N_CHIPS - 1, h, cdim // N_CHIPS if cs else cdim), BF16))
        send, recv, bufs, token = _exchange_start("grad_chip_start_" + tag, partials + lands, 3 * m,
                                                  _chip_plan(flags))
        return (send, recv, bufs, flags, tag), token

    def reduce_end(state, after):
        send, recv, bufs, flags, tag = state
        m = len(flags)
        bufs = _exchange_wait("grad_chip_wait_" + tag, bufs, send, recv, after, _chip_plan(flags))
        return [_chip_sum(bufs[i], bufs[m + i], where, flags[i]) for i in range(m)]

    g_wdn = _mm(act, dx3b, "tn", F32, name="mm_down_dw")
    red_down, token = reduce_start([g_wdn], [False], "down")
    dact = _mm(dx3b, wdn_b, "nt", F32, name="mm_down_dx", dep=token)
    red_down, token = reduce_mid(red_down, dact)
    dupg, dupv, dfcw_g, dfcw_v = _ffn_act_bwd(up, dact, fcw, d_ff, cb_ff, tc, dep=token)
    dup = jnp.concatenate([dupg, dupv], axis=1)
    g_wup = _mm(h2, dup, "tn", F32, name="mm_up_dw")
    red_up, token = reduce_start([g_wup], [True], "up")
    dh2 = _mm(dup, wup_b, "nt", F32, name="mm_up_dx", dep=token)
    red_up, token = reduce_mid(red_up, dh2)
    dx2, dx2b, dg_ffn = _rms_bwd(x2, g_ffn, dh2, dx3, "rms_ffn_bwd", True, dep=token)
    g_wo = _mm(merged, dx2b, "tn", F32, name="mm_o_dw")
    dmerged = _mm(dx2b, wo_b, "nt", F32, name="mm_o_dx")
    dgl, dgs, dyl, dys = _merge_bwd(p, y_lru, y_sc, dmerged, col_gates, tc)
    g_wlo = _mm(y_lru_pre, dyl, "tn", F32, name="mm_lru_out_dw")
    g_wso = _mm(y_sc_pre, dys, "tn", F32, name="mm_sc_out_dw")
    red_mix, token = reduce_start([g_wlo, g_wso, g_wo], [True, True, False], "mix")
    dylp = _mm(dyl, wlo_b, "nt", F32, name="mm_lru_out_dx", dep=token)
    dysp = _mm(dys, wso_b, "nt", F32, name="mm_sc_out_dx")
    red_mix, token = reduce_mid(red_mix, dysp)
    dlx, dlgate, dlcw, dlcb, dwa_bd, dba, dwx_bd, dbx, dlam = _lru_bwd(
        p, hseq, dylp, lcw, lru_conv_b, wa_bd, lru_ba, wx_bd, lru_bx, lru_lambda, d_lru, gc, tc, dep=token)
    dsb, dsc, dsv, dscw = _sc_bwd(p, dysp, scw, col_sc, d_sc, cb_sc, tc)
    dp = jnp.concatenate([dlx, dlgate, dsb, dsc, dsv, dgl, dgs], axis=1)
    g_win = _mm(h1, dp, "tn", F32, name="mm_in_dw")
    red_in, token = reduce_start([g_win], [True], "in")
    dh1 = _mm(dp, win_b, "nt", F32, name="mm_in_dx", dep=token)
    grad_x, dg_mix = _rms_bwd(xs, g_mix, dh1, dx2, "rms_mix_bwd", False)

    small_g = [dg_mix, dlcw, dlcb, _diag_blocks(dwa_bd, per_group, head_dim), dba,
               _diag_blocks(dwx_bd, per_group, head_dim), dbx, dlam, dscw, dg_ffn,
               jnp.concatenate([dfcw_g, dfcw_v], axis=1), dg_final]
    small_shapes = [a.shape for a in small_g]
    small_sum = _small_allreduce(_pack(small_g))
    red_in, token = reduce_mid(red_in, small_sum)
    (h_wdn,) = reduce_end(red_down, token)
    (h_wup,) = reduce_end(red_up, token)
    h_wlo, h_wso, h_wo = reduce_end(red_mix, token)
    s_wlo, s_wso, s_wo, s_wup, s_wdn = _exchange("grad_share_a", [h_wlo, h_wso, h_wo, h_wup, h_wdn], 5,
                                                 _share_plan(5))
    early = {1: s_wlo, 2: s_wso, 3: s_wo, 4: s_wup, 5: s_wdn}
    big_out = [None] * n_big
    last = None
    for k, g in early.items():
        big_out[k] = _adamw(big_w[k], g, big_m[k], big_v[k], "adamw_" + big_names[k], dep=last)
        last = big_out[k][1]
    (h_win,) = reduce_end(red_in, last)
    (s_win,) = _exchange("grad_share_b", [h_win], 1, _share_plan(1))
    big_out[0] = _adamw(big_w[0], s_win, big_m[0], big_v[0], "adamw_" + big_names[0])
    sg = _unpack(small_sum, small_shapes)
    for idx in (1, 8, 10):
        nq = sg[idx].shape[1] // N_CHIPS
        sg[idx] = lax.dynamic_slice_in_dim(sg[idx], chip * nq, nq, axis=1)
    small_w = [g_mix, lru_conv_w, lru_conv_b, lru_wa, lru_ba, lru_wx, lru_bx, lru_lambda, sc_conv_w,
               g_ffn, ffn_conv_w, g_final]
    small_m = [m_g_mix, m_lru_conv_w, m_lru_conv_b, m_lru_wa, m_lru_ba, m_lru_wx, m_lru_bx, m_lru_lambda,
               m_sc_conv_w, m_g_ffn, m_ffn_conv_w, m_g_final]
    small_v = [v_g_mix, v_lru_conv_w, v_lru_conv_b, v_lru_wa, v_lru_ba, v_lru_wx, v_lru_bx, v_lru_lambda,
               v_sc_conv_w, v_g_ffn, v_ffn_conv_w, v_g_final]
    sg = [g.reshape(w.shape) for g, w in zip(sg, small_w)]
    w_shapes = [w.shape for w in small_w]
    packed = _adamw(_pack(small_w), _pack(sg), _pack(small_m), _pack(small_v), "adamw_small")
    small_out = [_unpack(pk, w_shapes) for pk in packed]

    order = [(0, 0), (1, 0), (0, 1), (0, 2), (0, 3), (0, 4), (0, 5), (0, 6), (0, 7), (1, 1), (0, 8), (1, 2),
             (1, 3), (0, 9), (1, 4), (0, 10), (1, 5), (0, 11)]
    by_kind = []
    for kind in range(4):
        by_kind.append([big_out[i][kind] if is_big else small_out[kind][i] for is_big, i in order])
    loss = lax.psum(loss_part[0, 0], ("x", "y", "c"))
    return (loss, grad_x.reshape(x.shape), *by_kind[0], *by_kind[1], *by_kind[2], *by_kind[3])
```

```python
import functools
import math

import jax
import jax.numpy as jnp
from jax import lax
from jax.experimental import pallas as pl
from jax.experimental.pallas import tpu as pltpu

F32 = jnp.float32
BF16 = jnp.bfloat16

LANE = 128
SUBLANE = 8
BF16_ROWS = 16
VMEM_BYTES_V7X = 64 * 1024 * 1024
VMEM_BUDGET = VMEM_BYTES_V7X - 8 * 1024 * 1024
MM_VMEM_BUDGET = 42 * 1024 * 1024

EPS = 1e-6
LRU_C = 8.0
ADAM_LR = 0.001
ADAM_B1 = 0.9
ADAM_B2 = 0.999
ADAM_EPS = 1e-08
ADAM_WD = 0.01
ADAM_STEP = 10

N_CHIPS = 4
N_DEV = 8
MESH = pl.DeviceIdType.MESH
ANY = pl.BlockSpec(memory_space=pl.ANY)
VMEM_SPEC = pl.BlockSpec(memory_space=pltpu.VMEM)
HBM_SPEC = pl.BlockSpec(memory_space=pltpu.HBM)
SEM_SPEC = pl.BlockSpec(memory_space=pltpu.SEMAPHORE)
DATAFLOW_EFFECT = pltpu.SideEffectType.DATAFLOW_SIDE_EFFECTING


def _pick(n, cap, mult=LANE):
    best = None
    d = mult
    while d <= min(n, cap):
        if n % d == 0:
            best = d
        d += mult
    return n if best is None else best


def _cparams(semantics, block_bytes):
    limit = min(VMEM_BUDGET, max(32 * 1024 * 1024, int(block_bytes * 1.25) + (4 << 20)))
    return pltpu.CompilerParams(dimension_semantics=semantics, vmem_limit_bytes=limit)


def _nbytes(shape, dtype):
    return math.prod(shape) * jnp.dtype(dtype).itemsize


def _sigmoid(z):
    return 1.0 / (1.0 + jnp.exp(-z))


def _softplus(z):
    e = jnp.exp(-jnp.abs(z))
    u = 1.0 + e
    log1p = jnp.where(u == 1.0, e, jnp.log(u) * (e / (u - 1.0)))
    return jnp.maximum(z, 0.0) + log1p


def _neg_expm1(z):
    small = z * (1.0 + z * (0.5 + z * (1.0 / 6.0 + z * (1.0 / 24.0))))
    return -jnp.where(jnp.abs(z) < 0.03, small, jnp.exp(z) - 1.0)


_GELU_K = math.sqrt(2.0 / math.pi)
_GELU_C = 0.044715


def _gelu_and_grad(z):
    z2 = z * z
    th = jnp.tanh(_GELU_K * (z + _GELU_C * z2 * z))
    val = 0.5 * z * (1.0 + th)
    grad = 0.5 * (1.0 + th) + 0.5 * z * (1.0 - th * th) * (_GELU_K * (1.0 + 3.0 * _GELU_C * z2))
    return val, grad


def _rows_before(cat, k):
    if k == 0:
        return cat[SUBLANE:, :]
    return pltpu.roll(cat, k, 0)[SUBLANE:, :]


def _rows_after(cat, k):
    n = cat.shape[0]
    if k == 0:
        return cat[:n - SUBLANE, :]
    return pltpu.roll(cat, n - k, 0)[:n - SUBLANE, :]


def _conv_fwd(cat, w, width):
    y = _rows_before(cat, width - 1) * w[0:1, :]
    for k in range(1, width):
        y = y + _rows_before(cat, width - 1 - k) * w[k:k + 1, :]
    return y


def _conv_bwd_input(cat, w, width):
    dx = _rows_after(cat, width - 1) * w[0:1, :]
    for k in range(1, width):
        dx = dx + _rows_after(cat, width - 1 - k) * w[k:k + 1, :]
    return dx


def _conv_bwd_weight(dw_ref, dy, catx, width):
    for k in range(width):
        dw_ref[k:k + 1, :] += jnp.sum(dy * _rows_before(catx, width - 1 - k), axis=0, keepdims=True)


def _scan_tiles(a_ref, b_ref, out_ref, carry0, n_rows, reverse):
    cols = a_ref.shape[1]
    row = lax.broadcasted_iota(jnp.int32, (SUBLANE, cols), 0)
    n_tiles = n_rows // SUBLANE

    def step(j, carry):
        tile = (n_tiles - 1 - j) if reverse else j
        off = pl.multiple_of(tile * SUBLANE, SUBLANE)
        a = a_ref[pl.ds(off, SUBLANE), :]
        b = b_ref[pl.ds(off, SUBLANE), :]
        for s in (1, 2, 4):
            if reverse:
                keep = row < SUBLANE - s
                shift = SUBLANE - s
            else:
                keep = row >= s
                shift = s
            a_sh = jnp.where(keep, pltpu.roll(a, shift, 0), 1.0)
            b_sh = jnp.where(keep, pltpu.roll(b, shift, 0), 0.0)
            b = a * b_sh + b
            a = a * a_sh
        out = a * carry + b
        out_ref[pl.ds(off, SUBLANE), :] = out
        return out[0:1, :] if reverse else out[SUBLANE - 1:SUBLANE, :]

    return lax.fori_loop(0, n_tiles, step, carry0)


def _out(shape, dtype):
    return jax.ShapeDtypeStruct(shape, dtype)


def _hbm(x):
    return x


def _dep_args(body, in_specs, operands, *deps):
    deps = [d for d in deps if d is not None]
    if not deps:
        return body, in_specs, operands
    n = len(operands)

    def wrapped(*refs):
        return body(*refs[:n], *refs[n + len(deps):])

    return wrapped, list(in_specs) + [ANY] * len(deps), list(operands) + deps


def _mm(a, b, mode, out_dtype, res=None, name=None, dep=None, slabs=None):
    assert a.dtype == BF16 and b.dtype == BF16
    a_slabbed, b_slabbed = a.ndim == 3, b.ndim == 3
    assert not a_slabbed or (mode == "nt" and slabs is not None)
    assert not b_slabbed or (mode == "tn" and slabs is not None)
    if mode == "nn":
        (m, k), (k2, n) = a.shape, b.shape
        dims = (((1,), (0,)), ((), ()))
    elif mode == "nt":
        m, k = (a.shape[1], a.shape[0] * a.shape[2]) if a_slabbed else a.shape
        n, k2 = b.shape
        dims = (((1,), (1,)), ((), ()))
    else:
        k, m = a.shape
        k2, n = (b.shape[1], b.shape[0] * b.shape[2]) if b_slabbed else b.shape
        dims = (((0,), (0,)), ((), ()))
    assert k == k2
    n_unit = b.shape[2] if b_slabbed else n
    out_bytes = jnp.dtype(out_dtype).itemsize
    bm = _pick(m, 1024)
    bn = _pick(n_unit, 1024)
    bk = k

    def est(bm_, bn_, bk_):
        e = 2 * (bm_ * bk_ + bk_ * bn_) * 2 + 2 * bm_ * bn_ * out_bytes
        if k // bk_ > 1:
            e += bm_ * bn_ * 4
        if res is not None:
            e += 2 * bm_ * bn_ * 4
        return e

    for shrink_n, floor in ((True, 512), (False, 512), (True, 256), (False, 256)):
        while est(bm, bn, bk) > MM_VMEM_BUDGET:
            if shrink_n and bn > floor and bn % 2 == 0 and n_unit % (bn // 2) == 0:
                bn //= 2
            elif not shrink_n and bm > floor and bm % 2 == 0 and m % (bm // 2) == 0:
                bm //= 2
            else:
                break
    while (est(bm, bn, bk) > MM_VMEM_BUDGET and not a_slabbed and bk % (2 * LANE) == 0
           and k % (bk // 2) == 0):
        bk //= 2
    nk = k // bk
    per_slab = n_unit // bn

    def out_col(j):
        if not b_slabbed:
            return j
        s = j // per_slab
        where = sum(jnp.where(s == t, slabs[t], 0) for t in range(len(slabs)))
        return where * per_slab + j % per_slab

    if mode == "tn":
        a_spec = pl.BlockSpec((bk, bm), lambda i, j, kk: (kk, i))
    elif a_slabbed:
        a_spec = pl.BlockSpec((a.shape[0], bm, a.shape[2]), lambda i, j, kk: (0, i, 0))
    else:
        a_spec = pl.BlockSpec((bm, bk), lambda i, j, kk: (i, kk))
    if mode == "nt":
        b_spec = pl.BlockSpec((bn, bk), lambda i, j, kk: (j, kk))
    elif b_slabbed:
        b_spec = pl.BlockSpec((None, bk, bn), lambda i, j, kk: (j // per_slab, kk, j % per_slab))
    else:
        b_spec = pl.BlockSpec((bk, bn), lambda i, j, kk: (kk, j))
    o_spec = pl.BlockSpec((bm, bn), lambda i, j, kk: (i, out_col(j)))
    in_specs = [a_spec, b_spec]
    operands = [a, b]
    if res is not None:
        in_specs.append(o_spec)
        operands.append(res)
    has_res = res is not None

    def body(*refs):
        a_ref, b_ref = refs[0], refs[1]
        res_ref = refs[2] if has_res else None
        o_ref = refs[2 + has_res]
        if a_slabbed:
            width = a_ref.shape[2]
            part = None
            for s, col in enumerate(slabs):
                term = lax.dot_general(a_ref[s], b_ref[:, col * width:(col + 1) * width], dims,
                                       preferred_element_type=F32)
                part = term if part is None else part + term
        else:
            part = lax.dot_general(a_ref[...], b_ref[...], dims, preferred_element_type=F32)
        if nk == 1:
            if has_res:
                part = part + res_ref[...]
            o_ref[...] = part.astype(o_ref.dtype)
            return
        acc_ref = refs[-1]
        kk = pl.program_id(2)

        @pl.when(kk == 0)
        def _():
            acc_ref[...] = part

        @pl.when(kk > 0)
        def _():
            acc_ref[...] += part

        @pl.when(kk == nk - 1)
        def _():
            total = acc_ref[...]
            if has_res:
                total = total + res_ref[...]
            o_ref[...] = total.astype(o_ref.dtype)

    scratch = [pltpu.VMEM((bm, bn), F32)] if nk > 1 else []
    body, in_specs, operands = _dep_args(body, in_specs, operands, dep)
    return pl.pallas_call(
        body,
        out_shape=jax.ShapeDtypeStruct((m, n), out_dtype),
        grid=(m // bm, n // bn, nk),
        in_specs=in_specs,
        out_specs=o_spec,
        scratch_shapes=scratch,
        compiler_params=_cparams(("parallel", "parallel", "arbitrary"), est(bm, bn, bk)),
        name=name,
    )(*operands)


def _rms_fwd(x, g, name, dep=None):
    t, d = x.shape
    tb = _pick(t, 512, SUBLANE)

    def body(x_ref, g_ref, h_ref):
        xv = x_ref[...]
        r = lax.rsqrt(jnp.mean(xv * xv, axis=-1, keepdims=True) + EPS)
        h_ref[...] = ((xv * r) * g_ref[...]).astype(BF16)

    blk = pl.BlockSpec((tb, d), lambda i: (i, 0))
    body, in_specs, operands = _dep_args(
        body, [blk, pl.BlockSpec((1, d), lambda i: (0, 0))], [_hbm(x), g.reshape(1, d)], dep)
    return pl.pallas_call(
        body,
        out_shape=_out((t, d), BF16),
        grid=(t // tb,),
        in_specs=in_specs,
        out_specs=blk,
        compiler_params=_cparams(("parallel",), 2 * tb * d * 6),
        name=name,
    )(*operands)


def _rms_bwd(x, g, dh, dres, name, want_bf16, dep=None):
    t, d = x.shape
    tb = _pick(t, 256, SUBLANE)

    def body(x_ref, g_ref, dh_ref, dres_ref, *outs):
        dx_ref, dg_ref = outs[0], outs[-1]
        xv = x_ref[...]
        r = lax.rsqrt(jnp.mean(xv * xv, axis=-1, keepdims=True) + EPS)
        xhat = xv * r
        dhv = dh_ref[...]
        dxhat = dhv * g_ref[...]
        dx = dres_ref[...] + r * (dxhat - xhat * jnp.mean(dxhat * xhat, axis=-1, keepdims=True))
        dx_ref[...] = dx
        if want_bf16:
            outs[1][...] = dx.astype(BF16)

        @pl.when(pl.program_id(0) == 0)
        def _():
            dg_ref[...] = jnp.zeros_like(dg_ref)

        dg_ref[...] += jnp.sum(dhv * xhat, axis=0, keepdims=True)

    blk = pl.BlockSpec((tb, d), lambda i: (i, 0))
    row = pl.BlockSpec((1, d), lambda i: (0, 0))
    out_shape = [_out((t, d), F32)]
    out_specs = [blk]
    if want_bf16:
        out_shape.append(_out((t, d), BF16))
        out_specs.append(blk)
    out_shape.append(jax.ShapeDtypeStruct((1, d), F32))
    out_specs.append(row)
    body, in_specs, operands = _dep_args(
        body, [blk, row, blk, blk], [_hbm(x), g.reshape(1, d), _hbm(dh), _hbm(dres)], dep)
    return pl.pallas_call(
        body,
        out_shape=out_shape,
        grid=(t // tb,),
        in_specs=in_specs,
        out_specs=out_specs,
        compiler_params=_cparams(("arbitrary",), 2 * tb * d * 18),
        name=name,
    )(*operands)


def _loss_head(x3, g, target):
    t, d = x3.shape
    tb = _pick(t, 256, SUBLANE)

    def body(x_ref, g_ref, t_ref, loss_ref, dx_ref, dxb_ref, dg_ref):
        xv = x_ref[...]
        gv = g_ref[...]
        r = lax.rsqrt(jnp.mean(xv * xv, axis=-1, keepdims=True) + EPS)
        xhat = xv * r
        err = xhat * gv - t_ref[...]
        dy = err * (1.0 / d)
        dxhat = dy * gv
        dx = r * (dxhat - xhat * jnp.mean(dxhat * xhat, axis=-1, keepdims=True))
        dx_ref[...] = dx
        dxb_ref[...] = dx.astype(BF16)

        @pl.when(pl.program_id(0) == 0)
        def _():
            dg_ref[...] = jnp.zeros_like(dg_ref)
            loss_ref[...] = jnp.zeros_like(loss_ref)

        dg_ref[...] += jnp.sum(dy * xhat, axis=0, keepdims=True)
        per_token = jnp.mean(err * err, axis=-1, keepdims=True)
        loss_ref[...] += 0.5 * jnp.sum(per_token, axis=0, keepdims=True)

    blk = pl.BlockSpec((tb, d), lambda i: (i, 0))
    row = pl.BlockSpec((1, d), lambda i: (0, 0))
    return pl.pallas_call(
        body,
        out_shape=[jax.ShapeDtypeStruct((1, 1), F32), _out((t, d), F32),
                   _out((t, d), BF16), jax.ShapeDtypeStruct((1, d), F32)],
        grid=(t // tb,),
        in_specs=[blk, row, blk],
        out_specs=[pl.BlockSpec((1, 1), lambda i: (0, 0)), blk, blk, row],
        compiler_params=_cparams(("arbitrary",), 2 * tb * d * 14),
        name="loss_head",
    )(_hbm(x3), g.reshape(1, d), _hbm(target))


def _lru_gates(xc, wa, ba, wx, bx, lam):
    nn = (((1,), (0,)), ((), ()))
    xcb = xc.astype(BF16)
    r = _sigmoid(lax.dot_general(xcb, wa, nn, preferred_element_type=F32) + ba)
    i = _sigmoid(lax.dot_general(xcb, wx, nn, preferred_element_type=F32) + bx)
    cl = -LRU_C * _softplus(-lam)
    log_a = cl * r
    a = jnp.exp(log_a)
    one_minus_a2 = _neg_expm1(2.0 * log_a)
    return xcb, r, i, a, one_minus_a2, cl


def _lru_fwd(p, conv_w, conv_b, wa_bd, ba, wx_bd, bx, lam, d_lru, gc, tc):
    t = p.shape[0]
    ng = d_lru // gc
    nt = t // tc
    width = conv_w.shape[0]

    def body(lx_ref, gate_ref, cw_ref, cb_ref, wa_ref, ba_ref, wx_ref, bx_ref, lam_ref,
             y_ref, h_ref, halo, hcar, a_s, u_s):
        @pl.when(pl.program_id(1) == 0)
        def _():
            halo[...] = jnp.zeros_like(halo)
            hcar[...] = jnp.zeros_like(hcar)

        x = lx_ref[...]
        cat = jnp.concatenate([halo[...], x], axis=0)
        halo[...] = x[tc - SUBLANE:, :]
        xc = _conv_fwd(cat, cw_ref[...], width) + cb_ref[...]
        _, r, i, a, om, _ = _lru_gates(xc, wa_ref[...], ba_ref[...], wx_ref[...], bx_ref[...], lam_ref[...])
        a_s[...] = a
        u_s[...] = jnp.sqrt(om) * (i * xc)
        hcar[0:1, :] = _scan_tiles(a_s, u_s, h_ref, hcar[0:1, :], tc, reverse=False)
        gl, _ = _gelu_and_grad(gate_ref[...])
        y_ref[...] = (gl * h_ref[...]).astype(BF16)

    blk = lambda off: pl.BlockSpec((tc, gc), lambda g, s, off=off: (s, off + g))
    rowv = lambda rows: pl.BlockSpec((rows, gc), lambda g, s: (0, g))
    wspec = pl.BlockSpec((None, gc, gc), lambda g, s: (g, 0, 0))
    out_blk = pl.BlockSpec((tc, gc), lambda g, s: (s, g))
    return pl.pallas_call(
        body,
        out_shape=[_out((t, d_lru), BF16), _out((t, d_lru), F32)],
        grid=(ng, nt),
        in_specs=[blk(0), blk(ng), rowv(width), rowv(1), wspec, rowv(1), wspec, rowv(1), rowv(1)],
        out_specs=[out_blk, out_blk],
        scratch_shapes=[pltpu.VMEM((SUBLANE, gc), F32), pltpu.VMEM((SUBLANE, gc), F32),
                        pltpu.VMEM((tc, gc), F32), pltpu.VMEM((tc, gc), F32)],
        compiler_params=_cparams(("parallel", "arbitrary"), 40 * tc * gc * 4),
        name="lru_fwd",
    )(_hbm(p), _hbm(p), conv_w, conv_b.reshape(1, -1), wa_bd, ba.reshape(1, -1), wx_bd, bx.reshape(1, -1),
      lam.reshape(1, -1))


def _lru_bwd(p, hseq, dyp, conv_w, conv_b, wa_bd, ba, wx_bd, bx, lam, d_lru, gc, tc, dp, slab0, dep=None):
    t = p.shape[0]
    ng = d_lru // gc
    nt = t // tc
    width = conv_w.shape[0]
    halo_blocks = tc // SUBLANE
    nn = (((1,), (0,)), ((), ()))
    nt_dims = (((1,), (1,)), ((), ()))
    tn_dims = (((0,), (0,)), ((), ()))

    def body(lx_ref, lxh_ref, gate_ref, h_ref, hh_ref, dyp_ref,
             cw_ref, cb_ref, wa_ref, ba_ref, wx_ref, bx_ref, lam_ref,
             dp_ref, dcw_ref, dcb_ref, dwa_ref, dba_ref, dwx_ref, dbx_ref, dlam_ref,
             nxt_dxc, nxt_a, nxt_g, al_s, b_s, g_s):
        s = pl.program_id(1)
        first_chunk = s == nt - 1

        @pl.when(s == 0)
        def _():
            nxt_dxc[...] = jnp.zeros_like(nxt_dxc)
            nxt_a[...] = jnp.zeros_like(nxt_a)
            nxt_g[...] = jnp.zeros_like(nxt_g)
            for ref in (dcw_ref, dcb_ref, dwa_ref, dba_ref, dwx_ref, dbx_ref, dlam_ref):
                ref[...] = jnp.zeros_like(ref)

        keep = jnp.where(first_chunk, 0.0, 1.0)
        x = lx_ref[...]
        catx = jnp.concatenate([lxh_ref[...] * keep, x], axis=0)
        cw = cw_ref[...]
        xc = _conv_fwd(catx, cw, width) + cb_ref[...]
        wa = wa_ref[...]
        wx = wx_ref[...]
        lam_v = lam_ref[...]
        xcb, r, i, a, om, cl = _lru_gates(xc, wa, ba_ref[...], wx, bx_ref[...], lam_v)
        mult = jnp.sqrt(om)

        h = h_ref[...]
        hprev = _rows_before(jnp.concatenate([hh_ref[...] * keep, h], axis=0), 1)
        gl, dgl = _gelu_and_grad(gate_ref[...])
        dyp_v = dyp_ref[...]
        dp_ref[1] = (dyp_v * h * dgl).astype(BF16)

        al_s[...] = _rows_after(jnp.concatenate([a, nxt_a[...]], axis=0), 1)
        b_s[...] = dyp_v * gl
        nxt_g[0:1, :] = _scan_tiles(al_s, b_s, g_s, nxt_g[0:1, :], tc, reverse=True)
        nxt_a[...] = a[0:SUBLANE, :]
        du = g_s[...]

        da = du * hprev
        dmult = du * (i * xc)
        di = du * mult * xc
        dxc = du * mult * i
        dlog_a = da * a - dmult * (a * a / mult)
        dlam_ref[...] += jnp.sum(dlog_a * r, axis=0, keepdims=True) * (LRU_C * _sigmoid(-lam_v))
        dza = (dlog_a * cl) * r * (1.0 - r)
        dzx = di * i * (1.0 - i)
        dba_ref[...] += jnp.sum(dza, axis=0, keepdims=True)
        dbx_ref[...] += jnp.sum(dzx, axis=0, keepdims=True)
        dzab = dza.astype(BF16)
        dzxb = dzx.astype(BF16)
        dwa_ref[...] += lax.dot_general(xcb, dzab, tn_dims, preferred_element_type=F32)
        dwx_ref[...] += lax.dot_general(xcb, dzxb, tn_dims, preferred_element_type=F32)
        dxc = dxc + lax.dot_general(dzab, wa, nt_dims, preferred_element_type=F32)
        dxc = dxc + lax.dot_general(dzxb, wx, nt_dims, preferred_element_type=F32)
        dcb_ref[...] += jnp.sum(dxc, axis=0, keepdims=True)
        _conv_bwd_weight(dcw_ref, dxc, catx, width)
        catd = jnp.concatenate([dxc, nxt_dxc[...]], axis=0)
        dp_ref[0] = _conv_bwd_input(catd, cw, width).astype(BF16)
        nxt_dxc[...] = dxc[0:SUBLANE, :]

    rev = lambda s: nt - 1 - s
    blk = lambda off: pl.BlockSpec((tc, gc), lambda g, s, off=off: (rev(s), off + g))
    halo = lambda off: pl.BlockSpec(
        (SUBLANE, gc), lambda g, s, off=off: (jnp.maximum(rev(s) * halo_blocks - 1, 0), off + g))
    rowv = lambda rows: pl.BlockSpec((rows, gc), lambda g, s: (0, g))
    wspec = pl.BlockSpec((None, gc, gc), lambda g, s: (g, 0, 0))
    out_blk = pl.BlockSpec((tc, gc), lambda g, s: (rev(s), g))
    vec = lambda rows: jax.ShapeDtypeStruct((rows, d_lru), F32)
    wshape = jax.ShapeDtypeStruct((ng, gc, gc), F32)
    body, in_specs, operands = _dep_args(
        body,
        [blk(0), halo(0), blk(ng), blk(0), halo(0), blk(0),
         rowv(width), rowv(1), wspec, rowv(1), wspec, rowv(1), rowv(1)],
        [_hbm(p), _hbm(p), _hbm(p), _hbm(hseq), _hbm(hseq), _hbm(dyp),
         conv_w, conv_b.reshape(1, -1), wa_bd, ba.reshape(1, -1), wx_bd,
         bx.reshape(1, -1), lam.reshape(1, -1)], dp, dep)
    assert dp.shape[2] == d_lru and slab0 % 2 == 0
    return pl.pallas_call(
        body,
        out_shape=[jax.ShapeDtypeStruct(dp.shape, dp.dtype),
                   vec(width), vec(1), wshape, vec(1), wshape, vec(1), vec(1)],
        grid=(ng, nt),
        in_specs=in_specs,
        out_specs=[pl.BlockSpec((2, tc, gc), lambda g, s: (slab0 // 2, rev(s), g)),
                   rowv(width), rowv(1), wspec, rowv(1), wspec, rowv(1), rowv(1)],
        input_output_aliases={13: 0},
        scratch_shapes=[pltpu.VMEM((SUBLANE, gc), F32), pltpu.VMEM((SUBLANE, gc), F32),
                        pltpu.VMEM((SUBLANE, gc), F32),
                        pltpu.VMEM((tc, gc), F32), pltpu.VMEM((tc, gc), F32), pltpu.VMEM((tc, gc), F32)],
        compiler_params=_cparams(("parallel", "arbitrary"), 80 * tc * gc * 4),
        name="lru_bwd",
    )(*operands)


def _sc_fwd(p, conv_w, col0, d_sc, cb, tc):
    t = p.shape[0]
    nc = d_sc // cb
    nt = t // tc
    width = conv_w.shape[0]
    base = col0 // cb

    def body(b_ref, c_ref, v_ref, w_ref, y_ref, halo):
        @pl.when(pl.program_id(1) == 0)
        def _():
            halo[...] = jnp.zeros_like(halo)

        cv = c_ref[...] * v_ref[...]
        cat = jnp.concatenate([halo[...], cv], axis=0)
        halo[...] = cv[tc - SUBLANE:, :]
        y_ref[...] = (b_ref[...] * _conv_fwd(cat, w_ref[...], width)).astype(BF16)

    blk = lambda slab: pl.BlockSpec((tc, cb), lambda j, s, slab=slab: (s, base + slab * nc + j))
    return pl.pallas_call(
        body,
        out_shape=_out((t, d_sc), BF16),
        grid=(nc, nt),
        in_specs=[blk(0), blk(1), blk(2), pl.BlockSpec((width, cb), lambda j, s: (0, j))],
        out_specs=pl.BlockSpec((tc, cb), lambda j, s: (s, j)),
        scratch_shapes=[pltpu.VMEM((SUBLANE, cb), F32)],
        compiler_params=_cparams(("parallel", "arbitrary"), 20 * tc * cb * 4),
        name="sc_fwd",
    )(_hbm(p), _hbm(p), _hbm(p), conv_w)


def _sc_bwd(p, dyp, conv_w, col0, d_sc, cb, tc, dp, slab0):
    t = p.shape[0]
    nc = d_sc // cb
    nt = t // tc
    width = conv_w.shape[0]
    base = col0 // cb
    halo_blocks = tc // SUBLANE

    def body(b_ref, c_ref, ch_ref, v_ref, vh_ref, dyp_ref, w_ref,
             dp_ref, dw_ref, nxt_dq):
        s = pl.program_id(1)

        @pl.when(s == 0)
        def _():
            nxt_dq[...] = jnp.zeros_like(nxt_dq)
            dw_ref[...] = jnp.zeros_like(dw_ref)

        keep = jnp.where(s == nt - 1, 0.0, 1.0)
        cvals = c_ref[...]
        vvals = v_ref[...]
        w = w_ref[...]
        catcv = jnp.concatenate([ch_ref[...] * vh_ref[...] * keep, cvals * vvals], axis=0)
        q = _conv_fwd(catcv, w, width)
        dyp_v = dyp_ref[...]
        dp_ref[0] = (dyp_v * q).astype(BF16)
        dq = dyp_v * b_ref[...]
        _conv_bwd_weight(dw_ref, dq, catcv, width)
        dcv = _conv_bwd_input(jnp.concatenate([dq, nxt_dq[...]], axis=0), w, width)
        nxt_dq[...] = dq[0:SUBLANE, :]
        dp_ref[1] = (dcv * vvals).astype(BF16)
        dp_ref[2] = (dcv * cvals).astype(BF16)

    rev = lambda s: nt - 1 - s
    blk = lambda slab: pl.BlockSpec((tc, cb), lambda j, s, slab=slab: (rev(s), base + slab * nc + j))
    halo = lambda slab: pl.BlockSpec(
        (SUBLANE, cb),
        lambda j, s, slab=slab: (jnp.maximum(rev(s) * halo_blocks - 1, 0), base + slab * nc + j))
    out_blk = pl.BlockSpec((tc, cb), lambda j, s: (rev(s), j))
    wblk = pl.BlockSpec((width, cb), lambda j, s: (0, j))
    assert dp.shape[2] == d_sc and slab0 % 3 == 0
    operands = [p, p, p, p, p, dyp, conv_w]
    body, in_specs, operands = _dep_args(
        body, [blk(0), blk(1), halo(1), blk(2), halo(2), out_blk, wblk], operands, dp)
    return pl.pallas_call(
        body,
        out_shape=[jax.ShapeDtypeStruct(dp.shape, dp.dtype), jax.ShapeDtypeStruct((width, d_sc), F32)],
        grid=(nc, nt),
        in_specs=in_specs,
        out_specs=[pl.BlockSpec((3, tc, cb), lambda j, s: (slab0 // 3, rev(s), j)), wblk],
        input_output_aliases={7: 0},
        scratch_shapes=[pltpu.VMEM((SUBLANE, cb), F32)],
        compiler_params=_cparams(("parallel", "arbitrary"), 30 * tc * cb * 4),
        name="sc_bwd",
    )(*operands)


def _merge_fwd(p, y_lru, y_sc, col0, tc):
    t, d = y_lru.shape
    cb = _pick(math.gcd(d, col0), 1024)
    nc = d // cb
    base = col0 // cb

    def body(gl_ref, gs_ref, yl_ref, ys_ref, o_ref):
        o_ref[...] = (_sigmoid(gl_ref[...]) * yl_ref[...] + _sigmoid(gs_ref[...]) * ys_ref[...]).astype(BF16)

    gate = lambda slab: pl.BlockSpec((tc, cb), lambda s, j, slab=slab: (s, base + slab * nc + j))
    blk = pl.BlockSpec((tc, cb), lambda s, j: (s, j))
    return pl.pallas_call(
        body,
        out_shape=_out((t, d), BF16),
        grid=(t // tc, nc),
        in_specs=[gate(0), gate(1), blk, blk],
        out_specs=blk,
        compiler_params=_cparams(("parallel", "parallel"), 2 * tc * cb * 20),
        name="merge_fwd",
    )(_hbm(p), _hbm(p), _hbm(y_lru), _hbm(y_sc))


def _merge_bwd(p, y_lru, y_sc, dmerged, col0, tc, n_slabs):
    t, d = y_lru.shape
    cb = _pick(math.gcd(d, col0), 1024)
    nc = d // cb
    base = col0 // cb

    def body(gl_ref, gs_ref, yl_ref, ys_ref, dm_ref, dp_ref, dyl_ref, dys_ref):
        dm = dm_ref[...]
        sl = _sigmoid(gl_ref[...])
        ss = _sigmoid(gs_ref[...])
        dp_ref[0] = (dm * yl_ref[...] * (sl * (1.0 - sl))).astype(BF16)
        dp_ref[1] = (dm * ys_ref[...] * (ss * (1.0 - ss))).astype(BF16)
        dyl_ref[...] = (dm * sl).astype(BF16)
        dys_ref[...] = (dm * ss).astype(BF16)

    gate = lambda slab: pl.BlockSpec((tc, cb), lambda s, j, slab=slab: (s, base + slab * nc + j))
    blk = pl.BlockSpec((tc, cb), lambda s, j: (s, j))
    act = _out((t, d), BF16)
    return pl.pallas_call(
        body,
        out_shape=[jax.ShapeDtypeStruct((n_slabs, t, cb), BF16), act, act],
        grid=(t // tc, nc),
        in_specs=[gate(0), gate(1), blk, blk, blk],
        out_specs=[pl.BlockSpec((2, tc, cb), lambda s, j: (j, s, 0)), blk, blk],
        compiler_params=_cparams(("parallel", "parallel"), 2 * tc * cb * 28),
        name="merge_bwd",
    )(_hbm(p), _hbm(p), _hbm(y_lru), _hbm(y_sc), _hbm(dmerged))


def _ffn_act_fwd(up, conv_w, d_ff, cb, tc):
    t = up.shape[0]
    nc = d_ff // cb
    nt = t // tc
    width = conv_w.shape[0]

    def body(g_ref, v_ref, wg_ref, wv_ref, o_ref, halo_g, halo_v):
        @pl.when(pl.program_id(1) == 0)
        def _():
            halo_g[...] = jnp.zeros_like(halo_g)
            halo_v[...] = jnp.zeros_like(halo_v)

        g = g_ref[...]
        v = v_ref[...]
        ug = _conv_fwd(jnp.concatenate([halo_g[...], g], axis=0), wg_ref[...], width)
        uv = _conv_fwd(jnp.concatenate([halo_v[...], v], axis=0), wv_ref[...], width)
        halo_g[...] = g[tc - SUBLANE:, :]
        halo_v[...] = v[tc - SUBLANE:, :]
        o_ref[...] = (ug * _sigmoid(ug) * uv).astype(BF16)

    blk = lambda half: pl.BlockSpec((tc, cb), lambda j, s, half=half: (s, half * nc + j))
    wblk = lambda half: pl.BlockSpec((width, cb), lambda j, s, half=half: (0, half * nc + j))
    return pl.pallas_call(
        body,
        out_shape=_out((t, d_ff), BF16),
        grid=(nc, nt),
        in_specs=[blk(0), blk(1), wblk(0), wblk(1)],
        out_specs=pl.BlockSpec((tc, cb), lambda j, s: (s, j)),
        scratch_shapes=[pltpu.VMEM((SUBLANE, cb), F32), pltpu.VMEM((SUBLANE, cb), F32)],
        compiler_params=_cparams(("parallel", "arbitrary"), 24 * tc * cb * 4),
        name="ffn_act_fwd",
    )(_hbm(up), _hbm(up), conv_w, conv_w)


def _ffn_act_bwd(up, dact, conv_w, d_ff, cb, tc, dep=None):
    t = up.shape[0]
    nc = d_ff // cb
    nt = t // tc
    width = conv_w.shape[0]
    halo_blocks = tc // SUBLANE

    def body(g_ref, gh_ref, v_ref, vh_ref, da_ref, wg_ref, wv_ref,
             dup_ref, dwg_ref, dwv_ref, nxt_g, nxt_v):
        s = pl.program_id(1)

        @pl.when(s == 0)
        def _():
            nxt_g[...] = jnp.zeros_like(nxt_g)
            nxt_v[...] = jnp.zeros_like(nxt_v)
            dwg_ref[...] = jnp.zeros_like(dwg_ref)
            dwv_ref[...] = jnp.zeros_like(dwv_ref)

        keep = jnp.where(s == nt - 1, 0.0, 1.0)
        wg = wg_ref[...]
        wv = wv_ref[...]
        catg = jnp.concatenate([gh_ref[...] * keep, g_ref[...]], axis=0)
        catv = jnp.concatenate([vh_ref[...] * keep, v_ref[...]], axis=0)
        ug = _conv_fwd(catg, wg, width)
        uv = _conv_fwd(catv, wv, width)
        sg = _sigmoid(ug)
        da = da_ref[...]
        dug = da * uv * (sg * (1.0 + ug * (1.0 - sg)))
        duv = da * (ug * sg)
        _conv_bwd_weight(dwg_ref, dug, catg, width)
        _conv_bwd_weight(dwv_ref, duv, catv, width)
        dup_ref[0] = _conv_bwd_input(jnp.concatenate([dug, nxt_g[...]], axis=0), wg, width).astype(BF16)
        dup_ref[1] = _conv_bwd_input(jnp.concatenate([duv, nxt_v[...]], axis=0), wv, width).astype(BF16)
        nxt_g[...] = dug[0:SUBLANE, :]
        nxt_v[...] = duv[0:SUBLANE, :]

    rev = lambda s: nt - 1 - s
    blk = lambda half: pl.BlockSpec((tc, cb), lambda j, s, half=half: (rev(s), half * nc + j))
    halo = lambda half: pl.BlockSpec(
        (SUBLANE, cb), lambda j, s, half=half: (jnp.maximum(rev(s) * halo_blocks - 1, 0), half * nc + j))
    wblk = lambda half: pl.BlockSpec((width, cb), lambda j, s, half=half: (0, half * nc + j))
    out_blk = pl.BlockSpec((tc, cb), lambda j, s: (rev(s), j))
    wout = pl.BlockSpec((width, cb), lambda j, s: (0, j))
    act = _out((t, d_ff), BF16)
    wshape = jax.ShapeDtypeStruct((width, d_ff), F32)
    body, in_specs, operands = _dep_args(
        body, [blk(0), halo(0), blk(1), halo(1), out_blk, wblk(0), wblk(1)],
        [_hbm(up), _hbm(up), _hbm(up), _hbm(up), _hbm(dact), conv_w, conv_w], dep)
    return pl.pallas_call(
        body,
        out_shape=[jax.ShapeDtypeStruct((2, t, d_ff), BF16), wshape, wshape],
        grid=(nc, nt),
        in_specs=in_specs,
        out_specs=[pl.BlockSpec((2, tc, cb), lambda j, s: (0, rev(s), j)), wout, wout],
        scratch_shapes=[pltpu.VMEM((SUBLANE, cb), F32), pltpu.VMEM((SUBLANE, cb), F32)],
        compiler_params=_cparams(("parallel", "arbitrary"), 40 * tc * cb * 4),
        name="ffn_act_bwd",
    )(*operands)


def _mesh_pos():
    x, y, c = lax.axis_index("x"), lax.axis_index("y"), lax.axis_index("c")
    return x, y, c


def _other_chips(x, y):
    return [(1 - x, y), (x, 1 - y), (1 - x, 1 - y)]


def _cast_place(w, chip, col_sharded, name):
    r, cdim = w.shape
    full = (r, cdim * N_CHIPS) if col_sharded else (r * N_CHIPS, cdim)
    rb = _pick(r, max(BF16_ROWS, (512 * 1024) // cdim), BF16_ROWS)
    nb = r // rb

    def body(chip_ref, w_ref, o_ref):
        o_ref[...] = w_ref[...].astype(BF16)

    if col_sharded:
        out_map = lambda i, chip_ref: (i, chip_ref[0])
    else:
        out_map = lambda i, chip_ref: (chip_ref[0] * nb + i, 0)
    grid_spec = pltpu.PrefetchScalarGridSpec(
        num_scalar_prefetch=1,
        grid=(nb,),
        in_specs=[pl.BlockSpec((rb, cdim), lambda i, chip_ref: (i, 0))],
        out_specs=pl.BlockSpec((rb, cdim), out_map),
    )
    return pl.pallas_call(
        body,
        out_shape=jax.ShapeDtypeStruct(full, BF16),
        grid_spec=grid_spec,
        compiler_params=_cparams(("parallel",), 2 * rb * cdim * 6),
        name=name,
    )(chip, _hbm(w))


def _remote(src, dst, send_sems, recv_sems, idx, to):
    return pltpu.make_async_remote_copy(
        src_ref=src, dst_ref=dst, send_sem=send_sems.at[idx], recv_sem=recv_sems.at[idx],
        device_id=to, device_id_type=MESH)


def _exchange(name, arrays, n_sems, plan):
    n = len(arrays)

    def body(*refs):
        bufs = refs[n:2 * n]
        send_sems, recv_sems = refs[2 * n:]
        sends, arrivals = plan(bufs, send_sems, recv_sems)
        for cp in sends:
            cp.start()
        for cp in arrivals:
            cp.wait_recv()
        for cp in sends:
            cp.wait_send()

    outs = pl.pallas_call(
        body,
        out_shape=[jax.ShapeDtypeStruct(a.shape, a.dtype) for a in arrays],
        in_specs=[ANY] * n,
        out_specs=[ANY] * n,
        input_output_aliases={k: k for k in range(n)},
        scratch_shapes=[pltpu.SemaphoreType.DMA((n_sems,)), pltpu.SemaphoreType.DMA((n_sems,))],
        name=name,
    )(*arrays)
    return list(outs)


def _exchange_start(name, arrays, n_sems, plan, after=None):
    n = len(arrays)
    n_in = n + (after is not None)

    def body(*refs):
        bufs = refs[:n]
        send_sems, recv_sems = refs[n_in], refs[n_in + 1]
        token = refs[-1]
        sends, _ = plan(bufs, send_sems, recv_sems)
        for cp in sends:
            cp.start()
        token[...] = jnp.zeros_like(token)

    out = pl.pallas_call(
        body,
        out_shape=(pltpu.SemaphoreType.DMA((n_sems,)), pltpu.SemaphoreType.DMA((n_sems,)),
                   *[pltpu.HBM(a.shape, a.dtype) for a in arrays],
                   jax.ShapeDtypeStruct((SUBLANE, LANE), F32)),
        in_specs=[HBM_SPEC] * n + [ANY] * (n_in - n),
        out_specs=(SEM_SPEC, SEM_SPEC, *[HBM_SPEC] * n, VMEM_SPEC),
        input_output_aliases={k: 2 + k for k in range(n)},
        compiler_params=pltpu.CompilerParams(has_side_effects=DATAFLOW_EFFECT),
        name=name,
    )(*[pltpu.with_memory_space_constraint(a, pltpu.HBM) for a in arrays], *([after] if after is not None else []))
    return out[0], out[1], list(out[2:2 + n]), out[-1]


def _exchange_wait(name, arrays, send_sems, recv_sems, after, plan):
    n = len(arrays)

    def body(*refs):
        bufs = refs[:n]
        sends, arrivals = plan(bufs, refs[n], refs[n + 1])
        for cp in arrivals:
            cp.wait_recv()
        for cp in sends:
            cp.wait_send()

    outs = pl.pallas_call(
        body,
        out_shape=[pltpu.HBM(a.shape, a.dtype) for a in arrays],
        in_specs=[HBM_SPEC] * n + [SEM_SPEC, SEM_SPEC, ANY],
        out_specs=[HBM_SPEC] * n,
        input_output_aliases={k: k for k in range(n)},
        compiler_params=pltpu.CompilerParams(has_side_effects=DATAFLOW_EFFECT),
        name=name,
    )(*arrays, send_sems, recv_sems, after)
    return list(outs)


def _half_block(ref, shard_shape, col_sharded, chip, half):
    r, cdim = shard_shape
    h = r // 2
    if col_sharded:
        return ref.at[pl.ds(pl.multiple_of(half * h, BF16_ROWS), h),
                      pl.ds(pl.multiple_of(chip * cdim, LANE), cdim)]
    return ref.at[pl.ds(pl.multiple_of(chip * r + half * h, BF16_ROWS), h), :]


def _gather_plan(shard_shapes, col_sharded, ks):
    def plan(bufs, send_sems, recv_sems):
        x, y, c = _mesh_pos()
        sends, arrivals = [], []
        for ref, k in zip(bufs, ks):
            mine = _half_block(ref, shard_shapes[k], col_sharded[k], 2 * x + y, c)
            for j, (px, py) in enumerate(_other_chips(x, y)):
                landed = _half_block(ref, shard_shapes[k], col_sharded[k], 2 * px + py, c)
                sends.append(_remote(mine, mine, send_sems, recv_sems, 3 * k + j, (px, py, c)))
                arrivals.append(_remote(landed, landed, send_sems, recv_sems, 3 * k + j, (px, py, c)))
        return sends, arrivals
    return plan


def _forward_plan(shard_shapes, col_sharded, ks):
    def plan(bufs, send_sems, recv_sems):
        x, y, c = _mesh_pos()
        sends, arrivals = [], []
        for i, (ref, k) in enumerate(zip(bufs, ks)):
            for j, (px, py) in enumerate(_other_chips(x, y)):
                landed = _half_block(ref, shard_shapes[k], col_sharded[k], 2 * px + py, c)
                theirs = _half_block(ref, shard_shapes[k], col_sharded[k], 2 * px + py, 1 - c)
                sends.append(_remote(landed, landed, send_sems, recv_sems, 3 * i + j, (x, y, 1 - c)))
                arrivals.append(_remote(theirs, theirs, send_sems, recv_sems, 3 * i + j, (x, y, 1 - c)))
        return sends, arrivals
    return plan


def _small_gather(small):
    def body(small_ref, out_ref, send_sems, recv_sems):
        x, y, c = _mesh_pos()
        me = 2 * x + y
        out_ref[me] = small_ref[...]
        copies = []
        for j, (px, py) in enumerate(_other_chips(x, y)):
            cp = _remote(small_ref, out_ref.at[me], send_sems, recv_sems, j, (px, py, c))
            cp.start()
            copies.append(cp)
        for j, (px, py) in enumerate(_other_chips(x, y)):
            _remote(small_ref, out_ref.at[2 * px + py], send_sems, recv_sems, j, (px, py, c)).wait_recv()
        for cp in copies:
            cp.wait_send()

    return pl.pallas_call(
        body,
        out_shape=jax.ShapeDtypeStruct((N_CHIPS,) + small.shape, small.dtype),
        in_specs=[VMEM_SPEC],
        out_specs=VMEM_SPEC,
        scratch_shapes=[pltpu.SemaphoreType.DMA((N_CHIPS - 1,)), pltpu.SemaphoreType.DMA((N_CHIPS - 1,))],
        name="gather_small",
    )(small)


def _as3d(g, col_sharded):
    r, cdim = g.shape
    return g.reshape(1, r, cdim) if col_sharded else g.reshape(N_CHIPS, r // N_CHIPS, cdim)


def _pair_plan(m):
    def plan(bufs, send_sems, recv_sems):
        x, y, c = _mesh_pos()
        copies = []
        for i in range(m):
            h = bufs[i].shape[1] // 2
            src = bufs[i].at[:, pl.ds(pl.multiple_of((1 - c) * h, SUBLANE), h), :]
            copies.append(_remote(src, bufs[m + i], send_sems, recv_sems, i, (x, y, 1 - c)))
        return copies, copies
    return plan


def _chip_plan(col_flags):
    m = len(col_flags)

    def plan(bufs, send_sems, recv_sems):
        x, y, c = _mesh_pos()
        copies = []
        for i in range(m):
            land = bufs[m + i]
            width = land.shape[2]
            for j, (px, py) in enumerate(_other_chips(x, y)):
                q = 2 * px + py
                if col_flags[i]:
                    src = bufs[i].at[0, :, pl.ds(pl.multiple_of(q * width, LANE), width)]
                else:
                    src = bufs[i].at[q]
                copies.append(_remote(src, land.at[j], send_sems, recv_sems, 3 * i + j, (px, py, c)))
        return copies, copies
    return plan


def _share_plan(m):
    def plan(bufs, send_sems, recv_sems):
        x, y, c = _mesh_pos()
        sends, arrivals = [], []
        for i in range(m):
            h = bufs[i].shape[0] // 2
            mine = bufs[i].at[pl.ds(pl.multiple_of(c * h, SUBLANE), h), :]
            theirs = bufs[i].at[pl.ds(pl.multiple_of((1 - c) * h, SUBLANE), h), :]
            sends.append(_remote(mine, mine, send_sems, recv_sems, i, (x, y, 1 - c)))
            arrivals.append(_remote(theirs, theirs, send_sems, recv_sems, i, (x, y, 1 - c)))
        return sends, arrivals
    return plan


def _pair_add(g3, other, core):
    a, r, cdim = g3.shape
    h = r // 2
    rb = _pick(h, max(BF16_ROWS, (512 * 1024) // cdim), BF16_ROWS)
    nb = h // rb

    def body(core_ref, g_ref, o_ref, out_ref):
        out_ref[...] = (g_ref[...] + o_ref[...]).astype(BF16)

    grid_spec = pltpu.PrefetchScalarGridSpec(
        num_scalar_prefetch=1,
        grid=(a, nb),
        in_specs=[pl.BlockSpec((None, rb, cdim), lambda i, j, core_ref: (i, core_ref[0] * nb + j, 0)),
                  pl.BlockSpec((None, rb, cdim), lambda i, j, core_ref: (i, j, 0))],
        out_specs=pl.BlockSpec((None, rb, cdim), lambda i, j, core_ref: (i, j, 0)),
    )
    return pl.pallas_call(
        body,
        out_shape=_out((a, h, cdim), BF16),
        grid_spec=grid_spec,
        compiler_params=_cparams(("parallel", "parallel"), 2 * rb * cdim * 10),
        name="grad_pair_add",
    )(core, _hbm(g3), _hbm(other))


def _small_allreduce(small):
    def body(small_ref, out_ref, slots, send_sems, recv_sems):
        x, y, c = _mesh_pos()
        my_dev = 4 * x + 2 * y + c
        slots[my_dev] = small_ref[...]
        copies = []
        for m in range(1, N_DEV):
            peer = (x ^ ((m >> 2) & 1), y ^ ((m >> 1) & 1), c ^ (m & 1))
            cp = _remote(small_ref, slots.at[my_dev], send_sems, recv_sems, m - 1, peer)
            cp.start()
            copies.append(cp)
        for m in range(1, N_DEV):
            _remote(small_ref, slots.at[my_dev ^ m], send_sems, recv_sems, m - 1, (x, y, c)).wait_recv()
        total = slots[0]
        for d in range(1, N_DEV):
            total = total + slots[d]
        out_ref[...] = total
        for cp in copies:
            cp.wait_send()

    return pl.pallas_call(
        body,
        out_shape=jax.ShapeDtypeStruct(small.shape, F32),
        in_specs=[VMEM_SPEC],
        out_specs=VMEM_SPEC,
        scratch_shapes=[pltpu.VMEM((N_DEV,) + small.shape, F32),
                        pltpu.SemaphoreType.DMA((N_DEV - 1,)), pltpu.SemaphoreType.DMA((N_DEV - 1,))],
        compiler_params=pltpu.CompilerParams(
            vmem_limit_bytes=min(VMEM_BUDGET, (N_DEV + 4) * _nbytes(small.shape, F32) + (8 << 20))),
        name="grad_small_allreduce",
    )(small)


def _chip_sum(partial, land, where, col_sharded):
    _, h, cdim = land.shape
    rb = _pick(h, max(BF16_ROWS, (512 * 1024) // cdim), BF16_ROWS)
    nb = h // rb

    def body(where_ref, own_ref, l_ref, o_ref):
        total = own_ref[...].astype(F32)
        for j in range(N_CHIPS - 1):
            total = total + l_ref[j].astype(F32)
        o_ref[...] = total

    if col_sharded:
        own_map = lambda i, w: (0, i, w[0])
    else:
        own_map = lambda i, w: (w[0], i, 0)
    grid_spec = pltpu.PrefetchScalarGridSpec(
        num_scalar_prefetch=1,
        grid=(nb,),
        in_specs=[pl.BlockSpec((None, rb, cdim), own_map),
                  pl.BlockSpec((N_CHIPS - 1, rb, cdim), lambda i, w: (0, i, 0))],
        out_specs=pl.BlockSpec((rb, cdim), lambda i, w: (w[1] * nb + i, 0)),
    )
    return pl.pallas_call(
        body,
        out_shape=_out((2 * h, cdim), F32),
        grid_spec=grid_spec,
        compiler_params=_cparams(("parallel",), 2 * rb * cdim * 12),
        name="grad_chip_sum",
    )(where, _hbm(partial), _hbm(land))


def _adamw(w, g, m, v, name, dep=None):
    r, cdim = w.shape
    rb = _pick(r, max(SUBLANE, (256 * 1024) // cdim), SUBLANE)
    c1 = 1.0 - ADAM_B1 ** ADAM_STEP
    c2 = 1.0 - ADAM_B2 ** ADAM_STEP

    def body(w_ref, g_ref, m_ref, v_ref, go_ref, d_ref, mo_ref, vo_ref):
        gv = g_ref[...]
        mn = ADAM_B1 * m_ref[...] + (1.0 - ADAM_B1) * gv
        vn = ADAM_B2 * v_ref[...] + (1.0 - ADAM_B2) * (gv * gv)
        m_hat = mn / c1
        v_hat = vn / c2
        d_ref[...] = -ADAM_LR * (m_hat / (jnp.sqrt(v_hat) + ADAM_EPS) + ADAM_WD * w_ref[...])
        go_ref[...] = gv
        mo_ref[...] = mn
        vo_ref[...] = vn

    blk = pl.BlockSpec((rb, cdim), lambda i: (i, 0))
    shape = jax.ShapeDtypeStruct((r, cdim), F32)
    body, in_specs, operands = _dep_args(body, [blk] * 4, [_hbm(w), _hbm(g), _hbm(m), _hbm(v)], dep)
    return pl.pallas_call(
        body,
        out_shape=[shape] * 4,
        grid=(r // rb,),
        in_specs=in_specs,
        out_specs=[blk] * 4,
        compiler_params=_cparams(("parallel",), 2 * rb * cdim * 4 * 8),
        name=name,
    )(*operands)


def _pack(arrays):
    tile = SUBLANE * LANE
    pieces = []
    for arr in arrays:
        flat = arr.reshape(-1)
        pad = (-flat.shape[0]) % tile
        if pad:
            flat = jnp.concatenate([flat, jnp.zeros((pad,), flat.dtype)])
        pieces.append(flat)
    return jnp.concatenate(pieces).reshape(-1, LANE)


def _unpack(packed, shapes):
    tile = SUBLANE * LANE
    flat = packed.reshape(-1)
    out, off = [], 0
    for shp in shapes:
        size = math.prod(shp)
        out.append(flat[off:off + size].reshape(shp))
        off += size + ((-size) % tile)
    return out


def _block_diag_groups(w, per_group):
    hcount, hd, _ = w.shape
    ng = hcount // per_group
    w4 = w.reshape(ng, per_group, hd, hd)
    eye = jnp.eye(per_group, dtype=w.dtype)
    bd = w4[:, :, :, None, :] * eye[None, :, None, :, None]
    return bd.reshape(ng, per_group * hd, per_group * hd).astype(BF16)


def _diag_blocks(wbd, per_group, hd):
    ng = wbd.shape[0]
    w5 = wbd.reshape(ng, per_group, hd, per_group, hd)
    blocks = [w5[:, i, :, i, :] for i in range(per_group)]
    return jnp.stack(blocks, axis=1).reshape(ng * per_group, hd, hd)


def kernel(x, g_mix, w_in, lru_conv_w, lru_conv_b, lru_wa, lru_ba, lru_wx, lru_bx, lru_lambda, lru_w_out, sc_conv_w, sc_w_out, w_o, g_ffn, ffn_w_up, ffn_conv_w, ffn_w_down, g_final, loss_target, m_g_mix, m_w_in, m_lru_conv_w, m_lru_conv_b, m_lru_wa, m_lru_ba, m_lru_wx, m_lru_bx, m_lru_lambda, m_lru_w_out, m_sc_conv_w, m_sc_w_out, m_w_o, m_g_ffn, m_ffn_w_up, m_ffn_conv_w, m_ffn_w_down, m_g_final, v_g_mix, v_w_in, v_lru_conv_w, v_lru_conv_b, v_lru_wa, v_lru_ba, v_lru_wx, v_lru_bx, v_lru_lambda, v_lru_w_out, v_sc_conv_w, v_sc_w_out, v_w_o, v_g_ffn, v_ffn_w_up, v_ffn_conv_w, v_ffn_w_down, v_g_final):
    seq, d_model = x.shape[1], x.shape[2]
    heads, head_dim, _ = lru_wa.shape
    d_lru = heads * head_dim
    d_sc = sc_w_out.shape[0]
    d_ff = ffn_w_down.shape[0] * N_CHIPS
    assert x.shape[0] == 1 and w_in.shape[1] * N_CHIPS == 2 * d_lru + 3 * d_sc + 2 * d_model
    xs = x.reshape(seq, d_model)
    target = loss_target.reshape(seq, d_model)

    chip = 2 * lax.axis_index("x") + lax.axis_index("y")
    core = lax.axis_index("c").astype(jnp.int32).reshape(1)

    big_w = [w_in, lru_w_out, sc_w_out, w_o, ffn_w_up, ffn_w_down]
    big_m = [m_w_in, m_lru_w_out, m_sc_w_out, m_w_o, m_ffn_w_up, m_ffn_w_down]
    big_v = [v_w_in, v_lru_w_out, v_sc_w_out, v_w_o, v_ffn_w_up, v_ffn_w_down]
    col_sharded = [True, True, True, False, True, False]
    conv_shards = [lru_conv_w, sc_conv_w, ffn_conv_w]
    conv_pack = jnp.concatenate(
        [jnp.pad(w, ((0, SUBLANE - w.shape[0]), (0, 0))) for w in conv_shards], axis=1)
    big_names = ["w_in", "lru_w_out", "sc_w_out", "w_o", "ffn_w_up", "ffn_w_down"]
    chip_arr = chip.astype(jnp.int32).reshape(1)
    placed = [_cast_place(w, chip_arr, cs, "cast_" + nm) for w, cs, nm in zip(big_w, col_sharded, big_names)]
    conv_all = _small_gather(conv_pack)
    shard_shapes = [w.shape for w in big_w]
    n_big = len(big_w)

    def gather_start(ks, after, tag):
        send, recv, bufs, token = _exchange_start(
            "gather_start_" + tag, [placed[k] for k in ks], 3 * n_big,
            _gather_plan(shard_shapes, col_sharded, ks), after=after)
        return (send, recv, dict(zip(ks, bufs))), token

    def arrived(state, ks, after, tag):
        send, recv, bufs = state
        got = _exchange_wait("gather_wait_" + tag, [bufs[k] for k in ks], send, recv, after,
                             _gather_plan(shard_shapes, col_sharded, ks))
        return _exchange("gather_forward_" + tag, got, 3 * len(ks), _forward_plan(shard_shapes, col_sharded, ks))

    conv_full, off = [], 0
    for w in conv_shards:
        kw, nq = w.shape
        piece = conv_all[:, :kw, off:off + nq]
        conv_full.append(piece.transpose(1, 0, 2).reshape(kw, N_CHIPS * nq))
        off += nq
    lcw, scw, fcw = conv_full

    per_group = max(1, min(heads, 256 // head_dim))
    gc = per_group * head_dim
    wa_bd = _block_diag_groups(lru_wa, per_group)
    wx_bd = _block_diag_groups(lru_wx, per_group)
    tc = _pick(seq, 256, SUBLANE)
    cb_sc = _pick(d_sc, 512)
    cb_ff = _pick(d_ff, 512)
    col_sc = 2 * d_lru
    col_gates = 2 * d_lru + 3 * d_sc

    first, token = gather_start([0], conv_all, "in")
    h1 = _rms_fwd(xs, g_mix, "rms_mix", dep=token)
    (win_b,) = arrived(first, [0], h1, "in")
    rest, token = gather_start([1, 2, 3, 4, 5], win_b, "rest")
    p = _mm(h1, win_b, "nn", F32, name="mm_in", dep=token)
    wlo_b, wso_b, wo_b = arrived(rest, [1, 2, 3], p, "mix")
    y_lru_pre, hseq = _lru_fwd(p, lcw, lru_conv_b, wa_bd, lru_ba, wx_bd, lru_bx, lru_lambda, d_lru, gc, tc)
    y_sc_pre = _sc_fwd(p, scw, col_sc, d_sc, cb_sc, tc)
    y_lru = _mm(y_lru_pre, wlo_b, "nn", F32, name="mm_lru_out")
    y_sc = _mm(y_sc_pre, wso_b, "nn", F32, name="mm_sc_out")
    merged = _merge_fwd(p, y_lru, y_sc, col_gates, tc)
    x2 = _mm(merged, wo_b, "nn", F32, res=xs, name="mm_o")
    (wup_b,) = arrived(rest, [4], x2, "up")
    h2 = _rms_fwd(x2, g_ffn, "rms_ffn")
    up = _mm(h2, wup_b, "nn", F32, name="mm_up")
    (wdn_b,) = arrived(rest, [5], up, "down")
    act = _ffn_act_fwd(up, fcw, d_ff, cb_ff, tc)
    x3 = _mm(act, wdn_b, "nn", F32, res=x2, name="mm_down")
    loss_part, dx3, dx3b, dg_final = _loss_head(x3, g_final, target)

    where = jnp.concatenate([chip_arr, core])

    def reduce_start(grads, flags, tag):
        views = [_as3d(g, cs) for g, cs in zip(grads, flags)]
        lands = [lax.empty((v.shape[0], v.shape[1] // 2, v.shape[2]), F32) for v in views]
        send, recv, bufs, token = _exchange_start("grad_pair_start_" + tag, views + lands, len(views),
                                                  _pair_plan(len(views)))
        return (send, recv, bufs, flags, tag), token

    def reduce_mid(state, after):
        send, recv, bufs, flags, tag = state
        m = len(flags)
        bufs = _exchange_wait("grad_pair_wait_" + tag, bufs, send, recv, after, _pair_plan(m))
        partials = [_pair_add(bufs[i], bufs[m + i], core) for i in range(m)]
        lands = []
        for pz, cs in zip(partials, flags):
            _, h, cdim = pz.shape
            lands.append(lax.empty((N_CHIPS - 1, h, cdim // N_CHIPS if cs else cdim), BF16))
        send, recv, bufs, token = _exchange_start("grad_chip_start_" + tag, partials + lands, 3 * m,
                                                  _chip_plan(flags))
        return (send, recv, bufs, flags, tag), token

    def reduce_end(state, after):
        send, recv, bufs, flags, tag = state
        m = len(flags)
        bufs = _exchange_wait("grad_chip_wait_" + tag, bufs, send, recv, after, _chip_plan(flags))
        return [_chip_sum(bufs[i], bufs[m + i], where, flags[i]) for i in range(m)]

    g_wdn = _mm(act, dx3b, "tn", F32, name="mm_down_dw")
    red_down, token = reduce_start([g_wdn], [False], "down")
    dact = _mm(dx3b, wdn_b, "nt", F32, name="mm_down_dx", dep=token)
    red_down, token = reduce_mid(red_down, dact)
    dup, dfcw_g, dfcw_v = _ffn_act_bwd(up, dact, fcw, d_ff, cb_ff, tc, dep=token)
    g_wup = _mm(h2, dup, "tn", F32, name="mm_up_dw", slabs=[0, 1])
    red_up, token = reduce_start([g_wup], [True], "up")
    dh2 = _mm(dup, wup_b, "nt", F32, name="mm_up_dx", dep=token, slabs=[0, 1])
    red_up, token = reduce_mid(red_up, dh2)
    dx2, dx2b, dg_ffn = _rms_bwd(x2, g_ffn, dh2, dx3, "rms_ffn_bwd", True, dep=token)
    g_wo = _mm(merged, dx2b, "tn", F32, name="mm_o_dw")
    dmerged = _mm(dx2b, wo_b, "nt", F32, name="mm_o_dx")
    slab_w = d_lru
    assert d_sc == slab_w and d_model % slab_w == 0 and col_gates % slab_w == 0
    n_gate = d_model // slab_w
    gate0 = col_gates // slab_w
    dp_slabs = [gate0 + kind * n_gate + j for j in range(n_gate) for kind in (0, 1)] + [0, 1, 2, 3, 4]
    dp, dyl, dys = _merge_bwd(p, y_lru, y_sc, dmerged, col_gates, tc, len(dp_slabs))
    assert dp.shape[2] == slab_w
    g_wlo = _mm(y_lru_pre, dyl, "tn", F32, name="mm_lru_out_dw")
    g_wso = _mm(y_sc_pre, dys, "tn", F32, name="mm_sc_out_dw")
    red_mix, token = reduce_start([g_wlo, g_wso, g_wo], [True, True, False], "mix")
    dylp = _mm(dyl, wlo_b, "nt", F32, name="mm_lru_out_dx", dep=token)
    dysp = _mm(dys, wso_b, "nt", F32, name="mm_sc_out_dx")
    red_mix, token = reduce_mid(red_mix, dysp)
    dp, dlcw, dlcb, dwa_bd, dba, dwx_bd, dbx, dlam = _lru_bwd(
        p, hseq, dylp, lcw, lru_conv_b, wa_bd, lru_ba, wx_bd, lru_bx, lru_lambda, d_lru, gc, tc,
        dp, 2 * n_gate, dep=token)
    dp, dscw = _sc_bwd(p, dysp, scw, col_sc, d_sc, cb_sc, tc, dp, 2 * n_gate + 2)
    g_win = _mm(h1, dp, "tn", F32, name="mm_in_dw", slabs=dp_slabs)
    red_in, token = reduce_start([g_win], [True], "in")
    dh1 = _mm(dp, win_b, "nt", F32, name="mm_in_dx", dep=token, slabs=dp_slabs)
    grad_x, dg_mix = _rms_bwd(xs, g_mix, dh1, dx2, "rms_mix_bwd", False)

    small_g = [dg_mix, dlcw, dlcb, _diag_blocks(dwa_bd, per_group, head_dim), dba,
               _diag_blocks(dwx_bd, per_group, head_dim), dbx, dlam, dscw, dg_ffn,
               jnp.concatenate([dfcw_g, dfcw_v], axis=1), dg_final]
    small_shapes = [a.shape for a in small_g]
    small_sum = _small_allreduce(_pack(small_g))
    red_in, token = reduce_mid(red_in, small_sum)
    (h_wdn,) = reduce_end(red_down, token)
    (h_wup,) = reduce_end(red_up, token)
    h_wlo, h_wso, h_wo = reduce_end(red_mix, token)
    s_wlo, s_wso, s_wo, s_wup, s_wdn = _exchange("grad_share_a", [h_wlo, h_wso, h_wo, h_wup, h_wdn], 5,
                                                 _share_plan(5))
    early = {1: s_wlo, 2: s_wso, 3: s_wo, 4: s_wup, 5: s_wdn}
    big_out = [None] * n_big
    last = None
    for k, g in early.items():
        big_out[k] = _adamw(big_w[k], g, big_m[k], big_v[k], "adamw_" + big_names[k], dep=last)
        last = big_out[k][1]
    (h_win,) = reduce_end(red_in, last)
    (s_win,) = _exchange("grad_share_b", [h_win], 1, _share_plan(1))
    big_out[0] = _adamw(big_w[0], s_win, big_m[0], big_v[0], "adamw_" + big_names[0])
    sg = _unpack(small_sum, small_shapes)
    for idx in (1, 8, 10):
        nq = sg[idx].shape[1] // N_CHIPS
        sg[idx] = lax.dynamic_slice_in_dim(sg[idx], chip * nq, nq, axis=1)
    small_w = [g_mix, lru_conv_w, lru_conv_b, lru_wa, lru_ba, lru_wx, lru_bx, lru_lambda, sc_conv_w,
               g_ffn, ffn_conv_w, g_final]
    small_m = [m_g_mix, m_lru_conv_w, m_lru_conv_b, m_lru_wa, m_lru_ba, m_lru_wx, m_lru_bx, m_lru_lambda,
               m_sc_conv_w, m_g_ffn, m_ffn_conv_w, m_g_final]
    small_v = [v_g_mix, v_lru_conv_w, v_lru_conv_b, v_lru_wa, v_lru_ba, v_lru_wx, v_lru_bx, v_lru_lambda,
               v_sc_conv_w, v_g_ffn, v_ffn_conv_w, v_g_final]
    sg = [g.reshape(w.shape) for g, w in zip(sg, small_w)]
    w_shapes = [w.shape for w in small_w]
    packed = _adamw(_pack(small_w), _pack(sg), _pack(small_m), _pack(small_v), "adamw_small")
    small_out = [_unpack(pk, w_shapes) for pk in packed]

    order = [(0, 0), (1, 0), (0, 1), (0, 2), (0, 3), (0, 4), (0, 5), (0, 6), (0, 7), (1, 1), (0, 8), (1, 2),
             (1, 3), (0, 9), (1, 4), (0, 10), (1, 5), (0, 11)]
    by_kind = []
    for kind in range(4):
        by_kind.append([big_out[i][kind] if is_big else small_out[kind][i] for is_big, i in order])
    loss = lax.psum(loss_part[0, 0], ("x", "y", "c"))
    return (loss, grad_x.reshape(x.shape), *by_kind[0], *by_kind[1], *by_kind[2], *by_kind[3])
```

```python
import functools
import math

import jax
import jax.numpy as jnp
from jax import lax
from jax.experimental import pallas as pl
from jax.experimental.pallas import tpu as pltpu

F32 = jnp.float32
BF16 = jnp.bfloat16

LANE = 128
SUBLANE = 8
BF16_ROWS = 16
VMEM_BYTES_V7X = 64 * 1024 * 1024
VMEM_BUDGET = VMEM_BYTES_V7X - 8 * 1024 * 1024
MM_VMEM_BUDGET = 42 * 1024 * 1024

EPS = 1e-6
LRU_C = 8.0
ADAM_LR = 0.001
ADAM_B1 = 0.9
ADAM_B2 = 0.999
ADAM_EPS = 1e-08
ADAM_WD = 0.01
ADAM_STEP = 10

N_CHIPS = 4
N_DEV = 8
MESH = pl.DeviceIdType.MESH
ANY = pl.BlockSpec(memory_space=pl.ANY)
VMEM_SPEC = pl.BlockSpec(memory_space=pltpu.VMEM)
HBM_SPEC = pl.BlockSpec(memory_space=pltpu.HBM)
SEM_SPEC = pl.BlockSpec(memory_space=pltpu.SEMAPHORE)
DATAFLOW_EFFECT = pltpu.SideEffectType.DATAFLOW_SIDE_EFFECTING


def _pick(n, cap, mult=LANE):
    best = None
    d = mult
    while d <= min(n, cap):
        if n % d == 0:
            best = d
        d += mult
    return n if best is None else best


def _cparams(semantics, block_bytes):
    limit = min(VMEM_BUDGET, max(32 * 1024 * 1024, int(block_bytes * 1.25) + (4 << 20)))
    return pltpu.CompilerParams(dimension_semantics=semantics, vmem_limit_bytes=limit)


def _nbytes(shape, dtype):
    return math.prod(shape) * jnp.dtype(dtype).itemsize


def _sigmoid(z):
    return 1.0 / (1.0 + jnp.exp(-z))


def _softplus(z):
    e = jnp.exp(-jnp.abs(z))
    u = 1.0 + e
    log1p = jnp.where(u == 1.0, e, jnp.log(u) * (e / (u - 1.0)))
    return jnp.maximum(z, 0.0) + log1p


def _neg_expm1(z):
    small = z * (1.0 + z * (0.5 + z * (1.0 / 6.0 + z * (1.0 / 24.0))))
    return -jnp.where(jnp.abs(z) < 0.03, small, jnp.exp(z) - 1.0)


_GELU_K = math.sqrt(2.0 / math.pi)
_GELU_C = 0.044715


def _gelu_and_grad(z):
    z2 = z * z
    th = jnp.tanh(_GELU_K * (z + _GELU_C * z2 * z))
    val = 0.5 * z * (1.0 + th)
    grad = 0.5 * (1.0 + th) + 0.5 * z * (1.0 - th * th) * (_GELU_K * (1.0 + 3.0 * _GELU_C * z2))
    return val, grad


def _rows_before(cat, k):
    if k == 0:
        return cat[SUBLANE:, :]
    return pltpu.roll(cat, k, 0)[SUBLANE:, :]


def _rows_after(cat, k):
    n = cat.shape[0]
    if k == 0:
        return cat[:n - SUBLANE, :]
    return pltpu.roll(cat, n - k, 0)[:n - SUBLANE, :]


def _conv_fwd(cat, w, width):
    y = _rows_before(cat, width - 1) * w[0:1, :]
    for k in range(1, width):
        y = y + _rows_before(cat, width - 1 - k) * w[k:k + 1, :]
    return y


def _conv_bwd_input(cat, w, width):
    dx = _rows_after(cat, width - 1) * w[0:1, :]
    for k in range(1, width):
        dx = dx + _rows_after(cat, width - 1 - k) * w[k:k + 1, :]
    return dx


def _conv_bwd_weight(dw_ref, dy, catx, width):
    for k in range(width):
        dw_ref[k:k + 1, :] += jnp.sum(dy * _rows_before(catx, width - 1 - k), axis=0, keepdims=True)


def _scan_tiles(a_ref, b_ref, out_ref, carry0, n_rows, reverse):
    cols = a_ref.shape[1]
    row = lax.broadcasted_iota(jnp.int32, (SUBLANE, cols), 0)
    n_tiles = n_rows // SUBLANE

    def step(j, carry):
        tile = (n_tiles - 1 - j) if reverse else j
        off = pl.multiple_of(tile * SUBLANE, SUBLANE)
        a = a_ref[pl.ds(off, SUBLANE), :]
        b = b_ref[pl.ds(off, SUBLANE), :]
        for s in (1, 2, 4):
            if reverse:
                keep = row < SUBLANE - s
                shift = SUBLANE - s
            else:
                keep = row >= s
                shift = s
            a_sh = jnp.where(keep, pltpu.roll(a, shift, 0), 1.0)
            b_sh = jnp.where(keep, pltpu.roll(b, shift, 0), 0.0)
            b = a * b_sh + b
            a = a * a_sh
        out = a * carry + b
        out_ref[pl.ds(off, SUBLANE), :] = out
        return out[0:1, :] if reverse else out[SUBLANE - 1:SUBLANE, :]

    return lax.fori_loop(0, n_tiles, step, carry0)


def _out(shape, dtype):
    return jax.ShapeDtypeStruct(shape, dtype)


def _hbm(x):
    return x


def _dep_args(body, in_specs, operands, *deps):
    deps = [d for d in deps if d is not None]
    if not deps:
        return body, in_specs, operands
    n = len(operands)

    def wrapped(*refs):
        return body(*refs[:n], *refs[n + len(deps):])

    return wrapped, list(in_specs) + [ANY] * len(deps), list(operands) + deps


def _mm(a, b, mode, out_dtype, res=None, name=None, dep=None, slabs=None):
    assert a.dtype == BF16 and b.dtype == BF16
    a_slabbed, b_slabbed = a.ndim == 3, b.ndim == 3
    assert not a_slabbed or (mode == "nt" and slabs is not None)
    assert not b_slabbed or (mode == "tn" and slabs is not None)
    if mode == "nn":
        (m, k), (k2, n) = a.shape, b.shape
        dims = (((1,), (0,)), ((), ()))
    elif mode == "nt":
        m, k = (a.shape[1], a.shape[0] * a.shape[2]) if a_slabbed else a.shape
        n, k2 = b.shape
        dims = (((1,), (1,)), ((), ()))
    else:
        k, m = a.shape
        k2, n = (b.shape[1], b.shape[0] * b.shape[2]) if b_slabbed else b.shape
        dims = (((0,), (0,)), ((), ()))
    assert k == k2
    n_unit = b.shape[2] if b_slabbed else n
    out_bytes = jnp.dtype(out_dtype).itemsize
    bm = _pick(m, 1024)
    bn = _pick(n_unit, 1024)
    bk = k

    def est(bm_, bn_, bk_):
        e = 2 * (bm_ * bk_ + bk_ * bn_) * 2 + 2 * bm_ * bn_ * out_bytes
        if k // bk_ > 1:
            e += bm_ * bn_ * 4
        if res is not None:
            e += 2 * bm_ * bn_ * 4
        return e

    for shrink_n, floor in ((True, 512), (False, 512), (True, 256), (False, 256)):
        while est(bm, bn, bk) > MM_VMEM_BUDGET:
            if shrink_n and bn > floor and bn % 2 == 0 and n_unit % (bn // 2) == 0:
                bn //= 2
            elif not shrink_n and bm > floor and bm % 2 == 0 and m % (bm // 2) == 0:
                bm //= 2
            else:
                break
    while (est(bm, bn, bk) > MM_VMEM_BUDGET and not a_slabbed and bk % (2 * LANE) == 0
           and k % (bk // 2) == 0):
        bk //= 2
    nk = k // bk
    per_slab = n_unit // bn

    def out_col(j):
        if not b_slabbed:
            return j
        s = j // per_slab
        where = sum(jnp.where(s == t, slabs[t], 0) for t in range(len(slabs)))
        return where * per_slab + j % per_slab

    if mode == "tn":
        a_spec = pl.BlockSpec((bk, bm), lambda i, j, kk: (kk, i))
    elif a_slabbed:
        a_spec = pl.BlockSpec((a.shape[0], bm, a.shape[2]), lambda i, j, kk: (0, i, 0))
    else:
        a_spec = pl.BlockSpec((bm, bk), lambda i, j, kk: (i, kk))
    if mode == "nt":
        b_spec = pl.BlockSpec((bn, bk), lambda i, j, kk: (j, kk))
    elif b_slabbed:
        b_spec = pl.BlockSpec((None, bk, bn), lambda i, j, kk: (j // per_slab, kk, j % per_slab))
    else:
        b_spec = pl.BlockSpec((bk, bn), lambda i, j, kk: (kk, j))
    o_spec = pl.BlockSpec((bm, bn), lambda i, j, kk: (i, out_col(j)))
    in_specs = [a_spec, b_spec]
    operands = [a, b]
    if res is not None:
        in_specs.append(o_spec)
        operands.append(res)
    has_res = res is not None

    def body(*refs):
        a_ref, b_ref = refs[0], refs[1]
        res_ref = refs[2] if has_res else None
        o_ref = refs[2 + has_res]
        if a_slabbed:
            width = a_ref.shape[2]
            part = None
            for s, col in enumerate(slabs):
                term = lax.dot_general(a_ref[s], b_ref[:, col * width:(col + 1) * width], dims,
                                       preferred_element_type=F32)
                part = term if part is None else part + term
        else:
            part = lax.dot_general(a_ref[...], b_ref[...], dims, preferred_element_type=F32)
        if nk == 1:
            if has_res:
                part = part + res_ref[...]
            o_ref[...] = part.astype(o_ref.dtype)
            return
        acc_ref = refs[-1]
        kk = pl.program_id(2)

        @pl.when(kk == 0)
        def _():
            acc_ref[...] = part

        @pl.when(kk > 0)
        def _():
            acc_ref[...] += part

        @pl.when(kk == nk - 1)
        def _():
            total = acc_ref[...]
            if has_res:
                total = total + res_ref[...]
            o_ref[...] = total.astype(o_ref.dtype)

    scratch = [pltpu.VMEM((bm, bn), F32)] if nk > 1 else []
    body, in_specs, operands = _dep_args(body, in_specs, operands, dep)
    return pl.pallas_call(
        body,
        out_shape=jax.ShapeDtypeStruct((m, n), out_dtype),
        grid=(m // bm, n // bn, nk),
        in_specs=in_specs,
        out_specs=o_spec,
        scratch_shapes=scratch,
        compiler_params=_cparams(("parallel", "parallel", "arbitrary"), est(bm, bn, bk)),
        name=name,
    )(*operands)


def _rms_fwd(x, g, name, dep=None):
    t, d = x.shape
    tb = _pick(t, 512, SUBLANE)

    def body(x_ref, g_ref, h_ref):
        xv = x_ref[...]
        r = lax.rsqrt(jnp.mean(xv * xv, axis=-1, keepdims=True) + EPS)
        h_ref[...] = ((xv * r) * g_ref[...]).astype(BF16)

    blk = pl.BlockSpec((tb, d), lambda i: (i, 0))
    body, in_specs, operands = _dep_args(
        body, [blk, pl.BlockSpec((1, d), lambda i: (0, 0))], [_hbm(x), g.reshape(1, d)], dep)
    return pl.pallas_call(
        body,
        out_shape=_out((t, d), BF16),
        grid=(t // tb,),
        in_specs=in_specs,
        out_specs=blk,
        compiler_params=_cparams(("parallel",), 2 * tb * d * 6),
        name=name,
    )(*operands)


def _rms_bwd(x, g, dh, dres, name, want_bf16, dep=None):
    t, d = x.shape
    tb = _pick(t, 256, SUBLANE)

    def body(x_ref, g_ref, dh_ref, dres_ref, *outs):
        dx_ref, dg_ref = outs[0], outs[-1]
        xv = x_ref[...]
        r = lax.rsqrt(jnp.mean(xv * xv, axis=-1, keepdims=True) + EPS)
        xhat = xv * r
        dhv = dh_ref[...]
        dxhat = dhv * g_ref[...]
        dx = dres_ref[...] + r * (dxhat - xhat * jnp.mean(dxhat * xhat, axis=-1, keepdims=True))
        dx_ref[...] = dx
        if want_bf16:
            outs[1][...] = dx.astype(BF16)

        @pl.when(pl.program_id(0) == 0)
        def _():
            dg_ref[...] = jnp.zeros_like(dg_ref)

        dg_ref[...] += jnp.sum(dhv * xhat, axis=0, keepdims=True)

    blk = pl.BlockSpec((tb, d), lambda i: (i, 0))
    row = pl.BlockSpec((1, d), lambda i: (0, 0))
    out_shape = [_out((t, d), F32)]
    out_specs = [blk]
    if want_bf16:
        out_shape.append(_out((t, d), BF16))
        out_specs.append(blk)
    out_shape.append(jax.ShapeDtypeStruct((1, d), F32))
    out_specs.append(row)
    body, in_specs, operands = _dep_args(
        body, [blk, row, blk, blk], [_hbm(x), g.reshape(1, d), _hbm(dh), _hbm(dres)], dep)
    return pl.pallas_call(
        body,
        out_shape=out_shape,
        grid=(t // tb,),
        in_specs=in_specs,
        out_specs=out_specs,
        compiler_params=_cparams(("arbitrary",), 2 * tb * d * 18),
        name=name,
    )(*operands)


def _loss_head(x3, g, target):
    t, d = x3.shape
    tb = _pick(t, 256, SUBLANE)

    def body(x_ref, g_ref, t_ref, loss_ref, dx_ref, dxb_ref, dg_ref):
        xv = x_ref[...]
        gv = g_ref[...]
        r = lax.rsqrt(jnp.mean(xv * xv, axis=-1, keepdims=True) + EPS)
        xhat = xv * r
        err = xhat * gv - t_ref[...]
        dy = err * (1.0 / d)
        dxhat = dy * gv
        dx = r * (dxhat - xhat * jnp.mean(dxhat * xhat, axis=-1, keepdims=True))
        dx_ref[...] = dx
        dxb_ref[...] = dx.astype(BF16)

        @pl.when(pl.program_id(0) == 0)
        def _():
            dg_ref[...] = jnp.zeros_like(dg_ref)
            loss_ref[...] = jnp.zeros_like(loss_ref)

        dg_ref[...] += jnp.sum(dy * xhat, axis=0, keepdims=True)
        per_token = jnp.mean(err * err, axis=-1, keepdims=True)
        loss_ref[...] += 0.5 * jnp.sum(per_token, axis=0, keepdims=True)

    blk = pl.BlockSpec((tb, d), lambda i: (i, 0))
    row = pl.BlockSpec((1, d), lambda i: (0, 0))
    return pl.pallas_call(
        body,
        out_shape=[jax.ShapeDtypeStruct((1, 1), F32), _out((t, d), F32),
                   _out((t, d), BF16), jax.ShapeDtypeStruct((1, d), F32)],
        grid=(t // tb,),
        in_specs=[blk, row, blk],
        out_specs=[pl.BlockSpec((1, 1), lambda i: (0, 0)), blk, blk, row],
        compiler_params=_cparams(("arbitrary",), 2 * tb * d * 14),
        name="loss_head",
    )(_hbm(x3), g.reshape(1, d), _hbm(target))


def _lru_gates(xc, wa, ba, wx, bx, lam):
    nn = (((1,), (0,)), ((), ()))
    xcb = xc.astype(BF16)
    r = _sigmoid(lax.dot_general(xcb, wa, nn, preferred_element_type=F32) + ba)
    i = _sigmoid(lax.dot_general(xcb, wx, nn, preferred_element_type=F32) + bx)
    cl = -LRU_C * _softplus(-lam)
    log_a = cl * r
    a = jnp.exp(log_a)
    one_minus_a2 = _neg_expm1(2.0 * log_a)
    return xcb, r, i, a, one_minus_a2, cl


def _lru_fwd(p, conv_w, conv_b, wa_bd, ba, wx_bd, bx, lam, d_lru, gc, tc):
    t = p.shape[0]
    ng = d_lru // gc
    nt = t // tc
    width = conv_w.shape[0]

    def body(lx_ref, gate_ref, cw_ref, cb_ref, wa_ref, ba_ref, wx_ref, bx_ref, lam_ref,
             y_ref, h_ref, halo, hcar, a_s, u_s):
        @pl.when(pl.program_id(1) == 0)
        def _():
            halo[...] = jnp.zeros_like(halo)
            hcar[...] = jnp.zeros_like(hcar)

        x = lx_ref[...]
        cat = jnp.concatenate([halo[...], x], axis=0)
        halo[...] = x[tc - SUBLANE:, :]
        xc = _conv_fwd(cat, cw_ref[...], width) + cb_ref[...]
        _, r, i, a, om, _ = _lru_gates(xc, wa_ref[...], ba_ref[...], wx_ref[...], bx_ref[...], lam_ref[...])
        a_s[...] = a
        u_s[...] = jnp.sqrt(om) * (i * xc)
        hcar[0:1, :] = _scan_tiles(a_s, u_s, h_ref, hcar[0:1, :], tc, reverse=False)
        gl, _ = _gelu_and_grad(gate_ref[...])
        y_ref[...] = (gl * h_ref[...]).astype(BF16)

    blk = lambda off: pl.BlockSpec((tc, gc), lambda g, s, off=off: (s, off + g))
    rowv = lambda rows: pl.BlockSpec((rows, gc), lambda g, s: (0, g))
    wspec = pl.BlockSpec((None, gc, gc), lambda g, s: (g, 0, 0))
    out_blk = pl.BlockSpec((tc, gc), lambda g, s: (s, g))
    return pl.pallas_call(
        body,
        out_shape=[_out((t, d_lru), BF16), _out((t, d_lru), F32)],
        grid=(ng, nt),
        in_specs=[blk(0), blk(ng), rowv(width), rowv(1), wspec, rowv(1), wspec, rowv(1), rowv(1)],
        out_specs=[out_blk, out_blk],
        scratch_shapes=[pltpu.VMEM((SUBLANE, gc), F32), pltpu.VMEM((SUBLANE, gc), F32),
                        pltpu.VMEM((tc, gc), F32), pltpu.VMEM((tc, gc), F32)],
        compiler_params=_cparams(("parallel", "arbitrary"), 40 * tc * gc * 4),
        name="lru_fwd",
    )(_hbm(p), _hbm(p), conv_w, conv_b.reshape(1, -1), wa_bd, ba.reshape(1, -1), wx_bd, bx.reshape(1, -1),
      lam.reshape(1, -1))


def _lru_bwd(p, hseq, dyp, conv_w, conv_b, wa_bd, ba, wx_bd, bx, lam, d_lru, gc, tc, dp, slab0, dep=None):
    t = p.shape[0]
    ng = d_lru // gc
    nt = t // tc
    width = conv_w.shape[0]
    halo_blocks = tc // SUBLANE
    nn = (((1,), (0,)), ((), ()))
    nt_dims = (((1,), (1,)), ((), ()))
    tn_dims = (((0,), (0,)), ((), ()))

    def body(lx_ref, lxh_ref, gate_ref, h_ref, hh_ref, dyp_ref,
             cw_ref, cb_ref, wa_ref, ba_ref, wx_ref, bx_ref, lam_ref,
             dp_ref, dcw_ref, dcb_ref, dwa_ref, dba_ref, dwx_ref, dbx_ref, dlam_ref,
             nxt_dxc, nxt_a, nxt_g, al_s, b_s, g_s):
        s = pl.program_id(1)
        first_chunk = s == nt - 1

        @pl.when(s == 0)
        def _():
            nxt_dxc[...] = jnp.zeros_like(nxt_dxc)
            nxt_a[...] = jnp.zeros_like(nxt_a)
            nxt_g[...] = jnp.zeros_like(nxt_g)
            for ref in (dcw_ref, dcb_ref, dwa_ref, dba_ref, dwx_ref, dbx_ref, dlam_ref):
                ref[...] = jnp.zeros_like(ref)

        keep = jnp.where(first_chunk, 0.0, 1.0)
        x = lx_ref[...]
        catx = jnp.concatenate([lxh_ref[...] * keep, x], axis=0)
        cw = cw_ref[...]
        xc = _conv_fwd(catx, cw, width) + cb_ref[...]
        wa = wa_ref[...]
        wx = wx_ref[...]
        lam_v = lam_ref[...]
        xcb, r, i, a, om, cl = _lru_gates(xc, wa, ba_ref[...], wx, bx_ref[...], lam_v)
        mult = jnp.sqrt(om)

        h = h_ref[...]
        hprev = _rows_before(jnp.concatenate([hh_ref[...] * keep, h], axis=0), 1)
        gl, dgl = _gelu_and_grad(gate_ref[...])
        dyp_v = dyp_ref[...]
        dp_ref[1] = (dyp_v * h * dgl).astype(BF16)

        al_s[...] = _rows_after(jnp.concatenate([a, nxt_a[...]], axis=0), 1)
        b_s[...] = dyp_v * gl
        nxt_g[0:1, :] = _scan_tiles(al_s, b_s, g_s, nxt_g[0:1, :], tc, reverse=True)
        nxt_a[...] = a[0:SUBLANE, :]
        du = g_s[...]

        da = du * hprev
        dmult = du * (i * xc)
        di = du * mult * xc
        dxc = du * mult * i
        dlog_a = da * a - dmult * (a * a / mult)
        dlam_ref[...] += jnp.sum(dlog_a * r, axis=0, keepdims=True) * (LRU_C * _sigmoid(-lam_v))
        dza = (dlog_a * cl) * r * (1.0 - r)
        dzx = di * i * (1.0 - i)
        dba_ref[...] += jnp.sum(dza, axis=0, keepdims=True)
        dbx_ref[...] += jnp.sum(dzx, axis=0, keepdims=True)
        dzab = dza.astype(BF16)
        dzxb = dzx.astype(BF16)
        dwa_ref[...] += lax.dot_general(xcb, dzab, tn_dims, preferred_element_type=F32)
        dwx_ref[...] += lax.dot_general(xcb, dzxb, tn_dims, preferred_element_type=F32)
        dxc = dxc + lax.dot_general(dzab, wa, nt_dims, preferred_element_type=F32)
        dxc = dxc + lax.dot_general(dzxb, wx, nt_dims, preferred_element_type=F32)
        dcb_ref[...] += jnp.sum(dxc, axis=0, keepdims=True)
        _conv_bwd_weight(dcw_ref, dxc, catx, width)
        catd = jnp.concatenate([dxc, nxt_dxc[...]], axis=0)
        dp_ref[0] = _conv_bwd_input(catd, cw, width).astype(BF16)
        nxt_dxc[...] = dxc[0:SUBLANE, :]

    rev = lambda s: nt - 1 - s
    blk = lambda off: pl.BlockSpec((tc, gc), lambda g, s, off=off: (rev(s), off + g))
    halo = lambda off: pl.BlockSpec(
        (SUBLANE, gc), lambda g, s, off=off: (jnp.maximum(rev(s) * halo_blocks - 1, 0), off + g))
    rowv = lambda rows: pl.BlockSpec((rows, gc), lambda g, s: (0, g))
    wspec = pl.BlockSpec((None, gc, gc), lambda g, s: (g, 0, 0))
    out_blk = pl.BlockSpec((tc, gc), lambda g, s: (rev(s), g))
    vec = lambda rows: jax.ShapeDtypeStruct((rows, d_lru), F32)
    wshape = jax.ShapeDtypeStruct((ng, gc, gc), F32)
    body, in_specs, operands = _dep_args(
        body,
        [blk(0), halo(0), blk(ng), blk(0), halo(0), blk(0),
         rowv(width), rowv(1), wspec, rowv(1), wspec, rowv(1), rowv(1)],
        [_hbm(p), _hbm(p), _hbm(p), _hbm(hseq), _hbm(hseq), _hbm(dyp),
         conv_w, conv_b.reshape(1, -1), wa_bd, ba.reshape(1, -1), wx_bd,
         bx.reshape(1, -1), lam.reshape(1, -1)], dp, dep)
    assert dp.shape[2] == d_lru and slab0 % 2 == 0
    return pl.pallas_call(
        body,
        out_shape=[jax.ShapeDtypeStruct(dp.shape, dp.dtype),
                   vec(width), vec(1), wshape, vec(1), wshape, vec(1), vec(1)],
        grid=(ng, nt),
        in_specs=in_specs,
        out_specs=[pl.BlockSpec((2, tc, gc), lambda g, s: (slab0 // 2, rev(s), g)),
                   rowv(width), rowv(1), wspec, rowv(1), wspec, rowv(1), rowv(1)],
        input_output_aliases={13: 0},
        scratch_shapes=[pltpu.VMEM((SUBLANE, gc), F32), pltpu.VMEM((SUBLANE, gc), F32),
                        pltpu.VMEM((SUBLANE, gc), F32),
                        pltpu.VMEM((tc, gc), F32), pltpu.VMEM((tc, gc), F32), pltpu.VMEM((tc, gc), F32)],
        compiler_params=_cparams(("parallel", "arbitrary"), 80 * tc * gc * 4),
        name="lru_bwd",
    )(*operands)


def _sc_fwd(p, conv_w, col0, d_sc, cb, tc):
    t = p.shape[0]
    nc = d_sc // cb
    nt = t // tc
    width = conv_w.shape[0]
    base = col0 // cb

    def body(b_ref, c_ref, v_ref, w_ref, y_ref, halo):
        @pl.when(pl.program_id(1) == 0)
        def _():
            halo[...] = jnp.zeros_like(halo)

        cv = c_ref[...] * v_ref[...]
        cat = jnp.concatenate([halo[...], cv], axis=0)
        halo[...] = cv[tc - SUBLANE:, :]
        y_ref[...] = (b_ref[...] * _conv_fwd(cat, w_ref[...], width)).astype(BF16)

    blk = lambda slab: pl.BlockSpec((tc, cb), lambda j, s, slab=slab: (s, base + slab * nc + j))
    return pl.pallas_call(
        body,
        out_shape=_out((t, d_sc), BF16),
        grid=(nc, nt),
        in_specs=[blk(0), blk(1), blk(2), pl.BlockSpec((width, cb), lambda j, s: (0, j))],
        out_specs=pl.BlockSpec((tc, cb), lambda j, s: (s, j)),
        scratch_shapes=[pltpu.VMEM((SUBLANE, cb), F32)],
        compiler_params=_cparams(("parallel", "arbitrary"), 20 * tc * cb * 4),
        name="sc_fwd",
    )(_hbm(p), _hbm(p), _hbm(p), conv_w)


def _sc_bwd(p, dyp, conv_w, col0, d_sc, cb, tc, dp, slab0):
    t = p.shape[0]
    nc = d_sc // cb
    nt = t // tc
    width = conv_w.shape[0]
    base = col0 // cb
    halo_blocks = tc // SUBLANE

    def body(b_ref, c_ref, ch_ref, v_ref, vh_ref, dyp_ref, w_ref,
             dp_ref, dw_ref, nxt_dq):
        s = pl.program_id(1)

        @pl.when(s == 0)
        def _():
            nxt_dq[...] = jnp.zeros_like(nxt_dq)
            dw_ref[...] = jnp.zeros_like(dw_ref)

        keep = jnp.where(s == nt - 1, 0.0, 1.0)
        cvals = c_ref[...]
        vvals = v_ref[...]
        w = w_ref[...]
        catcv = jnp.concatenate([ch_ref[...] * vh_ref[...] * keep, cvals * vvals], axis=0)
        q = _conv_fwd(catcv, w, width)
        dyp_v = dyp_ref[...]
        dp_ref[0] = (dyp_v * q).astype(BF16)
        dq = dyp_v * b_ref[...]
        _conv_bwd_weight(dw_ref, dq, catcv, width)
        dcv = _conv_bwd_input(jnp.concatenate([dq, nxt_dq[...]], axis=0), w, width)
        nxt_dq[...] = dq[0:SUBLANE, :]
        dp_ref[1] = (dcv * vvals).astype(BF16)
        dp_ref[2] = (dcv * cvals).astype(BF16)

    rev = lambda s: nt - 1 - s
    blk = lambda slab: pl.BlockSpec((tc, cb), lambda j, s, slab=slab: (rev(s), base + slab * nc + j))
    halo = lambda slab: pl.BlockSpec(
        (SUBLANE, cb),
        lambda j, s, slab=slab: (jnp.maximum(rev(s) * halo_blocks - 1, 0), base + slab * nc + j))
    out_blk = pl.BlockSpec((tc, cb), lambda j, s: (rev(s), j))
    wblk = pl.BlockSpec((width, cb), lambda j, s: (0, j))
    assert dp.shape[2] == d_sc and slab0 % 3 == 0
    operands = [p, p, p, p, p, dyp, conv_w]
    body, in_specs, operands = _dep_args(
        body, [blk(0), blk(1), halo(1), blk(2), halo(2), out_blk, wblk], operands, dp)
    return pl.pallas_call(
        body,
        out_shape=[jax.ShapeDtypeStruct(dp.shape, dp.dtype), jax.ShapeDtypeStruct((width, d_sc), F32)],
        grid=(nc, nt),
        in_specs=in_specs,
        out_specs=[pl.BlockSpec((3, tc, cb), lambda j, s: (slab0 // 3, rev(s), j)), wblk],
        input_output_aliases={7: 0},
        scratch_shapes=[pltpu.VMEM((SUBLANE, cb), F32)],
        compiler_params=_cparams(("parallel", "arbitrary"), 30 * tc * cb * 4),
        name="sc_bwd",
    )(*operands)


def _merge_fwd(p, y_lru, y_sc, col0, tc):
    t, d = y_lru.shape
    cb = _pick(math.gcd(d, col0), 1024)
    nc = d // cb
    base = col0 // cb

    def body(gl_ref, gs_ref, yl_ref, ys_ref, o_ref):
        o_ref[...] = (_sigmoid(gl_ref[...]) * yl_ref[...] + _sigmoid(gs_ref[...]) * ys_ref[...]).astype(BF16)

    gate = lambda slab: pl.BlockSpec((tc, cb), lambda s, j, slab=slab: (s, base + slab * nc + j))
    blk = pl.BlockSpec((tc, cb), lambda s, j: (s, j))
    return pl.pallas_call(
        body,
        out_shape=_out((t, d), BF16),
        grid=(t // tc, nc),
        in_specs=[gate(0), gate(1), blk, blk],
        out_specs=blk,
        compiler_params=_cparams(("parallel", "parallel"), 2 * tc * cb * 20),
        name="merge_fwd",
    )(_hbm(p), _hbm(p), _hbm(y_lru), _hbm(y_sc))


def _merge_bwd(p, y_lru, y_sc, dmerged, col0, tc, n_slabs):
    t, d = y_lru.shape
    cb = _pick(math.gcd(d, col0), 1024)
    nc = d // cb
    base = col0 // cb

    def body(gl_ref, gs_ref, yl_ref, ys_ref, dm_ref, dp_ref, dyl_ref, dys_ref):
        dm = dm_ref[...]
        sl = _sigmoid(gl_ref[...])
        ss = _sigmoid(gs_ref[...])
        dp_ref[0] = (dm * yl_ref[...] * (sl * (1.0 - sl))).astype(BF16)
        dp_ref[1] = (dm * ys_ref[...] * (ss * (1.0 - ss))).astype(BF16)
        dyl_ref[...] = (dm * sl).astype(BF16)
        dys_ref[...] = (dm * ss).astype(BF16)

    gate = lambda slab: pl.BlockSpec((tc, cb), lambda s, j, slab=slab: (s, base + slab * nc + j))
    blk = pl.BlockSpec((tc, cb), lambda s, j: (s, j))
    act = _out((t, d), BF16)
    return pl.pallas_call(
        body,
        out_shape=[jax.ShapeDtypeStruct((n_slabs, t, cb), BF16), act, act],
        grid=(t // tc, nc),
        in_specs=[gate(0), gate(1), blk, blk, blk],
        out_specs=[pl.BlockSpec((2, tc, cb), lambda s, j: (j, s, 0)), blk, blk],
        compiler_params=_cparams(("parallel", "parallel"), 2 * tc * cb * 28),
        name="merge_bwd",
    )(_hbm(p), _hbm(p), _hbm(y_lru), _hbm(y_sc), _hbm(dmerged))


def _ffn_act_fwd(up, conv_w, d_ff, cb, tc):
    t = up.shape[0]
    nc = d_ff // cb
    nt = t // tc
    width = conv_w.shape[0]

    def body(g_ref, v_ref, wg_ref, wv_ref, o_ref, halo_g, halo_v):
        @pl.when(pl.program_id(1) == 0)
        def _():
            halo_g[...] = jnp.zeros_like(halo_g)
            halo_v[...] = jnp.zeros_like(halo_v)

        g = g_ref[...]
        v = v_ref[...]
        ug = _conv_fwd(jnp.concatenate([halo_g[...], g], axis=0), wg_ref[...], width)
        uv = _conv_fwd(jnp.concatenate([halo_v[...], v], axis=0), wv_ref[...], width)
        halo_g[...] = g[tc - SUBLANE:, :]
        halo_v[...] = v[tc - SUBLANE:, :]
        o_ref[...] = (ug * _sigmoid(ug) * uv).astype(BF16)

    blk = lambda half: pl.BlockSpec((tc, cb), lambda j, s, half=half: (s, half * nc + j))
    wblk = lambda half: pl.BlockSpec((width, cb), lambda j, s, half=half: (0, half * nc + j))
    return pl.pallas_call(
        body,
        out_shape=_out((t, d_ff), BF16),
        grid=(nc, nt),
        in_specs=[blk(0), blk(1), wblk(0), wblk(1)],
        out_specs=pl.BlockSpec((tc, cb), lambda j, s: (s, j)),
        scratch_shapes=[pltpu.VMEM((SUBLANE, cb), F32), pltpu.VMEM((SUBLANE, cb), F32)],
        compiler_params=_cparams(("parallel", "arbitrary"), 24 * tc * cb * 4),
        name="ffn_act_fwd",
    )(_hbm(up), _hbm(up), conv_w, conv_w)


def _ffn_act_bwd(up, dact, conv_w, d_ff, cb, tc, dep=None):
    t = up.shape[0]
    nc = d_ff // cb
    nt = t // tc
    width = conv_w.shape[0]
    halo_blocks = tc // SUBLANE

    def body(g_ref, gh_ref, v_ref, vh_ref, da_ref, wg_ref, wv_ref,
             dup_ref, dwg_ref, dwv_ref, nxt_g, nxt_v):
        s = pl.program_id(1)

        @pl.when(s == 0)
        def _():
            nxt_g[...] = jnp.zeros_like(nxt_g)
            nxt_v[...] = jnp.zeros_like(nxt_v)
            dwg_ref[...] = jnp.zeros_like(dwg_ref)
            dwv_ref[...] = jnp.zeros_like(dwv_ref)

        keep = jnp.where(s == nt - 1, 0.0, 1.0)
        wg = wg_ref[...]
        wv = wv_ref[...]
        catg = jnp.concatenate([gh_ref[...] * keep, g_ref[...]], axis=0)
        catv = jnp.concatenate([vh_ref[...] * keep, v_ref[...]], axis=0)
        ug = _conv_fwd(catg, wg, width)
        uv = _conv_fwd(catv, wv, width)
        sg = _sigmoid(ug)
        da = da_ref[...]
        duv = da * (ug * sg)
        dup_ref[1] = _conv_bwd_input(jnp.concatenate([duv, nxt_v[...]], axis=0), wv, width).astype(BF16)
        nxt_v[...] = duv[0:SUBLANE, :]
        _conv_bwd_weight(dwv_ref, duv, catv, width)
        dug = da * uv * (sg * (1.0 + ug * (1.0 - sg)))
        dup_ref[0] = _conv_bwd_input(jnp.concatenate([dug, nxt_g[...]], axis=0), wg, width).astype(BF16)
        nxt_g[...] = dug[0:SUBLANE, :]
        _conv_bwd_weight(dwg_ref, dug, catg, width)

    rev = lambda s: nt - 1 - s
    blk = lambda half: pl.BlockSpec((tc, cb), lambda j, s, half=half: (rev(s), half * nc + j))
    halo = lambda half: pl.BlockSpec(
        (SUBLANE, cb), lambda j, s, half=half: (jnp.maximum(rev(s) * halo_blocks - 1, 0), half * nc + j))
    wblk = lambda half: pl.BlockSpec((width, cb), lambda j, s, half=half: (0, half * nc + j))
    out_blk = pl.BlockSpec((tc, cb), lambda j, s: (rev(s), j))
    wout = pl.BlockSpec((width, cb), lambda j, s: (0, j))
    act = _out((t, d_ff), BF16)
    wshape = jax.ShapeDtypeStruct((width, d_ff), F32)
    body, in_specs, operands = _dep_args(
        body, [blk(0), halo(0), blk(1), halo(1), out_blk, wblk(0), wblk(1)],
        [_hbm(up), _hbm(up), _hbm(up), _hbm(up), _hbm(dact), conv_w, conv_w], dep)
    return pl.pallas_call(
        body,
        out_shape=[jax.ShapeDtypeStruct((2, t, d_ff), BF16), wshape, wshape],
        grid=(nc, nt),
        in_specs=in_specs,
        out_specs=[pl.BlockSpec((2, tc, cb), lambda j, s: (0, rev(s), j)), wout, wout],
        scratch_shapes=[pltpu.VMEM((SUBLANE, cb), F32), pltpu.VMEM((SUBLANE, cb), F32)],
        compiler_params=_cparams(("parallel", "arbitrary"), 40 * tc * cb * 4),
        name="ffn_act_bwd",
    )(*operands)


def _mesh_pos():
    x, y, c = lax.axis_index("x"), lax.axis_index("y"), lax.axis_index("c")
    return x, y, c


def _other_chips(x, y):
    return [(1 - x, y), (x, 1 - y), (1 - x, 1 - y)]


def _cast_place(w, chip, col_sharded, name, dep=None):
    r, cdim = w.shape
    full = (r, cdim * N_CHIPS) if col_sharded else (r * N_CHIPS, cdim)
    rb = _pick(r, max(BF16_ROWS, (512 * 1024) // cdim), BF16_ROWS)
    nb = r // rb

    def body(chip_ref, w_ref, o_ref):
        o_ref[...] = w_ref[...].astype(BF16)

    if col_sharded:
        out_map = lambda i, chip_ref: (i, chip_ref[0])
    else:
        out_map = lambda i, chip_ref: (chip_ref[0] * nb + i, 0)
    grid_spec = pltpu.PrefetchScalarGridSpec(
        num_scalar_prefetch=1,
        grid=(nb,),
        in_specs=[pl.BlockSpec((rb, cdim), lambda i, chip_ref: (i, 0))] + ([ANY] if dep is not None else []),
        out_specs=pl.BlockSpec((rb, cdim), out_map),
    )
    body, _, operands = _dep_args(body, [], [chip, w], dep)
    return pl.pallas_call(
        body,
        out_shape=jax.ShapeDtypeStruct(full, BF16),
        grid_spec=grid_spec,
        compiler_params=_cparams(("parallel",), 2 * rb * cdim * 6),
        name=name,
    )(*operands)


def _remote(src, dst, send_sems, recv_sems, idx, to):
    return pltpu.make_async_remote_copy(
        src_ref=src, dst_ref=dst, send_sem=send_sems.at[idx], recv_sem=recv_sems.at[idx],
        device_id=to, device_id_type=MESH)


def _exchange(name, arrays, n_sems, plan):
    n = len(arrays)

    def body(*refs):
        bufs = refs[n:2 * n]
        send_sems, recv_sems = refs[2 * n:]
        sends, arrivals = plan(bufs, send_sems, recv_sems)
        for cp in sends:
            cp.start()
        for cp in arrivals:
            cp.wait_recv()
        for cp in sends:
            cp.wait_send()

    outs = pl.pallas_call(
        body,
        out_shape=[jax.ShapeDtypeStruct(a.shape, a.dtype) for a in arrays],
        in_specs=[ANY] * n,
        out_specs=[ANY] * n,
        input_output_aliases={k: k for k in range(n)},
        scratch_shapes=[pltpu.SemaphoreType.DMA((n_sems,)), pltpu.SemaphoreType.DMA((n_sems,))],
        name=name,
    )(*arrays)
    return list(outs)


def _exchange_start(name, arrays, n_sems, plan, after=None):
    n = len(arrays)
    n_in = n + (after is not None)

    def body(*refs):
        bufs = refs[:n]
        send_sems, recv_sems = refs[n_in], refs[n_in + 1]
        token = refs[-1]
        sends, _ = plan(bufs, send_sems, recv_sems)
        for cp in sends:
            cp.start()
        token[...] = jnp.zeros_like(token)

    out = pl.pallas_call(
        body,
        out_shape=(pltpu.SemaphoreType.DMA((n_sems,)), pltpu.SemaphoreType.DMA((n_sems,)),
                   *[pltpu.HBM(a.shape, a.dtype) for a in arrays],
                   jax.ShapeDtypeStruct((SUBLANE, LANE), F32)),
        in_specs=[HBM_SPEC] * n + [ANY] * (n_in - n),
        out_specs=(SEM_SPEC, SEM_SPEC, *[HBM_SPEC] * n, VMEM_SPEC),
        input_output_aliases={k: 2 + k for k in range(n)},
        compiler_params=pltpu.CompilerParams(has_side_effects=DATAFLOW_EFFECT),
        name=name,
    )(*[pltpu.with_memory_space_constraint(a, pltpu.HBM) for a in arrays], *([after] if after is not None else []))
    return out[0], out[1], list(out[2:2 + n]), out[-1]


def _exchange_wait(name, arrays, send_sems, recv_sems, after, plan):
    n = len(arrays)

    def body(*refs):
        bufs = refs[:n]
        sends, arrivals = plan(bufs, refs[n], refs[n + 1])
        for cp in arrivals:
            cp.wait_recv()
        for cp in sends:
            cp.wait_send()

    outs = pl.pallas_call(
        body,
        out_shape=[pltpu.HBM(a.shape, a.dtype) for a in arrays],
        in_specs=[HBM_SPEC] * n + [SEM_SPEC, SEM_SPEC, ANY],
        out_specs=[HBM_SPEC] * n,
        input_output_aliases={k: k for k in range(n)},
        compiler_params=pltpu.CompilerParams(has_side_effects=DATAFLOW_EFFECT),
        name=name,
    )(*arrays, send_sems, recv_sems, after)
    return list(outs)


def _half_block(ref, shard_shape, col_sharded, chip, half):
    r, cdim = shard_shape
    h = r // 2
    if col_sharded:
        return ref.at[pl.ds(pl.multiple_of(half * h, BF16_ROWS), h),
                      pl.ds(pl.multiple_of(chip * cdim, LANE), cdim)]
    return ref.at[pl.ds(pl.multiple_of(chip * r + half * h, BF16_ROWS), h), :]


def _gather_plan(shard_shapes, col_sharded, ks):
    def plan(bufs, send_sems, recv_sems):
        x, y, c = _mesh_pos()
        sends, arrivals = [], []
        for ref, k in zip(bufs, ks):
            mine = _half_block(ref, shard_shapes[k], col_sharded[k], 2 * x + y, c)
            for j, (px, py) in enumerate(_other_chips(x, y)):
                landed = _half_block(ref, shard_shapes[k], col_sharded[k], 2 * px + py, c)
                sends.append(_remote(mine, mine, send_sems, recv_sems, 3 * k + j, (px, py, c)))
                arrivals.append(_remote(landed, landed, send_sems, recv_sems, 3 * k + j, (px, py, c)))
        return sends, arrivals
    return plan


def _forward_plan(shard_shapes, col_sharded, ks):
    def plan(bufs, send_sems, recv_sems):
        x, y, c = _mesh_pos()
        sends, arrivals = [], []
        for i, (ref, k) in enumerate(zip(bufs, ks)):
            for j, (px, py) in enumerate(_other_chips(x, y)):
                landed = _half_block(ref, shard_shapes[k], col_sharded[k], 2 * px + py, c)
                theirs = _half_block(ref, shard_shapes[k], col_sharded[k], 2 * px + py, 1 - c)
                sends.append(_remote(landed, landed, send_sems, recv_sems, 3 * i + j, (x, y, 1 - c)))
                arrivals.append(_remote(theirs, theirs, send_sems, recv_sems, 3 * i + j, (x, y, 1 - c)))
        return sends, arrivals
    return plan


def _small_gather(small):
    def body(small_ref, out_ref, send_sems, recv_sems):
        x, y, c = _mesh_pos()
        me = 2 * x + y
        out_ref[me] = small_ref[...]
        copies = []
        for j, (px, py) in enumerate(_other_chips(x, y)):
            cp = _remote(small_ref, out_ref.at[me], send_sems, recv_sems, j, (px, py, c))
            cp.start()
            copies.append(cp)
        for j, (px, py) in enumerate(_other_chips(x, y)):
            _remote(small_ref, out_ref.at[2 * px + py], send_sems, recv_sems, j, (px, py, c)).wait_recv()
        for cp in copies:
            cp.wait_send()

    return pl.pallas_call(
        body,
        out_shape=jax.ShapeDtypeStruct((N_CHIPS,) + small.shape, small.dtype),
        in_specs=[VMEM_SPEC],
        out_specs=VMEM_SPEC,
        scratch_shapes=[pltpu.SemaphoreType.DMA((N_CHIPS - 1,)), pltpu.SemaphoreType.DMA((N_CHIPS - 1,))],
        name="gather_small",
    )(small)


def _as3d(g, col_sharded):
    r, cdim = g.shape
    return g.reshape(1, r, cdim) if col_sharded else g.reshape(N_CHIPS, r // N_CHIPS, cdim)


def _pair_plan(m):
    def plan(bufs, send_sems, recv_sems):
        x, y, c = _mesh_pos()
        copies = []
        for i in range(m):
            h = bufs[i].shape[1] // 2
            src = bufs[i].at[:, pl.ds(pl.multiple_of((1 - c) * h, SUBLANE), h), :]
            copies.append(_remote(src, bufs[m + i], send_sems, recv_sems, i, (x, y, 1 - c)))
        return copies, copies
    return plan


def _chip_plan(col_flags):
    m = len(col_flags)

    def plan(bufs, send_sems, recv_sems):
        x, y, c = _mesh_pos()
        copies = []
        for i in range(m):
            land = bufs[m + i]
            width = land.shape[2]
            for j, (px, py) in enumerate(_other_chips(x, y)):
                q = 2 * px + py
                if col_flags[i]:
                    src = bufs[i].at[0, :, pl.ds(pl.multiple_of(q * width, LANE), width)]
                else:
                    src = bufs[i].at[q]
                copies.append(_remote(src, land.at[j], send_sems, recv_sems, 3 * i + j, (px, py, c)))
        return copies, copies
    return plan


def _share_plan(m):
    def plan(bufs, send_sems, recv_sems):
        x, y, c = _mesh_pos()
        sends, arrivals = [], []
        for i in range(m):
            h = bufs[i].shape[0] // 2
            mine = bufs[i].at[pl.ds(pl.multiple_of(c * h, SUBLANE), h), :]
            theirs = bufs[i].at[pl.ds(pl.multiple_of((1 - c) * h, SUBLANE), h), :]
            sends.append(_remote(mine, mine, send_sems, recv_sems, i, (x, y, 1 - c)))
            arrivals.append(_remote(theirs, theirs, send_sems, recv_sems, i, (x, y, 1 - c)))
        return sends, arrivals
    return plan


def _pair_add(g3, other, core):
    a, r, cdim = g3.shape
    h = r // 2
    rb = _pick(h, max(BF16_ROWS, (512 * 1024) // cdim), BF16_ROWS)
    nb = h // rb

    def body(core_ref, g_ref, o_ref, out_ref):
        out_ref[...] = (g_ref[...] + o_ref[...]).astype(BF16)

    grid_spec = pltpu.PrefetchScalarGridSpec(
        num_scalar_prefetch=1,
        grid=(a, nb),
        in_specs=[pl.BlockSpec((None, rb, cdim), lambda i, j, core_ref: (i, core_ref[0] * nb + j, 0)),
                  pl.BlockSpec((None, rb, cdim), lambda i, j, core_ref: (i, j, 0))],
        out_specs=pl.BlockSpec((None, rb, cdim), lambda i, j, core_ref: (i, j, 0)),
    )
    return pl.pallas_call(
        body,
        out_shape=_out((a, h, cdim), BF16),
        grid_spec=grid_spec,
        compiler_params=_cparams(("parallel", "parallel"), 2 * rb * cdim * 10),
        name="grad_pair_add",
    )(core, _hbm(g3), _hbm(other))


def _small_allreduce(small):
    rows = small.shape[0]
    pad = (-rows) % (2 * SUBLANE)
    if pad:
        small = jnp.pad(small, ((0, pad), (0, 0)))
    h = small.shape[0] // 2
    half_shape = (h, small.shape[1])

    def body(small_ref, out_ref, theirs, by_chip, send_sems, recv_sems):
        x, y, c = _mesh_pos()
        me = 2 * x + y
        sibling = (x, y, 1 - c)
        mine = pl.ds(pl.multiple_of(c * h, SUBLANE), h)
        other = pl.ds(pl.multiple_of((1 - c) * h, SUBLANE), h)
        swap = _remote(small_ref, theirs, send_sems, recv_sems, 0, sibling)
        swap.start()
        swap.wait()
        by_chip[me] = small_ref[mine, :] + theirs[mine, :]
        copies = []
        for j, (px, py) in enumerate(_other_chips(x, y)):
            cp = _remote(by_chip.at[me], by_chip.at[me], send_sems, recv_sems, 1 + j, (px, py, c))
            cp.start()
            copies.append(cp)
        for j, (px, py) in enumerate(_other_chips(x, y)):
            landed = by_chip.at[2 * px + py]
            _remote(landed, landed, send_sems, recv_sems, 1 + j, (px, py, c)).wait_recv()
        total = by_chip[0]
        for q in range(1, N_CHIPS):
            total = total + by_chip[q]
        out_ref[mine, :] = total
        for cp in copies:
            cp.wait_send()
        share = _remote(out_ref.at[mine, :], out_ref.at[mine, :], send_sems, recv_sems, 4, sibling)
        share.start()
        _remote(out_ref.at[other, :], out_ref.at[other, :], send_sems, recv_sems, 4, sibling).wait_recv()
        share.wait_send()

    out = pl.pallas_call(
        body,
        out_shape=jax.ShapeDtypeStruct(small.shape, F32),
        in_specs=[VMEM_SPEC],
        out_specs=VMEM_SPEC,
        scratch_shapes=[pltpu.VMEM(small.shape, F32), pltpu.VMEM((N_CHIPS,) + half_shape, F32),
                        pltpu.SemaphoreType.DMA((5,)), pltpu.SemaphoreType.DMA((5,))],
        compiler_params=pltpu.CompilerParams(
            vmem_limit_bytes=min(VMEM_BUDGET, 8 * _nbytes(small.shape, F32) + (8 << 20))),
        name="grad_small_allreduce",
    )(small)
    return out[:rows]


def _chip_sum(partial, land, where, col_sharded):
    _, h, cdim = land.shape
    rb = _pick(h, max(BF16_ROWS, (512 * 1024) // cdim), BF16_ROWS)
    nb = h // rb

    def body(where_ref, own_ref, l_ref, o_ref):
        total = own_ref[...].astype(F32)
        for j in range(N_CHIPS - 1):
            total = total + l_ref[j].astype(F32)
        o_ref[...] = total

    if col_sharded:
        own_map = lambda i, w: (0, i, w[0])
    else:
        own_map = lambda i, w: (w[0], i, 0)
    grid_spec = pltpu.PrefetchScalarGridSpec(
        num_scalar_prefetch=1,
        grid=(nb,),
        in_specs=[pl.BlockSpec((None, rb, cdim), own_map),
                  pl.BlockSpec((N_CHIPS - 1, rb, cdim), lambda i, w: (0, i, 0))],
        out_specs=pl.BlockSpec((rb, cdim), lambda i, w: (w[1] * nb + i, 0)),
    )
    return pl.pallas_call(
        body,
        out_shape=_out((2 * h, cdim), F32),
        grid_spec=grid_spec,
        compiler_params=_cparams(("parallel",), 2 * rb * cdim * 12),
        name="grad_chip_sum",
    )(where, _hbm(partial), _hbm(land))


def _adamw(w, g, m, v, name, dep=None):
    r, cdim = w.shape
    rb = _pick(r, max(SUBLANE, (256 * 1024) // cdim), SUBLANE)
    c1 = 1.0 - ADAM_B1 ** ADAM_STEP
    c2 = 1.0 - ADAM_B2 ** ADAM_STEP

    def body(w_ref, g_ref, m_ref, v_ref, go_ref, d_ref, mo_ref, vo_ref):
        gv = g_ref[...]
        mn = ADAM_B1 * m_ref[...] + (1.0 - ADAM_B1) * gv
        vn = ADAM_B2 * v_ref[...] + (1.0 - ADAM_B2) * (gv * gv)
        m_hat = mn / c1
        v_hat = vn / c2
        d_ref[...] = -ADAM_LR * (m_hat / (jnp.sqrt(v_hat) + ADAM_EPS) + ADAM_WD * w_ref[...])
        go_ref[...] = gv
        mo_ref[...] = mn
        vo_ref[...] = vn

    blk = pl.BlockSpec((rb, cdim), lambda i: (i, 0))
    shape = jax.ShapeDtypeStruct((r, cdim), F32)
    body, in_specs, operands = _dep_args(body, [blk] * 4, [_hbm(w), _hbm(g), _hbm(m), _hbm(v)], dep)
    return pl.pallas_call(
        body,
        out_shape=[shape] * 4,
        grid=(r // rb,),
        in_specs=in_specs,
        out_specs=[blk] * 4,
        compiler_params=_cparams(("parallel",), 2 * rb * cdim * 4 * 8),
        name=name,
    )(*operands)


def _pack(arrays):
    tile = SUBLANE * LANE
    pieces = []
    for arr in arrays:
        flat = arr.reshape(-1)
        pad = (-flat.shape[0]) % tile
        if pad:
            flat = jnp.concatenate([flat, jnp.zeros((pad,), flat.dtype)])
        pieces.append(flat)
    return jnp.concatenate(pieces).reshape(-1, LANE)


def _unpack(packed, shapes):
    tile = SUBLANE * LANE
    flat = packed.reshape(-1)
    out, off = [], 0
    for shp in shapes:
        size = math.prod(shp)
        out.append(flat[off:off + size].reshape(shp))
        off += size + ((-size) % tile)
    return out


def _block_diag_groups(w, per_group):
    hcount, hd, _ = w.shape
    ng = hcount // per_group
    w4 = w.reshape(ng, per_group, hd, hd)
    eye = jnp.eye(per_group, dtype=w.dtype)
    bd = w4[:, :, :, None, :] * eye[None, :, None, :, None]
    return bd.reshape(ng, per_group * hd, per_group * hd).astype(BF16)


def _diag_blocks(wbd, per_group, hd):
    ng = wbd.shape[0]
    w5 = wbd.reshape(ng, per_group, hd, per_group, hd)
    blocks = [w5[:, i, :, i, :] for i in range(per_group)]
    return jnp.stack(blocks, axis=1).reshape(ng * per_group, hd, hd)


def kernel(x, g_mix, w_in, lru_conv_w, lru_conv_b, lru_wa, lru_ba, lru_wx, lru_bx, lru_lambda, lru_w_out, sc_conv_w, sc_w_out, w_o, g_ffn, ffn_w_up, ffn_conv_w, ffn_w_down, g_final, loss_target, m_g_mix, m_w_in, m_lru_conv_w, m_lru_conv_b, m_lru_wa, m_lru_ba, m_lru_wx, m_lru_bx, m_lru_lambda, m_lru_w_out, m_sc_conv_w, m_sc_w_out, m_w_o, m_g_ffn, m_ffn_w_up, m_ffn_conv_w, m_ffn_w_down, m_g_final, v_g_mix, v_w_in, v_lru_conv_w, v_lru_conv_b, v_lru_wa, v_lru_ba, v_lru_wx, v_lru_bx, v_lru_lambda, v_lru_w_out, v_sc_conv_w, v_sc_w_out, v_w_o, v_g_ffn, v_ffn_w_up, v_ffn_conv_w, v_ffn_w_down, v_g_final):
    seq, d_model = x.shape[1], x.shape[2]
    heads, head_dim, _ = lru_wa.shape
    d_lru = heads * head_dim
    d_sc = sc_w_out.shape[0]
    d_ff = ffn_w_down.shape[0] * N_CHIPS
    assert x.shape[0] == 1 and w_in.shape[1] * N_CHIPS == 2 * d_lru + 3 * d_sc + 2 * d_model
    xs = x.reshape(seq, d_model)
    target = loss_target.reshape(seq, d_model)

    chip = 2 * lax.axis_index("x") + lax.axis_index("y")
    core = lax.axis_index("c").astype(jnp.int32).reshape(1)

    big_w = [w_in, lru_w_out, sc_w_out, w_o, ffn_w_up, ffn_w_down]
    big_m = [m_w_in, m_lru_w_out, m_sc_w_out, m_w_o, m_ffn_w_up, m_ffn_w_down]
    big_v = [v_w_in, v_lru_w_out, v_sc_w_out, v_w_o, v_ffn_w_up, v_ffn_w_down]
    col_sharded = [True, True, True, False, True, False]
    conv_shards = [lru_conv_w, sc_conv_w, ffn_conv_w]
    conv_pack = jnp.concatenate(
        [jnp.pad(w, ((0, SUBLANE - w.shape[0]), (0, 0))) for w in conv_shards], axis=1)
    big_names = ["w_in", "lru_w_out", "sc_w_out", "w_o", "ffn_w_up", "ffn_w_down"]
    chip_arr = chip.astype(jnp.int32).reshape(1)
    placed = [_cast_place(big_w[0], chip_arr, col_sharded[0], "cast_" + big_names[0])]
    conv_all = _small_gather(conv_pack)
    shard_shapes = [w.shape for w in big_w]
    n_big = len(big_w)

    def gather_start(ks, after, tag):
        send, recv, bufs, token = _exchange_start(
            "gather_start_" + tag, [placed[k] for k in ks], 3 * n_big,
            _gather_plan(shard_shapes, col_sharded, ks), after=after)
        return (send, recv, dict(zip(ks, bufs))), token

    def arrived(state, ks, after, tag):
        send, recv, bufs = state
        got = _exchange_wait("gather_wait_" + tag, [bufs[k] for k in ks], send, recv, after,
                             _gather_plan(shard_shapes, col_sharded, ks))
        return _exchange("gather_forward_" + tag, got, 3 * len(ks), _forward_plan(shard_shapes, col_sharded, ks))

    conv_full, off = [], 0
    for w in conv_shards:
        kw, nq = w.shape
        piece = conv_all[:, :kw, off:off + nq]
        conv_full.append(piece.transpose(1, 0, 2).reshape(kw, N_CHIPS * nq))
        off += nq
    lcw, scw, fcw = conv_full

    per_group = max(1, min(heads, 256 // head_dim))
    gc = per_group * head_dim
    wa_bd = _block_diag_groups(lru_wa, per_group)
    wx_bd = _block_diag_groups(lru_wx, per_group)
    tc = _pick(seq, 256, SUBLANE)
    cb_sc = _pick(d_sc, 512)
    cb_ff = _pick(d_ff, 512)
    col_sc = 2 * d_lru
    col_gates = 2 * d_lru + 3 * d_sc

    first, token = gather_start([0], conv_all, "in")
    for k in range(1, n_big):
        placed.append(_cast_place(big_w[k], chip_arr, col_sharded[k], "cast_" + big_names[k], dep=token))
        token = placed[-1]
    h1 = _rms_fwd(xs, g_mix, "rms_mix", dep=token)
    (win_b,) = arrived(first, [0], h1, "in")
    rest, token = gather_start([1, 2, 3, 4, 5], win_b, "rest")
    p = _mm(h1, win_b, "nn", F32, name="mm_in", dep=token)
    wlo_b, wso_b, wo_b = arrived(rest, [1, 2, 3], p, "mix")
    y_lru_pre, hseq = _lru_fwd(p, lcw, lru_conv_b, wa_bd, lru_ba, wx_bd, lru_bx, lru_lambda, d_lru, gc, tc)
    y_sc_pre = _sc_fwd(p, scw, col_sc, d_sc, cb_sc, tc)
    y_lru = _mm(y_lru_pre, wlo_b, "nn", BF16, name="mm_lru_out")
    y_sc = _mm(y_sc_pre, wso_b, "nn", BF16, name="mm_sc_out")
    merged = _merge_fwd(p, y_lru, y_sc, col_gates, tc)
    x2 = _mm(merged, wo_b, "nn", F32, res=xs, name="mm_o")
    (wup_b,) = arrived(rest, [4], x2, "up")
    h2 = _rms_fwd(x2, g_ffn, "rms_ffn")
    up = _mm(h2, wup_b, "nn", F32, name="mm_up")
    (wdn_b,) = arrived(rest, [5], up, "down")
    act = _ffn_act_fwd(up, fcw, d_ff, cb_ff, tc)
    x3 = _mm(act, wdn_b, "nn", F32, res=x2, name="mm_down")
    loss_part, dx3, dx3b, dg_final = _loss_head(x3, g_final, target)

    where = jnp.concatenate([chip_arr, core])

    def reduce_start(grads, flags, tag):
        views = [_as3d(g, cs) for g, cs in zip(grads, flags)]
        lands = [lax.empty((v.shape[0], v.shape[1] // 2, v.shape[2]), F32) for v in views]
        send, recv, bufs, token = _exchange_start("grad_pair_start_" + tag, views + lands, len(views),
                                                  _pair_plan(len(views)))
        return (send, recv, bufs, flags, tag), token

    def reduce_mid(state, after):
        send, recv, bufs, flags, tag = state
        m = len(flags)
        bufs = _exchange_wait("grad_pair_wait_" + tag, bufs, send, recv, after, _pair_plan(m))
        partials = [_pair_add(bufs[i], bufs[m + i], core) for i in range(m)]
        lands = []
        for pz, cs in zip(partials, flags):
            _, h, cdim = pz.shape
            lands.append(lax.empty((N_CHIPS - 1, h, cdim // N_CHIPS if cs else cdim), BF16))
        send, recv, bufs, token = _exchange_start("grad_chip_start_" + tag, partials + lands, 3 * m,
                                                  _chip_plan(flags))
        return (send, recv, bufs, flags, tag), token

    def reduce_end(state, after):
        send, recv, bufs, flags, tag = state
        m = len(flags)
        bufs = _exchange_wait("grad_chip_wait_" + tag, bufs, send, recv, after, _chip_plan(flags))
        return [_chip_sum(bufs[i], bufs[m + i], where, flags[i]) for i in range(m)]

    g_wdn = _mm(act, dx3b, "tn", F32, name="mm_down_dw")
    red_down, token = reduce_start([g_wdn], [False], "down")
    dact = _mm(dx3b, wdn_b, "nt", F32, name="mm_down_dx", dep=token)
    red_down, token = reduce_mid(red_down, dact)
    dup, dfcw_g, dfcw_v = _ffn_act_bwd(up, dact, fcw, d_ff, cb_ff, tc, dep=token)
    g_wup = _mm(h2, dup, "tn", F32, name="mm_up_dw", slabs=[0, 1])
    red_up, token = reduce_start([g_wup], [True], "up")
    dh2 = _mm(dup, wup_b, "nt", F32, name="mm_up_dx", dep=token, slabs=[0, 1])
    red_up, token = reduce_mid(red_up, dh2)
    dx2, dx2b, dg_ffn = _rms_bwd(x2, g_ffn, dh2, dx3, "rms_ffn_bwd", True, dep=token)
    g_wo = _mm(merged, dx2b, "tn", F32, name="mm_o_dw")
    dmerged = _mm(dx2b, wo_b, "nt", BF16, name="mm_o_dx")
    slab_w = d_lru
    assert d_sc == slab_w and d_model % slab_w == 0 and col_gates % slab_w == 0
    n_gate = d_model // slab_w
    gate0 = col_gates // slab_w
    dp_slabs = [gate0 + kind * n_gate + j for j in range(n_gate) for kind in (0, 1)] + [0, 1, 2, 3, 4]
    dp, dyl, dys = _merge_bwd(p, y_lru, y_sc, dmerged, col_gates, tc, len(dp_slabs))
    assert dp.shape[2] == slab_w
    g_wlo = _mm(y_lru_pre, dyl, "tn", F32, name="mm_lru_out_dw")
    g_wso = _mm(y_sc_pre, dys, "tn", F32, name="mm_sc_out_dw")
    red_mix, token = reduce_start([g_wlo, g_wso, g_wo], [True, True, False], "mix")
    dylp = _mm(dyl, wlo_b, "nt", F32, name="mm_lru_out_dx", dep=token)
    dysp = _mm(dys, wso_b, "nt", F32, name="mm_sc_out_dx")
    red_mix, token = reduce_mid(red_mix, dysp)
    dp, dlcw, dlcb, dwa_bd, dba, dwx_bd, dbx, dlam = _lru_bwd(
        p, hseq, dylp, lcw, lru_conv_b, wa_bd, lru_ba, wx_bd, lru_bx, lru_lambda, d_lru, gc, tc,
        dp, 2 * n_gate, dep=token)
    dp, dscw = _sc_bwd(p, dysp, scw, col_sc, d_sc, cb_sc, tc, dp, 2 * n_gate + 2)
    g_win = _mm(h1, dp, "tn", F32, name="mm_in_dw", slabs=dp_slabs)
    red_in, token = reduce_start([g_win], [True], "in")
    dh1 = _mm(dp, win_b, "nt", F32, name="mm_in_dx", dep=token, slabs=dp_slabs)
    grad_x, dg_mix = _rms_bwd(xs, g_mix, dh1, dx2, "rms_mix_bwd", False)

    small_g = [dg_mix, dlcw, dlcb, _diag_blocks(dwa_bd, per_group, head_dim), dba,
               _diag_blocks(dwx_bd, per_group, head_dim), dbx, dlam, dscw, dg_ffn,
               jnp.concatenate([dfcw_g, dfcw_v], axis=1), dg_final]
    small_shapes = [a.shape for a in small_g]
    small_sum = _small_allreduce(_pack(small_g))
    red_in, token = reduce_mid(red_in, small_sum)
    (h_wdn,) = reduce_end(red_down, token)
    (h_wup,) = reduce_end(red_up, token)
    h_wlo, h_wso, h_wo = reduce_end(red_mix, token)
    s_wlo, s_wso, s_wo, s_wup, s_wdn = _exchange("grad_share_a", [h_wlo, h_wso, h_wo, h_wup, h_wdn], 5,
                                                 _share_plan(5))
    early = {1: s_wlo, 2: s_wso, 3: s_wo, 4: s_wup, 5: s_wdn}
    big_out = [None] * n_big
    last = None
    for k, g in early.items():
        big_out[k] = _adamw(big_w[k], g, big_m[k], big_v[k], "adamw_" + big_names[k], dep=last)
        last = big_out[k][1]
    (h_win,) = reduce_end(red_in, last)
    (s_win,) = _exchange("grad_share_b", [h_win], 1, _share_plan(1))
    big_out[0] = _adamw(big_w[0], s_win, big_m[0], big_v[0], "adamw_" + big_names[0])
    sg = _unpack(small_sum, small_shapes)
    for idx in (1, 8, 10):
        nq = sg[idx].shape[1] // N_CHIPS
        sg[idx] = lax.dynamic_slice_in_dim(sg[idx], chip * nq, nq, axis=1)
    small_w = [g_mix, lru_conv_w, lru_conv_b, lru_wa, lru_ba, lru_wx, lru_bx, lru_lambda, sc_conv_w,
               g_ffn, ffn_conv_w, g_final]
    small_m = [m_g_mix, m_lru_conv_w, m_lru_conv_b, m_lru_wa, m_lru_ba, m_lru_wx, m_lru_bx, m_lru_lambda,
               m_sc_conv_w, m_g_ffn, m_ffn_conv_w, m_g_final]
    small_v = [v_g_mix, v_lru_conv_w, v_lru_conv_b, v_lru_wa, v_lru_ba, v_lru_wx, v_lru_bx, v_lru_lambda,
               v_sc_conv_w, v_g_ffn, v_ffn_conv_w, v_g_final]
    sg = [g.reshape(w.shape) for g, w in zip(sg, small_w)]
    w_shapes = [w.shape for w in small_w]
    packed = _adamw(_pack(small_w), _pack(sg), _pack(small_m), _pack(small_v), "adamw_small")
    small_out = [_unpack(pk, w_shapes) for pk in packed]

    order = [(0, 0), (1, 0), (0, 1), (0, 2), (0, 3), (0, 4), (0, 5), (0, 6), (0, 7), (1, 1), (0, 8), (1, 2),
             (1, 3), (0, 9), (1, 4), (0, 10), (1, 5), (0, 11)]
    by_kind = []
    for kind in range(4):
        by_kind.append([big_out[i][kind] if is_big else small_out[kind][i] for is_big, i in order])
    loss = lax.psum(loss_part[0, 0], ("x", "y", "c"))
    return (loss, grad_x.reshape(x.shape), *by_kind[0], *by_kind[1], *by_kind[2], *by_kind[3])
```

```python
import functools
import math

import jax
import jax.numpy as jnp
from jax import lax
from jax.experimental import pallas as pl
from jax.experimental.pallas import tpu as pltpu

F32 = jnp.float32
BF16 = jnp.bfloat16

LANE = 128
SUBLANE = 8
BF16_ROWS = 16
VMEM_BYTES_V7X = 64 * 1024 * 1024
VMEM_BUDGET = VMEM_BYTES_V7X - 8 * 1024 * 1024
MM_VMEM_BUDGET = 42 * 1024 * 1024
EPS = 1e-6
LRU_C = 8.0
ADAM_LR = 0.001
ADAM_B1 = 0.9
ADAM_B2 = 0.999
ADAM_EPS = 1e-08
ADAM_WD = 0.01
ADAM_STEP = 10

N_CHIPS = 4
N_DEV = 8
MESH = pl.DeviceIdType.MESH
ANY = pl.BlockSpec(memory_space=pl.ANY)
VMEM_SPEC = pl.BlockSpec(memory_space=pltpu.VMEM)
HBM_SPEC = pl.BlockSpec(memory_space=pltpu.HBM)
SEM_SPEC = pl.BlockSpec(memory_space=pltpu.SEMAPHORE)
DATAFLOW_EFFECT = pltpu.SideEffectType.DATAFLOW_SIDE_EFFECTING


def _pick(n, cap, mult=LANE):
    best = None
    d = mult
    while d <= min(n, cap):
        if n % d == 0:
            best = d
        d += mult
    return n if best is None else best


def _cparams(semantics, block_bytes):
    limit = min(VMEM_BUDGET, max(32 * 1024 * 1024, int(block_bytes * 1.25) + (4 << 20)))
    return pltpu.CompilerParams(dimension_semantics=semantics, vmem_limit_bytes=limit)


def _nbytes(shape, dtype):
    return math.prod(shape) * jnp.dtype(dtype).itemsize


def _sigmoid(z):
    return 1.0 / (1.0 + jnp.exp(-z))


def _softplus(z):
    e = jnp.exp(-jnp.abs(z))
    u = 1.0 + e
    log1p = jnp.where(u == 1.0, e, jnp.log(u) * (e / (u - 1.0)))
    return jnp.maximum(z, 0.0) + log1p


def _neg_expm1(z):
    small = z * (1.0 + z * (0.5 + z * (1.0 / 6.0 + z * (1.0 / 24.0))))
    return -jnp.where(jnp.abs(z) < 0.03, small, jnp.exp(z) - 1.0)


_GELU_K = math.sqrt(2.0 / math.pi)
_GELU_C = 0.044715


def _gelu_and_grad(z):
    z2 = z * z
    th = jnp.tanh(_GELU_K * (z + _GELU_C * z2 * z))
    val = 0.5 * z * (1.0 + th)
    grad = 0.5 * (1.0 + th) + 0.5 * z * (1.0 - th * th) * (_GELU_K * (1.0 + 3.0 * _GELU_C * z2))
    return val, grad


def _rows_before(cat, k):
    if k == 0:
        return cat[SUBLANE:, :]
    return pltpu.roll(cat, k, 0)[SUBLANE:, :]


def _rows_after(cat, k):
    n = cat.shape[0]
    if k == 0:
        return cat[:n - SUBLANE, :]
    return pltpu.roll(cat, n - k, 0)[:n - SUBLANE, :]


def _conv_fwd(cat, w, width):
    y = _rows_before(cat, width - 1) * w[0:1, :]
    for k in range(1, width):
        y = y + _rows_before(cat, width - 1 - k) * w[k:k + 1, :]
    return y


def _conv_bwd_input(cat, w, width):
    dx = _rows_after(cat, width - 1) * w[0:1, :]
    for k in range(1, width):
        dx = dx + _rows_after(cat, width - 1 - k) * w[k:k + 1, :]
    return dx


def _conv_bwd_weight(dw_ref, dy, catx, width):
    for k in range(width):
        dw_ref[k:k + 1, :] += jnp.sum(dy * _rows_before(catx, width - 1 - k), axis=0, keepdims=True)


def _scan_tiles(a_ref, b_ref, out_ref, carry0, n_rows, reverse):
    cols = a_ref.shape[1]
    row = lax.broadcasted_iota(jnp.int32, (SUBLANE, cols), 0)
    n_tiles = n_rows // SUBLANE

    def step(j, carry):
        tile = (n_tiles - 1 - j) if reverse else j
        off = pl.multiple_of(tile * SUBLANE, SUBLANE)
        a = a_ref[pl.ds(off, SUBLANE), :]
        b = b_ref[pl.ds(off, SUBLANE), :]
        for s in (1, 2, 4):
            if reverse:
                keep = row < SUBLANE - s
                shift = SUBLANE - s
            else:
                keep = row >= s
                shift = s
            a_sh = jnp.where(keep, pltpu.roll(a, shift, 0), 1.0)
            b_sh = jnp.where(keep, pltpu.roll(b, shift, 0), 0.0)
            b = a * b_sh + b
            a = a * a_sh
        out = a * carry + b
        out_ref[pl.ds(off, SUBLANE), :] = out
        return out[0:1, :] if reverse else out[SUBLANE - 1:SUBLANE, :]

    return lax.fori_loop(0, n_tiles, step, carry0)


def _out(shape, dtype):
    return jax.ShapeDtypeStruct(shape, dtype)


def _hbm(x):
    return x


def _dep_args(body, in_specs, operands, *deps):
    deps = [d for d in deps if d is not None]
    if not deps:
        return body, in_specs, operands
    n = len(operands)

    def wrapped(*refs):
        return body(*refs[:n], *refs[n + len(deps):])

    return wrapped, list(in_specs) + [ANY] * len(deps), list(operands) + deps


def _mm(a, b, mode, out_dtype, res=None, name=None, dep=None, slabs=None):
    assert a.dtype == BF16 and b.dtype == BF16
    a_slabbed, b_slabbed = a.ndim == 3, b.ndim == 3
    assert not a_slabbed or (mode == "nt" and slabs is not None)
    assert not b_slabbed or (mode == "tn" and slabs is not None)
    if mode == "nn":
        (m, k), (k2, n) = a.shape, b.shape
        dims = (((1,), (0,)), ((), ()))
    elif mode == "nt":
        m, k = (a.shape[1], a.shape[0] * a.shape[2]) if a_slabbed else a.shape
        n, k2 = b.shape
        dims = (((1,), (1,)), ((), ()))
    else:
        k, m = a.shape
        k2, n = (b.shape[1], b.shape[0] * b.shape[2]) if b_slabbed else b.shape
        dims = (((0,), (0,)), ((), ()))
    assert k == k2
    n_unit = b.shape[2] if b_slabbed else n
    out_bytes = jnp.dtype(out_dtype).itemsize
    bm = _pick(m, 1024)
    bn = _pick(n_unit, 1024)
    bk = k

    a_buffers = 1 if mode == "nt" else 2

    def est(bm_, bn_, bk_):
        e = (a_buffers * bm_ * bk_ + 2 * bk_ * bn_) * 2 + 2 * bm_ * bn_ * out_bytes
        if k // bk_ > 1:
            e += bm_ * bn_ * 4
        if res is not None:
            e += 2 * bm_ * bn_ * 4
        return e

    order = ((True, 256), (False, 512), (False, 256)) if mode == "nt" else (
        (True, 512), (False, 512), (True, 256), (False, 256))
    for shrink_n, floor in order:
        while est(bm, bn, bk) > MM_VMEM_BUDGET:
            if shrink_n and bn > floor and bn % 2 == 0 and n_unit % (bn // 2) == 0:
                bn //= 2
            elif not shrink_n and bm > floor and bm % 2 == 0 and m % (bm // 2) == 0:
                bm //= 2
            else:
                break
    while (est(bm, bn, bk) > MM_VMEM_BUDGET and not a_slabbed and bk % (2 * LANE) == 0
           and k % (bk // 2) == 0):
        bk //= 2
    nk = k // bk
    per_slab = n_unit // bn

    def out_col(j):
        if not b_slabbed:
            return j
        s = j // per_slab
        where = sum(jnp.where(s == t, slabs[t], 0) for t in range(len(slabs)))
        return where * per_slab + j % per_slab

    a_mode = pl.Buffered(1) if a_buffers == 1 and nk == 1 else None
    if mode == "tn":
        a_spec = pl.BlockSpec((bk, bm), lambda i, j, kk: (kk, i))
    elif a_slabbed:
        a_spec = pl.BlockSpec((a.shape[0], bm, a.shape[2]), lambda i, j, kk: (0, i, 0), pipeline_mode=a_mode)
    else:
        a_spec = pl.BlockSpec((bm, bk), lambda i, j, kk: (i, kk), pipeline_mode=a_mode)
    if mode == "nt":
        b_spec = pl.BlockSpec((bn, bk), lambda i, j, kk: (j, kk))
    elif b_slabbed:
        b_spec = pl.BlockSpec((None, bk, bn), lambda i, j, kk: (j // per_slab, kk, j % per_slab))
    else:
        b_spec = pl.BlockSpec((bk, bn), lambda i, j, kk: (kk, j))
    o_spec = pl.BlockSpec((bm, bn), lambda i, j, kk: (i, out_col(j)))
    in_specs = [a_spec, b_spec]
    operands = [a, b]
    if res is not None:
        in_specs.append(o_spec)
        operands.append(res)
    has_res = res is not None

    def body(*refs):
        a_ref, b_ref = refs[0], refs[1]
        res_ref = refs[2] if has_res else None
        o_ref = refs[2 + has_res]
        if a_slabbed:
            width = a_ref.shape[2]
            part = None
            for s, col in enumerate(slabs):
                term = lax.dot_general(a_ref[s], b_ref[:, col * width:(col + 1) * width], dims,
                                       preferred_element_type=F32)
                part = term if part is None else part + term
        else:
            part = lax.dot_general(a_ref[...], b_ref[...], dims, preferred_element_type=F32)
        if nk == 1:
            if has_res:
                part = part + res_ref[...]
            o_ref[...] = part.astype(o_ref.dtype)
            return
        acc_ref = refs[-1]
        kk = pl.program_id(2)

        @pl.when(kk == 0)
        def _():
            acc_ref[...] = part

        @pl.when(kk > 0)
        def _():
            acc_ref[...] += part

        @pl.when(kk == nk - 1)
        def _():
            total = acc_ref[...]
            if has_res:
                total = total + res_ref[...]
            o_ref[...] = total.astype(o_ref.dtype)

    scratch = [pltpu.VMEM((bm, bn), F32)] if nk > 1 else []
    body, in_specs, operands = _dep_args(body, in_specs, operands, dep)
    return pl.pallas_call(
        body,
        out_shape=jax.ShapeDtypeStruct((m, n), out_dtype),
        grid=(m // bm, n // bn, nk),
        in_specs=in_specs,
        out_specs=o_spec,
        scratch_shapes=scratch,
        compiler_params=_cparams(("parallel", "parallel", "arbitrary"), est(bm, bn, bk)),
        name=name,
    )(*operands)


def _rms_fwd(x, g, name, dep=None):
    t, d = x.shape
    tb = _pick(t, 512, SUBLANE)

    def body(x_ref, g_ref, h_ref):
        xv = x_ref[...]
        r = lax.rsqrt(jnp.mean(xv * xv, axis=-1, keepdims=True) + EPS)
        h_ref[...] = ((xv * r) * g_ref[...]).astype(BF16)

    blk = pl.BlockSpec((tb, d), lambda i: (i, 0))
    body, in_specs, operands = _dep_args(
        body, [blk, pl.BlockSpec((1, d), lambda i: (0, 0))], [_hbm(x), g.reshape(1, d)], dep)
    return pl.pallas_call(
        body,
        out_shape=_out((t, d), BF16),
        grid=(t // tb,),
        in_specs=in_specs,
        out_specs=blk,
        compiler_params=_cparams(("parallel",), 2 * tb * d * 6),
        name=name,
    )(*operands)


def _rms_bwd(x, g, dh, dres, name, want_bf16, dep=None):
    t, d = x.shape
    tb = _pick(t, 256, SUBLANE)

    def body(x_ref, g_ref, dh_ref, dres_ref, *outs):
        dx_ref, dg_ref = outs[0], outs[-1]
        xv = x_ref[...]
        r = lax.rsqrt(jnp.mean(xv * xv, axis=-1, keepdims=True) + EPS)
        xhat = xv * r
        dhv = dh_ref[...]
        dxhat = dhv * g_ref[...]
        dx = dres_ref[...] + r * (dxhat - xhat * jnp.mean(dxhat * xhat, axis=-1, keepdims=True))
        dx_ref[...] = dx
        if want_bf16:
            outs[1][...] = dx.astype(BF16)

        @pl.when(pl.program_id(0) == 0)
        def _():
            dg_ref[...] = jnp.zeros_like(dg_ref)

        dg_ref[...] += jnp.sum(dhv * xhat, axis=0, keepdims=True)

    blk = pl.BlockSpec((tb, d), lambda i: (i, 0))
    row = pl.BlockSpec((1, d), lambda i: (0, 0))
    out_shape = [_out((t, d), F32)]
    out_specs = [blk]
    if want_bf16:
        out_shape.append(_out((t, d), BF16))
        out_specs.append(blk)
    out_shape.append(jax.ShapeDtypeStruct((1, d), F32))
    out_specs.append(row)
    body, in_specs, operands = _dep_args(
        body, [blk, row, blk, blk], [_hbm(x), g.reshape(1, d), _hbm(dh), _hbm(dres)], dep)
    return pl.pallas_call(
        body,
        out_shape=out_shape,
        grid=(t // tb,),
        in_specs=in_specs,
        out_specs=out_specs,
        compiler_params=_cparams(("arbitrary",), 2 * tb * d * 18),
        name=name,
    )(*operands)


def _loss_head(x3, g, target):
    t, d = x3.shape
    tb = _pick(t, 256, SUBLANE)

    def body(x_ref, g_ref, t_ref, loss_ref, dx_ref, dxb_ref, dg_ref):
        xv = x_ref[...]
        gv = g_ref[...]
        r = lax.rsqrt(jnp.mean(xv * xv, axis=-1, keepdims=True) + EPS)
        xhat = xv * r
        err = xhat * gv - t_ref[...]
        dy = err * (1.0 / d)
        dxhat = dy * gv
        dx = r * (dxhat - xhat * jnp.mean(dxhat * xhat, axis=-1, keepdims=True))
        dx_ref[...] = dx
        dxb_ref[...] = dx.astype(BF16)

        @pl.when(pl.program_id(0) == 0)
        def _():
            dg_ref[...] = jnp.zeros_like(dg_ref)
            loss_ref[...] = jnp.zeros_like(loss_ref)

        dg_ref[...] += jnp.sum(dy * xhat, axis=0, keepdims=True)
        per_token = jnp.mean(err * err, axis=-1, keepdims=True)
        loss_ref[...] += 0.5 * jnp.sum(per_token, axis=0, keepdims=True)

    blk = pl.BlockSpec((tb, d), lambda i: (i, 0))
    row = pl.BlockSpec((1, d), lambda i: (0, 0))
    return pl.pallas_call(
        body,
        out_shape=[jax.ShapeDtypeStruct((1, 1), F32), _out((t, d), F32),
                   _out((t, d), BF16), jax.ShapeDtypeStruct((1, d), F32)],
        grid=(t // tb,),
        in_specs=[blk, row, blk],
        out_specs=[pl.BlockSpec((1, 1), lambda i: (0, 0)), blk, blk, row],
        compiler_params=_cparams(("arbitrary",), 2 * tb * d * 14),
        name="loss_head",
    )(_hbm(x3), g.reshape(1, d), _hbm(target))


def _lru_gates(xc, wa, ba, wx, bx, lam):
    nn = (((1,), (0,)), ((), ()))
    xcb = xc.astype(BF16)
    r = _sigmoid(lax.dot_general(xcb, wa, nn, preferred_element_type=F32) + ba)
    i = _sigmoid(lax.dot_general(xcb, wx, nn, preferred_element_type=F32) + bx)
    cl = -LRU_C * _softplus(-lam)
    log_a = cl * r
    a = jnp.exp(log_a)
    one_minus_a2 = _neg_expm1(2.0 * log_a)
    return xcb, r, i, a, one_minus_a2, cl


def _lru_fwd(p, conv_w, conv_b, wa_bd, ba, wx_bd, bx, lam, d_lru, gc, tc):
    t = p.shape[0]
    ng = d_lru // gc
    nt = t // tc
    width = conv_w.shape[0]

    def body(lx_ref, gate_ref, cw_ref, cb_ref, wa_ref, ba_ref, wx_ref, bx_ref, lam_ref,
             y_ref, h_ref, halo, hcar, a_s, u_s):
        @pl.when(pl.program_id(1) == 0)
        def _():
            halo[...] = jnp.zeros_like(halo)
            hcar[...] = jnp.zeros_like(hcar)

        x = lx_ref[...]
        cat = jnp.concatenate([halo[...], x], axis=0)
        halo[...] = x[tc - SUBLANE:, :]
        xc = _conv_fwd(cat, cw_ref[...], width) + cb_ref[...]
        _, r, i, a, om, _ = _lru_gates(xc, wa_ref[...], ba_ref[...], wx_ref[...], bx_ref[...], lam_ref[...])
        a_s[...] = a
        u_s[...] = jnp.sqrt(om) * (i * xc)
        hcar[0:1, :] = _scan_tiles(a_s, u_s, h_ref, hcar[0:1, :], tc, reverse=False)
        gl, _ = _gelu_and_grad(gate_ref[...])
        y_ref[...] = (gl * h_ref[...]).astype(BF16)

    blk = lambda off: pl.BlockSpec((tc, gc), lambda g, s, off=off: (s, off + g))
    rowv = lambda rows: pl.BlockSpec((rows, gc), lambda g, s: (0, g))
    wspec = pl.BlockSpec((None, gc, gc), lambda g, s: (g, 0, 0))
    out_blk = pl.BlockSpec((tc, gc), lambda g, s: (s, g))
    return pl.pallas_call(
        body,
        out_shape=[_out((t, d_lru), BF16), _out((t, d_lru), F32)],
        grid=(ng, nt),
        in_specs=[blk(0), blk(ng), rowv(width), rowv(1), wspec, rowv(1), wspec, rowv(1), rowv(1)],
        out_specs=[out_blk, out_blk],
        scratch_shapes=[pltpu.VMEM((SUBLANE, gc), F32), pltpu.VMEM((SUBLANE, gc), F32),
                        pltpu.VMEM((tc, gc), F32), pltpu.VMEM((tc, gc), F32)],
        compiler_params=_cparams(("parallel", "arbitrary"), 40 * tc * gc * 4),
        name="lru_fwd",
    )(_hbm(p), _hbm(p), conv_w, conv_b.reshape(1, -1), wa_bd, ba.reshape(1, -1), wx_bd, bx.reshape(1, -1),
      lam.reshape(1, -1))


def _lru_bwd(p, hseq, dyp, conv_w, conv_b, wa_bd, ba, wx_bd, bx, lam, d_lru, gc, tc, dp, slab0, dep=None):
    t = p.shape[0]
    ng = d_lru // gc
    nt = t // tc
    width = conv_w.shape[0]
    halo_blocks = tc // SUBLANE
    nn = (((1,), (0,)), ((), ()))
    nt_dims = (((1,), (1,)), ((), ()))
    tn_dims = (((0,), (0,)), ((), ()))

    def body(lx_ref, lxh_ref, gate_ref, h_ref, hh_ref, dyp_ref,
             cw_ref, cb_ref, wa_ref, ba_ref, wx_ref, bx_ref, lam_ref,
             dp_ref, dcw_ref, dcb_ref, dwa_ref, dba_ref, dwx_ref, dbx_ref, dlam_ref,
             nxt_dxc, nxt_a, nxt_g, al_s, b_s, g_s):
        s = pl.program_id(1)
        first_chunk = s == nt - 1

        @pl.when(s == 0)
        def _():
            nxt_dxc[...] = jnp.zeros_like(nxt_dxc)
            nxt_a[...] = jnp.zeros_like(nxt_a)
            nxt_g[...] = jnp.zeros_like(nxt_g)
            for ref in (dcw_ref, dcb_ref, dwa_ref, dba_ref, dwx_ref, dbx_ref, dlam_ref):
                ref[...] = jnp.zeros_like(ref)

        keep = jnp.where(first_chunk, 0.0, 1.0)
        x = lx_ref[...]
        catx = jnp.concatenate([lxh_ref[...] * keep, x], axis=0)
        cw = cw_ref[...]
        xc = _conv_fwd(catx, cw, width) + cb_ref[...]
        wa = wa_ref[...]
        wx = wx_ref[...]
        lam_v = lam_ref[...]
        xcb, r, i, a, om, cl = _lru_gates(xc, wa, ba_ref[...], wx, bx_ref[...], lam_v)
        mult = jnp.sqrt(om)

        h = h_ref[...]
        hprev = _rows_before(jnp.concatenate([hh_ref[...] * keep, h], axis=0), 1)
        gl, dgl = _gelu_and_grad(gate_ref[...])
        dyp_v = dyp_ref[...]
        dp_ref[1] = (dyp_v * h * dgl).astype(BF16)

        al_s[...] = _rows_after(jnp.concatenate([a, nxt_a[...]], axis=0), 1)
        b_s[...] = dyp_v * gl
        nxt_g[0:1, :] = _scan_tiles(al_s, b_s, g_s, nxt_g[0:1, :], tc, reverse=True)
        nxt_a[...] = a[0:SUBLANE, :]
        du = g_s[...]

        da = du * hprev
        dmult = du * (i * xc)
        di = du * mult * xc
        dxc = du * mult * i
        dlog_a = da * a - dmult * (a * a / mult)
        dlam_ref[...] += jnp.sum(dlog_a * r, axis=0, keepdims=True) * (LRU_C * _sigmoid(-lam_v))
        dza = (dlog_a * cl) * r * (1.0 - r)
        dzx = di * i * (1.0 - i)
        dba_ref[...] += jnp.sum(dza, axis=0, keepdims=True)
        dbx_ref[...] += jnp.sum(dzx, axis=0, keepdims=True)
        dzab = dza.astype(BF16)
        dzxb = dzx.astype(BF16)
        dwa_ref[...] += lax.dot_general(xcb, dzab, tn_dims, preferred_element_type=F32)
        dwx_ref[...] += lax.dot_general(xcb, dzxb, tn_dims, preferred_element_type=F32)
        dxc = dxc + lax.dot_general(dzab, wa, nt_dims, preferred_element_type=F32)
        dxc = dxc + lax.dot_general(dzxb, wx, nt_dims, preferred_element_type=F32)
        dcb_ref[...] += jnp.sum(dxc, axis=0, keepdims=True)
        _conv_bwd_weight(dcw_ref, dxc, catx, width)
        catd = jnp.concatenate([dxc, nxt_dxc[...]], axis=0)
        dp_ref[0] = _conv_bwd_input(catd, cw, width).astype(BF16)
        nxt_dxc[...] = dxc[0:SUBLANE, :]

    rev = lambda s: nt - 1 - s
    blk = lambda off: pl.BlockSpec((tc, gc), lambda g, s, off=off: (rev(s), off + g))
    halo = lambda off: pl.BlockSpec(
        (SUBLANE, gc), lambda g, s, off=off: (jnp.maximum(rev(s) * halo_blocks - 1, 0), off + g))
    rowv = lambda rows: pl.BlockSpec((rows, gc), lambda g, s: (0, g))
    wspec = pl.BlockSpec((None, gc, gc), lambda g, s: (g, 0, 0))
    out_blk = pl.BlockSpec((tc, gc), lambda g, s: (rev(s), g))
    vec = lambda rows: jax.ShapeDtypeStruct((rows, d_lru), F32)
    wshape = jax.ShapeDtypeStruct((ng, gc, gc), F32)
    body, in_specs, operands = _dep_args(
        body,
        [blk(0), halo(0), blk(ng), blk(0), halo(0), blk(0),
         rowv(width), rowv(1), wspec, rowv(1), wspec, rowv(1), rowv(1)],
        [_hbm(p), _hbm(p), _hbm(p), _hbm(hseq), _hbm(hseq), _hbm(dyp),
         conv_w, conv_b.reshape(1, -1), wa_bd, ba.reshape(1, -1), wx_bd,
         bx.reshape(1, -1), lam.reshape(1, -1)], dp, dep)
    assert dp.shape[2] == d_lru and slab0 % 2 == 0
    return pl.pallas_call(
        body,
        out_shape=[jax.ShapeDtypeStruct(dp.shape, dp.dtype),
                   vec(width), vec(1), wshape, vec(1), wshape, vec(1), vec(1)],
        grid=(ng, nt),
        in_specs=in_specs,
        out_specs=[pl.BlockSpec((2, tc, gc), lambda g, s: (slab0 // 2, rev(s), g)),
                   rowv(width), rowv(1), wspec, rowv(1), wspec, rowv(1), rowv(1)],
        input_output_aliases={13: 0},
        scratch_shapes=[pltpu.VMEM((SUBLANE, gc), F32), pltpu.VMEM((SUBLANE, gc), F32),
                        pltpu.VMEM((SUBLANE, gc), F32),
                        pltpu.VMEM((tc, gc), F32), pltpu.VMEM((tc, gc), F32), pltpu.VMEM((tc, gc), F32)],
        compiler_params=_cparams(("parallel", "arbitrary"), 80 * tc * gc * 4),
        name="lru_bwd",
    )(*operands)


def _sc_fwd(p, conv_w, col0, d_sc, cb, tc):
    t = p.shape[0]
    nc = d_sc // cb
    nt = t // tc
    width = conv_w.shape[0]
    base = col0 // cb

    def body(b_ref, c_ref, v_ref, w_ref, y_ref, halo):
        @pl.when(pl.program_id(1) == 0)
        def _():
            halo[...] = jnp.zeros_like(halo)

        cv = c_ref[...] * v_ref[...]
        cat = jnp.concatenate([halo[...], cv], axis=0)
        halo[...] = cv[tc - SUBLANE:, :]
        y_ref[...] = (b_ref[...] * _conv_fwd(cat, w_ref[...], width)).astype(BF16)

    blk = lambda slab: pl.BlockSpec((tc, cb), lambda j, s, slab=slab: (s, base + slab * nc + j))
    return pl.pallas_call(
        body,
        out_shape=_out((t, d_sc), BF16),
        grid=(nc, nt),
        in_specs=[blk(0), blk(1), blk(2), pl.BlockSpec((width, cb), lambda j, s: (0, j))],
        out_specs=pl.BlockSpec((tc, cb), lambda j, s: (s, j)),
        scratch_shapes=[pltpu.VMEM((SUBLANE, cb), F32)],
        compiler_params=_cparams(("parallel", "arbitrary"), 20 * tc * cb * 4),
        name="sc_fwd",
    )(_hbm(p), _hbm(p), _hbm(p), conv_w)


def _sc_bwd(p, dyp, conv_w, col0, d_sc, cb, tc, dp, slab0):
    t = p.shape[0]
    nc = d_sc // cb
    nt = t // tc
    width = conv_w.shape[0]
    base = col0 // cb
    halo_blocks = tc // SUBLANE

    def body(b_ref, c_ref, ch_ref, v_ref, vh_ref, dyp_ref, w_ref,
             dp_ref, dw_ref, nxt_dq):
        s = pl.program_id(1)

        @pl.when(s == 0)
        def _():
            nxt_dq[...] = jnp.zeros_like(nxt_dq)
            dw_ref[...] = jnp.zeros_like(dw_ref)

        keep = jnp.where(s == nt - 1, 0.0, 1.0)
        cvals = c_ref[...]
        vvals = v_ref[...]
        w = w_ref[...]
        catcv = jnp.concatenate([ch_ref[...] * vh_ref[...] * keep, cvals * vvals], axis=0)
        q = _conv_fwd(catcv, w, width)
        dyp_v = dyp_ref[...]
        dp_ref[0] = (dyp_v * q).astype(BF16)
        dq = dyp_v * b_ref[...]
        _conv_bwd_weight(dw_ref, dq, catcv, width)
        dcv = _conv_bwd_input(jnp.concatenate([dq, nxt_dq[...]], axis=0), w, width)
        nxt_dq[...] = dq[0:SUBLANE, :]
        dp_ref[1] = (dcv * vvals).astype(BF16)
        dp_ref[2] = (dcv * cvals).astype(BF16)

    rev = lambda s: nt - 1 - s
    blk = lambda slab: pl.BlockSpec((tc, cb), lambda j, s, slab=slab: (rev(s), base + slab * nc + j))
    halo = lambda slab: pl.BlockSpec(
        (SUBLANE, cb),
        lambda j, s, slab=slab: (jnp.maximum(rev(s) * halo_blocks - 1, 0), base + slab * nc + j))
    out_blk = pl.BlockSpec((tc, cb), lambda j, s: (rev(s), j))
    wblk = pl.BlockSpec((width, cb), lambda j, s: (0, j))
    assert dp.shape[2] == d_sc and slab0 % 3 == 0
    operands = [p, p, p, p, p, dyp, conv_w]
    body, in_specs, operands = _dep_args(
        body, [blk(0), blk(1), halo(1), blk(2), halo(2), out_blk, wblk], operands, dp)
    return pl.pallas_call(
        body,
        out_shape=[jax.ShapeDtypeStruct(dp.shape, dp.dtype), jax.ShapeDtypeStruct((width, d_sc), F32)],
        grid=(nc, nt),
        in_specs=in_specs,
        out_specs=[pl.BlockSpec((3, tc, cb), lambda j, s: (slab0 // 3, rev(s), j)), wblk],
        input_output_aliases={7: 0},
        scratch_shapes=[pltpu.VMEM((SUBLANE, cb), F32)],
        compiler_params=_cparams(("parallel", "arbitrary"), 30 * tc * cb * 4),
        name="sc_bwd",
    )(*operands)


def _merge_fwd(p, y_lru, y_sc, col0, tc):
    t, d = y_lru.shape
    cb = _pick(math.gcd(d, col0), 1024)
    nc = d // cb
    base = col0 // cb

    def body(gl_ref, gs_ref, yl_ref, ys_ref, o_ref):
        o_ref[...] = (_sigmoid(gl_ref[...]) * yl_ref[...] + _sigmoid(gs_ref[...]) * ys_ref[...]).astype(BF16)

    gate = lambda slab: pl.BlockSpec((tc, cb), lambda s, j, slab=slab: (s, base + slab * nc + j))
    blk = pl.BlockSpec((tc, cb), lambda s, j: (s, j))
    return pl.pallas_call(
        body,
        out_shape=_out((t, d), BF16),
        grid=(t // tc, nc),
        in_specs=[gate(0), gate(1), blk, blk],
        out_specs=blk,
        compiler_params=_cparams(("parallel", "parallel"), 2 * tc * cb * 20),
        name="merge_fwd",
    )(_hbm(p), _hbm(p), _hbm(y_lru), _hbm(y_sc))


def _merge_bwd(p, y_lru, y_sc, dmerged, col0, tc, n_slabs):
    t, d = y_lru.shape
    cb = _pick(math.gcd(d, col0), 1024)
    nc = d // cb
    base = col0 // cb

    def body(gl_ref, gs_ref, yl_ref, ys_ref, dm_ref, dp_ref, dyl_ref, dys_ref):
        dm = dm_ref[...]
        sl = _sigmoid(gl_ref[...])
        ss = _sigmoid(gs_ref[...])
        dp_ref[0] = (dm * yl_ref[...] * (sl * (1.0 - sl))).astype(BF16)
        dp_ref[1] = (dm * ys_ref[...] * (ss * (1.0 - ss))).astype(BF16)
        dyl_ref[...] = (dm * sl).astype(BF16)
        dys_ref[...] = (dm * ss).astype(BF16)

    gate = lambda slab: pl.BlockSpec((tc, cb), lambda s, j, slab=slab: (s, base + slab * nc + j))
    blk = pl.BlockSpec((tc, cb), lambda s, j: (s, j))
    act = _out((t, d), BF16)
    return pl.pallas_call(
        body,
        out_shape=[jax.ShapeDtypeStruct((n_slabs, t, cb), BF16), act, act],
        grid=(t // tc, nc),
        in_specs=[gate(0), gate(1), blk, blk, blk],
        out_specs=[pl.BlockSpec((2, tc, cb), lambda s, j: (j, s, 0)), blk, blk],
        compiler_params=_cparams(("parallel", "parallel"), 2 * tc * cb * 28),
        name="merge_bwd",
    )(_hbm(p), _hbm(p), _hbm(y_lru), _hbm(y_sc), _hbm(dmerged))


def _ffn_act_fwd(up, conv_w, d_ff, cb, tc):
    t = up.shape[0]
    nc = d_ff // cb
    nt = t // tc
    width = conv_w.shape[0]

    def body(g_ref, v_ref, wg_ref, wv_ref, o_ref, halo_g, halo_v):
        @pl.when(pl.program_id(1) == 0)
        def _():
            halo_g[...] = jnp.zeros_like(halo_g)
            halo_v[...] = jnp.zeros_like(halo_v)

        g = g_ref[...]
        v = v_ref[...]
        ug = _conv_fwd(jnp.concatenate([halo_g[...], g], axis=0), wg_ref[...], width)
        uv = _conv_fwd(jnp.concatenate([halo_v[...], v], axis=0), wv_ref[...], width)
        halo_g[...] = g[tc - SUBLANE:, :]
        halo_v[...] = v[tc - SUBLANE:, :]
        o_ref[...] = (ug * _sigmoid(ug) * uv).astype(BF16)

    blk = lambda half: pl.BlockSpec((tc, cb), lambda j, s, half=half: (s, half * nc + j))
    wblk = lambda half: pl.BlockSpec((width, cb), lambda j, s, half=half: (0, half * nc + j))
    return pl.pallas_call(
        body,
        out_shape=_out((t, d_ff), BF16),
        grid=(nc, nt),
        in_specs=[blk(0), blk(1), wblk(0), wblk(1)],
        out_specs=pl.BlockSpec((tc, cb), lambda j, s: (s, j)),
        scratch_shapes=[pltpu.VMEM((SUBLANE, cb), F32), pltpu.VMEM((SUBLANE, cb), F32)],
        compiler_params=_cparams(("parallel", "arbitrary"), 24 * tc * cb * 4),
        name="ffn_act_fwd",
    )(_hbm(up), _hbm(up), conv_w, conv_w)


def _ffn_act_bwd(up, dact, conv_w, d_ff, cb, tc, dep=None):
    t = up.shape[0]
    nc = d_ff // cb
    nt = t // tc
    width = conv_w.shape[0]
    halo_blocks = tc // SUBLANE

    def body(g_ref, gh_ref, v_ref, vh_ref, da_ref, wg_ref, wv_ref,
             dup_ref, dwg_ref, dwv_ref, nxt_g, nxt_v):
        s = pl.program_id(1)

        @pl.when(s == 0)
        def _():
            nxt_g[...] = jnp.zeros_like(nxt_g)
            nxt_v[...] = jnp.zeros_like(nxt_v)
            dwg_ref[...] = jnp.zeros_like(dwg_ref)
            dwv_ref[...] = jnp.zeros_like(dwv_ref)

        keep = jnp.where(s == nt - 1, 0.0, 1.0)
        wg = wg_ref[...]
        wv = wv_ref[...]
        catg = jnp.concatenate([gh_ref[...] * keep, g_ref[...]], axis=0)
        catv = jnp.concatenate([vh_ref[...] * keep, v_ref[...]], axis=0)
        ug = _conv_fwd(catg, wg, width)
        uv = _conv_fwd(catv, wv, width)
        sg = _sigmoid(ug)
        da = da_ref[...]
        duv = da * (ug * sg)
        dup_ref[1] = _conv_bwd_input(jnp.concatenate([duv, nxt_v[...]], axis=0), wv, width).astype(BF16)
        nxt_v[...] = duv[0:SUBLANE, :]
        _conv_bwd_weight(dwv_ref, duv, catv, width)
        dug = da * uv * (sg * (1.0 + ug * (1.0 - sg)))
        dup_ref[0] = _conv_bwd_input(jnp.concatenate([dug, nxt_g[...]], axis=0), wg, width).astype(BF16)
        nxt_g[...] = dug[0:SUBLANE, :]
        _conv_bwd_weight(dwg_ref, dug, catg, width)

    rev = lambda s: nt - 1 - s
    blk = lambda half: pl.BlockSpec((tc, cb), lambda j, s, half=half: (rev(s), half * nc + j))
    halo = lambda half: pl.BlockSpec(
        (SUBLANE, cb), lambda j, s, half=half: (jnp.maximum(rev(s) * halo_blocks - 1, 0), half * nc + j))
    wblk = lambda half: pl.BlockSpec((width, cb), lambda j, s, half=half: (0, half * nc + j))
    out_blk = pl.BlockSpec((tc, cb), lambda j, s: (rev(s), j))
    wout = pl.BlockSpec((width, cb), lambda j, s: (0, j))
    act = _out((t, d_ff), BF16)
    wshape = jax.ShapeDtypeStruct((width, d_ff), F32)
    body, in_specs, operands = _dep_args(
        body, [blk(0), halo(0), blk(1), halo(1), out_blk, wblk(0), wblk(1)],
        [_hbm(up), _hbm(up), _hbm(up), _hbm(up), _hbm(dact), conv_w, conv_w], dep)
    return pl.pallas_call(
        body,
        out_shape=[jax.ShapeDtypeStruct((2, t, d_ff), BF16), wshape, wshape],
        grid=(nc, nt),
        in_specs=in_specs,
        out_specs=[pl.BlockSpec((2, tc, cb), lambda j, s: (0, rev(s), j)), wout, wout],
        scratch_shapes=[pltpu.VMEM((SUBLANE, cb), F32), pltpu.VMEM((SUBLANE, cb), F32)],
        compiler_params=_cparams(("parallel", "arbitrary"), 40 * tc * cb * 4),
        name="ffn_act_bwd",
    )(*operands)


def _mesh_pos():
    x, y, c = lax.axis_index("x"), lax.axis_index("y"), lax.axis_index("c")
    return x, y, c


def _other_chips(x, y):
    return [(1 - x, y), (x, 1 - y), (1 - x, 1 - y)]


def _cast_place(w, chip, col_sharded, name, dep=None):
    r, cdim = w.shape
    full = (r, cdim * N_CHIPS) if col_sharded else (r * N_CHIPS, cdim)
    rb = _pick(r, max(BF16_ROWS, (512 * 1024) // cdim), BF16_ROWS)
    nb = r // rb

    def body(chip_ref, w_ref, o_ref):
        o_ref[...] = w_ref[...].astype(BF16)

    if col_sharded:
        out_map = lambda i, chip_ref: (i, chip_ref[0])
    else:
        out_map = lambda i, chip_ref: (chip_ref[0] * nb + i, 0)
    grid_spec = pltpu.PrefetchScalarGridSpec(
        num_scalar_prefetch=1,
        grid=(nb,),
        in_specs=[pl.BlockSpec((rb, cdim), lambda i, chip_ref: (i, 0))] + ([ANY] if dep is not None else []),
        out_specs=pl.BlockSpec((rb, cdim), out_map),
    )
    body, _, operands = _dep_args(body, [], [chip, w], dep)
    return pl.pallas_call(
        body,
        out_shape=jax.ShapeDtypeStruct(full, BF16),
        grid_spec=grid_spec,
        compiler_params=_cparams(("parallel",), 2 * rb * cdim * 6),
        name=name,
    )(*operands)


def _remote(src, dst, send_sems, recv_sems, idx, to):
    return pltpu.make_async_remote_copy(
        src_ref=src, dst_ref=dst, send_sem=send_sems.at[idx], recv_sem=recv_sems.at[idx],
        device_id=to, device_id_type=MESH)


def _exchange(name, arrays, n_sems, plan):
    n = len(arrays)

    def body(*refs):
        bufs = refs[n:2 * n]
        send_sems, recv_sems = refs[2 * n:]
        sends, arrivals = plan(bufs, send_sems, recv_sems)
        for cp in sends:
            cp.start()
        for cp in arrivals:
            cp.wait_recv()
        for cp in sends:
            cp.wait_send()

    outs = pl.pallas_call(
        body,
        out_shape=[jax.ShapeDtypeStruct(a.shape, a.dtype) for a in arrays],
        in_specs=[ANY] * n,
        out_specs=[ANY] * n,
        input_output_aliases={k: k for k in range(n)},
        scratch_shapes=[pltpu.SemaphoreType.DMA((n_sems,)), pltpu.SemaphoreType.DMA((n_sems,))],
        name=name,
    )(*arrays)
    return list(outs)


def _exchange_start(name, arrays, n_sems, plan, after=None):
    n = len(arrays)
    n_in = n + (after is not None)

    def body(*refs):
        bufs = refs[:n]
        send_sems, recv_sems = refs[n_in], refs[n_in + 1]
        token = refs[-1]
        sends, _ = plan(bufs, send_sems, recv_sems)
        for cp in sends:
            cp.start()
        token[...] = jnp.zeros_like(token)

    out = pl.pallas_call(
        body,
        out_shape=(pltpu.SemaphoreType.DMA((n_sems,)), pltpu.SemaphoreType.DMA((n_sems,)),
                   *[pltpu.HBM(a.shape, a.dtype) for a in arrays],
                   jax.ShapeDtypeStruct((SUBLANE, LANE), F32)),
        in_specs=[HBM_SPEC] * n + [ANY] * (n_in - n),
        out_specs=(SEM_SPEC, SEM_SPEC, *[HBM_SPEC] * n, VMEM_SPEC),
        input_output_aliases={k: 2 + k for k in range(n)},
        compiler_params=pltpu.CompilerParams(has_side_effects=DATAFLOW_EFFECT),
        name=name,
    )(*[pltpu.with_memory_space_constraint(a, pltpu.HBM) for a in arrays], *([after] if after is not None else []))
    return out[0], out[1], list(out[2:2 + n]), out[-1]


def _exchange_wait(name, arrays, send_sems, recv_sems, after, plan):
    n = len(arrays)

    def body(*refs):
        bufs = refs[:n]
        sends, arrivals = plan(bufs, refs[n], refs[n + 1])
        for cp in arrivals:
            cp.wait_recv()
        for cp in sends:
            cp.wait_send()

    outs = pl.pallas_call(
        body,
        out_shape=[pltpu.HBM(a.shape, a.dtype) for a in arrays],
        in_specs=[HBM_SPEC] * n + [SEM_SPEC, SEM_SPEC, ANY],
        out_specs=[HBM_SPEC] * n,
        input_output_aliases={k: k for k in range(n)},
        compiler_params=pltpu.CompilerParams(has_side_effects=DATAFLOW_EFFECT),
        name=name,
    )(*arrays, send_sems, recv_sems, after)
    return list(outs)


def _half_block(ref, shard_shape, col_sharded, chip, half):
    r, cdim = shard_shape
    h = r // 2
    if col_sharded:
        return ref.at[pl.ds(pl.multiple_of(half * h, BF16_ROWS), h),
                      pl.ds(pl.multiple_of(chip * cdim, LANE), cdim)]
    return ref.at[pl.ds(pl.multiple_of(chip * r + half * h, BF16_ROWS), h), :]


def _gather_plan(shard_shapes, col_sharded, ks):
    def plan(bufs, send_sems, recv_sems):
        x, y, c = _mesh_pos()
        sends, arrivals = [], []
        for ref, k in zip(bufs, ks):
            mine = _half_block(ref, shard_shapes[k], col_sharded[k], 2 * x + y, c)
            for j, (px, py) in enumerate(_other_chips(x, y)):
                landed = _half_block(ref, shard_shapes[k], col_sharded[k], 2 * px + py, c)
                sends.append(_remote(mine, mine, send_sems, recv_sems, 3 * k + j, (px, py, c)))
                arrivals.append(_remote(landed, landed, send_sems, recv_sems, 3 * k + j, (px, py, c)))
        return sends, arrivals
    return plan


def _forward_plan(shard_shapes, col_sharded, ks):
    def plan(bufs, send_sems, recv_sems):
        x, y, c = _mesh_pos()
        sends, arrivals = [], []
        for i, (ref, k) in enumerate(zip(bufs, ks)):
            for j, (px, py) in enumerate(_other_chips(x, y)):
                landed = _half_block(ref, shard_shapes[k], col_sharded[k], 2 * px + py, c)
                theirs = _half_block(ref, shard_shapes[k], col_sharded[k], 2 * px + py, 1 - c)
                sends.append(_remote(landed, landed, send_sems, recv_sems, 3 * i + j, (x, y, 1 - c)))
                arrivals.append(_remote(theirs, theirs, send_sems, recv_sems, 3 * i + j, (x, y, 1 - c)))
        return sends, arrivals
    return plan


def _small_gather(small):
    def body(small_ref, out_ref, send_sems, recv_sems):
        x, y, c = _mesh_pos()
        me = 2 * x + y
        out_ref[me] = small_ref[...]
        copies = []
        for j, (px, py) in enumerate(_other_chips(x, y)):
            cp = _remote(small_ref, out_ref.at[me], send_sems, recv_sems, j, (px, py, c))
            cp.start()
            copies.append(cp)
        for j, (px, py) in enumerate(_other_chips(x, y)):
            _remote(small_ref, out_ref.at[2 * px + py], send_sems, recv_sems, j, (px, py, c)).wait_recv()
        for cp in copies:
            cp.wait_send()

    return pl.pallas_call(
        body,
        out_shape=jax.ShapeDtypeStruct((N_CHIPS,) + small.shape, small.dtype),
        in_specs=[VMEM_SPEC],
        out_specs=VMEM_SPEC,
        scratch_shapes=[pltpu.SemaphoreType.DMA((N_CHIPS - 1,)), pltpu.SemaphoreType.DMA((N_CHIPS - 1,))],
        name="gather_small",
    )(small)


def _as3d(g, col_sharded):
    r, cdim = g.shape
    return g.reshape(1, r, cdim) if col_sharded else g.reshape(N_CHIPS, r // N_CHIPS, cdim)


def _pair_plan(m):
    def plan(bufs, send_sems, recv_sems):
        x, y, c = _mesh_pos()
        copies = []
        for i in range(m):
            h = bufs[i].shape[1] // 2
            src = bufs[i].at[:, pl.ds(pl.multiple_of((1 - c) * h, SUBLANE), h), :]
            copies.append(_remote(src, bufs[m + i], send_sems, recv_sems, i, (x, y, 1 - c)))
        return copies, copies
    return plan


def _chip_plan(col_flags):
    m = len(col_flags)

    def plan(bufs, send_sems, recv_sems):
        x, y, c = _mesh_pos()
        copies = []
        for i in range(m):
            land = bufs[m + i]
            width = land.shape[2]
            for j, (px, py) in enumerate(_other_chips(x, y)):
                q = 2 * px + py
                if col_flags[i]:
                    src = bufs[i].at[0, :, pl.ds(pl.multiple_of(q * width, LANE), width)]
                else:
                    src = bufs[i].at[q]
                copies.append(_remote(src, land.at[j], send_sems, recv_sems, 3 * i + j, (px, py, c)))
        return copies, copies
    return plan


def _share_plan(m):
    def plan(bufs, send_sems, recv_sems):
        x, y, c = _mesh_pos()
        sends, arrivals = [], []
        for i in range(m):
            h = bufs[i].shape[0] // 2
            mine = bufs[i].at[pl.ds(pl.multiple_of(c * h, SUBLANE), h), :]
            theirs = bufs[i].at[pl.ds(pl.multiple_of((1 - c) * h, SUBLANE), h), :]
            sends.append(_remote(mine, mine, send_sems, recv_sems, i, (x, y, 1 - c)))
            arrivals.append(_remote(theirs, theirs, send_sems, recv_sems, i, (x, y, 1 - c)))
        return sends, arrivals
    return plan


def _pair_add(g3, other, core):
    a, r, cdim = g3.shape
    h = r // 2
    rb = _pick(h, max(BF16_ROWS, (512 * 1024) // cdim), BF16_ROWS)
    nb = h // rb

    def body(core_ref, g_ref, o_ref, out_ref):
        out_ref[...] = (g_ref[...] + o_ref[...]).astype(BF16)

    grid_spec = pltpu.PrefetchScalarGridSpec(
        num_scalar_prefetch=1,
        grid=(a, nb),
        in_specs=[pl.BlockSpec((None, rb, cdim), lambda i, j, core_ref: (i, core_ref[0] * nb + j, 0)),
                  pl.BlockSpec((None, rb, cdim), lambda i, j, core_ref: (i, j, 0))],
        out_specs=pl.BlockSpec((None, rb, cdim), lambda i, j, core_ref: (i, j, 0)),
    )
    return pl.pallas_call(
        body,
        out_shape=_out((a, h, cdim), BF16),
        grid_spec=grid_spec,
        compiler_params=_cparams(("parallel", "parallel"), 2 * rb * cdim * 10),
        name="grad_pair_add",
    )(core, _hbm(g3), _hbm(other))


def _small_allreduce(small):
    rows = small.shape[0]
    pad = (-rows) % (2 * SUBLANE)
    if pad:
        small = jnp.pad(small, ((0, pad), (0, 0)))
    h = small.shape[0] // 2
    half_shape = (h, small.shape[1])

    def body(small_ref, out_ref, theirs, by_chip, send_sems, recv_sems):
        x, y, c = _mesh_pos()
        me = 2 * x + y
        sibling = (x, y, 1 - c)
        mine = pl.ds(pl.multiple_of(c * h, SUBLANE), h)
        other = pl.ds(pl.multiple_of((1 - c) * h, SUBLANE), h)
        swap = _remote(small_ref, theirs, send_sems, recv_sems, 0, sibling)
        swap.start()
        swap.wait()
        by_chip[me] = small_ref[mine, :] + theirs[mine, :]
        copies = []
        for j, (px, py) in enumerate(_other_chips(x, y)):
            cp = _remote(by_chip.at[me], by_chip.at[me], send_sems, recv_sems, 1 + j, (px, py, c))
            cp.start()
            copies.append(cp)
        for j, (px, py) in enumerate(_other_chips(x, y)):
            landed = by_chip.at[2 * px + py]
            _remote(landed, landed, send_sems, recv_sems, 1 + j, (px, py, c)).wait_recv()
        total = by_chip[0]
        for q in range(1, N_CHIPS):
            total = total + by_chip[q]
        out_ref[mine, :] = total
        for cp in copies:
            cp.wait_send()
        share = _remote(out_ref.at[mine, :], out_ref.at[mine, :], send_sems, recv_sems, 4, sibling)
        share.start()
        _remote(out_ref.at[other, :], out_ref.at[other, :], send_sems, recv_sems, 4, sibling).wait_recv()
        share.wait_send()

    out = pl.pallas_call(
        body,
        out_shape=jax.ShapeDtypeStruct(small.shape, F32),
        in_specs=[VMEM_SPEC],
        out_specs=VMEM_SPEC,
        scratch_shapes=[pltpu.VMEM(small.shape, F32), pltpu.VMEM((N_CHIPS,) + half_shape, F32),
                        pltpu.SemaphoreType.DMA((5,)), pltpu.SemaphoreType.DMA((5,))],
        compiler_params=pltpu.CompilerParams(
            vmem_limit_bytes=min(VMEM_BUDGET, 8 * _nbytes(small.shape, F32) + (8 << 20))),
        name="grad_small_allreduce",
    )(small)
    return out[:rows]


def _chip_sum(partial, land, where, col_sharded):
    _, h, cdim = land.shape
    rb = _pick(h, max(BF16_ROWS, (512 * 1024) // cdim), BF16_ROWS)
    nb = h // rb

    def body(where_ref, own_ref, l_ref, o_ref):
        total = own_ref[...].astype(F32)
        for j in range(N_CHIPS - 1):
            total = total + l_ref[j].astype(F32)
        o_ref[...] = total

    if col_sharded:
        own_map = lambda i, w: (0, i, w[0])
    else:
        own_map = lambda i, w: (w[0], i, 0)
    grid_spec = pltpu.PrefetchScalarGridSpec(
        num_scalar_prefetch=1,
        grid=(nb,),
        in_specs=[pl.BlockSpec((None, rb, cdim), own_map),
                  pl.BlockSpec((N_CHIPS - 1, rb, cdim), lambda i, w: (0, i, 0))],
        out_specs=pl.BlockSpec((rb, cdim), lambda i, w: (w[1] * nb + i, 0)),
    )
    return pl.pallas_call(
        body,
        out_shape=_out((2 * h, cdim), F32),
        grid_spec=grid_spec,
        compiler_params=_cparams(("parallel",), 2 * rb * cdim * 12),
        name="grad_chip_sum",
    )(where, _hbm(partial), _hbm(land))


def _adamw(w, g, m, v, name, dep=None):
    r, cdim = w.shape
    rb = _pick(r, max(SUBLANE, (256 * 1024) // cdim), SUBLANE)
    inv_c1 = 1.0 / (1.0 - ADAM_B1 ** ADAM_STEP)
    inv_c2 = 1.0 / (1.0 - ADAM_B2 ** ADAM_STEP)

    def body(w_ref, g_ref, m_ref, v_ref, go_ref, d_ref, mo_ref, vo_ref):
        gv = g_ref[...]
        mn = ADAM_B1 * m_ref[...] + (1.0 - ADAM_B1) * gv
        vn = ADAM_B2 * v_ref[...] + (1.0 - ADAM_B2) * (gv * gv)
        m_hat = mn * inv_c1
        v_hat = vn * inv_c2
        d_ref[...] = -ADAM_LR * (m_hat / (jnp.sqrt(v_hat) + ADAM_EPS) + ADAM_WD * w_ref[...])
        go_ref[...] = gv
        mo_ref[...] = mn
        vo_ref[...] = vn

    blk = pl.BlockSpec((rb, cdim), lambda i: (i, 0))
    shape = jax.ShapeDtypeStruct((r, cdim), F32)
    body, in_specs, operands = _dep_args(body, [blk] * 4, [_hbm(w), _hbm(g), _hbm(m), _hbm(v)], dep)
    return pl.pallas_call(
        body,
        out_shape=[shape] * 4,
        grid=(r // rb,),
        in_specs=in_specs,
        out_specs=[blk] * 4,
        compiler_params=_cparams(("parallel",), 2 * rb * cdim * 4 * 8),
        name=name,
    )(*operands)


def _pack(arrays):
    tile = SUBLANE * LANE
    pieces = []
    for arr in arrays:
        flat = arr.reshape(-1)
        pad = (-flat.shape[0]) % tile
        if pad:
            flat = jnp.concatenate([flat, jnp.zeros((pad,), flat.dtype)])
        pieces.append(flat)
    return jnp.concatenate(pieces).reshape(-1, LANE)


def _unpack(packed, shapes):
    tile = SUBLANE * LANE
    flat = packed.reshape(-1)
    out, off = [], 0
    for shp in shapes:
        size = math.prod(shp)
        out.append(flat[off:off + size].reshape(shp))
        off += size + ((-size) % tile)
    return out


def _block_diag_groups(w, per_group):
    hcount, hd, _ = w.shape
    ng = hcount // per_group
    w4 = w.reshape(ng, per_group, hd, hd)
    eye = jnp.eye(per_group, dtype=w.dtype)
    bd = w4[:, :, :, None, :] * eye[None, :, None, :, None]
    return bd.reshape(ng, per_group * hd, per_group * hd).astype(BF16)


def _diag_blocks(wbd, per_group, hd):
    ng = wbd.shape[0]
    w5 = wbd.reshape(ng, per_group, hd, per_group, hd)
    blocks = [w5[:, i, :, i, :] for i in range(per_group)]
    return jnp.stack(blocks, axis=1).reshape(ng * per_group, hd, hd)


def kernel(x, g_mix, w_in, lru_conv_w, lru_conv_b, lru_wa, lru_ba, lru_wx, lru_bx, lru_lambda, lru_w_out, sc_conv_w, sc_w_out, w_o, g_ffn, ffn_w_up, ffn_conv_w, ffn_w_down, g_final, loss_target, m_g_mix, m_w_in, m_lru_conv_w, m_lru_conv_b, m_lru_wa, m_lru_ba, m_lru_wx, m_lru_bx, m_lru_lambda, m_lru_w_out, m_sc_conv_w, m_sc_w_out, m_w_o, m_g_ffn, m_ffn_w_up, m_ffn_conv_w, m_ffn_w_down, m_g_final, v_g_mix, v_w_in, v_lru_conv_w, v_lru_conv_b, v_lru_wa, v_lru_ba, v_lru_wx, v_lru_bx, v_lru_lambda, v_lru_w_out, v_sc_conv_w, v_sc_w_out, v_w_o, v_g_ffn, v_ffn_w_up, v_ffn_conv_w, v_ffn_w_down, v_g_final):
    seq, d_model = x.shape[1], x.shape[2]
    heads, head_dim, _ = lru_wa.shape
    d_lru = heads * head_dim
    d_sc = sc_w_out.shape[0]
    d_ff = ffn_w_down.shape[0] * N_CHIPS
    assert x.shape[0] == 1 and w_in.shape[1] * N_CHIPS == 2 * d_lru + 3 * d_sc + 2 * d_model
    xs = x.reshape(seq, d_model)
    target = loss_target.reshape(seq, d_model)

    chip = 2 * lax.axis_index("x") + lax.axis_index("y")
    core = lax.axis_index("c").astype(jnp.int32).reshape(1)

    big_w = [w_in, lru_w_out, sc_w_out, w_o, ffn_w_up, ffn_w_down]
    big_m = [m_w_in, m_lru_w_out, m_sc_w_out, m_w_o, m_ffn_w_up, m_ffn_w_down]
    big_v = [v_w_in, v_lru_w_out, v_sc_w_out, v_w_o, v_ffn_w_up, v_ffn_w_down]
    col_sharded = [True, True, True, False, True, False]
    conv_shards = [lru_conv_w, sc_conv_w, ffn_conv_w]
    conv_pack = jnp.concatenate(
        [jnp.pad(w, ((0, SUBLANE - w.shape[0]), (0, 0))) for w in conv_shards], axis=1)
    big_names = ["w_in", "lru_w_out", "sc_w_out", "w_o", "ffn_w_up", "ffn_w_down"]
    chip_arr = chip.astype(jnp.int32).reshape(1)
    placed = [_cast_place(big_w[0], chip_arr, col_sharded[0], "cast_" + big_names[0])]
    conv_all = _small_gather(conv_pack)
    shard_shapes = [w.shape for w in big_w]
    n_big = len(big_w)

    def gather_start(ks, after, tag):
        send, recv, bufs, token = _exchange_start(
            "gather_start_" + tag, [placed[k] for k in ks], 3 * n_big,
            _gather_plan(shard_shapes, col_sharded, ks), after=after)
        return (send, recv, dict(zip(ks, bufs))), token

    def arrived(state, ks, after, tag):
        send, recv, bufs = state
        got = _exchange_wait("gather_wait_" + tag, [bufs[k] for k in ks], send, recv, after,
                             _gather_plan(shard_shapes, col_sharded, ks))
        return _exchange("gather_forward_" + tag, got, 3 * len(ks), _forward_plan(shard_shapes, col_sharded, ks))

    conv_full, off = [], 0
    for w in conv_shards:
        kw, nq = w.shape
        piece = conv_all[:, :kw, off:off + nq]
        conv_full.append(piece.transpose(1, 0, 2).reshape(kw, N_CHIPS * nq))
        off += nq
    lcw, scw, fcw = conv_full

    per_group = max(1, min(heads, 256 // head_dim))
    gc = per_group * head_dim
    wa_bd = _block_diag_groups(lru_wa, per_group)
    wx_bd = _block_diag_groups(lru_wx, per_group)
    tc = _pick(seq, 256, SUBLANE)
    cb_sc = _pick(d_sc, 512)
    cb_ff = _pick(d_ff, 512)
    col_sc = 2 * d_lru
    col_gates = 2 * d_lru + 3 * d_sc

    first, token = gather_start([0], conv_all, "in")
    for k in range(1, n_big):
        placed.append(_cast_place(big_w[k], chip_arr, col_sharded[k], "cast_" + big_names[k], dep=token))
        token = placed[-1]
    h1 = _rms_fwd(xs, g_mix, "rms_mix", dep=token)
    (win_b,) = arrived(first, [0], h1, "in")
    rest, token = gather_start([1, 2, 3, 4, 5], win_b, "rest")
    p = _mm(h1, win_b, "nn", F32, name="mm_in", dep=token)
    wlo_b, wso_b, wo_b = arrived(rest, [1, 2, 3], p, "mix")
    y_lru_pre, hseq = _lru_fwd(p, lcw, lru_conv_b, wa_bd, lru_ba, wx_bd, lru_bx, lru_lambda, d_lru, gc, tc)
    y_sc_pre = _sc_fwd(p, scw, col_sc, d_sc, cb_sc, tc)
    y_lru = _mm(y_lru_pre, wlo_b, "nn", BF16, name="mm_lru_out")
    y_sc = _mm(y_sc_pre, wso_b, "nn", BF16, name="mm_sc_out")
    merged = _merge_fwd(p, y_lru, y_sc, col_gates, tc)
    x2 = _mm(merged, wo_b, "nn", F32, res=xs, name="mm_o")
    (wup_b,) = arrived(rest, [4], x2, "up")
    h2 = _rms_fwd(x2, g_ffn, "rms_ffn")
    up = _mm(h2, wup_b, "nn", F32, name="mm_up")
    (wdn_b,) = arrived(rest, [5], up, "down")
    act = _ffn_act_fwd(up, fcw, d_ff, cb_ff, tc)
    x3 = _mm(act, wdn_b, "nn", F32, res=x2, name="mm_down")
    loss_part, dx3, dx3b, dg_final = _loss_head(x3, g_final, target)

    where = jnp.concatenate([chip_arr, core])

    def reduce_start(grads, flags, tag):
        views = [_as3d(g, cs) for g, cs in zip(grads, flags)]
        lands = [lax.empty((v.shape[0], v.shape[1] // 2, v.shape[2]), F32) for v in views]
        send, recv, bufs, token = _exchange_start("grad_pair_start_" + tag, views + lands, len(views),
                                                  _pair_plan(len(views)))
        return (send, recv, bufs, flags, tag), token

    def reduce_mid(state, after):
        send, recv, bufs, flags, tag = state
        m = len(flags)
        bufs = _exchange_wait("grad_pair_wait_" + tag, bufs, send, recv, after, _pair_plan(m))
        partials = [_pair_add(bufs[i], bufs[m + i], core) for i in range(m)]
        lands = []
        for pz, cs in zip(partials, flags):
            _, h, cdim = pz.shape
            lands.append(lax.empty((N_CHIPS - 1, h, cdim // N_CHIPS if cs else cdim), BF16))
        send, recv, bufs, token = _exchange_start("grad_chip_start_" + tag, partials + lands, 3 * m,
                                                  _chip_plan(flags))
        return (send, recv, bufs, flags, tag), token

    def reduce_end(state, after):
        send, recv, bufs, flags, tag = state
        m = len(flags)
        bufs = _exchange_wait("grad_chip_wait_" + tag, bufs, send, recv, after, _chip_plan(flags))
        return [_chip_sum(bufs[i], bufs[m + i], where, flags[i]) for i in range(m)]

    g_wdn = _mm(act, dx3b, "tn", F32, name="mm_down_dw")
    red_down, token = reduce_start([g_wdn], [False], "down")
    dact = _mm(dx3b, wdn_b, "nt", F32, name="mm_down_dx", dep=token)
    red_down, token = reduce_mid(red_down, dact)
    dup, dfcw_g, dfcw_v = _ffn_act_bwd(up, dact, fcw, d_ff, cb_ff, tc, dep=token)
    g_wup = _mm(h2, dup, "tn", F32, name="mm_up_dw", slabs=[0, 1])
    red_up, token = reduce_start([g_wup], [True], "up")
    dh2 = _mm(dup, wup_b, "nt", F32, name="mm_up_dx", dep=token, slabs=[0, 1])
    red_up, token = reduce_mid(red_up, dh2)
    dx2, dx2b, dg_ffn = _rms_bwd(x2, g_ffn, dh2, dx3, "rms_ffn_bwd", True, dep=token)
    g_wo = _mm(merged, dx2b, "tn", F32, name="mm_o_dw")
    dmerged = _mm(dx2b, wo_b, "nt", BF16, name="mm_o_dx")
    slab_w = d_lru
    assert d_sc == slab_w and d_model % slab_w == 0 and col_gates % slab_w == 0
    n_gate = d_model // slab_w
    gate0 = col_gates // slab_w
    dp_slabs = [gate0 + kind * n_gate + j for j in range(n_gate) for kind in (0, 1)] + [0, 1, 2, 3, 4]
    dp, dyl, dys = _merge_bwd(p, y_lru, y_sc, dmerged, col_gates, tc, len(dp_slabs))
    assert dp.shape[2] == slab_w
    g_wlo = _mm(y_lru_pre, dyl, "tn", F32, name="mm_lru_out_dw")
    g_wso = _mm(y_sc_pre, dys, "tn", F32, name="mm_sc_out_dw")
    red_mix, token = reduce_start([g_wlo, g_wso, g_wo], [True, True, False], "mix")
    dylp = _mm(dyl, wlo_b, "nt", F32, name="mm_lru_out_dx", dep=token)
    dysp = _mm(dys, wso_b, "nt", F32, name="mm_sc_out_dx")
    red_mix, token = reduce_mid(red_mix, dysp)
    dp, dlcw, dlcb, dwa_bd, dba, dwx_bd, dbx, dlam = _lru_bwd(
        p, hseq, dylp, lcw, lru_conv_b, wa_bd, lru_ba, wx_bd, lru_bx, lru_lambda, d_lru, gc, tc,
        dp, 2 * n_gate, dep=token)
    dp, dscw = _sc_bwd(p, dysp, scw, col_sc, d_sc, cb_sc, tc, dp, 2 * n_gate + 2)
    g_win = _mm(h1, dp, "tn", F32, name="mm_in_dw", slabs=dp_slabs)
    red_in, token = reduce_start([g_win], [True], "in")
    dh1 = _mm(dp, win_b, "nt", F32, name="mm_in_dx", dep=token, slabs=dp_slabs)
    grad_x, dg_mix = _rms_bwd(xs, g_mix, dh1, dx2, "rms_mix_bwd", False)

    small_g = [dg_mix, dlcw, dlcb, _diag_blocks(dwa_bd, per_group, head_dim), dba,
               _diag_blocks(dwx_bd, per_group, head_dim), dbx, dlam, dscw, dg_ffn,
               jnp.concatenate([dfcw_g, dfcw_v], axis=1), dg_final]
    small_shapes = [a.shape for a in small_g]
    small_sum = _small_allreduce(_pack(small_g))
    red_in, token = reduce_mid(red_in, small_sum)
    (h_wdn,) = reduce_end(red_down, token)
    (h_wup,) = reduce_end(red_up, token)
    h_wlo, h_wso, h_wo = reduce_end(red_mix, token)
    s_wlo, s_wso, s_wo, s_wup, s_wdn = _exchange("grad_share_a", [h_wlo, h_wso, h_wo, h_wup, h_wdn], 5,
                                                 _share_plan(5))
    early = {1: s_wlo, 2: s_wso, 3: s_wo, 4: s_wup, 5: s_wdn}
    big_out = [None] * n_big
    last = None
    for k, g in early.items():
        big_out[k] = _adamw(big_w[k], g, big_m[k], big_v[k], "adamw_" + big_names[k], dep=last)
        last = big_out[k][1]
    (h_win,) = reduce_end(red_in, last)
    (s_win,) = _exchange("grad_share_b", [h_win], 1, _share_plan(1))
    big_out[0] = _adamw(big_w[0], s_win, big_m[0], big_v[0], "adamw_" + big_names[0])
    sg = _unpack(small_sum, small_shapes)
    for idx in (1, 8, 10):
        nq = sg[idx].shape[1] // N_CHIPS
        sg[idx] = lax.dynamic_slice_in_dim(sg[idx], chip * nq, nq, axis=1)
    small_w = [g_mix, lru_conv_w, lru_conv_b, lru_wa, lru_ba, lru_wx, lru_bx, lru_lambda, sc_conv_w,
               g_ffn, ffn_conv_w, g_final]
    small_m = [m_g_mix, m_lru_conv_w, m_lru_conv_b, m_lru_wa, m_lru_ba, m_lru_wx, m_lru_bx, m_lru_lambda,
               m_sc_conv_w, m_g_ffn, m_ffn_conv_w, m_g_final]
    small_v = [v_g_mix, v_lru_conv_w, v_lru_conv_b, v_lru_wa, v_lru_ba, v_lru_wx, v_lru_bx, v_lru_lambda,
               v_sc_conv_w, v_g_ffn, v_ffn_conv_w, v_g_final]
    sg = [g.reshape(w.shape) for g, w in zip(sg, small_w)]
    w_shapes = [w.shape for w in small_w]
    packed = _adamw(_pack(small_w), _pack(sg), _pack(small_m), _pack(small_v), "adamw_small")
    small_out = [_unpack(pk, w_shapes) for pk in packed]

    order = [(0, 0), (1, 0), (0, 1), (0, 2), (0, 3), (0, 4), (0, 5), (0, 6), (0, 7), (1, 1), (0, 8), (1, 2),
             (1, 3), (0, 9), (1, 4), (0, 10), (1, 5), (0, 11)]
    by_kind = []
    for kind in range(4):
        by_kind.append([big_out[i][kind] if is_big else small_out[kind][i] for is_big, i in order])
    loss = lax.psum(loss_part[0, 0], ("x", "y", "c"))
    return (loss, grad_x.reshape(x.shape), *by_kind[0], *by_kind[1], *by_kind[2], *by_kind[3])
```

```python
import math

import jax
import jax.numpy as jnp
from jax import lax
from jax.experimental import pallas as pl
from jax.experimental.pallas import tpu as pltpu

F32 = jnp.float32
BF16 = jnp.bfloat16

LANE = 128
SUBLANE = 8
BF16_ROWS = 16
VMEM_BYTES_V7X = 64 * 1024 * 1024
VMEM_BUDGET = VMEM_BYTES_V7X - 8 * 1024 * 1024
MM_VMEM_BUDGET = 42 * 1024 * 1024
EPS = 1e-6
LRU_C = 8.0
ADAM_LR = 0.001
ADAM_B1 = 0.9
ADAM_B2 = 0.999
ADAM_EPS = 1e-08
ADAM_WD = 0.01
ADAM_STEP = 10

N_CHIPS = 4
N_DEV = 8
MESH = pl.DeviceIdType.MESH
ANY = pl.BlockSpec(memory_space=pl.ANY)
VMEM_SPEC = pl.BlockSpec(memory_space=pltpu.VMEM)
HBM_SPEC = pl.BlockSpec(memory_space=pltpu.HBM)
SEM_SPEC = pl.BlockSpec(memory_space=pltpu.SEMAPHORE)
DATAFLOW_EFFECT = pltpu.SideEffectType.DATAFLOW_SIDE_EFFECTING


def _pick(n, cap, mult=LANE):
    best = None
    d = mult
    while d <= min(n, cap):
        if n % d == 0:
            best = d
        d += mult
    return n if best is None else best


def _cparams(semantics, block_bytes):
    limit = min(VMEM_BUDGET, max(32 * 1024 * 1024, int(block_bytes * 1.25) + (4 << 20)))
    return pltpu.CompilerParams(dimension_semantics=semantics, vmem_limit_bytes=limit)


def _nbytes(shape, dtype):
    return math.prod(shape) * jnp.dtype(dtype).itemsize


def _sigmoid(z):
    return 1.0 / (1.0 + jnp.exp(-z))


def _softplus(z):
    e = jnp.exp(-jnp.abs(z))
    u = 1.0 + e
    log1p = jnp.where(u == 1.0, e, jnp.log(u) * (e / (u - 1.0)))
    return jnp.maximum(z, 0.0) + log1p


def _neg_expm1(z):
    small = z * (1.0 + z * (0.5 + z * (1.0 / 6.0 + z * (1.0 / 24.0))))
    return -jnp.where(jnp.abs(z) < 0.03, small, jnp.exp(z) - 1.0)


_GELU_K = math.sqrt(2.0 / math.pi)
_GELU_C = 0.044715


def _gelu_and_grad(z):
    z2 = z * z
    th = jnp.tanh(_GELU_K * (z + _GELU_C * z2 * z))
    val = 0.5 * z * (1.0 + th)
    grad = 0.5 * (1.0 + th) + 0.5 * z * (1.0 - th * th) * (_GELU_K * (1.0 + 3.0 * _GELU_C * z2))
    return val, grad


def _rows_before(cat, k):
    if k == 0:
        return cat[SUBLANE:, :]
    return pltpu.roll(cat, k, 0)[SUBLANE:, :]


def _rows_after(cat, k):
    n = cat.shape[0]
    if k == 0:
        return cat[:n - SUBLANE, :]
    return pltpu.roll(cat, n - k, 0)[:n - SUBLANE, :]


def _conv_fwd(cat, w, width):
    y = _rows_before(cat, width - 1) * w[0:1, :]
    for k in range(1, width):
        y = y + _rows_before(cat, width - 1 - k) * w[k:k + 1, :]
    return y


def _conv_bwd_input(cat, w, width):
    dx = _rows_after(cat, width - 1) * w[0:1, :]
    for k in range(1, width):
        dx = dx + _rows_after(cat, width - 1 - k) * w[k:k + 1, :]
    return dx


def _conv_bwd_weight(dw_ref, dy, catx, width):
    for k in range(width):
        dw_ref[k:k + 1, :] += jnp.sum(dy * _rows_before(catx, width - 1 - k), axis=0, keepdims=True)


def _scan_tiles(a_ref, b_ref, out_ref, carry0, n_rows, reverse):
    cols = a_ref.shape[1]
    row = lax.broadcasted_iota(jnp.int32, (SUBLANE, cols), 0)
    n_tiles = n_rows // SUBLANE

    def step(j, carry):
        tile = (n_tiles - 1 - j) if reverse else j
        off = pl.multiple_of(tile * SUBLANE, SUBLANE)
        a = a_ref[pl.ds(off, SUBLANE), :]
        b = b_ref[pl.ds(off, SUBLANE), :]
        for s in (1, 2, 4):
            if reverse:
                keep = row < SUBLANE - s
                shift = SUBLANE - s
            else:
                keep = row >= s
                shift = s
            a_sh = jnp.where(keep, pltpu.roll(a, shift, 0), 1.0)
            b_sh = jnp.where(keep, pltpu.roll(b, shift, 0), 0.0)
            b = a * b_sh + b
            a = a * a_sh
        out = a * carry + b
        out_ref[pl.ds(off, SUBLANE), :] = out
        return out[0:1, :] if reverse else out[SUBLANE - 1:SUBLANE, :]

    return lax.fori_loop(0, n_tiles, step, carry0)


def _dep_args(body, in_specs, operands, *deps):
    deps = [d for d in deps if d is not None]
    if not deps:
        return body, in_specs, operands
    n = len(operands)

    def wrapped(*refs):
        return body(*refs[:n], *refs[n + len(deps):])

    return wrapped, list(in_specs) + [ANY] * len(deps), list(operands) + deps


def _mm(a, b, mode, out_dtype, res=None, name=None, dep=None, slabs=None):
    assert a.dtype == BF16 and b.dtype == BF16
    a_slabbed, b_slabbed = a.ndim == 3, b.ndim == 3
    assert not a_slabbed or (mode == "nt" and slabs is not None)
    assert not b_slabbed or (mode == "tn" and slabs is not None)
    if mode == "nn":
        (m, k), (k2, n) = a.shape, b.shape
        dims = (((1,), (0,)), ((), ()))
    elif mode == "nt":
        m, k = (a.shape[1], a.shape[0] * a.shape[2]) if a_slabbed else a.shape
        n, k2 = b.shape
        dims = (((1,), (1,)), ((), ()))
    else:
        k, m = a.shape
        k2, n = (b.shape[1], b.shape[0] * b.shape[2]) if b_slabbed else b.shape
        dims = (((0,), (0,)), ((), ()))
    assert k == k2
    n_unit = b.shape[2] if b_slabbed else n
    out_bytes = jnp.dtype(out_dtype).itemsize
    bm = _pick(m, 1024)
    bn = _pick(n_unit, 1024)
    bk = k

    def est(bm_, bn_, bk_):
        e = 2 * (bm_ * bk_ + bk_ * bn_) * 2 + 2 * bm_ * bn_ * out_bytes
        if k // bk_ > 1:
            e += bm_ * bn_ * 4
        if res is not None:
            e += 2 * bm_ * bn_ * 4
        return e

    for shrink_n, floor in ((True, 512), (False, 512), (True, 256), (False, 256)):
        while est(bm, bn, bk) > MM_VMEM_BUDGET:
            if shrink_n and bn > floor and bn % 2 == 0 and n_unit % (bn // 2) == 0:
                bn //= 2
            elif not shrink_n and bm > floor and bm % 2 == 0 and m % (bm // 2) == 0:
                bm //= 2
            else:
                break
    while (est(bm, bn, bk) > MM_VMEM_BUDGET and not a_slabbed and bk % (2 * LANE) == 0
           and k % (bk // 2) == 0):
        bk //= 2
    nk = k // bk
    per_slab = n_unit // bn

    def out_col(j):
        if not b_slabbed:
            return j
        s = j // per_slab
        where = sum(jnp.where(s == t, slabs[t], 0) for t in range(len(slabs)))
        return where * per_slab + j % per_slab

    if mode == "tn":
        a_spec = pl.BlockSpec((bk, bm), lambda i, j, kk: (kk, i))
    elif a_slabbed:
        a_spec = pl.BlockSpec((a.shape[0], bm, a.shape[2]), lambda i, j, kk: (0, i, 0))
    else:
        a_spec = pl.BlockSpec((bm, bk), lambda i, j, kk: (i, kk))
    if mode == "nt":
        b_spec = pl.BlockSpec((bn, bk), lambda i, j, kk: (j, kk))
    elif b_slabbed:
        b_spec = pl.BlockSpec((None, bk, bn), lambda i, j, kk: (j // per_slab, kk, j % per_slab))
    else:
        b_spec = pl.BlockSpec((bk, bn), lambda i, j, kk: (kk, j))
    o_spec = pl.BlockSpec((bm, bn), lambda i, j, kk: (i, out_col(j)))
    in_specs = [a_spec, b_spec]
    operands = [a, b]
    if res is not None:
        in_specs.append(o_spec)
        operands.append(res)
    has_res = res is not None

    def body(*refs):
        a_ref, b_ref = refs[0], refs[1]
        res_ref = refs[2] if has_res else None
        o_ref = refs[2 + has_res]
        if a_slabbed:
            width = a_ref.shape[2]
            part = None
            for s, col in enumerate(slabs):
                term = lax.dot_general(a_ref[s], b_ref[:, col * width:(col + 1) * width], dims,
                                       preferred_element_type=F32)
                part = term if part is None else part + term
        else:
            part = lax.dot_general(a_ref[...], b_ref[...], dims, preferred_element_type=F32)
        if nk == 1:
            if has_res:
                part = part + res_ref[...]
            o_ref[...] = part.astype(o_ref.dtype)
            return
        acc_ref = refs[-1]
        kk = pl.program_id(2)

        @pl.when(kk == 0)
        def _():
            acc_ref[...] = part

        @pl.when(kk > 0)
        def _():
            acc_ref[...] += part

        @pl.when(kk == nk - 1)
        def _():
            total = acc_ref[...]
            if has_res:
                total = total + res_ref[...]
            o_ref[...] = total.astype(o_ref.dtype)

    scratch = [pltpu.VMEM((bm, bn), F32)] if nk > 1 else []
    body, in_specs, operands = _dep_args(body, in_specs, operands, dep)
    return pl.pallas_call(
        body,
        out_shape=jax.ShapeDtypeStruct((m, n), out_dtype),
        grid=(m // bm, n // bn, nk),
        in_specs=in_specs,
        out_specs=o_spec,
        scratch_shapes=scratch,
        compiler_params=_cparams(("parallel", "parallel", "arbitrary"), est(bm, bn, bk)),
        name=name,
    )(*operands)


def _rms_fwd(x, g, name, dep=None):
    t, d = x.shape
    tb = _pick(t, 512, SUBLANE)

    def body(x_ref, g_ref, h_ref):
        xv = x_ref[...]
        r = lax.rsqrt(jnp.mean(xv * xv, axis=-1, keepdims=True) + EPS)
        h_ref[...] = ((xv * r) * g_ref[...]).astype(BF16)

    blk = pl.BlockSpec((tb, d), lambda i: (i, 0))
    body, in_specs, operands = _dep_args(
        body, [blk, pl.BlockSpec((1, d), lambda i: (0, 0))], [x, g.reshape(1, d)], dep)
    return pl.pallas_call(
        body,
        out_shape=jax.ShapeDtypeStruct((t, d), BF16),
        grid=(t // tb,),
        in_specs=in_specs,
        out_specs=blk,
        compiler_params=_cparams(("parallel",), 2 * tb * d * 6),
        name=name,
    )(*operands)


def _rms_bwd(x, g, dh, dres, name, want_bf16, dep=None):
    t, d = x.shape
    tb = _pick(t, 256, SUBLANE)

    def body(x_ref, g_ref, dh_ref, dres_ref, *outs):
        dx_ref, dg_ref = outs[0], outs[-1]
        xv = x_ref[...]
        r = lax.rsqrt(jnp.mean(xv * xv, axis=-1, keepdims=True) + EPS)
        xhat = xv * r
        dhv = dh_ref[...]
        dxhat = dhv * g_ref[...]
        dx = dres_ref[...] + r * (dxhat - xhat * jnp.mean(dxhat * xhat, axis=-1, keepdims=True))
        dx_ref[...] = dx
        if want_bf16:
            outs[1][...] = dx.astype(BF16)

        @pl.when(pl.program_id(0) == 0)
        def _():
            dg_ref[...] = jnp.zeros_like(dg_ref)

        dg_ref[...] += jnp.sum(dhv * xhat, axis=0, keepdims=True)

    blk = pl.BlockSpec((tb, d), lambda i: (i, 0))
    row = pl.BlockSpec((1, d), lambda i: (0, 0))
    out_shape = [jax.ShapeDtypeStruct((t, d), F32)]
    out_specs = [blk]
    if want_bf16:
        out_shape.append(jax.ShapeDtypeStruct((t, d), BF16))
        out_specs.append(blk)
    out_shape.append(jax.ShapeDtypeStruct((1, d), F32))
    out_specs.append(row)
    body, in_specs, operands = _dep_args(
        body, [blk, row, blk, blk], [x, g.reshape(1, d), dh, dres], dep)
    return pl.pallas_call(
        body,
        out_shape=out_shape,
        grid=(t // tb,),
        in_specs=in_specs,
        out_specs=out_specs,
        compiler_params=_cparams(("arbitrary",), 2 * tb * d * 18),
        name=name,
    )(*operands)


def _loss_head(x3, g, target):
    t, d = x3.shape
    tb = _pick(t, 256, SUBLANE)

    def body(x_ref, g_ref, t_ref, loss_ref, dx_ref, dxb_ref, dg_ref):
        xv = x_ref[...]
        gv = g_ref[...]
        r = lax.rsqrt(jnp.mean(xv * xv, axis=-1, keepdims=True) + EPS)
        xhat = xv * r
        err = xhat * gv - t_ref[...]
        dy = err * (1.0 / d)
        dxhat = dy * gv
        dx = r * (dxhat - xhat * jnp.mean(dxhat * xhat, axis=-1, keepdims=True))
        dx_ref[...] = dx
        dxb_ref[...] = dx.astype(BF16)

        @pl.when(pl.program_id(0) == 0)
        def _():
            dg_ref[...] = jnp.zeros_like(dg_ref)
            loss_ref[...] = jnp.zeros_like(loss_ref)

        dg_ref[...] += jnp.sum(dy * xhat, axis=0, keepdims=True)
        per_token = jnp.mean(err * err, axis=-1, keepdims=True)
        loss_ref[...] += 0.5 * jnp.sum(per_token, axis=0, keepdims=True)

    blk = pl.BlockSpec((tb, d), lambda i: (i, 0))
    row = pl.BlockSpec((1, d), lambda i: (0, 0))
    return pl.pallas_call(
        body,
        out_shape=[jax.ShapeDtypeStruct((1, 1), F32), jax.ShapeDtypeStruct((t, d), F32),
                   jax.ShapeDtypeStruct((t, d), BF16), jax.ShapeDtypeStruct((1, d), F32)],
        grid=(t // tb,),
        in_specs=[blk, row, blk],
        out_specs=[pl.BlockSpec((1, 1), lambda i: (0, 0)), blk, blk, row],
        compiler_params=_cparams(("arbitrary",), 2 * tb * d * 14),
        name="loss_head",
    )(x3, g.reshape(1, d), target)


def _lru_gates(xc, wa, ba, wx, bx, lam):
    nn = (((1,), (0,)), ((), ()))
    xcb = xc.astype(BF16)
    r = _sigmoid(lax.dot_general(xcb, wa, nn, preferred_element_type=F32) + ba)
    i = _sigmoid(lax.dot_general(xcb, wx, nn, preferred_element_type=F32) + bx)
    cl = -LRU_C * _softplus(-lam)
    log_a = cl * r
    a = jnp.exp(log_a)
    one_minus_a2 = _neg_expm1(2.0 * log_a)
    return xcb, r, i, a, one_minus_a2, cl


def _lru_fwd(p, conv_w, conv_b, wa_bd, ba, wx_bd, bx, lam, d_lru, gc, tc):
    t = p.shape[0]
    ng = d_lru // gc
    nt = t // tc
    width = conv_w.shape[0]

    def body(lx_ref, gate_ref, cw_ref, cb_ref, wa_ref, ba_ref, wx_ref, bx_ref, lam_ref,
             y_ref, h_ref, halo, hcar, a_s, u_s):
        @pl.when(pl.program_id(1) == 0)
        def _():
            halo[...] = jnp.zeros_like(halo)
            hcar[...] = jnp.zeros_like(hcar)

        x = lx_ref[...]
        cat = jnp.concatenate([halo[...], x], axis=0)
        halo[...] = x[tc - SUBLANE:, :]
        xc = _conv_fwd(cat, cw_ref[...], width) + cb_ref[...]
        _, r, i, a, om, _ = _lru_gates(xc, wa_ref[...], ba_ref[...], wx_ref[...], bx_ref[...], lam_ref[...])
        a_s[...] = a
        u_s[...] = jnp.sqrt(om) * (i * xc)
        hcar[0:1, :] = _scan_tiles(a_s, u_s, h_ref, hcar[0:1, :], tc, reverse=False)
        gl, _ = _gelu_and_grad(gate_ref[...])
        y_ref[...] = (gl * h_ref[...]).astype(BF16)

    blk = lambda off: pl.BlockSpec((tc, gc), lambda g, s, off=off: (s, off + g))
    rowv = lambda rows: pl.BlockSpec((rows, gc), lambda g, s: (0, g))
    wspec = pl.BlockSpec((None, gc, gc), lambda g, s: (g, 0, 0))
    out_blk = pl.BlockSpec((tc, gc), lambda g, s: (s, g))
    return pl.pallas_call(
        body,
        out_shape=[jax.ShapeDtypeStruct((t, d_lru), BF16), jax.ShapeDtypeStruct((t, d_lru), F32)],
        grid=(ng, nt),
        in_specs=[blk(0), blk(ng), rowv(width), rowv(1), wspec, rowv(1), wspec, rowv(1), rowv(1)],
        out_specs=[out_blk, out_blk],
        scratch_shapes=[pltpu.VMEM((SUBLANE, gc), F32), pltpu.VMEM((SUBLANE, gc), F32),
                        pltpu.VMEM((tc, gc), F32), pltpu.VMEM((tc, gc), F32)],
        compiler_params=_cparams(("parallel", "arbitrary"), 40 * tc * gc * 4),
        name="lru_fwd",
    )(p, p, conv_w, conv_b.reshape(1, -1), wa_bd, ba.reshape(1, -1), wx_bd, bx.reshape(1, -1),
      lam.reshape(1, -1))


def _lru_bwd(p, hseq, dyp, conv_w, conv_b, wa_bd, ba, wx_bd, bx, lam, d_lru, gc, tc, dp, slab0, dep=None):
    t = p.shape[0]
    ng = d_lru // gc
    nt = t // tc
    width = conv_w.shape[0]
    halo_blocks = tc // SUBLANE
    nn = (((1,), (0,)), ((), ()))
    nt_dims = (((1,), (1,)), ((), ()))
    tn_dims = (((0,), (0,)), ((), ()))

    def body(lx_ref, lxh_ref, gate_ref, h_ref, hh_ref, dyp_ref,
             cw_ref, cb_ref, wa_ref, ba_ref, wx_ref, bx_ref, lam_ref,
             dp_ref, dcw_ref, dcb_ref, dwa_ref, dba_ref, dwx_ref, dbx_ref, dlam_ref,
             nxt_dxc, nxt_a, nxt_g, al_s, b_s, g_s):
        s = pl.program_id(1)
        first_chunk = s == nt - 1

        @pl.when(s == 0)
        def _():
            nxt_dxc[...] = jnp.zeros_like(nxt_dxc)
            nxt_a[...] = jnp.zeros_like(nxt_a)
            nxt_g[...] = jnp.zeros_like(nxt_g)
            for ref in (dcw_ref, dcb_ref, dwa_ref, dba_ref, dwx_ref, dbx_ref, dlam_ref):
                ref[...] = jnp.zeros_like(ref)

        keep = jnp.where(first_chunk, 0.0, 1.0)
        x = lx_ref[...]
        catx = jnp.concatenate([lxh_ref[...] * keep, x], axis=0)
        cw = cw_ref[...]
        xc = _conv_fwd(catx, cw, width) + cb_ref[...]
        wa = wa_ref[...]
        wx = wx_ref[...]
        lam_v = lam_ref[...]
        xcb, r, i, a, om, cl = _lru_gates(xc, wa, ba_ref[...], wx, bx_ref[...], lam_v)
        mult = jnp.sqrt(om)

        h = h_ref[...]
        hprev = _rows_before(jnp.concatenate([hh_ref[...] * keep, h], axis=0), 1)
        gl, dgl = _gelu_and_grad(gate_ref[...])
        dyp_v = dyp_ref[...]
        dp_ref[1] = (dyp_v * h * dgl).astype(BF16)

        al_s[...] = _rows_after(jnp.concatenate([a, nxt_a[...]], axis=0), 1)
        b_s[...] = dyp_v * gl
        nxt_g[0:1, :] = _scan_tiles(al_s, b_s, g_s, nxt_g[0:1, :], tc, reverse=True)
        nxt_a[...] = a[0:SUBLANE, :]
        du = g_s[...]

        da = du * hprev
        dmult = du * (i * xc)
        di = du * mult * xc
        dxc = du * mult * i
        dlog_a = da * a - dmult * (a * a / mult)
        dlam_ref[...] += jnp.sum(dlog_a * r, axis=0, keepdims=True) * (LRU_C * _sigmoid(-lam_v))
        dza = (dlog_a * cl) * r * (1.0 - r)
        dzx = di * i * (1.0 - i)
        dba_ref[...] += jnp.sum(dza, axis=0, keepdims=True)
        dbx_ref[...] += jnp.sum(dzx, axis=0, keepdims=True)
        dzab = dza.astype(BF16)
        dzxb = dzx.astype(BF16)
        dwa_ref[...] += lax.dot_general(xcb, dzab, tn_dims, preferred_element_type=F32)
        dwx_ref[...] += lax.dot_general(xcb, dzxb, tn_dims, preferred_element_type=F32)
        dxc = dxc + lax.dot_general(dzab, wa, nt_dims, preferred_element_type=F32)
        dxc = dxc + lax.dot_general(dzxb, wx, nt_dims, preferred_element_type=F32)
        dcb_ref[...] += jnp.sum(dxc, axis=0, keepdims=True)
        _conv_bwd_weight(dcw_ref, dxc, catx, width)
        catd = jnp.concatenate([dxc, nxt_dxc[...]], axis=0)
        dp_ref[0] = _conv_bwd_input(catd, cw, width).astype(BF16)
        nxt_dxc[...] = dxc[0:SUBLANE, :]

    rev = lambda s: nt - 1 - s
    blk = lambda off: pl.BlockSpec((tc, gc), lambda g, s, off=off: (rev(s), off + g))
    halo = lambda off: pl.BlockSpec(
        (SUBLANE, gc), lambda g, s, off=off: (jnp.maximum(rev(s) * halo_blocks - 1, 0), off + g))
    rowv = lambda rows: pl.BlockSpec((rows, gc), lambda g, s: (0, g))
    wspec = pl.BlockSpec((None, gc, gc), lambda g, s: (g, 0, 0))
    out_blk = pl.BlockSpec((tc, gc), lambda g, s: (rev(s), g))
    vec = lambda rows: jax.ShapeDtypeStruct((rows, d_lru), F32)
    wshape = jax.ShapeDtypeStruct((ng, gc, gc), F32)
    body, in_specs, operands = _dep_args(
        body,
        [blk(0), halo(0), blk(ng), blk(0), halo(0), blk(0),
         rowv(width), rowv(1), wspec, rowv(1), wspec, rowv(1), rowv(1)],
        [p, p, p, hseq, hseq, dyp,
         conv_w, conv_b.reshape(1, -1), wa_bd, ba.reshape(1, -1), wx_bd,
         bx.reshape(1, -1), lam.reshape(1, -1)], dp, dep)
    assert dp.shape[2] == d_lru and slab0 % 2 == 0
    return pl.pallas_call(
        body,
        out_shape=[jax.ShapeDtypeStruct(dp.shape, dp.dtype),
                   vec(width), vec(1), wshape, vec(1), wshape, vec(1), vec(1)],
        grid=(ng, nt),
        in_specs=in_specs,
        out_specs=[pl.BlockSpec((2, tc, gc), lambda g, s: (slab0 // 2, rev(s), g)),
                   rowv(width), rowv(1), wspec, rowv(1), wspec, rowv(1), rowv(1)],
        input_output_aliases={13: 0},
        scratch_shapes=[pltpu.VMEM((SUBLANE, gc), F32), pltpu.VMEM((SUBLANE, gc), F32),
                        pltpu.VMEM((SUBLANE, gc), F32),
                        pltpu.VMEM((tc, gc), F32), pltpu.VMEM((tc, gc), F32), pltpu.VMEM((tc, gc), F32)],
        compiler_params=_cparams(("parallel", "arbitrary"), 80 * tc * gc * 4),
        name="lru_bwd",
    )(*operands)


def _sc_fwd(p, conv_w, col0, d_sc, cb, tc):
    t = p.shape[0]
    nc = d_sc // cb
    nt = t // tc
    width = conv_w.shape[0]
    base = col0 // cb

    def body(b_ref, c_ref, v_ref, w_ref, y_ref, halo):
        @pl.when(pl.program_id(1) == 0)
        def _():
            halo[...] = jnp.zeros_like(halo)

        cv = c_ref[...] * v_ref[...]
        cat = jnp.concatenate([halo[...], cv], axis=0)
        halo[...] = cv[tc - SUBLANE:, :]
        y_ref[...] = (b_ref[...] * _conv_fwd(cat, w_ref[...], width)).astype(BF16)

    blk = lambda slab: pl.BlockSpec((tc, cb), lambda j, s, slab=slab: (s, base + slab * nc + j))
    return pl.pallas_call(
        body,
        out_shape=jax.ShapeDtypeStruct((t, d_sc), BF16),
        grid=(nc, nt),
        in_specs=[blk(0), blk(1), blk(2), pl.BlockSpec((width, cb), lambda j, s: (0, j))],
        out_specs=pl.BlockSpec((tc, cb), lambda j, s: (s, j)),
        scratch_shapes=[pltpu.VMEM((SUBLANE, cb), F32)],
        compiler_params=_cparams(("parallel", "arbitrary"), 20 * tc * cb * 4),
        name="sc_fwd",
    )(p, p, p, conv_w)


def _sc_bwd(p, dyp, conv_w, col0, d_sc, cb, tc, dp, slab0):
    t = p.shape[0]
    nc = d_sc // cb
    nt = t // tc
    width = conv_w.shape[0]
    base = col0 // cb
    halo_blocks = tc // SUBLANE

    def body(b_ref, c_ref, ch_ref, v_ref, vh_ref, dyp_ref, w_ref,
             dp_ref, dw_ref, nxt_dq):
        s = pl.program_id(1)

        @pl.when(s == 0)
        def _():
            nxt_dq[...] = jnp.zeros_like(nxt_dq)
            dw_ref[...] = jnp.zeros_like(dw_ref)

        keep = jnp.where(s == nt - 1, 0.0, 1.0)
        cvals = c_ref[...]
        vvals = v_ref[...]
        w = w_ref[...]
        catcv = jnp.concatenate([ch_ref[...] * vh_ref[...] * keep, cvals * vvals], axis=0)
        q = _conv_fwd(catcv, w, width)
        dyp_v = dyp_ref[...]
        dp_ref[0] = (dyp_v * q).astype(BF16)
        dq = dyp_v * b_ref[...]
        _conv_bwd_weight(dw_ref, dq, catcv, width)
        dcv = _conv_bwd_input(jnp.concatenate([dq, nxt_dq[...]], axis=0), w, width)
        nxt_dq[...] = dq[0:SUBLANE, :]
        dp_ref[1] = (dcv * vvals).astype(BF16)
        dp_ref[2] = (dcv * cvals).astype(BF16)

    rev = lambda s: nt - 1 - s
    blk = lambda slab: pl.BlockSpec((tc, cb), lambda j, s, slab=slab: (rev(s), base + slab * nc + j))
    halo = lambda slab: pl.BlockSpec(
        (SUBLANE, cb),
        lambda j, s, slab=slab: (jnp.maximum(rev(s) * halo_blocks - 1, 0), base + slab * nc + j))
    out_blk = pl.BlockSpec((tc, cb), lambda j, s: (rev(s), j))
    wblk = pl.BlockSpec((width, cb), lambda j, s: (0, j))
    assert dp.shape[2] == d_sc and slab0 % 3 == 0
    operands = [p, p, p, p, p, dyp, conv_w]
    body, in_specs, operands = _dep_args(
        body, [blk(0), blk(1), halo(1), blk(2), halo(2), out_blk, wblk], operands, dp)
    return pl.pallas_call(
        body,
        out_shape=[jax.ShapeDtypeStruct(dp.shape, dp.dtype), jax.ShapeDtypeStruct((width, d_sc), F32)],
        grid=(nc, nt),
        in_specs=in_specs,
        out_specs=[pl.BlockSpec((3, tc, cb), lambda j, s: (slab0 // 3, rev(s), j)), wblk],
        input_output_aliases={7: 0},
        scratch_shapes=[pltpu.VMEM((SUBLANE, cb), F32)],
        compiler_params=_cparams(("parallel", "arbitrary"), 30 * tc * cb * 4),
        name="sc_bwd",
    )(*operands)


def _merge_fwd(p, y_lru, y_sc, col0, tc):
    t, d = y_lru.shape
    cb = _pick(math.gcd(d, col0), 1024)
    nc = d // cb
    base = col0 // cb

    def body(gl_ref, gs_ref, yl_ref, ys_ref, o_ref):
        o_ref[...] = (_sigmoid(gl_ref[...]) * yl_ref[...] + _sigmoid(gs_ref[...]) * ys_ref[...]).astype(BF16)

    gate = lambda slab: pl.BlockSpec((tc, cb), lambda s, j, slab=slab: (s, base + slab * nc + j))
    blk = pl.BlockSpec((tc, cb), lambda s, j: (s, j))
    return pl.pallas_call(
        body,
        out_shape=jax.ShapeDtypeStruct((t, d), BF16),
        grid=(t // tc, nc),
        in_specs=[gate(0), gate(1), blk, blk],
        out_specs=blk,
        compiler_params=_cparams(("parallel", "parallel"), 2 * tc * cb * 20),
        name="merge_fwd",
    )(p, p, y_lru, y_sc)


def _merge_bwd(p, y_lru, y_sc, dmerged, col0, tc, n_slabs):
    t, d = y_lru.shape
    cb = _pick(math.gcd(d, col0), 1024)
    nc = d // cb
    base = col0 // cb

    def body(gl_ref, gs_ref, yl_ref, ys_ref, dm_ref, dp_ref, dyl_ref, dys_ref):
        dm = dm_ref[...]
        sl = _sigmoid(gl_ref[...])
        ss = _sigmoid(gs_ref[...])
        dp_ref[0] = (dm * yl_ref[...] * (sl * (1.0 - sl))).astype(BF16)
        dp_ref[1] = (dm * ys_ref[...] * (ss * (1.0 - ss))).astype(BF16)
        dyl_ref[...] = (dm * sl).astype(BF16)
        dys_ref[...] = (dm * ss).astype(BF16)

    gate = lambda slab: pl.BlockSpec((tc, cb), lambda s, j, slab=slab: (s, base + slab * nc + j))
    blk = pl.BlockSpec((tc, cb), lambda s, j: (s, j))
    act = jax.ShapeDtypeStruct((t, d), BF16)
    return pl.pallas_call(
        body,
        out_shape=[jax.ShapeDtypeStruct((n_slabs, t, cb), BF16), act, act],
        grid=(t // tc, nc),
        in_specs=[gate(0), gate(1), blk, blk, blk],
        out_specs=[pl.BlockSpec((2, tc, cb), lambda s, j: (j, s, 0)), blk, blk],
        compiler_params=_cparams(("parallel", "parallel"), 2 * tc * cb * 28),
        name="merge_bwd",
    )(p, p, y_lru, y_sc, dmerged)


def _ffn_act_fwd(up, conv_w, d_ff, cb, tc):
    t = up.shape[0]
    nc = d_ff // cb
    nt = t // tc
    width = conv_w.shape[0]

    def body(g_ref, v_ref, wg_ref, wv_ref, o_ref, halo_g, halo_v):
        @pl.when(pl.program_id(1) == 0)
        def _():
            halo_g[...] = jnp.zeros_like(halo_g)
            halo_v[...] = jnp.zeros_like(halo_v)

        g = g_ref[...]
        v = v_ref[...]
        ug = _conv_fwd(jnp.concatenate([halo_g[...], g], axis=0), wg_ref[...], width)
        uv = _conv_fwd(jnp.concatenate([halo_v[...], v], axis=0), wv_ref[...], width)
        halo_g[...] = g[tc - SUBLANE:, :]
        halo_v[...] = v[tc - SUBLANE:, :]
        o_ref[...] = (ug * _sigmoid(ug) * uv).astype(BF16)

    blk = lambda half: pl.BlockSpec((tc, cb), lambda j, s, half=half: (s, half * nc + j))
    wblk = lambda half: pl.BlockSpec((width, cb), lambda j, s, half=half: (0, half * nc + j))
    return pl.pallas_call(
        body,
        out_shape=jax.ShapeDtypeStruct((t, d_ff), BF16),
        grid=(nc, nt),
        in_specs=[blk(0), blk(1), wblk(0), wblk(1)],
        out_specs=pl.BlockSpec((tc, cb), lambda j, s: (s, j)),
        scratch_shapes=[pltpu.VMEM((SUBLANE, cb), F32), pltpu.VMEM((SUBLANE, cb), F32)],
        compiler_params=_cparams(("parallel", "arbitrary"), 24 * tc * cb * 4),
        name="ffn_act_fwd",
    )(up, up, conv_w, conv_w)


def _ffn_act_bwd(up, dact, conv_w, d_ff, cb, tc, dep=None):
    t = up.shape[0]
    nc = d_ff // cb
    nt = t // tc
    width = conv_w.shape[0]
    halo_blocks = tc // SUBLANE

    def body(g_ref, gh_ref, v_ref, vh_ref, da_ref, wg_ref, wv_ref,
             dup_ref, dwg_ref, dwv_ref, nxt_g, nxt_v):
        s = pl.program_id(1)

        @pl.when(s == 0)
        def _():
            nxt_g[...] = jnp.zeros_like(nxt_g)
            nxt_v[...] = jnp.zeros_like(nxt_v)
            dwg_ref[...] = jnp.zeros_like(dwg_ref)
            dwv_ref[...] = jnp.zeros_like(dwv_ref)

        keep = jnp.where(s == nt - 1, 0.0, 1.0)
        wg = wg_ref[...]
        wv = wv_ref[...]
        catg = jnp.concatenate([gh_ref[...] * keep, g_ref[...]], axis=0)
        catv = jnp.concatenate([vh_ref[...] * keep, v_ref[...]], axis=0)
        ug = _conv_fwd(catg, wg, width)
        uv = _conv_fwd(catv, wv, width)
        sg = _sigmoid(ug)
        da = da_ref[...]
        duv = da * (ug * sg)
        dup_ref[1] = _conv_bwd_input(jnp.concatenate([duv, nxt_v[...]], axis=0), wv, width).astype(BF16)
        nxt_v[...] = duv[0:SUBLANE, :]
        _conv_bwd_weight(dwv_ref, duv, catv, width)
        dug = da * uv * (sg * (1.0 + ug * (1.0 - sg)))
        dup_ref[0] = _conv_bwd_input(jnp.concatenate([dug, nxt_g[...]], axis=0), wg, width).astype(BF16)
        nxt_g[...] = dug[0:SUBLANE, :]
        _conv_bwd_weight(dwg_ref, dug, catg, width)

    rev = lambda s: nt - 1 - s
    blk = lambda half: pl.BlockSpec((tc, cb), lambda j, s, half=half: (rev(s), half * nc + j))
    halo = lambda half: pl.BlockSpec(
        (SUBLANE, cb), lambda j, s, half=half: (jnp.maximum(rev(s) * halo_blocks - 1, 0), half * nc + j))
    wblk = lambda half: pl.BlockSpec((width, cb), lambda j, s, half=half: (0, half * nc + j))
    out_blk = pl.BlockSpec((tc, cb), lambda j, s: (rev(s), j))
    wout = pl.BlockSpec((width, cb), lambda j, s: (0, j))
    act = jax.ShapeDtypeStruct((t, d_ff), BF16)
    wshape = jax.ShapeDtypeStruct((width, d_ff), F32)
    body, in_specs, operands = _dep_args(
        body, [blk(0), halo(0), blk(1), halo(1), out_blk, wblk(0), wblk(1)],
        [up, up, up, up, dact, conv_w, conv_w], dep)
    return pl.pallas_call(
        body,
        out_shape=[jax.ShapeDtypeStruct((2, t, d_ff), BF16), wshape, wshape],
        grid=(nc, nt),
        in_specs=in_specs,
        out_specs=[pl.BlockSpec((2, tc, cb), lambda j, s: (0, rev(s), j)), wout, wout],
        scratch_shapes=[pltpu.VMEM((SUBLANE, cb), F32), pltpu.VMEM((SUBLANE, cb), F32)],
        compiler_params=_cparams(("parallel", "arbitrary"), 40 * tc * cb * 4),
        name="ffn_act_bwd",
    )(*operands)


def _mesh_pos():
    x, y, c = lax.axis_index("x"), lax.axis_index("y"), lax.axis_index("c")
    return x, y, c


def _other_chips(x, y):
    return [(1 - x, y), (x, 1 - y), (1 - x, 1 - y)]


def _cast_place(w, chip, col_sharded, name, dep=None):
    r, cdim = w.shape
    full = (r, cdim * N_CHIPS) if col_sharded else (r * N_CHIPS, cdim)
    rb = _pick(r, max(BF16_ROWS, (512 * 1024) // cdim), BF16_ROWS)
    nb = r // rb

    def body(chip_ref, w_ref, o_ref):
        o_ref[...] = w_ref[...].astype(BF16)

    if col_sharded:
        out_map = lambda i, chip_ref: (i, chip_ref[0])
    else:
        out_map = lambda i, chip_ref: (chip_ref[0] * nb + i, 0)
    grid_spec = pltpu.PrefetchScalarGridSpec(
        num_scalar_prefetch=1,
        grid=(nb,),
        in_specs=[pl.BlockSpec((rb, cdim), lambda i, chip_ref: (i, 0))] + ([ANY] if dep is not None else []),
        out_specs=pl.BlockSpec((rb, cdim), out_map),
    )
    body, _, operands = _dep_args(body, [], [chip, w], dep)
    return pl.pallas_call(
        body,
        out_shape=jax.ShapeDtypeStruct(full, BF16),
        grid_spec=grid_spec,
        compiler_params=_cparams(("parallel",), 2 * rb * cdim * 6),
        name=name,
    )(*operands)


def _remote(src, dst, send_sems, recv_sems, idx, to):
    return pltpu.make_async_remote_copy(
        src_ref=src, dst_ref=dst, send_sem=send_sems.at[idx], recv_sem=recv_sems.at[idx],
        device_id=to, device_id_type=MESH)


def _exchange(name, arrays, n_sems, plan):
    n = len(arrays)

    def body(*refs):
        bufs = refs[n:2 * n]
        send_sems, recv_sems = refs[2 * n:]
        sends, arrivals = plan(bufs, send_sems, recv_sems)
        for cp in sends:
            cp.start()
        for cp in arrivals:
            cp.wait_recv()
        for cp in sends:
            cp.wait_send()

    outs = pl.pallas_call(
        body,
        out_shape=[jax.ShapeDtypeStruct(a.shape, a.dtype) for a in arrays],
        in_specs=[ANY] * n,
        out_specs=[ANY] * n,
        input_output_aliases={k: k for k in range(n)},
        scratch_shapes=[pltpu.SemaphoreType.DMA((n_sems,)), pltpu.SemaphoreType.DMA((n_sems,))],
        name=name,
    )(*arrays)
    return list(outs)


def _exchange_start(name, arrays, n_sems, plan, after=None):
    n = len(arrays)
    n_in = n + (after is not None)

    def body(*refs):
        bufs = refs[:n]
        send_sems, recv_sems = refs[n_in], refs[n_in + 1]
        token = refs[-1]
        sends, _ = plan(bufs, send_sems, recv_sems)
        for cp in sends:
            cp.start()
        token[...] = jnp.zeros_like(token)

    out = pl.pallas_call(
        body,
        out_shape=(pltpu.SemaphoreType.DMA((n_sems,)), pltpu.SemaphoreType.DMA((n_sems,)),
                   *[pltpu.HBM(a.shape, a.dtype) for a in arrays],
                   jax.ShapeDtypeStruct((SUBLANE, LANE), F32)),
        in_specs=[HBM_SPEC] * n + [ANY] * (n_in - n),
        out_specs=(SEM_SPEC, SEM_SPEC, *[HBM_SPEC] * n, VMEM_SPEC),
        input_output_aliases={k: 2 + k for k in range(n)},
        compiler_params=pltpu.CompilerParams(has_side_effects=DATAFLOW_EFFECT),
        name=name,
    )(*[pltpu.with_memory_space_constraint(a, pltpu.HBM) for a in arrays], *([after] if after is not None else []))
    return out[0], out[1], list(out[2:2 + n]), out[-1]


def _exchange_wait(name, arrays, send_sems, recv_sems, after, plan):
    n = len(arrays)

    def body(*refs):
        bufs = refs[:n]
        sends, arrivals = plan(bufs, refs[n], refs[n + 1])
        for cp in arrivals:
            cp.wait_recv()
        for cp in sends:
            cp.wait_send()

    outs = pl.pallas_call(
        body,
        out_shape=[pltpu.HBM(a.shape, a.dtype) for a in arrays],
        in_specs=[HBM_SPEC] * n + [SEM_SPEC, SEM_SPEC, ANY],
        out_specs=[HBM_SPEC] * n,
        input_output_aliases={k: k for k in range(n)},
        compiler_params=pltpu.CompilerParams(has_side_effects=DATAFLOW_EFFECT),
        name=name,
    )(*arrays, send_sems, recv_sems, after)
    return list(outs)


def _half_block(ref, shard_shape, col_sharded, chip, half):
    r, cdim = shard_shape
    h = r // 2
    if col_sharded:
        return ref.at[pl.ds(pl.multiple_of(half * h, BF16_ROWS), h),
                      pl.ds(pl.multiple_of(chip * cdim, LANE), cdim)]
    return ref.at[pl.ds(pl.multiple_of(chip * r + half * h, BF16_ROWS), h), :]


def _gather_plan(shard_shapes, col_sharded, ks):
    def plan(bufs, send_sems, recv_sems):
        x, y, c = _mesh_pos()
        sends, arrivals = [], []
        for ref, k in zip(bufs, ks):
            mine = _half_block(ref, shard_shapes[k], col_sharded[k], 2 * x + y, c)
            for j, (px, py) in enumerate(_other_chips(x, y)):
                landed = _half_block(ref, shard_shapes[k], col_sharded[k], 2 * px + py, c)
                sends.append(_remote(mine, mine, send_sems, recv_sems, 3 * k + j, (px, py, c)))
                arrivals.append(_remote(landed, landed, send_sems, recv_sems, 3 * k + j, (px, py, c)))
        return sends, arrivals
    return plan


def _forward_plan(shard_shapes, col_sharded, ks):
    def plan(bufs, send_sems, recv_sems):
        x, y, c = _mesh_pos()
        sends, arrivals = [], []
        for i, (ref, k) in enumerate(zip(bufs, ks)):
            for j, (px, py) in enumerate(_other_chips(x, y)):
                landed = _half_block(ref, shard_shapes[k], col_sharded[k], 2 * px + py, c)
                theirs = _half_block(ref, shard_shapes[k], col_sharded[k], 2 * px + py, 1 - c)
                sends.append(_remote(landed, landed, send_sems, recv_sems, 3 * i + j, (x, y, 1 - c)))
                arrivals.append(_remote(theirs, theirs, send_sems, recv_sems, 3 * i + j, (x, y, 1 - c)))
        return sends, arrivals
    return plan


def _small_gather(small):
    def body(small_ref, out_ref, send_sems, recv_sems):
        x, y, c = _mesh_pos()
        me = 2 * x + y
        out_ref[me] = small_ref[...]
        copies = []
        for j, (px, py) in enumerate(_other_chips(x, y)):
            cp = _remote(small_ref, out_ref.at[me], send_sems, recv_sems, j, (px, py, c))
            cp.start()
            copies.append(cp)
        for j, (px, py) in enumerate(_other_chips(x, y)):
            _remote(small_ref, out_ref.at[2 * px + py], send_sems, recv_sems, j, (px, py, c)).wait_recv()
        for cp in copies:
            cp.wait_send()

    return pl.pallas_call(
        body,
        out_shape=jax.ShapeDtypeStruct((N_CHIPS,) + small.shape, small.dtype),
        in_specs=[VMEM_SPEC],
        out_specs=VMEM_SPEC,
        scratch_shapes=[pltpu.SemaphoreType.DMA((N_CHIPS - 1,)), pltpu.SemaphoreType.DMA((N_CHIPS - 1,))],
        name="gather_small",
    )(small)


def _as3d(g, col_sharded):
    r, cdim = g.shape
    return g.reshape(1, r, cdim) if col_sharded else g.reshape(N_CHIPS, r // N_CHIPS, cdim)


def _pair_plan(m):
    def plan(bufs, send_sems, recv_sems):
        x, y, c = _mesh_pos()
        copies = []
        for i in range(m):
            h = bufs[i].shape[1] // 2
            src = bufs[i].at[:, pl.ds(pl.multiple_of((1 - c) * h, BF16_ROWS), h), :]
            copies.append(_remote(src, bufs[m + i], send_sems, recv_sems, i, (x, y, 1 - c)))
        return copies, copies
    return plan


def _chip_plan(col_flags):
    m = len(col_flags)

    def plan(bufs, send_sems, recv_sems):
        x, y, c = _mesh_pos()
        copies = []
        for i in range(m):
            land = bufs[m + i]
            width = land.shape[2]
            for j, (px, py) in enumerate(_other_chips(x, y)):
                q = 2 * px + py
                if col_flags[i]:
                    src = bufs[i].at[0, :, pl.ds(pl.multiple_of(q * width, LANE), width)]
                else:
                    src = bufs[i].at[q]
                copies.append(_remote(src, land.at[j], send_sems, recv_sems, 3 * i + j, (px, py, c)))
        return copies, copies
    return plan


def _share_plan(m):
    def plan(bufs, send_sems, recv_sems):
        x, y, c = _mesh_pos()
        sends, arrivals = [], []
        for i in range(m):
            h = bufs[i].shape[0] // 2
            mine = bufs[i].at[pl.ds(pl.multiple_of(c * h, SUBLANE), h), :]
            theirs = bufs[i].at[pl.ds(pl.multiple_of((1 - c) * h, SUBLANE), h), :]
            sends.append(_remote(mine, mine, send_sems, recv_sems, i, (x, y, 1 - c)))
            arrivals.append(_remote(theirs, theirs, send_sems, recv_sems, i, (x, y, 1 - c)))
        return sends, arrivals
    return plan


def _pair_add(g3, other, core):
    a, r, cdim = g3.shape
    h = r // 2
    rb = _pick(h, max(BF16_ROWS, (512 * 1024) // cdim), BF16_ROWS)
    nb = h // rb

    def body(core_ref, g_ref, o_ref, out_ref):
        out_ref[...] = (g_ref[...].astype(F32) + o_ref[...].astype(F32)).astype(BF16)

    grid_spec = pltpu.PrefetchScalarGridSpec(
        num_scalar_prefetch=1,
        grid=(a, nb),
        in_specs=[pl.BlockSpec((None, rb, cdim), lambda i, j, core_ref: (i, core_ref[0] * nb + j, 0)),
                  pl.BlockSpec((None, rb, cdim), lambda i, j, core_ref: (i, j, 0))],
        out_specs=pl.BlockSpec((None, rb, cdim), lambda i, j, core_ref: (i, j, 0)),
    )
    return pl.pallas_call(
        body,
        out_shape=jax.ShapeDtypeStruct((a, h, cdim), BF16),
        grid_spec=grid_spec,
        compiler_params=_cparams(("parallel", "parallel"), 2 * rb * cdim * 10),
        name="grad_pair_add",
    )(core, g3, other)


def _small_allreduce(small):
    rows = small.shape[0]
    pad = (-rows) % (2 * SUBLANE)
    if pad:
        small = jnp.pad(small, ((0, pad), (0, 0)))
    h = small.shape[0] // 2
    half_shape = (h, small.shape[1])

    def body(small_ref, out_ref, theirs, by_chip, send_sems, recv_sems):
        x, y, c = _mesh_pos()
        me = 2 * x + y
        sibling = (x, y, 1 - c)
        mine = pl.ds(pl.multiple_of(c * h, SUBLANE), h)
        other = pl.ds(pl.multiple_of((1 - c) * h, SUBLANE), h)
        swap = _remote(small_ref, theirs, send_sems, recv_sems, 0, sibling)
        swap.start()
        swap.wait()
        by_chip[me] = small_ref[mine, :] + theirs[mine, :]
        copies = []
        for j, (px, py) in enumerate(_other_chips(x, y)):
            cp = _remote(by_chip.at[me], by_chip.at[me], send_sems, recv_sems, 1 + j, (px, py, c))
            cp.start()
            copies.append(cp)
        for j, (px, py) in enumerate(_other_chips(x, y)):
            landed = by_chip.at[2 * px + py]
            _remote(landed, landed, send_sems, recv_sems, 1 + j, (px, py, c)).wait_recv()
        total = by_chip[0]
        for q in range(1, N_CHIPS):
            total = total + by_chip[q]
        out_ref[mine, :] = total
        for cp in copies:
            cp.wait_send()
        share = _remote(out_ref.at[mine, :], out_ref.at[mine, :], send_sems, recv_sems, 4, sibling)
        share.start()
        _remote(out_ref.at[other, :], out_ref.at[other, :], send_sems, recv_sems, 4, sibling).wait_recv()
        share.wait_send()

    out = pl.pallas_call(
        body,
        out_shape=jax.ShapeDtypeStruct(small.shape, F32),
        in_specs=[VMEM_SPEC],
        out_specs=VMEM_SPEC,
        scratch_shapes=[pltpu.VMEM(small.shape, F32), pltpu.VMEM((N_CHIPS,) + half_shape, F32),
                        pltpu.SemaphoreType.DMA((5,)), pltpu.SemaphoreType.DMA((5,))],
        compiler_params=pltpu.CompilerParams(
            vmem_limit_bytes=min(VMEM_BUDGET, 8 * _nbytes(small.shape, F32) + (8 << 20))),
        name="grad_small_allreduce",
    )(small)
    return out[:rows]


def _chip_sum(partial, land, where, col_sharded):
    _, h, cdim = land.shape
    rb = _pick(h, max(BF16_ROWS, (512 * 1024) // cdim), BF16_ROWS)
    nb = h // rb

    def body(where_ref, own_ref, l_ref, o_ref):
        total = own_ref[...].astype(F32)
        for j in range(N_CHIPS - 1):
            total = total + l_ref[j].astype(F32)
        o_ref[...] = total

    if col_sharded:
        own_map = lambda i, w: (0, i, w[0])
    else:
        own_map = lambda i, w: (w[0], i, 0)
    grid_spec = pltpu.PrefetchScalarGridSpec(
        num_scalar_prefetch=1,
        grid=(nb,),
        in_specs=[pl.BlockSpec((None, rb, cdim), own_map),
                  pl.BlockSpec((N_CHIPS - 1, rb, cdim), lambda i, w: (0, i, 0))],
        out_specs=pl.BlockSpec((rb, cdim), lambda i, w: (w[1] * nb + i, 0)),
    )
    return pl.pallas_call(
        body,
        out_shape=jax.ShapeDtypeStruct((2 * h, cdim), F32),
        grid_spec=grid_spec,
        compiler_params=_cparams(("parallel",), 2 * rb * cdim * 12),
        name="grad_chip_sum",
    )(where, partial, land)


def _adamw(w, g, m, v, name, dep=None):
    r, cdim = w.shape
    rb = _pick(r, max(SUBLANE, (256 * 1024) // cdim), SUBLANE)
    c1 = 1.0 - ADAM_B1 ** ADAM_STEP
    c2 = 1.0 - ADAM_B2 ** ADAM_STEP

    def body(w_ref, g_ref, m_ref, v_ref, go_ref, d_ref, mo_ref, vo_ref):
        gv = g_ref[...]
        mn = ADAM_B1 * m_ref[...] + (1.0 - ADAM_B1) * gv
        vn = ADAM_B2 * v_ref[...] + (1.0 - ADAM_B2) * (gv * gv)
        m_hat = mn / c1
        v_hat = vn / c2
        d_ref[...] = -ADAM_LR * (m_hat / (jnp.sqrt(v_hat) + ADAM_EPS) + ADAM_WD * w_ref[...])
        go_ref[...] = gv
        mo_ref[...] = mn
        vo_ref[...] = vn

    blk = pl.BlockSpec((rb, cdim), lambda i: (i, 0))
    shape = jax.ShapeDtypeStruct((r, cdim), F32)
    body, in_specs, operands = _dep_args(body, [blk] * 4, [w, g, m, v], dep)
    return pl.pallas_call(
        body,
        out_shape=[shape] * 4,
        grid=(r // rb,),
        in_specs=in_specs,
        out_specs=[blk] * 4,
        compiler_params=_cparams(("parallel",), 2 * rb * cdim * 4 * 8),
        name=name,
    )(*operands)


def _pack(arrays):
    tile = SUBLANE * LANE
    pieces = []
    for arr in arrays:
        flat = arr.reshape(-1)
        pad = (-flat.shape[0]) % tile
        if pad:
            flat = jnp.concatenate([flat, jnp.zeros((pad,), flat.dtype)])
        pieces.append(flat)
    return jnp.concatenate(pieces).reshape(-1, LANE)


def _unpack(packed, shapes):
    tile = SUBLANE * LANE
    flat = packed.reshape(-1)
    out, off = [], 0
    for shp in shapes:
        size = math.prod(shp)
        out.append(flat[off:off + size].reshape(shp))
        off += size + ((-size) % tile)
    return out


def _block_diag_groups(w, per_group):
    hcount, hd, _ = w.shape
    ng = hcount // per_group
    w4 = w.reshape(ng, per_group, hd, hd)
    eye = jnp.eye(per_group, dtype=w.dtype)
    bd = w4[:, :, :, None, :] * eye[None, :, None, :, None]
    return bd.reshape(ng, per_group * hd, per_group * hd).astype(BF16)


def _diag_blocks(wbd, per_group, hd):
    ng = wbd.shape[0]
    w5 = wbd.reshape(ng, per_group, hd, per_group, hd)
    blocks = [w5[:, i, :, i, :] for i in range(per_group)]
    return jnp.stack(blocks, axis=1).reshape(ng * per_group, hd, hd)


def kernel(x, g_mix, w_in, lru_conv_w, lru_conv_b, lru_wa, lru_ba, lru_wx, lru_bx, lru_lambda, lru_w_out, sc_conv_w, sc_w_out, w_o, g_ffn, ffn_w_up, ffn_conv_w, ffn_w_down, g_final, loss_target, m_g_mix, m_w_in, m_lru_conv_w, m_lru_conv_b, m_lru_wa, m_lru_ba, m_lru_wx, m_lru_bx, m_lru_lambda, m_lru_w_out, m_sc_conv_w, m_sc_w_out, m_w_o, m_g_ffn, m_ffn_w_up, m_ffn_conv_w, m_ffn_w_down, m_g_final, v_g_mix, v_w_in, v_lru_conv_w, v_lru_conv_b, v_lru_wa, v_lru_ba, v_lru_wx, v_lru_bx, v_lru_lambda, v_lru_w_out, v_sc_conv_w, v_sc_w_out, v_w_o, v_g_ffn, v_ffn_w_up, v_ffn_conv_w, v_ffn_w_down, v_g_final):
    seq, d_model = x.shape[1], x.shape[2]
    heads, head_dim, _ = lru_wa.shape
    d_lru = heads * head_dim
    d_sc = sc_w_out.shape[0]
    d_ff = ffn_w_down.shape[0] * N_CHIPS
    assert x.shape[0] == 1 and w_in.shape[1] * N_CHIPS == 2 * d_lru + 3 * d_sc + 2 * d_model
    xs = x.reshape(seq, d_model)
    target = loss_target.reshape(seq, d_model)

    chip = 2 * lax.axis_index("x") + lax.axis_index("y")
    core = lax.axis_index("c").astype(jnp.int32).reshape(1)

    big_w = [w_in, lru_w_out, sc_w_out, w_o, ffn_w_up, ffn_w_down]
    big_m = [m_w_in, m_lru_w_out, m_sc_w_out, m_w_o, m_ffn_w_up, m_ffn_w_down]
    big_v = [v_w_in, v_lru_w_out, v_sc_w_out, v_w_o, v_ffn_w_up, v_ffn_w_down]
    col_sharded = [True, True, True, False, True, False]
    conv_shards = [lru_conv_w, sc_conv_w, ffn_conv_w]
    conv_pack = jnp.concatenate(
        [jnp.pad(w, ((0, SUBLANE - w.shape[0]), (0, 0))) for w in conv_shards], axis=1)
    big_names = ["w_in", "lru_w_out", "sc_w_out", "w_o", "ffn_w_up", "ffn_w_down"]
    chip_arr = chip.astype(jnp.int32).reshape(1)
    placed = [_cast_place(big_w[0], chip_arr, col_sharded[0], "cast_" + big_names[0])]
    conv_all = _small_gather(conv_pack)
    shard_shapes = [w.shape for w in big_w]
    n_big = len(big_w)

    def gather_start(ks, after, tag):
        send, recv, bufs, token = _exchange_start(
            "gather_start_" + tag, [placed[k] for k in ks], 3 * n_big,
            _gather_plan(shard_shapes, col_sharded, ks), after=after)
        return (send, recv, dict(zip(ks, bufs))), token

    def arrived(state, ks, after, tag):
        send, recv, bufs = state
        got = _exchange_wait("gather_wait_" + tag, [bufs[k] for k in ks], send, recv, after,
                             _gather_plan(shard_shapes, col_sharded, ks))
        return _exchange("gather_forward_" + tag, got, 3 * len(ks), _forward_plan(shard_shapes, col_sharded, ks))

    conv_full, off = [], 0
    for w in conv_shards:
        kw, nq = w.shape
        piece = conv_all[:, :kw, off:off + nq]
        conv_full.append(piece.transpose(1, 0, 2).reshape(kw, N_CHIPS * nq))
        off += nq
    lcw, scw, fcw = conv_full

    per_group = max(1, min(heads, 256 // head_dim))
    gc = per_group * head_dim
    wa_bd = _block_diag_groups(lru_wa, per_group)
    wx_bd = _block_diag_groups(lru_wx, per_group)
    tc = _pick(seq, 256, SUBLANE)
    cb_sc = _pick(d_sc, 512)
    cb_ff = _pick(d_ff, 512)
    col_sc = 2 * d_lru
    col_gates = 2 * d_lru + 3 * d_sc

    first, token = gather_start([0], conv_all, "in")
    for k in range(1, n_big):
        placed.append(_cast_place(big_w[k], chip_arr, col_sharded[k], "cast_" + big_names[k], dep=token))
        token = placed[-1]
    h1 = _rms_fwd(xs, g_mix, "rms_mix", dep=token)
    (win_b,) = arrived(first, [0], h1, "in")
    rest, token = gather_start([1, 2, 3, 4, 5], win_b, "rest")
    p = _mm(h1, win_b, "nn", F32, name="mm_in", dep=token)
    wlo_b, wso_b, wo_b = arrived(rest, [1, 2, 3], p, "mix")
    y_lru_pre, hseq = _lru_fwd(p, lcw, lru_conv_b, wa_bd, lru_ba, wx_bd, lru_bx, lru_lambda, d_lru, gc, tc)
    y_sc_pre = _sc_fwd(p, scw, col_sc, d_sc, cb_sc, tc)
    y_lru = _mm(y_lru_pre, wlo_b, "nn", BF16, name="mm_lru_out")
    y_sc = _mm(y_sc_pre, wso_b, "nn", BF16, name="mm_sc_out")
    merged = _merge_fwd(p, y_lru, y_sc, col_gates, tc)
    x2 = _mm(merged, wo_b, "nn", F32, res=xs, name="mm_o")
    (wup_b,) = arrived(rest, [4], x2, "up")
    h2 = _rms_fwd(x2, g_ffn, "rms_ffn")
    up = _mm(h2, wup_b, "nn", F32, name="mm_up")
    (wdn_b,) = arrived(rest, [5], up, "down")
    act = _ffn_act_fwd(up, fcw, d_ff, cb_ff, tc)
    x3 = _mm(act, wdn_b, "nn", F32, res=x2, name="mm_down")
    loss_part, dx3, dx3b, dg_final = _loss_head(x3, g_final, target)

    where = jnp.concatenate([chip_arr, core])

    def reduce_start(grads, flags, tag):
        views = [_as3d(g, cs) for g, cs in zip(grads, flags)]
        lands = [lax.empty((v.shape[0], v.shape[1] // 2, v.shape[2]), v.dtype) for v in views]
        send, recv, bufs, token = _exchange_start("grad_pair_start_" + tag, views + lands, len(views),
                                                  _pair_plan(len(views)))
        return (send, recv, bufs, flags, tag), token

    def reduce_mid(state, after):
        send, recv, bufs, flags, tag = state
        m = len(flags)
        bufs = _exchange_wait("grad_pair_wait_" + tag, bufs, send, recv, after, _pair_plan(m))
        partials = [_pair_add(bufs[i], bufs[m + i], core) for i in range(m)]
        lands = []
        for pz, cs in zip(partials, flags):
            _, h, cdim = pz.shape
            lands.append(lax.empty((N_CHIPS - 1, h, cdim // N_CHIPS if cs else cdim), BF16))
        send, recv, bufs, token = _exchange_start("grad_chip_start_" + tag, partials + lands, 3 * m,
                                                  _chip_plan(flags))
        return (send, recv, bufs, flags, tag), token

    def reduce_end(state, after):
        send, recv, bufs, flags, tag = state
        m = len(flags)
        bufs = _exchange_wait("grad_chip_wait_" + tag, bufs, send, recv, after, _chip_plan(flags))
        return [_chip_sum(bufs[i], bufs[m + i], where, flags[i]) for i in range(m)]

    g_wdn = _mm(act, dx3b, "tn", BF16, name="mm_down_dw")
    red_down, token = reduce_start([g_wdn], [False], "down")
    dact = _mm(dx3b, wdn_b, "nt", F32, name="mm_down_dx", dep=token)
    red_down, token = reduce_mid(red_down, dact)
    dup, dfcw_g, dfcw_v = _ffn_act_bwd(up, dact, fcw, d_ff, cb_ff, tc, dep=token)
    g_wup = _mm(h2, dup, "tn", BF16, name="mm_up_dw", slabs=[0, 1])
    red_up, token = reduce_start([g_wup], [True], "up")
    dh2 = _mm(dup, wup_b, "nt", F32, name="mm_up_dx", dep=token, slabs=[0, 1])
    red_up, token = reduce_mid(red_up, dh2)
    dx2, dx2b, dg_ffn = _rms_bwd(x2, g_ffn, dh2, dx3, "rms_ffn_bwd", True, dep=token)
    g_wo = _mm(merged, dx2b, "tn", BF16, name="mm_o_dw")
    dmerged = _mm(dx2b, wo_b, "nt", BF16, name="mm_o_dx")
    slab_w = d_lru
    assert d_sc == slab_w and d_model % slab_w == 0 and col_gates % slab_w == 0
    n_gate = d_model // slab_w
    gate0 = col_gates // slab_w
    dp_slabs = [gate0 + kind * n_gate + j for j in range(n_gate) for kind in (0, 1)] + [0, 1, 2, 3, 4]
    dp, dyl, dys = _merge_bwd(p, y_lru, y_sc, dmerged, col_gates, tc, len(dp_slabs))
    assert dp.shape[2] == slab_w
    g_wlo = _mm(y_lru_pre, dyl, "tn", BF16, name="mm_lru_out_dw")
    g_wso = _mm(y_sc_pre, dys, "tn", BF16, name="mm_sc_out_dw")
    red_mix, token = reduce_start([g_wlo, g_wso, g_wo], [True, True, False], "mix")
    dylp = _mm(dyl, wlo_b, "nt", F32, name="mm_lru_out_dx", dep=token)
    dysp = _mm(dys, wso_b, "nt", F32, name="mm_sc_out_dx")
    red_mix, token = reduce_mid(red_mix, dysp)
    dp, dlcw, dlcb, dwa_bd, dba, dwx_bd, dbx, dlam = _lru_bwd(
        p, hseq, dylp, lcw, lru_conv_b, wa_bd, lru_ba, wx_bd, lru_bx, lru_lambda, d_lru, gc, tc,
        dp, 2 * n_gate, dep=token)
    dp, dscw = _sc_bwd(p, dysp, scw, col_sc, d_sc, cb_sc, tc, dp, 2 * n_gate + 2)
    g_win = _mm(h1, dp, "tn", BF16, name="mm_in_dw", slabs=dp_slabs)
    red_in, token = reduce_start([g_win], [True], "in")
    dh1 = _mm(dp, win_b, "nt", F32, name="mm_in_dx", dep=token, slabs=dp_slabs)
    grad_x, dg_mix = _rms_bwd(xs, g_mix, dh1, dx2, "rms_mix_bwd", False)

    small_g = [dg_mix, dlcw, dlcb, _diag_blocks(dwa_bd, per_group, head_dim), dba,
               _diag_blocks(dwx_bd, per_group, head_dim), dbx, dlam, dscw, dg_ffn,
               jnp.concatenate([dfcw_g, dfcw_v], axis=1), dg_final]
    small_shapes = [a.shape for a in small_g]
    small_sum = _small_allreduce(_pack(small_g))
    red_in, token = reduce_mid(red_in, small_sum)
    (h_wdn,) = reduce_end(red_down, token)
    (h_wup,) = reduce_end(red_up, token)
    h_wlo, h_wso, h_wo = reduce_end(red_mix, token)
    s_wlo, s_wso, s_wo, s_wup, s_wdn = _exchange("grad_share_a", [h_wlo, h_wso, h_wo, h_wup, h_wdn], 5,
                                                 _share_plan(5))
    early = {1: s_wlo, 2: s_wso, 3: s_wo, 4: s_wup, 5: s_wdn}
    big_out = [None] * n_big
    last = None
    for k, g in early.items():
        big_out[k] = _adamw(big_w[k], g, big_m[k], big_v[k], "adamw_" + big_names[k], dep=last)
        last = big_out[k][1]
    (h_win,) = reduce_end(red_in, last)
    (s_win,) = _exchange("grad_share_b", [h_win], 1, _share_plan(1))
    big_out[0] = _adamw(big_w[0], s_win, big_m[0], big_v[0], "adamw_" + big_names[0])
    sg = _unpack(small_sum, small_shapes)
    for idx in (1, 8, 10):
        nq = sg[idx].shape[1] // N_CHIPS
        sg[idx] = lax.dynamic_slice_in_dim(sg[idx], chip * nq, nq, axis=1)
    small_w = [g_mix, lru_conv_w, lru_conv_b, lru_wa, lru_ba, lru_wx, lru_bx, lru_lambda, sc_conv_w,
               g_ffn, ffn_conv_w, g_final]
    small_m = [m_g_mix, m_lru_conv_w, m_lru_conv_b, m_lru_wa, m_lru_ba, m_lru_wx, m_lru_bx, m_lru_lambda,
               m_sc_conv_w, m_g_ffn, m_ffn_conv_w, m_g_final]
    small_v = [v_g_mix, v_lru_conv_w, v_lru_conv_b, v_lru_wa, v_lru_ba, v_lru_wx, v_lru_bx, v_lru_lambda,
               v_sc_conv_w, v_g_ffn, v_ffn_conv_w, v_g_final]
    sg = [g.reshape(w.shape) for g, w in zip(sg, small_w)]
    w_shapes = [w.shape for w in small_w]
    packed = _adamw(_pack(small_w), _pack(sg), _pack(small_m), _pack(small_v), "adamw_small")
    small_out = [_unpack(pk, w_shapes) for pk in packed]

    order = [(0, 0), (1, 0), (0, 1), (0, 2), (0, 3), (0, 4), (0, 5), (0, 6), (0, 7), (1, 1), (0, 8), (1, 2),
             (1, 3), (0, 9), (1, 4), (0, 10), (1, 5), (0, 11)]
    by_kind = []
    for kind in range(4):
        by_kind.append([big_out[i][kind] if is_big else small_out[kind][i] for is_big, i in order])
    loss = lax.psum(loss_part[0, 0], ("x", "y", "c"))
    return (loss, grad_x.reshape(x.shape), *by_kind[0], *by_kind[1], *by_kind[2], *by_kind[3])
```

```python
import math

import jax
import jax.numpy as jnp
from jax import lax
from jax.experimental import pallas as pl
from jax.experimental.pallas import tpu as pltpu

F32 = jnp.float32
BF16 = jnp.bfloat16

LANE = 128
SUBLANE = 8
BF16_ROWS = 16
VMEM_BYTES_V7X = 64 * 1024 * 1024
VMEM_BUDGET = VMEM_BYTES_V7X - 8 * 1024 * 1024
MM_VMEM_BUDGET = 42 * 1024 * 1024
EPS = 1e-6
LRU_C = 8.0
ADAM_LR = 0.001
ADAM_B1 = 0.9
ADAM_B2 = 0.999
ADAM_EPS = 1e-08
ADAM_WD = 0.01
ADAM_STEP = 10

N_CHIPS = 4
N_DEV = 8
MESH = pl.DeviceIdType.MESH
ANY = pl.BlockSpec(memory_space=pl.ANY)
VMEM_SPEC = pl.BlockSpec(memory_space=pltpu.VMEM)
HBM_SPEC = pl.BlockSpec(memory_space=pltpu.HBM)
SEM_SPEC = pl.BlockSpec(memory_space=pltpu.SEMAPHORE)
DATAFLOW_EFFECT = pltpu.SideEffectType.DATAFLOW_SIDE_EFFECTING


def _pick(n, cap, mult=LANE):
    best = None
    d = mult
    while d <= min(n, cap):
        if n % d == 0:
            best = d
        d += mult
    return n if best is None else best


def _cparams(semantics, block_bytes):
    limit = min(VMEM_BUDGET, max(32 * 1024 * 1024, int(block_bytes * 1.25) + (4 << 20)))
    return pltpu.CompilerParams(dimension_semantics=semantics, vmem_limit_bytes=limit)


def _nbytes(shape, dtype):
    return math.prod(shape) * jnp.dtype(dtype).itemsize


def _sigmoid(z):
    return 1.0 / (1.0 + jnp.exp(-z))


def _softplus(z):
    e = jnp.exp(-jnp.abs(z))
    u = 1.0 + e
    log1p = jnp.where(u == 1.0, e, jnp.log(u) * (e / (u - 1.0)))
    return jnp.maximum(z, 0.0) + log1p


def _neg_expm1(z):
    small = z * (1.0 + z * (0.5 + z * (1.0 / 6.0 + z * (1.0 / 24.0))))
    return -jnp.where(jnp.abs(z) < 0.03, small, jnp.exp(z) - 1.0)


_GELU_K = math.sqrt(2.0 / math.pi)
_GELU_C = 0.044715


def _gelu_and_grad(z):
    z2 = z * z
    th = jnp.tanh(_GELU_K * (z + _GELU_C * z2 * z))
    val = 0.5 * z * (1.0 + th)
    grad = 0.5 * (1.0 + th) + 0.5 * z * (1.0 - th * th) * (_GELU_K * (1.0 + 3.0 * _GELU_C * z2))
    return val, grad


def _rows_before(cat, k):
    if k == 0:
        return cat[SUBLANE:, :]
    return pltpu.roll(cat, k, 0)[SUBLANE:, :]


def _rows_after(cat, k):
    n = cat.shape[0]
    if k == 0:
        return cat[:n - SUBLANE, :]
    return pltpu.roll(cat, n - k, 0)[:n - SUBLANE, :]


def _conv_fwd(cat, w, width):
    y = _rows_before(cat, width - 1) * w[0:1, :]
    for k in range(1, width):
        y = y + _rows_before(cat, width - 1 - k) * w[k:k + 1, :]
    return y


def _conv_bwd_input(cat, w, width):
    dx = _rows_after(cat, width - 1) * w[0:1, :]
    for k in range(1, width):
        dx = dx + _rows_after(cat, width - 1 - k) * w[k:k + 1, :]
    return dx


def _conv_bwd_weight(dw_ref, dy, catx, width):
    for k in range(width):
        dw_ref[k:k + 1, :] += jnp.sum(dy * _rows_before(catx, width - 1 - k), axis=0, keepdims=True)


def _scan_tiles(a_ref, b_ref, out_ref, carry0, n_rows, reverse):
    cols = a_ref.shape[1]
    row = lax.broadcasted_iota(jnp.int32, (SUBLANE, cols), 0)
    n_tiles = n_rows // SUBLANE

    def step(j, carry):
        tile = (n_tiles - 1 - j) if reverse else j
        off = pl.multiple_of(tile * SUBLANE, SUBLANE)
        a = a_ref[pl.ds(off, SUBLANE), :]
        b = b_ref[pl.ds(off, SUBLANE), :]
        for s in (1, 2, 4):
            if reverse:
                keep = row < SUBLANE - s
                shift = SUBLANE - s
            else:
                keep = row >= s
                shift = s
            a_sh = jnp.where(keep, pltpu.roll(a, shift, 0), 1.0)
            b_sh = jnp.where(keep, pltpu.roll(b, shift, 0), 0.0)
            b = a * b_sh + b
            a = a * a_sh
        out = a * carry + b
        out_ref[pl.ds(off, SUBLANE), :] = out
        return out[0:1, :] if reverse else out[SUBLANE - 1:SUBLANE, :]

    return lax.fori_loop(0, n_tiles, step, carry0)


def _dep_args(body, in_specs, operands, *deps):
    deps = [d for d in deps if d is not None]
    if not deps:
        return body, in_specs, operands
    n = len(operands)

    def wrapped(*refs):
        return body(*refs[:n], *refs[n + len(deps):])

    return wrapped, list(in_specs) + [ANY] * len(deps), list(operands) + deps


def _mm(a, b, mode, out_dtype, res=None, name=None, dep=None, slabs=None):
    assert a.dtype == BF16 and b.dtype == BF16
    a_slabbed, b_slabbed = a.ndim == 3, b.ndim == 3
    assert not a_slabbed or (mode == "nt" and slabs is not None)
    assert not b_slabbed or (mode == "tn" and slabs is not None)
    if mode == "nn":
        (m, k), (k2, n) = a.shape, b.shape
        dims = (((1,), (0,)), ((), ()))
    elif mode == "nt":
        m, k = (a.shape[1], a.shape[0] * a.shape[2]) if a_slabbed else a.shape
        n, k2 = b.shape
        dims = (((1,), (1,)), ((), ()))
    else:
        k, m = a.shape
        k2, n = (b.shape[1], b.shape[0] * b.shape[2]) if b_slabbed else b.shape
        dims = (((0,), (0,)), ((), ()))
    assert k == k2
    n_unit = b.shape[2] if b_slabbed else n
    out_bytes = jnp.dtype(out_dtype).itemsize
    bm = _pick(m, 1024)
    bn = _pick(n_unit, 1024)
    bk = k

    def est(bm_, bn_, bk_):
        e = 2 * (bm_ * bk_ + bk_ * bn_) * 2 + 2 * bm_ * bn_ * out_bytes
        if k // bk_ > 1:
            e += bm_ * bn_ * 4
        if res is not None:
            e += 2 * bm_ * bn_ * 4
        return e

    for shrink_n, floor in ((True, 512), (False, 512), (True, 256), (False, 256)):
        while est(bm, bn, bk) > MM_VMEM_BUDGET:
            if shrink_n and bn > floor and bn % 2 == 0 and n_unit % (bn // 2) == 0:
                bn //= 2
            elif not shrink_n and bm > floor and bm % 2 == 0 and m % (bm // 2) == 0:
                bm //= 2
            else:
                break
    while (est(bm, bn, bk) > MM_VMEM_BUDGET and not a_slabbed and bk % (2 * LANE) == 0
           and k % (bk // 2) == 0):
        bk //= 2
    nk = k // bk
    per_slab = n_unit // bn

    def out_col(j):
        if not b_slabbed:
            return j
        s = j // per_slab
        where = sum(jnp.where(s == t, slabs[t], 0) for t in range(len(slabs)))
        return where * per_slab + j % per_slab

    if mode == "tn":
        a_spec = pl.BlockSpec((bk, bm), lambda i, j, kk: (kk, i))
    elif a_slabbed:
        a_spec = pl.BlockSpec((a.shape[0], bm, a.shape[2]), lambda i, j, kk: (0, i, 0))
    else:
        a_spec = pl.BlockSpec((bm, bk), lambda i, j, kk: (i, kk))
    if mode == "nt":
        b_spec = pl.BlockSpec((bn, bk), lambda i, j, kk: (j, kk))
    elif b_slabbed:
        b_spec = pl.BlockSpec((None, bk, bn), lambda i, j, kk: (j // per_slab, kk, j % per_slab))
    else:
        b_spec = pl.BlockSpec((bk, bn), lambda i, j, kk: (kk, j))
    o_spec = pl.BlockSpec((bm, bn), lambda i, j, kk: (i, out_col(j)))
    in_specs = [a_spec, b_spec]
    operands = [a, b]
    if res is not None:
        in_specs.append(o_spec)
        operands.append(res)
    has_res = res is not None

    def body(*refs):
        a_ref, b_ref = refs[0], refs[1]
        res_ref = refs[2] if has_res else None
        o_ref = refs[2 + has_res]
        if a_slabbed:
            width = a_ref.shape[2]
            part = None
            for s, col in enumerate(slabs):
                term = lax.dot_general(a_ref[s], b_ref[:, col * width:(col + 1) * width], dims,
                                       preferred_element_type=F32)
                part = term if part is None else part + term
        else:
            part = lax.dot_general(a_ref[...], b_ref[...], dims, preferred_element_type=F32)
        if nk == 1:
            if has_res:
                part = part + res_ref[...]
            o_ref[...] = part.astype(o_ref.dtype)
            return
        acc_ref = refs[-1]
        kk = pl.program_id(2)

        @pl.when(kk == 0)
        def _():
            acc_ref[...] = part

        @pl.when(kk > 0)
        def _():
            acc_ref[...] += part

        @pl.when(kk == nk - 1)
        def _():
            total = acc_ref[...]
            if has_res:
                total = total + res_ref[...]
            o_ref[...] = total.astype(o_ref.dtype)

    scratch = [pltpu.VMEM((bm, bn), F32)] if nk > 1 else []
    body, in_specs, operands = _dep_args(body, in_specs, operands, dep)
    return pl.pallas_call(
        body,
        out_shape=jax.ShapeDtypeStruct((m, n), out_dtype),
        grid=(m // bm, n // bn, nk),
        in_specs=in_specs,
        out_specs=o_spec,
        scratch_shapes=scratch,
        compiler_params=_cparams(("parallel", "parallel", "arbitrary"), est(bm, bn, bk)),
        name=name,
    )(*operands)


def _rms_fwd(x, g, name, dep=None):
    t, d = x.shape
    tb = _pick(t, 512, SUBLANE)

    def body(x_ref, g_ref, h_ref):
        xv = x_ref[...]
        r = lax.rsqrt(jnp.mean(xv * xv, axis=-1, keepdims=True) + EPS)
        h_ref[...] = ((xv * r) * g_ref[...]).astype(BF16)

    blk = pl.BlockSpec((tb, d), lambda i: (i, 0))
    body, in_specs, operands = _dep_args(
        body, [blk, pl.BlockSpec((1, d), lambda i: (0, 0))], [x, g.reshape(1, d)], dep)
    return pl.pallas_call(
        body,
        out_shape=jax.ShapeDtypeStruct((t, d), BF16),
        grid=(t // tb,),
        in_specs=in_specs,
        out_specs=blk,
        compiler_params=_cparams(("parallel",), 2 * tb * d * 6),
        name=name,
    )(*operands)


def _rms_bwd(x, g, dh, dres, name, want_bf16, dep=None):
    t, d = x.shape
    tb = _pick(t, 256, SUBLANE)

    def body(x_ref, g_ref, dh_ref, dres_ref, *outs):
        dx_ref, dg_ref = outs[0], outs[-1]
        xv = x_ref[...]
        r = lax.rsqrt(jnp.mean(xv * xv, axis=-1, keepdims=True) + EPS)
        xhat = xv * r
        dhv = dh_ref[...]
        dxhat = dhv * g_ref[...]
        dx = dres_ref[...] + r * (dxhat - xhat * jnp.mean(dxhat * xhat, axis=-1, keepdims=True))
        dx_ref[...] = dx
        if want_bf16:
            outs[1][...] = dx.astype(BF16)

        @pl.when(pl.program_id(0) == 0)
        def _():
            dg_ref[...] = jnp.zeros_like(dg_ref)

        dg_ref[...] += jnp.sum(dhv * xhat, axis=0, keepdims=True)

    blk = pl.BlockSpec((tb, d), lambda i: (i, 0))
    row = pl.BlockSpec((1, d), lambda i: (0, 0))
    out_shape = [jax.ShapeDtypeStruct((t, d), F32)]
    out_specs = [blk]
    if want_bf16:
        out_shape.append(jax.ShapeDtypeStruct((t, d), BF16))
        out_specs.append(blk)
    out_shape.append(jax.ShapeDtypeStruct((1, d), F32))
    out_specs.append(row)
    body, in_specs, operands = _dep_args(
        body, [blk, row, blk, blk], [x, g.reshape(1, d), dh, dres], dep)
    return pl.pallas_call(
        body,
        out_shape=out_shape,
        grid=(t // tb,),
        in_specs=in_specs,
        out_specs=out_specs,
        compiler_params=_cparams(("arbitrary",), 2 * tb * d * 18),
        name=name,
    )(*operands)


def _loss_head(x3, g, target):
    t, d = x3.shape
    tb = _pick(t, 256, SUBLANE)

    def body(x_ref, g_ref, t_ref, loss_ref, dx_ref, dxb_ref, dg_ref):
        xv = x_ref[...]
        gv = g_ref[...]
        r = lax.rsqrt(jnp.mean(xv * xv, axis=-1, keepdims=True) + EPS)
        xhat = xv * r
        err = xhat * gv - t_ref[...]
        dy = err * (1.0 / d)
        dxhat = dy * gv
        dx = r * (dxhat - xhat * jnp.mean(dxhat * xhat, axis=-1, keepdims=True))
        dx_ref[...] = dx
        dxb_ref[...] = dx.astype(BF16)

        @pl.when(pl.program_id(0) == 0)
        def _():
            dg_ref[...] = jnp.zeros_like(dg_ref)
            loss_ref[...] = jnp.zeros_like(loss_ref)

        dg_ref[...] += jnp.sum(dy * xhat, axis=0, keepdims=True)
        per_token = jnp.mean(err * err, axis=-1, keepdims=True)
        loss_ref[...] += 0.5 * jnp.sum(per_token, axis=0, keepdims=True)

    blk = pl.BlockSpec((tb, d), lambda i: (i, 0))
    row = pl.BlockSpec((1, d), lambda i: (0, 0))
    return pl.pallas_call(
        body,
        out_shape=[jax.ShapeDtypeStruct((1, 1), F32), jax.ShapeDtypeStruct((t, d), F32),
                   jax.ShapeDtypeStruct((t, d), BF16), jax.ShapeDtypeStruct((1, d), F32)],
        grid=(t // tb,),
        in_specs=[blk, row, blk],
        out_specs=[pl.BlockSpec((1, 1), lambda i: (0, 0)), blk, blk, row],
        compiler_params=_cparams(("arbitrary",), 2 * tb * d * 14),
        name="loss_head",
    )(x3, g.reshape(1, d), target)


def _lru_gates(xc, wa, ba, wx, bx, lam):
    nn = (((1,), (0,)), ((), ()))
    xcb = xc.astype(BF16)
    r = _sigmoid(lax.dot_general(xcb, wa, nn, preferred_element_type=F32) + ba)
    i = _sigmoid(lax.dot_general(xcb, wx, nn, preferred_element_type=F32) + bx)
    cl = -LRU_C * _softplus(-lam)
    log_a = cl * r
    a = jnp.exp(log_a)
    one_minus_a2 = _neg_expm1(2.0 * log_a)
    return xcb, r, i, a, one_minus_a2, cl


def _lru_fwd(p, conv_w, conv_b, wa_bd, ba, wx_bd, bx, lam, d_lru, gc, tc):
    t = p.shape[0]
    ng = d_lru // gc
    nt = t // tc
    width = conv_w.shape[0]

    def body(lx_ref, gate_ref, cw_ref, cb_ref, wa_ref, ba_ref, wx_ref, bx_ref, lam_ref,
             y_ref, h_ref, halo, hcar, a_s, u_s):
        @pl.when(pl.program_id(1) == 0)
        def _():
            halo[...] = jnp.zeros_like(halo)
            hcar[...] = jnp.zeros_like(hcar)

        x = lx_ref[...]
        cat = jnp.concatenate([halo[...], x], axis=0)
        halo[...] = x[tc - SUBLANE:, :]
        xc = _conv_fwd(cat, cw_ref[...], width) + cb_ref[...]
        _, r, i, a, om, _ = _lru_gates(xc, wa_ref[...], ba_ref[...], wx_ref[...], bx_ref[...], lam_ref[...])
        a_s[...] = a
        u_s[...] = jnp.sqrt(om) * (i * xc)
        hcar[0:1, :] = _scan_tiles(a_s, u_s, h_ref, hcar[0:1, :], tc, reverse=False)
        gl, _ = _gelu_and_grad(gate_ref[...])
        y_ref[...] = (gl * h_ref[...]).astype(BF16)

    blk = lambda off: pl.BlockSpec((tc, gc), lambda g, s, off=off: (s, off + g))
    rowv = lambda rows: pl.BlockSpec((rows, gc), lambda g, s: (0, g))
    wspec = pl.BlockSpec((None, gc, gc), lambda g, s: (g, 0, 0))
    out_blk = pl.BlockSpec((tc, gc), lambda g, s: (s, g))
    return pl.pallas_call(
        body,
        out_shape=[jax.ShapeDtypeStruct((t, d_lru), BF16), jax.ShapeDtypeStruct((t, d_lru), F32)],
        grid=(ng, nt),
        in_specs=[blk(0), blk(ng), rowv(width), rowv(1), wspec, rowv(1), wspec, rowv(1), rowv(1)],
        out_specs=[out_blk, out_blk],
        scratch_shapes=[pltpu.VMEM((SUBLANE, gc), F32), pltpu.VMEM((SUBLANE, gc), F32),
                        pltpu.VMEM((tc, gc), F32), pltpu.VMEM((tc, gc), F32)],
        compiler_params=_cparams(("parallel", "arbitrary"), 40 * tc * gc * 4),
        name="lru_fwd",
    )(p, p, conv_w, conv_b.reshape(1, -1), wa_bd, ba.reshape(1, -1), wx_bd, bx.reshape(1, -1),
      lam.reshape(1, -1))


def _lru_bwd(p, hseq, dyp, conv_w, conv_b, wa_bd, ba, wx_bd, bx, lam, d_lru, gc, tc, dp, slab0, dep=None):
    t = p.shape[0]
    ng = d_lru // gc
    nt = t // tc
    width = conv_w.shape[0]
    halo_blocks = tc // SUBLANE
    nn = (((1,), (0,)), ((), ()))
    nt_dims = (((1,), (1,)), ((), ()))
    tn_dims = (((0,), (0,)), ((), ()))

    def body(lx_ref, lxh_ref, gate_ref, h_ref, hh_ref, dyp_ref,
             cw_ref, cb_ref, wa_ref, ba_ref, wx_ref, bx_ref, lam_ref,
             dp_ref, dcw_ref, dcb_ref, dwa_ref, dba_ref, dwx_ref, dbx_ref, dlam_ref,
             nxt_dxc, nxt_a, nxt_g, al_s, b_s, g_s):
        s = pl.program_id(1)
        first_chunk = s == nt - 1

        @pl.when(s == 0)
        def _():
            nxt_dxc[...] = jnp.zeros_like(nxt_dxc)
            nxt_a[...] = jnp.zeros_like(nxt_a)
            nxt_g[...] = jnp.zeros_like(nxt_g)
            for ref in (dcw_ref, dcb_ref, dwa_ref, dba_ref, dwx_ref, dbx_ref, dlam_ref):
                ref[...] = jnp.zeros_like(ref)

        keep = jnp.where(first_chunk, 0.0, 1.0)
        x = lx_ref[...]
        catx = jnp.concatenate([lxh_ref[...] * keep, x], axis=0)
        cw = cw_ref[...]
        xc = _conv_fwd(catx, cw, width) + cb_ref[...]
        wa = wa_ref[...]
        wx = wx_ref[...]
        lam_v = lam_ref[...]
        xcb, r, i, a, om, cl = _lru_gates(xc, wa, ba_ref[...], wx, bx_ref[...], lam_v)
        mult = jnp.sqrt(om)

        h = h_ref[...]
        hprev = _rows_before(jnp.concatenate([hh_ref[...] * keep, h], axis=0), 1)
        gl, dgl = _gelu_and_grad(gate_ref[...])
        dyp_v = dyp_ref[...]
        dp_ref[1] = (dyp_v * h * dgl).astype(BF16)

        al_s[...] = _rows_after(jnp.concatenate([a, nxt_a[...]], axis=0), 1)
        b_s[...] = dyp_v * gl
        nxt_g[0:1, :] = _scan_tiles(al_s, b_s, g_s, nxt_g[0:1, :], tc, reverse=True)
        nxt_a[...] = a[0:SUBLANE, :]
        du = g_s[...]

        da = du * hprev
        dmult = du * (i * xc)
        di = du * mult * xc
        dxc = du * mult * i
        dlog_a = da * a - dmult * (a * a / mult)
        dlam_ref[...] += jnp.sum(dlog_a * r, axis=0, keepdims=True) * (LRU_C * _sigmoid(-lam_v))
        dza = (dlog_a * cl) * r * (1.0 - r)
        dzx = di * i * (1.0 - i)
        dba_ref[...] += jnp.sum(dza, axis=0, keepdims=True)
        dbx_ref[...] += jnp.sum(dzx, axis=0, keepdims=True)
        dzab = dza.astype(BF16)
        dzxb = dzx.astype(BF16)
        dwa_ref[...] += lax.dot_general(xcb, dzab, tn_dims, preferred_element_type=F32)
        dwx_ref[...] += lax.dot_general(xcb, dzxb, tn_dims, preferred_element_type=F32)
        dxc = dxc + lax.dot_general(dzab, wa, nt_dims, preferred_element_type=F32)
        dxc = dxc + lax.dot_general(dzxb, wx, nt_dims, preferred_element_type=F32)
        dcb_ref[...] += jnp.sum(dxc, axis=0, keepdims=True)
        _conv_bwd_weight(dcw_ref, dxc, catx, width)
        catd = jnp.concatenate([dxc, nxt_dxc[...]], axis=0)
        dp_ref[0] = _conv_bwd_input(catd, cw, width).astype(BF16)
        nxt_dxc[...] = dxc[0:SUBLANE, :]

    rev = lambda s: nt - 1 - s
    blk = lambda off: pl.BlockSpec((tc, gc), lambda g, s, off=off: (rev(s), off + g))
    halo = lambda off: pl.BlockSpec(
        (SUBLANE, gc), lambda g, s, off=off: (jnp.maximum(rev(s) * halo_blocks - 1, 0), off + g))
    rowv = lambda rows: pl.BlockSpec((rows, gc), lambda g, s: (0, g))
    wspec = pl.BlockSpec((None, gc, gc), lambda g, s: (g, 0, 0))
    out_blk = pl.BlockSpec((tc, gc), lambda g, s: (rev(s), g))
    vec = lambda rows: jax.ShapeDtypeStruct((rows, d_lru), F32)
    wshape = jax.ShapeDtypeStruct((ng, gc, gc), F32)
    body, in_specs, operands = _dep_args(
        body,
        [blk(0), halo(0), blk(ng), blk(0), halo(0), blk(0),
         rowv(width), rowv(1), wspec, rowv(1), wspec, rowv(1), rowv(1)],
        [p, p, p, hseq, hseq, dyp,
         conv_w, conv_b.reshape(1, -1), wa_bd, ba.reshape(1, -1), wx_bd,
         bx.reshape(1, -1), lam.reshape(1, -1)], dp, dep)
    assert dp.shape[2] == d_lru and slab0 % 2 == 0
    return pl.pallas_call(
        body,
        out_shape=[jax.ShapeDtypeStruct(dp.shape, dp.dtype),
                   vec(width), vec(1), wshape, vec(1), wshape, vec(1), vec(1)],
        grid=(ng, nt),
        in_specs=in_specs,
        out_specs=[pl.BlockSpec((2, tc, gc), lambda g, s: (slab0 // 2, rev(s), g)),
                   rowv(width), rowv(1), wspec, rowv(1), wspec, rowv(1), rowv(1)],
        input_output_aliases={13: 0},
        scratch_shapes=[pltpu.VMEM((SUBLANE, gc), F32), pltpu.VMEM((SUBLANE, gc), F32),
                        pltpu.VMEM((SUBLANE, gc), F32),
                        pltpu.VMEM((tc, gc), F32), pltpu.VMEM((tc, gc), F32), pltpu.VMEM((tc, gc), F32)],
        compiler_params=_cparams(("parallel", "arbitrary"), 80 * tc * gc * 4),
        name="lru_bwd",
    )(*operands)


def _sc_fwd(p, conv_w, col0, d_sc, cb, tc):
    t = p.shape[0]
    nc = d_sc // cb
    nt = t // tc
    width = conv_w.shape[0]
    base = col0 // cb

    def body(b_ref, c_ref, v_ref, w_ref, y_ref, halo):
        @pl.when(pl.program_id(1) == 0)
        def _():
            halo[...] = jnp.zeros_like(halo)

        cv = c_ref[...] * v_ref[...]
        cat = jnp.concatenate([halo[...], cv], axis=0)
        halo[...] = cv[tc - SUBLANE:, :]
        y_ref[...] = (b_ref[...] * _conv_fwd(cat, w_ref[...], width)).astype(BF16)

    blk = lambda slab: pl.BlockSpec((tc, cb), lambda j, s, slab=slab: (s, base + slab * nc + j))
    return pl.pallas_call(
        body,
        out_shape=jax.ShapeDtypeStruct((t, d_sc), BF16),
        grid=(nc, nt),
        in_specs=[blk(0), blk(1), blk(2), pl.BlockSpec((width, cb), lambda j, s: (0, j))],
        out_specs=pl.BlockSpec((tc, cb), lambda j, s: (s, j)),
        scratch_shapes=[pltpu.VMEM((SUBLANE, cb), F32)],
        compiler_params=_cparams(("parallel", "arbitrary"), 20 * tc * cb * 4),
        name="sc_fwd",
    )(p, p, p, conv_w)


def _sc_bwd(p, dyp, conv_w, col0, d_sc, cb, tc, dp, slab0):
    t = p.shape[0]
    nc = d_sc // cb
    nt = t // tc
    width = conv_w.shape[0]
    base = col0 // cb
    halo_blocks = tc // SUBLANE

    def body(b_ref, c_ref, ch_ref, v_ref, vh_ref, dyp_ref, w_ref,
             dp_ref, dw_ref, nxt_dq):
        s = pl.program_id(1)

        @pl.when(s == 0)
        def _():
            nxt_dq[...] = jnp.zeros_like(nxt_dq)
            dw_ref[...] = jnp.zeros_like(dw_ref)

        keep = jnp.where(s == nt - 1, 0.0, 1.0)
        cvals = c_ref[...]
        vvals = v_ref[...]
        w = w_ref[...]
        catcv = jnp.concatenate([ch_ref[...] * vh_ref[...] * keep, cvals * vvals], axis=0)
        q = _conv_fwd(catcv, w, width)
        dyp_v = dyp_ref[...]
        dp_ref[0] = (dyp_v * q).astype(BF16)
        dq = dyp_v * b_ref[...]
        _conv_bwd_weight(dw_ref, dq, catcv, width)
        dcv = _conv_bwd_input(jnp.concatenate([dq, nxt_dq[...]], axis=0), w, width)
        nxt_dq[...] = dq[0:SUBLANE, :]
        dp_ref[1] = (dcv * vvals).astype(BF16)
        dp_ref[2] = (dcv * cvals).astype(BF16)

    rev = lambda s: nt - 1 - s
    blk = lambda slab: pl.BlockSpec((tc, cb), lambda j, s, slab=slab: (rev(s), base + slab * nc + j))
    halo = lambda slab: pl.BlockSpec(
        (SUBLANE, cb),
        lambda j, s, slab=slab: (jnp.maximum(rev(s) * halo_blocks - 1, 0), base + slab * nc + j))
    out_blk = pl.BlockSpec((tc, cb), lambda j, s: (rev(s), j))
    wblk = pl.BlockSpec((width, cb), lambda j, s: (0, j))
    assert dp.shape[2] == d_sc and slab0 % 3 == 0
    operands = [p, p, p, p, p, dyp, conv_w]
    body, in_specs, operands = _dep_args(
        body, [blk(0), blk(1), halo(1), blk(2), halo(2), out_blk, wblk], operands, dp)
    return pl.pallas_call(
        body,
        out_shape=[jax.ShapeDtypeStruct(dp.shape, dp.dtype), jax.ShapeDtypeStruct((width, d_sc), F32)],
        grid=(nc, nt),
        in_specs=in_specs,
        out_specs=[pl.BlockSpec((3, tc, cb), lambda j, s: (slab0 // 3, rev(s), j)), wblk],
        input_output_aliases={7: 0},
        scratch_shapes=[pltpu.VMEM((SUBLANE, cb), F32)],
        compiler_params=_cparams(("parallel", "arbitrary"), 30 * tc * cb * 4),
        name="sc_bwd",
    )(*operands)


def _merge_fwd(p, y_lru, y_sc, col0, tc):
    t, d = y_lru.shape
    cb = _pick(math.gcd(d, col0), 1024)
    nc = d // cb
    base = col0 // cb

    def body(gl_ref, gs_ref, yl_ref, ys_ref, o_ref):
        o_ref[...] = (_sigmoid(gl_ref[...]) * yl_ref[...] + _sigmoid(gs_ref[...]) * ys_ref[...]).astype(BF16)

    gate = lambda slab: pl.BlockSpec((tc, cb), lambda s, j, slab=slab: (s, base + slab * nc + j))
    blk = pl.BlockSpec((tc, cb), lambda s, j: (s, j))
    return pl.pallas_call(
        body,
        out_shape=jax.ShapeDtypeStruct((t, d), BF16),
        grid=(t // tc, nc),
        in_specs=[gate(0), gate(1), blk, blk],
        out_specs=blk,
        compiler_params=_cparams(("parallel", "parallel"), 2 * tc * cb * 20),
        name="merge_fwd",
    )(p, p, y_lru, y_sc)


def _merge_bwd(p, y_lru, y_sc, dmerged, col0, tc, n_slabs):
    t, d = y_lru.shape
    cb = _pick(math.gcd(d, col0), 1024)
    nc = d // cb
    base = col0 // cb

    def body(gl_ref, gs_ref, yl_ref, ys_ref, dm_ref, dp_ref, dyl_ref, dys_ref):
        dm = dm_ref[...]
        sl = _sigmoid(gl_ref[...])
        ss = _sigmoid(gs_ref[...])
        dp_ref[0] = (dm * yl_ref[...] * (sl * (1.0 - sl))).astype(BF16)
        dp_ref[1] = (dm * ys_ref[...] * (ss * (1.0 - ss))).astype(BF16)
        dyl_ref[...] = (dm * sl).astype(BF16)
        dys_ref[...] = (dm * ss).astype(BF16)

    gate = lambda slab: pl.BlockSpec((tc, cb), lambda s, j, slab=slab: (s, base + slab * nc + j))
    blk = pl.BlockSpec((tc, cb), lambda s, j: (s, j))
    act = jax.ShapeDtypeStruct((t, d), BF16)
    return pl.pallas_call(
        body,
        out_shape=[jax.ShapeDtypeStruct((n_slabs, t, cb), BF16), act, act],
        grid=(t // tc, nc),
        in_specs=[gate(0), gate(1), blk, blk, blk],
        out_specs=[pl.BlockSpec((2, tc, cb), lambda s, j: (j, s, 0)), blk, blk],
        compiler_params=_cparams(("parallel", "parallel"), 2 * tc * cb * 28),
        name="merge_bwd",
    )(p, p, y_lru, y_sc, dmerged)


def _ffn_act_fwd(up, conv_w, d_ff, cb, tc):
    t = up.shape[0]
    nc = d_ff // cb
    nt = t // tc
    width = conv_w.shape[0]

    def body(g_ref, v_ref, wg_ref, wv_ref, o_ref, halo_g, halo_v):
        @pl.when(pl.program_id(1) == 0)
        def _():
            halo_g[...] = jnp.zeros_like(halo_g)
            halo_v[...] = jnp.zeros_like(halo_v)

        g = g_ref[...]
        v = v_ref[...]
        ug = _conv_fwd(jnp.concatenate([halo_g[...], g], axis=0), wg_ref[...], width)
        uv = _conv_fwd(jnp.concatenate([halo_v[...], v], axis=0), wv_ref[...], width)
        halo_g[...] = g[tc - SUBLANE:, :]
        halo_v[...] = v[tc - SUBLANE:, :]
        o_ref[...] = (ug * _sigmoid(ug) * uv).astype(BF16)

    blk = lambda half: pl.BlockSpec((tc, cb), lambda j, s, half=half: (s, half * nc + j))
    wblk = lambda half: pl.BlockSpec((width, cb), lambda j, s, half=half: (0, half * nc + j))
    return pl.pallas_call(
        body,
        out_shape=jax.ShapeDtypeStruct((t, d_ff), BF16),
        grid=(nc, nt),
        in_specs=[blk(0), blk(1), wblk(0), wblk(1)],
        out_specs=pl.BlockSpec((tc, cb), lambda j, s: (s, j)),
        scratch_shapes=[pltpu.VMEM((SUBLANE, cb), F32), pltpu.VMEM((SUBLANE, cb), F32)],
        compiler_params=_cparams(("parallel", "arbitrary"), 24 * tc * cb * 4),
        name="ffn_act_fwd",
    )(up, up, conv_w, conv_w)


def _ffn_act_bwd(up, dact, conv_w, d_ff, cb, tc, dep=None):
    t = up.shape[0]
    nc = d_ff // cb
    nt = t // tc
    width = conv_w.shape[0]
    halo_blocks = tc // SUBLANE

    def body(g_ref, gh_ref, v_ref, vh_ref, da_ref, wg_ref, wv_ref,
             dup_ref, dwg_ref, dwv_ref, nxt_g, nxt_v):
        s = pl.program_id(1)

        @pl.when(s == 0)
        def _():
            nxt_g[...] = jnp.zeros_like(nxt_g)
            nxt_v[...] = jnp.zeros_like(nxt_v)
            dwg_ref[...] = jnp.zeros_like(dwg_ref)
            dwv_ref[...] = jnp.zeros_like(dwv_ref)

        keep = jnp.where(s == nt - 1, 0.0, 1.0)
        wg = wg_ref[...]
        wv = wv_ref[...]
        catg = jnp.concatenate([gh_ref[...] * keep, g_ref[...]], axis=0)
        catv = jnp.concatenate([vh_ref[...] * keep, v_ref[...]], axis=0)
        ug = _conv_fwd(catg, wg, width)
        uv = _conv_fwd(catv, wv, width)
        sg = _sigmoid(ug)
        da = da_ref[...]
        duv = da * (ug * sg)
        dup_ref[1] = _conv_bwd_input(jnp.concatenate([duv, nxt_v[...]], axis=0), wv, width).astype(BF16)
        nxt_v[...] = duv[0:SUBLANE, :]
        _conv_bwd_weight(dwv_ref, duv, catv, width)
        dug = da * uv * (sg * (1.0 + ug * (1.0 - sg)))
        dup_ref[0] = _conv_bwd_input(jnp.concatenate([dug, nxt_g[...]], axis=0), wg, width).astype(BF16)
        nxt_g[...] = dug[0:SUBLANE, :]
        _conv_bwd_weight(dwg_ref, dug, catg, width)

    rev = lambda s: nt - 1 - s
    blk = lambda half: pl.BlockSpec((tc, cb), lambda j, s, half=half: (rev(s), half * nc + j))
    halo = lambda half: pl.BlockSpec(
        (SUBLANE, cb), lambda j, s, half=half: (jnp.maximum(rev(s) * halo_blocks - 1, 0), half * nc + j))
    wblk = lambda half: pl.BlockSpec((width, cb), lambda j, s, half=half: (0, half * nc + j))
    out_blk = pl.BlockSpec((tc, cb), lambda j, s: (rev(s), j))
    wout = pl.BlockSpec((width, cb), lambda j, s: (0, j))
    act = jax.ShapeDtypeStruct((t, d_ff), BF16)
    wshape = jax.ShapeDtypeStruct((width, d_ff), F32)
    body, in_specs, operands = _dep_args(
        body, [blk(0), halo(0), blk(1), halo(1), out_blk, wblk(0), wblk(1)],
        [up, up, up, up, dact, conv_w, conv_w], dep)
    return pl.pallas_call(
        body,
        out_shape=[jax.ShapeDtypeStruct((2, t, d_ff), BF16), wshape, wshape],
        grid=(nc, nt),
        in_specs=in_specs,
        out_specs=[pl.BlockSpec((2, tc, cb), lambda j, s: (0, rev(s), j)), wout, wout],
        scratch_shapes=[pltpu.VMEM((SUBLANE, cb), F32), pltpu.VMEM((SUBLANE, cb), F32)],
        compiler_params=_cparams(("parallel", "arbitrary"), 40 * tc * cb * 4),
        name="ffn_act_bwd",
    )(*operands)


def _mesh_pos():
    x, y, c = lax.axis_index("x"), lax.axis_index("y"), lax.axis_index("c")
    return x, y, c


def _other_chips(x, y):
    return [(1 - x, y), (x, 1 - y), (1 - x, 1 - y)]


def _cast_place(w, chip, col_sharded, name, dep=None):
    r, cdim = w.shape
    full = (r, cdim * N_CHIPS) if col_sharded else (r * N_CHIPS, cdim)
    rb = _pick(r, max(BF16_ROWS, (512 * 1024) // cdim), BF16_ROWS)
    nb = r // rb

    def body(chip_ref, w_ref, o_ref):
        o_ref[...] = w_ref[...].astype(BF16)

    if col_sharded:
        out_map = lambda i, chip_ref: (i, chip_ref[0])
    else:
        out_map = lambda i, chip_ref: (chip_ref[0] * nb + i, 0)
    grid_spec = pltpu.PrefetchScalarGridSpec(
        num_scalar_prefetch=1,
        grid=(nb,),
        in_specs=[pl.BlockSpec((rb, cdim), lambda i, chip_ref: (i, 0))] + ([ANY] if dep is not None else []),
        out_specs=pl.BlockSpec((rb, cdim), out_map),
    )
    body, _, operands = _dep_args(body, [], [chip, w], dep)
    return pl.pallas_call(
        body,
        out_shape=jax.ShapeDtypeStruct(full, BF16),
        grid_spec=grid_spec,
        compiler_params=_cparams(("parallel",), 2 * rb * cdim * 6),
        name=name,
    )(*operands)


def _remote(src, dst, send_sems, recv_sems, idx, to):
    return pltpu.make_async_remote_copy(
        src_ref=src, dst_ref=dst, send_sem=send_sems.at[idx], recv_sem=recv_sems.at[idx],
        device_id=to, device_id_type=MESH)


def _exchange(name, arrays, n_sems, plan):
    n = len(arrays)

    def body(*refs):
        bufs = refs[n:2 * n]
        send_sems, recv_sems = refs[2 * n:]
        sends, arrivals = plan(bufs, send_sems, recv_sems)
        for cp in sends:
            cp.start()
        for cp in arrivals:
            cp.wait_recv()
        for cp in sends:
            cp.wait_send()

    outs = pl.pallas_call(
        body,
        out_shape=[jax.ShapeDtypeStruct(a.shape, a.dtype) for a in arrays],
        in_specs=[ANY] * n,
        out_specs=[ANY] * n,
        input_output_aliases={k: k for k in range(n)},
        scratch_shapes=[pltpu.SemaphoreType.DMA((n_sems,)), pltpu.SemaphoreType.DMA((n_sems,))],
        name=name,
    )(*arrays)
    return list(outs)


def _exchange_start(name, arrays, n_sems, plan, after=None):
    n = len(arrays)
    n_in = n + (after is not None)

    def body(*refs):
        bufs = refs[:n]
        send_sems, recv_sems = refs[n_in], refs[n_in + 1]
        token = refs[-1]
        sends, _ = plan(bufs, send_sems, recv_sems)
        for cp in sends:
            cp.start()
        token[...] = jnp.zeros_like(token)

    out = pl.pallas_call(
        body,
        out_shape=(pltpu.SemaphoreType.DMA((n_sems,)), pltpu.SemaphoreType.DMA((n_sems,)),
                   *[pltpu.HBM(a.shape, a.dtype) for a in arrays],
                   jax.ShapeDtypeStruct((SUBLANE, LANE), F32)),
        in_specs=[HBM_SPEC] * n + [ANY] * (n_in - n),
        out_specs=(SEM_SPEC, SEM_SPEC, *[HBM_SPEC] * n, VMEM_SPEC),
        input_output_aliases={k: 2 + k for k in range(n)},
        compiler_params=pltpu.CompilerParams(has_side_effects=DATAFLOW_EFFECT),
        name=name,
    )(*[pltpu.with_memory_space_constraint(a, pltpu.HBM) for a in arrays], *([after] if after is not None else []))
    return out[0], out[1], list(out[2:2 + n]), out[-1]


def _exchange_wait(name, arrays, send_sems, recv_sems, after, plan):
    n = len(arrays)

    def body(*refs):
        bufs = refs[:n]
        sends, arrivals = plan(bufs, refs[n], refs[n + 1])
        for cp in arrivals:
            cp.wait_recv()
        for cp in sends:
            cp.wait_send()

    outs = pl.pallas_call(
        body,
        out_shape=[pltpu.HBM(a.shape, a.dtype) for a in arrays],
        in_specs=[HBM_SPEC] * n + [SEM_SPEC, SEM_SPEC, ANY],
        out_specs=[HBM_SPEC] * n,
        input_output_aliases={k: k for k in range(n)},
        compiler_params=pltpu.CompilerParams(has_side_effects=DATAFLOW_EFFECT),
        name=name,
    )(*arrays, send_sems, recv_sems, after)
    return list(outs)


def _half_block(ref, shard_shape, col_sharded, chip, half):
    r, cdim = shard_shape
    h = r // 2
    if col_sharded:
        return ref.at[pl.ds(pl.multiple_of(half * h, BF16_ROWS), h),
                      pl.ds(pl.multiple_of(chip * cdim, LANE), cdim)]
    return ref.at[pl.ds(pl.multiple_of(chip * r + half * h, BF16_ROWS), h), :]


def _gather_plan(shard_shapes, col_sharded, ks):
    def plan(bufs, send_sems, recv_sems):
        x, y, c = _mesh_pos()
        sends, arrivals = [], []
        for ref, k in zip(bufs, ks):
            mine = _half_block(ref, shard_shapes[k], col_sharded[k], 2 * x + y, c)
            for j, (px, py) in enumerate(_other_chips(x, y)):
                landed = _half_block(ref, shard_shapes[k], col_sharded[k], 2 * px + py, c)
                sends.append(_remote(mine, mine, send_sems, recv_sems, 3 * k + j, (px, py, c)))
                arrivals.append(_remote(landed, landed, send_sems, recv_sems, 3 * k + j, (px, py, c)))
        return sends, arrivals
    return plan


def _forward_plan(shard_shapes, col_sharded, ks):
    def plan(bufs, send_sems, recv_sems):
        x, y, c = _mesh_pos()
        sends, arrivals = [], []
        for i, (ref, k) in enumerate(zip(bufs, ks)):
            for j, (px, py) in enumerate(_other_chips(x, y)):
                landed = _half_block(ref, shard_shapes[k], col_sharded[k], 2 * px + py, c)
                theirs = _half_block(ref, shard_shapes[k], col_sharded[k], 2 * px + py, 1 - c)
                sends.append(_remote(landed, landed, send_sems, recv_sems, 3 * i + j, (x, y, 1 - c)))
                arrivals.append(_remote(theirs, theirs, send_sems, recv_sems, 3 * i + j, (x, y, 1 - c)))
        return sends, arrivals
    return plan


def _small_gather(small):
    def body(small_ref, out_ref, send_sems, recv_sems):
        x, y, c = _mesh_pos()
        me = 2 * x + y
        out_ref[me] = small_ref[...]
        copies = []
        for j, (px, py) in enumerate(_other_chips(x, y)):
            cp = _remote(small_ref, out_ref.at[me], send_sems, recv_sems, j, (px, py, c))
            cp.start()
            copies.append(cp)
        for j, (px, py) in enumerate(_other_chips(x, y)):
            _remote(small_ref, out_ref.at[2 * px + py], send_sems, recv_sems, j, (px, py, c)).wait_recv()
        for cp in copies:
            cp.wait_send()

    return pl.pallas_call(
        body,
        out_shape=jax.ShapeDtypeStruct((N_CHIPS,) + small.shape, small.dtype),
        in_specs=[VMEM_SPEC],
        out_specs=VMEM_SPEC,
        scratch_shapes=[pltpu.SemaphoreType.DMA((N_CHIPS - 1,)), pltpu.SemaphoreType.DMA((N_CHIPS - 1,))],
        name="gather_small",
    )(small)


def _as3d(g, col_sharded):
    r, cdim = g.shape
    return g.reshape(1, r, cdim) if col_sharded else g.reshape(N_CHIPS, r // N_CHIPS, cdim)


def _pair_plan(m):
    def plan(bufs, send_sems, recv_sems):
        x, y, c = _mesh_pos()
        copies = []
        for i in range(m):
            h = bufs[i].shape[1] // 2
            src = bufs[i].at[:, pl.ds(pl.multiple_of((1 - c) * h, BF16_ROWS), h), :]
            copies.append(_remote(src, bufs[m + i], send_sems, recv_sems, i, (x, y, 1 - c)))
        return copies, copies
    return plan


def _chip_plan(col_flags):
    m = len(col_flags)

    def plan(bufs, send_sems, recv_sems):
        x, y, c = _mesh_pos()
        copies = []
        for i in range(m):
            land = bufs[m + i]
            width = land.shape[2]
            for j, (px, py) in enumerate(_other_chips(x, y)):
                q = 2 * px + py
                if col_flags[i]:
                    src = bufs[i].at[0, :, pl.ds(pl.multiple_of(q * width, LANE), width)]
                else:
                    src = bufs[i].at[q]
                copies.append(_remote(src, land.at[j], send_sems, recv_sems, 3 * i + j, (px, py, c)))
        return copies, copies
    return plan


def _share_plan(m):
    def plan(bufs, send_sems, recv_sems):
        x, y, c = _mesh_pos()
        sends, arrivals = [], []
        for i in range(m):
            h = bufs[i].shape[0] // 2
            mine = bufs[i].at[pl.ds(pl.multiple_of(c * h, SUBLANE), h), :]
            theirs = bufs[i].at[pl.ds(pl.multiple_of((1 - c) * h, SUBLANE), h), :]
            sends.append(_remote(mine, mine, send_sems, recv_sems, i, (x, y, 1 - c)))
            arrivals.append(_remote(theirs, theirs, send_sems, recv_sems, i, (x, y, 1 - c)))
        return sends, arrivals
    return plan


def _pair_add(g3, other, core):
    a, r, cdim = g3.shape
    h = r // 2
    rb = _pick(h, max(BF16_ROWS, (512 * 1024) // cdim), BF16_ROWS)
    nb = h // rb

    def body(core_ref, g_ref, o_ref, out_ref):
        out_ref[...] = (g_ref[...].astype(F32) + o_ref[...].astype(F32)).astype(BF16)

    grid_spec = pltpu.PrefetchScalarGridSpec(
        num_scalar_prefetch=1,
        grid=(a, nb),
        in_specs=[pl.BlockSpec((None, rb, cdim), lambda i, j, core_ref: (i, core_ref[0] * nb + j, 0)),
                  pl.BlockSpec((None, rb, cdim), lambda i, j, core_ref: (i, j, 0))],
        out_specs=pl.BlockSpec((None, rb, cdim), lambda i, j, core_ref: (i, j, 0)),
    )
    return pl.pallas_call(
        body,
        out_shape=jax.ShapeDtypeStruct((a, h, cdim), BF16),
        grid_spec=grid_spec,
        compiler_params=_cparams(("parallel", "parallel"), 2 * rb * cdim * 10),
        name="grad_pair_add",
    )(core, g3, other)


def _small_allreduce(small):
    rows = small.shape[0]
    pad = (-rows) % (2 * SUBLANE)
    if pad:
        small = jnp.pad(small, ((0, pad), (0, 0)))
    h = small.shape[0] // 2
    half_shape = (h, small.shape[1])

    def body(small_ref, out_ref, theirs, by_chip, send_sems, recv_sems):
        x, y, c = _mesh_pos()
        me = 2 * x + y
        sibling = (x, y, 1 - c)
        mine = pl.ds(pl.multiple_of(c * h, SUBLANE), h)
        other = pl.ds(pl.multiple_of((1 - c) * h, SUBLANE), h)
        swap = _remote(small_ref, theirs, send_sems, recv_sems, 0, sibling)
        swap.start()
        swap.wait()
        by_chip[me] = small_ref[mine, :] + theirs[mine, :]
        copies = []
        for j, (px, py) in enumerate(_other_chips(x, y)):
            cp = _remote(by_chip.at[me], by_chip.at[me], send_sems, recv_sems, 1 + j, (px, py, c))
            cp.start()
            copies.append(cp)
        for j, (px, py) in enumerate(_other_chips(x, y)):
            landed = by_chip.at[2 * px + py]
            _remote(landed, landed, send_sems, recv_sems, 1 + j, (px, py, c)).wait_recv()
        total = by_chip[0]
        for q in range(1, N_CHIPS):
            total = total + by_chip[q]
        out_ref[mine, :] = total
        for cp in copies:
            cp.wait_send()
        share = _remote(out_ref.at[mine, :], out_ref.at[mine, :], send_sems, recv_sems, 4, sibling)
        share.start()
        _remote(out_ref.at[other, :], out_ref.at[other, :], send_sems, recv_sems, 4, sibling).wait_recv()
        share.wait_send()

    out = pl.pallas_call(
        body,
        out_shape=jax.ShapeDtypeStruct(small.shape, F32),
        in_specs=[VMEM_SPEC],
        out_specs=VMEM_SPEC,
        scratch_shapes=[pltpu.VMEM(small.shape, F32), pltpu.VMEM((N_CHIPS,) + half_shape, F32),
                        pltpu.SemaphoreType.DMA((5,)), pltpu.SemaphoreType.DMA((5,))],
        compiler_params=pltpu.CompilerParams(
            vmem_limit_bytes=min(VMEM_BUDGET, 8 * _nbytes(small.shape, F32) + (8 << 20))),
        name="grad_small_allreduce",
    )(small)
    return out[:rows]


def _chip_sum(partial, land, where, col_sharded):
    _, h, cdim = land.shape
    rb = _pick(h, max(BF16_ROWS, (512 * 1024) // cdim), BF16_ROWS)
    nb = h // rb

    def body(where_ref, own_ref, l_ref, o_ref, half_ref):
        total = own_ref[...].astype(F32)
        for j in range(N_CHIPS - 1):
            total = total + l_ref[j].astype(F32)
        o_ref[...] = total
        half_ref[...] = total

    if col_sharded:
        own_map = lambda i, w: (0, i, w[0])
    else:
        own_map = lambda i, w: (w[0], i, 0)
    grid_spec = pltpu.PrefetchScalarGridSpec(
        num_scalar_prefetch=1,
        grid=(nb,),
        in_specs=[pl.BlockSpec((None, rb, cdim), own_map),
                  pl.BlockSpec((N_CHIPS - 1, rb, cdim), lambda i, w: (0, i, 0))],
        out_specs=[pl.BlockSpec((rb, cdim), lambda i, w: (w[1] * nb + i, 0)),
                   pl.BlockSpec((rb, cdim), lambda i, w: (i, 0))],
    )
    return pl.pallas_call(
        body,
        out_shape=[jax.ShapeDtypeStruct((2 * h, cdim), F32), jax.ShapeDtypeStruct((h, cdim), F32)],
        grid_spec=grid_spec,
        compiler_params=_cparams(("parallel",), 2 * rb * cdim * 16),
        name="grad_chip_sum",
    )(where, partial, land)


def _adamw(w, g, m, v, name, dep=None):
    r, cdim = w.shape
    rb = _pick(r, max(SUBLANE, (256 * 1024) // cdim), SUBLANE)
    blk = pl.BlockSpec((rb, cdim), lambda i: (i, 0))
    shape = jax.ShapeDtypeStruct((r, cdim), F32)
    body, in_specs, operands = _dep_args(_adamw_rows, [blk] * 4, [w, g, m, v], dep)
    return pl.pallas_call(
        body,
        out_shape=[shape] * 4,
        grid=(r // rb,),
        in_specs=in_specs,
        out_specs=[blk] * 4,
        compiler_params=_cparams(("parallel",), 2 * rb * cdim * 4 * 8),
        name=name,
    )(*operands)


def _adamw_rows(w_ref, g_ref, m_ref, v_ref, go_ref, d_ref, mo_ref, vo_ref):
    c1 = 1.0 - ADAM_B1 ** ADAM_STEP
    c2 = 1.0 - ADAM_B2 ** ADAM_STEP
    gv = g_ref[...]
    mn = ADAM_B1 * m_ref[...] + (1.0 - ADAM_B1) * gv
    vn = ADAM_B2 * v_ref[...] + (1.0 - ADAM_B2) * (gv * gv)
    m_hat = mn / c1
    v_hat = vn / c2
    d_ref[...] = -ADAM_LR * (m_hat / (jnp.sqrt(v_hat) + ADAM_EPS) + ADAM_WD * w_ref[...])
    go_ref[...] = gv
    mo_ref[...] = mn
    vo_ref[...] = vn


def _adamw_half(w, g, m, v, where, mine, prev, name, dep=None):
    r, cdim = w.shape
    h = r // 2
    rb = _pick(h, max(SUBLANE, (256 * 1024) // cdim), SUBLANE)
    nb = h // rb

    def rows(i, where_ref):
        half = where_ref[1] if mine else 1 - where_ref[1]
        return half * nb + i, 0

    blk = pl.BlockSpec((rb, cdim), rows)
    g_blk = pl.BlockSpec((rb, cdim), lambda i, where_ref: (i, 0)) if mine else blk
    n_prev = 0 if prev is None else len(prev)
    grid_spec = pltpu.PrefetchScalarGridSpec(
        num_scalar_prefetch=1,
        grid=(nb,),
        in_specs=[blk, g_blk, blk, blk] + [ANY] * (n_prev + (dep is not None)),
        out_specs=[blk] * 4,
    )
    body, _, operands = _dep_args(_adamw_rows, [], [w, g, m, v], *(prev or ()), dep)

    def with_where(where_ref, *refs):
        return body(*refs)

    shape = jax.ShapeDtypeStruct((r, cdim), F32)
    return pl.pallas_call(
        with_where,
        out_shape=[shape] * 4,
        grid_spec=grid_spec,
        input_output_aliases={5 + i: i for i in range(n_prev)},
        compiler_params=_cparams(("parallel",), 2 * rb * cdim * 4 * 8),
        name=name,
    )(where, *operands)


def _pack(arrays):
    tile = SUBLANE * LANE
    pieces = []
    for arr in arrays:
        flat = arr.reshape(-1)
        pad = (-flat.shape[0]) % tile
        if pad:
            flat = jnp.concatenate([flat, jnp.zeros((pad,), flat.dtype)])
        pieces.append(flat)
    return jnp.concatenate(pieces).reshape(-1, LANE)


def _unpack(packed, shapes):
    tile = SUBLANE * LANE
    flat = packed.reshape(-1)
    out, off = [], 0
    for shp in shapes:
        size = math.prod(shp)
        out.append(flat[off:off + size].reshape(shp))
        off += size + ((-size) % tile)
    return out


def _block_diag_groups(w, per_group):
    hcount, hd, _ = w.shape
    ng = hcount // per_group
    w4 = w.reshape(ng, per_group, hd, hd)
    eye = jnp.eye(per_group, dtype=w.dtype)
    bd = w4[:, :, :, None, :] * eye[None, :, None, :, None]
    return bd.reshape(ng, per_group * hd, per_group * hd).astype(BF16)


def _diag_blocks(wbd, per_group, hd):
    ng = wbd.shape[0]
    w5 = wbd.reshape(ng, per_group, hd, per_group, hd)
    blocks = [w5[:, i, :, i, :] for i in range(per_group)]
    return jnp.stack(blocks, axis=1).reshape(ng * per_group, hd, hd)


def kernel(x, g_mix, w_in, lru_conv_w, lru_conv_b, lru_wa, lru_ba, lru_wx, lru_bx, lru_lambda, lru_w_out, sc_conv_w, sc_w_out, w_o, g_ffn, ffn_w_up, ffn_conv_w, ffn_w_down, g_final, loss_target, m_g_mix, m_w_in, m_lru_conv_w, m_lru_conv_b, m_lru_wa, m_lru_ba, m_lru_wx, m_lru_bx, m_lru_lambda, m_lru_w_out, m_sc_conv_w, m_sc_w_out, m_w_o, m_g_ffn, m_ffn_w_up, m_ffn_conv_w, m_ffn_w_down, m_g_final, v_g_mix, v_w_in, v_lru_conv_w, v_lru_conv_b, v_lru_wa, v_lru_ba, v_lru_wx, v_lru_bx, v_lru_lambda, v_lru_w_out, v_sc_conv_w, v_sc_w_out, v_w_o, v_g_ffn, v_ffn_w_up, v_ffn_conv_w, v_ffn_w_down, v_g_final):
    seq, d_model = x.shape[1], x.shape[2]
    heads, head_dim, _ = lru_wa.shape
    d_lru = heads * head_dim
    d_sc = sc_w_out.shape[0]
    d_ff = ffn_w_down.shape[0] * N_CHIPS
    assert x.shape[0] == 1 and w_in.shape[1] * N_CHIPS == 2 * d_lru + 3 * d_sc + 2 * d_model
    xs = x.reshape(seq, d_model)
    target = loss_target.reshape(seq, d_model)

    chip = 2 * lax.axis_index("x") + lax.axis_index("y")
    core = lax.axis_index("c").astype(jnp.int32).reshape(1)

    big_w = [w_in, lru_w_out, sc_w_out, w_o, ffn_w_up, ffn_w_down]
    big_m = [m_w_in, m_lru_w_out, m_sc_w_out, m_w_o, m_ffn_w_up, m_ffn_w_down]
    big_v = [v_w_in, v_lru_w_out, v_sc_w_out, v_w_o, v_ffn_w_up, v_ffn_w_down]
    col_sharded = [True, True, True, False, True, False]
    conv_shards = [lru_conv_w, sc_conv_w, ffn_conv_w]
    conv_pack = jnp.concatenate(
        [jnp.pad(w, ((0, SUBLANE - w.shape[0]), (0, 0))) for w in conv_shards], axis=1)
    big_names = ["w_in", "lru_w_out", "sc_w_out", "w_o", "ffn_w_up", "ffn_w_down"]
    chip_arr = chip.astype(jnp.int32).reshape(1)
    placed = [_cast_place(big_w[0], chip_arr, col_sharded[0], "cast_" + big_names[0])]
    conv_all = _small_gather(conv_pack)
    shard_shapes = [w.shape for w in big_w]
    n_big = len(big_w)

    def gather_start(ks, after, tag):
        send, recv, bufs, token = _exchange_start(
            "gather_start_" + tag, [placed[k] for k in ks], 3 * n_big,
            _gather_plan(shard_shapes, col_sharded, ks), after=after)
        return (send, recv, dict(zip(ks, bufs))), token

    def arrived(state, ks, after, tag):
        send, recv, bufs = state
        got = _exchange_wait("gather_wait_" + tag, [bufs[k] for k in ks], send, recv, after,
                             _gather_plan(shard_shapes, col_sharded, ks))
        return _exchange("gather_forward_" + tag, got, 3 * len(ks), _forward_plan(shard_shapes, col_sharded, ks))

    conv_full, off = [], 0
    for w in conv_shards:
        kw, nq = w.shape
        piece = conv_all[:, :kw, off:off + nq]
        conv_full.append(piece.transpose(1, 0, 2).reshape(kw, N_CHIPS * nq))
        off += nq
    lcw, scw, fcw = conv_full

    per_group = max(1, min(heads, 256 // head_dim))
    gc = per_group * head_dim
    wa_bd = _block_diag_groups(lru_wa, per_group)
    wx_bd = _block_diag_groups(lru_wx, per_group)
    tc = _pick(seq, 256, SUBLANE)
    cb_sc = _pick(d_sc, 512)
    cb_ff = _pick(d_ff, 512)
    col_sc = 2 * d_lru
    col_gates = 2 * d_lru + 3 * d_sc

    first, token = gather_start([0], conv_all, "in")
    for k in range(1, n_big):
        placed.append(_cast_place(big_w[k], chip_arr, col_sharded[k], "cast_" + big_names[k], dep=token))
        token = placed[-1]
    h1 = _rms_fwd(xs, g_mix, "rms_mix", dep=token)
    (win_b,) = arrived(first, [0], h1, "in")
    rest, token = gather_start([1, 2, 3, 4, 5], win_b, "rest")
    p = _mm(h1, win_b, "nn", F32, name="mm_in", dep=token)
    wlo_b, wso_b, wo_b = arrived(rest, [1, 2, 3], p, "mix")
    y_lru_pre, hseq = _lru_fwd(p, lcw, lru_conv_b, wa_bd, lru_ba, wx_bd, lru_bx, lru_lambda, d_lru, gc, tc)
    y_sc_pre = _sc_fwd(p, scw, col_sc, d_sc, cb_sc, tc)
    y_lru = _mm(y_lru_pre, wlo_b, "nn", BF16, name="mm_lru_out")
    y_sc = _mm(y_sc_pre, wso_b, "nn", BF16, name="mm_sc_out")
    merged = _merge_fwd(p, y_lru, y_sc, col_gates, tc)
    x2 = _mm(merged, wo_b, "nn", F32, res=xs, name="mm_o")
    (wup_b,) = arrived(rest, [4], x2, "up")
    h2 = _rms_fwd(x2, g_ffn, "rms_ffn")
    up = _mm(h2, wup_b, "nn", F32, name="mm_up")
    (wdn_b,) = arrived(rest, [5], up, "down")
    act = _ffn_act_fwd(up, fcw, d_ff, cb_ff, tc)
    x3 = _mm(act, wdn_b, "nn", F32, res=x2, name="mm_down")
    loss_part, dx3, dx3b, dg_final = _loss_head(x3, g_final, target)

    where = jnp.concatenate([chip_arr, core])

    def reduce_start(grads, flags, tag):
        views = [_as3d(g, cs) for g, cs in zip(grads, flags)]
        lands = [lax.empty((v.shape[0], v.shape[1] // 2, v.shape[2]), v.dtype) for v in views]
        send, recv, bufs, token = _exchange_start("grad_pair_start_" + tag, views + lands, len(views),
                                                  _pair_plan(len(views)))
        return (send, recv, bufs, flags, tag), token

    def reduce_mid(state, after):
        send, recv, bufs, flags, tag = state
        m = len(flags)
        bufs = _exchange_wait("grad_pair_wait_" + tag, bufs, send, recv, after, _pair_plan(m))
        partials = [_pair_add(bufs[i], bufs[m + i], core) for i in range(m)]
        lands = []
        for pz, cs in zip(partials, flags):
            _, h, cdim = pz.shape
            lands.append(lax.empty((N_CHIPS - 1, h, cdim // N_CHIPS if cs else cdim), BF16))
        send, recv, bufs, token = _exchange_start("grad_chip_start_" + tag, partials + lands, 3 * m,
                                                  _chip_plan(flags))
        return (send, recv, bufs, flags, tag), token

    def reduce_end(state, after):
        send, recv, bufs, flags, tag = state
        m = len(flags)
        bufs = _exchange_wait("grad_chip_wait_" + tag, bufs, send, recv, after, _chip_plan(flags))
        return [_chip_sum(bufs[i], bufs[m + i], where, flags[i]) for i in range(m)]

    g_wdn = _mm(act, dx3b, "tn", BF16, name="mm_down_dw")
    red_down, token = reduce_start([g_wdn], [False], "down")
    dact = _mm(dx3b, wdn_b, "nt", F32, name="mm_down_dx", dep=token)
    red_down, token = reduce_mid(red_down, dact)
    dup, dfcw_g, dfcw_v = _ffn_act_bwd(up, dact, fcw, d_ff, cb_ff, tc, dep=token)
    g_wup = _mm(h2, dup, "tn", BF16, name="mm_up_dw", slabs=[0, 1])
    red_up, token = reduce_start([g_wup], [True], "up")
    dh2 = _mm(dup, wup_b, "nt", F32, name="mm_up_dx", dep=token, slabs=[0, 1])
    red_up, token = reduce_mid(red_up, dh2)
    dx2, dx2b, dg_ffn = _rms_bwd(x2, g_ffn, dh2, dx3, "rms_ffn_bwd", True, dep=token)
    g_wo = _mm(merged, dx2b, "tn", BF16, name="mm_o_dw")
    dmerged = _mm(dx2b, wo_b, "nt", BF16, name="mm_o_dx")
    slab_w = d_lru
    assert d_sc == slab_w and d_model % slab_w == 0 and col_gates % slab_w == 0
    n_gate = d_model // slab_w
    gate0 = col_gates // slab_w
    dp_slabs = [gate0 + kind * n_gate + j for j in range(n_gate) for kind in (0, 1)] + [0, 1, 2, 3, 4]
    dp, dyl, dys = _merge_bwd(p, y_lru, y_sc, dmerged, col_gates, tc, len(dp_slabs))
    assert dp.shape[2] == slab_w
    g_wlo = _mm(y_lru_pre, dyl, "tn", BF16, name="mm_lru_out_dw")
    g_wso = _mm(y_sc_pre, dys, "tn", BF16, name="mm_sc_out_dw")
    red_mix, token = reduce_start([g_wlo, g_wso, g_wo], [True, True, False], "mix")
    dylp = _mm(dyl, wlo_b, "nt", F32, name="mm_lru_out_dx", dep=token)
    dysp = _mm(dys, wso_b, "nt", F32, name="mm_sc_out_dx")
    red_mix, token = reduce_mid(red_mix, dysp)
    dp, dlcw, dlcb, dwa_bd, dba, dwx_bd, dbx, dlam = _lru_bwd(
        p, hseq, dylp, lcw, lru_conv_b, wa_bd, lru_ba, wx_bd, lru_bx, lru_lambda, d_lru, gc, tc,
        dp, 2 * n_gate, dep=token)
    dp, dscw = _sc_bwd(p, dysp, scw, col_sc, d_sc, cb_sc, tc, dp, 2 * n_gate + 2)
    g_win = _mm(h1, dp, "tn", BF16, name="mm_in_dw", slabs=dp_slabs)
    red_in, token = reduce_start([g_win], [True], "in")
    dh1 = _mm(dp, win_b, "nt", F32, name="mm_in_dx", dep=token, slabs=dp_slabs)
    grad_x, dg_mix = _rms_bwd(xs, g_mix, dh1, dx2, "rms_mix_bwd", False)

    small_g = [dg_mix, dlcw, dlcb, _diag_blocks(dwa_bd, per_group, head_dim), dba,
               _diag_blocks(dwx_bd, per_group, head_dim), dbx, dlam, dscw, dg_ffn,
               jnp.concatenate([dfcw_g, dfcw_v], axis=1), dg_final]
    small_shapes = [a.shape for a in small_g]
    small_sum = _small_allreduce(_pack(small_g))
    red_in, token = reduce_mid(red_in, small_sum)
    big_out = [None] * n_big

    def share_and_update(ks, sums, after, tag):
        send, recv, bufs, last = _exchange_start("grad_share_start_" + tag, [s[0] for s in sums], len(ks),
                                                 _share_plan(len(ks)), after=after)
        for k, (_, own_half) in zip(ks, sums):
            big_out[k] = _adamw_half(big_w[k], own_half, big_m[k], big_v[k], where, True, None,
                                     "adamw_own_" + big_names[k], dep=last)
            last = big_out[k][1]
        bufs = _exchange_wait("grad_share_wait_" + tag, bufs, send, recv, last, _share_plan(len(ks)))
        for k, whole in zip(ks, bufs):
            big_out[k] = _adamw_half(big_w[k], whole, big_m[k], big_v[k], where, False, big_out[k],
                                     "adamw_sib_" + big_names[k])
            last = big_out[k][1]
        return last

    early = reduce_end(red_mix, token) + reduce_end(red_up, token) + reduce_end(red_down, token)
    last = share_and_update([1, 2, 3, 4, 5], early, early[-1][1], "a")
    last = share_and_update([0], reduce_end(red_in, last), None, "b")
    sg = _unpack(small_sum, small_shapes)
    for idx in (1, 8, 10):
        nq = sg[idx].shape[1] // N_CHIPS
        sg[idx] = lax.dynamic_slice_in_dim(sg[idx], chip * nq, nq, axis=1)
    small_w = [g_mix, lru_conv_w, lru_conv_b, lru_wa, lru_ba, lru_wx, lru_bx, lru_lambda, sc_conv_w,
               g_ffn, ffn_conv_w, g_final]
    small_m = [m_g_mix, m_lru_conv_w, m_lru_conv_b, m_lru_wa, m_lru_ba, m_lru_wx, m_lru_bx, m_lru_lambda,
               m_sc_conv_w, m_g_ffn, m_ffn_conv_w, m_g_final]
    small_v = [v_g_mix, v_lru_conv_w, v_lru_conv_b, v_lru_wa, v_lru_ba, v_lru_wx, v_lru_bx, v_lru_lambda,
               v_sc_conv_w, v_g_ffn, v_ffn_conv_w, v_g_final]
    sg = [g.reshape(w.shape) for g, w in zip(sg, small_w)]
    w_shapes = [w.shape for w in small_w]
    packed = _adamw(_pack(small_w), _pack(sg), _pack(small_m), _pack(small_v), "adamw_small")
    small_out = [_unpack(pk, w_shapes) for pk in packed]

    order = [(0, 0), (1, 0), (0, 1), (0, 2), (0, 3), (0, 4), (0, 5), (0, 6), (0, 7), (1, 1), (0, 8), (1, 2),
             (1, 3), (0, 9), (1, 4), (0, 10), (1, 5), (0, 11)]
    by_kind = []
    for kind in range(4):
        by_kind.append([big_out[i][kind] if is_big else small_out[kind][i] for is_big, i in order])
    loss = lax.psum(loss_part[0, 0], ("x", "y", "c"))
    return (loss, grad_x.reshape(x.shape), *by_kind[0], *by_kind[1], *by_kind[2], *by_kind[3])
```

```python
import math

import jax
import jax.numpy as jnp
from jax import lax
from jax.experimental import pallas as pl
from jax.experimental.pallas import tpu as pltpu

F32 = jnp.float32
BF16 = jnp.bfloat16

LANE = 128
SUBLANE = 8
BF16_ROWS = 16
VMEM_BYTES_V7X = 64 * 1024 * 1024
VMEM_BUDGET = VMEM_BYTES_V7X - 8 * 1024 * 1024
MM_VMEM_BUDGET = 42 * 1024 * 1024
EPS = 1e-6
LRU_C = 8.0
ADAM_LR = 0.001
ADAM_B1 = 0.9
ADAM_B2 = 0.999
ADAM_EPS = 1e-08
ADAM_WD = 0.01
ADAM_STEP = 10

N_CHIPS = 4
N_DEV = 8
MESH = pl.DeviceIdType.MESH
ANY = pl.BlockSpec(memory_space=pl.ANY)
VMEM_SPEC = pl.BlockSpec(memory_space=pltpu.VMEM)
HBM_SPEC = pl.BlockSpec(memory_space=pltpu.HBM)
SEM_SPEC = pl.BlockSpec(memory_space=pltpu.SEMAPHORE)
DATAFLOW_EFFECT = pltpu.SideEffectType.DATAFLOW_SIDE_EFFECTING


def _pick(n, cap, mult=LANE):
    best = None
    d = mult
    while d <= min(n, cap):
        if n % d == 0:
            best = d
        d += mult
    return n if best is None else best


def _cparams(semantics, block_bytes):
    limit = min(VMEM_BUDGET, max(32 * 1024 * 1024, int(block_bytes * 1.25) + (4 << 20)))
    return pltpu.CompilerParams(dimension_semantics=semantics, vmem_limit_bytes=limit)


def _nbytes(shape, dtype):
    return math.prod(shape) * jnp.dtype(dtype).itemsize


def _sigmoid(z):
    return 0.5 * jnp.tanh(0.5 * z) + 0.5


def _softplus(z):
    e = jnp.exp(-jnp.abs(z))
    u = 1.0 + e
    log1p = jnp.where(u == 1.0, e, jnp.log(u) * (e / (u - 1.0)))
    return jnp.maximum(z, 0.0) + log1p


def _neg_expm1(z):
    small = z * (1.0 + z * (0.5 + z * (1.0 / 6.0 + z * (1.0 / 24.0))))
    return -jnp.where(jnp.abs(z) < 0.03, small, jnp.exp(z) - 1.0)


_GELU_K = math.sqrt(2.0 / math.pi)
_GELU_C = 0.044715


def _gelu_and_grad(z):
    z2 = z * z
    th = jnp.tanh(_GELU_K * (z + _GELU_C * z2 * z))
    val = 0.5 * z * (1.0 + th)
    grad = 0.5 * (1.0 + th) + 0.5 * z * (1.0 - th * th) * (_GELU_K * (1.0 + 3.0 * _GELU_C * z2))
    return val, grad


def _rows_before(cat, k):
    if k == 0:
        return cat[SUBLANE:, :]
    return pltpu.roll(cat, k, 0)[SUBLANE:, :]


def _rows_after(cat, k):
    n = cat.shape[0]
    if k == 0:
        return cat[:n - SUBLANE, :]
    return pltpu.roll(cat, n - k, 0)[:n - SUBLANE, :]


def _conv_fwd(cat, w, width):
    y = _rows_before(cat, width - 1) * w[0:1, :]
    for k in range(1, width):
        y = y + _rows_before(cat, width - 1 - k) * w[k:k + 1, :]
    return y


def _conv_bwd_input(cat, w, width):
    dx = _rows_after(cat, width - 1) * w[0:1, :]
    for k in range(1, width):
        dx = dx + _rows_after(cat, width - 1 - k) * w[k:k + 1, :]
    return dx


def _conv_bwd_weight(dw_ref, dy, catx, width):
    for k in range(width):
        dw_ref[k:k + 1, :] += jnp.sum(dy * _rows_before(catx, width - 1 - k), axis=0, keepdims=True)


def _scan_tiles(a_ref, b_ref, out_ref, carry0, n_rows, reverse):
    cols = a_ref.shape[1]
    row = lax.broadcasted_iota(jnp.int32, (SUBLANE, cols), 0)
    n_tiles = n_rows // SUBLANE

    def step(j, carry):
        tile = (n_tiles - 1 - j) if reverse else j
        off = pl.multiple_of(tile * SUBLANE, SUBLANE)
        a = a_ref[pl.ds(off, SUBLANE), :]
        b = b_ref[pl.ds(off, SUBLANE), :]
        for s in (1, 2, 4):
            if reverse:
                keep = row < SUBLANE - s
                shift = SUBLANE - s
            else:
                keep = row >= s
                shift = s
            a_sh = jnp.where(keep, pltpu.roll(a, shift, 0), 1.0)
            b_sh = jnp.where(keep, pltpu.roll(b, shift, 0), 0.0)
            b = a * b_sh + b
            a = a * a_sh
        out = a * carry + b
        out_ref[pl.ds(off, SUBLANE), :] = out
        return out[0:1, :] if reverse else out[SUBLANE - 1:SUBLANE, :]

    return lax.fori_loop(0, n_tiles, step, carry0)


def _dep_args(body, in_specs, operands, *deps):
    deps = [d for d in deps if d is not None]
    if not deps:
        return body, in_specs, operands
    n = len(operands)

    def wrapped(*refs):
        return body(*refs[:n], *refs[n + len(deps):])

    return wrapped, list(in_specs) + [ANY] * len(deps), list(operands) + deps


def _mm(a, b, mode, out_dtype, res=None, name=None, dep=None, slabs=None):
    assert a.dtype == BF16 and b.dtype == BF16
    a_slabbed, b_slabbed = a.ndim == 3, b.ndim == 3
    assert not a_slabbed or (mode == "nt" and slabs is not None)
    assert not b_slabbed or (mode == "tn" and slabs is not None)
    if mode == "nn":
        (m, k), (k2, n) = a.shape, b.shape
        dims = (((1,), (0,)), ((), ()))
    elif mode == "nt":
        m, k = (a.shape[1], a.shape[0] * a.shape[2]) if a_slabbed else a.shape
        n, k2 = b.shape
        dims = (((1,), (1,)), ((), ()))
    else:
        k, m = a.shape
        k2, n = (b.shape[1], b.shape[0] * b.shape[2]) if b_slabbed else b.shape
        dims = (((0,), (0,)), ((), ()))
    assert k == k2
    n_unit = b.shape[2] if b_slabbed else n
    out_bytes = jnp.dtype(out_dtype).itemsize
    bm = _pick(m, 1024)
    bn = _pick(n_unit, 1024)
    bk = k

    def est(bm_, bn_, bk_):
        e = 2 * (bm_ * bk_ + bk_ * bn_) * 2 + 2 * bm_ * bn_ * out_bytes
        if k // bk_ > 1:
            e += bm_ * bn_ * 4
        if res is not None:
            e += 2 * bm_ * bn_ * 4
        return e

    for shrink_n, floor in ((True, 512), (False, 512), (True, 256), (False, 256)):
        while est(bm, bn, bk) > MM_VMEM_BUDGET:
            if shrink_n and bn > floor and bn % 2 == 0 and n_unit % (bn // 2) == 0:
                bn //= 2
            elif not shrink_n and bm > floor and bm % 2 == 0 and m % (bm // 2) == 0:
                bm //= 2
            else:
                break
    while (est(bm, bn, bk) > MM_VMEM_BUDGET and not a_slabbed and bk % (2 * LANE) == 0
           and k % (bk // 2) == 0):
        bk //= 2
    nk = k // bk
    per_slab = n_unit // bn

    def out_col(j):
        if not b_slabbed:
            return j
        s = j // per_slab
        where = sum(jnp.where(s == t, slabs[t], 0) for t in range(len(slabs)))
        return where * per_slab + j % per_slab

    if mode == "tn":
        a_spec = pl.BlockSpec((bk, bm), lambda i, j, kk: (kk, i))
    elif a_slabbed:
        a_spec = pl.BlockSpec((a.shape[0], bm, a.shape[2]), lambda i, j, kk: (0, i, 0))
    else:
        a_spec = pl.BlockSpec((bm, bk), lambda i, j, kk: (i, kk))
    if mode == "nt":
        b_spec = pl.BlockSpec((bn, bk), lambda i, j, kk: (j, kk))
    elif b_slabbed:
        b_spec = pl.BlockSpec((None, bk, bn), lambda i, j, kk: (j // per_slab, kk, j % per_slab))
    else:
        b_spec = pl.BlockSpec((bk, bn), lambda i, j, kk: (kk, j))
    o_spec = pl.BlockSpec((bm, bn), lambda i, j, kk: (i, out_col(j)))
    in_specs = [a_spec, b_spec]
    operands = [a, b]
    if res is not None:
        in_specs.append(o_spec)
        operands.append(res)
    has_res = res is not None

    def body(*refs):
        a_ref, b_ref = refs[0], refs[1]
        res_ref = refs[2] if has_res else None
        o_ref = refs[2 + has_res]
        if a_slabbed:
            width = a_ref.shape[2]
            part = None
            for s, col in enumerate(slabs):
                term = lax.dot_general(a_ref[s], b_ref[:, col * width:(col + 1) * width], dims,
                                       preferred_element_type=F32)
                part = term if part is None else part + term
        else:
            part = lax.dot_general(a_ref[...], b_ref[...], dims, preferred_element_type=F32)
        if nk == 1:
            if has_res:
                part = part + res_ref[...]
            o_ref[...] = part.astype(o_ref.dtype)
            return
        acc_ref = refs[-1]
        kk = pl.program_id(2)

        @pl.when(kk == 0)
        def _():
            acc_ref[...] = part

        @pl.when(kk > 0)
        def _():
            acc_ref[...] += part

        @pl.when(kk == nk - 1)
        def _():
            total = acc_ref[...]
            if has_res:
                total = total + res_ref[...]
            o_ref[...] = total.astype(o_ref.dtype)

    scratch = [pltpu.VMEM((bm, bn), F32)] if nk > 1 else []
    body, in_specs, operands = _dep_args(body, in_specs, operands, dep)
    return pl.pallas_call(
        body,
        out_shape=jax.ShapeDtypeStruct((m, n), out_dtype),
        grid=(m // bm, n // bn, nk),
        in_specs=in_specs,
        out_specs=o_spec,
        scratch_shapes=scratch,
        compiler_params=_cparams(("parallel", "parallel", "arbitrary"), est(bm, bn, bk)),
        name=name,
    )(*operands)


def _rms_fwd(x, g, name, dep=None):
    t, d = x.shape
    tb = _pick(t, 512, SUBLANE)

    def body(x_ref, g_ref, h_ref):
        xv = x_ref[...]
        r = lax.rsqrt(jnp.mean(xv * xv, axis=-1, keepdims=True) + EPS)
        h_ref[...] = ((xv * r) * g_ref[...]).astype(BF16)

    blk = pl.BlockSpec((tb, d), lambda i: (i, 0))
    body, in_specs, operands = _dep_args(
        body, [blk, pl.BlockSpec((1, d), lambda i: (0, 0))], [x, g.reshape(1, d)], dep)
    return pl.pallas_call(
        body,
        out_shape=jax.ShapeDtypeStruct((t, d), BF16),
        grid=(t // tb,),
        in_specs=in_specs,
        out_specs=blk,
        compiler_params=_cparams(("parallel",), 2 * tb * d * 6),
        name=name,
    )(*operands)


def _rms_bwd(x, g, dh, dres, name, want_bf16, dep=None):
    t, d = x.shape
    tb = _pick(t, 256, SUBLANE)

    def body(x_ref, g_ref, dh_ref, dres_ref, *outs):
        dx_ref, dg_ref = outs[0], outs[-1]
        xv = x_ref[...]
        r = lax.rsqrt(jnp.mean(xv * xv, axis=-1, keepdims=True) + EPS)
        xhat = xv * r
        dhv = dh_ref[...]
        dxhat = dhv * g_ref[...]
        dx = dres_ref[...] + r * (dxhat - xhat * jnp.mean(dxhat * xhat, axis=-1, keepdims=True))
        dx_ref[...] = dx
        if want_bf16:
            outs[1][...] = dx.astype(BF16)

        @pl.when(pl.program_id(0) == 0)
        def _():
            dg_ref[...] = jnp.zeros_like(dg_ref)

        dg_ref[...] += jnp.sum(dhv * xhat, axis=0, keepdims=True)

    blk = pl.BlockSpec((tb, d), lambda i: (i, 0))
    row = pl.BlockSpec((1, d), lambda i: (0, 0))
    out_shape = [jax.ShapeDtypeStruct((t, d), F32)]
    out_specs = [blk]
    if want_bf16:
        out_shape.append(jax.ShapeDtypeStruct((t, d), BF16))
        out_specs.append(blk)
    out_shape.append(jax.ShapeDtypeStruct((1, d), F32))
    out_specs.append(row)
    body, in_specs, operands = _dep_args(
        body, [blk, row, blk, blk], [x, g.reshape(1, d), dh, dres], dep)
    return pl.pallas_call(
        body,
        out_shape=out_shape,
        grid=(t // tb,),
        in_specs=in_specs,
        out_specs=out_specs,
        compiler_params=_cparams(("arbitrary",), 2 * tb * d * 18),
        name=name,
    )(*operands)


def _loss_head(x3, g, target):
    t, d = x3.shape
    tb = _pick(t, 256, SUBLANE)

    def body(x_ref, g_ref, t_ref, loss_ref, dx_ref, dxb_ref, dg_ref):
        xv = x_ref[...]
        gv = g_ref[...]
        r = lax.rsqrt(jnp.mean(xv * xv, axis=-1, keepdims=True) + EPS)
        xhat = xv * r
        err = xhat * gv - t_ref[...]
        dy = err * (1.0 / d)
        dxhat = dy * gv
        dx = r * (dxhat - xhat * jnp.mean(dxhat * xhat, axis=-1, keepdims=True))
        dx_ref[...] = dx
        dxb_ref[...] = dx.astype(BF16)

        @pl.when(pl.program_id(0) == 0)
        def _():
            dg_ref[...] = jnp.zeros_like(dg_ref)
            loss_ref[...] = jnp.zeros_like(loss_ref)

        dg_ref[...] += jnp.sum(dy * xhat, axis=0, keepdims=True)
        per_token = jnp.mean(err * err, axis=-1, keepdims=True)
        loss_ref[...] += 0.5 * jnp.sum(per_token, axis=0, keepdims=True)

    blk = pl.BlockSpec((tb, d), lambda i: (i, 0))
    row = pl.BlockSpec((1, d), lambda i: (0, 0))
    return pl.pallas_call(
        body,
        out_shape=[jax.ShapeDtypeStruct((1, 1), F32), jax.ShapeDtypeStruct((t, d), F32),
                   jax.ShapeDtypeStruct((t, d), BF16), jax.ShapeDtypeStruct((1, d), F32)],
        grid=(t // tb,),
        in_specs=[blk, row, blk],
        out_specs=[pl.BlockSpec((1, 1), lambda i: (0, 0)), blk, blk, row],
        compiler_params=_cparams(("arbitrary",), 2 * tb * d * 14),
        name="loss_head",
    )(x3, g.reshape(1, d), target)


def _lru_gates(xc, wa, ba, wx, bx, lam):
    nn = (((1,), (0,)), ((), ()))
    xcb = xc.astype(BF16)
    r = _sigmoid(lax.dot_general(xcb, wa, nn, preferred_element_type=F32) + ba)
    i = _sigmoid(lax.dot_general(xcb, wx, nn, preferred_element_type=F32) + bx)
    cl = -LRU_C * _softplus(-lam)
    log_a = cl * r
    a = jnp.exp(log_a)
    one_minus_a2 = _neg_expm1(2.0 * log_a)
    return xcb, r, i, a, one_minus_a2, cl


def _lru_fwd(p, conv_w, conv_b, wa_bd, ba, wx_bd, bx, lam, d_lru, gc, tc):
    t = p.shape[0]
    ng = d_lru // gc
    nt = t // tc
    width = conv_w.shape[0]

    def body(lx_ref, gate_ref, cw_ref, cb_ref, wa_ref, ba_ref, wx_ref, bx_ref, lam_ref,
             y_ref, h_ref, halo, hcar, a_s, u_s):
        @pl.when(pl.program_id(1) == 0)
        def _():
            halo[...] = jnp.zeros_like(halo)
            hcar[...] = jnp.zeros_like(hcar)

        x = lx_ref[...]
        cat = jnp.concatenate([halo[...], x], axis=0)
        halo[...] = x[tc - SUBLANE:, :]
        xc = _conv_fwd(cat, cw_ref[...], width) + cb_ref[...]
        _, r, i, a, om, _ = _lru_gates(xc, wa_ref[...], ba_ref[...], wx_ref[...], bx_ref[...], lam_ref[...])
        a_s[...] = a
        u_s[...] = jnp.sqrt(om) * (i * xc)
        hcar[0:1, :] = _scan_tiles(a_s, u_s, h_ref, hcar[0:1, :], tc, reverse=False)
        gl, _ = _gelu_and_grad(gate_ref[...])
        y_ref[...] = (gl * h_ref[...]).astype(BF16)

    blk = lambda off: pl.BlockSpec((tc, gc), lambda g, s, off=off: (s, off + g))
    rowv = lambda rows: pl.BlockSpec((rows, gc), lambda g, s: (0, g))
    wspec = pl.BlockSpec((None, gc, gc), lambda g, s: (g, 0, 0))
    out_blk = pl.BlockSpec((tc, gc), lambda g, s: (s, g))
    return pl.pallas_call(
        body,
        out_shape=[jax.ShapeDtypeStruct((t, d_lru), BF16), jax.ShapeDtypeStruct((t, d_lru), F32)],
        grid=(ng, nt),
        in_specs=[blk(0), blk(ng), rowv(width), rowv(1), wspec, rowv(1), wspec, rowv(1), rowv(1)],
        out_specs=[out_blk, out_blk],
        scratch_shapes=[pltpu.VMEM((SUBLANE, gc), F32), pltpu.VMEM((SUBLANE, gc), F32),
                        pltpu.VMEM((tc, gc), F32), pltpu.VMEM((tc, gc), F32)],
        compiler_params=_cparams(("parallel", "arbitrary"), 40 * tc * gc * 4),
        name="lru_fwd",
    )(p, p, conv_w, conv_b.reshape(1, -1), wa_bd, ba.reshape(1, -1), wx_bd, bx.reshape(1, -1),
      lam.reshape(1, -1))


def _lru_bwd(p, hseq, dyp, conv_w, conv_b, wa_bd, ba, wx_bd, bx, lam, d_lru, gc, tc, dp, slab0, dep=None):
    t = p.shape[0]
    ng = d_lru // gc
    nt = t // tc
    width = conv_w.shape[0]
    halo_blocks = tc // SUBLANE
    nn = (((1,), (0,)), ((), ()))
    nt_dims = (((1,), (1,)), ((), ()))
    tn_dims = (((0,), (0,)), ((), ()))

    def body(lx_ref, lxh_ref, gate_ref, h_ref, hh_ref, dyp_ref,
             cw_ref, cb_ref, wa_ref, ba_ref, wx_ref, bx_ref, lam_ref,
             dp_ref, dcw_ref, dcb_ref, dwa_ref, dba_ref, dwx_ref, dbx_ref, dlam_ref,
             nxt_dxc, nxt_a, nxt_g, al_s, b_s, g_s):
        s = pl.program_id(1)
        first_chunk = s == nt - 1

        @pl.when(s == 0)
        def _():
            nxt_dxc[...] = jnp.zeros_like(nxt_dxc)
            nxt_a[...] = jnp.zeros_like(nxt_a)
            nxt_g[...] = jnp.zeros_like(nxt_g)
            for ref in (dcw_ref, dcb_ref, dwa_ref, dba_ref, dwx_ref, dbx_ref, dlam_ref):
                ref[...] = jnp.zeros_like(ref)

        keep = jnp.where(first_chunk, 0.0, 1.0)
        x = lx_ref[...]
        catx = jnp.concatenate([lxh_ref[...] * keep, x], axis=0)
        cw = cw_ref[...]
        xc = _conv_fwd(catx, cw, width) + cb_ref[...]
        wa = wa_ref[...]
        wx = wx_ref[...]
        lam_v = lam_ref[...]
        xcb, r, i, a, om, cl = _lru_gates(xc, wa, ba_ref[...], wx, bx_ref[...], lam_v)
        mult = jnp.sqrt(om)

        h = h_ref[...]
        hprev = _rows_before(jnp.concatenate([hh_ref[...] * keep, h], axis=0), 1)
        gl, dgl = _gelu_and_grad(gate_ref[...])
        dyp_v = dyp_ref[...]
        dp_ref[1] = (dyp_v * h * dgl).astype(BF16)

        al_s[...] = _rows_after(jnp.concatenate([a, nxt_a[...]], axis=0), 1)
        b_s[...] = dyp_v * gl
        nxt_g[0:1, :] = _scan_tiles(al_s, b_s, g_s, nxt_g[0:1, :], tc, reverse=True)
        nxt_a[...] = a[0:SUBLANE, :]
        du = g_s[...]

        da = du * hprev
        dmult = du * (i * xc)
        di = du * mult * xc
        dxc = du * mult * i
        dlog_a = da * a - dmult * (a * a / mult)
        dlam_ref[...] += jnp.sum(dlog_a * r, axis=0, keepdims=True) * (LRU_C * _sigmoid(-lam_v))
        dza = (dlog_a * cl) * r * (1.0 - r)
        dzx = di * i * (1.0 - i)
        dba_ref[...] += jnp.sum(dza, axis=0, keepdims=True)
        dbx_ref[...] += jnp.sum(dzx, axis=0, keepdims=True)
        dzab = dza.astype(BF16)
        dzxb = dzx.astype(BF16)
        dwa_ref[...] += lax.dot_general(xcb, dzab, tn_dims, preferred_element_type=F32)
        dwx_ref[...] += lax.dot_general(xcb, dzxb, tn_dims, preferred_element_type=F32)
        dxc = dxc + lax.dot_general(dzab, wa, nt_dims, preferred_element_type=F32)
        dxc = dxc + lax.dot_general(dzxb, wx, nt_dims, preferred_element_type=F32)
        dcb_ref[...] += jnp.sum(dxc, axis=0, keepdims=True)
        _conv_bwd_weight(dcw_ref, dxc, catx, width)
        catd = jnp.concatenate([dxc, nxt_dxc[...]], axis=0)
        dp_ref[0] = _conv_bwd_input(catd, cw, width).astype(BF16)
        nxt_dxc[...] = dxc[0:SUBLANE, :]

    rev = lambda s: nt - 1 - s
    blk = lambda off: pl.BlockSpec((tc, gc), lambda g, s, off=off: (rev(s), off + g))
    halo = lambda off: pl.BlockSpec(
        (SUBLANE, gc), lambda g, s, off=off: (jnp.maximum(rev(s) * halo_blocks - 1, 0), off + g))
    rowv = lambda rows: pl.BlockSpec((rows, gc), lambda g, s: (0, g))
    wspec = pl.BlockSpec((None, gc, gc), lambda g, s: (g, 0, 0))
    out_blk = pl.BlockSpec((tc, gc), lambda g, s: (rev(s), g))
    vec = lambda rows: jax.ShapeDtypeStruct((rows, d_lru), F32)
    wshape = jax.ShapeDtypeStruct((ng, gc, gc), F32)
    body, in_specs, operands = _dep_args(
        body,
        [blk(0), halo(0), blk(ng), blk(0), halo(0), blk(0),
         rowv(width), rowv(1), wspec, rowv(1), wspec, rowv(1), rowv(1)],
        [p, p, p, hseq, hseq, dyp,
         conv_w, conv_b.reshape(1, -1), wa_bd, ba.reshape(1, -1), wx_bd,
         bx.reshape(1, -1), lam.reshape(1, -1)], dp, dep)
    assert dp.shape[2] == d_lru and slab0 % 2 == 0
    return pl.pallas_call(
        body,
        out_shape=[jax.ShapeDtypeStruct(dp.shape, dp.dtype),
                   vec(width), vec(1), wshape, vec(1), wshape, vec(1), vec(1)],
        grid=(ng, nt),
        in_specs=in_specs,
        out_specs=[pl.BlockSpec((2, tc, gc), lambda g, s: (slab0 // 2, rev(s), g)),
                   rowv(width), rowv(1), wspec, rowv(1), wspec, rowv(1), rowv(1)],
        input_output_aliases={13: 0},
        scratch_shapes=[pltpu.VMEM((SUBLANE, gc), F32), pltpu.VMEM((SUBLANE, gc), F32),
                        pltpu.VMEM((SUBLANE, gc), F32),
                        pltpu.VMEM((tc, gc), F32), pltpu.VMEM((tc, gc), F32), pltpu.VMEM((tc, gc), F32)],
        compiler_params=_cparams(("parallel", "arbitrary"), 80 * tc * gc * 4),
        name="lru_bwd",
    )(*operands)


def _sc_fwd(p, conv_w, col0, d_sc, cb, tc):
    t = p.shape[0]
    nc = d_sc // cb
    nt = t // tc
    width = conv_w.shape[0]
    base = col0 // cb

    def body(b_ref, c_ref, v_ref, w_ref, y_ref, halo):
        @pl.when(pl.program_id(1) == 0)
        def _():
            halo[...] = jnp.zeros_like(halo)

        cv = c_ref[...] * v_ref[...]
        cat = jnp.concatenate([halo[...], cv], axis=0)
        halo[...] = cv[tc - SUBLANE:, :]
        y_ref[...] = (b_ref[...] * _conv_fwd(cat, w_ref[...], width)).astype(BF16)

    blk = lambda slab: pl.BlockSpec((tc, cb), lambda j, s, slab=slab: (s, base + slab * nc + j))
    return pl.pallas_call(
        body,
        out_shape=jax.ShapeDtypeStruct((t, d_sc), BF16),
        grid=(nc, nt),
        in_specs=[blk(0), blk(1), blk(2), pl.BlockSpec((width, cb), lambda j, s: (0, j))],
        out_specs=pl.BlockSpec((tc, cb), lambda j, s: (s, j)),
        scratch_shapes=[pltpu.VMEM((SUBLANE, cb), F32)],
        compiler_params=_cparams(("parallel", "arbitrary"), 20 * tc * cb * 4),
        name="sc_fwd",
    )(p, p, p, conv_w)


def _sc_bwd(p, dyp, conv_w, col0, d_sc, cb, tc, dp, slab0):
    t = p.shape[0]
    nc = d_sc // cb
    nt = t // tc
    width = conv_w.shape[0]
    base = col0 // cb
    halo_blocks = tc // SUBLANE

    def body(b_ref, c_ref, ch_ref, v_ref, vh_ref, dyp_ref, w_ref,
             dp_ref, dw_ref, nxt_dq):
        s = pl.program_id(1)

        @pl.when(s == 0)
        def _():
            nxt_dq[...] = jnp.zeros_like(nxt_dq)
            dw_ref[...] = jnp.zeros_like(dw_ref)

        keep = jnp.where(s == nt - 1, 0.0, 1.0)
        cvals = c_ref[...]
        vvals = v_ref[...]
        w = w_ref[...]
        catcv = jnp.concatenate([ch_ref[...] * vh_ref[...] * keep, cvals * vvals], axis=0)
        q = _conv_fwd(catcv, w, width)
        dyp_v = dyp_ref[...]
        dp_ref[0] = (dyp_v * q).astype(BF16)
        dq = dyp_v * b_ref[...]
        _conv_bwd_weight(dw_ref, dq, catcv, width)
        dcv = _conv_bwd_input(jnp.concatenate([dq, nxt_dq[...]], axis=0), w, width)
        nxt_dq[...] = dq[0:SUBLANE, :]
        dp_ref[1] = (dcv * vvals).astype(BF16)
        dp_ref[2] = (dcv * cvals).astype(BF16)

    rev = lambda s: nt - 1 - s
    blk = lambda slab: pl.BlockSpec((tc, cb), lambda j, s, slab=slab: (rev(s), base + slab * nc + j))
    halo = lambda slab: pl.BlockSpec(
        (SUBLANE, cb),
        lambda j, s, slab=slab: (jnp.maximum(rev(s) * halo_blocks - 1, 0), base + slab * nc + j))
    out_blk = pl.BlockSpec((tc, cb), lambda j, s: (rev(s), j))
    wblk = pl.BlockSpec((width, cb), lambda j, s: (0, j))
    assert dp.shape[2] == d_sc and slab0 % 3 == 0
    operands = [p, p, p, p, p, dyp, conv_w]
    body, in_specs, operands = _dep_args(
        body, [blk(0), blk(1), halo(1), blk(2), halo(2), out_blk, wblk], operands, dp)
    return pl.pallas_call(
        body,
        out_shape=[jax.ShapeDtypeStruct(dp.shape, dp.dtype), jax.ShapeDtypeStruct((width, d_sc), F32)],
        grid=(nc, nt),
        in_specs=in_specs,
        out_specs=[pl.BlockSpec((3, tc, cb), lambda j, s: (slab0 // 3, rev(s), j)), wblk],
        input_output_aliases={7: 0},
        scratch_shapes=[pltpu.VMEM((SUBLANE, cb), F32)],
        compiler_params=_cparams(("parallel", "arbitrary"), 30 * tc * cb * 4),
        name="sc_bwd",
    )(*operands)


def _merge_fwd(p, y_lru, y_sc, col0, tc):
    t, d = y_lru.shape
    cb = _pick(math.gcd(d, col0), 1024)
    nc = d // cb
    base = col0 // cb

    def body(gl_ref, gs_ref, yl_ref, ys_ref, o_ref):
        o_ref[...] = (_sigmoid(gl_ref[...]) * yl_ref[...] + _sigmoid(gs_ref[...]) * ys_ref[...]).astype(BF16)

    gate = lambda slab: pl.BlockSpec((tc, cb), lambda s, j, slab=slab: (s, base + slab * nc + j))
    blk = pl.BlockSpec((tc, cb), lambda s, j: (s, j))
    return pl.pallas_call(
        body,
        out_shape=jax.ShapeDtypeStruct((t, d), BF16),
        grid=(t // tc, nc),
        in_specs=[gate(0), gate(1), blk, blk],
        out_specs=blk,
        compiler_params=_cparams(("parallel", "parallel"), 2 * tc * cb * 20),
        name="merge_fwd",
    )(p, p, y_lru, y_sc)


def _merge_bwd(p, y_lru, y_sc, dmerged, col0, tc, n_slabs):
    t, d = y_lru.shape
    cb = _pick(math.gcd(d, col0), 1024)
    nc = d // cb
    base = col0 // cb

    def body(gl_ref, gs_ref, yl_ref, ys_ref, dm_ref, dp_ref, dyl_ref, dys_ref):
        dm = dm_ref[...]
        sl = _sigmoid(gl_ref[...])
        ss = _sigmoid(gs_ref[...])
        dp_ref[0] = (dm * yl_ref[...] * (sl * (1.0 - sl))).astype(BF16)
        dp_ref[1] = (dm * ys_ref[...] * (ss * (1.0 - ss))).astype(BF16)
        dyl_ref[...] = (dm * sl).astype(BF16)
        dys_ref[...] = (dm * ss).astype(BF16)

    gate = lambda slab: pl.BlockSpec((tc, cb), lambda s, j, slab=slab: (s, base + slab * nc + j))
    blk = pl.BlockSpec((tc, cb), lambda s, j: (s, j))
    act = jax.ShapeDtypeStruct((t, d), BF16)
    return pl.pallas_call(
        body,
        out_shape=[jax.ShapeDtypeStruct((n_slabs, t, cb), BF16), act, act],
        grid=(t // tc, nc),
        in_specs=[gate(0), gate(1), blk, blk, blk],
        out_specs=[pl.BlockSpec((2, tc, cb), lambda s, j: (j, s, 0)), blk, blk],
        compiler_params=_cparams(("parallel", "parallel"), 2 * tc * cb * 28),
        name="merge_bwd",
    )(p, p, y_lru, y_sc, dmerged)


def _ffn_act_fwd(up, conv_w, d_ff, cb, tc):
    t = up.shape[0]
    nc = d_ff // cb
    nt = t // tc
    width = conv_w.shape[0]

    def body(g_ref, v_ref, wg_ref, wv_ref, o_ref, halo_g, halo_v):
        @pl.when(pl.program_id(1) == 0)
        def _():
            halo_g[...] = jnp.zeros_like(halo_g)
            halo_v[...] = jnp.zeros_like(halo_v)

        g = g_ref[...]
        v = v_ref[...]
        ug = _conv_fwd(jnp.concatenate([halo_g[...], g], axis=0), wg_ref[...], width)
        uv = _conv_fwd(jnp.concatenate([halo_v[...], v], axis=0), wv_ref[...], width)
        halo_g[...] = g[tc - SUBLANE:, :]
        halo_v[...] = v[tc - SUBLANE:, :]
        o_ref[...] = (ug * _sigmoid(ug) * uv).astype(BF16)

    blk = lambda half: pl.BlockSpec((tc, cb), lambda j, s, half=half: (s, half * nc + j))
    wblk = lambda half: pl.BlockSpec((width, cb), lambda j, s, half=half: (0, half * nc + j))
    return pl.pallas_call(
        body,
        out_shape=jax.ShapeDtypeStruct((t, d_ff), BF16),
        grid=(nc, nt),
        in_specs=[blk(0), blk(1), wblk(0), wblk(1)],
        out_specs=pl.BlockSpec((tc, cb), lambda j, s: (s, j)),
        scratch_shapes=[pltpu.VMEM((SUBLANE, cb), F32), pltpu.VMEM((SUBLANE, cb), F32)],
        compiler_params=_cparams(("parallel", "arbitrary"), 24 * tc * cb * 4),
        name="ffn_act_fwd",
    )(up, up, conv_w, conv_w)


def _ffn_act_bwd(up, dact, conv_w, d_ff, cb, tc, dep=None):
    t = up.shape[0]
    nc = d_ff // cb
    nt = t // tc
    width = conv_w.shape[0]
    halo_blocks = tc // SUBLANE

    def body(g_ref, gh_ref, v_ref, vh_ref, da_ref, wg_ref, wv_ref,
             dup_ref, dwg_ref, dwv_ref, nxt_g, nxt_v):
        s = pl.program_id(1)

        @pl.when(s == 0)
        def _():
            nxt_g[...] = jnp.zeros_like(nxt_g)
            nxt_v[...] = jnp.zeros_like(nxt_v)
            dwg_ref[...] = jnp.zeros_like(dwg_ref)
            dwv_ref[...] = jnp.zeros_like(dwv_ref)

        keep = jnp.where(s == nt - 1, 0.0, 1.0)
        wg = wg_ref[...]
        wv = wv_ref[...]
        catg = jnp.concatenate([gh_ref[...] * keep, g_ref[...]], axis=0)
        catv = jnp.concatenate([vh_ref[...] * keep, v_ref[...]], axis=0)
        ug = _conv_fwd(catg, wg, width)
        uv = _conv_fwd(catv, wv, width)
        sg = _sigmoid(ug)
        da = da_ref[...]
        duv = da * (ug * sg)
        dup_ref[1] = _conv_bwd_input(jnp.concatenate([duv, nxt_v[...]], axis=0), wv, width).astype(BF16)
        nxt_v[...] = duv[0:SUBLANE, :]
        _conv_bwd_weight(dwv_ref, duv, catv, width)
        dug = da * uv * (sg * (1.0 + ug * (1.0 - sg)))
        dup_ref[0] = _conv_bwd_input(jnp.concatenate([dug, nxt_g[...]], axis=0), wg, width).astype(BF16)
        nxt_g[...] = dug[0:SUBLANE, :]
        _conv_bwd_weight(dwg_ref, dug, catg, width)

    rev = lambda s: nt - 1 - s
    blk = lambda half: pl.BlockSpec((tc, cb), lambda j, s, half=half: (rev(s), half * nc + j))
    halo = lambda half: pl.BlockSpec(
        (SUBLANE, cb), lambda j, s, half=half: (jnp.maximum(rev(s) * halo_blocks - 1, 0), half * nc + j))
    wblk = lambda half: pl.BlockSpec((width, cb), lambda j, s, half=half: (0, half * nc + j))
    out_blk = pl.BlockSpec((tc, cb), lambda j, s: (rev(s), j))
    wout = pl.BlockSpec((width, cb), lambda j, s: (0, j))
    act = jax.ShapeDtypeStruct((t, d_ff), BF16)
    wshape = jax.ShapeDtypeStruct((width, d_ff), F32)
    body, in_specs, operands = _dep_args(
        body, [blk(0), halo(0), blk(1), halo(1), out_blk, wblk(0), wblk(1)],
        [up, up, up, up, dact, conv_w, conv_w], dep)
    return pl.pallas_call(
        body,
        out_shape=[jax.ShapeDtypeStruct((2, t, d_ff), BF16), wshape, wshape],
        grid=(nc, nt),
        in_specs=in_specs,
        out_specs=[pl.BlockSpec((2, tc, cb), lambda j, s: (0, rev(s), j)), wout, wout],
        scratch_shapes=[pltpu.VMEM((SUBLANE, cb), F32), pltpu.VMEM((SUBLANE, cb), F32)],
        compiler_params=_cparams(("parallel", "arbitrary"), 40 * tc * cb * 4),
        name="ffn_act_bwd",
    )(*operands)


def _mesh_pos():
    x, y, c = lax.axis_index("x"), lax.axis_index("y"), lax.axis_index("c")
    return x, y, c


def _other_chips(x, y):
    return [(1 - x, y), (x, 1 - y), (1 - x, 1 - y)]


def _cast_place(w, chip, col_sharded, name, dep=None):
    r, cdim = w.shape
    full = (r, cdim * N_CHIPS) if col_sharded else (r * N_CHIPS, cdim)
    rb = _pick(r, max(BF16_ROWS, (512 * 1024) // cdim), BF16_ROWS)
    nb = r // rb

    def body(chip_ref, w_ref, o_ref):
        o_ref[...] = w_ref[...].astype(BF16)

    if col_sharded:
        out_map = lambda i, chip_ref: (i, chip_ref[0])
    else:
        out_map = lambda i, chip_ref: (chip_ref[0] * nb + i, 0)
    grid_spec = pltpu.PrefetchScalarGridSpec(
        num_scalar_prefetch=1,
        grid=(nb,),
        in_specs=[pl.BlockSpec((rb, cdim), lambda i, chip_ref: (i, 0))] + ([ANY] if dep is not None else []),
        out_specs=pl.BlockSpec((rb, cdim), out_map),
    )
    body, _, operands = _dep_args(body, [], [chip, w], dep)
    return pl.pallas_call(
        body,
        out_shape=jax.ShapeDtypeStruct(full, BF16),
        grid_spec=grid_spec,
        compiler_params=_cparams(("parallel",), 2 * rb * cdim * 6),
        name=name,
    )(*operands)


def _remote(src, dst, send_sems, recv_sems, idx, to):
    return pltpu.make_async_remote_copy(
        src_ref=src, dst_ref=dst, send_sem=send_sems.at[idx], recv_sem=recv_sems.at[idx],
        device_id=to, device_id_type=MESH)


def _exchange(name, arrays, n_sems, plan):
    n = len(arrays)

    def body(*refs):
        bufs = refs[n:2 * n]
        send_sems, recv_sems = refs[2 * n:]
        sends, arrivals = plan(bufs, send_sems, recv_sems)
        for cp in sends:
            cp.start()
        for cp in arrivals:
            cp.wait_recv()
        for cp in sends:
            cp.wait_send()

    outs = pl.pallas_call(
        body,
        out_shape=[jax.ShapeDtypeStruct(a.shape, a.dtype) for a in arrays],
        in_specs=[ANY] * n,
        out_specs=[ANY] * n,
        input_output_aliases={k: k for k in range(n)},
        scratch_shapes=[pltpu.SemaphoreType.DMA((n_sems,)), pltpu.SemaphoreType.DMA((n_sems,))],
        name=name,
    )(*arrays)
    return list(outs)


def _exchange_start(name, arrays, n_sems, plan, after=None):
    n = len(arrays)
    n_in = n + (after is not None)

    def body(*refs):
        bufs = refs[:n]
        send_sems, recv_sems = refs[n_in], refs[n_in + 1]
        token = refs[-1]
        sends, _ = plan(bufs, send_sems, recv_sems)
        for cp in sends:
            cp.start()
        token[...] = jnp.zeros_like(token)

    out = pl.pallas_call(
        body,
        out_shape=(pltpu.SemaphoreType.DMA((n_sems,)), pltpu.SemaphoreType.DMA((n_sems,)),
                   *[pltpu.HBM(a.shape, a.dtype) for a in arrays],
                   jax.ShapeDtypeStruct((SUBLANE, LANE), F32)),
        in_specs=[HBM_SPEC] * n + [ANY] * (n_in - n),
        out_specs=(SEM_SPEC, SEM_SPEC, *[HBM_SPEC] * n, VMEM_SPEC),
        input_output_aliases={k: 2 + k for k in range(n)},
        compiler_params=pltpu.CompilerParams(has_side_effects=DATAFLOW_EFFECT),
        name=name,
    )(*[pltpu.with_memory_space_constraint(a, pltpu.HBM) for a in arrays], *([after] if after is not None else []))
    return out[0], out[1], list(out[2:2 + n]), out[-1]


def _exchange_wait(name, arrays, send_sems, recv_sems, after, plan):
    n = len(arrays)

    def body(*refs):
        bufs = refs[:n]
        sends, arrivals = plan(bufs, refs[n], refs[n + 1])
        for cp in arrivals:
            cp.wait_recv()
        for cp in sends:
            cp.wait_send()

    outs = pl.pallas_call(
        body,
        out_shape=[pltpu.HBM(a.shape, a.dtype) for a in arrays],
        in_specs=[HBM_SPEC] * n + [SEM_SPEC, SEM_SPEC, ANY],
        out_specs=[HBM_SPEC] * n,
        input_output_aliases={k: k for k in range(n)},
        compiler_params=pltpu.CompilerParams(has_side_effects=DATAFLOW_EFFECT),
        name=name,
    )(*arrays, send_sems, recv_sems, after)
    return list(outs)


def _half_block(ref, shard_shape, col_sharded, chip, half):
    r, cdim = shard_shape
    h = r // 2
    if col_sharded:
        return ref.at[pl.ds(pl.multiple_of(half * h, BF16_ROWS), h),
                      pl.ds(pl.multiple_of(chip * cdim, LANE), cdim)]
    return ref.at[pl.ds(pl.multiple_of(chip * r + half * h, BF16_ROWS), h), :]


def _gather_plan(shard_shapes, col_sharded, ks):
    def plan(bufs, send_sems, recv_sems):
        x, y, c = _mesh_pos()
        sends, arrivals = [], []
        for ref, k in zip(bufs, ks):
            mine = _half_block(ref, shard_shapes[k], col_sharded[k], 2 * x + y, c)
            for j, (px, py) in enumerate(_other_chips(x, y)):
                landed = _half_block(ref, shard_shapes[k], col_sharded[k], 2 * px + py, c)
                sends.append(_remote(mine, mine, send_sems, recv_sems, 3 * k + j, (px, py, c)))
                arrivals.append(_remote(landed, landed, send_sems, recv_sems, 3 * k + j, (px, py, c)))
        return sends, arrivals
    return plan


def _forward_plan(shard_shapes, col_sharded, ks):
    def plan(bufs, send_sems, recv_sems):
        x, y, c = _mesh_pos()
        sends, arrivals = [], []
        for i, (ref, k) in enumerate(zip(bufs, ks)):
            for j, (px, py) in enumerate(_other_chips(x, y)):
                landed = _half_block(ref, shard_shapes[k], col_sharded[k], 2 * px + py, c)
                theirs = _half_block(ref, shard_shapes[k], col_sharded[k], 2 * px + py, 1 - c)
                sends.append(_remote(landed, landed, send_sems, recv_sems, 3 * i + j, (x, y, 1 - c)))
                arrivals.append(_remote(theirs, theirs, send_sems, recv_sems, 3 * i + j, (x, y, 1 - c)))
        return sends, arrivals
    return plan


def _small_gather(small):
    def body(small_ref, out_ref, send_sems, recv_sems):
        x, y, c = _mesh_pos()
        me = 2 * x + y
        out_ref[me] = small_ref[...]
        copies = []
        for j, (px, py) in enumerate(_other_chips(x, y)):
            cp = _remote(small_ref, out_ref.at[me], send_sems, recv_sems, j, (px, py, c))
            cp.start()
            copies.append(cp)
        for j, (px, py) in enumerate(_other_chips(x, y)):
            _remote(small_ref, out_ref.at[2 * px + py], send_sems, recv_sems, j, (px, py, c)).wait_recv()
        for cp in copies:
            cp.wait_send()

    return pl.pallas_call(
        body,
        out_shape=jax.ShapeDtypeStruct((N_CHIPS,) + small.shape, small.dtype),
        in_specs=[VMEM_SPEC],
        out_specs=VMEM_SPEC,
        scratch_shapes=[pltpu.SemaphoreType.DMA((N_CHIPS - 1,)), pltpu.SemaphoreType.DMA((N_CHIPS - 1,))],
        name="gather_small",
    )(small)


def _as3d(g, col_sharded):
    r, cdim = g.shape
    return g.reshape(1, r, cdim) if col_sharded else g.reshape(N_CHIPS, r // N_CHIPS, cdim)


def _pair_plan(m):
    def plan(bufs, send_sems, recv_sems):
        x, y, c = _mesh_pos()
        copies = []
        for i in range(m):
            h = bufs[i].shape[1] // 2
            src = bufs[i].at[:, pl.ds(pl.multiple_of((1 - c) * h, BF16_ROWS), h), :]
            copies.append(_remote(src, bufs[m + i], send_sems, recv_sems, i, (x, y, 1 - c)))
        return copies, copies
    return plan


def _chip_plan(col_flags):
    m = len(col_flags)

    def plan(bufs, send_sems, recv_sems):
        x, y, c = _mesh_pos()
        copies = []
        for i in range(m):
            land = bufs[m + i]
            width = land.shape[2]
            for j, (px, py) in enumerate(_other_chips(x, y)):
                q = 2 * px + py
                if col_flags[i]:
                    src = bufs[i].at[0, :, pl.ds(pl.multiple_of(q * width, LANE), width)]
                else:
                    src = bufs[i].at[q]
                copies.append(_remote(src, land.at[j], send_sems, recv_sems, 3 * i + j, (px, py, c)))
        return copies, copies
    return plan


def _share_plan(m):
    def plan(bufs, send_sems, recv_sems):
        x, y, c = _mesh_pos()
        sends, arrivals = [], []
        for i in range(m):
            h = bufs[i].shape[0] // 2
            mine = bufs[i].at[pl.ds(pl.multiple_of(c * h, SUBLANE), h), :]
            theirs = bufs[i].at[pl.ds(pl.multiple_of((1 - c) * h, SUBLANE), h), :]
            sends.append(_remote(mine, mine, send_sems, recv_sems, i, (x, y, 1 - c)))
            arrivals.append(_remote(theirs, theirs, send_sems, recv_sems, i, (x, y, 1 - c)))
        return sends, arrivals
    return plan


def _pair_add(g3, other, core):
    a, r, cdim = g3.shape
    h = r // 2
    rb = _pick(h, max(BF16_ROWS, (512 * 1024) // cdim), BF16_ROWS)
    nb = h // rb

    def body(core_ref, g_ref, o_ref, out_ref):
        out_ref[...] = (g_ref[...].astype(F32) + o_ref[...].astype(F32)).astype(BF16)

    grid_spec = pltpu.PrefetchScalarGridSpec(
        num_scalar_prefetch=1,
        grid=(a, nb),
        in_specs=[pl.BlockSpec((None, rb, cdim), lambda i, j, core_ref: (i, core_ref[0] * nb + j, 0)),
                  pl.BlockSpec((None, rb, cdim), lambda i, j, core_ref: (i, j, 0))],
        out_specs=pl.BlockSpec((None, rb, cdim), lambda i, j, core_ref: (i, j, 0)),
    )
    return pl.pallas_call(
        body,
        out_shape=jax.ShapeDtypeStruct((a, h, cdim), BF16),
        grid_spec=grid_spec,
        compiler_params=_cparams(("parallel", "parallel"), 2 * rb * cdim * 10),
        name="grad_pair_add",
    )(core, g3, other)


def _small_allreduce(small):
    rows = small.shape[0]
    pad = (-rows) % (2 * SUBLANE)
    if pad:
        small = jnp.pad(small, ((0, pad), (0, 0)))
    h = small.shape[0] // 2
    half_shape = (h, small.shape[1])

    def body(small_ref, out_ref, theirs, by_chip, send_sems, recv_sems):
        x, y, c = _mesh_pos()
        me = 2 * x + y
        sibling = (x, y, 1 - c)
        mine = pl.ds(pl.multiple_of(c * h, SUBLANE), h)
        other = pl.ds(pl.multiple_of((1 - c) * h, SUBLANE), h)
        swap = _remote(small_ref, theirs, send_sems, recv_sems, 0, sibling)
        swap.start()
        swap.wait()
        by_chip[me] = small_ref[mine, :] + theirs[mine, :]
        copies = []
        for j, (px, py) in enumerate(_other_chips(x, y)):
            cp = _remote(by_chip.at[me], by_chip.at[me], send_sems, recv_sems, 1 + j, (px, py, c))
            cp.start()
            copies.append(cp)
        for j, (px, py) in enumerate(_other_chips(x, y)):
            landed = by_chip.at[2 * px + py]
            _remote(landed, landed, send_sems, recv_sems, 1 + j, (px, py, c)).wait_recv()
        total = by_chip[0]
        for q in range(1, N_CHIPS):
            total = total + by_chip[q]
        out_ref[mine, :] = total
        for cp in copies:
            cp.wait_send()
        share = _remote(out_ref.at[mine, :], out_ref.at[mine, :], send_sems, recv_sems, 4, sibling)
        share.start()
        _remote(out_ref.at[other, :], out_ref.at[other, :], send_sems, recv_sems, 4, sibling).wait_recv()
        share.wait_send()

    out = pl.pallas_call(
        body,
        out_shape=jax.ShapeDtypeStruct(small.shape, F32),
        in_specs=[VMEM_SPEC],
        out_specs=VMEM_SPEC,
        scratch_shapes=[pltpu.VMEM(small.shape, F32), pltpu.VMEM((N_CHIPS,) + half_shape, F32),
                        pltpu.SemaphoreType.DMA((5,)), pltpu.SemaphoreType.DMA((5,))],
        compiler_params=pltpu.CompilerParams(
            vmem_limit_bytes=min(VMEM_BUDGET, 8 * _nbytes(small.shape, F32) + (8 << 20))),
        name="grad_small_allreduce",
    )(small)
    return out[:rows]


def _chip_sum(partial, land, where, col_sharded):
    _, h, cdim = land.shape
    rb = _pick(h, max(BF16_ROWS, (512 * 1024) // cdim), BF16_ROWS)
    nb = h // rb

    def body(where_ref, own_ref, l_ref, o_ref):
        total = own_ref[...].astype(F32)
        for j in range(N_CHIPS - 1):
            total = total + l_ref[j].astype(F32)
        o_ref[...] = total

    if col_sharded:
        own_map = lambda i, w: (0, i, w[0])
    else:
        own_map = lambda i, w: (w[0], i, 0)
    grid_spec = pltpu.PrefetchScalarGridSpec(
        num_scalar_prefetch=1,
        grid=(nb,),
        in_specs=[pl.BlockSpec((None, rb, cdim), own_map),
                  pl.BlockSpec((N_CHIPS - 1, rb, cdim), lambda i, w: (0, i, 0))],
        out_specs=pl.BlockSpec((rb, cdim), lambda i, w: (w[1] * nb + i, 0)),
    )
    return pl.pallas_call(
        body,
        out_shape=jax.ShapeDtypeStruct((2 * h, cdim), F32),
        grid_spec=grid_spec,
        compiler_params=_cparams(("parallel",), 2 * rb * cdim * 12),
        name="grad_chip_sum",
    )(where, partial, land)


def _adamw(w, g, m, v, name, dep=None):
    r, cdim = w.shape
    rb = _pick(r, max(SUBLANE, (256 * 1024) // cdim), SUBLANE)
    c1 = 1.0 - ADAM_B1 ** ADAM_STEP
    c2 = 1.0 - ADAM_B2 ** ADAM_STEP

    def body(w_ref, g_ref, m_ref, v_ref, go_ref, d_ref, mo_ref, vo_ref):
        gv = g_ref[...]
        mn = ADAM_B1 * m_ref[...] + (1.0 - ADAM_B1) * gv
        vn = ADAM_B2 * v_ref[...] + (1.0 - ADAM_B2) * (gv * gv)
        m_hat = mn / c1
        v_hat = vn / c2
        d_ref[...] = -ADAM_LR * (m_hat / (jnp.sqrt(v_hat) + ADAM_EPS) + ADAM_WD * w_ref[...])
        go_ref[...] = gv
        mo_ref[...] = mn
        vo_ref[...] = vn

    blk = pl.BlockSpec((rb, cdim), lambda i: (i, 0))
    shape = jax.ShapeDtypeStruct((r, cdim), F32)
    body, in_specs, operands = _dep_args(body, [blk] * 4, [w, g, m, v], dep)
    return pl.pallas_call(
        body,
        out_shape=[shape] * 4,
        grid=(r // rb,),
        in_specs=in_specs,
        out_specs=[blk] * 4,
        compiler_params=_cparams(("parallel",), 2 * rb * cdim * 4 * 8),
        name=name,
    )(*operands)


def _pack(arrays):
    tile = SUBLANE * LANE
    pieces = []
    for arr in arrays:
        flat = arr.reshape(-1)
        pad = (-flat.shape[0]) % tile
        if pad:
            flat = jnp.concatenate([flat, jnp.zeros((pad,), flat.dtype)])
        pieces.append(flat)
    return jnp.concatenate(pieces).reshape(-1, LANE)


def _unpack(packed, shapes):
    tile = SUBLANE * LANE
    flat = packed.reshape(-1)
    out, off = [], 0
    for shp in shapes:
        size = math.prod(shp)
        out.append(flat[off:off + size].reshape(shp))
        off += size + ((-size) % tile)
    return out


def _block_diag_groups(w, per_group):
    hcount, hd, _ = w.shape
    ng = hcount // per_group
    w4 = w.reshape(ng, per_group, hd, hd)
    eye = jnp.eye(per_group, dtype=w.dtype)
    bd = w4[:, :, :, None, :] * eye[None, :, None, :, None]
    return bd.reshape(ng, per_group * hd, per_group * hd).astype(BF16)


def _diag_blocks(wbd, per_group, hd):
    ng = wbd.shape[0]
    w5 = wbd.reshape(ng, per_group, hd, per_group, hd)
    blocks = [w5[:, i, :, i, :] for i in range(per_group)]
    return jnp.stack(blocks, axis=1).reshape(ng * per_group, hd, hd)


def kernel(x, g_mix, w_in, lru_conv_w, lru_conv_b, lru_wa, lru_ba, lru_wx, lru_bx, lru_lambda, lru_w_out, sc_conv_w, sc_w_out, w_o, g_ffn, ffn_w_up, ffn_conv_w, ffn_w_down, g_final, loss_target, m_g_mix, m_w_in, m_lru_conv_w, m_lru_conv_b, m_lru_wa, m_lru_ba, m_lru_wx, m_lru_bx, m_lru_lambda, m_lru_w_out, m_sc_conv_w, m_sc_w_out, m_w_o, m_g_ffn, m_ffn_w_up, m_ffn_conv_w, m_ffn_w_down, m_g_final, v_g_mix, v_w_in, v_lru_conv_w, v_lru_conv_b, v_lru_wa, v_lru_ba, v_lru_wx, v_lru_bx, v_lru_lambda, v_lru_w_out, v_sc_conv_w, v_sc_w_out, v_w_o, v_g_ffn, v_ffn_w_up, v_ffn_conv_w, v_ffn_w_down, v_g_final):
    seq, d_model = x.shape[1], x.shape[2]
    heads, head_dim, _ = lru_wa.shape
    d_lru = heads * head_dim
    d_sc = sc_w_out.shape[0]
    d_ff = ffn_w_down.shape[0] * N_CHIPS
    assert x.shape[0] == 1 and w_in.shape[1] * N_CHIPS == 2 * d_lru + 3 * d_sc + 2 * d_model
    xs = x.reshape(seq, d_model)
    target = loss_target.reshape(seq, d_model)

    chip = 2 * lax.axis_index("x") + lax.axis_index("y")
    core = lax.axis_index("c").astype(jnp.int32).reshape(1)

    big_w = [w_in, lru_w_out, sc_w_out, w_o, ffn_w_up, ffn_w_down]
    big_m = [m_w_in, m_lru_w_out, m_sc_w_out, m_w_o, m_ffn_w_up, m_ffn_w_down]
    big_v = [v_w_in, v_lru_w_out, v_sc_w_out, v_w_o, v_ffn_w_up, v_ffn_w_down]
    col_sharded = [True, True, True, False, True, False]
    conv_shards = [lru_conv_w, sc_conv_w, ffn_conv_w]
    conv_pack = jnp.concatenate(
        [jnp.pad(w, ((0, SUBLANE - w.shape[0]), (0, 0))) for w in conv_shards], axis=1)
    big_names = ["w_in", "lru_w_out", "sc_w_out", "w_o", "ffn_w_up", "ffn_w_down"]
    chip_arr = chip.astype(jnp.int32).reshape(1)
    placed = [_cast_place(big_w[0], chip_arr, col_sharded[0], "cast_" + big_names[0])]
    conv_all = _small_gather(conv_pack)
    shard_shapes = [w.shape for w in big_w]
    n_big = len(big_w)

    def gather_start(ks, after, tag):
        send, recv, bufs, token = _exchange_start(
            "gather_start_" + tag, [placed[k] for k in ks], 3 * n_big,
            _gather_plan(shard_shapes, col_sharded, ks), after=after)
        return (send, recv, dict(zip(ks, bufs))), token

    def arrived(state, ks, after, tag):
        send, recv, bufs = state
        got = _exchange_wait("gather_wait_" + tag, [bufs[k] for k in ks], send, recv, after,
                             _gather_plan(shard_shapes, col_sharded, ks))
        return _exchange("gather_forward_" + tag, got, 3 * len(ks), _forward_plan(shard_shapes, col_sharded, ks))

    conv_full, off = [], 0
    for w in conv_shards:
        kw, nq = w.shape
        piece = conv_all[:, :kw, off:off + nq]
        conv_full.append(piece.transpose(1, 0, 2).reshape(kw, N_CHIPS * nq))
        off += nq
    lcw, scw, fcw = conv_full

    per_group = max(1, min(heads, 256 // head_dim))
    gc = per_group * head_dim
    wa_bd = _block_diag_groups(lru_wa, per_group)
    wx_bd = _block_diag_groups(lru_wx, per_group)
    tc = _pick(seq, 256, SUBLANE)
    cb_sc = _pick(d_sc, 512)
    cb_ff = _pick(d_ff, 512)
    col_sc = 2 * d_lru
    col_gates = 2 * d_lru + 3 * d_sc

    first, token = gather_start([0], conv_all, "in")
    for k in range(1, n_big):
        placed.append(_cast_place(big_w[k], chip_arr, col_sharded[k], "cast_" + big_names[k], dep=token))
        token = placed[-1]
    h1 = _rms_fwd(xs, g_mix, "rms_mix", dep=token)
    (win_b,) = arrived(first, [0], h1, "in")
    rest, token = gather_start([1, 2, 3, 4, 5], win_b, "rest")
    p = _mm(h1, win_b, "nn", F32, name="mm_in", dep=token)
    wlo_b, wso_b, wo_b = arrived(rest, [1, 2, 3], p, "mix")
    y_lru_pre, hseq = _lru_fwd(p, lcw, lru_conv_b, wa_bd, lru_ba, wx_bd, lru_bx, lru_lambda, d_lru, gc, tc)
    y_sc_pre = _sc_fwd(p, scw, col_sc, d_sc, cb_sc, tc)
    y_lru = _mm(y_lru_pre, wlo_b, "nn", BF16, name="mm_lru_out")
    y_sc = _mm(y_sc_pre, wso_b, "nn", BF16, name="mm_sc_out")
    merged = _merge_fwd(p, y_lru, y_sc, col_gates, tc)
    x2 = _mm(merged, wo_b, "nn", F32, res=xs, name="mm_o")
    (wup_b,) = arrived(rest, [4], x2, "up")
    h2 = _rms_fwd(x2, g_ffn, "rms_ffn")
    up = _mm(h2, wup_b, "nn", F32, name="mm_up")
    (wdn_b,) = arrived(rest, [5], up, "down")
    act = _ffn_act_fwd(up, fcw, d_ff, cb_ff, tc)
    x3 = _mm(act, wdn_b, "nn", F32, res=x2, name="mm_down")
    loss_part, dx3, dx3b, dg_final = _loss_head(x3, g_final, target)

    where = jnp.concatenate([chip_arr, core])

    def reduce_start(grads, flags, tag):
        views = [_as3d(g, cs) for g, cs in zip(grads, flags)]
        lands = [lax.empty((v.shape[0], v.shape[1] // 2, v.shape[2]), v.dtype) for v in views]
        send, recv, bufs, token = _exchange_start("grad_pair_start_" + tag, views + lands, len(views),
                                                  _pair_plan(len(views)))
        return (send, recv, bufs, flags, tag), token

    def reduce_mid(state, after):
        send, recv, bufs, flags, tag = state
        m = len(flags)
        bufs = _exchange_wait("grad_pair_wait_" + tag, bufs, send, recv, after, _pair_plan(m))
        partials = [_pair_add(bufs[i], bufs[m + i], core) for i in range(m)]
        lands = []
        for pz, cs in zip(partials, flags):
            _, h, cdim = pz.shape
            lands.append(lax.empty((N_CHIPS - 1, h, cdim // N_CHIPS if cs else cdim), BF16))
        send, recv, bufs, token = _exchange_start("grad_chip_start_" + tag, partials + lands, 3 * m,
                                                  _chip_plan(flags))
        return (send, recv, bufs, flags, tag), token

    def reduce_end(state, after):
        send, recv, bufs, flags, tag = state
        m = len(flags)
        bufs = _exchange_wait("grad_chip_wait_" + tag, bufs, send, recv, after, _chip_plan(flags))
        return [_chip_sum(bufs[i], bufs[m + i], where, flags[i]) for i in range(m)]

    g_wdn = _mm(act, dx3b, "tn", F32, name="mm_down_dw")
    red_down, token = reduce_start([g_wdn], [False], "down")
    dact = _mm(dx3b, wdn_b, "nt", F32, name="mm_down_dx", dep=token)
    red_down, token = reduce_mid(red_down, dact)
    dup, dfcw_g, dfcw_v = _ffn_act_bwd(up, dact, fcw, d_ff, cb_ff, tc, dep=token)
    g_wup = _mm(h2, dup, "tn", BF16, name="mm_up_dw", slabs=[0, 1])
    red_up, token = reduce_start([g_wup], [True], "up")
    dh2 = _mm(dup, wup_b, "nt", F32, name="mm_up_dx", dep=token, slabs=[0, 1])
    red_up, token = reduce_mid(red_up, dh2)
    dx2, dx2b, dg_ffn = _rms_bwd(x2, g_ffn, dh2, dx3, "rms_ffn_bwd", True, dep=token)
    g_wo = _mm(merged, dx2b, "tn", BF16, name="mm_o_dw")
    dmerged = _mm(dx2b, wo_b, "nt", BF16, name="mm_o_dx")
    slab_w = d_lru
    assert d_sc == slab_w and d_model % slab_w == 0 and col_gates % slab_w == 0
    n_gate = d_model // slab_w
    gate0 = col_gates // slab_w
    dp_slabs = [gate0 + kind * n_gate + j for j in range(n_gate) for kind in (0, 1)] + [0, 1, 2, 3, 4]
    dp, dyl, dys = _merge_bwd(p, y_lru, y_sc, dmerged, col_gates, tc, len(dp_slabs))
    assert dp.shape[2] == slab_w
    g_wlo = _mm(y_lru_pre, dyl, "tn", BF16, name="mm_lru_out_dw")
    g_wso = _mm(y_sc_pre, dys, "tn", BF16, name="mm_sc_out_dw")
    red_mix, token = reduce_start([g_wlo, g_wso, g_wo], [True, True, False], "mix")
    dylp = _mm(dyl, wlo_b, "nt", F32, name="mm_lru_out_dx", dep=token)
    dysp = _mm(dys, wso_b, "nt", F32, name="mm_sc_out_dx")
    red_mix, token = reduce_mid(red_mix, dysp)
    dp, dlcw, dlcb, dwa_bd, dba, dwx_bd, dbx, dlam = _lru_bwd(
        p, hseq, dylp, lcw, lru_conv_b, wa_bd, lru_ba, wx_bd, lru_bx, lru_lambda, d_lru, gc, tc,
        dp, 2 * n_gate, dep=token)
    dp, dscw = _sc_bwd(p, dysp, scw, col_sc, d_sc, cb_sc, tc, dp, 2 * n_gate + 2)
    g_win = _mm(h1, dp, "tn", BF16, name="mm_in_dw", slabs=dp_slabs)
    red_in, token = reduce_start([g_win], [True], "in")
    dh1 = _mm(dp, win_b, "nt", F32, name="mm_in_dx", dep=token, slabs=dp_slabs)
    grad_x, dg_mix = _rms_bwd(xs, g_mix, dh1, dx2, "rms_mix_bwd", False)

    small_g = [dg_mix, dlcw, dlcb, _diag_blocks(dwa_bd, per_group, head_dim), dba,
               _diag_blocks(dwx_bd, per_group, head_dim), dbx, dlam, dscw, dg_ffn,
               jnp.concatenate([dfcw_g, dfcw_v], axis=1), dg_final]
    small_shapes = [a.shape for a in small_g]
    small_sum = _small_allreduce(_pack(small_g))
    red_in, token = reduce_mid(red_in, small_sum)
    (h_wdn,) = reduce_end(red_down, token)
    (h_wup,) = reduce_end(red_up, token)
    h_wlo, h_wso, h_wo = reduce_end(red_mix, token)
    s_wlo, s_wso, s_wo, s_wup, s_wdn = _exchange("grad_share_a", [h_wlo, h_wso, h_wo, h_wup, h_wdn], 5,
                                                 _share_plan(5))
    early = {1: s_wlo, 2: s_wso, 3: s_wo, 4: s_wup, 5: s_wdn}
    big_out = [None] * n_big
    last = None
    for k, g in early.items():
        big_out[k] = _adamw(big_w[k], g, big_m[k], big_v[k], "adamw_" + big_names[k], dep=last)
        last = big_out[k][1]
    (h_win,) = reduce_end(red_in, last)
    (s_win,) = _exchange("grad_share_b", [h_win], 1, _share_plan(1))
    big_out[0] = _adamw(big_w[0], s_win, big_m[0], big_v[0], "adamw_" + big_names[0])
    sg = _unpack(small_sum, small_shapes)
    for idx in (1, 8, 10):
        nq = sg[idx].shape[1] // N_CHIPS
        sg[idx] = lax.dynamic_slice_in_dim(sg[idx], chip * nq, nq, axis=1)
    small_w = [g_mix, lru_conv_w, lru_conv_b, lru_wa, lru_ba, lru_wx, lru_bx, lru_lambda, sc_conv_w,
               g_ffn, ffn_conv_w, g_final]
    small_m = [m_g_mix, m_lru_conv_w, m_lru_conv_b, m_lru_wa, m_lru_ba, m_lru_wx, m_lru_bx, m_lru_lambda,
               m_sc_conv_w, m_g_ffn, m_ffn_conv_w, m_g_final]
    small_v = [v_g_mix, v_lru_conv_w, v_lru_conv_b, v_lru_wa, v_lru_ba, v_lru_wx, v_lru_bx, v_lru_lambda,
               v_sc_conv_w, v_g_ffn, v_ffn_conv_w, v_g_final]
    sg = [g.reshape(w.shape) for g, w in zip(sg, small_w)]
    w_shapes = [w.shape for w in small_w]
    packed = _adamw(_pack(small_w), _pack(sg), _pack(small_m), _pack(small_v), "adamw_small")
    small_out = [_unpack(pk, w_shapes) for pk in packed]

    order = [(0, 0), (1, 0), (0, 1), (0, 2), (0, 3), (0, 4), (0, 5), (0, 6), (0, 7), (1, 1), (0, 8), (1, 2),
             (1, 3), (0, 9), (1, 4), (0, 10), (1, 5), (0, 11)]
    by_kind = []
    for kind in range(4):
        by_kind.append([big_out[i][kind] if is_big else small_out[kind][i] for is_big, i in order])
    loss = lax.psum(loss_part[0, 0], ("x", "y", "c"))
    return (loss, grad_x.reshape(x.shape), *by_kind[0], *by_kind[1], *by_kind[2], *by_kind[3])
```

```python
import math

import jax
import jax.numpy as jnp
from jax import lax
from jax.experimental import pallas as pl
from jax.experimental.pallas import tpu as pltpu

F32 = jnp.float32
BF16 = jnp.bfloat16

LANE = 128
SUBLANE = 8
BF16_ROWS = 16
VMEM_BYTES_V7X = 64 * 1024 * 1024
VMEM_BUDGET = VMEM_BYTES_V7X - 8 * 1024 * 1024
MM_VMEM_BUDGET = 42 * 1024 * 1024
EPS = 1e-6
LRU_C = 8.0
ADAM_LR = 0.001
ADAM_B1 = 0.9
ADAM_B2 = 0.999
ADAM_EPS = 1e-08
ADAM_WD = 0.01
ADAM_STEP = 10

N_CHIPS = 4
N_DEV = 8
MESH = pl.DeviceIdType.MESH
ANY = pl.BlockSpec(memory_space=pl.ANY)
VMEM_SPEC = pl.BlockSpec(memory_space=pltpu.VMEM)
HBM_SPEC = pl.BlockSpec(memory_space=pltpu.HBM)
SEM_SPEC = pl.BlockSpec(memory_space=pltpu.SEMAPHORE)
DATAFLOW_EFFECT = pltpu.SideEffectType.DATAFLOW_SIDE_EFFECTING


def _pick(n, cap, mult=LANE):
    best = None
    d = mult
    while d <= min(n, cap):
        if n % d == 0:
            best = d
        d += mult
    return n if best is None else best


def _cparams(semantics, block_bytes):
    limit = min(VMEM_BUDGET, max(32 * 1024 * 1024, int(block_bytes * 1.25) + (4 << 20)))
    return pltpu.CompilerParams(dimension_semantics=semantics, vmem_limit_bytes=limit)


def _nbytes(shape, dtype):
    return math.prod(shape) * jnp.dtype(dtype).itemsize


def _sigmoid(z):
    return 0.5 * jnp.tanh(0.5 * z) + 0.5


def _softplus(z):
    e = jnp.exp(-jnp.abs(z))
    u = 1.0 + e
    log1p = jnp.where(u == 1.0, e, jnp.log(u) * (e / (u - 1.0)))
    return jnp.maximum(z, 0.0) + log1p


def _neg_expm1(z):
    small = z * (1.0 + z * (0.5 + z * (1.0 / 6.0 + z * (1.0 / 24.0))))
    return -jnp.where(jnp.abs(z) < 0.03, small, jnp.exp(z) - 1.0)


_GELU_K = math.sqrt(2.0 / math.pi)
_GELU_C = 0.044715


def _gelu_and_grad(z):
    z2 = z * z
    th = jnp.tanh(_GELU_K * (z + _GELU_C * z2 * z))
    val = 0.5 * z * (1.0 + th)
    grad = 0.5 * (1.0 + th) + 0.5 * z * (1.0 - th * th) * (_GELU_K * (1.0 + 3.0 * _GELU_C * z2))
    return val, grad


def _rows_before(cat, k):
    if k == 0:
        return cat[SUBLANE:, :]
    return pltpu.roll(cat, k, 0)[SUBLANE:, :]


def _rows_after(cat, k):
    n = cat.shape[0]
    if k == 0:
        return cat[:n - SUBLANE, :]
    return pltpu.roll(cat, n - k, 0)[:n - SUBLANE, :]


def _conv_fwd(cat, w, width):
    y = _rows_before(cat, width - 1) * w[0:1, :]
    for k in range(1, width):
        y = y + _rows_before(cat, width - 1 - k) * w[k:k + 1, :]
    return y


def _conv_bwd_input(cat, w, width):
    dx = _rows_after(cat, width - 1) * w[0:1, :]
    for k in range(1, width):
        dx = dx + _rows_after(cat, width - 1 - k) * w[k:k + 1, :]
    return dx


def _conv_bwd_weight(dw_ref, dy, catx, width):
    for k in range(width):
        dw_ref[k:k + 1, :] += jnp.sum(dy * _rows_before(catx, width - 1 - k), axis=0, keepdims=True)


def _scan_tiles(a_ref, b_ref, out_ref, carry0, n_rows, reverse):
    cols = a_ref.shape[1]
    row = lax.broadcasted_iota(jnp.int32, (SUBLANE, cols), 0)
    n_tiles = n_rows // SUBLANE

    def step(j, carry):
        tile = (n_tiles - 1 - j) if reverse else j
        off = pl.multiple_of(tile * SUBLANE, SUBLANE)
        a = a_ref[pl.ds(off, SUBLANE), :]
        b = b_ref[pl.ds(off, SUBLANE), :]
        for s in (1, 2, 4):
            if reverse:
                keep = row < SUBLANE - s
                shift = SUBLANE - s
            else:
                keep = row >= s
                shift = s
            a_sh = jnp.where(keep, pltpu.roll(a, shift, 0), 1.0)
            b_sh = jnp.where(keep, pltpu.roll(b, shift, 0), 0.0)
            b = a * b_sh + b
            a = a * a_sh
        out = a * carry + b
        out_ref[pl.ds(off, SUBLANE), :] = out
        return out[0:1, :] if reverse else out[SUBLANE - 1:SUBLANE, :]

    return lax.fori_loop(0, n_tiles, step, carry0)


def _dep_args(body, in_specs, operands, *deps):
    deps = [d for d in deps if d is not None]
    if not deps:
        return body, in_specs, operands
    n = len(operands)

    def wrapped(*refs):
        return body(*refs[:n], *refs[n + len(deps):])

    return wrapped, list(in_specs) + [ANY] * len(deps), list(operands) + deps


def _mm(a, b, mode, out_dtype, res=None, name=None, dep=None, slabs=None):
    assert a.dtype == BF16 and b.dtype == BF16
    a_slabbed, b_slabbed = a.ndim == 3, b.ndim == 3
    assert not a_slabbed or (mode == "nt" and slabs is not None)
    assert not b_slabbed or (mode == "tn" and slabs is not None)
    if mode == "nn":
        (m, k), (k2, n) = a.shape, b.shape
        dims = (((1,), (0,)), ((), ()))
    elif mode == "nt":
        m, k = (a.shape[1], a.shape[0] * a.shape[2]) if a_slabbed else a.shape
        n, k2 = b.shape
        dims = (((1,), (1,)), ((), ()))
    else:
        k, m = a.shape
        k2, n = (b.shape[1], b.shape[0] * b.shape[2]) if b_slabbed else b.shape
        dims = (((0,), (0,)), ((), ()))
    assert k == k2
    n_unit = b.shape[2] if b_slabbed else n
    out_bytes = jnp.dtype(out_dtype).itemsize
    bm = _pick(m, 1024)
    bn = _pick(n_unit, 1024)
    bk = k

    def est(bm_, bn_, bk_):
        e = 2 * (bm_ * bk_ + bk_ * bn_) * 2 + 2 * bm_ * bn_ * out_bytes
        if k // bk_ > 1:
            e += bm_ * bn_ * 4
        if res is not None:
            e += 2 * bm_ * bn_ * 4
        return e

    for shrink_n, floor in ((True, 512), (False, 512), (True, 256), (False, 256)):
        while est(bm, bn, bk) > MM_VMEM_BUDGET:
            if shrink_n and bn > floor and bn % 2 == 0 and n_unit % (bn // 2) == 0:
                bn //= 2
            elif not shrink_n and bm > floor and bm % 2 == 0 and m % (bm // 2) == 0:
                bm //= 2
            else:
                break
    while (est(bm, bn, bk) > MM_VMEM_BUDGET and not a_slabbed and bk % (2 * LANE) == 0
           and k % (bk // 2) == 0):
        bk //= 2
    nk = k // bk
    per_slab = n_unit // bn

    def out_col(j):
        if not b_slabbed:
            return j
        s = j // per_slab
        where = sum(jnp.where(s == t, slabs[t], 0) for t in range(len(slabs)))
        return where * per_slab + j % per_slab

    if mode == "tn":
        a_spec = pl.BlockSpec((bk, bm), lambda i, j, kk: (kk, i))
    elif a_slabbed:
        a_spec = pl.BlockSpec((a.shape[0], bm, a.shape[2]), lambda i, j, kk: (0, i, 0))
    else:
        a_spec = pl.BlockSpec((bm, bk), lambda i, j, kk: (i, kk))
    if mode == "nt":
        b_spec = pl.BlockSpec((bn, bk), lambda i, j, kk: (j, kk))
    elif b_slabbed:
        b_spec = pl.BlockSpec((None, bk, bn), lambda i, j, kk: (j // per_slab, kk, j % per_slab))
    else:
        b_spec = pl.BlockSpec((bk, bn), lambda i, j, kk: (kk, j))
    o_spec = pl.BlockSpec((bm, bn), lambda i, j, kk: (i, out_col(j)))
    in_specs = [a_spec, b_spec]
    operands = [a, b]
    if res is not None:
        in_specs.append(o_spec)
        operands.append(res)
    has_res = res is not None

    def body(*refs):
        a_ref, b_ref = refs[0], refs[1]
        res_ref = refs[2] if has_res else None
        o_ref = refs[2 + has_res]
        if a_slabbed:
            width = a_ref.shape[2]
            part = None
            for s, col in enumerate(slabs):
                term = lax.dot_general(a_ref[s], b_ref[:, col * width:(col + 1) * width], dims,
                                       preferred_element_type=F32)
                part = term if part is None else part + term
        else:
            part = lax.dot_general(a_ref[...], b_ref[...], dims, preferred_element_type=F32)
        if nk == 1:
            if has_res:
                part = part + res_ref[...]
            o_ref[...] = part.astype(o_ref.dtype)
            return
        acc_ref = refs[-1]
        kk = pl.program_id(2)

        @pl.when(kk == 0)
        def _():
            acc_ref[...] = part

        @pl.when(kk > 0)
        def _():
            acc_ref[...] += part

        @pl.when(kk == nk - 1)
        def _():
            total = acc_ref[...]
            if has_res:
                total = total + res_ref[...]
            o_ref[...] = total.astype(o_ref.dtype)

    scratch = [pltpu.VMEM((bm, bn), F32)] if nk > 1 else []
    body, in_specs, operands = _dep_args(body, in_specs, operands, dep)
    return pl.pallas_call(
        body,
        out_shape=jax.ShapeDtypeStruct((m, n), out_dtype),
        grid=(m // bm, n // bn, nk),
        in_specs=in_specs,
        out_specs=o_spec,
        scratch_shapes=scratch,
        compiler_params=_cparams(("parallel", "parallel", "arbitrary"), est(bm, bn, bk)),
        name=name,
    )(*operands)


def _rms_fwd(x, g, name, dep=None):
    t, d = x.shape
    tb = _pick(t, 512, SUBLANE)

    def body(x_ref, g_ref, h_ref):
        xv = x_ref[...]
        r = lax.rsqrt(jnp.mean(xv * xv, axis=-1, keepdims=True) + EPS)
        h_ref[...] = ((xv * r) * g_ref[...]).astype(BF16)

    blk = pl.BlockSpec((tb, d), lambda i: (i, 0))
    body, in_specs, operands = _dep_args(
        body, [blk, pl.BlockSpec((1, d), lambda i: (0, 0))], [x, g.reshape(1, d)], dep)
    return pl.pallas_call(
        body,
        out_shape=jax.ShapeDtypeStruct((t, d), BF16),
        grid=(t // tb,),
        in_specs=in_specs,
        out_specs=blk,
        compiler_params=_cparams(("parallel",), 2 * tb * d * 6),
        name=name,
    )(*operands)


def _rms_bwd(x, g, dh, dres, name, want_bf16, dep=None):
    t, d = x.shape
    tb = _pick(t, 256, SUBLANE)

    def body(x_ref, g_ref, dh_ref, dres_ref, *outs):
        dx_ref, dg_ref = outs[0], outs[-1]
        xv = x_ref[...]
        r = lax.rsqrt(jnp.mean(xv * xv, axis=-1, keepdims=True) + EPS)
        xhat = xv * r
        dhv = dh_ref[...]
        dxhat = dhv * g_ref[...]
        dx = dres_ref[...] + r * (dxhat - xhat * jnp.mean(dxhat * xhat, axis=-1, keepdims=True))
        dx_ref[...] = dx
        if want_bf16:
            outs[1][...] = dx.astype(BF16)

        @pl.when(pl.program_id(0) == 0)
        def _():
            dg_ref[...] = jnp.zeros_like(dg_ref)

        dg_ref[...] += jnp.sum(dhv * xhat, axis=0, keepdims=True)

    blk = pl.BlockSpec((tb, d), lambda i: (i, 0))
    row = pl.BlockSpec((1, d), lambda i: (0, 0))
    out_shape = [jax.ShapeDtypeStruct((t, d), F32)]
    out_specs = [blk]
    if want_bf16:
        out_shape.append(jax.ShapeDtypeStruct((t, d), BF16))
        out_specs.append(blk)
    out_shape.append(jax.ShapeDtypeStruct((1, d), F32))
    out_specs.append(row)
    body, in_specs, operands = _dep_args(
        body, [blk, row, blk, blk], [x, g.reshape(1, d), dh, dres], dep)
    return pl.pallas_call(
        body,
        out_shape=out_shape,
        grid=(t // tb,),
        in_specs=in_specs,
        out_specs=out_specs,
        compiler_params=_cparams(("arbitrary",), 2 * tb * d * 18),
        name=name,
    )(*operands)


def _loss_head(x3, g, target):
    t, d = x3.shape
    tb = _pick(t, 256, SUBLANE)

    def body(x_ref, g_ref, t_ref, loss_ref, dx_ref, dxb_ref, dg_ref):
        xv = x_ref[...]
        gv = g_ref[...]
        r = lax.rsqrt(jnp.mean(xv * xv, axis=-1, keepdims=True) + EPS)
        xhat = xv * r
        err = xhat * gv - t_ref[...]
        dy = err * (1.0 / d)
        dxhat = dy * gv
        dx = r * (dxhat - xhat * jnp.mean(dxhat * xhat, axis=-1, keepdims=True))
        dx_ref[...] = dx
        dxb_ref[...] = dx.astype(BF16)

        @pl.when(pl.program_id(0) == 0)
        def _():
            dg_ref[...] = jnp.zeros_like(dg_ref)
            loss_ref[...] = jnp.zeros_like(loss_ref)

        dg_ref[...] += jnp.sum(dy * xhat, axis=0, keepdims=True)
        per_token = jnp.mean(err * err, axis=-1, keepdims=True)
        loss_ref[...] += 0.5 * jnp.sum(per_token, axis=0, keepdims=True)

    blk = pl.BlockSpec((tb, d), lambda i: (i, 0))
    row = pl.BlockSpec((1, d), lambda i: (0, 0))
    return pl.pallas_call(
        body,
        out_shape=[jax.ShapeDtypeStruct((1, 1), F32), jax.ShapeDtypeStruct((t, d), F32),
                   jax.ShapeDtypeStruct((t, d), BF16), jax.ShapeDtypeStruct((1, d), F32)],
        grid=(t // tb,),
        in_specs=[blk, row, blk],
        out_specs=[pl.BlockSpec((1, 1), lambda i: (0, 0)), blk, blk, row],
        compiler_params=_cparams(("arbitrary",), 2 * tb * d * 14),
        name="loss_head",
    )(x3, g.reshape(1, d), target)


def _lru_gates(xc, wa, ba, wx, bx, lam):
    nn = (((1,), (0,)), ((), ()))
    xcb = xc.astype(BF16)
    r = _sigmoid(lax.dot_general(xcb, wa, nn, preferred_element_type=F32) + ba)
    i = _sigmoid(lax.dot_general(xcb, wx, nn, preferred_element_type=F32) + bx)
    cl = -LRU_C * _softplus(-lam)
    log_a = cl * r
    a = jnp.exp(log_a)
    one_minus_a2 = _neg_expm1(2.0 * log_a)
    return xcb, r, i, a, one_minus_a2, cl


def _lru_fwd(p, conv_w, conv_b, wa_bd, ba, wx_bd, bx, lam, d_lru, gc, tc):
    t = p.shape[0]
    ng = d_lru // gc
    nt = t // tc
    width = conv_w.shape[0]

    def body(lx_ref, gate_ref, cw_ref, cb_ref, wa_ref, ba_ref, wx_ref, bx_ref, lam_ref,
             y_ref, h_ref, halo, hcar, a_s, u_s):
        @pl.when(pl.program_id(1) == 0)
        def _():
            halo[...] = jnp.zeros_like(halo)
            hcar[...] = jnp.zeros_like(hcar)

        x = lx_ref[...]
        cat = jnp.concatenate([halo[...], x], axis=0)
        halo[...] = x[tc - SUBLANE:, :]
        xc = _conv_fwd(cat, cw_ref[...], width) + cb_ref[...]
        _, r, i, a, om, _ = _lru_gates(xc, wa_ref[...], ba_ref[...], wx_ref[...], bx_ref[...], lam_ref[...])
        a_s[...] = a
        u_s[...] = jnp.sqrt(om) * (i * xc)
        hcar[0:1, :] = _scan_tiles(a_s, u_s, h_ref, hcar[0:1, :], tc, reverse=False)
        gl, _ = _gelu_and_grad(gate_ref[...])
        y_ref[...] = (gl * h_ref[...]).astype(BF16)

    blk = lambda off: pl.BlockSpec((tc, gc), lambda g, s, off=off: (s, off + g))
    rowv = lambda rows: pl.BlockSpec((rows, gc), lambda g, s: (0, g))
    wspec = pl.BlockSpec((None, gc, gc), lambda g, s: (g, 0, 0))
    out_blk = pl.BlockSpec((tc, gc), lambda g, s: (s, g))
    return pl.pallas_call(
        body,
        out_shape=[jax.ShapeDtypeStruct((t, d_lru), BF16), jax.ShapeDtypeStruct((t, d_lru), F32)],
        grid=(ng, nt),
        in_specs=[blk(0), blk(ng), rowv(width), rowv(1), wspec, rowv(1), wspec, rowv(1), rowv(1)],
        out_specs=[out_blk, out_blk],
        scratch_shapes=[pltpu.VMEM((SUBLANE, gc), F32), pltpu.VMEM((SUBLANE, gc), F32),
                        pltpu.VMEM((tc, gc), F32), pltpu.VMEM((tc, gc), F32)],
        compiler_params=_cparams(("parallel", "arbitrary"), 40 * tc * gc * 4),
        name="lru_fwd",
    )(p, p, conv_w, conv_b.reshape(1, -1), wa_bd, ba.reshape(1, -1), wx_bd, bx.reshape(1, -1),
      lam.reshape(1, -1))


def _lru_bwd(p, hseq, dyp, conv_w, conv_b, wa_bd, ba, wx_bd, bx, lam, d_lru, gc, tc, dp, slab0, dep=None):
    t = p.shape[0]
    ng = d_lru // gc
    nt = t // tc
    width = conv_w.shape[0]
    halo_blocks = tc // SUBLANE
    nn = (((1,), (0,)), ((), ()))
    nt_dims = (((1,), (1,)), ((), ()))
    tn_dims = (((0,), (0,)), ((), ()))

    def body(lx_ref, lxh_ref, gate_ref, h_ref, hh_ref, dyp_ref,
             cw_ref, cb_ref, wa_ref, ba_ref, wx_ref, bx_ref, lam_ref,
             dp_ref, dcw_ref, dcb_ref, dwa_ref, dba_ref, dwx_ref, dbx_ref, dlam_ref,
             nxt_dxc, nxt_a, nxt_g, al_s, b_s, g_s):
        s = pl.program_id(1)
        first_chunk = s == nt - 1

        @pl.when(s == 0)
        def _():
            nxt_dxc[...] = jnp.zeros_like(nxt_dxc)
            nxt_a[...] = jnp.zeros_like(nxt_a)
            nxt_g[...] = jnp.zeros_like(nxt_g)
            for ref in (dcw_ref, dcb_ref, dwa_ref, dba_ref, dwx_ref, dbx_ref, dlam_ref):
                ref[...] = jnp.zeros_like(ref)

        keep = jnp.where(first_chunk, 0.0, 1.0)
        x = lx_ref[...]
        catx = jnp.concatenate([lxh_ref[...] * keep, x], axis=0)
        cw = cw_ref[...]
        xc = _conv_fwd(catx, cw, width) + cb_ref[...]
        wa = wa_ref[...]
        wx = wx_ref[...]
        lam_v = lam_ref[...]
        xcb, r, i, a, om, cl = _lru_gates(xc, wa, ba_ref[...], wx, bx_ref[...], lam_v)
        mult = jnp.sqrt(om)

        h = h_ref[...]
        hprev = _rows_before(jnp.concatenate([hh_ref[...] * keep, h], axis=0), 1)
        gl, dgl = _gelu_and_grad(gate_ref[...])
        dyp_v = dyp_ref[...]
        dp_ref[1] = (dyp_v * h * dgl).astype(BF16)

        al_s[...] = _rows_after(jnp.concatenate([a, nxt_a[...]], axis=0), 1)
        b_s[...] = dyp_v * gl
        nxt_g[0:1, :] = _scan_tiles(al_s, b_s, g_s, nxt_g[0:1, :], tc, reverse=True)
        nxt_a[...] = a[0:SUBLANE, :]
        du = g_s[...]

        da = du * hprev
        dmult = du * (i * xc)
        di = du * mult * xc
        dxc = du * mult * i
        dlog_a = da * a - dmult * (a * a / mult)
        dlam_ref[...] += jnp.sum(dlog_a * r, axis=0, keepdims=True) * (LRU_C * _sigmoid(-lam_v))
        dza = (dlog_a * cl) * r * (1.0 - r)
        dzx = di * i * (1.0 - i)
        dba_ref[...] += jnp.sum(dza, axis=0, keepdims=True)
        dbx_ref[...] += jnp.sum(dzx, axis=0, keepdims=True)
        dzab = dza.astype(BF16)
        dzxb = dzx.astype(BF16)
        dwa_ref[...] += lax.dot_general(xcb, dzab, tn_dims, preferred_element_type=F32)
        dwx_ref[...] += lax.dot_general(xcb, dzxb, tn_dims, preferred_element_type=F32)
        dxc = dxc + lax.dot_general(dzab, wa, nt_dims, preferred_element_type=F32)
        dxc = dxc + lax.dot_general(dzxb, wx, nt_dims, preferred_element_type=F32)
        dcb_ref[...] += jnp.sum(dxc, axis=0, keepdims=True)
        _conv_bwd_weight(dcw_ref, dxc, catx, width)
        catd = jnp.concatenate([dxc, nxt_dxc[...]], axis=0)
        dp_ref[0] = _conv_bwd_input(catd, cw, width).astype(BF16)
        nxt_dxc[...] = dxc[0:SUBLANE, :]

    rev = lambda s: nt - 1 - s
    blk = lambda off: pl.BlockSpec((tc, gc), lambda g, s, off=off: (rev(s), off + g))
    halo = lambda off: pl.BlockSpec(
        (SUBLANE, gc), lambda g, s, off=off: (jnp.maximum(rev(s) * halo_blocks - 1, 0), off + g))
    rowv = lambda rows: pl.BlockSpec((rows, gc), lambda g, s: (0, g))
    wspec = pl.BlockSpec((None, gc, gc), lambda g, s: (g, 0, 0))
    out_blk = pl.BlockSpec((tc, gc), lambda g, s: (rev(s), g))
    vec = lambda rows: jax.ShapeDtypeStruct((rows, d_lru), F32)
    wshape = jax.ShapeDtypeStruct((ng, gc, gc), F32)
    body, in_specs, operands = _dep_args(
        body,
        [blk(0), halo(0), blk(ng), blk(0), halo(0), blk(0),
         rowv(width), rowv(1), wspec, rowv(1), wspec, rowv(1), rowv(1)],
        [p, p, p, hseq, hseq, dyp,
         conv_w, conv_b.reshape(1, -1), wa_bd, ba.reshape(1, -1), wx_bd,
         bx.reshape(1, -1), lam.reshape(1, -1)], dp, dep)
    assert dp.shape[2] == d_lru and slab0 % 2 == 0
    return pl.pallas_call(
        body,
        out_shape=[jax.ShapeDtypeStruct(dp.shape, dp.dtype),
                   vec(width), vec(1), wshape, vec(1), wshape, vec(1), vec(1)],
        grid=(ng, nt),
        in_specs=in_specs,
        out_specs=[pl.BlockSpec((2, tc, gc), lambda g, s: (slab0 // 2, rev(s), g)),
                   rowv(width), rowv(1), wspec, rowv(1), wspec, rowv(1), rowv(1)],
        input_output_aliases={13: 0},
        scratch_shapes=[pltpu.VMEM((SUBLANE, gc), F32), pltpu.VMEM((SUBLANE, gc), F32),
                        pltpu.VMEM((SUBLANE, gc), F32),
                        pltpu.VMEM((tc, gc), F32), pltpu.VMEM((tc, gc), F32), pltpu.VMEM((tc, gc), F32)],
        compiler_params=_cparams(("parallel", "arbitrary"), 80 * tc * gc * 4),
        name="lru_bwd",
    )(*operands)


def _sc_fwd(p, conv_w, col0, d_sc, cb, tc):
    t = p.shape[0]
    nc = d_sc // cb
    nt = t // tc
    width = conv_w.shape[0]
    base = col0 // cb

    def body(b_ref, c_ref, v_ref, w_ref, y_ref, halo):
        @pl.when(pl.program_id(1) == 0)
        def _():
            halo[...] = jnp.zeros_like(halo)

        cv = c_ref[...] * v_ref[...]
        cat = jnp.concatenate([halo[...], cv], axis=0)
        halo[...] = cv[tc - SUBLANE:, :]
        y_ref[...] = (b_ref[...] * _conv_fwd(cat, w_ref[...], width)).astype(BF16)

    blk = lambda slab: pl.BlockSpec((tc, cb), lambda j, s, slab=slab: (s, base + slab * nc + j))
    return pl.pallas_call(
        body,
        out_shape=jax.ShapeDtypeStruct((t, d_sc), BF16),
        grid=(nc, nt),
        in_specs=[blk(0), blk(1), blk(2), pl.BlockSpec((width, cb), lambda j, s: (0, j))],
        out_specs=pl.BlockSpec((tc, cb), lambda j, s: (s, j)),
        scratch_shapes=[pltpu.VMEM((SUBLANE, cb), F32)],
        compiler_params=_cparams(("parallel", "arbitrary"), 20 * tc * cb * 4),
        name="sc_fwd",
    )(p, p, p, conv_w)


def _sc_bwd(p, dyp, conv_w, col0, d_sc, cb, tc, dp, slab0):
    t = p.shape[0]
    nc = d_sc // cb
    nt = t // tc
    width = conv_w.shape[0]
    base = col0 // cb
    halo_blocks = tc // SUBLANE

    def body(b_ref, c_ref, ch_ref, v_ref, vh_ref, dyp_ref, w_ref,
             dp_ref, dw_ref, nxt_dq):
        s = pl.program_id(1)

        @pl.when(s == 0)
        def _():
            nxt_dq[...] = jnp.zeros_like(nxt_dq)
            dw_ref[...] = jnp.zeros_like(dw_ref)

        keep = jnp.where(s == nt - 1, 0.0, 1.0)
        cvals = c_ref[...]
        vvals = v_ref[...]
        w = w_ref[...]
        catcv = jnp.concatenate([ch_ref[...] * vh_ref[...] * keep, cvals * vvals], axis=0)
        q = _conv_fwd(catcv, w, width)
        dyp_v = dyp_ref[...]
        dp_ref[0] = (dyp_v * q).astype(BF16)
        dq = dyp_v * b_ref[...]
        _conv_bwd_weight(dw_ref, dq, catcv, width)
        dcv = _conv_bwd_input(jnp.concatenate([dq, nxt_dq[...]], axis=0), w, width)
        nxt_dq[...] = dq[0:SUBLANE, :]
        dp_ref[1] = (dcv * vvals).astype(BF16)
        dp_ref[2] = (dcv * cvals).astype(BF16)

    rev = lambda s: nt - 1 - s
    blk = lambda slab: pl.BlockSpec((tc, cb), lambda j, s, slab=slab: (rev(s), base + slab * nc + j))
    halo = lambda slab: pl.BlockSpec(
        (SUBLANE, cb),
        lambda j, s, slab=slab: (jnp.maximum(rev(s) * halo_blocks - 1, 0), base + slab * nc + j))
    out_blk = pl.BlockSpec((tc, cb), lambda j, s: (rev(s), j))
    wblk = pl.BlockSpec((width, cb), lambda j, s: (0, j))
    assert dp.shape[2] == d_sc and slab0 % 3 == 0
    operands = [p, p, p, p, p, dyp, conv_w]
    body, in_specs, operands = _dep_args(
        body, [blk(0), blk(1), halo(1), blk(2), halo(2), out_blk, wblk], operands, dp)
    return pl.pallas_call(
        body,
        out_shape=[jax.ShapeDtypeStruct(dp.shape, dp.dtype), jax.ShapeDtypeStruct((width, d_sc), F32)],
        grid=(nc, nt),
        in_specs=in_specs,
        out_specs=[pl.BlockSpec((3, tc, cb), lambda j, s: (slab0 // 3, rev(s), j)), wblk],
        input_output_aliases={7: 0},
        scratch_shapes=[pltpu.VMEM((SUBLANE, cb), F32)],
        compiler_params=_cparams(("parallel", "arbitrary"), 30 * tc * cb * 4),
        name="sc_bwd",
    )(*operands)


def _merge_fwd(p, y_lru, y_sc, col0, tc):
    t, d = y_lru.shape
    cb = _pick(math.gcd(d, col0), 1024)
    nc = d // cb
    base = col0 // cb

    def body(gl_ref, gs_ref, yl_ref, ys_ref, o_ref):
        o_ref[...] = (_sigmoid(gl_ref[...]) * yl_ref[...] + _sigmoid(gs_ref[...]) * ys_ref[...]).astype(BF16)

    gate = lambda slab: pl.BlockSpec((tc, cb), lambda s, j, slab=slab: (s, base + slab * nc + j))
    blk = pl.BlockSpec((tc, cb), lambda s, j: (s, j))
    return pl.pallas_call(
        body,
        out_shape=jax.ShapeDtypeStruct((t, d), BF16),
        grid=(t // tc, nc),
        in_specs=[gate(0), gate(1), blk, blk],
        out_specs=blk,
        compiler_params=_cparams(("parallel", "parallel"), 2 * tc * cb * 20),
        name="merge_fwd",
    )(p, p, y_lru, y_sc)


def _merge_bwd(p, y_lru, y_sc, dmerged, col0, tc, n_slabs):
    t, d = y_lru.shape
    cb = _pick(math.gcd(d, col0), 1024)
    nc = d // cb
    base = col0 // cb

    def body(gl_ref, gs_ref, yl_ref, ys_ref, dm_ref, dp_ref, dyl_ref, dys_ref):
        dm = dm_ref[...]
        sl = _sigmoid(gl_ref[...])
        ss = _sigmoid(gs_ref[...])
        dp_ref[0] = (dm * yl_ref[...] * (sl * (1.0 - sl))).astype(BF16)
        dp_ref[1] = (dm * ys_ref[...] * (ss * (1.0 - ss))).astype(BF16)
        dyl_ref[...] = (dm * sl).astype(BF16)
        dys_ref[...] = (dm * ss).astype(BF16)

    gate = lambda slab: pl.BlockSpec((tc, cb), lambda s, j, slab=slab: (s, base + slab * nc + j))
    blk = pl.BlockSpec((tc, cb), lambda s, j: (s, j))
    act = jax.ShapeDtypeStruct((t, d), BF16)
    return pl.pallas_call(
        body,
        out_shape=[jax.ShapeDtypeStruct((n_slabs, t, cb), BF16), act, act],
        grid=(t // tc, nc),
        in_specs=[gate(0), gate(1), blk, blk, blk],
        out_specs=[pl.BlockSpec((2, tc, cb), lambda s, j: (j, s, 0)), blk, blk],
        compiler_params=_cparams(("parallel", "parallel"), 2 * tc * cb * 28),
        name="merge_bwd",
    )(p, p, y_lru, y_sc, dmerged)


def _ffn_act_fwd(up, conv_w, d_ff, cb, tc):
    t = up.shape[0]
    nc = d_ff // cb
    nt = t // tc
    width = conv_w.shape[0]

    def body(g_ref, v_ref, wg_ref, wv_ref, o_ref, halo_g, halo_v):
        @pl.when(pl.program_id(1) == 0)
        def _():
            halo_g[...] = jnp.zeros_like(halo_g)
            halo_v[...] = jnp.zeros_like(halo_v)

        g = g_ref[...]
        v = v_ref[...]
        ug = _conv_fwd(jnp.concatenate([halo_g[...], g], axis=0), wg_ref[...], width)
        uv = _conv_fwd(jnp.concatenate([halo_v[...], v], axis=0), wv_ref[...], width)
        halo_g[...] = g[tc - SUBLANE:, :]
        halo_v[...] = v[tc - SUBLANE:, :]
        o_ref[...] = (ug * _sigmoid(ug) * uv).astype(BF16)

    blk = lambda half: pl.BlockSpec((tc, cb), lambda j, s, half=half: (s, half * nc + j))
    wblk = lambda half: pl.BlockSpec((width, cb), lambda j, s, half=half: (0, half * nc + j))
    return pl.pallas_call(
        body,
        out_shape=jax.ShapeDtypeStruct((t, d_ff), BF16),
        grid=(nc, nt),
        in_specs=[blk(0), blk(1), wblk(0), wblk(1)],
        out_specs=pl.BlockSpec((tc, cb), lambda j, s: (s, j)),
        scratch_shapes=[pltpu.VMEM((SUBLANE, cb), F32), pltpu.VMEM((SUBLANE, cb), F32)],
        compiler_params=_cparams(("parallel", "arbitrary"), 24 * tc * cb * 4),
        name="ffn_act_fwd",
    )(up, up, conv_w, conv_w)


def _ffn_act_bwd(up, dact, conv_w, d_ff, cb, tc, dep=None):
    t = up.shape[0]
    nc = d_ff // cb
    nt = t // tc
    width = conv_w.shape[0]
    halo_blocks = tc // SUBLANE

    def body(g_ref, gh_ref, v_ref, vh_ref, da_ref, wg_ref, wv_ref,
             dup_ref, dwg_ref, dwv_ref, nxt_g, nxt_v):
        s = pl.program_id(1)

        @pl.when(s == 0)
        def _():
            nxt_g[...] = jnp.zeros_like(nxt_g)
            nxt_v[...] = jnp.zeros_like(nxt_v)
            dwg_ref[...] = jnp.zeros_like(dwg_ref)
            dwv_ref[...] = jnp.zeros_like(dwv_ref)

        keep = jnp.where(s == nt - 1, 0.0, 1.0)
        wg = wg_ref[...]
        wv = wv_ref[...]
        catg = jnp.concatenate([gh_ref[...] * keep, g_ref[...]], axis=0)
        catv = jnp.concatenate([vh_ref[...] * keep, v_ref[...]], axis=0)
        ug = _conv_fwd(catg, wg, width)
        uv = _conv_fwd(catv, wv, width)
        sg = _sigmoid(ug)
        da = da_ref[...]
        duv = da * (ug * sg)
        dup_ref[1] = _conv_bwd_input(jnp.concatenate([duv, nxt_v[...]], axis=0), wv, width).astype(BF16)
        nxt_v[...] = duv[0:SUBLANE, :]
        _conv_bwd_weight(dwv_ref, duv, catv, width)
        dug = da * uv * (sg * (1.0 + ug * (1.0 - sg)))
        dup_ref[0] = _conv_bwd_input(jnp.concatenate([dug, nxt_g[...]], axis=0), wg, width).astype(BF16)
        nxt_g[...] = dug[0:SUBLANE, :]
        _conv_bwd_weight(dwg_ref, dug, catg, width)

    rev = lambda s: nt - 1 - s
    blk = lambda half: pl.BlockSpec((tc, cb), lambda j, s, half=half: (rev(s), half * nc + j))
    halo = lambda half: pl.BlockSpec(
        (SUBLANE, cb), lambda j, s, half=half: (jnp.maximum(rev(s) * halo_blocks - 1, 0), half * nc + j))
    wblk = lambda half: pl.BlockSpec((width, cb), lambda j, s, half=half: (0, half * nc + j))
    out_blk = pl.BlockSpec((tc, cb), lambda j, s: (rev(s), j))
    wout = pl.BlockSpec((width, cb), lambda j, s: (0, j))
    act = jax.ShapeDtypeStruct((t, d_ff), BF16)
    wshape = jax.ShapeDtypeStruct((width, d_ff), F32)
    body, in_specs, operands = _dep_args(
        body, [blk(0), halo(0), blk(1), halo(1), out_blk, wblk(0), wblk(1)],
        [up, up, up, up, dact, conv_w, conv_w], dep)
    return pl.pallas_call(
        body,
        out_shape=[jax.ShapeDtypeStruct((2, t, d_ff), BF16), wshape, wshape],
        grid=(nc, nt),
        in_specs=in_specs,
        out_specs=[pl.BlockSpec((2, tc, cb), lambda j, s: (0, rev(s), j)), wout, wout],
        scratch_shapes=[pltpu.VMEM((SUBLANE, cb), F32), pltpu.VMEM((SUBLANE, cb), F32)],
        compiler_params=_cparams(("parallel", "arbitrary"), 40 * tc * cb * 4),
        name="ffn_act_bwd",
    )(*operands)


def _mesh_pos():
    x, y, c = lax.axis_index("x"), lax.axis_index("y"), lax.axis_index("c")
    return x, y, c


def _other_chips(x, y):
    return [(1 - x, y), (x, 1 - y), (1 - x, 1 - y)]


def _cast_place(w, chip, col_sharded, name, dep=None):
    r, cdim = w.shape
    full = (r, cdim * N_CHIPS) if col_sharded else (r * N_CHIPS, cdim)
    rb = _pick(r, max(BF16_ROWS, (512 * 1024) // cdim), BF16_ROWS)
    nb = r // rb

    def body(chip_ref, w_ref, o_ref):
        o_ref[...] = w_ref[...].astype(BF16)

    if col_sharded:
        out_map = lambda i, chip_ref: (i, chip_ref[0])
    else:
        out_map = lambda i, chip_ref: (chip_ref[0] * nb + i, 0)
    grid_spec = pltpu.PrefetchScalarGridSpec(
        num_scalar_prefetch=1,
        grid=(nb,),
        in_specs=[pl.BlockSpec((rb, cdim), lambda i, chip_ref: (i, 0))] + ([ANY] if dep is not None else []),
        out_specs=pl.BlockSpec((rb, cdim), out_map),
    )
    body, _, operands = _dep_args(body, [], [chip, w], dep)
    return pl.pallas_call(
        body,
        out_shape=jax.ShapeDtypeStruct(full, BF16),
        grid_spec=grid_spec,
        compiler_params=_cparams(("parallel",), 2 * rb * cdim * 6),
        name=name,
    )(*operands)


def _remote(src, dst, send_sems, recv_sems, idx, to):
    return pltpu.make_async_remote_copy(
        src_ref=src, dst_ref=dst, send_sem=send_sems.at[idx], recv_sem=recv_sems.at[idx],
        device_id=to, device_id_type=MESH)


def _exchange(name, arrays, n_sems, plan):
    n = len(arrays)

    def body(*refs):
        bufs = refs[n:2 * n]
        send_sems, recv_sems = refs[2 * n:]
        sends, arrivals = plan(bufs, send_sems, recv_sems)
        for cp in sends:
            cp.start()
        for cp in arrivals:
            cp.wait_recv()
        for cp in sends:
            cp.wait_send()

    outs = pl.pallas_call(
        body,
        out_shape=[jax.ShapeDtypeStruct(a.shape, a.dtype) for a in arrays],
        in_specs=[ANY] * n,
        out_specs=[ANY] * n,
        input_output_aliases={k: k for k in range(n)},
        scratch_shapes=[pltpu.SemaphoreType.DMA((n_sems,)), pltpu.SemaphoreType.DMA((n_sems,))],
        name=name,
    )(*arrays)
    return list(outs)


def _exchange_start(name, arrays, n_sems, plan, after=None):
    n = len(arrays)
    n_in = n + (after is not None)

    def body(*refs):
        bufs = refs[:n]
        send_sems, recv_sems = refs[n_in], refs[n_in + 1]
        token = refs[-1]
        sends, _ = plan(bufs, send_sems, recv_sems)
        for cp in sends:
            cp.start()
        token[...] = jnp.zeros_like(token)

    out = pl.pallas_call(
        body,
        out_shape=(pltpu.SemaphoreType.DMA((n_sems,)), pltpu.SemaphoreType.DMA((n_sems,)),
                   *[pltpu.HBM(a.shape, a.dtype) for a in arrays],
                   jax.ShapeDtypeStruct((SUBLANE, LANE), F32)),
        in_specs=[HBM_SPEC] * n + [ANY] * (n_in - n),
        out_specs=(SEM_SPEC, SEM_SPEC, *[HBM_SPEC] * n, VMEM_SPEC),
        input_output_aliases={k: 2 + k for k in range(n)},
        compiler_params=pltpu.CompilerParams(has_side_effects=DATAFLOW_EFFECT),
        name=name,
    )(*[pltpu.with_memory_space_constraint(a, pltpu.HBM) for a in arrays], *([after] if after is not None else []))
    return out[0], out[1], list(out[2:2 + n]), out[-1]


def _exchange_wait(name, arrays, send_sems, recv_sems, after, plan):
    n = len(arrays)

    def body(*refs):
        bufs = refs[:n]
        sends, arrivals = plan(bufs, refs[n], refs[n + 1])
        for cp in arrivals:
            cp.wait_recv()
        for cp in sends:
            cp.wait_send()

    outs = pl.pallas_call(
        body,
        out_shape=[pltpu.HBM(a.shape, a.dtype) for a in arrays],
        in_specs=[HBM_SPEC] * n + [SEM_SPEC, SEM_SPEC, ANY],
        out_specs=[HBM_SPEC] * n,
        input_output_aliases={k: k for k in range(n)},
        compiler_params=pltpu.CompilerParams(has_side_effects=DATAFLOW_EFFECT),
        name=name,
    )(*arrays, send_sems, recv_sems, after)
    return list(outs)


def _half_block(ref, shard_shape, col_sharded, chip, half):
    r, cdim = shard_shape
    h = r // 2
    if col_sharded:
        return ref.at[pl.ds(pl.multiple_of(half * h, BF16_ROWS), h),
                      pl.ds(pl.multiple_of(chip * cdim, LANE), cdim)]
    return ref.at[pl.ds(pl.multiple_of(chip * r + half * h, BF16_ROWS), h), :]


def _gather_plan(shard_shapes, col_sharded, ks):
    def plan(bufs, send_sems, recv_sems):
        x, y, c = _mesh_pos()
        sends, arrivals = [], []
        for ref, k in zip(bufs, ks):
            mine = _half_block(ref, shard_shapes[k], col_sharded[k], 2 * x + y, c)
            for j, (px, py) in enumerate(_other_chips(x, y)):
                landed = _half_block(ref, shard_shapes[k], col_sharded[k], 2 * px + py, c)
                sends.append(_remote(mine, mine, send_sems, recv_sems, 3 * k + j, (px, py, c)))
                arrivals.append(_remote(landed, landed, send_sems, recv_sems, 3 * k + j, (px, py, c)))
        return sends, arrivals
    return plan


def _forward_plan(shard_shapes, col_sharded, ks):
    def plan(bufs, send_sems, recv_sems):
        x, y, c = _mesh_pos()
        sends, arrivals = [], []
        for i, (ref, k) in enumerate(zip(bufs, ks)):
            for j, (px, py) in enumerate(_other_chips(x, y)):
                landed = _half_block(ref, shard_shapes[k], col_sharded[k], 2 * px + py, c)
                theirs = _half_block(ref, shard_shapes[k], col_sharded[k], 2 * px + py, 1 - c)
                sends.append(_remote(landed, landed, send_sems, recv_sems, 3 * i + j, (x, y, 1 - c)))
                arrivals.append(_remote(theirs, theirs, send_sems, recv_sems, 3 * i + j, (x, y, 1 - c)))
        return sends, arrivals
    return plan


def _small_gather(small):
    def body(small_ref, out_ref, send_sems, recv_sems):
        x, y, c = _mesh_pos()
        me = 2 * x + y
        out_ref[me] = small_ref[...]
        copies = []
        for j, (px, py) in enumerate(_other_chips(x, y)):
            cp = _remote(small_ref, out_ref.at[me], send_sems, recv_sems, j, (px, py, c))
            cp.start()
            copies.append(cp)
        for j, (px, py) in enumerate(_other_chips(x, y)):
            _remote(small_ref, out_ref.at[2 * px + py], send_sems, recv_sems, j, (px, py, c)).wait_recv()
        for cp in copies:
            cp.wait_send()

    return pl.pallas_call(
        body,
        out_shape=jax.ShapeDtypeStruct((N_CHIPS,) + small.shape, small.dtype),
        in_specs=[VMEM_SPEC],
        out_specs=VMEM_SPEC,
        scratch_shapes=[pltpu.SemaphoreType.DMA((N_CHIPS - 1,)), pltpu.SemaphoreType.DMA((N_CHIPS - 1,))],
        name="gather_small",
    )(small)


def _as3d(g, col_sharded):
    r, cdim = g.shape
    return g.reshape(1, r, cdim) if col_sharded else g.reshape(N_CHIPS, r // N_CHIPS, cdim)


def _pair_plan(m):
    def plan(bufs, send_sems, recv_sems):
        x, y, c = _mesh_pos()
        copies = []
        for i in range(m):
            h = bufs[i].shape[1] // 2
            src = bufs[i].at[:, pl.ds(pl.multiple_of((1 - c) * h, BF16_ROWS), h), :]
            copies.append(_remote(src, bufs[m + i], send_sems, recv_sems, i, (x, y, 1 - c)))
        return copies, copies
    return plan


def _chip_plan(col_flags):
    m = len(col_flags)

    def plan(bufs, send_sems, recv_sems):
        x, y, c = _mesh_pos()
        copies = []
        for i in range(m):
            land = bufs[m + i]
            width = land.shape[2]
            for j, (px, py) in enumerate(_other_chips(x, y)):
                q = 2 * px + py
                if col_flags[i]:
                    src = bufs[i].at[0, :, pl.ds(pl.multiple_of(q * width, LANE), width)]
                else:
                    src = bufs[i].at[q]
                copies.append(_remote(src, land.at[j], send_sems, recv_sems, 3 * i + j, (px, py, c)))
        return copies, copies
    return plan


def _share_plan(m):
    def plan(bufs, send_sems, recv_sems):
        x, y, c = _mesh_pos()
        sends, arrivals = [], []
        for i in range(m):
            h = bufs[i].shape[0] // 2
            mine = bufs[i].at[pl.ds(pl.multiple_of(c * h, SUBLANE), h), :]
            theirs = bufs[i].at[pl.ds(pl.multiple_of((1 - c) * h, SUBLANE), h), :]
            sends.append(_remote(mine, mine, send_sems, recv_sems, i, (x, y, 1 - c)))
            arrivals.append(_remote(theirs, theirs, send_sems, recv_sems, i, (x, y, 1 - c)))
        return sends, arrivals
    return plan


def _pair_add(g3, other, core):
    a, r, cdim = g3.shape
    h = r // 2
    rb = _pick(h, max(BF16_ROWS, (512 * 1024) // cdim), BF16_ROWS)
    nb = h // rb

    def body(core_ref, g_ref, o_ref, out_ref):
        out_ref[...] = (g_ref[...].astype(F32) + o_ref[...].astype(F32)).astype(BF16)

    grid_spec = pltpu.PrefetchScalarGridSpec(
        num_scalar_prefetch=1,
        grid=(a, nb),
        in_specs=[pl.BlockSpec((None, rb, cdim), lambda i, j, core_ref: (i, core_ref[0] * nb + j, 0)),
                  pl.BlockSpec((None, rb, cdim), lambda i, j, core_ref: (i, j, 0))],
        out_specs=pl.BlockSpec((None, rb, cdim), lambda i, j, core_ref: (i, j, 0)),
    )
    return pl.pallas_call(
        body,
        out_shape=jax.ShapeDtypeStruct((a, h, cdim), BF16),
        grid_spec=grid_spec,
        compiler_params=_cparams(("parallel", "parallel"), 2 * rb * cdim * 10),
        name="grad_pair_add",
    )(core, g3, other)


def _small_allreduce(small):
    rows = small.shape[0]
    pad = (-rows) % (2 * SUBLANE)
    if pad:
        small = jnp.pad(small, ((0, pad), (0, 0)))
    h = small.shape[0] // 2
    half_shape = (h, small.shape[1])

    def body(small_ref, out_ref, theirs, by_chip, send_sems, recv_sems):
        x, y, c = _mesh_pos()
        me = 2 * x + y
        sibling = (x, y, 1 - c)
        mine = pl.ds(pl.multiple_of(c * h, SUBLANE), h)
        other = pl.ds(pl.multiple_of((1 - c) * h, SUBLANE), h)
        swap = _remote(small_ref, theirs, send_sems, recv_sems, 0, sibling)
        swap.start()
        swap.wait()
        by_chip[me] = small_ref[mine, :] + theirs[mine, :]
        copies = []
        for j, (px, py) in enumerate(_other_chips(x, y)):
            cp = _remote(by_chip.at[me], by_chip.at[me], send_sems, recv_sems, 1 + j, (px, py, c))
            cp.start()
            copies.append(cp)
        for j, (px, py) in enumerate(_other_chips(x, y)):
            landed = by_chip.at[2 * px + py]
            _remote(landed, landed, send_sems, recv_sems, 1 + j, (px, py, c)).wait_recv()
        total = by_chip[0]
        for q in range(1, N_CHIPS):
            total = total + by_chip[q]
        out_ref[mine, :] = total
        for cp in copies:
            cp.wait_send()
        share = _remote(out_ref.at[mine, :], out_ref.at[mine, :], send_sems, recv_sems, 4, sibling)
        share.start()
        _remote(out_ref.at[other, :], out_ref.at[other, :], send_sems, recv_sems, 4, sibling).wait_recv()
        share.wait_send()

    out = pl.pallas_call(
        body,
        out_shape=jax.ShapeDtypeStruct(small.shape, F32),
        in_specs=[VMEM_SPEC],
        out_specs=VMEM_SPEC,
        scratch_shapes=[pltpu.VMEM(small.shape, F32), pltpu.VMEM((N_CHIPS,) + half_shape, F32),
                        pltpu.SemaphoreType.DMA((5,)), pltpu.SemaphoreType.DMA((5,))],
        compiler_params=pltpu.CompilerParams(
            vmem_limit_bytes=min(VMEM_BUDGET, 8 * _nbytes(small.shape, F32) + (8 << 20))),
        name="grad_small_allreduce",
    )(small)
    return out[:rows]


def _chip_sum(partial, land, where, col_sharded):
    _, h, cdim = land.shape
    rb = _pick(h, max(BF16_ROWS, (512 * 1024) // cdim), BF16_ROWS)
    nb = h // rb

    def body(where_ref, own_ref, l_ref, o_ref):
        total = own_ref[...].astype(F32)
        for j in range(N_CHIPS - 1):
            total = total + l_ref[j].astype(F32)
        o_ref[...] = total

    if col_sharded:
        own_map = lambda i, w: (0, i, w[0])
    else:
        own_map = lambda i, w: (w[0], i, 0)
    grid_spec = pltpu.PrefetchScalarGridSpec(
        num_scalar_prefetch=1,
        grid=(nb,),
        in_specs=[pl.BlockSpec((None, rb, cdim), own_map),
                  pl.BlockSpec((N_CHIPS - 1, rb, cdim), lambda i, w: (0, i, 0))],
        out_specs=pl.BlockSpec((rb, cdim), lambda i, w: (w[1] * nb + i, 0)),
    )
    return pl.pallas_call(
        body,
        out_shape=jax.ShapeDtypeStruct((2 * h, cdim), F32),
        grid_spec=grid_spec,
        compiler_params=_cparams(("parallel",), 2 * rb * cdim * 12),
        name="grad_chip_sum",
    )(where, partial, land)


def _adamw(w, g, m, v, name, dep=None):
    r, cdim = w.shape
    rb = _pick(r, max(SUBLANE, (256 * 1024) // cdim), SUBLANE)
    c1 = 1.0 - ADAM_B1 ** ADAM_STEP
    c2 = 1.0 - ADAM_B2 ** ADAM_STEP

    def body(w_ref, g_ref, m_ref, v_ref, go_ref, d_ref, mo_ref, vo_ref):
        gv = g_ref[...]
        mn = ADAM_B1 * m_ref[...] + (1.0 - ADAM_B1) * gv
        vn = ADAM_B2 * v_ref[...] + (1.0 - ADAM_B2) * (gv * gv)
        m_hat = mn / c1
        v_hat = vn / c2
        d_ref[...] = -ADAM_LR * (m_hat / (jnp.sqrt(v_hat) + ADAM_EPS) + ADAM_WD * w_ref[...])
        go_ref[...] = gv
        mo_ref[...] = mn
        vo_ref[...] = vn

    blk = pl.BlockSpec((rb, cdim), lambda i: (i, 0))
    shape = jax.ShapeDtypeStruct((r, cdim), F32)
    body, in_specs, operands = _dep_args(body, [blk] * 4, [w, g, m, v], dep)
    return pl.pallas_call(
        body,
        out_shape=[shape] * 4,
        grid=(r // rb,),
        in_specs=in_specs,
        out_specs=[blk] * 4,
        compiler_params=_cparams(("parallel",), 2 * rb * cdim * 4 * 8),
        name=name,
    )(*operands)


def _pack(arrays):
    tile = SUBLANE * LANE
    pieces = []
    for arr in arrays:
        flat = arr.reshape(-1)
        pad = (-flat.shape[0]) % tile
        if pad:
            flat = jnp.concatenate([flat, jnp.zeros((pad,), flat.dtype)])
        pieces.append(flat)
    return jnp.concatenate(pieces).reshape(-1, LANE)


def _unpack(packed, shapes):
    tile = SUBLANE * LANE
    flat = packed.reshape(-1)
    out, off = [], 0
    for shp in shapes:
        size = math.prod(shp)
        out.append(flat[off:off + size].reshape(shp))
        off += size + ((-size) % tile)
    return out


def _block_diag_groups(w, per_group):
    hcount, hd, _ = w.shape
    ng = hcount // per_group
    w4 = w.reshape(ng, per_group, hd, hd)
    eye = jnp.eye(per_group, dtype=w.dtype)
    bd = w4[:, :, :, None, :] * eye[None, :, None, :, None]
    return bd.reshape(ng, per_group * hd, per_group * hd).astype(BF16)


def _diag_blocks(wbd, per_group, hd):
    ng = wbd.shape[0]
    w5 = wbd.reshape(ng, per_group, hd, per_group, hd)
    blocks = [w5[:, i, :, i, :] for i in range(per_group)]
    return jnp.stack(blocks, axis=1).reshape(ng * per_group, hd, hd)


def kernel(x, g_mix, w_in, lru_conv_w, lru_conv_b, lru_wa, lru_ba, lru_wx, lru_bx, lru_lambda, lru_w_out, sc_conv_w, sc_w_out, w_o, g_ffn, ffn_w_up, ffn_conv_w, ffn_w_down, g_final, loss_target, m_g_mix, m_w_in, m_lru_conv_w, m_lru_conv_b, m_lru_wa, m_lru_ba, m_lru_wx, m_lru_bx, m_lru_lambda, m_lru_w_out, m_sc_conv_w, m_sc_w_out, m_w_o, m_g_ffn, m_ffn_w_up, m_ffn_conv_w, m_ffn_w_down, m_g_final, v_g_mix, v_w_in, v_lru_conv_w, v_lru_conv_b, v_lru_wa, v_lru_ba, v_lru_wx, v_lru_bx, v_lru_lambda, v_lru_w_out, v_sc_conv_w, v_sc_w_out, v_w_o, v_g_ffn, v_ffn_w_up, v_ffn_conv_w, v_ffn_w_down, v_g_final):
    seq, d_model = x.shape[1], x.shape[2]
    heads, head_dim, _ = lru_wa.shape
    d_lru = heads * head_dim
    d_sc = sc_w_out.shape[0]
    d_ff = ffn_w_down.shape[0] * N_CHIPS
    assert x.shape[0] == 1 and w_in.shape[1] * N_CHIPS == 2 * d_lru + 3 * d_sc + 2 * d_model
    xs = x.reshape(seq, d_model)
    target = loss_target.reshape(seq, d_model)

    chip = 2 * lax.axis_index("x") + lax.axis_index("y")
    core = lax.axis_index("c").astype(jnp.int32).reshape(1)

    big_w = [w_in, lru_w_out, sc_w_out, w_o, ffn_w_up, ffn_w_down]
    big_m = [m_w_in, m_lru_w_out, m_sc_w_out, m_w_o, m_ffn_w_up, m_ffn_w_down]
    big_v = [v_w_in, v_lru_w_out, v_sc_w_out, v_w_o, v_ffn_w_up, v_ffn_w_down]
    col_sharded = [True, True, True, False, True, False]
    conv_shards = [lru_conv_w, sc_conv_w, ffn_conv_w]
    conv_pack = jnp.concatenate(
        [jnp.pad(w, ((0, SUBLANE - w.shape[0]), (0, 0))) for w in conv_shards], axis=1)
    big_names = ["w_in", "lru_w_out", "sc_w_out", "w_o", "ffn_w_up", "ffn_w_down"]
    chip_arr = chip.astype(jnp.int32).reshape(1)
    placed = [_cast_place(big_w[0], chip_arr, col_sharded[0], "cast_" + big_names[0])]
    conv_all = _small_gather(conv_pack)
    shard_shapes = [w.shape for w in big_w]
    n_big = len(big_w)

    def gather_start(ks, after, tag):
        send, recv, bufs, token = _exchange_start(
            "gather_start_" + tag, [placed[k] for k in ks], 3 * n_big,
            _gather_plan(shard_shapes, col_sharded, ks), after=after)
        return (send, recv, dict(zip(ks, bufs))), token

    def arrived(state, ks, after, tag):
        send, recv, bufs = state
        got = _exchange_wait("gather_wait_" + tag, [bufs[k] for k in ks], send, recv, after,
                             _gather_plan(shard_shapes, col_sharded, ks))
        return _exchange("gather_forward_" + tag, got, 3 * len(ks), _forward_plan(shard_shapes, col_sharded, ks))

    conv_full, off = [], 0
    for w in conv_shards:
        kw, nq = w.shape
        piece = conv_all[:, :kw, off:off + nq]
        conv_full.append(piece.transpose(1, 0, 2).reshape(kw, N_CHIPS * nq))
        off += nq
    lcw, scw, fcw = conv_full

    per_group = max(1, min(heads, 256 // head_dim))
    gc = per_group * head_dim
    wa_bd = _block_diag_groups(lru_wa, per_group)
    wx_bd = _block_diag_groups(lru_wx, per_group)
    tc = _pick(seq, 256, SUBLANE)
    cb_sc = _pick(d_sc, 512)
    cb_ff = _pick(d_ff, 128)
    tc_ff = _pick(seq, 2048, SUBLANE)
    tc_lru = _pick(seq, 512, SUBLANE)
    col_sc = 2 * d_lru
    col_gates = 2 * d_lru + 3 * d_sc

    first, token = gather_start([0], conv_all, "in")
    for k in range(1, n_big):
        placed.append(_cast_place(big_w[k], chip_arr, col_sharded[k], "cast_" + big_names[k], dep=token))
        token = placed[-1]
    h1 = _rms_fwd(xs, g_mix, "rms_mix", dep=token)
    (win_b,) = arrived(first, [0], h1, "in")
    rest, token = gather_start([1, 2, 3, 4, 5], win_b, "rest")
    p = _mm(h1, win_b, "nn", F32, name="mm_in", dep=token)
    wlo_b, wso_b, wo_b = arrived(rest, [1, 2, 3], p, "mix")
    y_lru_pre, hseq = _lru_fwd(p, lcw, lru_conv_b, wa_bd, lru_ba, wx_bd, lru_bx, lru_lambda, d_lru, gc, tc_lru)
    y_sc_pre = _sc_fwd(p, scw, col_sc, d_sc, cb_sc, tc)
    y_lru = _mm(y_lru_pre, wlo_b, "nn", BF16, name="mm_lru_out")
    y_sc = _mm(y_sc_pre, wso_b, "nn", BF16, name="mm_sc_out")
    merged = _merge_fwd(p, y_lru, y_sc, col_gates, tc)
    x2 = _mm(merged, wo_b, "nn", F32, res=xs, name="mm_o")
    (wup_b,) = arrived(rest, [4], x2, "up")
    h2 = _rms_fwd(x2, g_ffn, "rms_ffn")
    up = _mm(h2, wup_b, "nn", F32, name="mm_up")
    (wdn_b,) = arrived(rest, [5], up, "down")
    act = _ffn_act_fwd(up, fcw, d_ff, cb_ff, tc_ff)
    x3 = _mm(act, wdn_b, "nn", F32, res=x2, name="mm_down")
    loss_part, dx3, dx3b, dg_final = _loss_head(x3, g_final, target)

    where = jnp.concatenate([chip_arr, core])

    def reduce_start(grads, flags, tag):
        views = [_as3d(g, cs) for g, cs in zip(grads, flags)]
        lands = [lax.empty((v.shape[0], v.shape[1] // 2, v.shape[2]), v.dtype) for v in views]
        send, recv, bufs, token = _exchange_start("grad_pair_start_" + tag, views + lands, len(views),
                                                  _pair_plan(len(views)))
        return (send, recv, bufs, flags, tag), token

    def reduce_mid(state, after):
        send, recv, bufs, flags, tag = state
        m = len(flags)
        bufs = _exchange_wait("grad_pair_wait_" + tag, bufs, send, recv, after, _pair_plan(m))
        partials = [_pair_add(bufs[i], bufs[m + i], core) for i in range(m)]
        lands = []
        for pz, cs in zip(partials, flags):
            _, h, cdim = pz.shape
            lands.append(lax.empty((N_CHIPS - 1, h, cdim // N_CHIPS if cs else cdim), BF16))
        send, recv, bufs, token = _exchange_start("grad_chip_start_" + tag, partials + lands, 3 * m,
                                                  _chip_plan(flags))
        return (send, recv, bufs, flags, tag), token

    def reduce_end(state, after):
        send, recv, bufs, flags, tag = state
        m = len(flags)
        bufs = _exchange_wait("grad_chip_wait_" + tag, bufs, send, recv, after, _chip_plan(flags))
        return [_chip_sum(bufs[i], bufs[m + i], where, flags[i]) for i in range(m)]

    g_wdn = _mm(act, dx3b, "tn", F32, name="mm_down_dw")
    red_down, token = reduce_start([g_wdn], [False], "down")
    dact = _mm(dx3b, wdn_b, "nt", F32, name="mm_down_dx", dep=token)
    red_down, token = reduce_mid(red_down, dact)
    dup, dfcw_g, dfcw_v = _ffn_act_bwd(up, dact, fcw, d_ff, cb_ff, tc_ff, dep=token)
    g_wup = _mm(h2, dup, "tn", BF16, name="mm_up_dw", slabs=[0, 1])
    red_up, token = reduce_start([g_wup], [True], "up")
    dh2 = _mm(dup, wup_b, "nt", F32, name="mm_up_dx", dep=token, slabs=[0, 1])
    red_up, token = reduce_mid(red_up, dh2)
    dx2, dx2b, dg_ffn = _rms_bwd(x2, g_ffn, dh2, dx3, "rms_ffn_bwd", True, dep=token)
    g_wo = _mm(merged, dx2b, "tn", BF16, name="mm_o_dw")
    dmerged = _mm(dx2b, wo_b, "nt", BF16, name="mm_o_dx")
    slab_w = d_lru
    assert d_sc == slab_w and d_model % slab_w == 0 and col_gates % slab_w == 0
    n_gate = d_model // slab_w
    gate0 = col_gates // slab_w
    dp_slabs = [gate0 + kind * n_gate + j for j in range(n_gate) for kind in (0, 1)] + [0, 1, 2, 3, 4]
    dp, dyl, dys = _merge_bwd(p, y_lru, y_sc, dmerged, col_gates, tc, len(dp_slabs))
    assert dp.shape[2] == slab_w
    g_wlo = _mm(y_lru_pre, dyl, "tn", BF16, name="mm_lru_out_dw")
    g_wso = _mm(y_sc_pre, dys, "tn", BF16, name="mm_sc_out_dw")
    red_mix, token = reduce_start([g_wlo, g_wso, g_wo], [True, True, False], "mix")
    dylp = _mm(dyl, wlo_b, "nt", F32, name="mm_lru_out_dx", dep=token)
    dysp = _mm(dys, wso_b, "nt", F32, name="mm_sc_out_dx")
    red_mix, token = reduce_mid(red_mix, dysp)
    dp, dlcw, dlcb, dwa_bd, dba, dwx_bd, dbx, dlam = _lru_bwd(
        p, hseq, dylp, lcw, lru_conv_b, wa_bd, lru_ba, wx_bd, lru_bx, lru_lambda, d_lru, gc, tc_lru,
        dp, 2 * n_gate, dep=token)
    dp, dscw = _sc_bwd(p, dysp, scw, col_sc, d_sc, cb_sc, tc, dp, 2 * n_gate + 2)
    g_win = _mm(h1, dp, "tn", BF16, name="mm_in_dw", slabs=dp_slabs)
    red_in, token = reduce_start([g_win], [True], "in")
    dh1 = _mm(dp, win_b, "nt", F32, name="mm_in_dx", dep=token, slabs=dp_slabs)
    grad_x, dg_mix = _rms_bwd(xs, g_mix, dh1, dx2, "rms_mix_bwd", False)

    small_g = [dg_mix, dlcw, dlcb, _diag_blocks(dwa_bd, per_group, head_dim), dba,
               _diag_blocks(dwx_bd, per_group, head_dim), dbx, dlam, dscw, dg_ffn,
               jnp.concatenate([dfcw_g, dfcw_v], axis=1), dg_final]
    small_shapes = [a.shape for a in small_g]
    small_sum = _small_allreduce(_pack(small_g))
    red_in, token = reduce_mid(red_in, small_sum)
    (h_wdn,) = reduce_end(red_down, token)
    (h_wup,) = reduce_end(red_up, token)
    h_wlo, h_wso, h_wo = reduce_end(red_mix, token)
    s_wlo, s_wso, s_wo, s_wup, s_wdn = _exchange("grad_share_a", [h_wlo, h_wso, h_wo, h_wup, h_wdn], 5,
                                                 _share_plan(5))
    early = {1: s_wlo, 2: s_wso, 3: s_wo, 4: s_wup, 5: s_wdn}
    big_out = [None] * n_big
    last = None
    for k, g in early.items():
        big_out[k] = _adamw(big_w[k], g, big_m[k], big_v[k], "adamw_" + big_names[k], dep=last)
        last = big_out[k][1]
    (h_win,) = reduce_end(red_in, last)
    (s_win,) = _exchange("grad_share_b", [h_win], 1, _share_plan(1))
    big_out[0] = _adamw(big_w[0], s_win, big_m[0], big_v[0], "adamw_" + big_names[0])
    sg = _unpack(small_sum, small_shapes)
    for idx in (1, 8, 10):
        nq = sg[idx].shape[1] // N_CHIPS
        sg[idx] = lax.dynamic_slice_in_dim(sg[idx], chip * nq, nq, axis=1)
    small_w = [g_mix, lru_conv_w, lru_conv_b, lru_wa, lru_ba, lru_wx, lru_bx, lru_lambda, sc_conv_w,
               g_ffn, ffn_conv_w, g_final]
    small_m = [m_g_mix, m_lru_conv_w, m_lru_conv_b, m_lru_wa, m_lru_ba, m_lru_wx, m_lru_bx, m_lru_lambda,
               m_sc_conv_w, m_g_ffn, m_ffn_conv_w, m_g_final]
    small_v = [v_g_mix, v_lru_conv_w, v_lru_conv_b, v_lru_wa, v_lru_ba, v_lru_wx, v_lru_bx, v_lru_lambda,
               v_sc_conv_w, v_g_ffn, v_ffn_conv_w, v_g_final]
    sg = [g.reshape(w.shape) for g, w in zip(sg, small_w)]
    w_shapes = [w.shape for w in small_w]
    packed = _adamw(_pack(small_w), _pack(sg), _pack(small_m), _pack(small_v), "adamw_small")
    small_out = [_unpack(pk, w_shapes) for pk in packed]

    order = [(0, 0), (1, 0), (0, 1), (0, 2), (0, 3), (0, 4), (0, 5), (0, 6), (0, 7), (1, 1), (0, 8), (1, 2),
             (1, 3), (0, 9), (1, 4), (0, 10), (1, 5), (0, 11)]
    by_kind = []
    for kind in range(4):
        by_kind.append([big_out[i][kind] if is_big else small_out[kind][i] for is_big, i in order])
    loss = lax.psum(loss_part[0, 0], ("x", "y", "c"))
    return (loss, grad_x.reshape(x.shape), *by_kind[0], *by_kind[1], *by_kind[2], *by_kind[3])
```

```python
import math

import jax
import jax.numpy as jnp
from jax import lax
from jax.experimental import pallas as pl
from jax.experimental.pallas import tpu as pltpu

F32 = jnp.float32
BF16 = jnp.bfloat16

LANE = 128
SUBLANE = 8
BF16_ROWS = 16
VMEM_BYTES_V7X = 64 * 1024 * 1024
VMEM_BUDGET = VMEM_BYTES_V7X - 8 * 1024 * 1024
MM_VMEM_BUDGET = 42 * 1024 * 1024
EPS = 1e-6
LRU_C = 8.0
ADAM_LR = 0.001
ADAM_B1 = 0.9
ADAM_B2 = 0.999
ADAM_EPS = 1e-08
ADAM_WD = 0.01
ADAM_STEP = 10

N_CHIPS = 4
N_DEV = 8
MESH = pl.DeviceIdType.MESH
ANY = pl.BlockSpec(memory_space=pl.ANY)
VMEM_SPEC = pl.BlockSpec(memory_space=pltpu.VMEM)
HBM_SPEC = pl.BlockSpec(memory_space=pltpu.HBM)
SEM_SPEC = pl.BlockSpec(memory_space=pltpu.SEMAPHORE)
DATAFLOW_EFFECT = pltpu.SideEffectType.DATAFLOW_SIDE_EFFECTING


def _pick(n, cap, mult=LANE):
    best = None
    d = mult
    while d <= min(n, cap):
        if n % d == 0:
            best = d
        d += mult
    return n if best is None else best


def _cparams(semantics, block_bytes):
    limit = min(VMEM_BUDGET, max(32 * 1024 * 1024, int(block_bytes * 1.25) + (4 << 20)))
    return pltpu.CompilerParams(dimension_semantics=semantics, vmem_limit_bytes=limit)


def _nbytes(shape, dtype):
    return math.prod(shape) * jnp.dtype(dtype).itemsize


def _sigmoid(z):
    return 0.5 * jnp.tanh(0.5 * z) + 0.5


def _softplus(z):
    e = jnp.exp(-jnp.abs(z))
    u = 1.0 + e
    log1p = jnp.where(u == 1.0, e, jnp.log(u) * (e / (u - 1.0)))
    return jnp.maximum(z, 0.0) + log1p


def _neg_expm1(z):
    small = z * (1.0 + z * (0.5 + z * (1.0 / 6.0 + z * (1.0 / 24.0))))
    return -jnp.where(jnp.abs(z) < 0.03, small, jnp.exp(z) - 1.0)


_GELU_K = math.sqrt(2.0 / math.pi)
_GELU_C = 0.044715


def _gelu_and_grad(z):
    z2 = z * z
    th = jnp.tanh(_GELU_K * (z + _GELU_C * z2 * z))
    val = 0.5 * z * (1.0 + th)
    grad = 0.5 * (1.0 + th) + 0.5 * z * (1.0 - th * th) * (_GELU_K * (1.0 + 3.0 * _GELU_C * z2))
    return val, grad


def _rows_before(cat, k):
    if k == 0:
        return cat[SUBLANE:, :]
    return pltpu.roll(cat, k, 0)[SUBLANE:, :]


def _rows_after(cat, k):
    n = cat.shape[0]
    if k == 0:
        return cat[:n - SUBLANE, :]
    return pltpu.roll(cat, n - k, 0)[:n - SUBLANE, :]


def _conv_fwd(cat, w, width):
    y = _rows_before(cat, width - 1) * w[0:1, :]
    for k in range(1, width):
        y = y + _rows_before(cat, width - 1 - k) * w[k:k + 1, :]
    return y


def _conv_bwd_input(cat, w, width):
    dx = _rows_after(cat, width - 1) * w[0:1, :]
    for k in range(1, width):
        dx = dx + _rows_after(cat, width - 1 - k) * w[k:k + 1, :]
    return dx


def _conv_bwd_weight(dw_ref, dy, catx, width):
    for k in range(width):
        dw_ref[k:k + 1, :] += jnp.sum(dy * _rows_before(catx, width - 1 - k), axis=0, keepdims=True)


def _scan_tiles(a_ref, b_ref, out_ref, carry0, n_rows, reverse):
    cols = a_ref.shape[1]
    row = lax.broadcasted_iota(jnp.int32, (SUBLANE, cols), 0)
    n_tiles = n_rows // SUBLANE

    def step(j, carry):
        tile = (n_tiles - 1 - j) if reverse else j
        off = pl.multiple_of(tile * SUBLANE, SUBLANE)
        a = a_ref[pl.ds(off, SUBLANE), :]
        b = b_ref[pl.ds(off, SUBLANE), :]
        for s in (1, 2, 4):
            if reverse:
                keep = row < SUBLANE - s
                shift = SUBLANE - s
            else:
                keep = row >= s
                shift = s
            a_sh = jnp.where(keep, pltpu.roll(a, shift, 0), 1.0)
            b_sh = jnp.where(keep, pltpu.roll(b, shift, 0), 0.0)
            b = a * b_sh + b
            a = a * a_sh
        out = a * carry + b
        out_ref[pl.ds(off, SUBLANE), :] = out
        return out[0:1, :] if reverse else out[SUBLANE - 1:SUBLANE, :]

    return lax.fori_loop(0, n_tiles, step, carry0)


def _dep_args(body, in_specs, operands, *deps):
    deps = [d for d in deps if d is not None]
    if not deps:
        return body, in_specs, operands
    n = len(operands)

    def wrapped(*refs):
        return body(*refs[:n], *refs[n + len(deps):])

    return wrapped, list(in_specs) + [ANY] * len(deps), list(operands) + deps


def _mm(a, b, mode, out_dtype, res=None, name=None, dep=None, slabs=None):
    assert a.dtype == BF16 and b.dtype == BF16
    a_slabbed, b_slabbed = a.ndim == 3, b.ndim == 3
    assert not a_slabbed or (mode == "nt" and slabs is not None)
    assert not b_slabbed or (mode == "tn" and slabs is not None)
    if mode == "nn":
        (m, k), (k2, n) = a.shape, b.shape
        dims = (((1,), (0,)), ((), ()))
    elif mode == "nt":
        m, k = (a.shape[1], a.shape[0] * a.shape[2]) if a_slabbed else a.shape
        n, k2 = b.shape
        dims = (((1,), (1,)), ((), ()))
    else:
        k, m = a.shape
        k2, n = (b.shape[1], b.shape[0] * b.shape[2]) if b_slabbed else b.shape
        dims = (((0,), (0,)), ((), ()))
    assert k == k2
    n_unit = b.shape[2] if b_slabbed else n
    out_bytes = jnp.dtype(out_dtype).itemsize
    bm = _pick(m, 1024)
    bn = _pick(n_unit, 1024)
    bk = k

    def est(bm_, bn_, bk_):
        e = 2 * (bm_ * bk_ + bk_ * bn_) * 2 + 2 * bm_ * bn_ * out_bytes
        if k // bk_ > 1:
            e += bm_ * bn_ * 4
        if res is not None:
            e += 2 * bm_ * bn_ * 4
        return e

    for shrink_n, floor in ((True, 512), (False, 512), (True, 256), (False, 256)):
        while est(bm, bn, bk) > MM_VMEM_BUDGET:
            if shrink_n and bn > floor and bn % 2 == 0 and n_unit % (bn // 2) == 0:
                bn //= 2
            elif not shrink_n and bm > floor and bm % 2 == 0 and m % (bm // 2) == 0:
                bm //= 2
            else:
                break
    while (est(bm, bn, bk) > MM_VMEM_BUDGET and not a_slabbed and bk % (2 * LANE) == 0
           and k % (bk // 2) == 0):
        bk //= 2
    nk = k // bk
    per_slab = n_unit // bn

    def out_col(j):
        if not b_slabbed:
            return j
        s = j // per_slab
        where = sum(jnp.where(s == t, slabs[t], 0) for t in range(len(slabs)))
        return where * per_slab + j % per_slab

    if mode == "tn":
        a_spec = pl.BlockSpec((bk, bm), lambda i, j, kk: (kk, i))
    elif a_slabbed:
        a_spec = pl.BlockSpec((a.shape[0], bm, a.shape[2]), lambda i, j, kk: (0, i, 0))
    else:
        a_spec = pl.BlockSpec((bm, bk), lambda i, j, kk: (i, kk))
    if mode == "nt":
        b_spec = pl.BlockSpec((bn, bk), lambda i, j, kk: (j, kk))
    elif b_slabbed:
        b_spec = pl.BlockSpec((None, bk, bn), lambda i, j, kk: (j // per_slab, kk, j % per_slab))
    else:
        b_spec = pl.BlockSpec((bk, bn), lambda i, j, kk: (kk, j))
    o_spec = pl.BlockSpec((bm, bn), lambda i, j, kk: (i, out_col(j)))
    in_specs = [a_spec, b_spec]
    operands = [a, b]
    if res is not None:
        in_specs.append(o_spec)
        operands.append(res)
    has_res = res is not None

    def body(*refs):
        a_ref, b_ref = refs[0], refs[1]
        res_ref = refs[2] if has_res else None
        o_ref = refs[2 + has_res]
        if a_slabbed:
            width = a_ref.shape[2]
            part = None
            for s, col in enumerate(slabs):
                term = lax.dot_general(a_ref[s], b_ref[:, col * width:(col + 1) * width], dims,
                                       preferred_element_type=F32)
                part = term if part is None else part + term
        else:
            part = lax.dot_general(a_ref[...], b_ref[...], dims, preferred_element_type=F32)
        if nk == 1:
            if has_res:
                part = part + res_ref[...]
            o_ref[...] = part.astype(o_ref.dtype)
            return
        acc_ref = refs[-1]
        kk = pl.program_id(2)

        @pl.when(kk == 0)
        def _():
            acc_ref[...] = part

        @pl.when(kk > 0)
        def _():
            acc_ref[...] += part

        @pl.when(kk == nk - 1)
        def _():
            total = acc_ref[...]
            if has_res:
                total = total + res_ref[...]
            o_ref[...] = total.astype(o_ref.dtype)

    scratch = [pltpu.VMEM((bm, bn), F32)] if nk > 1 else []
    body, in_specs, operands = _dep_args(body, in_specs, operands, dep)
    return pl.pallas_call(
        body,
        out_shape=jax.ShapeDtypeStruct((m, n), out_dtype),
        grid=(m // bm, n // bn, nk),
        in_specs=in_specs,
        out_specs=o_spec,
        scratch_shapes=scratch,
        compiler_params=_cparams(("parallel", "parallel", "arbitrary"), est(bm, bn, bk)),
        name=name,
    )(*operands)


def _rms_fwd(x, g, name, dep=None):
    t, d = x.shape
    tb = _pick(t, 512, SUBLANE)

    def body(x_ref, g_ref, h_ref):
        xv = x_ref[...]
        r = lax.rsqrt(jnp.mean(xv * xv, axis=-1, keepdims=True) + EPS)
        h_ref[...] = ((xv * r) * g_ref[...]).astype(BF16)

    blk = pl.BlockSpec((tb, d), lambda i: (i, 0))
    body, in_specs, operands = _dep_args(
        body, [blk, pl.BlockSpec((1, d), lambda i: (0, 0))], [x, g.reshape(1, d)], dep)
    return pl.pallas_call(
        body,
        out_shape=jax.ShapeDtypeStruct((t, d), BF16),
        grid=(t // tb,),
        in_specs=in_specs,
        out_specs=blk,
        compiler_params=_cparams(("parallel",), 2 * tb * d * 6),
        name=name,
    )(*operands)


def _rms_bwd(x, g, dh, dres, name, want_bf16, dep=None):
    t, d = x.shape
    tb = _pick(t, 256, SUBLANE)

    def body(x_ref, g_ref, dh_ref, dres_ref, *outs):
        dx_ref, dg_ref = outs[0], outs[-1]
        xv = x_ref[...]
        r = lax.rsqrt(jnp.mean(xv * xv, axis=-1, keepdims=True) + EPS)
        xhat = xv * r
        dhv = dh_ref[...]
        dxhat = dhv * g_ref[...]
        dx = dres_ref[...] + r * (dxhat - xhat * jnp.mean(dxhat * xhat, axis=-1, keepdims=True))
        dx_ref[...] = dx
        if want_bf16:
            outs[1][...] = dx.astype(BF16)

        @pl.when(pl.program_id(0) == 0)
        def _():
            dg_ref[...] = jnp.zeros_like(dg_ref)

        dg_ref[...] += jnp.sum(dhv * xhat, axis=0, keepdims=True)

    blk = pl.BlockSpec((tb, d), lambda i: (i, 0))
    row = pl.BlockSpec((1, d), lambda i: (0, 0))
    out_shape = [jax.ShapeDtypeStruct((t, d), F32)]
    out_specs = [blk]
    if want_bf16:
        out_shape.append(jax.ShapeDtypeStruct((t, d), BF16))
        out_specs.append(blk)
    out_shape.append(jax.ShapeDtypeStruct((1, d), F32))
    out_specs.append(row)
    body, in_specs, operands = _dep_args(
        body, [blk, row, blk, blk], [x, g.reshape(1, d), dh, dres], dep)
    return pl.pallas_call(
        body,
        out_shape=out_shape,
        grid=(t // tb,),
        in_specs=in_specs,
        out_specs=out_specs,
        compiler_params=_cparams(("arbitrary",), 2 * tb * d * 18),
        name=name,
    )(*operands)


def _loss_head(x3, g, target):
    t, d = x3.shape
    tb = _pick(t, 256, SUBLANE)

    def body(x_ref, g_ref, t_ref, loss_ref, dx_ref, dxb_ref, dg_ref):
        xv = x_ref[...]
        gv = g_ref[...]
        r = lax.rsqrt(jnp.mean(xv * xv, axis=-1, keepdims=True) + EPS)
        xhat = xv * r
        err = xhat * gv - t_ref[...]
        dy = err * (1.0 / d)
        dxhat = dy * gv
        dx = r * (dxhat - xhat * jnp.mean(dxhat * xhat, axis=-1, keepdims=True))
        dx_ref[...] = dx
        dxb_ref[...] = dx.astype(BF16)

        @pl.when(pl.program_id(0) == 0)
        def _():
            dg_ref[...] = jnp.zeros_like(dg_ref)
            loss_ref[...] = jnp.zeros_like(loss_ref)

        dg_ref[...] += jnp.sum(dy * xhat, axis=0, keepdims=True)
        per_token = jnp.mean(err * err, axis=-1, keepdims=True)
        loss_ref[...] += 0.5 * jnp.sum(per_token, axis=0, keepdims=True)

    blk = pl.BlockSpec((tb, d), lambda i: (i, 0))
    row = pl.BlockSpec((1, d), lambda i: (0, 0))
    return pl.pallas_call(
        body,
        out_shape=[jax.ShapeDtypeStruct((1, 1), F32), jax.ShapeDtypeStruct((t, d), F32),
                   jax.ShapeDtypeStruct((t, d), BF16), jax.ShapeDtypeStruct((1, d), F32)],
        grid=(t // tb,),
        in_specs=[blk, row, blk],
        out_specs=[pl.BlockSpec((1, 1), lambda i: (0, 0)), blk, blk, row],
        compiler_params=_cparams(("arbitrary",), 2 * tb * d * 14),
        name="loss_head",
    )(x3, g.reshape(1, d), target)


def _lru_gates(xc, wa, ba, wx, bx, lam):
    nn = (((1,), (0,)), ((), ()))
    xcb = xc.astype(BF16)
    r = _sigmoid(lax.dot_general(xcb, wa, nn, preferred_element_type=F32) + ba)
    i = _sigmoid(lax.dot_general(xcb, wx, nn, preferred_element_type=F32) + bx)
    cl = -LRU_C * _softplus(-lam)
    log_a = cl * r
    a = jnp.exp(log_a)
    one_minus_a2 = _neg_expm1(2.0 * log_a)
    return xcb, r, i, a, one_minus_a2, cl


def _lru_fwd(p, conv_w, conv_b, wa_bd, ba, wx_bd, bx, lam, d_lru, gc, tc, dep=None):
    t = p.shape[0]
    ng = d_lru // gc
    nt = t // tc
    width = conv_w.shape[0]

    def body(lx_ref, gate_ref, cw_ref, cb_ref, wa_ref, ba_ref, wx_ref, bx_ref, lam_ref,
             y_ref, h_ref, halo, hcar, a_s, u_s):
        @pl.when(pl.program_id(1) == 0)
        def _():
            halo[...] = jnp.zeros_like(halo)
            hcar[...] = jnp.zeros_like(hcar)

        x = lx_ref[...]
        cat = jnp.concatenate([halo[...], x], axis=0)
        halo[...] = x[tc - SUBLANE:, :]
        xc = _conv_fwd(cat, cw_ref[...], width) + cb_ref[...]
        _, r, i, a, om, _ = _lru_gates(xc, wa_ref[...], ba_ref[...], wx_ref[...], bx_ref[...], lam_ref[...])
        a_s[...] = a
        u_s[...] = jnp.sqrt(om) * (i * xc)
        hcar[0:1, :] = _scan_tiles(a_s, u_s, h_ref, hcar[0:1, :], tc, reverse=False)
        gl, _ = _gelu_and_grad(gate_ref[...])
        y_ref[...] = (gl * h_ref[...]).astype(BF16)

    blk = lambda off: pl.BlockSpec((tc, gc), lambda g, s, off=off: (s, off + g))
    rowv = lambda rows: pl.BlockSpec((rows, gc), lambda g, s: (0, g))
    wspec = pl.BlockSpec((None, gc, gc), lambda g, s: (g, 0, 0))
    out_blk = pl.BlockSpec((tc, gc), lambda g, s: (s, g))
    body, in_specs, operands = _dep_args(
        body, [blk(0), blk(ng), rowv(width), rowv(1), wspec, rowv(1), wspec, rowv(1), rowv(1)],
        [p, p, conv_w, conv_b.reshape(1, -1), wa_bd, ba.reshape(1, -1), wx_bd, bx.reshape(1, -1),
         lam.reshape(1, -1)], dep)
    return pl.pallas_call(
        body,
        out_shape=[jax.ShapeDtypeStruct((t, d_lru), BF16), jax.ShapeDtypeStruct((t, d_lru), F32)],
        grid=(ng, nt),
        in_specs=in_specs,
        out_specs=[out_blk, out_blk],
        scratch_shapes=[pltpu.VMEM((SUBLANE, gc), F32), pltpu.VMEM((SUBLANE, gc), F32),
                        pltpu.VMEM((tc, gc), F32), pltpu.VMEM((tc, gc), F32)],
        compiler_params=_cparams(("parallel", "arbitrary"), 40 * tc * gc * 4),
        name="lru_fwd",
    )(*operands)


def _lru_bwd(p, hseq, dyp, conv_w, conv_b, wa_bd, ba, wx_bd, bx, lam, d_lru, gc, tc, dp, slab0, dep=None):
    t = p.shape[0]
    ng = d_lru // gc
    nt = t // tc
    width = conv_w.shape[0]
    halo_blocks = tc // SUBLANE
    nn = (((1,), (0,)), ((), ()))
    nt_dims = (((1,), (1,)), ((), ()))
    tn_dims = (((0,), (0,)), ((), ()))

    def body(lx_ref, lxh_ref, gate_ref, h_ref, hh_ref, dyp_ref,
             cw_ref, cb_ref, wa_ref, ba_ref, wx_ref, bx_ref, lam_ref,
             dp_ref, dcw_ref, dcb_ref, dwa_ref, dba_ref, dwx_ref, dbx_ref, dlam_ref,
             nxt_dxc, nxt_a, nxt_g, al_s, b_s, g_s):
        s = pl.program_id(1)
        first_chunk = s == nt - 1

        @pl.when(s == 0)
        def _():
            nxt_dxc[...] = jnp.zeros_like(nxt_dxc)
            nxt_a[...] = jnp.zeros_like(nxt_a)
            nxt_g[...] = jnp.zeros_like(nxt_g)
            for ref in (dcw_ref, dcb_ref, dwa_ref, dba_ref, dwx_ref, dbx_ref, dlam_ref):
                ref[...] = jnp.zeros_like(ref)

        keep = jnp.where(first_chunk, 0.0, 1.0)
        x = lx_ref[...]
        catx = jnp.concatenate([lxh_ref[...] * keep, x], axis=0)
        cw = cw_ref[...]
        xc = _conv_fwd(catx, cw, width) + cb_ref[...]
        wa = wa_ref[...]
        wx = wx_ref[...]
        lam_v = lam_ref[...]
        xcb, r, i, a, om, cl = _lru_gates(xc, wa, ba_ref[...], wx, bx_ref[...], lam_v)
        mult = jnp.sqrt(om)

        h = h_ref[...]
        hprev = _rows_before(jnp.concatenate([hh_ref[...] * keep, h], axis=0), 1)
        gl, dgl = _gelu_and_grad(gate_ref[...])
        dyp_v = dyp_ref[...]
        dp_ref[1] = (dyp_v * h * dgl).astype(BF16)

        al_s[...] = _rows_after(jnp.concatenate([a, nxt_a[...]], axis=0), 1)
        b_s[...] = dyp_v * gl
        nxt_g[0:1, :] = _scan_tiles(al_s, b_s, g_s, nxt_g[0:1, :], tc, reverse=True)
        nxt_a[...] = a[0:SUBLANE, :]
        du = g_s[...]

        da = du * hprev
        dmult = du * (i * xc)
        di = du * mult * xc
        dxc = du * mult * i
        dlog_a = da * a - dmult * (a * a / mult)
        dlam_ref[...] += jnp.sum(dlog_a * r, axis=0, keepdims=True) * (LRU_C * _sigmoid(-lam_v))
        dza = (dlog_a * cl) * r * (1.0 - r)
        dzx = di * i * (1.0 - i)
        dba_ref[...] += jnp.sum(dza, axis=0, keepdims=True)
        dbx_ref[...] += jnp.sum(dzx, axis=0, keepdims=True)
        dzab = dza.astype(BF16)
        dzxb = dzx.astype(BF16)
        dwa_ref[...] += lax.dot_general(xcb, dzab, tn_dims, preferred_element_type=F32)
        dwx_ref[...] += lax.dot_general(xcb, dzxb, tn_dims, preferred_element_type=F32)
        dxc = dxc + lax.dot_general(dzab, wa, nt_dims, preferred_element_type=F32)
        dxc = dxc + lax.dot_general(dzxb, wx, nt_dims, preferred_element_type=F32)
        dcb_ref[...] += jnp.sum(dxc, axis=0, keepdims=True)
        _conv_bwd_weight(dcw_ref, dxc, catx, width)
        catd = jnp.concatenate([dxc, nxt_dxc[...]], axis=0)
        dp_ref[0] = _conv_bwd_input(catd, cw, width).astype(BF16)
        nxt_dxc[...] = dxc[0:SUBLANE, :]

    rev = lambda s: nt - 1 - s
    blk = lambda off: pl.BlockSpec((tc, gc), lambda g, s, off=off: (rev(s), off + g))
    halo = lambda off: pl.BlockSpec(
        (SUBLANE, gc), lambda g, s, off=off: (jnp.maximum(rev(s) * halo_blocks - 1, 0), off + g))
    rowv = lambda rows: pl.BlockSpec((rows, gc), lambda g, s: (0, g))
    wspec = pl.BlockSpec((None, gc, gc), lambda g, s: (g, 0, 0))
    out_blk = pl.BlockSpec((tc, gc), lambda g, s: (rev(s), g))
    vec = lambda rows: jax.ShapeDtypeStruct((rows, d_lru), F32)
    wshape = jax.ShapeDtypeStruct((ng, gc, gc), F32)
    body, in_specs, operands = _dep_args(
        body,
        [blk(0), halo(0), blk(ng), blk(0), halo(0), blk(0),
         rowv(width), rowv(1), wspec, rowv(1), wspec, rowv(1), rowv(1)],
        [p, p, p, hseq, hseq, dyp,
         conv_w, conv_b.reshape(1, -1), wa_bd, ba.reshape(1, -1), wx_bd,
         bx.reshape(1, -1), lam.reshape(1, -1)], dp, dep)
    assert dp.shape[2] == d_lru and slab0 % 2 == 0
    return pl.pallas_call(
        body,
        out_shape=[jax.ShapeDtypeStruct(dp.shape, dp.dtype),
                   vec(width), vec(1), wshape, vec(1), wshape, vec(1), vec(1)],
        grid=(ng, nt),
        in_specs=in_specs,
        out_specs=[pl.BlockSpec((2, tc, gc), lambda g, s: (slab0 // 2, rev(s), g)),
                   rowv(width), rowv(1), wspec, rowv(1), wspec, rowv(1), rowv(1)],
        input_output_aliases={13: 0},
        scratch_shapes=[pltpu.VMEM((SUBLANE, gc), F32), pltpu.VMEM((SUBLANE, gc), F32),
                        pltpu.VMEM((SUBLANE, gc), F32),
                        pltpu.VMEM((tc, gc), F32), pltpu.VMEM((tc, gc), F32), pltpu.VMEM((tc, gc), F32)],
        compiler_params=_cparams(("parallel", "arbitrary"), 80 * tc * gc * 4),
        name="lru_bwd",
    )(*operands)


def _sc_fwd(p, conv_w, col0, d_sc, cb, tc):
    t = p.shape[0]
    nc = d_sc // cb
    nt = t // tc
    width = conv_w.shape[0]
    base = col0 // cb

    def body(b_ref, c_ref, v_ref, w_ref, y_ref, halo):
        @pl.when(pl.program_id(1) == 0)
        def _():
            halo[...] = jnp.zeros_like(halo)

        cv = c_ref[...] * v_ref[...]
        cat = jnp.concatenate([halo[...], cv], axis=0)
        halo[...] = cv[tc - SUBLANE:, :]
        y_ref[...] = (b_ref[...] * _conv_fwd(cat, w_ref[...], width)).astype(BF16)

    blk = lambda slab: pl.BlockSpec((tc, cb), lambda j, s, slab=slab: (s, base + slab * nc + j))
    return pl.pallas_call(
        body,
        out_shape=jax.ShapeDtypeStruct((t, d_sc), BF16),
        grid=(nc, nt),
        in_specs=[blk(0), blk(1), blk(2), pl.BlockSpec((width, cb), lambda j, s: (0, j))],
        out_specs=pl.BlockSpec((tc, cb), lambda j, s: (s, j)),
        scratch_shapes=[pltpu.VMEM((SUBLANE, cb), F32)],
        compiler_params=_cparams(("parallel", "arbitrary"), 20 * tc * cb * 4),
        name="sc_fwd",
    )(p, p, p, conv_w)


def _sc_bwd(p, dyp, conv_w, col0, d_sc, cb, tc, dp, slab0):
    t = p.shape[0]
    nc = d_sc // cb
    nt = t // tc
    width = conv_w.shape[0]
    base = col0 // cb
    halo_blocks = tc // SUBLANE

    def body(b_ref, c_ref, ch_ref, v_ref, vh_ref, dyp_ref, w_ref,
             dp_ref, dw_ref, nxt_dq):
        s = pl.program_id(1)

        @pl.when(s == 0)
        def _():
            nxt_dq[...] = jnp.zeros_like(nxt_dq)
            dw_ref[...] = jnp.zeros_like(dw_ref)

        keep = jnp.where(s == nt - 1, 0.0, 1.0)
        cvals = c_ref[...]
        vvals = v_ref[...]
        w = w_ref[...]
        catcv = jnp.concatenate([ch_ref[...] * vh_ref[...] * keep, cvals * vvals], axis=0)
        q = _conv_fwd(catcv, w, width)
        dyp_v = dyp_ref[...]
        dp_ref[0] = (dyp_v * q).astype(BF16)
        dq = dyp_v * b_ref[...]
        _conv_bwd_weight(dw_ref, dq, catcv, width)
        dcv = _conv_bwd_input(jnp.concatenate([dq, nxt_dq[...]], axis=0), w, width)
        nxt_dq[...] = dq[0:SUBLANE, :]
        dp_ref[1] = (dcv * vvals).astype(BF16)
        dp_ref[2] = (dcv * cvals).astype(BF16)

    rev = lambda s: nt - 1 - s
    blk = lambda slab: pl.BlockSpec((tc, cb), lambda j, s, slab=slab: (rev(s), base + slab * nc + j))
    halo = lambda slab: pl.BlockSpec(
        (SUBLANE, cb),
        lambda j, s, slab=slab: (jnp.maximum(rev(s) * halo_blocks - 1, 0), base + slab * nc + j))
    out_blk = pl.BlockSpec((tc, cb), lambda j, s: (rev(s), j))
    wblk = pl.BlockSpec((width, cb), lambda j, s: (0, j))
    assert dp.shape[2] == d_sc and slab0 % 3 == 0
    operands = [p, p, p, p, p, dyp, conv_w]
    body, in_specs, operands = _dep_args(
        body, [blk(0), blk(1), halo(1), blk(2), halo(2), out_blk, wblk], operands, dp)
    return pl.pallas_call(
        body,
        out_shape=[jax.ShapeDtypeStruct(dp.shape, dp.dtype), jax.ShapeDtypeStruct((width, d_sc), F32)],
        grid=(nc, nt),
        in_specs=in_specs,
        out_specs=[pl.BlockSpec((3, tc, cb), lambda j, s: (slab0 // 3, rev(s), j)), wblk],
        input_output_aliases={7: 0},
        scratch_shapes=[pltpu.VMEM((SUBLANE, cb), F32)],
        compiler_params=_cparams(("parallel", "arbitrary"), 30 * tc * cb * 4),
        name="sc_bwd",
    )(*operands)


def _merge_fwd(p, y_lru, y_sc, col0, tc):
    t, d = y_lru.shape
    cb = _pick(math.gcd(d, col0), 1024)
    nc = d // cb
    base = col0 // cb

    def body(gl_ref, gs_ref, yl_ref, ys_ref, o_ref):
        o_ref[...] = (_sigmoid(gl_ref[...]) * yl_ref[...] + _sigmoid(gs_ref[...]) * ys_ref[...]).astype(BF16)

    gate = lambda slab: pl.BlockSpec((tc, cb), lambda s, j, slab=slab: (s, base + slab * nc + j))
    blk = pl.BlockSpec((tc, cb), lambda s, j: (s, j))
    return pl.pallas_call(
        body,
        out_shape=jax.ShapeDtypeStruct((t, d), BF16),
        grid=(t // tc, nc),
        in_specs=[gate(0), gate(1), blk, blk],
        out_specs=blk,
        compiler_params=_cparams(("parallel", "parallel"), 2 * tc * cb * 20),
        name="merge_fwd",
    )(p, p, y_lru, y_sc)


def _merge_bwd(p, y_lru, y_sc, dmerged, col0, tc, n_slabs):
    t, d = y_lru.shape
    cb = _pick(math.gcd(d, col0), 1024)
    nc = d // cb
    base = col0 // cb

    def body(gl_ref, gs_ref, yl_ref, ys_ref, dm_ref, dp_ref, dyl_ref, dys_ref):
        dm = dm_ref[...]
        sl = _sigmoid(gl_ref[...])
        ss = _sigmoid(gs_ref[...])
        dp_ref[0] = (dm * yl_ref[...] * (sl * (1.0 - sl))).astype(BF16)
        dp_ref[1] = (dm * ys_ref[...] * (ss * (1.0 - ss))).astype(BF16)
        dyl_ref[...] = (dm * sl).astype(BF16)
        dys_ref[...] = (dm * ss).astype(BF16)

    gate = lambda slab: pl.BlockSpec((tc, cb), lambda s, j, slab=slab: (s, base + slab * nc + j))
    blk = pl.BlockSpec((tc, cb), lambda s, j: (s, j))
    act = jax.ShapeDtypeStruct((t, d), BF16)
    return pl.pallas_call(
        body,
        out_shape=[jax.ShapeDtypeStruct((n_slabs, t, cb), BF16), act, act],
        grid=(t // tc, nc),
        in_specs=[gate(0), gate(1), blk, blk, blk],
        out_specs=[pl.BlockSpec((2, tc, cb), lambda s, j: (j, s, 0)), blk, blk],
        compiler_params=_cparams(("parallel", "parallel"), 2 * tc * cb * 28),
        name="merge_bwd",
    )(p, p, y_lru, y_sc, dmerged)


def _ffn_act_fwd(up, conv_w, d_ff, cb, tc, dep=None):
    t = up.shape[0]
    nc = d_ff // cb
    nt = t // tc
    width = conv_w.shape[0]

    def body(g_ref, v_ref, wg_ref, wv_ref, o_ref, halo_g, halo_v):
        @pl.when(pl.program_id(1) == 0)
        def _():
            halo_g[...] = jnp.zeros_like(halo_g)
            halo_v[...] = jnp.zeros_like(halo_v)

        g = g_ref[...]
        v = v_ref[...]
        ug = _conv_fwd(jnp.concatenate([halo_g[...], g], axis=0), wg_ref[...], width)
        uv = _conv_fwd(jnp.concatenate([halo_v[...], v], axis=0), wv_ref[...], width)
        halo_g[...] = g[tc - SUBLANE:, :]
        halo_v[...] = v[tc - SUBLANE:, :]
        o_ref[...] = (ug * _sigmoid(ug) * uv).astype(BF16)

    blk = lambda half: pl.BlockSpec((tc, cb), lambda j, s, half=half: (s, half * nc + j))
    wblk = lambda half: pl.BlockSpec((width, cb), lambda j, s, half=half: (0, half * nc + j))
    body, in_specs, operands = _dep_args(
        body, [blk(0), blk(1), wblk(0), wblk(1)], [up, up, conv_w, conv_w], dep)
    return pl.pallas_call(
        body,
        out_shape=jax.ShapeDtypeStruct((t, d_ff), BF16),
        grid=(nc, nt),
        in_specs=in_specs,
        out_specs=pl.BlockSpec((tc, cb), lambda j, s: (s, j)),
        scratch_shapes=[pltpu.VMEM((SUBLANE, cb), F32), pltpu.VMEM((SUBLANE, cb), F32)],
        compiler_params=_cparams(("parallel", "arbitrary"), 24 * tc * cb * 4),
        name="ffn_act_fwd",
    )(*operands)


def _ffn_act_bwd(up, dact, conv_w, d_ff, cb, tc, dep=None):
    t = up.shape[0]
    nc = d_ff // cb
    nt = t // tc
    width = conv_w.shape[0]
    halo_blocks = tc // SUBLANE

    def body(g_ref, gh_ref, v_ref, vh_ref, da_ref, wg_ref, wv_ref,
             dup_ref, dwg_ref, dwv_ref, nxt_g, nxt_v):
        s = pl.program_id(1)

        @pl.when(s == 0)
        def _():
            nxt_g[...] = jnp.zeros_like(nxt_g)
            nxt_v[...] = jnp.zeros_like(nxt_v)
            dwg_ref[...] = jnp.zeros_like(dwg_ref)
            dwv_ref[...] = jnp.zeros_like(dwv_ref)

        keep = jnp.where(s == nt - 1, 0.0, 1.0)
        wg = wg_ref[...]
        wv = wv_ref[...]
        catg = jnp.concatenate([gh_ref[...] * keep, g_ref[...]], axis=0)
        catv = jnp.concatenate([vh_ref[...] * keep, v_ref[...]], axis=0)
        ug = _conv_fwd(catg, wg, width)
        uv = _conv_fwd(catv, wv, width)
        sg = _sigmoid(ug)
        da = da_ref[...]
        duv = da * (ug * sg)
        dup_ref[1] = _conv_bwd_input(jnp.concatenate([duv, nxt_v[...]], axis=0), wv, width).astype(BF16)
        nxt_v[...] = duv[0:SUBLANE, :]
        _conv_bwd_weight(dwv_ref, duv, catv, width)
        dug = da * uv * (sg * (1.0 + ug * (1.0 - sg)))
        dup_ref[0] = _conv_bwd_input(jnp.concatenate([dug, nxt_g[...]], axis=0), wg, width).astype(BF16)
        nxt_g[...] = dug[0:SUBLANE, :]
        _conv_bwd_weight(dwg_ref, dug, catg, width)

    rev = lambda s: nt - 1 - s
    blk = lambda half: pl.BlockSpec((tc, cb), lambda j, s, half=half: (rev(s), half * nc + j))
    halo = lambda half: pl.BlockSpec(
        (SUBLANE, cb), lambda j, s, half=half: (jnp.maximum(rev(s) * halo_blocks - 1, 0), half * nc + j))
    wblk = lambda half: pl.BlockSpec((width, cb), lambda j, s, half=half: (0, half * nc + j))
    out_blk = pl.BlockSpec((tc, cb), lambda j, s: (rev(s), j))
    wout = pl.BlockSpec((width, cb), lambda j, s: (0, j))
    act = jax.ShapeDtypeStruct((t, d_ff), BF16)
    wshape = jax.ShapeDtypeStruct((width, d_ff), F32)
    body, in_specs, operands = _dep_args(
        body, [blk(0), halo(0), blk(1), halo(1), out_blk, wblk(0), wblk(1)],
        [up, up, up, up, dact, conv_w, conv_w], dep)
    return pl.pallas_call(
        body,
        out_shape=[jax.ShapeDtypeStruct((2, t, d_ff), BF16), wshape, wshape],
        grid=(nc, nt),
        in_specs=in_specs,
        out_specs=[pl.BlockSpec((2, tc, cb), lambda j, s: (0, rev(s), j)), wout, wout],
        scratch_shapes=[pltpu.VMEM((SUBLANE, cb), F32), pltpu.VMEM((SUBLANE, cb), F32)],
        compiler_params=_cparams(("parallel", "arbitrary"), 40 * tc * cb * 4),
        name="ffn_act_bwd",
    )(*operands)


def _mesh_pos():
    x, y, c = lax.axis_index("x"), lax.axis_index("y"), lax.axis_index("c")
    return x, y, c


def _other_chips(x, y):
    return [(1 - x, y), (x, 1 - y), (1 - x, 1 - y)]


def _cast_place(w, chip, col_sharded, name, dep=None):
    r, cdim = w.shape
    full = (r, cdim * N_CHIPS) if col_sharded else (r * N_CHIPS, cdim)
    rb = _pick(r, max(BF16_ROWS, (512 * 1024) // cdim), BF16_ROWS)
    nb = r // rb

    def body(chip_ref, w_ref, o_ref):
        o_ref[...] = w_ref[...].astype(BF16)

    if col_sharded:
        out_map = lambda i, chip_ref: (i, chip_ref[0])
    else:
        out_map = lambda i, chip_ref: (chip_ref[0] * nb + i, 0)
    grid_spec = pltpu.PrefetchScalarGridSpec(
        num_scalar_prefetch=1,
        grid=(nb,),
        in_specs=[pl.BlockSpec((rb, cdim), lambda i, chip_ref: (i, 0))] + ([ANY] if dep is not None else []),
        out_specs=pl.BlockSpec((rb, cdim), out_map),
    )
    body, _, operands = _dep_args(body, [], [chip, w], dep)
    return pl.pallas_call(
        body,
        out_shape=jax.ShapeDtypeStruct(full, BF16),
        grid_spec=grid_spec,
        compiler_params=_cparams(("parallel",), 2 * rb * cdim * 6),
        name=name,
    )(*operands)


def _remote(src, dst, send_sems, recv_sems, idx, to):
    return pltpu.make_async_remote_copy(
        src_ref=src, dst_ref=dst, send_sem=send_sems.at[idx], recv_sem=recv_sems.at[idx],
        device_id=to, device_id_type=MESH)


def _exchange(name, arrays, n_sems, plan):
    n = len(arrays)

    def body(*refs):
        bufs = refs[n:2 * n]
        send_sems, recv_sems = refs[2 * n:]
        sends, arrivals = plan(bufs, send_sems, recv_sems)
        for cp in sends:
            cp.start()
        for cp in arrivals:
            cp.wait_recv()
        for cp in sends:
            cp.wait_send()

    outs = pl.pallas_call(
        body,
        out_shape=[jax.ShapeDtypeStruct(a.shape, a.dtype) for a in arrays],
        in_specs=[ANY] * n,
        out_specs=[ANY] * n,
        input_output_aliases={k: k for k in range(n)},
        scratch_shapes=[pltpu.SemaphoreType.DMA((n_sems,)), pltpu.SemaphoreType.DMA((n_sems,))],
        name=name,
    )(*arrays)
    return list(outs)


def _exchange_start(name, arrays, n_sems, plan, after=None):
    n = len(arrays)
    n_in = n + (after is not None)

    def body(*refs):
        bufs = refs[:n]
        send_sems, recv_sems = refs[n_in], refs[n_in + 1]
        token = refs[-1]
        sends, _ = plan(bufs, send_sems, recv_sems)
        for cp in sends:
            cp.start()
        token[...] = jnp.zeros_like(token)

    out = pl.pallas_call(
        body,
        out_shape=(pltpu.SemaphoreType.DMA((n_sems,)), pltpu.SemaphoreType.DMA((n_sems,)),
                   *[pltpu.HBM(a.shape, a.dtype) for a in arrays],
                   jax.ShapeDtypeStruct((SUBLANE, LANE), F32)),
        in_specs=[HBM_SPEC] * n + [ANY] * (n_in - n),
        out_specs=(SEM_SPEC, SEM_SPEC, *[HBM_SPEC] * n, VMEM_SPEC),
        input_output_aliases={k: 2 + k for k in range(n)},
        compiler_params=pltpu.CompilerParams(has_side_effects=DATAFLOW_EFFECT),
        name=name,
    )(*[pltpu.with_memory_space_constraint(a, pltpu.HBM) for a in arrays], *([after] if after is not None else []))
    return out[0], out[1], list(out[2:2 + n]), out[-1]


def _exchange_wait(name, arrays, send_sems, recv_sems, after, plan):
    n = len(arrays)

    def body(*refs):
        bufs = refs[:n]
        sends, arrivals = plan(bufs, refs[n], refs[n + 1])
        for cp in arrivals:
            cp.wait_recv()
        for cp in sends:
            cp.wait_send()

    outs = pl.pallas_call(
        body,
        out_shape=[pltpu.HBM(a.shape, a.dtype) for a in arrays],
        in_specs=[HBM_SPEC] * n + [SEM_SPEC, SEM_SPEC, ANY],
        out_specs=[HBM_SPEC] * n,
        input_output_aliases={k: k for k in range(n)},
        compiler_params=pltpu.CompilerParams(has_side_effects=DATAFLOW_EFFECT),
        name=name,
    )(*arrays, send_sems, recv_sems, after)
    return list(outs)


def _half_block(ref, shard_shape, col_sharded, chip, half):
    r, cdim = shard_shape
    h = r // 2
    if col_sharded:
        return ref.at[pl.ds(pl.multiple_of(half * h, BF16_ROWS), h),
                      pl.ds(pl.multiple_of(chip * cdim, LANE), cdim)]
    return ref.at[pl.ds(pl.multiple_of(chip * r + half * h, BF16_ROWS), h), :]


def _gather_plan(shard_shapes, col_sharded, ks):
    def plan(bufs, send_sems, recv_sems):
        x, y, c = _mesh_pos()
        sends, arrivals = [], []
        for ref, k in zip(bufs, ks):
            mine = _half_block(ref, shard_shapes[k], col_sharded[k], 2 * x + y, c)
            for j, (px, py) in enumerate(_other_chips(x, y)):
                landed = _half_block(ref, shard_shapes[k], col_sharded[k], 2 * px + py, c)
                sends.append(_remote(mine, mine, send_sems, recv_sems, 3 * k + j, (px, py, c)))
                arrivals.append(_remote(landed, landed, send_sems, recv_sems, 3 * k + j, (px, py, c)))
        return sends, arrivals
    return plan


def _forward_plan(shard_shapes, col_sharded, ks):
    def plan(bufs, send_sems, recv_sems):
        x, y, c = _mesh_pos()
        sends, arrivals = [], []
        for i, (ref, k) in enumerate(zip(bufs, ks)):
            for j, (px, py) in enumerate(_other_chips(x, y)):
                landed = _half_block(ref, shard_shapes[k], col_sharded[k], 2 * px + py, c)
                theirs = _half_block(ref, shard_shapes[k], col_sharded[k], 2 * px + py, 1 - c)
                sends.append(_remote(landed, landed, send_sems, recv_sems, 3 * i + j, (x, y, 1 - c)))
                arrivals.append(_remote(theirs, theirs, send_sems, recv_sems, 3 * i + j, (x, y, 1 - c)))
        return sends, arrivals
    return plan


def _small_gather(small):
    def body(small_ref, out_ref, send_sems, recv_sems):
        x, y, c = _mesh_pos()
        me = 2 * x + y
        out_ref[me] = small_ref[...]
        copies = []
        for j, (px, py) in enumerate(_other_chips(x, y)):
            cp = _remote(small_ref, out_ref.at[me], send_sems, recv_sems, j, (px, py, c))
            cp.start()
            copies.append(cp)
        for j, (px, py) in enumerate(_other_chips(x, y)):
            _remote(small_ref, out_ref.at[2 * px + py], send_sems, recv_sems, j, (px, py, c)).wait_recv()
        for cp in copies:
            cp.wait_send()

    return pl.pallas_call(
        body,
        out_shape=jax.ShapeDtypeStruct((N_CHIPS,) + small.shape, small.dtype),
        in_specs=[VMEM_SPEC],
        out_specs=VMEM_SPEC,
        scratch_shapes=[pltpu.SemaphoreType.DMA((N_CHIPS - 1,)), pltpu.SemaphoreType.DMA((N_CHIPS - 1,))],
        name="gather_small",
    )(small)


def _as3d(g, col_sharded):
    r, cdim = g.shape
    return g.reshape(1, r, cdim) if col_sharded else g.reshape(N_CHIPS, r // N_CHIPS, cdim)


def _pair_plan(m):
    def plan(bufs, send_sems, recv_sems):
        x, y, c = _mesh_pos()
        copies = []
        for i in range(m):
            h = bufs[i].shape[1] // 2
            src = bufs[i].at[:, pl.ds(pl.multiple_of((1 - c) * h, BF16_ROWS), h), :]
            copies.append(_remote(src, bufs[m + i], send_sems, recv_sems, i, (x, y, 1 - c)))
        return copies, copies
    return plan


def _chip_plan(col_flags):
    m = len(col_flags)

    def plan(bufs, send_sems, recv_sems):
        x, y, c = _mesh_pos()
        copies = []
        for i in range(m):
            land = bufs[m + i]
            width = land.shape[2]
            for j, (px, py) in enumerate(_other_chips(x, y)):
                q = 2 * px + py
                if col_flags[i]:
                    src = bufs[i].at[0, :, pl.ds(pl.multiple_of(q * width, LANE), width)]
                else:
                    src = bufs[i].at[q]
                copies.append(_remote(src, land.at[j], send_sems, recv_sems, 3 * i + j, (px, py, c)))
        return copies, copies
    return plan


def _share_plan(m):
    def plan(bufs, send_sems, recv_sems):
        x, y, c = _mesh_pos()
        sends, arrivals = [], []
        for i in range(m):
            h = bufs[i].shape[0] // 2
            mine = bufs[i].at[pl.ds(pl.multiple_of(c * h, SUBLANE), h), :]
            theirs = bufs[i].at[pl.ds(pl.multiple_of((1 - c) * h, SUBLANE), h), :]
            sends.append(_remote(mine, mine, send_sems, recv_sems, i, (x, y, 1 - c)))
            arrivals.append(_remote(theirs, theirs, send_sems, recv_sems, i, (x, y, 1 - c)))
        return sends, arrivals
    return plan


def _pair_add(g3, other, core):
    a, r, cdim = g3.shape
    h = r // 2
    rb = _pick(h, max(BF16_ROWS, (512 * 1024) // cdim), BF16_ROWS)
    nb = h // rb

    def body(core_ref, g_ref, o_ref, out_ref):
        out_ref[...] = (g_ref[...].astype(F32) + o_ref[...].astype(F32)).astype(BF16)

    grid_spec = pltpu.PrefetchScalarGridSpec(
        num_scalar_prefetch=1,
        grid=(a, nb),
        in_specs=[pl.BlockSpec((None, rb, cdim), lambda i, j, core_ref: (i, core_ref[0] * nb + j, 0)),
                  pl.BlockSpec((None, rb, cdim), lambda i, j, core_ref: (i, j, 0))],
        out_specs=pl.BlockSpec((None, rb, cdim), lambda i, j, core_ref: (i, j, 0)),
    )
    return pl.pallas_call(
        body,
        out_shape=jax.ShapeDtypeStruct((a, h, cdim), BF16),
        grid_spec=grid_spec,
        compiler_params=_cparams(("parallel", "parallel"), 2 * rb * cdim * 10),
        name="grad_pair_add",
    )(core, g3, other)


def _small_allreduce(small):
    rows = small.shape[0]
    pad = (-rows) % (2 * SUBLANE)
    if pad:
        small = jnp.pad(small, ((0, pad), (0, 0)))
    h = small.shape[0] // 2
    half_shape = (h, small.shape[1])

    def body(small_ref, out_ref, theirs, by_chip, send_sems, recv_sems):
        x, y, c = _mesh_pos()
        me = 2 * x + y
        sibling = (x, y, 1 - c)
        mine = pl.ds(pl.multiple_of(c * h, SUBLANE), h)
        other = pl.ds(pl.multiple_of((1 - c) * h, SUBLANE), h)
        swap = _remote(small_ref, theirs, send_sems, recv_sems, 0, sibling)
        swap.start()
        swap.wait()
        by_chip[me] = small_ref[mine, :] + theirs[mine, :]
        copies = []
        for j, (px, py) in enumerate(_other_chips(x, y)):
            cp = _remote(by_chip.at[me], by_chip.at[me], send_sems, recv_sems, 1 + j, (px, py, c))
            cp.start()
            copies.append(cp)
        for j, (px, py) in enumerate(_other_chips(x, y)):
            landed = by_chip.at[2 * px + py]
            _remote(landed, landed, send_sems, recv_sems, 1 + j, (px, py, c)).wait_recv()
        total = by_chip[0]
        for q in range(1, N_CHIPS):
            total = total + by_chip[q]
        out_ref[mine, :] = total
        for cp in copies:
            cp.wait_send()
        share = _remote(out_ref.at[mine, :], out_ref.at[mine, :], send_sems, recv_sems, 4, sibling)
        share.start()
        _remote(out_ref.at[other, :], out_ref.at[other, :], send_sems, recv_sems, 4, sibling).wait_recv()
        share.wait_send()

    out = pl.pallas_call(
        body,
        out_shape=jax.ShapeDtypeStruct(small.shape, F32),
        in_specs=[VMEM_SPEC],
        out_specs=VMEM_SPEC,
        scratch_shapes=[pltpu.VMEM(small.shape, F32), pltpu.VMEM((N_CHIPS,) + half_shape, F32),
                        pltpu.SemaphoreType.DMA((5,)), pltpu.SemaphoreType.DMA((5,))],
        compiler_params=pltpu.CompilerParams(
            vmem_limit_bytes=min(VMEM_BUDGET, 8 * _nbytes(small.shape, F32) + (8 << 20))),
        name="grad_small_allreduce",
    )(small)
    return out[:rows]


def _chip_sum(partial, land, where, col_sharded):
    _, h, cdim = land.shape
    rb = _pick(h, max(BF16_ROWS, (512 * 1024) // cdim), BF16_ROWS)
    nb = h // rb

    def body(where_ref, own_ref, l_ref, o_ref):
        total = own_ref[...].astype(F32)
        for j in range(N_CHIPS - 1):
            total = total + l_ref[j].astype(F32)
        o_ref[...] = total

    if col_sharded:
        own_map = lambda i, w: (0, i, w[0])
    else:
        own_map = lambda i, w: (w[0], i, 0)
    grid_spec = pltpu.PrefetchScalarGridSpec(
        num_scalar_prefetch=1,
        grid=(nb,),
        in_specs=[pl.BlockSpec((None, rb, cdim), own_map),
                  pl.BlockSpec((N_CHIPS - 1, rb, cdim), lambda i, w: (0, i, 0))],
        out_specs=pl.BlockSpec((rb, cdim), lambda i, w: (w[1] * nb + i, 0)),
    )
    return pl.pallas_call(
        body,
        out_shape=jax.ShapeDtypeStruct((2 * h, cdim), F32),
        grid_spec=grid_spec,
        compiler_params=_cparams(("parallel",), 2 * rb * cdim * 12),
        name="grad_chip_sum",
    )(where, partial, land)


def _adamw(w, g, m, v, name, dep=None):
    r, cdim = w.shape
    rb = _pick(r, max(SUBLANE, (256 * 1024) // cdim), SUBLANE)
    c1 = 1.0 - ADAM_B1 ** ADAM_STEP
    c2 = 1.0 - ADAM_B2 ** ADAM_STEP

    def body(w_ref, g_ref, m_ref, v_ref, go_ref, d_ref, mo_ref, vo_ref):
        gv = g_ref[...]
        mn = ADAM_B1 * m_ref[...] + (1.0 - ADAM_B1) * gv
        vn = ADAM_B2 * v_ref[...] + (1.0 - ADAM_B2) * (gv * gv)
        m_hat = mn / c1
        v_hat = vn / c2
        d_ref[...] = -ADAM_LR * (m_hat / (jnp.sqrt(v_hat) + ADAM_EPS) + ADAM_WD * w_ref[...])
        go_ref[...] = gv
        mo_ref[...] = mn
        vo_ref[...] = vn

    blk = pl.BlockSpec((rb, cdim), lambda i: (i, 0))
    shape = jax.ShapeDtypeStruct((r, cdim), F32)
    body, in_specs, operands = _dep_args(body, [blk] * 4, [w, g, m, v], dep)
    return pl.pallas_call(
        body,
        out_shape=[shape] * 4,
        grid=(r // rb,),
        in_specs=in_specs,
        out_specs=[blk] * 4,
        compiler_params=_cparams(("parallel",), 2 * rb * cdim * 4 * 8),
        name=name,
    )(*operands)


def _pack(arrays):
    tile = SUBLANE * LANE
    pieces = []
    for arr in arrays:
        flat = arr.reshape(-1)
        pad = (-flat.shape[0]) % tile
        if pad:
            flat = jnp.concatenate([flat, jnp.zeros((pad,), flat.dtype)])
        pieces.append(flat)
    return jnp.concatenate(pieces).reshape(-1, LANE)


def _unpack(packed, shapes):
    tile = SUBLANE * LANE
    flat = packed.reshape(-1)
    out, off = [], 0
    for shp in shapes:
        size = math.prod(shp)
        out.append(flat[off:off + size].reshape(shp))
        off += size + ((-size) % tile)
    return out


def _block_diag_groups(w, per_group):
    hcount, hd, _ = w.shape
    ng = hcount // per_group
    w4 = w.reshape(ng, per_group, hd, hd)
    eye = jnp.eye(per_group, dtype=w.dtype)
    bd = w4[:, :, :, None, :] * eye[None, :, None, :, None]
    return bd.reshape(ng, per_group * hd, per_group * hd).astype(BF16)


def _diag_blocks(wbd, per_group, hd):
    ng = wbd.shape[0]
    w5 = wbd.reshape(ng, per_group, hd, per_group, hd)
    blocks = [w5[:, i, :, i, :] for i in range(per_group)]
    return jnp.stack(blocks, axis=1).reshape(ng * per_group, hd, hd)


def kernel(x, g_mix, w_in, lru_conv_w, lru_conv_b, lru_wa, lru_ba, lru_wx, lru_bx, lru_lambda, lru_w_out, sc_conv_w, sc_w_out, w_o, g_ffn, ffn_w_up, ffn_conv_w, ffn_w_down, g_final, loss_target, m_g_mix, m_w_in, m_lru_conv_w, m_lru_conv_b, m_lru_wa, m_lru_ba, m_lru_wx, m_lru_bx, m_lru_lambda, m_lru_w_out, m_sc_conv_w, m_sc_w_out, m_w_o, m_g_ffn, m_ffn_w_up, m_ffn_conv_w, m_ffn_w_down, m_g_final, v_g_mix, v_w_in, v_lru_conv_w, v_lru_conv_b, v_lru_wa, v_lru_ba, v_lru_wx, v_lru_bx, v_lru_lambda, v_lru_w_out, v_sc_conv_w, v_sc_w_out, v_w_o, v_g_ffn, v_ffn_w_up, v_ffn_conv_w, v_ffn_w_down, v_g_final):
    seq, d_model = x.shape[1], x.shape[2]
    heads, head_dim, _ = lru_wa.shape
    d_lru = heads * head_dim
    d_sc = sc_w_out.shape[0]
    d_ff = ffn_w_down.shape[0] * N_CHIPS
    assert x.shape[0] == 1 and w_in.shape[1] * N_CHIPS == 2 * d_lru + 3 * d_sc + 2 * d_model
    xs = x.reshape(seq, d_model)
    target = loss_target.reshape(seq, d_model)

    chip = 2 * lax.axis_index("x") + lax.axis_index("y")
    core = lax.axis_index("c").astype(jnp.int32).reshape(1)

    big_w = [w_in, lru_w_out, sc_w_out, w_o, ffn_w_up, ffn_w_down]
    big_m = [m_w_in, m_lru_w_out, m_sc_w_out, m_w_o, m_ffn_w_up, m_ffn_w_down]
    big_v = [v_w_in, v_lru_w_out, v_sc_w_out, v_w_o, v_ffn_w_up, v_ffn_w_down]
    col_sharded = [True, True, True, False, True, False]
    conv_shards = [lru_conv_w, sc_conv_w, ffn_conv_w]
    conv_pack = jnp.concatenate(
        [jnp.pad(w, ((0, SUBLANE - w.shape[0]), (0, 0))) for w in conv_shards], axis=1)
    big_names = ["w_in", "lru_w_out", "sc_w_out", "w_o", "ffn_w_up", "ffn_w_down"]
    chip_arr = chip.astype(jnp.int32).reshape(1)
    placed = [_cast_place(big_w[0], chip_arr, col_sharded[0], "cast_" + big_names[0])]
    conv_all = _small_gather(conv_pack)
    shard_shapes = [w.shape for w in big_w]
    n_big = len(big_w)

    def gather_start(ks, after, tag):
        send, recv, bufs, token = _exchange_start(
            "gather_start_" + tag, [placed[k] for k in ks], 3 * n_big,
            _gather_plan(shard_shapes, col_sharded, ks), after=after)
        return (send, recv, dict(zip(ks, bufs))), token

    def arrived(state, ks, after, tag):
        send, recv, bufs = state
        got = _exchange_wait("gather_wait_" + tag, [bufs[k] for k in ks], send, recv, after,
                             _gather_plan(shard_shapes, col_sharded, ks))
        return _exchange("gather_forward_" + tag, got, 3 * len(ks), _forward_plan(shard_shapes, col_sharded, ks))

    def arrived_behind(state, ks, after, tag):
        send, recv, bufs = state
        got = _exchange_wait("gather_wait_" + tag, [bufs[k] for k in ks], send, recv, after,
                             _gather_plan(shard_shapes, col_sharded, ks))
        plan = _forward_plan(shard_shapes, col_sharded, ks)
        send, recv, got, token = _exchange_start("gather_forward_start_" + tag, got, 3 * len(ks), plan)
        return (send, recv, got, plan, tag), token

    def forwarded(state, after):
        send, recv, got, plan, tag = state
        return _exchange_wait("gather_forward_wait_" + tag, got, send, recv, after, plan)

    conv_full, off = [], 0
    for w in conv_shards:
        kw, nq = w.shape
        piece = conv_all[:, :kw, off:off + nq]
        conv_full.append(piece.transpose(1, 0, 2).reshape(kw, N_CHIPS * nq))
        off += nq
    lcw, scw, fcw = conv_full

    per_group = max(1, min(heads, 256 // head_dim))
    gc = per_group * head_dim
    wa_bd = _block_diag_groups(lru_wa, per_group)
    wx_bd = _block_diag_groups(lru_wx, per_group)
    tc = _pick(seq, 256, SUBLANE)
    cb_sc = _pick(d_sc, 512)
    cb_ff = _pick(d_ff, 128)
    tc_ff = _pick(seq, 2048, SUBLANE)
    tc_lru = _pick(seq, 1024, SUBLANE)
    col_sc = 2 * d_lru
    col_gates = 2 * d_lru + 3 * d_sc

    first, token = gather_start([0], conv_all, "in")
    for k in range(1, n_big):
        placed.append(_cast_place(big_w[k], chip_arr, col_sharded[k], "cast_" + big_names[k], dep=token))
        token = placed[-1]
    h1 = _rms_fwd(xs, g_mix, "rms_mix", dep=token)
    (win_b,) = arrived(first, [0], h1, "in")
    rest, token = gather_start([1, 2, 3, 4, 5], win_b, "rest")
    p = _mm(h1, win_b, "nn", F32, name="mm_in", dep=token)
    mix, token = arrived_behind(rest, [1, 2, 3], p, "mix")
    y_lru_pre, hseq = _lru_fwd(p, lcw, lru_conv_b, wa_bd, lru_ba, wx_bd, lru_bx, lru_lambda, d_lru, gc, tc_lru,
                               dep=token)
    y_sc_pre = _sc_fwd(p, scw, col_sc, d_sc, cb_sc, tc)
    wlo_b, wso_b, wo_b = forwarded(mix, y_sc_pre)
    y_lru = _mm(y_lru_pre, wlo_b, "nn", BF16, name="mm_lru_out")
    y_sc = _mm(y_sc_pre, wso_b, "nn", BF16, name="mm_sc_out")
    merged = _merge_fwd(p, y_lru, y_sc, col_gates, tc)
    x2 = _mm(merged, wo_b, "nn", F32, res=xs, name="mm_o")
    (wup_b,) = arrived(rest, [4], x2, "up")
    h2 = _rms_fwd(x2, g_ffn, "rms_ffn")
    up = _mm(h2, wup_b, "nn", F32, name="mm_up")
    down, token = arrived_behind(rest, [5], up, "down")
    act = _ffn_act_fwd(up, fcw, d_ff, cb_ff, tc_ff, dep=token)
    (wdn_b,) = forwarded(down, act)
    x3 = _mm(act, wdn_b, "nn", F32, res=x2, name="mm_down")
    loss_part, dx3, dx3b, dg_final = _loss_head(x3, g_final, target)

    where = jnp.concatenate([chip_arr, core])

    def reduce_start(grads, flags, tag):
        views = [_as3d(g, cs) for g, cs in zip(grads, flags)]
        lands = [lax.empty((v.shape[0], v.shape[1] // 2, v.shape[2]), v.dtype) for v in views]
        send, recv, bufs, token = _exchange_start("grad_pair_start_" + tag, views + lands, len(views),
                                                  _pair_plan(len(views)))
        return (send, recv, bufs, flags, tag), token

    def reduce_mid(state, after):
        send, recv, bufs, flags, tag = state
        m = len(flags)
        bufs = _exchange_wait("grad_pair_wait_" + tag, bufs, send, recv, after, _pair_plan(m))
        partials = [_pair_add(bufs[i], bufs[m + i], core) for i in range(m)]
        lands = []
        for pz, cs in zip(partials, flags):
            _, h, cdim = pz.shape
            lands.append(lax.empty((N_CHIPS - 1, h, cdim // N_CHIPS if cs else cdim), BF16))
        send, recv, bufs, token = _exchange_start("grad_chip_start_" + tag, partials + lands, 3 * m,
                                                  _chip_plan(flags))
        return (send, recv, bufs, flags, tag), token

    def reduce_end(state, after):
        send, recv, bufs, flags, tag = state
        m = len(flags)
        bufs = _exchange_wait("grad_chip_wait_" + tag, bufs, send, recv, after, _chip_plan(flags))
        return [_chip_sum(bufs[i], bufs[m + i], where, flags[i]) for i in range(m)]

    g_wdn = _mm(act, dx3b, "tn", F32, name="mm_down_dw")
    red_down, token = reduce_start([g_wdn], [False], "down")
    dact = _mm(dx3b, wdn_b, "nt", F32, name="mm_down_dx", dep=token)
    red_down, token = reduce_mid(red_down, dact)
    dup, dfcw_g, dfcw_v = _ffn_act_bwd(up, dact, fcw, d_ff, cb_ff, tc_ff, dep=token)
    g_wup = _mm(h2, dup, "tn", BF16, name="mm_up_dw", slabs=[0, 1])
    red_up, token = reduce_start([g_wup], [True], "up")
    dh2 = _mm(dup, wup_b, "nt", F32, name="mm_up_dx", dep=token, slabs=[0, 1])
    red_up, token = reduce_mid(red_up, dh2)
    dx2, dx2b, dg_ffn = _rms_bwd(x2, g_ffn, dh2, dx3, "rms_ffn_bwd", True, dep=token)
    g_wo = _mm(merged, dx2b, "tn", BF16, name="mm_o_dw")
    dmerged = _mm(dx2b, wo_b, "nt", BF16, name="mm_o_dx")
    slab_w = d_lru
    assert d_sc == slab_w and d_model % slab_w == 0 and col_gates % slab_w == 0
    n_gate = d_model // slab_w
    gate0 = col_gates // slab_w
    dp_slabs = [gate0 + kind * n_gate + j for j in range(n_gate) for kind in (0, 1)] + [0, 1, 2, 3, 4]
    dp, dyl, dys = _merge_bwd(p, y_lru, y_sc, dmerged, col_gates, tc, len(dp_slabs))
    assert dp.shape[2] == slab_w
    g_wlo = _mm(y_lru_pre, dyl, "tn", BF16, name="mm_lru_out_dw")
    g_wso = _mm(y_sc_pre, dys, "tn", BF16, name="mm_sc_out_dw")
    red_mix, token = reduce_start([g_wlo, g_wso, g_wo], [True, True, False], "mix")
    dylp = _mm(dyl, wlo_b, "nt", F32, name="mm_lru_out_dx", dep=token)
    dysp = _mm(dys, wso_b, "nt", F32, name="mm_sc_out_dx")
    red_mix, token = reduce_mid(red_mix, dysp)
    dp, dlcw, dlcb, dwa_bd, dba, dwx_bd, dbx, dlam = _lru_bwd(
        p, hseq, dylp, lcw, lru_conv_b, wa_bd, lru_ba, wx_bd, lru_bx, lru_lambda, d_lru, gc, tc_lru,
        dp, 2 * n_gate, dep=token)
    dp, dscw = _sc_bwd(p, dysp, scw, col_sc, d_sc, cb_sc, tc, dp, 2 * n_gate + 2)
    g_win = _mm(h1, dp, "tn", BF16, name="mm_in_dw", slabs=dp_slabs)
    red_in, token = reduce_start([g_win], [True], "in")
    dh1 = _mm(dp, win_b, "nt", F32, name="mm_in_dx", dep=token, slabs=dp_slabs)
    grad_x, dg_mix = _rms_bwd(xs, g_mix, dh1, dx2, "rms_mix_bwd", False)

    small_g = [dg_mix, dlcw, dlcb, _diag_blocks(dwa_bd, per_group, head_dim), dba,
               _diag_blocks(dwx_bd, per_group, head_dim), dbx, dlam, dscw, dg_ffn,
               jnp.concatenate([dfcw_g, dfcw_v], axis=1), dg_final]
    small_shapes = [a.shape for a in small_g]
    small_sum = _small_allreduce(_pack(small_g))
    red_in, token = reduce_mid(red_in, small_sum)
    (h_wdn,) = reduce_end(red_down, token)
    (h_wup,) = reduce_end(red_up, token)
    h_wlo, h_wso, h_wo = reduce_end(red_mix, token)
    s_wlo, s_wso, s_wo, s_wup, s_wdn = _exchange("grad_share_a", [h_wlo, h_wso, h_wo, h_wup, h_wdn], 5,
                                                 _share_plan(5))
    early = {1: s_wlo, 2: s_wso, 3: s_wo, 4: s_wup, 5: s_wdn}
    big_out = [None] * n_big
    last = None
    for k, g in early.items():
        big_out[k] = _adamw(big_w[k], g, big_m[k], big_v[k], "adamw_" + big_names[k], dep=last)
        last = big_out[k][1]
    (h_win,) = reduce_end(red_in, last)
    (s_win,) = _exchange("grad_share_b", [h_win], 1, _share_plan(1))
    big_out[0] = _adamw(big_w[0], s_win, big_m[0], big_v[0], "adamw_" + big_names[0])
    sg = _unpack(small_sum, small_shapes)
    for idx in (1, 8, 10):
        nq = sg[idx].shape[1] // N_CHIPS
        sg[idx] = lax.dynamic_slice_in_dim(sg[idx], chip * nq, nq, axis=1)
    small_w = [g_mix, lru_conv_w, lru_conv_b, lru_wa, lru_ba, lru_wx, lru_bx, lru_lambda, sc_conv_w,
               g_ffn, ffn_conv_w, g_final]
    small_m = [m_g_mix, m_lru_conv_w, m_lru_conv_b, m_lru_wa, m_lru_ba, m_lru_wx, m_lru_bx, m_lru_lambda,
               m_sc_conv_w, m_g_ffn, m_ffn_conv_w, m_g_final]
    small_v = [v_g_mix, v_lru_conv_w, v_lru_conv_b, v_lru_wa, v_lru_ba, v_lru_wx, v_lru_bx, v_lru_lambda,
               v_sc_conv_w, v_g_ffn, v_ffn_conv_w, v_g_final]
    sg = [g.reshape(w.shape) for g, w in zip(sg, small_w)]
    w_shapes = [w.shape for w in small_w]
    packed = _adamw(_pack(small_w), _pack(sg), _pack(small_m), _pack(small_v), "adamw_small")
    small_out = [_unpack(pk, w_shapes) for pk in packed]

    order = [(0, 0), (1, 0), (0, 1), (0, 2), (0, 3), (0, 4), (0, 5), (0, 6), (0, 7), (1, 1), (0, 8), (1, 2),
             (1, 3), (0, 9), (1, 4), (0, 10), (1, 5), (0, 11)]
    by_kind = []
    for kind in range(4):
        by_kind.append([big_out[i][kind] if is_big else small_out[kind][i] for is_big, i in order])
    loss = lax.psum(loss_part[0, 0], ("x", "y", "c"))
    return (loss, grad_x.reshape(x.shape), *by_kind[0], *by_kind[1], *by_kind[2], *by_kind[3])
```

```python
import math

import jax
import jax.numpy as jnp
from jax import lax
from jax.experimental import pallas as pl
from jax.experimental.pallas import tpu as pltpu

F32 = jnp.float32
BF16 = jnp.bfloat16

LANE = 128
SUBLANE = 8
BF16_ROWS = 16
VMEM_BYTES_V7X = 64 * 1024 * 1024
VMEM_BUDGET = VMEM_BYTES_V7X - 8 * 1024 * 1024
MM_VMEM_BUDGET = 42 * 1024 * 1024
STREAM_BLOCK = 2 * 1024 * 1024
EPS = 1e-6
LRU_C = 8.0
ADAM_LR = 0.001
ADAM_B1 = 0.9
ADAM_B2 = 0.999
ADAM_EPS = 1e-08
ADAM_WD = 0.01
ADAM_STEP = 10

N_CHIPS = 4
N_DEV = 8
MESH = pl.DeviceIdType.MESH
ANY = pl.BlockSpec(memory_space=pl.ANY)
VMEM_SPEC = pl.BlockSpec(memory_space=pltpu.VMEM)
HBM_SPEC = pl.BlockSpec(memory_space=pltpu.HBM)
SEM_SPEC = pl.BlockSpec(memory_space=pltpu.SEMAPHORE)
DATAFLOW_EFFECT = pltpu.SideEffectType.DATAFLOW_SIDE_EFFECTING


def _pick(n, cap, mult=LANE):
    best = None
    d = mult
    while d <= min(n, cap):
        if n % d == 0:
            best = d
        d += mult
    return n if best is None else best


def _cparams(semantics, block_bytes):
    limit = min(VMEM_BUDGET, max(32 * 1024 * 1024, int(block_bytes * 1.25) + (4 << 20)))
    return pltpu.CompilerParams(dimension_semantics=semantics, vmem_limit_bytes=limit)


def _nbytes(shape, dtype):
    return math.prod(shape) * jnp.dtype(dtype).itemsize


def _sigmoid(z):
    return 0.5 * jnp.tanh(0.5 * z) + 0.5


def _softplus(z):
    e = jnp.exp(-jnp.abs(z))
    u = 1.0 + e
    log1p = jnp.where(u == 1.0, e, jnp.log(u) * (e / (u - 1.0)))
    return jnp.maximum(z, 0.0) + log1p


def _neg_expm1(z):
    small = z * (1.0 + z * (0.5 + z * (1.0 / 6.0 + z * (1.0 / 24.0))))
    return -jnp.where(jnp.abs(z) < 0.03, small, jnp.exp(z) - 1.0)


_GELU_K = math.sqrt(2.0 / math.pi)
_GELU_C = 0.044715


def _gelu_and_grad(z):
    z2 = z * z
    th = jnp.tanh(_GELU_K * (z + _GELU_C * z2 * z))
    val = 0.5 * z * (1.0 + th)
    grad = 0.5 * (1.0 + th) + 0.5 * z * (1.0 - th * th) * (_GELU_K * (1.0 + 3.0 * _GELU_C * z2))
    return val, grad


def _rows_before(cat, k):
    if k == 0:
        return cat[SUBLANE:, :]
    return pltpu.roll(cat, k, 0)[SUBLANE:, :]


def _rows_after(cat, k):
    n = cat.shape[0]
    if k == 0:
        return cat[:n - SUBLANE, :]
    return pltpu.roll(cat, n - k, 0)[:n - SUBLANE, :]


def _conv_fwd(cat, w, width):
    y = _rows_before(cat, width - 1) * w[0:1, :]
    for k in range(1, width):
        y = y + _rows_before(cat, width - 1 - k) * w[k:k + 1, :]
    return y


def _conv_bwd_input(cat, w, width):
    dx = _rows_after(cat, width - 1) * w[0:1, :]
    for k in range(1, width):
        dx = dx + _rows_after(cat, width - 1 - k) * w[k:k + 1, :]
    return dx


def _conv_bwd_weight(dw_ref, dy, catx, width):
    for k in range(width):
        dw_ref[k:k + 1, :] += jnp.sum(dy * _rows_before(catx, width - 1 - k), axis=0, keepdims=True)


def _scan_tiles(a_ref, b_ref, out_ref, carry0, n_rows, reverse):
    cols = a_ref.shape[1]
    row = lax.broadcasted_iota(jnp.int32, (SUBLANE, cols), 0)
    n_tiles = n_rows // SUBLANE

    def step(j, carry):
        tile = (n_tiles - 1 - j) if reverse else j
        off = pl.multiple_of(tile * SUBLANE, SUBLANE)
        a = a_ref[pl.ds(off, SUBLANE), :]
        b = b_ref[pl.ds(off, SUBLANE), :]
        for s in (1, 2, 4):
            if reverse:
                keep = row < SUBLANE - s
                shift = SUBLANE - s
            else:
                keep = row >= s
                shift = s
            a_sh = jnp.where(keep, pltpu.roll(a, shift, 0), 1.0)
            b_sh = jnp.where(keep, pltpu.roll(b, shift, 0), 0.0)
            b = a * b_sh + b
            a = a * a_sh
        out = a * carry + b
        out_ref[pl.ds(off, SUBLANE), :] = out
        return out[0:1, :] if reverse else out[SUBLANE - 1:SUBLANE, :]

    return lax.fori_loop(0, n_tiles, step, carry0)


def _dep_args(body, in_specs, operands, *deps):
    deps = [d for d in deps if d is not None]
    if not deps:
        return body, in_specs, operands
    n = len(operands)

    def wrapped(*refs):
        return body(*refs[:n], *refs[n + len(deps):])

    return wrapped, list(in_specs) + [ANY] * len(deps), list(operands) + deps


def _mm(a, b, mode, out_dtype, res=None, name=None, dep=None, slabs=None):
    assert a.dtype == BF16 and b.dtype == BF16
    a_slabbed, b_slabbed = a.ndim == 3, b.ndim == 3
    assert not a_slabbed or (mode == "nt" and slabs is not None)
    assert not b_slabbed or (mode == "tn" and slabs is not None)
    if mode == "nn":
        (m, k), (k2, n) = a.shape, b.shape
        dims = (((1,), (0,)), ((), ()))
    elif mode == "nt":
        m, k = (a.shape[1], a.shape[0] * a.shape[2]) if a_slabbed else a.shape
        n, k2 = b.shape
        dims = (((1,), (1,)), ((), ()))
    else:
        k, m = a.shape
        k2, n = (b.shape[1], b.shape[0] * b.shape[2]) if b_slabbed else b.shape
        dims = (((0,), (0,)), ((), ()))
    assert k == k2
    n_unit = b.shape[2] if b_slabbed else n
    out_bytes = jnp.dtype(out_dtype).itemsize
    bm = _pick(m, 1024)
    bn = _pick(n_unit, 1024)
    bk = k

    def est(bm_, bn_, bk_):
        e = 2 * (bm_ * bk_ + bk_ * bn_) * 2 + 2 * bm_ * bn_ * out_bytes
        if k // bk_ > 1:
            e += bm_ * bn_ * 4
        if res is not None:
            e += 2 * bm_ * bn_ * 4
        return e

    for shrink_n, floor in ((True, 512), (False, 512), (True, 256), (False, 256)):
        while est(bm, bn, bk) > MM_VMEM_BUDGET:
            if shrink_n and bn > floor and bn % 2 == 0 and n_unit % (bn // 2) == 0:
                bn //= 2
            elif not shrink_n and bm > floor and bm % 2 == 0 and m % (bm // 2) == 0:
                bm //= 2
            else:
                break
    while (est(bm, bn, bk) > MM_VMEM_BUDGET and not a_slabbed and bk % (2 * LANE) == 0
           and k % (bk // 2) == 0):
        bk //= 2
    nk = k // bk
    per_slab = n_unit // bn

    def out_col(j):
        if not b_slabbed:
            return j
        s = j // per_slab
        where = sum(jnp.where(s == t, slabs[t], 0) for t in range(len(slabs)))
        return where * per_slab + j % per_slab

    if mode == "tn":
        a_spec = pl.BlockSpec((bk, bm), lambda i, j, kk: (kk, i))
    elif a_slabbed:
        a_spec = pl.BlockSpec((a.shape[0], bm, a.shape[2]), lambda i, j, kk: (0, i, 0))
    else:
        a_spec = pl.BlockSpec((bm, bk), lambda i, j, kk: (i, kk))
    if mode == "nt":
        b_spec = pl.BlockSpec((bn, bk), lambda i, j, kk: (j, kk))
    elif b_slabbed:
        b_spec = pl.BlockSpec((None, bk, bn), lambda i, j, kk: (j // per_slab, kk, j % per_slab))
    else:
        b_spec = pl.BlockSpec((bk, bn), lambda i, j, kk: (kk, j))
    o_spec = pl.BlockSpec((bm, bn), lambda i, j, kk: (i, out_col(j)))
    in_specs = [a_spec, b_spec]
    operands = [a, b]
    if res is not None:
        in_specs.append(o_spec)
        operands.append(res)
    has_res = res is not None

    def body(*refs):
        a_ref, b_ref = refs[0], refs[1]
        res_ref = refs[2] if has_res else None
        o_ref = refs[2 + has_res]
        if a_slabbed:
            width = a_ref.shape[2]
            part = None
            for s, col in enumerate(slabs):
                term = lax.dot_general(a_ref[s], b_ref[:, col * width:(col + 1) * width], dims,
                                       preferred_element_type=F32)
                part = term if part is None else part + term
        else:
            part = lax.dot_general(a_ref[...], b_ref[...], dims, preferred_element_type=F32)
        if nk == 1:
            if has_res:
                part = part + res_ref[...]
            o_ref[...] = part.astype(o_ref.dtype)
            return
        acc_ref = refs[-1]
        kk = pl.program_id(2)

        @pl.when(kk == 0)
        def _():
            acc_ref[...] = part

        @pl.when(kk > 0)
        def _():
            acc_ref[...] += part

        @pl.when(kk == nk - 1)
        def _():
            total = acc_ref[...]
            if has_res:
                total = total + res_ref[...]
            o_ref[...] = total.astype(o_ref.dtype)

    scratch = [pltpu.VMEM((bm, bn), F32)] if nk > 1 else []
    body, in_specs, operands = _dep_args(body, in_specs, operands, dep)
    return pl.pallas_call(
        body,
        out_shape=jax.ShapeDtypeStruct((m, n), out_dtype),
        grid=(m // bm, n // bn, nk),
        in_specs=in_specs,
        out_specs=o_spec,
        scratch_shapes=scratch,
        compiler_params=_cparams(("parallel", "parallel", "arbitrary"), est(bm, bn, bk)),
        name=name,
    )(*operands)


def _rms_fwd(x, g, name, dep=None):
    t, d = x.shape
    tb = _pick(t, 512, SUBLANE)

    def body(x_ref, g_ref, h_ref):
        xv = x_ref[...]
        r = lax.rsqrt(jnp.mean(xv * xv, axis=-1, keepdims=True) + EPS)
        h_ref[...] = ((xv * r) * g_ref[...]).astype(BF16)

    blk = pl.BlockSpec((tb, d), lambda i: (i, 0))
    body, in_specs, operands = _dep_args(
        body, [blk, pl.BlockSpec((1, d), lambda i: (0, 0))], [x, g.reshape(1, d)], dep)
    return pl.pallas_call(
        body,
        out_shape=jax.ShapeDtypeStruct((t, d), BF16),
        grid=(t // tb,),
        in_specs=in_specs,
        out_specs=blk,
        compiler_params=_cparams(("parallel",), 2 * tb * d * 6),
        name=name,
    )(*operands)


def _rms_bwd(x, g, dh, dres, name, want_bf16, dep=None):
    t, d = x.shape
    tb = _pick(t, 256, SUBLANE)

    def body(x_ref, g_ref, dh_ref, dres_ref, *outs):
        dx_ref, dg_ref = outs[0], outs[-1]
        xv = x_ref[...]
        r = lax.rsqrt(jnp.mean(xv * xv, axis=-1, keepdims=True) + EPS)
        xhat = xv * r
        dhv = dh_ref[...]
        dxhat = dhv * g_ref[...]
        dx = dres_ref[...] + r * (dxhat - xhat * jnp.mean(dxhat * xhat, axis=-1, keepdims=True))
        dx_ref[...] = dx
        if want_bf16:
            outs[1][...] = dx.astype(BF16)

        @pl.when(pl.program_id(0) == 0)
        def _():
            dg_ref[...] = jnp.zeros_like(dg_ref)

        dg_ref[...] += jnp.sum(dhv * xhat, axis=0, keepdims=True)

    blk = pl.BlockSpec((tb, d), lambda i: (i, 0))
    row = pl.BlockSpec((1, d), lambda i: (0, 0))
    out_shape = [jax.ShapeDtypeStruct((t, d), F32)]
    out_specs = [blk]
    if want_bf16:
        out_shape.append(jax.ShapeDtypeStruct((t, d), BF16))
        out_specs.append(blk)
    out_shape.append(jax.ShapeDtypeStruct((1, d), F32))
    out_specs.append(row)
    body, in_specs, operands = _dep_args(
        body, [blk, row, blk, blk], [x, g.reshape(1, d), dh, dres], dep)
    return pl.pallas_call(
        body,
        out_shape=out_shape,
        grid=(t // tb,),
        in_specs=in_specs,
        out_specs=out_specs,
        compiler_params=_cparams(("arbitrary",), 2 * tb * d * 18),
        name=name,
    )(*operands)


def _loss_head(x3, g, target):
    t, d = x3.shape
    tb = _pick(t, 256, SUBLANE)

    def body(x_ref, g_ref, t_ref, loss_ref, dx_ref, dxb_ref, dg_ref):
        xv = x_ref[...]
        gv = g_ref[...]
        r = lax.rsqrt(jnp.mean(xv * xv, axis=-1, keepdims=True) + EPS)
        xhat = xv * r
        err = xhat * gv - t_ref[...]
        dy = err * (1.0 / d)
        dxhat = dy * gv
        dx = r * (dxhat - xhat * jnp.mean(dxhat * xhat, axis=-1, keepdims=True))
        dx_ref[...] = dx
        dxb_ref[...] = dx.astype(BF16)

        @pl.when(pl.program_id(0) == 0)
        def _():
            dg_ref[...] = jnp.zeros_like(dg_ref)
            loss_ref[...] = jnp.zeros_like(loss_ref)

        dg_ref[...] += jnp.sum(dy * xhat, axis=0, keepdims=True)
        per_token = jnp.mean(err * err, axis=-1, keepdims=True)
        loss_ref[...] += 0.5 * jnp.sum(per_token, axis=0, keepdims=True)

    blk = pl.BlockSpec((tb, d), lambda i: (i, 0))
    row = pl.BlockSpec((1, d), lambda i: (0, 0))
    return pl.pallas_call(
        body,
        out_shape=[jax.ShapeDtypeStruct((1, 1), F32), jax.ShapeDtypeStruct((t, d), F32),
                   jax.ShapeDtypeStruct((t, d), BF16), jax.ShapeDtypeStruct((1, d), F32)],
        grid=(t // tb,),
        in_specs=[blk, row, blk],
        out_specs=[pl.BlockSpec((1, 1), lambda i: (0, 0)), blk, blk, row],
        compiler_params=_cparams(("arbitrary",), 2 * tb * d * 14),
        name="loss_head",
    )(x3, g.reshape(1, d), target)


def _lru_gates(xc, wa, ba, wx, bx, lam):
    nn = (((1,), (0,)), ((), ()))
    xcb = xc.astype(BF16)
    r = _sigmoid(lax.dot_general(xcb, wa, nn, preferred_element_type=F32) + ba)
    i = _sigmoid(lax.dot_general(xcb, wx, nn, preferred_element_type=F32) + bx)
    cl = -LRU_C * _softplus(-lam)
    log_a = cl * r
    a = jnp.exp(log_a)
    one_minus_a2 = _neg_expm1(2.0 * log_a)
    return xcb, r, i, a, one_minus_a2, cl


def _lru_fwd(p, conv_w, conv_b, wa_bd, ba, wx_bd, bx, lam, d_lru, gc, tc, dep=None):
    t = p.shape[0]
    ng = d_lru // gc
    nt = t // tc
    width = conv_w.shape[0]

    def body(lx_ref, gate_ref, cw_ref, cb_ref, wa_ref, ba_ref, wx_ref, bx_ref, lam_ref,
             y_ref, h_ref, halo, hcar, a_s, u_s):
        @pl.when(pl.program_id(1) == 0)
        def _():
            halo[...] = jnp.zeros_like(halo)
            hcar[...] = jnp.zeros_like(hcar)

        x = lx_ref[...]
        cat = jnp.concatenate([halo[...], x], axis=0)
        halo[...] = x[tc - SUBLANE:, :]
        xc = _conv_fwd(cat, cw_ref[...], width) + cb_ref[...]
        _, r, i, a, om, _ = _lru_gates(xc, wa_ref[...], ba_ref[...], wx_ref[...], bx_ref[...], lam_ref[...])
        a_s[...] = a
        u_s[...] = jnp.sqrt(om) * (i * xc)
        hcar[0:1, :] = _scan_tiles(a_s, u_s, h_ref, hcar[0:1, :], tc, reverse=False)
        gl, _ = _gelu_and_grad(gate_ref[...])
        y_ref[...] = (gl * h_ref[...]).astype(BF16)

    blk = lambda off: pl.BlockSpec((tc, gc), lambda g, s, off=off: (s, off + g))
    rowv = lambda rows: pl.BlockSpec((rows, gc), lambda g, s: (0, g))
    wspec = pl.BlockSpec((None, gc, gc), lambda g, s: (g, 0, 0))
    out_blk = pl.BlockSpec((tc, gc), lambda g, s: (s, g))
    body, in_specs, operands = _dep_args(
        body, [blk(0), blk(ng), rowv(width), rowv(1), wspec, rowv(1), wspec, rowv(1), rowv(1)],
        [p, p, conv_w, conv_b.reshape(1, -1), wa_bd, ba.reshape(1, -1), wx_bd, bx.reshape(1, -1),
         lam.reshape(1, -1)], dep)
    return pl.pallas_call(
        body,
        out_shape=[jax.ShapeDtypeStruct((t, d_lru), BF16), jax.ShapeDtypeStruct((t, d_lru), F32)],
        grid=(ng, nt),
        in_specs=in_specs,
        out_specs=[out_blk, out_blk],
        scratch_shapes=[pltpu.VMEM((SUBLANE, gc), F32), pltpu.VMEM((SUBLANE, gc), F32),
                        pltpu.VMEM((tc, gc), F32), pltpu.VMEM((tc, gc), F32)],
        compiler_params=_cparams(("parallel", "arbitrary"), 40 * tc * gc * 4),
        name="lru_fwd",
    )(*operands)


def _lru_bwd(p, hseq, dyp, conv_w, conv_b, wa_bd, ba, wx_bd, bx, lam, d_lru, gc, tc, dp, slab0, dep=None):
    t = p.shape[0]
    ng = d_lru // gc
    nt = t // tc
    width = conv_w.shape[0]
    halo_blocks = tc // SUBLANE
    nn = (((1,), (0,)), ((), ()))
    nt_dims = (((1,), (1,)), ((), ()))
    tn_dims = (((0,), (0,)), ((), ()))

    def body(lx_ref, lxh_ref, gate_ref, h_ref, hh_ref, dyp_ref,
             cw_ref, cb_ref, wa_ref, ba_ref, wx_ref, bx_ref, lam_ref,
             dp_ref, dcw_ref, dcb_ref, dwa_ref, dba_ref, dwx_ref, dbx_ref, dlam_ref,
             nxt_dxc, nxt_a, nxt_g, al_s, b_s, g_s):
        s = pl.program_id(1)
        first_chunk = s == nt - 1

        @pl.when(s == 0)
        def _():
            nxt_dxc[...] = jnp.zeros_like(nxt_dxc)
            nxt_a[...] = jnp.zeros_like(nxt_a)
            nxt_g[...] = jnp.zeros_like(nxt_g)
            for ref in (dcw_ref, dcb_ref, dwa_ref, dba_ref, dwx_ref, dbx_ref, dlam_ref):
                ref[...] = jnp.zeros_like(ref)

        keep = jnp.where(first_chunk, 0.0, 1.0)
        x = lx_ref[...]
        catx = jnp.concatenate([lxh_ref[...] * keep, x], axis=0)
        cw = cw_ref[...]
        xc = _conv_fwd(catx, cw, width) + cb_ref[...]
        wa = wa_ref[...]
        wx = wx_ref[...]
        lam_v = lam_ref[...]
        xcb, r, i, a, om, cl = _lru_gates(xc, wa, ba_ref[...], wx, bx_ref[...], lam_v)
        mult = jnp.sqrt(om)

        h = h_ref[...]
        hprev = _rows_before(jnp.concatenate([hh_ref[...] * keep, h], axis=0), 1)
        gl, dgl = _gelu_and_grad(gate_ref[...])
        dyp_v = dyp_ref[...]
        dp_ref[1] = (dyp_v * h * dgl).astype(BF16)

        al_s[...] = _rows_after(jnp.concatenate([a, nxt_a[...]], axis=0), 1)
        b_s[...] = dyp_v * gl
        nxt_g[0:1, :] = _scan_tiles(al_s, b_s, g_s, nxt_g[0:1, :], tc, reverse=True)
        nxt_a[...] = a[0:SUBLANE, :]
        du = g_s[...]

        da = du * hprev
        dmult = du * (i * xc)
        di = du * mult * xc
        dxc = du * mult * i
        dlog_a = da * a - dmult * (a * a / mult)
        dlam_ref[...] += jnp.sum(dlog_a * r, axis=0, keepdims=True) * (LRU_C * _sigmoid(-lam_v))
        dza = (dlog_a * cl) * r * (1.0 - r)
        dzx = di * i * (1.0 - i)
        dba_ref[...] += jnp.sum(dza, axis=0, keepdims=True)
        dbx_ref[...] += jnp.sum(dzx, axis=0, keepdims=True)
        dzab = dza.astype(BF16)
        dzxb = dzx.astype(BF16)
        dwa_ref[...] += lax.dot_general(xcb, dzab, tn_dims, preferred_element_type=F32)
        dwx_ref[...] += lax.dot_general(xcb, dzxb, tn_dims, preferred_element_type=F32)
        dxc = dxc + lax.dot_general(dzab, wa, nt_dims, preferred_element_type=F32)
        dxc = dxc + lax.dot_general(dzxb, wx, nt_dims, preferred_element_type=F32)
        dcb_ref[...] += jnp.sum(dxc, axis=0, keepdims=True)
        _conv_bwd_weight(dcw_ref, dxc, catx, width)
        catd = jnp.concatenate([dxc, nxt_dxc[...]], axis=0)
        dp_ref[0] = _conv_bwd_input(catd, cw, width).astype(BF16)
        nxt_dxc[...] = dxc[0:SUBLANE, :]

    rev = lambda s: nt - 1 - s
    blk = lambda off: pl.BlockSpec((tc, gc), lambda g, s, off=off: (rev(s), off + g))
    halo = lambda off: pl.BlockSpec(
        (SUBLANE, gc), lambda g, s, off=off: (jnp.maximum(rev(s) * halo_blocks - 1, 0), off + g))
    rowv = lambda rows: pl.BlockSpec((rows, gc), lambda g, s: (0, g))
    wspec = pl.BlockSpec((None, gc, gc), lambda g, s: (g, 0, 0))
    out_blk = pl.BlockSpec((tc, gc), lambda g, s: (rev(s), g))
    vec = lambda rows: jax.ShapeDtypeStruct((rows, d_lru), F32)
    wshape = jax.ShapeDtypeStruct((ng, gc, gc), F32)
    body, in_specs, operands = _dep_args(
        body,
        [blk(0), halo(0), blk(ng), blk(0), halo(0), blk(0),
         rowv(width), rowv(1), wspec, rowv(1), wspec, rowv(1), rowv(1)],
        [p, p, p, hseq, hseq, dyp,
         conv_w, conv_b.reshape(1, -1), wa_bd, ba.reshape(1, -1), wx_bd,
         bx.reshape(1, -1), lam.reshape(1, -1)], dp, dep)
    assert dp.shape[2] == d_lru and slab0 % 2 == 0
    return pl.pallas_call(
        body,
        out_shape=[jax.ShapeDtypeStruct(dp.shape, dp.dtype),
                   vec(width), vec(1), wshape, vec(1), wshape, vec(1), vec(1)],
        grid=(ng, nt),
        in_specs=in_specs,
        out_specs=[pl.BlockSpec((2, tc, gc), lambda g, s: (slab0 // 2, rev(s), g)),
                   rowv(width), rowv(1), wspec, rowv(1), wspec, rowv(1), rowv(1)],
        input_output_aliases={13: 0},
        scratch_shapes=[pltpu.VMEM((SUBLANE, gc), F32), pltpu.VMEM((SUBLANE, gc), F32),
                        pltpu.VMEM((SUBLANE, gc), F32),
                        pltpu.VMEM((tc, gc), F32), pltpu.VMEM((tc, gc), F32), pltpu.VMEM((tc, gc), F32)],
        compiler_params=_cparams(("parallel", "arbitrary"), 80 * tc * gc * 4),
        name="lru_bwd",
    )(*operands)


def _sc_fwd(p, conv_w, col0, d_sc, cb, tc):
    t = p.shape[0]
    nc = d_sc // cb
    nt = t // tc
    width = conv_w.shape[0]
    base = col0 // cb

    def body(b_ref, c_ref, v_ref, w_ref, y_ref, halo):
        @pl.when(pl.program_id(1) == 0)
        def _():
            halo[...] = jnp.zeros_like(halo)

        cv = c_ref[...] * v_ref[...]
        cat = jnp.concatenate([halo[...], cv], axis=0)
        halo[...] = cv[tc - SUBLANE:, :]
        y_ref[...] = (b_ref[...] * _conv_fwd(cat, w_ref[...], width)).astype(BF16)

    blk = lambda slab: pl.BlockSpec((tc, cb), lambda j, s, slab=slab: (s, base + slab * nc + j))
    return pl.pallas_call(
        body,
        out_shape=jax.ShapeDtypeStruct((t, d_sc), BF16),
        grid=(nc, nt),
        in_specs=[blk(0), blk(1), blk(2), pl.BlockSpec((width, cb), lambda j, s: (0, j))],
        out_specs=pl.BlockSpec((tc, cb), lambda j, s: (s, j)),
        scratch_shapes=[pltpu.VMEM((SUBLANE, cb), F32)],
        compiler_params=_cparams(("parallel", "arbitrary"), 20 * tc * cb * 4),
        name="sc_fwd",
    )(p, p, p, conv_w)


def _sc_bwd(p, dyp, conv_w, col0, d_sc, cb, tc, dp, slab0):
    t = p.shape[0]
    nc = d_sc // cb
    nt = t // tc
    width = conv_w.shape[0]
    base = col0 // cb
    halo_blocks = tc // SUBLANE

    def body(b_ref, c_ref, ch_ref, v_ref, vh_ref, dyp_ref, w_ref,
             dp_ref, dw_ref, nxt_dq):
        s = pl.program_id(1)

        @pl.when(s == 0)
        def _():
            nxt_dq[...] = jnp.zeros_like(nxt_dq)
            dw_ref[...] = jnp.zeros_like(dw_ref)

        keep = jnp.where(s == nt - 1, 0.0, 1.0)
        cvals = c_ref[...]
        vvals = v_ref[...]
        w = w_ref[...]
        catcv = jnp.concatenate([ch_ref[...] * vh_ref[...] * keep, cvals * vvals], axis=0)
        q = _conv_fwd(catcv, w, width)
        dyp_v = dyp_ref[...]
        dp_ref[0] = (dyp_v * q).astype(BF16)
        dq = dyp_v * b_ref[...]
        _conv_bwd_weight(dw_ref, dq, catcv, width)
        dcv = _conv_bwd_input(jnp.concatenate([dq, nxt_dq[...]], axis=0), w, width)
        nxt_dq[...] = dq[0:SUBLANE, :]
        dp_ref[1] = (dcv * vvals).astype(BF16)
        dp_ref[2] = (dcv * cvals).astype(BF16)

    rev = lambda s: nt - 1 - s
    blk = lambda slab: pl.BlockSpec((tc, cb), lambda j, s, slab=slab: (rev(s), base + slab * nc + j))
    halo = lambda slab: pl.BlockSpec(
        (SUBLANE, cb),
        lambda j, s, slab=slab: (jnp.maximum(rev(s) * halo_blocks - 1, 0), base + slab * nc + j))
    out_blk = pl.BlockSpec((tc, cb), lambda j, s: (rev(s), j))
    wblk = pl.BlockSpec((width, cb), lambda j, s: (0, j))
    assert dp.shape[2] == d_sc and slab0 % 3 == 0
    operands = [p, p, p, p, p, dyp, conv_w]
    body, in_specs, operands = _dep_args(
        body, [blk(0), blk(1), halo(1), blk(2), halo(2), out_blk, wblk], operands, dp)
    return pl.pallas_call(
        body,
        out_shape=[jax.ShapeDtypeStruct(dp.shape, dp.dtype), jax.ShapeDtypeStruct((width, d_sc), F32)],
        grid=(nc, nt),
        in_specs=in_specs,
        out_specs=[pl.BlockSpec((3, tc, cb), lambda j, s: (slab0 // 3, rev(s), j)), wblk],
        input_output_aliases={7: 0},
        scratch_shapes=[pltpu.VMEM((SUBLANE, cb), F32)],
        compiler_params=_cparams(("parallel", "arbitrary"), 30 * tc * cb * 4),
        name="sc_bwd",
    )(*operands)


def _merge_fwd(p, y_lru, y_sc, col0, tc):
    t, d = y_lru.shape
    cb = _pick(math.gcd(d, col0), 1024)
    nc = d // cb
    base = col0 // cb

    def body(gl_ref, gs_ref, yl_ref, ys_ref, o_ref):
        o_ref[...] = (_sigmoid(gl_ref[...]) * yl_ref[...] + _sigmoid(gs_ref[...]) * ys_ref[...]).astype(BF16)

    gate = lambda slab: pl.BlockSpec((tc, cb), lambda s, j, slab=slab: (s, base + slab * nc + j))
    blk = pl.BlockSpec((tc, cb), lambda s, j: (s, j))
    return pl.pallas_call(
        body,
        out_shape=jax.ShapeDtypeStruct((t, d), BF16),
        grid=(t // tc, nc),
        in_specs=[gate(0), gate(1), blk, blk],
        out_specs=blk,
        compiler_params=_cparams(("parallel", "parallel"), 2 * tc * cb * 20),
        name="merge_fwd",
    )(p, p, y_lru, y_sc)


def _merge_bwd(p, y_lru, y_sc, dmerged, col0, tc, n_slabs):
    t, d = y_lru.shape
    cb = _pick(math.gcd(d, col0), 1024)
    nc = d // cb
    base = col0 // cb

    def body(gl_ref, gs_ref, yl_ref, ys_ref, dm_ref, dp_ref, dyl_ref, dys_ref):
        dm = dm_ref[...]
        sl = _sigmoid(gl_ref[...])
        ss = _sigmoid(gs_ref[...])
        dp_ref[0] = (dm * yl_ref[...] * (sl * (1.0 - sl))).astype(BF16)
        dp_ref[1] = (dm * ys_ref[...] * (ss * (1.0 - ss))).astype(BF16)
        dyl_ref[...] = (dm * sl).astype(BF16)
        dys_ref[...] = (dm * ss).astype(BF16)

    gate = lambda slab: pl.BlockSpec((tc, cb), lambda s, j, slab=slab: (s, base + slab * nc + j))
    blk = pl.BlockSpec((tc, cb), lambda s, j: (s, j))
    act = jax.ShapeDtypeStruct((t, d), BF16)
    return pl.pallas_call(
        body,
        out_shape=[jax.ShapeDtypeStruct((n_slabs, t, cb), BF16), act, act],
        grid=(t // tc, nc),
        in_specs=[gate(0), gate(1), blk, blk, blk],
        out_specs=[pl.BlockSpec((2, tc, cb), lambda s, j: (j, s, 0)), blk, blk],
        compiler_params=_cparams(("parallel", "parallel"), 2 * tc * cb * 28),
        name="merge_bwd",
    )(p, p, y_lru, y_sc, dmerged)


def _ffn_act_fwd(up, conv_w, d_ff, cb, tc, dep=None):
    t = up.shape[0]
    nc = d_ff // cb
    nt = t // tc
    width = conv_w.shape[0]

    def body(g_ref, v_ref, wg_ref, wv_ref, o_ref, halo_g, halo_v):
        @pl.when(pl.program_id(1) == 0)
        def _():
            halo_g[...] = jnp.zeros_like(halo_g)
            halo_v[...] = jnp.zeros_like(halo_v)

        g = g_ref[...]
        v = v_ref[...]
        ug = _conv_fwd(jnp.concatenate([halo_g[...], g], axis=0), wg_ref[...], width)
        uv = _conv_fwd(jnp.concatenate([halo_v[...], v], axis=0), wv_ref[...], width)
        halo_g[...] = g[tc - SUBLANE:, :]
        halo_v[...] = v[tc - SUBLANE:, :]
        o_ref[...] = (ug * _sigmoid(ug) * uv).astype(BF16)

    blk = lambda half: pl.BlockSpec((tc, cb), lambda j, s, half=half: (s, half * nc + j))
    wblk = lambda half: pl.BlockSpec((width, cb), lambda j, s, half=half: (0, half * nc + j))
    body, in_specs, operands = _dep_args(
        body, [blk(0), blk(1), wblk(0), wblk(1)], [up, up, conv_w, conv_w], dep)
    return pl.pallas_call(
        body,
        out_shape=jax.ShapeDtypeStruct((t, d_ff), BF16),
        grid=(nc, nt),
        in_specs=in_specs,
        out_specs=pl.BlockSpec((tc, cb), lambda j, s: (s, j)),
        scratch_shapes=[pltpu.VMEM((SUBLANE, cb), F32), pltpu.VMEM((SUBLANE, cb), F32)],
        compiler_params=_cparams(("parallel", "arbitrary"), 24 * tc * cb * 4),
        name="ffn_act_fwd",
    )(*operands)


def _ffn_act_bwd(up, dact, conv_w, d_ff, cb, tc, dep=None):
    t = up.shape[0]
    nc = d_ff // cb
    nt = t // tc
    width = conv_w.shape[0]
    halo_blocks = tc // SUBLANE

    def body(g_ref, gh_ref, v_ref, vh_ref, da_ref, wg_ref, wv_ref,
             dup_ref, dwg_ref, dwv_ref, nxt_g, nxt_v):
        s = pl.program_id(1)

        @pl.when(s == 0)
        def _():
            nxt_g[...] = jnp.zeros_like(nxt_g)
            nxt_v[...] = jnp.zeros_like(nxt_v)
            dwg_ref[...] = jnp.zeros_like(dwg_ref)
            dwv_ref[...] = jnp.zeros_like(dwv_ref)

        keep = jnp.where(s == nt - 1, 0.0, 1.0)
        wg = wg_ref[...]
        wv = wv_ref[...]
        catg = jnp.concatenate([gh_ref[...] * keep, g_ref[...]], axis=0)
        catv = jnp.concatenate([vh_ref[...] * keep, v_ref[...]], axis=0)
        ug = _conv_fwd(catg, wg, width)
        uv = _conv_fwd(catv, wv, width)
        sg = _sigmoid(ug)
        da = da_ref[...]
        duv = da * (ug * sg)
        dup_ref[1] = _conv_bwd_input(jnp.concatenate([duv, nxt_v[...]], axis=0), wv, width).astype(BF16)
        nxt_v[...] = duv[0:SUBLANE, :]
        _conv_bwd_weight(dwv_ref, duv, catv, width)
        dug = da * uv * (sg * (1.0 + ug * (1.0 - sg)))
        dup_ref[0] = _conv_bwd_input(jnp.concatenate([dug, nxt_g[...]], axis=0), wg, width).astype(BF16)
        nxt_g[...] = dug[0:SUBLANE, :]
        _conv_bwd_weight(dwg_ref, dug, catg, width)

    rev = lambda s: nt - 1 - s
    blk = lambda half: pl.BlockSpec((tc, cb), lambda j, s, half=half: (rev(s), half * nc + j))
    halo = lambda half: pl.BlockSpec(
        (SUBLANE, cb), lambda j, s, half=half: (jnp.maximum(rev(s) * halo_blocks - 1, 0), half * nc + j))
    wblk = lambda half: pl.BlockSpec((width, cb), lambda j, s, half=half: (0, half * nc + j))
    out_blk = pl.BlockSpec((tc, cb), lambda j, s: (rev(s), j))
    wout = pl.BlockSpec((width, cb), lambda j, s: (0, j))
    act = jax.ShapeDtypeStruct((t, d_ff), BF16)
    wshape = jax.ShapeDtypeStruct((width, d_ff), F32)
    body, in_specs, operands = _dep_args(
        body, [blk(0), halo(0), blk(1), halo(1), out_blk, wblk(0), wblk(1)],
        [up, up, up, up, dact, conv_w, conv_w], dep)
    return pl.pallas_call(
        body,
        out_shape=[jax.ShapeDtypeStruct((2, t, d_ff), BF16), wshape, wshape],
        grid=(nc, nt),
        in_specs=in_specs,
        out_specs=[pl.BlockSpec((2, tc, cb), lambda j, s: (0, rev(s), j)), wout, wout],
        scratch_shapes=[pltpu.VMEM((SUBLANE, cb), F32), pltpu.VMEM((SUBLANE, cb), F32)],
        compiler_params=_cparams(("parallel", "arbitrary"), 40 * tc * cb * 4),
        name="ffn_act_bwd",
    )(*operands)


def _mesh_pos():
    x, y, c = lax.axis_index("x"), lax.axis_index("y"), lax.axis_index("c")
    return x, y, c


def _other_chips(x, y):
    return [(1 - x, y), (x, 1 - y), (1 - x, 1 - y)]


def _cast_place(w, chip, col_sharded, name, dep=None):
    r, cdim = w.shape
    full = (r, cdim * N_CHIPS) if col_sharded else (r * N_CHIPS, cdim)
    rb = _pick(r, max(BF16_ROWS, STREAM_BLOCK // cdim), BF16_ROWS)
    nb = r // rb

    def body(chip_ref, w_ref, o_ref):
        o_ref[...] = w_ref[...].astype(BF16)

    if col_sharded:
        out_map = lambda i, chip_ref: (i, chip_ref[0])
    else:
        out_map = lambda i, chip_ref: (chip_ref[0] * nb + i, 0)
    grid_spec = pltpu.PrefetchScalarGridSpec(
        num_scalar_prefetch=1,
        grid=(nb,),
        in_specs=[pl.BlockSpec((rb, cdim), lambda i, chip_ref: (i, 0))] + ([ANY] if dep is not None else []),
        out_specs=pl.BlockSpec((rb, cdim), out_map),
    )
    body, _, operands = _dep_args(body, [], [chip, w], dep)
    return pl.pallas_call(
        body,
        out_shape=jax.ShapeDtypeStruct(full, BF16),
        grid_spec=grid_spec,
        compiler_params=_cparams(("parallel",), 2 * rb * cdim * 6),
        name=name,
    )(*operands)


def _remote(src, dst, send_sems, recv_sems, idx, to):
    return pltpu.make_async_remote_copy(
        src_ref=src, dst_ref=dst, send_sem=send_sems.at[idx], recv_sem=recv_sems.at[idx],
        device_id=to, device_id_type=MESH)


def _exchange(name, arrays, n_sems, plan):
    n = len(arrays)

    def body(*refs):
        bufs = refs[n:2 * n]
        send_sems, recv_sems = refs[2 * n:]
        sends, arrivals = plan(bufs, send_sems, recv_sems)
        for cp in sends:
            cp.start()
        for cp in arrivals:
            cp.wait_recv()
        for cp in sends:
            cp.wait_send()

    outs = pl.pallas_call(
        body,
        out_shape=[jax.ShapeDtypeStruct(a.shape, a.dtype) for a in arrays],
        in_specs=[ANY] * n,
        out_specs=[ANY] * n,
        input_output_aliases={k: k for k in range(n)},
        scratch_shapes=[pltpu.SemaphoreType.DMA((n_sems,)), pltpu.SemaphoreType.DMA((n_sems,))],
        name=name,
    )(*arrays)
    return list(outs)


def _exchange_start(name, arrays, n_sems, plan, after=None):
    n = len(arrays)
    n_in = n + (after is not None)

    def body(*refs):
        bufs = refs[:n]
        send_sems, recv_sems = refs[n_in], refs[n_in + 1]
        token = refs[-1]
        sends, _ = plan(bufs, send_sems, recv_sems)
        for cp in sends:
            cp.start()
        token[...] = jnp.zeros_like(token)

    out = pl.pallas_call(
        body,
        out_shape=(pltpu.SemaphoreType.DMA((n_sems,)), pltpu.SemaphoreType.DMA((n_sems,)),
                   *[pltpu.HBM(a.shape, a.dtype) for a in arrays],
                   jax.ShapeDtypeStruct((SUBLANE, LANE), F32)),
        in_specs=[HBM_SPEC] * n + [ANY] * (n_in - n),
        out_specs=(SEM_SPEC, SEM_SPEC, *[HBM_SPEC] * n, VMEM_SPEC),
        input_output_aliases={k: 2 + k for k in range(n)},
        compiler_params=pltpu.CompilerParams(has_side_effects=DATAFLOW_EFFECT),
        name=name,
    )(*[pltpu.with_memory_space_constraint(a, pltpu.HBM) for a in arrays], *([after] if after is not None else []))
    return out[0], out[1], list(out[2:2 + n]), out[-1]


def _exchange_wait(name, arrays, send_sems, recv_sems, after, plan):
    n = len(arrays)

    def body(*refs):
        bufs = refs[:n]
        sends, arrivals = plan(bufs, refs[n], refs[n + 1])
        for cp in arrivals:
            cp.wait_recv()
        for cp in sends:
            cp.wait_send()

    outs = pl.pallas_call(
        body,
        out_shape=[pltpu.HBM(a.shape, a.dtype) for a in arrays],
        in_specs=[HBM_SPEC] * n + [SEM_SPEC, SEM_SPEC, ANY],
        out_specs=[HBM_SPEC] * n,
        input_output_aliases={k: k for k in range(n)},
        compiler_params=pltpu.CompilerParams(has_side_effects=DATAFLOW_EFFECT),
        name=name,
    )(*arrays, send_sems, recv_sems, after)
    return list(outs)


def _half_block(ref, shard_shape, col_sharded, chip, half):
    r, cdim = shard_shape
    h = r // 2
    if col_sharded:
        return ref.at[pl.ds(pl.multiple_of(half * h, BF16_ROWS), h),
                      pl.ds(pl.multiple_of(chip * cdim, LANE), cdim)]
    return ref.at[pl.ds(pl.multiple_of(chip * r + half * h, BF16_ROWS), h), :]


def _gather_plan(shard_shapes, col_sharded, ks):
    def plan(bufs, send_sems, recv_sems):
        x, y, c = _mesh_pos()
        sends, arrivals = [], []
        for ref, k in zip(bufs, ks):
            mine = _half_block(ref, shard_shapes[k], col_sharded[k], 2 * x + y, c)
            for j, (px, py) in enumerate(_other_chips(x, y)):
                landed = _half_block(ref, shard_shapes[k], col_sharded[k], 2 * px + py, c)
                sends.append(_remote(mine, mine, send_sems, recv_sems, 3 * k + j, (px, py, c)))
                arrivals.append(_remote(landed, landed, send_sems, recv_sems, 3 * k + j, (px, py, c)))
        return sends, arrivals
    return plan


def _forward_plan(shard_shapes, col_sharded, ks):
    def plan(bufs, send_sems, recv_sems):
        x, y, c = _mesh_pos()
        sends, arrivals = [], []
        for i, (ref, k) in enumerate(zip(bufs, ks)):
            for j, (px, py) in enumerate(_other_chips(x, y)):
                landed = _half_block(ref, shard_shapes[k], col_sharded[k], 2 * px + py, c)
                theirs = _half_block(ref, shard_shapes[k], col_sharded[k], 2 * px + py, 1 - c)
                sends.append(_remote(landed, landed, send_sems, recv_sems, 3 * i + j, (x, y, 1 - c)))
                arrivals.append(_remote(theirs, theirs, send_sems, recv_sems, 3 * i + j, (x, y, 1 - c)))
        return sends, arrivals
    return plan


def _small_gather(small):
    def body(small_ref, out_ref, send_sems, recv_sems):
        x, y, c = _mesh_pos()
        me = 2 * x + y
        out_ref[me] = small_ref[...]
        copies = []
        for j, (px, py) in enumerate(_other_chips(x, y)):
            cp = _remote(small_ref, out_ref.at[me], send_sems, recv_sems, j, (px, py, c))
            cp.start()
            copies.append(cp)
        for j, (px, py) in enumerate(_other_chips(x, y)):
            _remote(small_ref, out_ref.at[2 * px + py], send_sems, recv_sems, j, (px, py, c)).wait_recv()
        for cp in copies:
            cp.wait_send()

    return pl.pallas_call(
        body,
        out_shape=jax.ShapeDtypeStruct((N_CHIPS,) + small.shape, small.dtype),
        in_specs=[VMEM_SPEC],
        out_specs=VMEM_SPEC,
        scratch_shapes=[pltpu.SemaphoreType.DMA((N_CHIPS - 1,)), pltpu.SemaphoreType.DMA((N_CHIPS - 1,))],
        name="gather_small",
    )(small)


def _as3d(g, col_sharded):
    r, cdim = g.shape
    return g.reshape(1, r, cdim) if col_sharded else g.reshape(N_CHIPS, r // N_CHIPS, cdim)


def _pair_plan(m):
    def plan(bufs, send_sems, recv_sems):
        x, y, c = _mesh_pos()
        copies = []
        for i in range(m):
            h = bufs[i].shape[1] // 2
            src = bufs[i].at[:, pl.ds(pl.multiple_of((1 - c) * h, BF16_ROWS), h), :]
            copies.append(_remote(src, bufs[m + i], send_sems, recv_sems, i, (x, y, 1 - c)))
        return copies, copies
    return plan


def _chip_plan(col_flags):
    m = len(col_flags)

    def plan(bufs, send_sems, recv_sems):
        x, y, c = _mesh_pos()
        copies = []
        for i in range(m):
            land = bufs[m + i]
            width = land.shape[2]
            for j, (px, py) in enumerate(_other_chips(x, y)):
                q = 2 * px + py
                if col_flags[i]:
                    src = bufs[i].at[0, :, pl.ds(pl.multiple_of(q * width, LANE), width)]
                else:
                    src = bufs[i].at[q]
                copies.append(_remote(src, land.at[j], send_sems, recv_sems, 3 * i + j, (px, py, c)))
        return copies, copies
    return plan


def _share_plan(m):
    def plan(bufs, send_sems, recv_sems):
        x, y, c = _mesh_pos()
        sends, arrivals = [], []
        for i in range(m):
            h = bufs[i].shape[0] // 2
            mine = bufs[i].at[pl.ds(pl.multiple_of(c * h, SUBLANE), h), :]
            theirs = bufs[i].at[pl.ds(pl.multiple_of((1 - c) * h, SUBLANE), h), :]
            sends.append(_remote(mine, mine, send_sems, recv_sems, i, (x, y, 1 - c)))
            arrivals.append(_remote(theirs, theirs, send_sems, recv_sems, i, (x, y, 1 - c)))
        return sends, arrivals
    return plan


def _pair_add(g3, other, core):
    a, r, cdim = g3.shape
    h = r // 2
    rb = _pick(h, max(BF16_ROWS, STREAM_BLOCK // cdim), BF16_ROWS)
    nb = h // rb

    def body(core_ref, g_ref, o_ref, out_ref):
        out_ref[...] = (g_ref[...].astype(F32) + o_ref[...].astype(F32)).astype(BF16)

    grid_spec = pltpu.PrefetchScalarGridSpec(
        num_scalar_prefetch=1,
        grid=(a, nb),
        in_specs=[pl.BlockSpec((None, rb, cdim), lambda i, j, core_ref: (i, core_ref[0] * nb + j, 0)),
                  pl.BlockSpec((None, rb, cdim), lambda i, j, core_ref: (i, j, 0))],
        out_specs=pl.BlockSpec((None, rb, cdim), lambda i, j, core_ref: (i, j, 0)),
    )
    return pl.pallas_call(
        body,
        out_shape=jax.ShapeDtypeStruct((a, h, cdim), BF16),
        grid_spec=grid_spec,
        compiler_params=_cparams(("parallel", "parallel"), 2 * rb * cdim * 10),
        name="grad_pair_add",
    )(core, g3, other)


def _small_allreduce(small):
    rows = small.shape[0]
    pad = (-rows) % (2 * SUBLANE)
    if pad:
        small = jnp.pad(small, ((0, pad), (0, 0)))
    h = small.shape[0] // 2
    half_shape = (h, small.shape[1])

    def body(small_ref, out_ref, theirs, by_chip, send_sems, recv_sems):
        x, y, c = _mesh_pos()
        me = 2 * x + y
        sibling = (x, y, 1 - c)
        mine = pl.ds(pl.multiple_of(c * h, SUBLANE), h)
        other = pl.ds(pl.multiple_of((1 - c) * h, SUBLANE), h)
        swap = _remote(small_ref, theirs, send_sems, recv_sems, 0, sibling)
        swap.start()
        swap.wait()
        by_chip[me] = small_ref[mine, :] + theirs[mine, :]
        copies = []
        for j, (px, py) in enumerate(_other_chips(x, y)):
            cp = _remote(by_chip.at[me], by_chip.at[me], send_sems, recv_sems, 1 + j, (px, py, c))
            cp.start()
            copies.append(cp)
        for j, (px, py) in enumerate(_other_chips(x, y)):
            landed = by_chip.at[2 * px + py]
            _remote(landed, landed, send_sems, recv_sems, 1 + j, (px, py, c)).wait_recv()
        total = by_chip[0]
        for q in range(1, N_CHIPS):
            total = total + by_chip[q]
        out_ref[mine, :] = total
        for cp in copies:
            cp.wait_send()
        share = _remote(out_ref.at[mine, :], out_ref.at[mine, :], send_sems, recv_sems, 4, sibling)
        share.start()
        _remote(out_ref.at[other, :], out_ref.at[other, :], send_sems, recv_sems, 4, sibling).wait_recv()
        share.wait_send()

    out = pl.pallas_call(
        body,
        out_shape=jax.ShapeDtypeStruct(small.shape, F32),
        in_specs=[VMEM_SPEC],
        out_specs=VMEM_SPEC,
        scratch_shapes=[pltpu.VMEM(small.shape, F32), pltpu.VMEM((N_CHIPS,) + half_shape, F32),
                        pltpu.SemaphoreType.DMA((5,)), pltpu.SemaphoreType.DMA((5,))],
        compiler_params=pltpu.CompilerParams(
            vmem_limit_bytes=min(VMEM_BUDGET, 8 * _nbytes(small.shape, F32) + (8 << 20))),
        name="grad_small_allreduce",
    )(small)
    return out[:rows]


def _chip_sum(partial, land, where, col_sharded):
    _, h, cdim = land.shape
    rb = _pick(h, max(BF16_ROWS, STREAM_BLOCK // cdim), BF16_ROWS)
    nb = h // rb

    def body(where_ref, own_ref, l_ref, o_ref):
        total = own_ref[...].astype(F32)
        for j in range(N_CHIPS - 1):
            total = total + l_ref[j].astype(F32)
        o_ref[...] = total

    if col_sharded:
        own_map = lambda i, w: (0, i, w[0])
    else:
        own_map = lambda i, w: (w[0], i, 0)
    grid_spec = pltpu.PrefetchScalarGridSpec(
        num_scalar_prefetch=1,
        grid=(nb,),
        in_specs=[pl.BlockSpec((None, rb, cdim), own_map),
                  pl.BlockSpec((N_CHIPS - 1, rb, cdim), lambda i, w: (0, i, 0))],
        out_specs=pl.BlockSpec((rb, cdim), lambda i, w: (w[1] * nb + i, 0)),
    )
    return pl.pallas_call(
        body,
        out_shape=jax.ShapeDtypeStruct((2 * h, cdim), F32),
        grid_spec=grid_spec,
        compiler_params=_cparams(("parallel",), 2 * rb * cdim * 12),
        name="grad_chip_sum",
    )(where, partial, land)


def _adamw(w, g, m, v, name, dep=None):
    r, cdim = w.shape
    rb = _pick(r, max(SUBLANE, (STREAM_BLOCK // 4) // cdim), SUBLANE)
    c1 = 1.0 - ADAM_B1 ** ADAM_STEP
    c2 = 1.0 - ADAM_B2 ** ADAM_STEP

    def body(w_ref, g_ref, m_ref, v_ref, go_ref, d_ref, mo_ref, vo_ref):
        gv = g_ref[...]
        mn = ADAM_B1 * m_ref[...] + (1.0 - ADAM_B1) * gv
        vn = ADAM_B2 * v_ref[...] + (1.0 - ADAM_B2) * (gv * gv)
        m_hat = mn / c1
        v_hat = vn / c2
        d_ref[...] = -ADAM_LR * (m_hat / (jnp.sqrt(v_hat) + ADAM_EPS) + ADAM_WD * w_ref[...])
        go_ref[...] = gv
        mo_ref[...] = mn
        vo_ref[...] = vn

    blk = pl.BlockSpec((rb, cdim), lambda i: (i, 0))
    shape = jax.ShapeDtypeStruct((r, cdim), F32)
    body, in_specs, operands = _dep_args(body, [blk] * 4, [w, g, m, v], dep)
    return pl.pallas_call(
        body,
        out_shape=[shape] * 4,
        grid=(r // rb,),
        in_specs=in_specs,
        out_specs=[blk] * 4,
        compiler_params=_cparams(("parallel",), 2 * rb * cdim * 4 * 8),
        name=name,
    )(*operands)


def _pack(arrays):
    tile = SUBLANE * LANE
    pieces = []
    for arr in arrays:
        flat = arr.reshape(-1)
        pad = (-flat.shape[0]) % tile
        if pad:
            flat = jnp.concatenate([flat, jnp.zeros((pad,), flat.dtype)])
        pieces.append(flat)
    return jnp.concatenate(pieces).reshape(-1, LANE)


def _unpack(packed, shapes):
    tile = SUBLANE * LANE
    flat = packed.reshape(-1)
    out, off = [], 0
    for shp in shapes:
        size = math.prod(shp)
        out.append(flat[off:off + size].reshape(shp))
        off += size + ((-size) % tile)
    return out


def _block_diag_groups(w, per_group):
    hcount, hd, _ = w.shape
    ng = hcount // per_group
    w4 = w.reshape(ng, per_group, hd, hd)
    eye = jnp.eye(per_group, dtype=w.dtype)
    bd = w4[:, :, :, None, :] * eye[None, :, None, :, None]
    return bd.reshape(ng, per_group * hd, per_group * hd).astype(BF16)


def _diag_blocks(wbd, per_group, hd):
    ng = wbd.shape[0]
    w5 = wbd.reshape(ng, per_group, hd, per_group, hd)
    blocks = [w5[:, i, :, i, :] for i in range(per_group)]
    return jnp.stack(blocks, axis=1).reshape(ng * per_group, hd, hd)


def kernel(x, g_mix, w_in, lru_conv_w, lru_conv_b, lru_wa, lru_ba, lru_wx, lru_bx, lru_lambda, lru_w_out, sc_conv_w, sc_w_out, w_o, g_ffn, ffn_w_up, ffn_conv_w, ffn_w_down, g_final, loss_target, m_g_mix, m_w_in, m_lru_conv_w, m_lru_conv_b, m_lru_wa, m_lru_ba, m_lru_wx, m_lru_bx, m_lru_lambda, m_lru_w_out, m_sc_conv_w, m_sc_w_out, m_w_o, m_g_ffn, m_ffn_w_up, m_ffn_conv_w, m_ffn_w_down, m_g_final, v_g_mix, v_w_in, v_lru_conv_w, v_lru_conv_b, v_lru_wa, v_lru_ba, v_lru_wx, v_lru_bx, v_lru_lambda, v_lru_w_out, v_sc_conv_w, v_sc_w_out, v_w_o, v_g_ffn, v_ffn_w_up, v_ffn_conv_w, v_ffn_w_down, v_g_final):
    seq, d_model = x.shape[1], x.shape[2]
    heads, head_dim, _ = lru_wa.shape
    d_lru = heads * head_dim
    d_sc = sc_w_out.shape[0]
    d_ff = ffn_w_down.shape[0] * N_CHIPS
    assert x.shape[0] == 1 and w_in.shape[1] * N_CHIPS == 2 * d_lru + 3 * d_sc + 2 * d_model
    xs = x.reshape(seq, d_model)
    target = loss_target.reshape(seq, d_model)

    chip = 2 * lax.axis_index("x") + lax.axis_index("y")
    core = lax.axis_index("c").astype(jnp.int32).reshape(1)

    big_w = [w_in, lru_w_out, sc_w_out, w_o, ffn_w_up, ffn_w_down]
    big_m = [m_w_in, m_lru_w_out, m_sc_w_out, m_w_o, m_ffn_w_up, m_ffn_w_down]
    big_v = [v_w_in, v_lru_w_out, v_sc_w_out, v_w_o, v_ffn_w_up, v_ffn_w_down]
    col_sharded = [True, True, True, False, True, False]
    conv_shards = [lru_conv_w, sc_conv_w, ffn_conv_w]
    conv_pack = jnp.concatenate(
        [jnp.pad(w, ((0, SUBLANE - w.shape[0]), (0, 0))) for w in conv_shards], axis=1)
    big_names = ["w_in", "lru_w_out", "sc_w_out", "w_o", "ffn_w_up", "ffn_w_down"]
    chip_arr = chip.astype(jnp.int32).reshape(1)
    placed = [_cast_place(big_w[0], chip_arr, col_sharded[0], "cast_" + big_names[0])]
    conv_all = _small_gather(conv_pack)
    shard_shapes = [w.shape for w in big_w]
    n_big = len(big_w)

    def gather_start(ks, after, tag):
        send, recv, bufs, token = _exchange_start(
            "gather_start_" + tag, [placed[k] for k in ks], 3 * n_big,
            _gather_plan(shard_shapes, col_sharded, ks), after=after)
        return (send, recv, dict(zip(ks, bufs))), token

    def arrived(state, ks, after, tag):
        send, recv, bufs = state
        got = _exchange_wait("gather_wait_" + tag, [bufs[k] for k in ks], send, recv, after,
                             _gather_plan(shard_shapes, col_sharded, ks))
        return _exchange("gather_forward_" + tag, got, 3 * len(ks), _forward_plan(shard_shapes, col_sharded, ks))

    def arrived_behind(state, ks, after, tag):
        send, recv, bufs = state
        got = _exchange_wait("gather_wait_" + tag, [bufs[k] for k in ks], send, recv, after,
                             _gather_plan(shard_shapes, col_sharded, ks))
        plan = _forward_plan(shard_shapes, col_sharded, ks)
        send, recv, got, token = _exchange_start("gather_forward_start_" + tag, got, 3 * len(ks), plan)
        return (send, recv, got, plan, tag), token

    def forwarded(state, after):
        send, recv, got, plan, tag = state
        return _exchange_wait("gather_forward_wait_" + tag, got, send, recv, after, plan)

    conv_full, off = [], 0
    for w in conv_shards:
        kw, nq = w.shape
        piece = conv_all[:, :kw, off:off + nq]
        conv_full.append(piece.transpose(1, 0, 2).reshape(kw, N_CHIPS * nq))
        off += nq
    lcw, scw, fcw = conv_full

    per_group = max(1, min(heads, 256 // head_dim))
    gc = per_group * head_dim
    wa_bd = _block_diag_groups(lru_wa, per_group)
    wx_bd = _block_diag_groups(lru_wx, per_group)
    tc = _pick(seq, 256, SUBLANE)
    cb_sc = _pick(d_sc, 512)
    cb_ff = _pick(d_ff, 128)
    tc_ff = _pick(seq, 2048, SUBLANE)
    tc_lru = _pick(seq, 1024, SUBLANE)
    col_sc = 2 * d_lru
    col_gates = 2 * d_lru + 3 * d_sc

    first, token = gather_start([0], conv_all, "in")
    for k in range(1, n_big):
        placed.append(_cast_place(big_w[k], chip_arr, col_sharded[k], "cast_" + big_names[k], dep=token))
        token = placed[-1]
    h1 = _rms_fwd(xs, g_mix, "rms_mix", dep=token)
    (win_b,) = arrived(first, [0], h1, "in")
    rest, token = gather_start([1, 2, 3, 4, 5], win_b, "rest")
    p = _mm(h1, win_b, "nn", F32, name="mm_in", dep=token)
    mix, token = arrived_behind(rest, [1, 2, 3], p, "mix")
    y_lru_pre, hseq = _lru_fwd(p, lcw, lru_conv_b, wa_bd, lru_ba, wx_bd, lru_bx, lru_lambda, d_lru, gc, tc_lru,
                               dep=token)
    y_sc_pre = _sc_fwd(p, scw, col_sc, d_sc, cb_sc, tc)
    wlo_b, wso_b, wo_b = forwarded(mix, y_sc_pre)
    y_lru = _mm(y_lru_pre, wlo_b, "nn", BF16, name="mm_lru_out")
    y_sc = _mm(y_sc_pre, wso_b, "nn", BF16, name="mm_sc_out")
    merged = _merge_fwd(p, y_lru, y_sc, col_gates, tc)
    x2 = _mm(merged, wo_b, "nn", F32, res=xs, name="mm_o")
    (wup_b,) = arrived(rest, [4], x2, "up")
    h2 = _rms_fwd(x2, g_ffn, "rms_ffn")
    up = _mm(h2, wup_b, "nn", F32, name="mm_up")
    down, token = arrived_behind(rest, [5], up, "down")
    act = _ffn_act_fwd(up, fcw, d_ff, cb_ff, tc_ff, dep=token)
    (wdn_b,) = forwarded(down, act)
    x3 = _mm(act, wdn_b, "nn", F32, res=x2, name="mm_down")
    loss_part, dx3, dx3b, dg_final = _loss_head(x3, g_final, target)

    where = jnp.concatenate([chip_arr, core])

    def reduce_start(grads, flags, tag):
        views = [_as3d(g, cs) for g, cs in zip(grads, flags)]
        lands = [lax.empty((v.shape[0], v.shape[1] // 2, v.shape[2]), v.dtype) for v in views]
        send, recv, bufs, token = _exchange_start("grad_pair_start_" + tag, views + lands, len(views),
                                                  _pair_plan(len(views)))
        return (send, recv, bufs, flags, tag), token

    def reduce_mid(state, after):
        send, recv, bufs, flags, tag = state
        m = len(flags)
        bufs = _exchange_wait("grad_pair_wait_" + tag, bufs, send, recv, after, _pair_plan(m))
        partials = [_pair_add(bufs[i], bufs[m + i], core) for i in range(m)]
        lands = []
        for pz, cs in zip(partials, flags):
            _, h, cdim = pz.shape
            lands.append(lax.empty((N_CHIPS - 1, h, cdim // N_CHIPS if cs else cdim), BF16))
        send, recv, bufs, token = _exchange_start("grad_chip_start_" + tag, partials + lands, 3 * m,
                                                  _chip_plan(flags))
        return (send, recv, bufs, flags, tag), token

    def reduce_end(state, after):
        send, recv, bufs, flags, tag = state
        m = len(flags)
        bufs = _exchange_wait("grad_chip_wait_" + tag, bufs, send, recv, after, _chip_plan(flags))
        return [_chip_sum(bufs[i], bufs[m + i], where, flags[i]) for i in range(m)]

    g_wdn = _mm(act, dx3b, "tn", F32, name="mm_down_dw")
    red_down, token = reduce_start([g_wdn], [False], "down")
    dact = _mm(dx3b, wdn_b, "nt", F32, name="mm_down_dx", dep=token)
    red_down, token = reduce_mid(red_down, dact)
    dup, dfcw_g, dfcw_v = _ffn_act_bwd(up, dact, fcw, d_ff, cb_ff, tc_ff, dep=token)
    g_wup = _mm(h2, dup, "tn", BF16, name="mm_up_dw", slabs=[0, 1])
    red_up, token = reduce_start([g_wup], [True], "up")
    dh2 = _mm(dup, wup_b, "nt", F32, name="mm_up_dx", dep=token, slabs=[0, 1])
    red_up, token = reduce_mid(red_up, dh2)
    dx2, dx2b, dg_ffn = _rms_bwd(x2, g_ffn, dh2, dx3, "rms_ffn_bwd", True, dep=token)
    g_wo = _mm(merged, dx2b, "tn", BF16, name="mm_o_dw")
    dmerged = _mm(dx2b, wo_b, "nt", BF16, name="mm_o_dx")
    slab_w = d_lru
    assert d_sc == slab_w and d_model % slab_w == 0 and col_gates % slab_w == 0
    n_gate = d_model // slab_w
    gate0 = col_gates // slab_w
    dp_slabs = [gate0 + kind * n_gate + j for j in range(n_gate) for kind in (0, 1)] + [0, 1, 2, 3, 4]
    dp, dyl, dys = _merge_bwd(p, y_lru, y_sc, dmerged, col_gates, tc, len(dp_slabs))
    assert dp.shape[2] == slab_w
    g_wlo = _mm(y_lru_pre, dyl, "tn", BF16, name="mm_lru_out_dw")
    g_wso = _mm(y_sc_pre, dys, "tn", BF16, name="mm_sc_out_dw")
    red_mix, token = reduce_start([g_wlo, g_wso, g_wo], [True, True, False], "mix")
    dylp = _mm(dyl, wlo_b, "nt", F32, name="mm_lru_out_dx", dep=token)
    dysp = _mm(dys, wso_b, "nt", F32, name="mm_sc_out_dx")
    red_mix, token = reduce_mid(red_mix, dysp)
    dp, dlcw, dlcb, dwa_bd, dba, dwx_bd, dbx, dlam = _lru_bwd(
        p, hseq, dylp, lcw, lru_conv_b, wa_bd, lru_ba, wx_bd, lru_bx, lru_lambda, d_lru, gc, tc_lru,
        dp, 2 * n_gate, dep=token)
    dp, dscw = _sc_bwd(p, dysp, scw, col_sc, d_sc, cb_sc, tc, dp, 2 * n_gate + 2)
    g_win = _mm(h1, dp, "tn", BF16, name="mm_in_dw", slabs=dp_slabs)
    red_in, token = reduce_start([g_win], [True], "in")
    dh1 = _mm(dp, win_b, "nt", F32, name="mm_in_dx", dep=token, slabs=dp_slabs)
    grad_x, dg_mix = _rms_bwd(xs, g_mix, dh1, dx2, "rms_mix_bwd", False)

    small_g = [dg_mix, dlcw, dlcb, _diag_blocks(dwa_bd, per_group, head_dim), dba,
               _diag_blocks(dwx_bd, per_group, head_dim), dbx, dlam, dscw, dg_ffn,
               jnp.concatenate([dfcw_g, dfcw_v], axis=1), dg_final]
    small_shapes = [a.shape for a in small_g]
    small_sum = _small_allreduce(_pack(small_g))
    red_in, token = reduce_mid(red_in, small_sum)
    (h_wdn,) = reduce_end(red_down, token)
    (h_wup,) = reduce_end(red_up, token)
    h_wlo, h_wso, h_wo = reduce_end(red_mix, token)
    s_wlo, s_wso, s_wo, s_wup, s_wdn = _exchange("grad_share_a", [h_wlo, h_wso, h_wo, h_wup, h_wdn], 5,
                                                 _share_plan(5))
    early = {1: s_wlo, 2: s_wso, 3: s_wo, 4: s_wup, 5: s_wdn}
    big_out = [None] * n_big
    last = None
    for k, g in early.items():
        big_out[k] = _adamw(big_w[k], g, big_m[k], big_v[k], "adamw_" + big_names[k], dep=last)
        last = big_out[k][1]
    (h_win,) = reduce_end(red_in, last)
    (s_win,) = _exchange("grad_share_b", [h_win], 1, _share_plan(1))
    big_out[0] = _adamw(big_w[0], s_win, big_m[0], big_v[0], "adamw_" + big_names[0])
    sg = _unpack(small_sum, small_shapes)
    for idx in (1, 8, 10):
        nq = sg[idx].shape[1] // N_CHIPS
        sg[idx] = lax.dynamic_slice_in_dim(sg[idx], chip * nq, nq, axis=1)
    small_w = [g_mix, lru_conv_w, lru_conv_b, lru_wa, lru_ba, lru_wx, lru_bx, lru_lambda, sc_conv_w,
               g_ffn, ffn_conv_w, g_final]
    small_m = [m_g_mix, m_lru_conv_w, m_lru_conv_b, m_lru_wa, m_lru_ba, m_lru_wx, m_lru_bx, m_lru_lambda,
               m_sc_conv_w, m_g_ffn, m_ffn_conv_w, m_g_final]
    small_v = [v_g_mix, v_lru_conv_w, v_lru_conv_b, v_lru_wa, v_lru_ba, v_lru_wx, v_lru_bx, v_lru_lambda,
               v_sc_conv_w, v_g_ffn, v_ffn_conv_w, v_g_final]
    sg = [g.reshape(w.shape) for g, w in zip(sg, small_w)]
    w_shapes = [w.shape for w in small_w]
    packed = _adamw(_pack(small_w), _pack(sg), _pack(small_m), _pack(small_v), "adamw_small")
    small_out = [_unpack(pk, w_shapes) for pk in packed]

    order = [(0, 0), (1, 0), (0, 1), (0, 2), (0, 3), (0, 4), (0, 5), (0, 6), (0, 7), (1, 1), (0, 8), (1, 2),
             (1, 3), (0, 9), (1, 4), (0, 10), (1, 5), (0, 11)]
    by_kind = []
    for kind in range(4):
        by_kind.append([big_out[i][kind] if is_big else small_out[kind][i] for is_big, i in order])
    loss = lax.psum(loss_part[0, 0], ("x", "y", "c"))
    return (loss, grad_x.reshape(x.shape), *by_kind[0], *by_kind[1], *by_kind[2], *by_kind[3])
```

```python
import math

import jax
import jax.numpy as jnp
from jax import lax
from jax.experimental import pallas as pl
from jax.experimental.pallas import tpu as pltpu

F32 = jnp.float32
BF16 = jnp.bfloat16

LANE = 128
SUBLANE = 8
BF16_ROWS = 16
VMEM_BYTES_V7X = 64 * 1024 * 1024
VMEM_BUDGET = VMEM_BYTES_V7X - 8 * 1024 * 1024
MM_VMEM_BUDGET = 42 * 1024 * 1024
STREAM_BLOCK = 2 * 1024 * 1024
EPS = 1e-6
LRU_C = 8.0
ADAM_LR = 0.001
ADAM_B1 = 0.9
ADAM_B2 = 0.999
ADAM_EPS = 1e-08
ADAM_WD = 0.01
ADAM_STEP = 10

N_CHIPS = 4
N_DEV = 8
MESH = pl.DeviceIdType.MESH
ANY = pl.BlockSpec(memory_space=pl.ANY)
VMEM_SPEC = pl.BlockSpec(memory_space=pltpu.VMEM)
HBM_SPEC = pl.BlockSpec(memory_space=pltpu.HBM)
SEM_SPEC = pl.BlockSpec(memory_space=pltpu.SEMAPHORE)
DATAFLOW_EFFECT = pltpu.SideEffectType.DATAFLOW_SIDE_EFFECTING


def _pick(n, cap, mult=LANE):
    best = None
    d = mult
    while d <= min(n, cap):
        if n % d == 0:
            best = d
        d += mult
    return n if best is None else best


def _cparams(semantics, block_bytes):
    limit = min(VMEM_BUDGET, max(32 * 1024 * 1024, int(block_bytes * 1.25) + (4 << 20)))
    return pltpu.CompilerParams(dimension_semantics=semantics, vmem_limit_bytes=limit)


def _nbytes(shape, dtype):
    return math.prod(shape) * jnp.dtype(dtype).itemsize


def _sigmoid(z):
    return 0.5 * jnp.tanh(0.5 * z) + 0.5


def _softplus(z):
    e = jnp.exp(-jnp.abs(z))
    u = 1.0 + e
    log1p = jnp.where(u == 1.0, e, jnp.log(u) * (e / (u - 1.0)))
    return jnp.maximum(z, 0.0) + log1p


def _neg_expm1(z):
    small = z * (1.0 + z * (0.5 + z * (1.0 / 6.0 + z * (1.0 / 24.0))))
    return -jnp.where(jnp.abs(z) < 0.03, small, jnp.exp(z) - 1.0)


_GELU_K = math.sqrt(2.0 / math.pi)
_GELU_C = 0.044715


def _gelu_and_grad(z):
    z2 = z * z
    th = jnp.tanh(_GELU_K * (z + _GELU_C * z2 * z))
    val = 0.5 * z * (1.0 + th)
    grad = 0.5 * (1.0 + th) + 0.5 * z * (1.0 - th * th) * (_GELU_K * (1.0 + 3.0 * _GELU_C * z2))
    return val, grad


def _rows_before(cat, k):
    if k == 0:
        return cat[SUBLANE:, :]
    return pltpu.roll(cat, k, 0)[SUBLANE:, :]


def _rows_after(cat, k):
    n = cat.shape[0]
    if k == 0:
        return cat[:n - SUBLANE, :]
    return pltpu.roll(cat, n - k, 0)[:n - SUBLANE, :]


def _conv_fwd(cat, w, width):
    y = _rows_before(cat, width - 1) * w[0:1, :]
    for k in range(1, width):
        y = y + _rows_before(cat, width - 1 - k) * w[k:k + 1, :]
    return y


def _conv_bwd_input(cat, w, width):
    dx = _rows_after(cat, width - 1) * w[0:1, :]
    for k in range(1, width):
        dx = dx + _rows_after(cat, width - 1 - k) * w[k:k + 1, :]
    return dx


def _conv_bwd_weight(dw_ref, dy, catx, width):
    for k in range(width):
        dw_ref[k:k + 1, :] += jnp.sum(dy * _rows_before(catx, width - 1 - k), axis=0, keepdims=True)


def _scan_tiles(a_ref, b_ref, out_ref, carry0, n_rows, reverse):
    cols = a_ref.shape[1]
    row = lax.broadcasted_iota(jnp.int32, (SUBLANE, cols), 0)
    n_tiles = n_rows // SUBLANE

    def step(j, carry):
        tile = (n_tiles - 1 - j) if reverse else j
        off = pl.multiple_of(tile * SUBLANE, SUBLANE)
        a = a_ref[pl.ds(off, SUBLANE), :]
        b = b_ref[pl.ds(off, SUBLANE), :]
        for s in (1, 2, 4):
            if reverse:
                keep = row < SUBLANE - s
                shift = SUBLANE - s
            else:
                keep = row >= s
                shift = s
            a_sh = jnp.where(keep, pltpu.roll(a, shift, 0), 1.0)
            b_sh = jnp.where(keep, pltpu.roll(b, shift, 0), 0.0)
            b = a * b_sh + b
            a = a * a_sh
        out = a * carry + b
        out_ref[pl.ds(off, SUBLANE), :] = out
        return out[0:1, :] if reverse else out[SUBLANE - 1:SUBLANE, :]

    return lax.fori_loop(0, n_tiles, step, carry0)


def _dep_args(body, in_specs, operands, *deps):
    deps = [d for d in deps if d is not None]
    if not deps:
        return body, in_specs, operands
    n = len(operands)

    def wrapped(*refs):
        return body(*refs[:n], *refs[n + len(deps):])

    return wrapped, list(in_specs) + [ANY] * len(deps), list(operands) + deps


def _mm(a, b, mode, out_dtype, res=None, name=None, dep=None, slabs=None):
    assert a.dtype == BF16 and b.dtype == BF16
    a_slabbed, b_slabbed = a.ndim == 3, b.ndim == 3
    assert not a_slabbed or (mode == "nt" and slabs is not None)
    assert not b_slabbed or (mode == "tn" and slabs is not None)
    if mode == "nn":
        (m, k), (k2, n) = a.shape, b.shape
        dims = (((1,), (0,)), ((), ()))
    elif mode == "nt":
        m, k = (a.shape[1], a.shape[0] * a.shape[2]) if a_slabbed else a.shape
        n, k2 = b.shape
        dims = (((1,), (1,)), ((), ()))
    else:
        k, m = a.shape
        k2, n = (b.shape[1], b.shape[0] * b.shape[2]) if b_slabbed else b.shape
        dims = (((0,), (0,)), ((), ()))
    assert k == k2
    n_unit = b.shape[2] if b_slabbed else n
    out_bytes = jnp.dtype(out_dtype).itemsize
    bm = _pick(m, 1024)
    bn = _pick(n_unit, 1024)
    bk = k

    def est(bm_, bn_, bk_):
        e = 2 * (bm_ * bk_ + bk_ * bn_) * 2 + 2 * bm_ * bn_ * out_bytes
        if k // bk_ > 1:
            e += bm_ * bn_ * 4
        if res is not None:
            e += 2 * bm_ * bn_ * 4
        return e

    for shrink_n, floor in ((True, 512), (False, 512), (True, 256), (False, 256)):
        while est(bm, bn, bk) > MM_VMEM_BUDGET:
            if shrink_n and bn > floor and bn % 2 == 0 and n_unit % (bn // 2) == 0:
                bn //= 2
            elif not shrink_n and bm > floor and bm % 2 == 0 and m % (bm // 2) == 0:
                bm //= 2
            else:
                break
    while (est(bm, bn, bk) > MM_VMEM_BUDGET and not a_slabbed and bk % (2 * LANE) == 0
           and k % (bk // 2) == 0):
        bk //= 2
    nk = k // bk
    per_slab = n_unit // bn

    def out_col(j):
        if not b_slabbed:
            return j
        s = j // per_slab
        where = sum(jnp.where(s == t, slabs[t], 0) for t in range(len(slabs)))
        return where * per_slab + j % per_slab

    if mode == "tn":
        a_spec = pl.BlockSpec((bk, bm), lambda i, j, kk: (kk, i))
    elif a_slabbed:
        a_spec = pl.BlockSpec((a.shape[0], bm, a.shape[2]), lambda i, j, kk: (0, i, 0))
    else:
        a_spec = pl.BlockSpec((bm, bk), lambda i, j, kk: (i, kk))
    if mode == "nt":
        b_spec = pl.BlockSpec((bn, bk), lambda i, j, kk: (j, kk))
    elif b_slabbed:
        b_spec = pl.BlockSpec((None, bk, bn), lambda i, j, kk: (j // per_slab, kk, j % per_slab))
    else:
        b_spec = pl.BlockSpec((bk, bn), lambda i, j, kk: (kk, j))
    o_spec = pl.BlockSpec((bm, bn), lambda i, j, kk: (i, out_col(j)))
    in_specs = [a_spec, b_spec]
    operands = [a, b]
    if res is not None:
        in_specs.append(o_spec)
        operands.append(res)
    has_res = res is not None

    def body(*refs):
        a_ref, b_ref = refs[0], refs[1]
        res_ref = refs[2] if has_res else None
        o_ref = refs[2 + has_res]
        if a_slabbed:
            width = a_ref.shape[2]
            part = None
            for s, col in enumerate(slabs):
                term = lax.dot_general(a_ref[s], b_ref[:, col * width:(col + 1) * width], dims,
                                       preferred_element_type=F32)
                part = term if part is None else part + term
        else:
            part = lax.dot_general(a_ref[...], b_ref[...], dims, preferred_element_type=F32)
        if nk == 1:
            if has_res:
                part = part + res_ref[...]
            o_ref[...] = part.astype(o_ref.dtype)
            return
        acc_ref = refs[-1]
        kk = pl.program_id(2)

        @pl.when(kk == 0)
        def _():
            acc_ref[...] = part

        @pl.when(kk > 0)
        def _():
            acc_ref[...] += part

        @pl.when(kk == nk - 1)
        def _():
            total = acc_ref[...]
            if has_res:
                total = total + res_ref[...]
            o_ref[...] = total.astype(o_ref.dtype)

    scratch = [pltpu.VMEM((bm, bn), F32)] if nk > 1 else []
    body, in_specs, operands = _dep_args(body, in_specs, operands, dep)
    return pl.pallas_call(
        body,
        out_shape=jax.ShapeDtypeStruct((m, n), out_dtype),
        grid=(m // bm, n // bn, nk),
        in_specs=in_specs,
        out_specs=o_spec,
        scratch_shapes=scratch,
        compiler_params=_cparams(("parallel", "parallel", "arbitrary"), est(bm, bn, bk)),
        name=name,
    )(*operands)


def _rms_fwd(x, g, name, dep=None):
    t, d = x.shape
    tb = _pick(t, 512, SUBLANE)

    def body(x_ref, g_ref, h_ref):
        xv = x_ref[...]
        r = lax.rsqrt(jnp.mean(xv * xv, axis=-1, keepdims=True) + EPS)
        h_ref[...] = ((xv * r) * g_ref[...]).astype(BF16)

    blk = pl.BlockSpec((tb, d), lambda i: (i, 0))
    body, in_specs, operands = _dep_args(
        body, [blk, pl.BlockSpec((1, d), lambda i: (0, 0))], [x, g.reshape(1, d)], dep)
    return pl.pallas_call(
        body,
        out_shape=jax.ShapeDtypeStruct((t, d), BF16),
        grid=(t // tb,),
        in_specs=in_specs,
        out_specs=blk,
        compiler_params=_cparams(("parallel",), 2 * tb * d * 6),
        name=name,
    )(*operands)


def _rms_bwd(x, g, dh, dres, name, want_bf16, dep=None):
    t, d = x.shape
    tb = _pick(t, 256, SUBLANE)

    def body(x_ref, g_ref, dh_ref, dres_ref, *outs):
        dx_ref, dg_ref = outs[0], outs[-1]
        xv = x_ref[...]
        r = lax.rsqrt(jnp.mean(xv * xv, axis=-1, keepdims=True) + EPS)
        xhat = xv * r
        dhv = dh_ref[...]
        dxhat = dhv * g_ref[...]
        dx = dres_ref[...] + r * (dxhat - xhat * jnp.mean(dxhat * xhat, axis=-1, keepdims=True))
        dx_ref[...] = dx
        if want_bf16:
            outs[1][...] = dx.astype(BF16)

        @pl.when(pl.program_id(0) == 0)
        def _():
            dg_ref[...] = jnp.zeros_like(dg_ref)

        dg_ref[...] += jnp.sum(dhv * xhat, axis=0, keepdims=True)

    blk = pl.BlockSpec((tb, d), lambda i: (i, 0))
    row = pl.BlockSpec((1, d), lambda i: (0, 0))
    out_shape = [jax.ShapeDtypeStruct((t, d), F32)]
    out_specs = [blk]
    if want_bf16:
        out_shape.append(jax.ShapeDtypeStruct((t, d), BF16))
        out_specs.append(blk)
    out_shape.append(jax.ShapeDtypeStruct((1, d), F32))
    out_specs.append(row)
    body, in_specs, operands = _dep_args(
        body, [blk, row, blk, blk], [x, g.reshape(1, d), dh, dres], dep)
    return pl.pallas_call(
        body,
        out_shape=out_shape,
        grid=(t // tb,),
        in_specs=in_specs,
        out_specs=out_specs,
        compiler_params=_cparams(("arbitrary",), 2 * tb * d * 18),
        name=name,
    )(*operands)


def _loss_head(x3, g, target):
    t, d = x3.shape
    tb = _pick(t, 256, SUBLANE)

    def body(x_ref, g_ref, t_ref, loss_ref, dx_ref, dxb_ref, dg_ref):
        xv = x_ref[...]
        gv = g_ref[...]
        r = lax.rsqrt(jnp.mean(xv * xv, axis=-1, keepdims=True) + EPS)
        xhat = xv * r
        err = xhat * gv - t_ref[...]
        dy = err * (1.0 / d)
        dxhat = dy * gv
        dx = r * (dxhat - xhat * jnp.mean(dxhat * xhat, axis=-1, keepdims=True))
        dx_ref[...] = dx
        dxb_ref[...] = dx.astype(BF16)

        @pl.when(pl.program_id(0) == 0)
        def _():
            dg_ref[...] = jnp.zeros_like(dg_ref)
            loss_ref[...] = jnp.zeros_like(loss_ref)

        dg_ref[...] += jnp.sum(dy * xhat, axis=0, keepdims=True)
        per_token = jnp.mean(err * err, axis=-1, keepdims=True)
        loss_ref[...] += 0.5 * jnp.sum(per_token, axis=0, keepdims=True)

    blk = pl.BlockSpec((tb, d), lambda i: (i, 0))
    row = pl.BlockSpec((1, d), lambda i: (0, 0))
    return pl.pallas_call(
        body,
        out_shape=[jax.ShapeDtypeStruct((1, 1), F32), jax.ShapeDtypeStruct((t, d), F32),
                   jax.ShapeDtypeStruct((t, d), BF16), jax.ShapeDtypeStruct((1, d), F32)],
        grid=(t // tb,),
        in_specs=[blk, row, blk],
        out_specs=[pl.BlockSpec((1, 1), lambda i: (0, 0)), blk, blk, row],
        compiler_params=_cparams(("arbitrary",), 2 * tb * d * 14),
        name="loss_head",
    )(x3, g.reshape(1, d), target)


def _lru_gates(xc, wa, ba, wx, bx, lam):
    nn = (((1,), (0,)), ((), ()))
    xcb = xc.astype(BF16)
    r = _sigmoid(lax.dot_general(xcb, wa, nn, preferred_element_type=F32) + ba)
    i = _sigmoid(lax.dot_general(xcb, wx, nn, preferred_element_type=F32) + bx)
    cl = -LRU_C * _softplus(-lam)
    log_a = cl * r
    a = jnp.exp(log_a)
    one_minus_a2 = _neg_expm1(2.0 * log_a)
    return xcb, r, i, a, one_minus_a2, cl


def _lru_fwd(p, conv_w, conv_b, wa_bd, ba, wx_bd, bx, lam, d_lru, gc, tc, dep=None):
    t = p.shape[0]
    ng = d_lru // gc
    nt = t // tc
    width = conv_w.shape[0]

    def body(lx_ref, gate_ref, cw_ref, cb_ref, wa_ref, ba_ref, wx_ref, bx_ref, lam_ref,
             y_ref, h_ref, halo, hcar, a_s, u_s):
        @pl.when(pl.program_id(1) == 0)
        def _():
            halo[...] = jnp.zeros_like(halo)
            hcar[...] = jnp.zeros_like(hcar)

        x = lx_ref[...]
        cat = jnp.concatenate([halo[...], x], axis=0)
        halo[...] = x[tc - SUBLANE:, :]
        xc = _conv_fwd(cat, cw_ref[...], width) + cb_ref[...]
        _, r, i, a, om, _ = _lru_gates(xc, wa_ref[...], ba_ref[...], wx_ref[...], bx_ref[...], lam_ref[...])
        a_s[...] = a
        u_s[...] = jnp.sqrt(om) * (i * xc)
        hcar[0:1, :] = _scan_tiles(a_s, u_s, h_ref, hcar[0:1, :], tc, reverse=False)
        gl, _ = _gelu_and_grad(gate_ref[...])
        y_ref[...] = (gl * h_ref[...]).astype(BF16)

    blk = lambda off: pl.BlockSpec((tc, gc), lambda g, s, off=off: (s, off + g))
    rowv = lambda rows: pl.BlockSpec((rows, gc), lambda g, s: (0, g))
    wspec = pl.BlockSpec((None, gc, gc), lambda g, s: (g, 0, 0))
    out_blk = pl.BlockSpec((tc, gc), lambda g, s: (s, g))
    body, in_specs, operands = _dep_args(
        body, [blk(0), blk(ng), rowv(width), rowv(1), wspec, rowv(1), wspec, rowv(1), rowv(1)],
        [p, p, conv_w, conv_b.reshape(1, -1), wa_bd, ba.reshape(1, -1), wx_bd, bx.reshape(1, -1),
         lam.reshape(1, -1)], dep)
    return pl.pallas_call(
        body,
        out_shape=[jax.ShapeDtypeStruct((t, d_lru), BF16), jax.ShapeDtypeStruct((t, d_lru), F32)],
        grid=(ng, nt),
        in_specs=in_specs,
        out_specs=[out_blk, out_blk],
        scratch_shapes=[pltpu.VMEM((SUBLANE, gc), F32), pltpu.VMEM((SUBLANE, gc), F32),
                        pltpu.VMEM((tc, gc), F32), pltpu.VMEM((tc, gc), F32)],
        compiler_params=_cparams(("parallel", "arbitrary"), 40 * tc * gc * 4),
        name="lru_fwd",
    )(*operands)


def _lru_bwd(p, hseq, dyp, conv_w, conv_b, wa_bd, ba, wx_bd, bx, lam, d_lru, gc, tc, dp, slab0, dep=None):
    t = p.shape[0]
    ng = d_lru // gc
    nt = t // tc
    width = conv_w.shape[0]
    halo_blocks = tc // SUBLANE
    nn = (((1,), (0,)), ((), ()))
    nt_dims = (((1,), (1,)), ((), ()))
    tn_dims = (((0,), (0,)), ((), ()))

    def body(lx_ref, lxh_ref, gate_ref, h_ref, hh_ref, dyp_ref,
             cw_ref, cb_ref, wa_ref, ba_ref, wx_ref, bx_ref, lam_ref,
             dp_ref, dcw_ref, dcb_ref, dwa_ref, dba_ref, dwx_ref, dbx_ref, dlam_ref,
             nxt_dxc, nxt_a, nxt_g, al_s, b_s, g_s):
        s = pl.program_id(1)
        first_chunk = s == nt - 1

        @pl.when(s == 0)
        def _():
            nxt_dxc[...] = jnp.zeros_like(nxt_dxc)
            nxt_a[...] = jnp.zeros_like(nxt_a)
            nxt_g[...] = jnp.zeros_like(nxt_g)
            for ref in (dcw_ref, dcb_ref, dwa_ref, dba_ref, dwx_ref, dbx_ref, dlam_ref):
                ref[...] = jnp.zeros_like(ref)

        keep = jnp.where(first_chunk, 0.0, 1.0)
        x = lx_ref[...]
        catx = jnp.concatenate([lxh_ref[...] * keep, x], axis=0)
        cw = cw_ref[...]
        xc = _conv_fwd(catx, cw, width) + cb_ref[...]
        wa = wa_ref[...]
        wx = wx_ref[...]
        lam_v = lam_ref[...]
        xcb, r, i, a, om, cl = _lru_gates(xc, wa, ba_ref[...], wx, bx_ref[...], lam_v)
        mult = jnp.sqrt(om)

        h = h_ref[...]
        hprev = _rows_before(jnp.concatenate([hh_ref[...] * keep, h], axis=0), 1)
        gl, dgl = _gelu_and_grad(gate_ref[...])
        dyp_v = dyp_ref[...]
        dp_ref[1] = (dyp_v * h * dgl).astype(BF16)

        al_s[...] = _rows_after(jnp.concatenate([a, nxt_a[...]], axis=0), 1)
        b_s[...] = dyp_v * gl
        nxt_g[0:1, :] = _scan_tiles(al_s, b_s, g_s, nxt_g[0:1, :], tc, reverse=True)
        nxt_a[...] = a[0:SUBLANE, :]
        du = g_s[...]

        da = du * hprev
        dmult = du * (i * xc)
        di = du * mult * xc
        dxc = du * mult * i
        dlog_a = da * a - dmult * (a * a / mult)
        dlam_ref[...] += jnp.sum(dlog_a * r, axis=0, keepdims=True) * (LRU_C * _sigmoid(-lam_v))
        dza = (dlog_a * cl) * r * (1.0 - r)
        dzx = di * i * (1.0 - i)
        dba_ref[...] += jnp.sum(dza, axis=0, keepdims=True)
        dbx_ref[...] += jnp.sum(dzx, axis=0, keepdims=True)
        dzab = dza.astype(BF16)
        dzxb = dzx.astype(BF16)
        dwa_ref[...] += lax.dot_general(xcb, dzab, tn_dims, preferred_element_type=F32)
        dwx_ref[...] += lax.dot_general(xcb, dzxb, tn_dims, preferred_element_type=F32)
        dxc = dxc + lax.dot_general(dzab, wa, nt_dims, preferred_element_type=F32)
        dxc = dxc + lax.dot_general(dzxb, wx, nt_dims, preferred_element_type=F32)
        dcb_ref[...] += jnp.sum(dxc, axis=0, keepdims=True)
        _conv_bwd_weight(dcw_ref, dxc, catx, width)
        catd = jnp.concatenate([dxc, nxt_dxc[...]], axis=0)
        dp_ref[0] = _conv_bwd_input(catd, cw, width).astype(BF16)
        nxt_dxc[...] = dxc[0:SUBLANE, :]

    rev = lambda s: nt - 1 - s
    blk = lambda off: pl.BlockSpec((tc, gc), lambda g, s, off=off: (rev(s), off + g))
    halo = lambda off: pl.BlockSpec(
        (SUBLANE, gc), lambda g, s, off=off: (jnp.maximum(rev(s) * halo_blocks - 1, 0), off + g))
    rowv = lambda rows: pl.BlockSpec((rows, gc), lambda g, s: (0, g))
    wspec = pl.BlockSpec((None, gc, gc), lambda g, s: (g, 0, 0))
    out_blk = pl.BlockSpec((tc, gc), lambda g, s: (rev(s), g))
    vec = lambda rows: jax.ShapeDtypeStruct((rows, d_lru), F32)
    wshape = jax.ShapeDtypeStruct((ng, gc, gc), F32)
    body, in_specs, operands = _dep_args(
        body,
        [blk(0), halo(0), blk(ng), blk(0), halo(0), blk(0),
         rowv(width), rowv(1), wspec, rowv(1), wspec, rowv(1), rowv(1)],
        [p, p, p, hseq, hseq, dyp,
         conv_w, conv_b.reshape(1, -1), wa_bd, ba.reshape(1, -1), wx_bd,
         bx.reshape(1, -1), lam.reshape(1, -1)], dp, dep)
    assert dp.shape[2] == d_lru and slab0 % 2 == 0
    return pl.pallas_call(
        body,
        out_shape=[jax.ShapeDtypeStruct(dp.shape, dp.dtype),
                   vec(width), vec(1), wshape, vec(1), wshape, vec(1), vec(1)],
        grid=(ng, nt),
        in_specs=in_specs,
        out_specs=[pl.BlockSpec((2, tc, gc), lambda g, s: (slab0 // 2, rev(s), g)),
                   rowv(width), rowv(1), wspec, rowv(1), wspec, rowv(1), rowv(1)],
        input_output_aliases={13: 0},
        scratch_shapes=[pltpu.VMEM((SUBLANE, gc), F32), pltpu.VMEM((SUBLANE, gc), F32),
                        pltpu.VMEM((SUBLANE, gc), F32),
                        pltpu.VMEM((tc, gc), F32), pltpu.VMEM((tc, gc), F32), pltpu.VMEM((tc, gc), F32)],
        compiler_params=_cparams(("parallel", "arbitrary"), 80 * tc * gc * 4),
        name="lru_bwd",
    )(*operands)


def _sc_fwd(p, conv_w, col0, d_sc, cb, tc):
    t = p.shape[0]
    nc = d_sc // cb
    nt = t // tc
    width = conv_w.shape[0]
    base = col0 // cb

    def body(b_ref, c_ref, v_ref, w_ref, y_ref, halo):
        @pl.when(pl.program_id(1) == 0)
        def _():
            halo[...] = jnp.zeros_like(halo)

        cv = c_ref[...] * v_ref[...]
        cat = jnp.concatenate([halo[...], cv], axis=0)
        halo[...] = cv[tc - SUBLANE:, :]
        y_ref[...] = (b_ref[...] * _conv_fwd(cat, w_ref[...], width)).astype(BF16)

    blk = lambda slab: pl.BlockSpec((tc, cb), lambda j, s, slab=slab: (s, base + slab * nc + j))
    return pl.pallas_call(
        body,
        out_shape=jax.ShapeDtypeStruct((t, d_sc), BF16),
        grid=(nc, nt),
        in_specs=[blk(0), blk(1), blk(2), pl.BlockSpec((width, cb), lambda j, s: (0, j))],
        out_specs=pl.BlockSpec((tc, cb), lambda j, s: (s, j)),
        scratch_shapes=[pltpu.VMEM((SUBLANE, cb), F32)],
        compiler_params=_cparams(("parallel", "arbitrary"), 20 * tc * cb * 4),
        name="sc_fwd",
    )(p, p, p, conv_w)


def _sc_bwd(p, dyp, conv_w, col0, d_sc, cb, tc, dp, slab0):
    t = p.shape[0]
    nc = d_sc // cb
    nt = t // tc
    width = conv_w.shape[0]
    base = col0 // cb
    halo_blocks = tc // SUBLANE

    def body(b_ref, c_ref, ch_ref, v_ref, vh_ref, dyp_ref, w_ref,
             dp_ref, dw_ref, nxt_dq):
        s = pl.program_id(1)

        @pl.when(s == 0)
        def _():
            nxt_dq[...] = jnp.zeros_like(nxt_dq)
            dw_ref[...] = jnp.zeros_like(dw_ref)

        keep = jnp.where(s == nt - 1, 0.0, 1.0)
        cvals = c_ref[...]
        vvals = v_ref[...]
        w = w_ref[...]
        catcv = jnp.concatenate([ch_ref[...] * vh_ref[...] * keep, cvals * vvals], axis=0)
        q = _conv_fwd(catcv, w, width)
        dyp_v = dyp_ref[...]
        dp_ref[0] = (dyp_v * q).astype(BF16)
        dq = dyp_v * b_ref[...]
        _conv_bwd_weight(dw_ref, dq, catcv, width)
        dcv = _conv_bwd_input(jnp.concatenate([dq, nxt_dq[...]], axis=0), w, width)
        nxt_dq[...] = dq[0:SUBLANE, :]
        dp_ref[1] = (dcv * vvals).astype(BF16)
        dp_ref[2] = (dcv * cvals).astype(BF16)

    rev = lambda s: nt - 1 - s
    blk = lambda slab: pl.BlockSpec((tc, cb), lambda j, s, slab=slab: (rev(s), base + slab * nc + j))
    halo = lambda slab: pl.BlockSpec(
        (SUBLANE, cb),
        lambda j, s, slab=slab: (jnp.maximum(rev(s) * halo_blocks - 1, 0), base + slab * nc + j))
    out_blk = pl.BlockSpec((tc, cb), lambda j, s: (rev(s), j))
    wblk = pl.BlockSpec((width, cb), lambda j, s: (0, j))
    assert dp.shape[2] == d_sc and slab0 % 3 == 0
    operands = [p, p, p, p, p, dyp, conv_w]
    body, in_specs, operands = _dep_args(
        body, [blk(0), blk(1), halo(1), blk(2), halo(2), out_blk, wblk], operands, dp)
    return pl.pallas_call(
        body,
        out_shape=[jax.ShapeDtypeStruct(dp.shape, dp.dtype), jax.ShapeDtypeStruct((width, d_sc), F32)],
        grid=(nc, nt),
        in_specs=in_specs,
        out_specs=[pl.BlockSpec((3, tc, cb), lambda j, s: (slab0 // 3, rev(s), j)), wblk],
        input_output_aliases={7: 0},
        scratch_shapes=[pltpu.VMEM((SUBLANE, cb), F32)],
        compiler_params=_cparams(("parallel", "arbitrary"), 30 * tc * cb * 4),
        name="sc_bwd",
    )(*operands)


def _merge_fwd(p, y_lru, y_sc, col0, tc):
    t, d = y_lru.shape
    cb = _pick(math.gcd(d, col0), 1024)
    nc = d // cb
    base = col0 // cb

    def body(gl_ref, gs_ref, yl_ref, ys_ref, o_ref):
        o_ref[...] = (_sigmoid(gl_ref[...]) * yl_ref[...] + _sigmoid(gs_ref[...]) * ys_ref[...]).astype(BF16)

    gate = lambda slab: pl.BlockSpec((tc, cb), lambda s, j, slab=slab: (s, base + slab * nc + j))
    blk = pl.BlockSpec((tc, cb), lambda s, j: (s, j))
    return pl.pallas_call(
        body,
        out_shape=jax.ShapeDtypeStruct((t, d), BF16),
        grid=(t // tc, nc),
        in_specs=[gate(0), gate(1), blk, blk],
        out_specs=blk,
        compiler_params=_cparams(("parallel", "parallel"), 2 * tc * cb * 20),
        name="merge_fwd",
    )(p, p, y_lru, y_sc)


def _merge_bwd(p, y_lru, y_sc, dmerged, col0, tc, n_slabs):
    t, d = y_lru.shape
    cb = _pick(math.gcd(d, col0), 1024)
    nc = d // cb
    base = col0 // cb

    def body(gl_ref, gs_ref, yl_ref, ys_ref, dm_ref, dp_ref, dyl_ref, dys_ref):
        dm = dm_ref[...]
        sl = _sigmoid(gl_ref[...])
        ss = _sigmoid(gs_ref[...])
        dp_ref[0] = (dm * yl_ref[...] * (sl * (1.0 - sl))).astype(BF16)
        dp_ref[1] = (dm * ys_ref[...] * (ss * (1.0 - ss))).astype(BF16)
        dyl_ref[...] = (dm * sl).astype(BF16)
        dys_ref[...] = (dm * ss).astype(BF16)

    gate = lambda slab: pl.BlockSpec((tc, cb), lambda s, j, slab=slab: (s, base + slab * nc + j))
    blk = pl.BlockSpec((tc, cb), lambda s, j: (s, j))
    act = jax.ShapeDtypeStruct((t, d), BF16)
    return pl.pallas_call(
        body,
        out_shape=[jax.ShapeDtypeStruct((n_slabs, t, cb), BF16), act, act],
        grid=(t // tc, nc),
        in_specs=[gate(0), gate(1), blk, blk, blk],
        out_specs=[pl.BlockSpec((2, tc, cb), lambda s, j: (j, s, 0)), blk, blk],
        compiler_params=_cparams(("parallel", "parallel"), 2 * tc * cb * 28),
        name="merge_bwd",
    )(p, p, y_lru, y_sc, dmerged)


def _ffn_act_fwd(up, conv_w, d_ff, cb, tc, dep=None):
    t = up.shape[0]
    nc = d_ff // cb
    nt = t // tc
    width = conv_w.shape[0]

    def body(g_ref, v_ref, wg_ref, wv_ref, o_ref, halo_g, halo_v):
        @pl.when(pl.program_id(1) == 0)
        def _():
            halo_g[...] = jnp.zeros_like(halo_g)
            halo_v[...] = jnp.zeros_like(halo_v)

        g = g_ref[...]
        v = v_ref[...]
        ug = _conv_fwd(jnp.concatenate([halo_g[...], g], axis=0), wg_ref[...], width)
        uv = _conv_fwd(jnp.concatenate([halo_v[...], v], axis=0), wv_ref[...], width)
        halo_g[...] = g[tc - SUBLANE:, :]
        halo_v[...] = v[tc - SUBLANE:, :]
        o_ref[...] = (ug * _sigmoid(ug) * uv).astype(BF16)

    blk = lambda half: pl.BlockSpec((tc, cb), lambda j, s, half=half: (s, half * nc + j))
    wblk = lambda half: pl.BlockSpec((width, cb), lambda j, s, half=half: (0, half * nc + j))
    body, in_specs, operands = _dep_args(
        body, [blk(0), blk(1), wblk(0), wblk(1)], [up, up, conv_w, conv_w], dep)
    return pl.pallas_call(
        body,
        out_shape=jax.ShapeDtypeStruct((t, d_ff), BF16),
        grid=(nc, nt),
        in_specs=in_specs,
        out_specs=pl.BlockSpec((tc, cb), lambda j, s: (s, j)),
        scratch_shapes=[pltpu.VMEM((SUBLANE, cb), F32), pltpu.VMEM((SUBLANE, cb), F32)],
        compiler_params=_cparams(("parallel", "arbitrary"), 24 * tc * cb * 4),
        name="ffn_act_fwd",
    )(*operands)


def _ffn_act_bwd(up, dact, conv_w, d_ff, cb, tc, dep=None):
    t = up.shape[0]
    nc = d_ff // cb
    nt = t // tc
    width = conv_w.shape[0]
    halo_blocks = tc // SUBLANE

    def body(g_ref, gh_ref, v_ref, vh_ref, da_ref, wg_ref, wv_ref,
             dup_ref, dwg_ref, dwv_ref, nxt_g, nxt_v):
        s = pl.program_id(1)

        @pl.when(s == 0)
        def _():
            nxt_g[...] = jnp.zeros_like(nxt_g)
            nxt_v[...] = jnp.zeros_like(nxt_v)
            dwg_ref[...] = jnp.zeros_like(dwg_ref)
            dwv_ref[...] = jnp.zeros_like(dwv_ref)

        keep = jnp.where(s == nt - 1, 0.0, 1.0)
        wg = wg_ref[...]
        wv = wv_ref[...]
        catg = jnp.concatenate([gh_ref[...] * keep, g_ref[...]], axis=0)
        catv = jnp.concatenate([vh_ref[...] * keep, v_ref[...]], axis=0)
        ug = _conv_fwd(catg, wg, width)
        uv = _conv_fwd(catv, wv, width)
        sg = _sigmoid(ug)
        da = da_ref[...]
        duv = da * (ug * sg)
        dup_ref[1] = _conv_bwd_input(jnp.concatenate([duv, nxt_v[...]], axis=0), wv, width).astype(BF16)
        nxt_v[...] = duv[0:SUBLANE, :]
        _conv_bwd_weight(dwv_ref, duv, catv, width)
        dug = da * uv * (sg * (1.0 + ug * (1.0 - sg)))
        dup_ref[0] = _conv_bwd_input(jnp.concatenate([dug, nxt_g[...]], axis=0), wg, width).astype(BF16)
        nxt_g[...] = dug[0:SUBLANE, :]
        _conv_bwd_weight(dwg_ref, dug, catg, width)

    rev = lambda s: nt - 1 - s
    blk = lambda half: pl.BlockSpec((tc, cb), lambda j, s, half=half: (rev(s), half * nc + j))
    halo = lambda half: pl.BlockSpec(
        (SUBLANE, cb), lambda j, s, half=half: (jnp.maximum(rev(s) * halo_blocks - 1, 0), half * nc + j))
    wblk = lambda half: pl.BlockSpec((width, cb), lambda j, s, half=half: (0, half * nc + j))
    out_blk = pl.BlockSpec((tc, cb), lambda j, s: (rev(s), j))
    wout = pl.BlockSpec((width, cb), lambda j, s: (0, j))
    act = jax.ShapeDtypeStruct((t, d_ff), BF16)
    wshape = jax.ShapeDtypeStruct((width, d_ff), F32)
    body, in_specs, operands = _dep_args(
        body, [blk(0), halo(0), blk(1), halo(1), out_blk, wblk(0), wblk(1)],
        [up, up, up, up, dact, conv_w, conv_w], dep)
    return pl.pallas_call(
        body,
        out_shape=[jax.ShapeDtypeStruct((2, t, d_ff), BF16), wshape, wshape],
        grid=(nc, nt),
        in_specs=in_specs,
        out_specs=[pl.BlockSpec((2, tc, cb), lambda j, s: (0, rev(s), j)), wout, wout],
        scratch_shapes=[pltpu.VMEM((SUBLANE, cb), F32), pltpu.VMEM((SUBLANE, cb), F32)],
        compiler_params=_cparams(("parallel", "arbitrary"), 40 * tc * cb * 4),
        name="ffn_act_bwd",
    )(*operands)


def _mesh_pos():
    x, y, c = lax.axis_index("x"), lax.axis_index("y"), lax.axis_index("c")
    return x, y, c


def _other_chips(x, y):
    return [(1 - x, y), (x, 1 - y), (1 - x, 1 - y)]


def _cast_place(w, chip, col_sharded, name, dep=None):
    r, cdim = w.shape
    full = (r, cdim * N_CHIPS) if col_sharded else (r * N_CHIPS, cdim)
    rb = _pick(r, max(BF16_ROWS, STREAM_BLOCK // cdim), BF16_ROWS)
    nb = r // rb

    def body(chip_ref, w_ref, o_ref):
        o_ref[...] = w_ref[...].astype(BF16)

    if col_sharded:
        out_map = lambda i, chip_ref: (i, chip_ref[0])
    else:
        out_map = lambda i, chip_ref: (chip_ref[0] * nb + i, 0)
    grid_spec = pltpu.PrefetchScalarGridSpec(
        num_scalar_prefetch=1,
        grid=(nb,),
        in_specs=[pl.BlockSpec((rb, cdim), lambda i, chip_ref: (i, 0))] + ([ANY] if dep is not None else []),
        out_specs=pl.BlockSpec((rb, cdim), out_map),
    )
    body, _, operands = _dep_args(body, [], [chip, w], dep)
    return pl.pallas_call(
        body,
        out_shape=jax.ShapeDtypeStruct(full, BF16),
        grid_spec=grid_spec,
        compiler_params=_cparams(("parallel",), 2 * rb * cdim * 6),
        name=name,
    )(*operands)


def _remote(src, dst, send_sems, recv_sems, idx, to):
    return pltpu.make_async_remote_copy(
        src_ref=src, dst_ref=dst, send_sem=send_sems.at[idx], recv_sem=recv_sems.at[idx],
        device_id=to, device_id_type=MESH)


def _exchange(name, arrays, n_sems, plan):
    n = len(arrays)

    def body(*refs):
        bufs = refs[n:2 * n]
        send_sems, recv_sems = refs[2 * n:]
        sends, arrivals = plan(bufs, send_sems, recv_sems)
        for cp in sends:
            cp.start()
        for cp in arrivals:
            cp.wait_recv()
        for cp in sends:
            cp.wait_send()

    outs = pl.pallas_call(
        body,
        out_shape=[jax.ShapeDtypeStruct(a.shape, a.dtype) for a in arrays],
        in_specs=[ANY] * n,
        out_specs=[ANY] * n,
        input_output_aliases={k: k for k in range(n)},
        scratch_shapes=[pltpu.SemaphoreType.DMA((n_sems,)), pltpu.SemaphoreType.DMA((n_sems,))],
        name=name,
    )(*arrays)
    return list(outs)


def _exchange_start(name, arrays, n_sems, plan, after=None):
    n = len(arrays)
    n_in = n + (after is not None)

    def body(*refs):
        bufs = refs[:n]
        send_sems, recv_sems = refs[n_in], refs[n_in + 1]
        token = refs[-1]
        sends, _ = plan(bufs, send_sems, recv_sems)
        for cp in sends:
            cp.start()
        token[...] = jnp.zeros_like(token)

    out = pl.pallas_call(
        body,
        out_shape=(pltpu.SemaphoreType.DMA((n_sems,)), pltpu.SemaphoreType.DMA((n_sems,)),
                   *[pltpu.HBM(a.shape, a.dtype) for a in arrays],
                   jax.ShapeDtypeStruct((SUBLANE, LANE), F32)),
        in_specs=[HBM_SPEC] * n + [ANY] * (n_in - n),
        out_specs=(SEM_SPEC, SEM_SPEC, *[HBM_SPEC] * n, VMEM_SPEC),
        input_output_aliases={k: 2 + k for k in range(n)},
        compiler_params=pltpu.CompilerParams(has_side_effects=DATAFLOW_EFFECT),
        name=name,
    )(*[pltpu.with_memory_space_constraint(a, pltpu.HBM) for a in arrays], *([after] if after is not None else []))
    return out[0], out[1], list(out[2:2 + n]), out[-1]


def _exchange_wait(name, arrays, send_sems, recv_sems, after, plan):
    n = len(arrays)

    def body(*refs):
        bufs = refs[:n]
        sends, arrivals = plan(bufs, refs[n], refs[n + 1])
        for cp in arrivals:
            cp.wait_recv()
        for cp in sends:
            cp.wait_send()

    outs = pl.pallas_call(
        body,
        out_shape=[pltpu.HBM(a.shape, a.dtype) for a in arrays],
        in_specs=[HBM_SPEC] * n + [SEM_SPEC, SEM_SPEC, ANY],
        out_specs=[HBM_SPEC] * n,
        input_output_aliases={k: k for k in range(n)},
        compiler_params=pltpu.CompilerParams(has_side_effects=DATAFLOW_EFFECT),
        name=name,
    )(*arrays, send_sems, recv_sems, after)
    return list(outs)


def _half_block(ref, shard_shape, col_sharded, chip, half):
    r, cdim = shard_shape
    h = r // 2
    if col_sharded:
        return ref.at[pl.ds(pl.multiple_of(half * h, BF16_ROWS), h),
                      pl.ds(pl.multiple_of(chip * cdim, LANE), cdim)]
    return ref.at[pl.ds(pl.multiple_of(chip * r + half * h, BF16_ROWS), h), :]


def _gather_plan(shard_shapes, col_sharded, ks):
    def plan(bufs, send_sems, recv_sems):
        x, y, c = _mesh_pos()
        sends, arrivals = [], []
        for ref, k in zip(bufs, ks):
            mine = _half_block(ref, shard_shapes[k], col_sharded[k], 2 * x + y, c)
            for j, (px, py) in enumerate(_other_chips(x, y)):
                landed = _half_block(ref, shard_shapes[k], col_sharded[k], 2 * px + py, c)
                sends.append(_remote(mine, mine, send_sems, recv_sems, 3 * k + j, (px, py, c)))
                arrivals.append(_remote(landed, landed, send_sems, recv_sems, 3 * k + j, (px, py, c)))
        return sends, arrivals
    return plan


def _forward_plan(shard_shapes, col_sharded, ks):
    def plan(bufs, send_sems, recv_sems):
        x, y, c = _mesh_pos()
        sends, arrivals = [], []
        for i, (ref, k) in enumerate(zip(bufs, ks)):
            for j, (px, py) in enumerate(_other_chips(x, y)):
                landed = _half_block(ref, shard_shapes[k], col_sharded[k], 2 * px + py, c)
                theirs = _half_block(ref, shard_shapes[k], col_sharded[k], 2 * px + py, 1 - c)
                sends.append(_remote(landed, landed, send_sems, recv_sems, 3 * i + j, (x, y, 1 - c)))
                arrivals.append(_remote(theirs, theirs, send_sems, recv_sems, 3 * i + j, (x, y, 1 - c)))
        return sends, arrivals
    return plan


def _small_gather(small):
    def body(small_ref, out_ref, send_sems, recv_sems):
        x, y, c = _mesh_pos()
        me = 2 * x + y
        out_ref[me] = small_ref[...]
        copies = []
        for j, (px, py) in enumerate(_other_chips(x, y)):
            cp = _remote(small_ref, out_ref.at[me], send_sems, recv_sems, j, (px, py, c))
            cp.start()
            copies.append(cp)
        for j, (px, py) in enumerate(_other_chips(x, y)):
            _remote(small_ref, out_ref.at[2 * px + py], send_sems, recv_sems, j, (px, py, c)).wait_recv()
        for cp in copies:
            cp.wait_send()

    return pl.pallas_call(
        body,
        out_shape=jax.ShapeDtypeStruct((N_CHIPS,) + small.shape, small.dtype),
        in_specs=[VMEM_SPEC],
        out_specs=VMEM_SPEC,
        scratch_shapes=[pltpu.SemaphoreType.DMA((N_CHIPS - 1,)), pltpu.SemaphoreType.DMA((N_CHIPS - 1,))],
        name="gather_small",
    )(small)


def _as3d(g, col_sharded):
    r, cdim = g.shape
    return g.reshape(1, r, cdim) if col_sharded else g.reshape(N_CHIPS, r // N_CHIPS, cdim)


def _pair_plan(m):
    def plan(bufs, send_sems, recv_sems):
        x, y, c = _mesh_pos()
        copies = []
        for i in range(m):
            h = bufs[i].shape[1] // 2
            src = bufs[i].at[:, pl.ds(pl.multiple_of((1 - c) * h, BF16_ROWS), h), :]
            copies.append(_remote(src, bufs[m + i], send_sems, recv_sems, i, (x, y, 1 - c)))
        return copies, copies
    return plan


def _chip_plan(col_flags):
    m = len(col_flags)

    def plan(bufs, send_sems, recv_sems):
        x, y, c = _mesh_pos()
        copies = []
        for i in range(m):
            land = bufs[m + i]
            width = land.shape[2]
            for j, (px, py) in enumerate(_other_chips(x, y)):
                q = 2 * px + py
                if col_flags[i]:
                    src = bufs[i].at[0, :, pl.ds(pl.multiple_of(q * width, LANE), width)]
                else:
                    src = bufs[i].at[q]
                copies.append(_remote(src, land.at[j], send_sems, recv_sems, 3 * i + j, (px, py, c)))
        return copies, copies
    return plan


def _share_plan(m):
    def plan(bufs, send_sems, recv_sems):
        x, y, c = _mesh_pos()
        sends, arrivals = [], []
        for i in range(m):
            h = bufs[i].shape[0] // 2
            mine = bufs[i].at[pl.ds(pl.multiple_of(c * h, SUBLANE), h), :]
            theirs = bufs[i].at[pl.ds(pl.multiple_of((1 - c) * h, SUBLANE), h), :]
            sends.append(_remote(mine, mine, send_sems, recv_sems, i, (x, y, 1 - c)))
            arrivals.append(_remote(theirs, theirs, send_sems, recv_sems, i, (x, y, 1 - c)))
        return sends, arrivals
    return plan


def _pair_add(g3, other, core):
    a, r, cdim = g3.shape
    h = r // 2
    rb = _pick(h, max(BF16_ROWS, STREAM_BLOCK // cdim), BF16_ROWS)
    nb = h // rb

    def body(core_ref, g_ref, o_ref, out_ref):
        out_ref[...] = (g_ref[...].astype(F32) + o_ref[...].astype(F32)).astype(BF16)

    grid_spec = pltpu.PrefetchScalarGridSpec(
        num_scalar_prefetch=1,
        grid=(a, nb),
        in_specs=[pl.BlockSpec((None, rb, cdim), lambda i, j, core_ref: (i, core_ref[0] * nb + j, 0)),
                  pl.BlockSpec((None, rb, cdim), lambda i, j, core_ref: (i, j, 0))],
        out_specs=pl.BlockSpec((None, rb, cdim), lambda i, j, core_ref: (i, j, 0)),
    )
    return pl.pallas_call(
        body,
        out_shape=jax.ShapeDtypeStruct((a, h, cdim), BF16),
        grid_spec=grid_spec,
        compiler_params=_cparams(("parallel", "parallel"), 2 * rb * cdim * 10),
        name="grad_pair_add",
    )(core, g3, other)


def _small_allreduce(small):
    rows = small.shape[0]
    pad = (-rows) % (2 * SUBLANE)
    if pad:
        small = jnp.pad(small, ((0, pad), (0, 0)))
    h = small.shape[0] // 2
    half_shape = (h, small.shape[1])

    def body(small_ref, out_ref, theirs, by_chip, send_sems, recv_sems):
        x, y, c = _mesh_pos()
        me = 2 * x + y
        sibling = (x, y, 1 - c)
        mine = pl.ds(pl.multiple_of(c * h, SUBLANE), h)
        other = pl.ds(pl.multiple_of((1 - c) * h, SUBLANE), h)
        swap = _remote(small_ref, theirs, send_sems, recv_sems, 0, sibling)
        swap.start()
        swap.wait()
        by_chip[me] = small_ref[mine, :] + theirs[mine, :]
        copies = []
        for j, (px, py) in enumerate(_other_chips(x, y)):
            cp = _remote(by_chip.at[me], by_chip.at[me], send_sems, recv_sems, 1 + j, (px, py, c))
            cp.start()
            copies.append(cp)
        for j, (px, py) in enumerate(_other_chips(x, y)):
            landed = by_chip.at[2 * px + py]
            _remote(landed, landed, send_sems, recv_sems, 1 + j, (px, py, c)).wait_recv()
        total = by_chip[0]
        for q in range(1, N_CHIPS):
            total = total + by_chip[q]
        out_ref[mine, :] = total
        for cp in copies:
            cp.wait_send()
        share = _remote(out_ref.at[mine, :], out_ref.at[mine, :], send_sems, recv_sems, 4, sibling)
        share.start()
        _remote(out_ref.at[other, :], out_ref.at[other, :], send_sems, recv_sems, 4, sibling).wait_recv()
        share.wait_send()

    out = pl.pallas_call(
        body,
        out_shape=jax.ShapeDtypeStruct(small.shape, F32),
        in_specs=[VMEM_SPEC],
        out_specs=VMEM_SPEC,
        scratch_shapes=[pltpu.VMEM(small.shape, F32), pltpu.VMEM((N_CHIPS,) + half_shape, F32),
                        pltpu.SemaphoreType.DMA((5,)), pltpu.SemaphoreType.DMA((5,))],
        compiler_params=pltpu.CompilerParams(
            vmem_limit_bytes=min(VMEM_BUDGET, 8 * _nbytes(small.shape, F32) + (8 << 20))),
        name="grad_small_allreduce",
    )(small)
    return out[:rows]


def _chip_sum(partial, land, where, col_sharded):
    _, h, cdim = land.shape
    rb = _pick(h, max(BF16_ROWS, STREAM_BLOCK // cdim), BF16_ROWS)
    nb = h // rb

    def body(where_ref, own_ref, l_ref, o_ref):
        total = own_ref[...].astype(F32)
        for j in range(N_CHIPS - 1):
            total = total + l_ref[j].astype(F32)
        o_ref[...] = total

    if col_sharded:
        own_map = lambda i, w: (0, i, w[0])
    else:
        own_map = lambda i, w: (w[0], i, 0)
    grid_spec = pltpu.PrefetchScalarGridSpec(
        num_scalar_prefetch=1,
        grid=(nb,),
        in_specs=[pl.BlockSpec((None, rb, cdim), own_map),
                  pl.BlockSpec((N_CHIPS - 1, rb, cdim), lambda i, w: (0, i, 0))],
        out_specs=pl.BlockSpec((rb, cdim), lambda i, w: (w[1] * nb + i, 0)),
    )
    return pl.pallas_call(
        body,
        out_shape=jax.ShapeDtypeStruct((2 * h, cdim), F32),
        grid_spec=grid_spec,
        compiler_params=_cparams(("parallel",), 2 * rb * cdim * 12),
        name="grad_chip_sum",
    )(where, partial, land)


def _adamw(w, g, m, v, name, dep=None):
    r, cdim = w.shape
    rb = _pick(r, max(SUBLANE, (STREAM_BLOCK // 4) // cdim), SUBLANE)
    c1 = 1.0 - ADAM_B1 ** ADAM_STEP
    c2 = 1.0 - ADAM_B2 ** ADAM_STEP

    def body(w_ref, g_ref, m_ref, v_ref, go_ref, d_ref, mo_ref, vo_ref):
        gv = g_ref[...]
        mn = ADAM_B1 * m_ref[...] + (1.0 - ADAM_B1) * gv
        vn = ADAM_B2 * v_ref[...] + (1.0 - ADAM_B2) * (gv * gv)
        m_hat = mn / c1
        v_hat = vn / c2
        d_ref[...] = -ADAM_LR * (m_hat / (jnp.sqrt(v_hat) + ADAM_EPS) + ADAM_WD * w_ref[...])
        go_ref[...] = gv
        mo_ref[...] = mn
        vo_ref[...] = vn

    blk = pl.BlockSpec((rb, cdim), lambda i: (i, 0))
    shape = jax.ShapeDtypeStruct((r, cdim), F32)
    body, in_specs, operands = _dep_args(body, [blk] * 4, [w, g, m, v], dep)
    return pl.pallas_call(
        body,
        out_shape=[shape] * 4,
        grid=(r // rb,),
        in_specs=in_specs,
        out_specs=[blk] * 4,
        compiler_params=_cparams(("parallel",), 2 * rb * cdim * 4 * 8),
        name=name,
    )(*operands)


def _pack(arrays):
    tile = SUBLANE * LANE
    pieces = []
    for arr in arrays:
        flat = arr.reshape(-1)
        pad = (-flat.shape[0]) % tile
        if pad:
            flat = jnp.concatenate([flat, jnp.zeros((pad,), flat.dtype)])
        pieces.append(flat)
    return jnp.concatenate(pieces).reshape(-1, LANE)


def _unpack(packed, shapes):
    tile = SUBLANE * LANE
    flat = packed.reshape(-1)
    out, off = [], 0
    for shp in shapes:
        size = math.prod(shp)
        out.append(flat[off:off + size].reshape(shp))
        off += size + ((-size) % tile)
    return out


def _block_diag_groups(w, per_group):
    hcount, hd, _ = w.shape
    ng = hcount // per_group
    w4 = w.reshape(ng, per_group, hd, hd)
    eye = jnp.eye(per_group, dtype=w.dtype)
    bd = w4[:, :, :, None, :] * eye[None, :, None, :, None]
    return bd.reshape(ng, per_group * hd, per_group * hd).astype(BF16)


def _diag_blocks(wbd, per_group, hd):
    ng = wbd.shape[0]
    w5 = wbd.reshape(ng, per_group, hd, per_group, hd)
    blocks = [w5[:, i, :, i, :] for i in range(per_group)]
    return jnp.stack(blocks, axis=1).reshape(ng * per_group, hd, hd)


def kernel(x, g_mix, w_in, lru_conv_w, lru_conv_b, lru_wa, lru_ba, lru_wx, lru_bx, lru_lambda, lru_w_out, sc_conv_w, sc_w_out, w_o, g_ffn, ffn_w_up, ffn_conv_w, ffn_w_down, g_final, loss_target, m_g_mix, m_w_in, m_lru_conv_w, m_lru_conv_b, m_lru_wa, m_lru_ba, m_lru_wx, m_lru_bx, m_lru_lambda, m_lru_w_out, m_sc_conv_w, m_sc_w_out, m_w_o, m_g_ffn, m_ffn_w_up, m_ffn_conv_w, m_ffn_w_down, m_g_final, v_g_mix, v_w_in, v_lru_conv_w, v_lru_conv_b, v_lru_wa, v_lru_ba, v_lru_wx, v_lru_bx, v_lru_lambda, v_lru_w_out, v_sc_conv_w, v_sc_w_out, v_w_o, v_g_ffn, v_ffn_w_up, v_ffn_conv_w, v_ffn_w_down, v_g_final):
    seq, d_model = x.shape[1], x.shape[2]
    heads, head_dim, _ = lru_wa.shape
    d_lru = heads * head_dim
    d_sc = sc_w_out.shape[0]
    d_ff = ffn_w_down.shape[0] * N_CHIPS
    assert x.shape[0] == 1 and w_in.shape[1] * N_CHIPS == 2 * d_lru + 3 * d_sc + 2 * d_model
    xs = x.reshape(seq, d_model)
    target = loss_target.reshape(seq, d_model)

    chip = 2 * lax.axis_index("x") + lax.axis_index("y")
    core = lax.axis_index("c").astype(jnp.int32).reshape(1)

    big_w = [w_in, lru_w_out, sc_w_out, w_o, ffn_w_up, ffn_w_down]
    big_m = [m_w_in, m_lru_w_out, m_sc_w_out, m_w_o, m_ffn_w_up, m_ffn_w_down]
    big_v = [v_w_in, v_lru_w_out, v_sc_w_out, v_w_o, v_ffn_w_up, v_ffn_w_down]
    col_sharded = [True, True, True, False, True, False]
    conv_shards = [lru_conv_w, sc_conv_w, ffn_conv_w]
    conv_pack = jnp.concatenate(
        [jnp.pad(w, ((0, SUBLANE - w.shape[0]), (0, 0))) for w in conv_shards], axis=1)
    big_names = ["w_in", "lru_w_out", "sc_w_out", "w_o", "ffn_w_up", "ffn_w_down"]
    chip_arr = chip.astype(jnp.int32).reshape(1)
    placed = [_cast_place(big_w[0], chip_arr, col_sharded[0], "cast_" + big_names[0])]
    conv_all = _small_gather(conv_pack)
    shard_shapes = [w.shape for w in big_w]
    n_big = len(big_w)

    def gather_start(ks, after, tag):
        send, recv, bufs, token = _exchange_start(
            "gather_start_" + tag, [placed[k] for k in ks], 3 * n_big,
            _gather_plan(shard_shapes, col_sharded, ks), after=after)
        return (send, recv, dict(zip(ks, bufs))), token

    def arrived(state, ks, after, tag):
        send, recv, bufs = state
        got = _exchange_wait("gather_wait_" + tag, [bufs[k] for k in ks], send, recv, after,
                             _gather_plan(shard_shapes, col_sharded, ks))
        return _exchange("gather_forward_" + tag, got, 3 * len(ks), _forward_plan(shard_shapes, col_sharded, ks))

    def arrived_behind(state, ks, after, tag):
        send, recv, bufs = state
        got = _exchange_wait("gather_wait_" + tag, [bufs[k] for k in ks], send, recv, after,
                             _gather_plan(shard_shapes, col_sharded, ks))
        plan = _forward_plan(shard_shapes, col_sharded, ks)
        send, recv, got, token = _exchange_start("gather_forward_start_" + tag, got, 3 * len(ks), plan)
        return (send, recv, got, plan, tag), token

    def forwarded(state, after):
        send, recv, got, plan, tag = state
        return _exchange_wait("gather_forward_wait_" + tag, got, send, recv, after, plan)

    conv_full, off = [], 0
    for w in conv_shards:
        kw, nq = w.shape
        piece = conv_all[:, :kw, off:off + nq]
        conv_full.append(piece.transpose(1, 0, 2).reshape(kw, N_CHIPS * nq))
        off += nq
    lcw, scw, fcw = conv_full

    per_group = max(1, min(heads, 256 // head_dim))
    gc = per_group * head_dim
    wa_bd = _block_diag_groups(lru_wa, per_group)
    wx_bd = _block_diag_groups(lru_wx, per_group)
    tc = _pick(seq, 256, SUBLANE)
    cb_sc = _pick(d_sc, 512)
    cb_ff = _pick(d_ff, 128)
    tc_ff = _pick(seq, 2048, SUBLANE)
    tc_lru = _pick(seq, 1024, SUBLANE)
    tc_merge = _pick(seq, 512, SUBLANE)
    col_sc = 2 * d_lru
    col_gates = 2 * d_lru + 3 * d_sc

    first, token = gather_start([0], conv_all, "in")
    for k in range(1, n_big):
        placed.append(_cast_place(big_w[k], chip_arr, col_sharded[k], "cast_" + big_names[k], dep=token))
        token = placed[-1]
    h1 = _rms_fwd(xs, g_mix, "rms_mix", dep=token)
    (win_b,) = arrived(first, [0], h1, "in")
    rest, token = gather_start([1, 2, 3, 4, 5], win_b, "rest")
    p = _mm(h1, win_b, "nn", F32, name="mm_in", dep=token)
    mix, token = arrived_behind(rest, [1, 2, 3], p, "mix")
    y_lru_pre, hseq = _lru_fwd(p, lcw, lru_conv_b, wa_bd, lru_ba, wx_bd, lru_bx, lru_lambda, d_lru, gc, tc_lru,
                               dep=token)
    y_sc_pre = _sc_fwd(p, scw, col_sc, d_sc, cb_sc, tc)
    wlo_b, wso_b, wo_b = forwarded(mix, y_sc_pre)
    y_lru = _mm(y_lru_pre, wlo_b, "nn", BF16, name="mm_lru_out")
    y_sc = _mm(y_sc_pre, wso_b, "nn", BF16, name="mm_sc_out")
    merged = _merge_fwd(p, y_lru, y_sc, col_gates, tc_merge)
    x2 = _mm(merged, wo_b, "nn", F32, res=xs, name="mm_o")
    (wup_b,) = arrived(rest, [4], x2, "up")
    h2 = _rms_fwd(x2, g_ffn, "rms_ffn")
    up = _mm(h2, wup_b, "nn", F32, name="mm_up")
    down, token = arrived_behind(rest, [5], up, "down")
    act = _ffn_act_fwd(up, fcw, d_ff, cb_ff, tc_ff, dep=token)
    (wdn_b,) = forwarded(down, act)
    x3 = _mm(act, wdn_b, "nn", F32, res=x2, name="mm_down")
    loss_part, dx3, dx3b, dg_final = _loss_head(x3, g_final, target)

    where = jnp.concatenate([chip_arr, core])

    def reduce_start(grads, flags, tag):
        views = [_as3d(g, cs) for g, cs in zip(grads, flags)]
        lands = [lax.empty((v.shape[0], v.shape[1] // 2, v.shape[2]), v.dtype) for v in views]
        send, recv, bufs, token = _exchange_start("grad_pair_start_" + tag, views + lands, len(views),
                                                  _pair_plan(len(views)))
        return (send, recv, bufs, flags, tag), token

    def reduce_mid(state, after):
        send, recv, bufs, flags, tag = state
        m = len(flags)
        bufs = _exchange_wait("grad_pair_wait_" + tag, bufs, send, recv, after, _pair_plan(m))
        partials = [_pair_add(bufs[i], bufs[m + i], core) for i in range(m)]
        lands = []
        for pz, cs in zip(partials, flags):
            _, h, cdim = pz.shape
            lands.append(lax.empty((N_CHIPS - 1, h, cdim // N_CHIPS if cs else cdim), BF16))
        send, recv, bufs, token = _exchange_start("grad_chip_start_" + tag, partials + lands, 3 * m,
                                                  _chip_plan(flags))
        return (send, recv, bufs, flags, tag), token

    def reduce_end(state, after):
        send, recv, bufs, flags, tag = state
        m = len(flags)
        bufs = _exchange_wait("grad_chip_wait_" + tag, bufs, send, recv, after, _chip_plan(flags))
        return [_chip_sum(bufs[i], bufs[m + i], where, flags[i]) for i in range(m)]

    g_wdn = _mm(act, dx3b, "tn", F32, name="mm_down_dw")
    red_down, token = reduce_start([g_wdn], [False], "down")
    dact = _mm(dx3b, wdn_b, "nt", F32, name="mm_down_dx", dep=token)
    red_down, token = reduce_mid(red_down, dact)
    dup, dfcw_g, dfcw_v = _ffn_act_bwd(up, dact, fcw, d_ff, cb_ff, tc_ff, dep=token)
    g_wup = _mm(h2, dup, "tn", BF16, name="mm_up_dw", slabs=[0, 1])
    red_up, token = reduce_start([g_wup], [True], "up")
    dh2 = _mm(dup, wup_b, "nt", F32, name="mm_up_dx", dep=token, slabs=[0, 1])
    red_up, token = reduce_mid(red_up, dh2)
    dx2, dx2b, dg_ffn = _rms_bwd(x2, g_ffn, dh2, dx3, "rms_ffn_bwd", True, dep=token)
    g_wo = _mm(merged, dx2b, "tn", BF16, name="mm_o_dw")
    dmerged = _mm(dx2b, wo_b, "nt", BF16, name="mm_o_dx")
    slab_w = d_lru
    assert d_sc == slab_w and d_model % slab_w == 0 and col_gates % slab_w == 0
    n_gate = d_model // slab_w
    gate0 = col_gates // slab_w
    dp_slabs = [gate0 + kind * n_gate + j for j in range(n_gate) for kind in (0, 1)] + [0, 1, 2, 3, 4]
    dp, dyl, dys = _merge_bwd(p, y_lru, y_sc, dmerged, col_gates, tc_merge, len(dp_slabs))
    assert dp.shape[2] == slab_w
    g_wlo = _mm(y_lru_pre, dyl, "tn", BF16, name="mm_lru_out_dw")
    g_wso = _mm(y_sc_pre, dys, "tn", BF16, name="mm_sc_out_dw")
    red_mix, token = reduce_start([g_wlo, g_wso, g_wo], [True, True, False], "mix")
    dylp = _mm(dyl, wlo_b, "nt", F32, name="mm_lru_out_dx", dep=token)
    dysp = _mm(dys, wso_b, "nt", F32, name="mm_sc_out_dx")
    red_mix, token = reduce_mid(red_mix, dysp)
    dp, dlcw, dlcb, dwa_bd, dba, dwx_bd, dbx, dlam = _lru_bwd(
        p, hseq, dylp, lcw, lru_conv_b, wa_bd, lru_ba, wx_bd, lru_bx, lru_lambda, d_lru, gc, tc_lru,
        dp, 2 * n_gate, dep=token)
    dp, dscw = _sc_bwd(p, dysp, scw, col_sc, d_sc, cb_sc, tc, dp, 2 * n_gate + 2)
    g_win = _mm(h1, dp, "tn", BF16, name="mm_in_dw", slabs=dp_slabs)
    red_in, token = reduce_start([g_win], [True], "in")
    dh1 = _mm(dp, win_b, "nt", F32, name="mm_in_dx", dep=token, slabs=dp_slabs)
    grad_x, dg_mix = _rms_bwd(xs, g_mix, dh1, dx2, "rms_mix_bwd", False)

    small_g = [dg_mix, dlcw, dlcb, _diag_blocks(dwa_bd, per_group, head_dim), dba,
               _diag_blocks(dwx_bd, per_group, head_dim), dbx, dlam, dscw, dg_ffn,
               jnp.concatenate([dfcw_g, dfcw_v], axis=1), dg_final]
    small_shapes = [a.shape for a in small_g]
    small_sum = _small_allreduce(_pack(small_g))
    red_in, token = reduce_mid(red_in, small_sum)
    (h_wdn,) = reduce_end(red_down, token)
    (h_wup,) = reduce_end(red_up, token)
    h_wlo, h_wso, h_wo = reduce_end(red_mix, token)
    s_wlo, s_wso, s_wo, s_wup, s_wdn = _exchange("grad_share_a", [h_wlo, h_wso, h_wo, h_wup, h_wdn], 5,
                                                 _share_plan(5))
    early = {1: s_wlo, 2: s_wso, 3: s_wo, 4: s_wup, 5: s_wdn}
    big_out = [None] * n_big
    last = None
    for k, g in early.items():
        big_out[k] = _adamw(big_w[k], g, big_m[k], big_v[k], "adamw_" + big_names[k], dep=last)
        last = big_out[k][1]
    (h_win,) = reduce_end(red_in, last)
    (s_win,) = _exchange("grad_share_b", [h_win], 1, _share_plan(1))
    big_out[0] = _adamw(big_w[0], s_win, big_m[0], big_v[0], "adamw_" + big_names[0])
    sg = _unpack(small_sum, small_shapes)
    for idx in (1, 8, 10):
        nq = sg[idx].shape[1] // N_CHIPS
        sg[idx] = lax.dynamic_slice_in_dim(sg[idx], chip * nq, nq, axis=1)
    small_w = [g_mix, lru_conv_w, lru_conv_b, lru_wa, lru_ba, lru_wx, lru_bx, lru_lambda, sc_conv_w,
               g_ffn, ffn_conv_w, g_final]
    small_m = [m_g_mix, m_lru_conv_w, m_lru_conv_b, m_lru_wa, m_lru_ba, m_lru_wx, m_lru_bx, m_lru_lambda,
               m_sc_conv_w, m_g_ffn, m_ffn_conv_w, m_g_final]
    small_v = [v_g_mix, v_lru_conv_w, v_lru_conv_b, v_lru_wa, v_lru_ba, v_lru_wx, v_lru_bx, v_lru_lambda,
               v_sc_conv_w, v_g_ffn, v_ffn_conv_w, v_g_final]
    sg = [g.reshape(w.shape) for g, w in zip(sg, small_w)]
    w_shapes = [w.shape for w in small_w]
    packed = _adamw(_pack(small_w), _pack(sg), _pack(small_m), _pack(small_v), "adamw_small")
    small_out = [_unpack(pk, w_shapes) for pk in packed]

    order = [(0, 0), (1, 0), (0, 1), (0, 2), (0, 3), (0, 4), (0, 5), (0, 6), (0, 7), (1, 1), (0, 8), (1, 2),
             (1, 3), (0, 9), (1, 4), (0, 10), (1, 5), (0, 11)]
    by_kind = []
    for kind in range(4):
        by_kind.append([big_out[i][kind] if is_big else small_out[kind][i] for is_big, i in order])
    loss = lax.psum(loss_part[0, 0], ("x", "y", "c"))
    return (loss, grad_x.reshape(x.shape), *by_kind[0], *by_kind[1], *by_kind[2], *by_kind[3])
```

```python
import math

import jax
import jax.numpy as jnp
from jax import lax
from jax.experimental import pallas as pl
from jax.experimental.pallas import tpu as pltpu

F32 = jnp.float32
BF16 = jnp.bfloat16

LANE = 128
SUBLANE = 8
BF16_ROWS = 16
VMEM_BYTES_V7X = 64 * 1024 * 1024
VMEM_BUDGET = VMEM_BYTES_V7X - 8 * 1024 * 1024
MM_VMEM_BUDGET = 42 * 1024 * 1024
STREAM_BLOCK = 4 * 1024 * 1024
EPS = 1e-6
LRU_C = 8.0
ADAM_LR = 0.001
ADAM_B1 = 0.9
ADAM_B2 = 0.999
ADAM_EPS = 1e-08
ADAM_WD = 0.01
ADAM_STEP = 10

N_CHIPS = 4
N_DEV = 8
MESH = pl.DeviceIdType.MESH
ANY = pl.BlockSpec(memory_space=pl.ANY)
VMEM_SPEC = pl.BlockSpec(memory_space=pltpu.VMEM)
HBM_SPEC = pl.BlockSpec(memory_space=pltpu.HBM)
SEM_SPEC = pl.BlockSpec(memory_space=pltpu.SEMAPHORE)
DATAFLOW_EFFECT = pltpu.SideEffectType.DATAFLOW_SIDE_EFFECTING


def _pick(n, cap, mult=LANE):
    best = None
    d = mult
    while d <= min(n, cap):
        if n % d == 0:
            best = d
        d += mult
    return n if best is None else best


def _cparams(semantics, block_bytes):
    limit = min(VMEM_BUDGET, max(32 * 1024 * 1024, int(block_bytes * 1.25) + (4 << 20)))
    return pltpu.CompilerParams(dimension_semantics=semantics, vmem_limit_bytes=limit)


def _nbytes(shape, dtype):
    return math.prod(shape) * jnp.dtype(dtype).itemsize


def _sigmoid(z):
    return 0.5 * jnp.tanh(0.5 * z) + 0.5


def _softplus(z):
    e = jnp.exp(-jnp.abs(z))
    u = 1.0 + e
    log1p = jnp.where(u == 1.0, e, jnp.log(u) * (e / (u - 1.0)))
    return jnp.maximum(z, 0.0) + log1p


def _neg_expm1(z):
    small = z * (1.0 + z * (0.5 + z * (1.0 / 6.0 + z * (1.0 / 24.0))))
    return -jnp.where(jnp.abs(z) < 0.03, small, jnp.exp(z) - 1.0)


_GELU_K = math.sqrt(2.0 / math.pi)
_GELU_C = 0.044715


def _gelu_and_grad(z):
    z2 = z * z
    th = jnp.tanh(_GELU_K * (z + _GELU_C * z2 * z))
    val = 0.5 * z * (1.0 + th)
    grad = 0.5 * (1.0 + th) + 0.5 * z * (1.0 - th * th) * (_GELU_K * (1.0 + 3.0 * _GELU_C * z2))
    return val, grad


def _rows_before(cat, k):
    if k == 0:
        return cat[SUBLANE:, :]
    return pltpu.roll(cat, k, 0)[SUBLANE:, :]


def _rows_after(cat, k):
    n = cat.shape[0]
    if k == 0:
        return cat[:n - SUBLANE, :]
    return pltpu.roll(cat, n - k, 0)[:n - SUBLANE, :]


def _conv_fwd(cat, w, width):
    y = _rows_before(cat, width - 1) * w[0:1, :]
    for k in range(1, width):
        y = y + _rows_before(cat, width - 1 - k) * w[k:k + 1, :]
    return y


def _conv_bwd_input(cat, w, width):
    dx = _rows_after(cat, width - 1) * w[0:1, :]
    for k in range(1, width):
        dx = dx + _rows_after(cat, width - 1 - k) * w[k:k + 1, :]
    return dx


def _conv_bwd_weight(dw_ref, dy, catx, width):
    for k in range(width):
        dw_ref[k:k + 1, :] += jnp.sum(dy * _rows_before(catx, width - 1 - k), axis=0, keepdims=True)


def _scan_tiles(a_ref, b_ref, out_ref, carry0, n_rows, reverse):
    cols = a_ref.shape[1]
    row = lax.broadcasted_iota(jnp.int32, (SUBLANE, cols), 0)
    n_tiles = n_rows // SUBLANE

    def step(j, carry):
        tile = (n_tiles - 1 - j) if reverse else j
        off = pl.multiple_of(tile * SUBLANE, SUBLANE)
        a = a_ref[pl.ds(off, SUBLANE), :]
        b = b_ref[pl.ds(off, SUBLANE), :]
        for s in (1, 2, 4):
            if reverse:
                keep = row < SUBLANE - s
                shift = SUBLANE - s
            else:
                keep = row >= s
                shift = s
            a_sh = jnp.where(keep, pltpu.roll(a, shift, 0), 1.0)
            b_sh = jnp.where(keep, pltpu.roll(b, shift, 0), 0.0)
            b = a * b_sh + b
            a = a * a_sh
        out = a * carry + b
        out_ref[pl.ds(off, SUBLANE), :] = out
        return out[0:1, :] if reverse else out[SUBLANE - 1:SUBLANE, :]

    return lax.fori_loop(0, n_tiles, step, carry0)


def _dep_args(body, in_specs, operands, *deps):
    deps = [d for d in deps if d is not None]
    if not deps:
        return body, in_specs, operands
    n = len(operands)

    def wrapped(*refs):
        return body(*refs[:n], *refs[n + len(deps):])

    return wrapped, list(in_specs) + [ANY] * len(deps), list(operands) + deps


def _mm(a, b, mode, out_dtype, res=None, name=None, dep=None, slabs=None):
    assert a.dtype == BF16 and b.dtype == BF16
    a_slabbed, b_slabbed = a.ndim == 3, b.ndim == 3
    assert not a_slabbed or (mode == "nt" and slabs is not None)
    assert not b_slabbed or (mode == "tn" and slabs is not None)
    if mode == "nn":
        (m, k), (k2, n) = a.shape, b.shape
        dims = (((1,), (0,)), ((), ()))
    elif mode == "nt":
        m, k = (a.shape[1], a.shape[0] * a.shape[2]) if a_slabbed else a.shape
        n, k2 = b.shape
        dims = (((1,), (1,)), ((), ()))
    else:
        k, m = a.shape
        k2, n = (b.shape[1], b.shape[0] * b.shape[2]) if b_slabbed else b.shape
        dims = (((0,), (0,)), ((), ()))
    assert k == k2
    n_unit = b.shape[2] if b_slabbed else n
    out_bytes = jnp.dtype(out_dtype).itemsize
    bm = _pick(m, 1024)
    bn = _pick(n_unit, 1024)
    bk = k

    def est(bm_, bn_, bk_):
        e = 2 * (bm_ * bk_ + bk_ * bn_) * 2 + 2 * bm_ * bn_ * out_bytes
        if k // bk_ > 1:
            e += bm_ * bn_ * 4
        if res is not None:
            e += 2 * bm_ * bn_ * 4
        return e

    for shrink_n, floor in ((True, 512), (False, 512), (True, 256), (False, 256)):
        while est(bm, bn, bk) > MM_VMEM_BUDGET:
            if shrink_n and bn > floor and bn % 2 == 0 and n_unit % (bn // 2) == 0:
                bn //= 2
            elif not shrink_n and bm > floor and bm % 2 == 0 and m % (bm // 2) == 0:
                bm //= 2
            else:
                break
    while (est(bm, bn, bk) > MM_VMEM_BUDGET and not a_slabbed and bk % (2 * LANE) == 0
           and k % (bk // 2) == 0):
        bk //= 2
    nk = k // bk
    per_slab = n_unit // bn

    def out_col(j):
        if not b_slabbed:
            return j
        s = j // per_slab
        where = sum(jnp.where(s == t, slabs[t], 0) for t in range(len(slabs)))
        return where * per_slab + j % per_slab

    if mode == "tn":
        a_spec = pl.BlockSpec((bk, bm), lambda i, j, kk: (kk, i))
    elif a_slabbed:
        a_spec = pl.BlockSpec((a.shape[0], bm, a.shape[2]), lambda i, j, kk: (0, i, 0))
    else:
        a_spec = pl.BlockSpec((bm, bk), lambda i, j, kk: (i, kk))
    if mode == "nt":
        b_spec = pl.BlockSpec((bn, bk), lambda i, j, kk: (j, kk))
    elif b_slabbed:
        b_spec = pl.BlockSpec((None, bk, bn), lambda i, j, kk: (j // per_slab, kk, j % per_slab))
    else:
        b_spec = pl.BlockSpec((bk, bn), lambda i, j, kk: (kk, j))
    o_spec = pl.BlockSpec((bm, bn), lambda i, j, kk: (i, out_col(j)))
    in_specs = [a_spec, b_spec]
    operands = [a, b]
    if res is not None:
        in_specs.append(o_spec)
        operands.append(res)
    has_res = res is not None

    def body(*refs):
        a_ref, b_ref = refs[0], refs[1]
        res_ref = refs[2] if has_res else None
        o_ref = refs[2 + has_res]
        if a_slabbed:
            width = a_ref.shape[2]
            part = None
            for s, col in enumerate(slabs):
                term = lax.dot_general(a_ref[s], b_ref[:, col * width:(col + 1) * width], dims,
                                       preferred_element_type=F32)
                part = term if part is None else part + term
        else:
            part = lax.dot_general(a_ref[...], b_ref[...], dims, preferred_element_type=F32)
        if nk == 1:
            if has_res:
                part = part + res_ref[...]
            o_ref[...] = part.astype(o_ref.dtype)
            return
        acc_ref = refs[-1]
        kk = pl.program_id(2)

        @pl.when(kk == 0)
        def _():
            acc_ref[...] = part

        @pl.when(kk > 0)
        def _():
            acc_ref[...] += part

        @pl.when(kk == nk - 1)
        def _():
            total = acc_ref[...]
            if has_res:
                total = total + res_ref[...]
            o_ref[...] = total.astype(o_ref.dtype)

    scratch = [pltpu.VMEM((bm, bn), F32)] if nk > 1 else []
    body, in_specs, operands = _dep_args(body, in_specs, operands, dep)
    return pl.pallas_call(
        body,
        out_shape=jax.ShapeDtypeStruct((m, n), out_dtype),
        grid=(m // bm, n // bn, nk),
        in_specs=in_specs,
        out_specs=o_spec,
        scratch_shapes=scratch,
        compiler_params=_cparams(("parallel", "parallel", "arbitrary"), est(bm, bn, bk)),
        name=name,
    )(*operands)


def _rms_fwd(x, g, name, dep=None):
    t, d = x.shape
    tb = _pick(t, 512, SUBLANE)

    def body(x_ref, g_ref, h_ref):
        xv = x_ref[...]
        r = lax.rsqrt(jnp.mean(xv * xv, axis=-1, keepdims=True) + EPS)
        h_ref[...] = ((xv * r) * g_ref[...]).astype(BF16)

    blk = pl.BlockSpec((tb, d), lambda i: (i, 0))
    body, in_specs, operands = _dep_args(
        body, [blk, pl.BlockSpec((1, d), lambda i: (0, 0))], [x, g.reshape(1, d)], dep)
    return pl.pallas_call(
        body,
        out_shape=jax.ShapeDtypeStruct((t, d), BF16),
        grid=(t // tb,),
        in_specs=in_specs,
        out_specs=blk,
        compiler_params=_cparams(("parallel",), 2 * tb * d * 6),
        name=name,
    )(*operands)


def _rms_bwd(x, g, dh, dres, name, want_bf16, dep=None):
    t, d = x.shape
    tb = _pick(t, 256, SUBLANE)

    def body(x_ref, g_ref, dh_ref, dres_ref, *outs):
        dx_ref, dg_ref = outs[0], outs[-1]
        xv = x_ref[...]
        r = lax.rsqrt(jnp.mean(xv * xv, axis=-1, keepdims=True) + EPS)
        xhat = xv * r
        dhv = dh_ref[...]
        dxhat = dhv * g_ref[...]
        dx = dres_ref[...] + r * (dxhat - xhat * jnp.mean(dxhat * xhat, axis=-1, keepdims=True))
        dx_ref[...] = dx
        if want_bf16:
            outs[1][...] = dx.astype(BF16)

        @pl.when(pl.program_id(0) == 0)
        def _():
            dg_ref[...] = jnp.zeros_like(dg_ref)

        dg_ref[...] += jnp.sum(dhv * xhat, axis=0, keepdims=True)

    blk = pl.BlockSpec((tb, d), lambda i: (i, 0))
    row = pl.BlockSpec((1, d), lambda i: (0, 0))
    out_shape = [jax.ShapeDtypeStruct((t, d), F32)]
    out_specs = [blk]
    if want_bf16:
        out_shape.append(jax.ShapeDtypeStruct((t, d), BF16))
        out_specs.append(blk)
    out_shape.append(jax.ShapeDtypeStruct((1, d), F32))
    out_specs.append(row)
    body, in_specs, operands = _dep_args(
        body, [blk, row, blk, blk], [x, g.reshape(1, d), dh, dres], dep)
    return pl.pallas_call(
        body,
        out_shape=out_shape,
        grid=(t // tb,),
        in_specs=in_specs,
        out_specs=out_specs,
        compiler_params=_cparams(("arbitrary",), 2 * tb * d * 18),
        name=name,
    )(*operands)


def _loss_head(x3, g, target):
    t, d = x3.shape
    tb = _pick(t, 256, SUBLANE)

    def body(x_ref, g_ref, t_ref, loss_ref, dx_ref, dxb_ref, dg_ref):
        xv = x_ref[...]
        gv = g_ref[...]
        r = lax.rsqrt(jnp.mean(xv * xv, axis=-1, keepdims=True) + EPS)
        xhat = xv * r
        err = xhat * gv - t_ref[...]
        dy = err * (1.0 / d)
        dxhat = dy * gv
        dx = r * (dxhat - xhat * jnp.mean(dxhat * xhat, axis=-1, keepdims=True))
        dx_ref[...] = dx
        dxb_ref[...] = dx.astype(BF16)

        @pl.when(pl.program_id(0) == 0)
        def _():
            dg_ref[...] = jnp.zeros_like(dg_ref)
            loss_ref[...] = jnp.zeros_like(loss_ref)

        dg_ref[...] += jnp.sum(dy * xhat, axis=0, keepdims=True)
        per_token = jnp.mean(err * err, axis=-1, keepdims=True)
        loss_ref[...] += 0.5 * jnp.sum(per_token, axis=0, keepdims=True)

    blk = pl.BlockSpec((tb, d), lambda i: (i, 0))
    row = pl.BlockSpec((1, d), lambda i: (0, 0))
    return pl.pallas_call(
        body,
        out_shape=[jax.ShapeDtypeStruct((1, 1), F32), jax.ShapeDtypeStruct((t, d), F32),
                   jax.ShapeDtypeStruct((t, d), BF16), jax.ShapeDtypeStruct((1, d), F32)],
        grid=(t // tb,),
        in_specs=[blk, row, blk],
        out_specs=[pl.BlockSpec((1, 1), lambda i: (0, 0)), blk, blk, row],
        compiler_params=_cparams(("arbitrary",), 2 * tb * d * 14),
        name="loss_head",
    )(x3, g.reshape(1, d), target)


def _lru_gates(xc, wa, ba, wx, bx, lam):
    nn = (((1,), (0,)), ((), ()))
    xcb = xc.astype(BF16)
    r = _sigmoid(lax.dot_general(xcb, wa, nn, preferred_element_type=F32) + ba)
    i = _sigmoid(lax.dot_general(xcb, wx, nn, preferred_element_type=F32) + bx)
    cl = -LRU_C * _softplus(-lam)
    log_a = cl * r
    a = jnp.exp(log_a)
    one_minus_a2 = _neg_expm1(2.0 * log_a)
    return xcb, r, i, a, one_minus_a2, cl


def _lru_fwd(p, conv_w, conv_b, wa_bd, ba, wx_bd, bx, lam, d_lru, gc, tc, dep=None):
    t = p.shape[0]
    ng = d_lru // gc
    nt = t // tc
    width = conv_w.shape[0]

    def body(lx_ref, gate_ref, cw_ref, cb_ref, wa_ref, ba_ref, wx_ref, bx_ref, lam_ref,
             y_ref, h_ref, halo, hcar, a_s, u_s):
        @pl.when(pl.program_id(1) == 0)
        def _():
            halo[...] = jnp.zeros_like(halo)
            hcar[...] = jnp.zeros_like(hcar)

        x = lx_ref[...]
        cat = jnp.concatenate([halo[...], x], axis=0)
        halo[...] = x[tc - SUBLANE:, :]
        xc = _conv_fwd(cat, cw_ref[...], width) + cb_ref[...]
        _, r, i, a, om, _ = _lru_gates(xc, wa_ref[...], ba_ref[...], wx_ref[...], bx_ref[...], lam_ref[...])
        a_s[...] = a
        u_s[...] = jnp.sqrt(om) * (i * xc)
        hcar[0:1, :] = _scan_tiles(a_s, u_s, h_ref, hcar[0:1, :], tc, reverse=False)
        gl, _ = _gelu_and_grad(gate_ref[...])
        y_ref[...] = (gl * h_ref[...]).astype(BF16)

    blk = lambda off: pl.BlockSpec((tc, gc), lambda g, s, off=off: (s, off + g))
    rowv = lambda rows: pl.BlockSpec((rows, gc), lambda g, s: (0, g))
    wspec = pl.BlockSpec((None, gc, gc), lambda g, s: (g, 0, 0))
    out_blk = pl.BlockSpec((tc, gc), lambda g, s: (s, g))
    body, in_specs, operands = _dep_args(
        body, [blk(0), blk(ng), rowv(width), rowv(1), wspec, rowv(1), wspec, rowv(1), rowv(1)],
        [p, p, conv_w, conv_b.reshape(1, -1), wa_bd, ba.reshape(1, -1), wx_bd, bx.reshape(1, -1),
         lam.reshape(1, -1)], dep)
    return pl.pallas_call(
        body,
        out_shape=[jax.ShapeDtypeStruct((t, d_lru), BF16), jax.ShapeDtypeStruct((t, d_lru), F32)],
        grid=(ng, nt),
        in_specs=in_specs,
        out_specs=[out_blk, out_blk],
        scratch_shapes=[pltpu.VMEM((SUBLANE, gc), F32), pltpu.VMEM((SUBLANE, gc), F32),
                        pltpu.VMEM((tc, gc), F32), pltpu.VMEM((tc, gc), F32)],
        compiler_params=_cparams(("parallel", "arbitrary"), 40 * tc * gc * 4),
        name="lru_fwd",
    )(*operands)


def _lru_bwd(p, hseq, dyp, conv_w, conv_b, wa_bd, ba, wx_bd, bx, lam, d_lru, gc, tc, dp, slab0, dep=None):
    t = p.shape[0]
    ng = d_lru // gc
    nt = t // tc
    width = conv_w.shape[0]
    halo_blocks = tc // SUBLANE
    nn = (((1,), (0,)), ((), ()))
    nt_dims = (((1,), (1,)), ((), ()))
    tn_dims = (((0,), (0,)), ((), ()))

    def body(lx_ref, lxh_ref, gate_ref, h_ref, hh_ref, dyp_ref,
             cw_ref, cb_ref, wa_ref, ba_ref, wx_ref, bx_ref, lam_ref,
             dp_ref, dcw_ref, dcb_ref, dwa_ref, dba_ref, dwx_ref, dbx_ref, dlam_ref,
             nxt_dxc, nxt_a, nxt_g, al_s, b_s, g_s):
        s = pl.program_id(1)
        first_chunk = s == nt - 1

        @pl.when(s == 0)
        def _():
            nxt_dxc[...] = jnp.zeros_like(nxt_dxc)
            nxt_a[...] = jnp.zeros_like(nxt_a)
            nxt_g[...] = jnp.zeros_like(nxt_g)
            for ref in (dcw_ref, dcb_ref, dwa_ref, dba_ref, dwx_ref, dbx_ref, dlam_ref):
                ref[...] = jnp.zeros_like(ref)

        keep = jnp.where(first_chunk, 0.0, 1.0)
        x = lx_ref[...]
        catx = jnp.concatenate([lxh_ref[...] * keep, x], axis=0)
        cw = cw_ref[...]
        xc = _conv_fwd(catx, cw, width) + cb_ref[...]
        wa = wa_ref[...]
        wx = wx_ref[...]
        lam_v = lam_ref[...]
        xcb, r, i, a, om, cl = _lru_gates(xc, wa, ba_ref[...], wx, bx_ref[...], lam_v)
        mult = jnp.sqrt(om)

        h = h_ref[...]
        hprev = _rows_before(jnp.concatenate([hh_ref[...] * keep, h], axis=0), 1)
        gl, dgl = _gelu_and_grad(gate_ref[...])
        dyp_v = dyp_ref[...]
        dp_ref[1] = (dyp_v * h * dgl).astype(BF16)

        al_s[...] = _rows_after(jnp.concatenate([a, nxt_a[...]], axis=0), 1)
        b_s[...] = dyp_v * gl
        nxt_g[0:1, :] = _scan_tiles(al_s, b_s, g_s, nxt_g[0:1, :], tc, reverse=True)
        nxt_a[...] = a[0:SUBLANE, :]
        du = g_s[...]

        da = du * hprev
        dmult = du * (i * xc)
        di = du * mult * xc
        dxc = du * mult * i
        dlog_a = da * a - dmult * (a * a / mult)
        dlam_ref[...] += jnp.sum(dlog_a * r, axis=0, keepdims=True) * (LRU_C * _sigmoid(-lam_v))
        dza = (dlog_a * cl) * r * (1.0 - r)
        dzx = di * i * (1.0 - i)
        dba_ref[...] += jnp.sum(dza, axis=0, keepdims=True)
        dbx_ref[...] += jnp.sum(dzx, axis=0, keepdims=True)
        dzab = dza.astype(BF16)
        dzxb = dzx.astype(BF16)
        dwa_ref[...] += lax.dot_general(xcb, dzab, tn_dims, preferred_element_type=F32)
        dwx_ref[...] += lax.dot_general(xcb, dzxb, tn_dims, preferred_element_type=F32)
        dxc = dxc + lax.dot_general(dzab, wa, nt_dims, preferred_element_type=F32)
        dxc = dxc + lax.dot_general(dzxb, wx, nt_dims, preferred_element_type=F32)
        dcb_ref[...] += jnp.sum(dxc, axis=0, keepdims=True)
        _conv_bwd_weight(dcw_ref, dxc, catx, width)
        catd = jnp.concatenate([dxc, nxt_dxc[...]], axis=0)
        dp_ref[0] = _conv_bwd_input(catd, cw, width).astype(BF16)
        nxt_dxc[...] = dxc[0:SUBLANE, :]

    rev = lambda s: nt - 1 - s
    blk = lambda off: pl.BlockSpec((tc, gc), lambda g, s, off=off: (rev(s), off + g))
    halo = lambda off: pl.BlockSpec(
        (SUBLANE, gc), lambda g, s, off=off: (jnp.maximum(rev(s) * halo_blocks - 1, 0), off + g))
    rowv = lambda rows: pl.BlockSpec((rows, gc), lambda g, s: (0, g))
    wspec = pl.BlockSpec((None, gc, gc), lambda g, s: (g, 0, 0))
    out_blk = pl.BlockSpec((tc, gc), lambda g, s: (rev(s), g))
    vec = lambda rows: jax.ShapeDtypeStruct((rows, d_lru), F32)
    wshape = jax.ShapeDtypeStruct((ng, gc, gc), F32)
    body, in_specs, operands = _dep_args(
        body,
        [blk(0), halo(0), blk(ng), blk(0), halo(0), blk(0),
         rowv(width), rowv(1), wspec, rowv(1), wspec, rowv(1), rowv(1)],
        [p, p, p, hseq, hseq, dyp,
         conv_w, conv_b.reshape(1, -1), wa_bd, ba.reshape(1, -1), wx_bd,
         bx.reshape(1, -1), lam.reshape(1, -1)], dp, dep)
    assert dp.shape[2] == d_lru and slab0 % 2 == 0
    return pl.pallas_call(
        body,
        out_shape=[jax.ShapeDtypeStruct(dp.shape, dp.dtype),
                   vec(width), vec(1), wshape, vec(1), wshape, vec(1), vec(1)],
        grid=(ng, nt),
        in_specs=in_specs,
        out_specs=[pl.BlockSpec((2, tc, gc), lambda g, s: (slab0 // 2, rev(s), g)),
                   rowv(width), rowv(1), wspec, rowv(1), wspec, rowv(1), rowv(1)],
        input_output_aliases={13: 0},
        scratch_shapes=[pltpu.VMEM((SUBLANE, gc), F32), pltpu.VMEM((SUBLANE, gc), F32),
                        pltpu.VMEM((SUBLANE, gc), F32),
                        pltpu.VMEM((tc, gc), F32), pltpu.VMEM((tc, gc), F32), pltpu.VMEM((tc, gc), F32)],
        compiler_params=_cparams(("parallel", "arbitrary"), 80 * tc * gc * 4),
        name="lru_bwd",
    )(*operands)


def _sc_fwd(p, conv_w, col0, d_sc, cb, tc):
    t = p.shape[0]
    nc = d_sc // cb
    nt = t // tc
    width = conv_w.shape[0]
    base = col0 // cb

    def body(b_ref, c_ref, v_ref, w_ref, y_ref, halo):
        @pl.when(pl.program_id(1) == 0)
        def _():
            halo[...] = jnp.zeros_like(halo)

        cv = c_ref[...] * v_ref[...]
        cat = jnp.concatenate([halo[...], cv], axis=0)
        halo[...] = cv[tc - SUBLANE:, :]
        y_ref[...] = (b_ref[...] * _conv_fwd(cat, w_ref[...], width)).astype(BF16)

    blk = lambda slab: pl.BlockSpec((tc, cb), lambda j, s, slab=slab: (s, base + slab * nc + j))
    return pl.pallas_call(
        body,
        out_shape=jax.ShapeDtypeStruct((t, d_sc), BF16),
        grid=(nc, nt),
        in_specs=[blk(0), blk(1), blk(2), pl.BlockSpec((width, cb), lambda j, s: (0, j))],
        out_specs=pl.BlockSpec((tc, cb), lambda j, s: (s, j)),
        scratch_shapes=[pltpu.VMEM((SUBLANE, cb), F32)],
        compiler_params=_cparams(("parallel", "arbitrary"), 20 * tc * cb * 4),
        name="sc_fwd",
    )(p, p, p, conv_w)


def _sc_bwd(p, dyp, conv_w, col0, d_sc, cb, tc, dp, slab0):
    t = p.shape[0]
    nc = d_sc // cb
    nt = t // tc
    width = conv_w.shape[0]
    base = col0 // cb
    halo_blocks = tc // SUBLANE

    def body(b_ref, c_ref, ch_ref, v_ref, vh_ref, dyp_ref, w_ref,
             dp_ref, dw_ref, nxt_dq):
        s = pl.program_id(1)

        @pl.when(s == 0)
        def _():
            nxt_dq[...] = jnp.zeros_like(nxt_dq)
            dw_ref[...] = jnp.zeros_like(dw_ref)

        keep = jnp.where(s == nt - 1, 0.0, 1.0)
        cvals = c_ref[...]
        vvals = v_ref[...]
        w = w_ref[...]
        catcv = jnp.concatenate([ch_ref[...] * vh_ref[...] * keep, cvals * vvals], axis=0)
        q = _conv_fwd(catcv, w, width)
        dyp_v = dyp_ref[...]
        dp_ref[0] = (dyp_v * q).astype(BF16)
        dq = dyp_v * b_ref[...]
        _conv_bwd_weight(dw_ref, dq, catcv, width)
        dcv = _conv_bwd_input(jnp.concatenate([dq, nxt_dq[...]], axis=0), w, width)
        nxt_dq[...] = dq[0:SUBLANE, :]
        dp_ref[1] = (dcv * vvals).astype(BF16)
        dp_ref[2] = (dcv * cvals).astype(BF16)

    rev = lambda s: nt - 1 - s
    blk = lambda slab: pl.BlockSpec((tc, cb), lambda j, s, slab=slab: (rev(s), base + slab * nc + j))
    halo = lambda slab: pl.BlockSpec(
        (SUBLANE, cb),
        lambda j, s, slab=slab: (jnp.maximum(rev(s) * halo_blocks - 1, 0), base + slab * nc + j))
    out_blk = pl.BlockSpec((tc, cb), lambda j, s: (rev(s), j))
    wblk = pl.BlockSpec((width, cb), lambda j, s: (0, j))
    assert dp.shape[2] == d_sc and slab0 % 3 == 0
    operands = [p, p, p, p, p, dyp, conv_w]
    body, in_specs, operands = _dep_args(
        body, [blk(0), blk(1), halo(1), blk(2), halo(2), out_blk, wblk], operands, dp)
    return pl.pallas_call(
        body,
        out_shape=[jax.ShapeDtypeStruct(dp.shape, dp.dtype), jax.ShapeDtypeStruct((width, d_sc), F32)],
        grid=(nc, nt),
        in_specs=in_specs,
        out_specs=[pl.BlockSpec((3, tc, cb), lambda j, s: (slab0 // 3, rev(s), j)), wblk],
        input_output_aliases={7: 0},
        scratch_shapes=[pltpu.VMEM((SUBLANE, cb), F32)],
        compiler_params=_cparams(("parallel", "arbitrary"), 30 * tc * cb * 4),
        name="sc_bwd",
    )(*operands)


def _merge_fwd(p, y_lru, y_sc, col0, tc):
    t, d = y_lru.shape
    cb = _pick(math.gcd(d, col0), 1024)
    nc = d // cb
    base = col0 // cb

    def body(gl_ref, gs_ref, yl_ref, ys_ref, o_ref):
        o_ref[...] = (_sigmoid(gl_ref[...]) * yl_ref[...] + _sigmoid(gs_ref[...]) * ys_ref[...]).astype(BF16)

    gate = lambda slab: pl.BlockSpec((tc, cb), lambda s, j, slab=slab: (s, base + slab * nc + j))
    blk = pl.BlockSpec((tc, cb), lambda s, j: (s, j))
    return pl.pallas_call(
        body,
        out_shape=jax.ShapeDtypeStruct((t, d), BF16),
        grid=(t // tc, nc),
        in_specs=[gate(0), gate(1), blk, blk],
        out_specs=blk,
        compiler_params=_cparams(("parallel", "parallel"), 2 * tc * cb * 20),
        name="merge_fwd",
    )(p, p, y_lru, y_sc)


def _merge_bwd(p, y_lru, y_sc, dmerged, col0, tc, n_slabs):
    t, d = y_lru.shape
    cb = _pick(math.gcd(d, col0), 1024)
    nc = d // cb
    base = col0 // cb

    def body(gl_ref, gs_ref, yl_ref, ys_ref, dm_ref, dp_ref, dyl_ref, dys_ref):
        dm = dm_ref[...]
        sl = _sigmoid(gl_ref[...])
        ss = _sigmoid(gs_ref[...])
        dp_ref[0] = (dm * yl_ref[...] * (sl * (1.0 - sl))).astype(BF16)
        dp_ref[1] = (dm * ys_ref[...] * (ss * (1.0 - ss))).astype(BF16)
        dyl_ref[...] = (dm * sl).astype(BF16)
        dys_ref[...] = (dm * ss).astype(BF16)

    gate = lambda slab: pl.BlockSpec((tc, cb), lambda s, j, slab=slab: (s, base + slab * nc + j))
    blk = pl.BlockSpec((tc, cb), lambda s, j: (s, j))
    act = jax.ShapeDtypeStruct((t, d), BF16)
    return pl.pallas_call(
        body,
        out_shape=[jax.ShapeDtypeStruct((n_slabs, t, cb), BF16), act, act],
        grid=(t // tc, nc),
        in_specs=[gate(0), gate(1), blk, blk, blk],
        out_specs=[pl.BlockSpec((2, tc, cb), lambda s, j: (j, s, 0)), blk, blk],
        compiler_params=_cparams(("parallel", "parallel"), 2 * tc * cb * 28),
        name="merge_bwd",
    )(p, p, y_lru, y_sc, dmerged)


def _ffn_act_fwd(up, conv_w, d_ff, cb, tc, dep=None):
    t = up.shape[0]
    nc = d_ff // cb
    nt = t // tc
    width = conv_w.shape[0]

    def body(g_ref, v_ref, wg_ref, wv_ref, o_ref, halo_g, halo_v):
        @pl.when(pl.program_id(1) == 0)
        def _():
            halo_g[...] = jnp.zeros_like(halo_g)
            halo_v[...] = jnp.zeros_like(halo_v)

        g = g_ref[...]
        v = v_ref[...]
        ug = _conv_fwd(jnp.concatenate([halo_g[...], g], axis=0), wg_ref[...], width)
        uv = _conv_fwd(jnp.concatenate([halo_v[...], v], axis=0), wv_ref[...], width)
        halo_g[...] = g[tc - SUBLANE:, :]
        halo_v[...] = v[tc - SUBLANE:, :]
        o_ref[...] = (ug * _sigmoid(ug) * uv).astype(BF16)

    blk = lambda half: pl.BlockSpec((tc, cb), lambda j, s, half=half: (s, half * nc + j))
    wblk = lambda half: pl.BlockSpec((width, cb), lambda j, s, half=half: (0, half * nc + j))
    body, in_specs, operands = _dep_args(
        body, [blk(0), blk(1), wblk(0), wblk(1)], [up, up, conv_w, conv_w], dep)
    return pl.pallas_call(
        body,
        out_shape=jax.ShapeDtypeStruct((t, d_ff), BF16),
        grid=(nc, nt),
        in_specs=in_specs,
        out_specs=pl.BlockSpec((tc, cb), lambda j, s: (s, j)),
        scratch_shapes=[pltpu.VMEM((SUBLANE, cb), F32), pltpu.VMEM((SUBLANE, cb), F32)],
        compiler_params=_cparams(("parallel", "arbitrary"), 24 * tc * cb * 4),
        name="ffn_act_fwd",
    )(*operands)


def _ffn_act_bwd(up, dact, conv_w, d_ff, cb, tc, dep=None):
    t = up.shape[0]
    nc = d_ff // cb
    nt = t // tc
    width = conv_w.shape[0]
    halo_blocks = tc // SUBLANE

    def body(g_ref, gh_ref, v_ref, vh_ref, da_ref, wg_ref, wv_ref,
             dup_ref, dwg_ref, dwv_ref, nxt_g, nxt_v):
        s = pl.program_id(1)

        @pl.when(s == 0)
        def _():
            nxt_g[...] = jnp.zeros_like(nxt_g)
            nxt_v[...] = jnp.zeros_like(nxt_v)
            dwg_ref[...] = jnp.zeros_like(dwg_ref)
            dwv_ref[...] = jnp.zeros_like(dwv_ref)

        keep = jnp.where(s == nt - 1, 0.0, 1.0)
        wg = wg_ref[...]
        wv = wv_ref[...]
        catg = jnp.concatenate([gh_ref[...] * keep, g_ref[...]], axis=0)
        catv = jnp.concatenate([vh_ref[...] * keep, v_ref[...]], axis=0)
        ug = _conv_fwd(catg, wg, width)
        uv = _conv_fwd(catv, wv, width)
        sg = _sigmoid(ug)
        da = da_ref[...]
        duv = da * (ug * sg)
        dup_ref[1] = _conv_bwd_input(jnp.concatenate([duv, nxt_v[...]], axis=0), wv, width).astype(BF16)
        nxt_v[...] = duv[0:SUBLANE, :]
        _conv_bwd_weight(dwv_ref, duv, catv, width)
        dug = da * uv * (sg * (1.0 + ug * (1.0 - sg)))
        dup_ref[0] = _conv_bwd_input(jnp.concatenate([dug, nxt_g[...]], axis=0), wg, width).astype(BF16)
        nxt_g[...] = dug[0:SUBLANE, :]
        _conv_bwd_weight(dwg_ref, dug, catg, width)

    rev = lambda s: nt - 1 - s
    blk = lambda half: pl.BlockSpec((tc, cb), lambda j, s, half=half: (rev(s), half * nc + j))
    halo = lambda half: pl.BlockSpec(
        (SUBLANE, cb), lambda j, s, half=half: (jnp.maximum(rev(s) * halo_blocks - 1, 0), half * nc + j))
    wblk = lambda half: pl.BlockSpec((width, cb), lambda j, s, half=half: (0, half * nc + j))
    out_blk = pl.BlockSpec((tc, cb), lambda j, s: (rev(s), j))
    wout = pl.BlockSpec((width, cb), lambda j, s: (0, j))
    act = jax.ShapeDtypeStruct((t, d_ff), BF16)
    wshape = jax.ShapeDtypeStruct((width, d_ff), F32)
    body, in_specs, operands = _dep_args(
        body, [blk(0), halo(0), blk(1), halo(1), out_blk, wblk(0), wblk(1)],
        [up, up, up, up, dact, conv_w, conv_w], dep)
    return pl.pallas_call(
        body,
        out_shape=[jax.ShapeDtypeStruct((2, t, d_ff), BF16), wshape, wshape],
        grid=(nc, nt),
        in_specs=in_specs,
        out_specs=[pl.BlockSpec((2, tc, cb), lambda j, s: (0, rev(s), j)), wout, wout],
        scratch_shapes=[pltpu.VMEM((SUBLANE, cb), F32), pltpu.VMEM((SUBLANE, cb), F32)],
        compiler_params=_cparams(("parallel", "arbitrary"), 40 * tc * cb * 4),
        name="ffn_act_bwd",
    )(*operands)


def _mesh_pos():
    x, y, c = lax.axis_index("x"), lax.axis_index("y"), lax.axis_index("c")
    return x, y, c


def _other_chips(x, y):
    return [(1 - x, y), (x, 1 - y), (1 - x, 1 - y)]


def _cast_place(w, chip, col_sharded, name, dep=None):
    r, cdim = w.shape
    full = (r, cdim * N_CHIPS) if col_sharded else (r * N_CHIPS, cdim)
    rb = _pick(r, max(BF16_ROWS, STREAM_BLOCK // cdim), BF16_ROWS)
    nb = r // rb

    def body(chip_ref, w_ref, o_ref):
        o_ref[...] = w_ref[...].astype(BF16)

    if col_sharded:
        out_map = lambda i, chip_ref: (i, chip_ref[0])
    else:
        out_map = lambda i, chip_ref: (chip_ref[0] * nb + i, 0)
    grid_spec = pltpu.PrefetchScalarGridSpec(
        num_scalar_prefetch=1,
        grid=(nb,),
        in_specs=[pl.BlockSpec((rb, cdim), lambda i, chip_ref: (i, 0))] + ([ANY] if dep is not None else []),
        out_specs=pl.BlockSpec((rb, cdim), out_map),
    )
    body, _, operands = _dep_args(body, [], [chip, w], dep)
    return pl.pallas_call(
        body,
        out_shape=jax.ShapeDtypeStruct(full, BF16),
        grid_spec=grid_spec,
        compiler_params=_cparams(("parallel",), 2 * rb * cdim * 6),
        name=name,
    )(*operands)


def _remote(src, dst, send_sems, recv_sems, idx, to):
    return pltpu.make_async_remote_copy(
        src_ref=src, dst_ref=dst, send_sem=send_sems.at[idx], recv_sem=recv_sems.at[idx],
        device_id=to, device_id_type=MESH)


def _exchange(name, arrays, n_sems, plan):
    n = len(arrays)

    def body(*refs):
        bufs = refs[n:2 * n]
        send_sems, recv_sems = refs[2 * n:]
        sends, arrivals = plan(bufs, send_sems, recv_sems)
        for cp in sends:
            cp.start()
        for cp in arrivals:
            cp.wait_recv()
        for cp in sends:
            cp.wait_send()

    outs = pl.pallas_call(
        body,
        out_shape=[jax.ShapeDtypeStruct(a.shape, a.dtype) for a in arrays],
        in_specs=[ANY] * n,
        out_specs=[ANY] * n,
        input_output_aliases={k: k for k in range(n)},
        scratch_shapes=[pltpu.SemaphoreType.DMA((n_sems,)), pltpu.SemaphoreType.DMA((n_sems,))],
        name=name,
    )(*arrays)
    return list(outs)


def _exchange_start(name, arrays, n_sems, plan, after=None):
    n = len(arrays)
    n_in = n + (after is not None)

    def body(*refs):
        bufs = refs[:n]
        send_sems, recv_sems = refs[n_in], refs[n_in + 1]
        token = refs[-1]
        sends, _ = plan(bufs, send_sems, recv_sems)
        for cp in sends:
            cp.start()
        token[...] = jnp.zeros_like(token)

    out = pl.pallas_call(
        body,
        out_shape=(pltpu.SemaphoreType.DMA((n_sems,)), pltpu.SemaphoreType.DMA((n_sems,)),
                   *[pltpu.HBM(a.shape, a.dtype) for a in arrays],
                   jax.ShapeDtypeStruct((SUBLANE, LANE), F32)),
        in_specs=[HBM_SPEC] * n + [ANY] * (n_in - n),
        out_specs=(SEM_SPEC, SEM_SPEC, *[HBM_SPEC] * n, VMEM_SPEC),
        input_output_aliases={k: 2 + k for k in range(n)},
        compiler_params=pltpu.CompilerParams(has_side_effects=DATAFLOW_EFFECT),
        name=name,
    )(*[pltpu.with_memory_space_constraint(a, pltpu.HBM) for a in arrays], *([after] if after is not None else []))
    return out[0], out[1], list(out[2:2 + n]), out[-1]


def _exchange_wait(name, arrays, send_sems, recv_sems, after, plan):
    n = len(arrays)

    def body(*refs):
        bufs = refs[:n]
        sends, arrivals = plan(bufs, refs[n], refs[n + 1])
        for cp in arrivals:
            cp.wait_recv()
        for cp in sends:
            cp.wait_send()

    outs = pl.pallas_call(
        body,
        out_shape=[pltpu.HBM(a.shape, a.dtype) for a in arrays],
        in_specs=[HBM_SPEC] * n + [SEM_SPEC, SEM_SPEC, ANY],
        out_specs=[HBM_SPEC] * n,
        input_output_aliases={k: k for k in range(n)},
        compiler_params=pltpu.CompilerParams(has_side_effects=DATAFLOW_EFFECT),
        name=name,
    )(*arrays, send_sems, recv_sems, after)
    return list(outs)


def _half_block(ref, shard_shape, col_sharded, chip, half):
    r, cdim = shard_shape
    h = r // 2
    if col_sharded:
        return ref.at[pl.ds(pl.multiple_of(half * h, BF16_ROWS), h),
                      pl.ds(pl.multiple_of(chip * cdim, LANE), cdim)]
    return ref.at[pl.ds(pl.multiple_of(chip * r + half * h, BF16_ROWS), h), :]


def _gather_plan(shard_shapes, col_sharded, ks):
    def plan(bufs, send_sems, recv_sems):
        x, y, c = _mesh_pos()
        sends, arrivals = [], []
        for ref, k in zip(bufs, ks):
            mine = _half_block(ref, shard_shapes[k], col_sharded[k], 2 * x + y, c)
            for j, (px, py) in enumerate(_other_chips(x, y)):
                landed = _half_block(ref, shard_shapes[k], col_sharded[k], 2 * px + py, c)
                sends.append(_remote(mine, mine, send_sems, recv_sems, 3 * k + j, (px, py, c)))
                arrivals.append(_remote(landed, landed, send_sems, recv_sems, 3 * k + j, (px, py, c)))
        return sends, arrivals
    return plan


def _forward_plan(shard_shapes, col_sharded, ks):
    def plan(bufs, send_sems, recv_sems):
        x, y, c = _mesh_pos()
        sends, arrivals = [], []
        for i, (ref, k) in enumerate(zip(bufs, ks)):
            for j, (px, py) in enumerate(_other_chips(x, y)):
                landed = _half_block(ref, shard_shapes[k], col_sharded[k], 2 * px + py, c)
                theirs = _half_block(ref, shard_shapes[k], col_sharded[k], 2 * px + py, 1 - c)
                sends.append(_remote(landed, landed, send_sems, recv_sems, 3 * i + j, (x, y, 1 - c)))
                arrivals.append(_remote(theirs, theirs, send_sems, recv_sems, 3 * i + j, (x, y, 1 - c)))
        return sends, arrivals
    return plan


def _small_gather(small):
    def body(small_ref, out_ref, send_sems, recv_sems):
        x, y, c = _mesh_pos()
        me = 2 * x + y
        out_ref[me] = small_ref[...]
        copies = []
        for j, (px, py) in enumerate(_other_chips(x, y)):
            cp = _remote(small_ref, out_ref.at[me], send_sems, recv_sems, j, (px, py, c))
            cp.start()
            copies.append(cp)
        for j, (px, py) in enumerate(_other_chips(x, y)):
            _remote(small_ref, out_ref.at[2 * px + py], send_sems, recv_sems, j, (px, py, c)).wait_recv()
        for cp in copies:
            cp.wait_send()

    return pl.pallas_call(
        body,
        out_shape=jax.ShapeDtypeStruct((N_CHIPS,) + small.shape, small.dtype),
        in_specs=[VMEM_SPEC],
        out_specs=VMEM_SPEC,
        scratch_shapes=[pltpu.SemaphoreType.DMA((N_CHIPS - 1,)), pltpu.SemaphoreType.DMA((N_CHIPS - 1,))],
        name="gather_small",
    )(small)


def _as3d(g, col_sharded):
    r, cdim = g.shape
    return g.reshape(1, r, cdim) if col_sharded else g.reshape(N_CHIPS, r // N_CHIPS, cdim)


def _pair_plan(m):
    def plan(bufs, send_sems, recv_sems):
        x, y, c = _mesh_pos()
        copies = []
        for i in range(m):
            h = bufs[i].shape[1] // 2
            src = bufs[i].at[:, pl.ds(pl.multiple_of((1 - c) * h, BF16_ROWS), h), :]
            copies.append(_remote(src, bufs[m + i], send_sems, recv_sems, i, (x, y, 1 - c)))
        return copies, copies
    return plan


def _chip_plan(col_flags):
    m = len(col_flags)

    def plan(bufs, send_sems, recv_sems):
        x, y, c = _mesh_pos()
        copies = []
        for i in range(m):
            land = bufs[m + i]
            width = land.shape[2]
            for j, (px, py) in enumerate(_other_chips(x, y)):
                q = 2 * px + py
                if col_flags[i]:
                    src = bufs[i].at[0, :, pl.ds(pl.multiple_of(q * width, LANE), width)]
                else:
                    src = bufs[i].at[q]
                copies.append(_remote(src, land.at[j], send_sems, recv_sems, 3 * i + j, (px, py, c)))
        return copies, copies
    return plan


def _share_plan(m):
    def plan(bufs, send_sems, recv_sems):
        x, y, c = _mesh_pos()
        sends, arrivals = [], []
        for i in range(m):
            h = bufs[i].shape[0] // 2
            mine = bufs[i].at[pl.ds(pl.multiple_of(c * h, SUBLANE), h), :]
            theirs = bufs[i].at[pl.ds(pl.multiple_of((1 - c) * h, SUBLANE), h), :]
            sends.append(_remote(mine, mine, send_sems, recv_sems, i, (x, y, 1 - c)))
            arrivals.append(_remote(theirs, theirs, send_sems, recv_sems, i, (x, y, 1 - c)))
        return sends, arrivals
    return plan


def _pair_add(g3, other, core):
    a, r, cdim = g3.shape
    h = r // 2
    rb = _pick(h, max(BF16_ROWS, STREAM_BLOCK // cdim), BF16_ROWS)
    nb = h // rb

    def body(core_ref, g_ref, o_ref, out_ref):
        out_ref[...] = (g_ref[...].astype(F32) + o_ref[...].astype(F32)).astype(BF16)

    grid_spec = pltpu.PrefetchScalarGridSpec(
        num_scalar_prefetch=1,
        grid=(a, nb),
        in_specs=[pl.BlockSpec((None, rb, cdim), lambda i, j, core_ref: (i, core_ref[0] * nb + j, 0)),
                  pl.BlockSpec((None, rb, cdim), lambda i, j, core_ref: (i, j, 0))],
        out_specs=pl.BlockSpec((None, rb, cdim), lambda i, j, core_ref: (i, j, 0)),
    )
    return pl.pallas_call(
        body,
        out_shape=jax.ShapeDtypeStruct((a, h, cdim), BF16),
        grid_spec=grid_spec,
        compiler_params=_cparams(("parallel", "parallel"), 2 * rb * cdim * 10),
        name="grad_pair_add",
    )(core, g3, other)


def _small_allreduce(small):
    rows = small.shape[0]
    pad = (-rows) % (2 * SUBLANE)
    if pad:
        small = jnp.pad(small, ((0, pad), (0, 0)))
    h = small.shape[0] // 2
    half_shape = (h, small.shape[1])

    def body(small_ref, out_ref, theirs, by_chip, send_sems, recv_sems):
        x, y, c = _mesh_pos()
        me = 2 * x + y
        sibling = (x, y, 1 - c)
        mine = pl.ds(pl.multiple_of(c * h, SUBLANE), h)
        other = pl.ds(pl.multiple_of((1 - c) * h, SUBLANE), h)
        swap = _remote(small_ref, theirs, send_sems, recv_sems, 0, sibling)
        swap.start()
        swap.wait()
        by_chip[me] = small_ref[mine, :] + theirs[mine, :]
        copies = []
        for j, (px, py) in enumerate(_other_chips(x, y)):
            cp = _remote(by_chip.at[me], by_chip.at[me], send_sems, recv_sems, 1 + j, (px, py, c))
            cp.start()
            copies.append(cp)
        for j, (px, py) in enumerate(_other_chips(x, y)):
            landed = by_chip.at[2 * px + py]
            _remote(landed, landed, send_sems, recv_sems, 1 + j, (px, py, c)).wait_recv()
        total = by_chip[0]
        for q in range(1, N_CHIPS):
            total = total + by_chip[q]
        out_ref[mine, :] = total
        for cp in copies:
            cp.wait_send()
        share = _remote(out_ref.at[mine, :], out_ref.at[mine, :], send_sems, recv_sems, 4, sibling)
        share.start()
        _remote(out_ref.at[other, :], out_ref.at[other, :], send_sems, recv_sems, 4, sibling).wait_recv()
        share.wait_send()

    out = pl.pallas_call(
        body,
        out_shape=jax.ShapeDtypeStruct(small.shape, F32),
        in_specs=[VMEM_SPEC],
        out_specs=VMEM_SPEC,
        scratch_shapes=[pltpu.VMEM(small.shape, F32), pltpu.VMEM((N_CHIPS,) + half_shape, F32),
                        pltpu.SemaphoreType.DMA((5,)), pltpu.SemaphoreType.DMA((5,))],
        compiler_params=pltpu.CompilerParams(
            vmem_limit_bytes=min(VMEM_BUDGET, 8 * _nbytes(small.shape, F32) + (8 << 20))),
        name="grad_small_allreduce",
    )(small)
    return out[:rows]


def _chip_sum(partial, land, where, col_sharded):
    _, h, cdim = land.shape
    rb = _pick(h, max(BF16_ROWS, STREAM_BLOCK // cdim), BF16_ROWS)
    nb = h // rb

    def body(where_ref, own_ref, l_ref, o_ref):
        total = own_ref[...].astype(F32)
        for j in range(N_CHIPS - 1):
            total = total + l_ref[j].astype(F32)
        o_ref[...] = total

    if col_sharded:
        own_map = lambda i, w: (0, i, w[0])
    else:
        own_map = lambda i, w: (w[0], i, 0)
    grid_spec = pltpu.PrefetchScalarGridSpec(
        num_scalar_prefetch=1,
        grid=(nb,),
        in_specs=[pl.BlockSpec((None, rb, cdim), own_map),
                  pl.BlockSpec((N_CHIPS - 1, rb, cdim), lambda i, w: (0, i, 0))],
        out_specs=pl.BlockSpec((rb, cdim), lambda i, w: (w[1] * nb + i, 0)),
    )
    return pl.pallas_call(
        body,
        out_shape=jax.ShapeDtypeStruct((2 * h, cdim), F32),
        grid_spec=grid_spec,
        compiler_params=_cparams(("parallel",), 2 * rb * cdim * 12),
        name="grad_chip_sum",
    )(where, partial, land)


def _adamw(w, g, m, v, name, dep=None):
    r, cdim = w.shape
    rb = _pick(r, max(SUBLANE, (STREAM_BLOCK // 8) // cdim), SUBLANE)
    c1 = 1.0 - ADAM_B1 ** ADAM_STEP
    c2 = 1.0 - ADAM_B2 ** ADAM_STEP

    def body(w_ref, g_ref, m_ref, v_ref, go_ref, d_ref, mo_ref, vo_ref):
        gv = g_ref[...]
        mn = ADAM_B1 * m_ref[...] + (1.0 - ADAM_B1) * gv
        vn = ADAM_B2 * v_ref[...] + (1.0 - ADAM_B2) * (gv * gv)
        m_hat = mn / c1
        v_hat = vn / c2
        d_ref[...] = -ADAM_LR * (m_hat / (jnp.sqrt(v_hat) + ADAM_EPS) + ADAM_WD * w_ref[...])
        go_ref[...] = gv
        mo_ref[...] = mn
        vo_ref[...] = vn

    blk = pl.BlockSpec((rb, cdim), lambda i: (i, 0))
    shape = jax.ShapeDtypeStruct((r, cdim), F32)
    body, in_specs, operands = _dep_args(body, [blk] * 4, [w, g, m, v], dep)
    return pl.pallas_call(
        body,
        out_shape=[shape] * 4,
        grid=(r // rb,),
        in_specs=in_specs,
        out_specs=[blk] * 4,
        compiler_params=_cparams(("parallel",), 2 * rb * cdim * 4 * 8),
        name=name,
    )(*operands)


def _pack(arrays):
    tile = SUBLANE * LANE
    pieces = []
    for arr in arrays:
        flat = arr.reshape(-1)
        pad = (-flat.shape[0]) % tile
        if pad:
            flat = jnp.concatenate([flat, jnp.zeros((pad,), flat.dtype)])
        pieces.append(flat)
    return jnp.concatenate(pieces).reshape(-1, LANE)


def _unpack(packed, shapes):
    tile = SUBLANE * LANE
    flat = packed.reshape(-1)
    out, off = [], 0
    for shp in shapes:
        size = math.prod(shp)
        out.append(flat[off:off + size].reshape(shp))
        off += size + ((-size) % tile)
    return out


def _block_diag_groups(w, per_group):
    hcount, hd, _ = w.shape
    ng = hcount // per_group
    w4 = w.reshape(ng, per_group, hd, hd)
    eye = jnp.eye(per_group, dtype=w.dtype)
    bd = w4[:, :, :, None, :] * eye[None, :, None, :, None]
    return bd.reshape(ng, per_group * hd, per_group * hd).astype(BF16)


def _diag_blocks(wbd, per_group, hd):
    ng = wbd.shape[0]
    w5 = wbd.reshape(ng, per_group, hd, per_group, hd)
    blocks = [w5[:, i, :, i, :] for i in range(per_group)]
    return jnp.stack(blocks, axis=1).reshape(ng * per_group, hd, hd)


def kernel(x, g_mix, w_in, lru_conv_w, lru_conv_b, lru_wa, lru_ba, lru_wx, lru_bx, lru_lambda, lru_w_out, sc_conv_w, sc_w_out, w_o, g_ffn, ffn_w_up, ffn_conv_w, ffn_w_down, g_final, loss_target, m_g_mix, m_w_in, m_lru_conv_w, m_lru_conv_b, m_lru_wa, m_lru_ba, m_lru_wx, m_lru_bx, m_lru_lambda, m_lru_w_out, m_sc_conv_w, m_sc_w_out, m_w_o, m_g_ffn, m_ffn_w_up, m_ffn_conv_w, m_ffn_w_down, m_g_final, v_g_mix, v_w_in, v_lru_conv_w, v_lru_conv_b, v_lru_wa, v_lru_ba, v_lru_wx, v_lru_bx, v_lru_lambda, v_lru_w_out, v_sc_conv_w, v_sc_w_out, v_w_o, v_g_ffn, v_ffn_w_up, v_ffn_conv_w, v_ffn_w_down, v_g_final):
    seq, d_model = x.shape[1], x.shape[2]
    heads, head_dim, _ = lru_wa.shape
    d_lru = heads * head_dim
    d_sc = sc_w_out.shape[0]
    d_ff = ffn_w_down.shape[0] * N_CHIPS
    assert x.shape[0] == 1 and w_in.shape[1] * N_CHIPS == 2 * d_lru + 3 * d_sc + 2 * d_model
    xs = x.reshape(seq, d_model)
    target = loss_target.reshape(seq, d_model)

    chip = 2 * lax.axis_index("x") + lax.axis_index("y")
    core = lax.axis_index("c").astype(jnp.int32).reshape(1)

    big_w = [w_in, lru_w_out, sc_w_out, w_o, ffn_w_up, ffn_w_down]
    big_m = [m_w_in, m_lru_w_out, m_sc_w_out, m_w_o, m_ffn_w_up, m_ffn_w_down]
    big_v = [v_w_in, v_lru_w_out, v_sc_w_out, v_w_o, v_ffn_w_up, v_ffn_w_down]
    col_sharded = [True, True, True, False, True, False]
    conv_shards = [lru_conv_w, sc_conv_w, ffn_conv_w]
    conv_pack = jnp.concatenate(
        [jnp.pad(w, ((0, SUBLANE - w.shape[0]), (0, 0))) for w in conv_shards], axis=1)
    big_names = ["w_in", "lru_w_out", "sc_w_out", "w_o", "ffn_w_up", "ffn_w_down"]
    chip_arr = chip.astype(jnp.int32).reshape(1)
    placed = [_cast_place(big_w[0], chip_arr, col_sharded[0], "cast_" + big_names[0])]
    conv_all = _small_gather(conv_pack)
    shard_shapes = [w.shape for w in big_w]
    n_big = len(big_w)

    def gather_start(ks, after, tag):
        send, recv, bufs, token = _exchange_start(
            "gather_start_" + tag, [placed[k] for k in ks], 3 * n_big,
            _gather_plan(shard_shapes, col_sharded, ks), after=after)
        return (send, recv, dict(zip(ks, bufs))), token

    def arrived(state, ks, after, tag):
        send, recv, bufs = state
        got = _exchange_wait("gather_wait_" + tag, [bufs[k] for k in ks], send, recv, after,
                             _gather_plan(shard_shapes, col_sharded, ks))
        return _exchange("gather_forward_" + tag, got, 3 * len(ks), _forward_plan(shard_shapes, col_sharded, ks))

    def arrived_behind(state, ks, after, tag):
        send, recv, bufs = state
        got = _exchange_wait("gather_wait_" + tag, [bufs[k] for k in ks], send, recv, after,
                             _gather_plan(shard_shapes, col_sharded, ks))
        plan = _forward_plan(shard_shapes, col_sharded, ks)
        send, recv, got, token = _exchange_start("gather_forward_start_" + tag, got, 3 * len(ks), plan)
        return (send, recv, got, plan, tag), token

    def forwarded(state, after):
        send, recv, got, plan, tag = state
        return _exchange_wait("gather_forward_wait_" + tag, got, send, recv, after, plan)

    conv_full, off = [], 0
    for w in conv_shards:
        kw, nq = w.shape
        piece = conv_all[:, :kw, off:off + nq]
        conv_full.append(piece.transpose(1, 0, 2).reshape(kw, N_CHIPS * nq))
        off += nq
    lcw, scw, fcw = conv_full

    per_group = max(1, min(heads, 256 // head_dim))
    gc = per_group * head_dim
    wa_bd = _block_diag_groups(lru_wa, per_group)
    wx_bd = _block_diag_groups(lru_wx, per_group)
    tc = _pick(seq, 256, SUBLANE)
    cb_sc = _pick(d_sc, 512)
    cb_ff = _pick(d_ff, 128)
    tc_ff = _pick(seq, 2048, SUBLANE)
    tc_lru = _pick(seq, 1024, SUBLANE)
    tc_merge = _pick(seq, 512, SUBLANE)
    col_sc = 2 * d_lru
    col_gates = 2 * d_lru + 3 * d_sc

    first, token = gather_start([0], conv_all, "in")
    for k in range(1, n_big):
        placed.append(_cast_place(big_w[k], chip_arr, col_sharded[k], "cast_" + big_names[k], dep=token))
        token = placed[-1]
    h1 = _rms_fwd(xs, g_mix, "rms_mix", dep=token)
    (win_b,) = arrived(first, [0], h1, "in")
    rest, token = gather_start([1, 2, 3, 4, 5], win_b, "rest")
    p = _mm(h1, win_b, "nn", F32, name="mm_in", dep=token)
    mix, token = arrived_behind(rest, [1, 2, 3], p, "mix")
    y_lru_pre, hseq = _lru_fwd(p, lcw, lru_conv_b, wa_bd, lru_ba, wx_bd, lru_bx, lru_lambda, d_lru, gc, tc_lru,
                               dep=token)
    y_sc_pre = _sc_fwd(p, scw, col_sc, d_sc, cb_sc, tc)
    wlo_b, wso_b, wo_b = forwarded(mix, y_sc_pre)
    y_lru = _mm(y_lru_pre, wlo_b, "nn", BF16, name="mm_lru_out")
    y_sc = _mm(y_sc_pre, wso_b, "nn", BF16, name="mm_sc_out")
    merged = _merge_fwd(p, y_lru, y_sc, col_gates, tc_merge)
    x2 = _mm(merged, wo_b, "nn", F32, res=xs, name="mm_o")
    (wup_b,) = arrived(rest, [4], x2, "up")
    h2 = _rms_fwd(x2, g_ffn, "rms_ffn")
    up = _mm(h2, wup_b, "nn", F32, name="mm_up")
    down, token = arrived_behind(rest, [5], up, "down")
    act = _ffn_act_fwd(up, fcw, d_ff, cb_ff, tc_ff, dep=token)
    (wdn_b,) = forwarded(down, act)
    x3 = _mm(act, wdn_b, "nn", F32, res=x2, name="mm_down")
    loss_part, dx3, dx3b, dg_final = _loss_head(x3, g_final, target)

    where = jnp.concatenate([chip_arr, core])

    def reduce_start(grads, flags, tag):
        views = [_as3d(g, cs) for g, cs in zip(grads, flags)]
        lands = [lax.empty((v.shape[0], v.shape[1] // 2, v.shape[2]), v.dtype) for v in views]
        send, recv, bufs, token = _exchange_start("grad_pair_start_" + tag, views + lands, len(views),
                                                  _pair_plan(len(views)))
        return (send, recv, bufs, flags, tag), token

    def reduce_mid(state, after):
        send, recv, bufs, flags, tag = state
        m = len(flags)
        bufs = _exchange_wait("grad_pair_wait_" + tag, bufs, send, recv, after, _pair_plan(m))
        partials = [_pair_add(bufs[i], bufs[m + i], core) for i in range(m)]
        lands = []
        for pz, cs in zip(partials, flags):
            _, h, cdim = pz.shape
            lands.append(lax.empty((N_CHIPS - 1, h, cdim // N_CHIPS if cs else cdim), BF16))
        send, recv, bufs, token = _exchange_start("grad_chip_start_" + tag, partials + lands, 3 * m,
                                                  _chip_plan(flags))
        return (send, recv, bufs, flags, tag), token

    def reduce_end(state, after):
        send, recv, bufs, flags, tag = state
        m = len(flags)
        bufs = _exchange_wait("grad_chip_wait_" + tag, bufs, send, recv, after, _chip_plan(flags))
        return [_chip_sum(bufs[i], bufs[m + i], where, flags[i]) for i in range(m)]

    g_wdn = _mm(act, dx3b, "tn", F32, name="mm_down_dw")
    red_down, token = reduce_start([g_wdn], [False], "down")
    dact = _mm(dx3b, wdn_b, "nt", F32, name="mm_down_dx", dep=token)
    red_down, token = reduce_mid(red_down, dact)
    dup, dfcw_g, dfcw_v = _ffn_act_bwd(up, dact, fcw, d_ff, cb_ff, tc_ff, dep=token)
    g_wup = _mm(h2, dup, "tn", BF16, name="mm_up_dw", slabs=[0, 1])
    red_up, token = reduce_start([g_wup], [True], "up")
    dh2 = _mm(dup, wup_b, "nt", F32, name="mm_up_dx", dep=token, slabs=[0, 1])
    red_up, token = reduce_mid(red_up, dh2)
    dx2, dx2b, dg_ffn = _rms_bwd(x2, g_ffn, dh2, dx3, "rms_ffn_bwd", True, dep=token)
    g_wo = _mm(merged, dx2b, "tn", BF16, name="mm_o_dw")
    dmerged = _mm(dx2b, wo_b, "nt", BF16, name="mm_o_dx")
    slab_w = d_lru
    assert d_sc == slab_w and d_model % slab_w == 0 and col_gates % slab_w == 0
    n_gate = d_model // slab_w
    gate0 = col_gates // slab_w
    dp_slabs = [gate0 + kind * n_gate + j for j in range(n_gate) for kind in (0, 1)] + [0, 1, 2, 3, 4]
    dp, dyl, dys = _merge_bwd(p, y_lru, y_sc, dmerged, col_gates, tc_merge, len(dp_slabs))
    assert dp.shape[2] == slab_w
    g_wlo = _mm(y_lru_pre, dyl, "tn", BF16, name="mm_lru_out_dw")
    g_wso = _mm(y_sc_pre, dys, "tn", BF16, name="mm_sc_out_dw")
    red_mix, token = reduce_start([g_wlo, g_wso, g_wo], [True, True, False], "mix")
    dylp = _mm(dyl, wlo_b, "nt", F32, name="mm_lru_out_dx", dep=token)
    dysp = _mm(dys, wso_b, "nt", F32, name="mm_sc_out_dx")
    red_mix, token = reduce_mid(red_mix, dysp)
    dp, dlcw, dlcb, dwa_bd, dba, dwx_bd, dbx, dlam = _lru_bwd(
        p, hseq, dylp, lcw, lru_conv_b, wa_bd, lru_ba, wx_bd, lru_bx, lru_lambda, d_lru, gc, tc_lru,
        dp, 2 * n_gate, dep=token)
    dp, dscw = _sc_bwd(p, dysp, scw, col_sc, d_sc, cb_sc, tc, dp, 2 * n_gate + 2)
    g_win = _mm(h1, dp, "tn", BF16, name="mm_in_dw", slabs=dp_slabs)
    red_in, token = reduce_start([g_win], [True], "in")
    dh1 = _mm(dp, win_b, "nt", F32, name="mm_in_dx", dep=token, slabs=dp_slabs)
    grad_x, dg_mix = _rms_bwd(xs, g_mix, dh1, dx2, "rms_mix_bwd", False)

    small_g = [dg_mix, dlcw, dlcb, _diag_blocks(dwa_bd, per_group, head_dim), dba,
               _diag_blocks(dwx_bd, per_group, head_dim), dbx, dlam, dscw, dg_ffn,
               jnp.concatenate([dfcw_g, dfcw_v], axis=1), dg_final]
    small_shapes = [a.shape for a in small_g]
    small_sum = _small_allreduce(_pack(small_g))
    red_in, token = reduce_mid(red_in, small_sum)
    (h_wdn,) = reduce_end(red_down, token)
    (h_wup,) = reduce_end(red_up, token)
    h_wlo, h_wso, h_wo = reduce_end(red_mix, token)
    s_wlo, s_wso, s_wo, s_wup, s_wdn = _exchange("grad_share_a", [h_wlo, h_wso, h_wo, h_wup, h_wdn], 5,
                                                 _share_plan(5))
    early = {1: s_wlo, 2: s_wso, 3: s_wo, 4: s_wup, 5: s_wdn}
    big_out = [None] * n_big
    last = None
    for k, g in early.items():
        big_out[k] = _adamw(big_w[k], g, big_m[k], big_v[k], "adamw_" + big_names[k], dep=last)
        last = big_out[k][1]
    (h_win,) = reduce_end(red_in, last)
    (s_win,) = _exchange("grad_share_b", [h_win], 1, _share_plan(1))
    big_out[0] = _adamw(big_w[0], s_win, big_m[0], big_v[0], "adamw_" + big_names[0])
    sg = _unpack(small_sum, small_shapes)
    for idx in (1, 8, 10):
        nq = sg[idx].shape[1] // N_CHIPS
        sg[idx] = lax.dynamic_slice_in_dim(sg[idx], chip * nq, nq, axis=1)
    small_w = [g_mix, lru_conv_w, lru_conv_b, lru_wa, lru_ba, lru_wx, lru_bx, lru_lambda, sc_conv_w,
               g_ffn, ffn_conv_w, g_final]
    small_m = [m_g_mix, m_lru_conv_w, m_lru_conv_b, m_lru_wa, m_lru_ba, m_lru_wx, m_lru_bx, m_lru_lambda,
               m_sc_conv_w, m_g_ffn, m_ffn_conv_w, m_g_final]
    small_v = [v_g_mix, v_lru_conv_w, v_lru_conv_b, v_lru_wa, v_lru_ba, v_lru_wx, v_lru_bx, v_lru_lambda,
               v_sc_conv_w, v_g_ffn, v_ffn_conv_w, v_g_final]
    sg = [g.reshape(w.shape) for g, w in zip(sg, small_w)]
    w_shapes = [w.shape for w in small_w]
    packed = _adamw(_pack(small_w), _pack(sg), _pack(small_m), _pack(small_v), "adamw_small")
    small_out = [_unpack(pk, w_shapes) for pk in packed]

    order = [(0, 0), (1, 0), (0, 1), (0, 2), (0, 3), (0, 4), (0, 5), (0, 6), (0, 7), (1, 1), (0, 8), (1, 2),
             (1, 3), (0, 9), (1, 4), (0, 10), (1, 5), (0, 11)]
    by_kind = []
    for kind in range(4):
        by_kind.append([big_out[i][kind] if is_big else small_out[kind][i] for is_big, i in order])
    loss = lax.psum(loss_part[0, 0], ("x", "y", "c"))
    return (loss, grad_x.reshape(x.shape), *by_kind[0], *by_kind[1], *by_kind[2], *by_kind[3])
```

```python
import math

import jax
import jax.numpy as jnp
from jax import lax
from jax.experimental import pallas as pl
from jax.experimental.pallas import tpu as pltpu

F32 = jnp.float32
BF16 = jnp.bfloat16

LANE = 128
SUBLANE = 8
BF16_ROWS = 16
VMEM_BYTES_V7X = 64 * 1024 * 1024
VMEM_BUDGET = VMEM_BYTES_V7X - 8 * 1024 * 1024
MM_VMEM_BUDGET = 42 * 1024 * 1024
STREAM_BLOCK = 4 * 1024 * 1024
EPS = 1e-6
LRU_C = 8.0
ADAM_LR = 0.001
ADAM_B1 = 0.9
ADAM_B2 = 0.999
ADAM_EPS = 1e-08
ADAM_WD = 0.01
ADAM_STEP = 10

N_CHIPS = 4
N_DEV = 8
MESH = pl.DeviceIdType.MESH
ANY = pl.BlockSpec(memory_space=pl.ANY)
VMEM_SPEC = pl.BlockSpec(memory_space=pltpu.VMEM)
HBM_SPEC = pl.BlockSpec(memory_space=pltpu.HBM)
SEM_SPEC = pl.BlockSpec(memory_space=pltpu.SEMAPHORE)
DATAFLOW_EFFECT = pltpu.SideEffectType.DATAFLOW_SIDE_EFFECTING


def _pick(n, cap, mult=LANE):
    best = None
    d = mult
    while d <= min(n, cap):
        if n % d == 0:
            best = d
        d += mult
    return n if best is None else best


def _cparams(semantics, block_bytes):
    limit = min(VMEM_BUDGET, max(32 * 1024 * 1024, int(block_bytes * 1.25) + (4 << 20)))
    return pltpu.CompilerParams(dimension_semantics=semantics, vmem_limit_bytes=limit)


def _nbytes(shape, dtype):
    return math.prod(shape) * jnp.dtype(dtype).itemsize


def _sigmoid(z):
    return 0.5 * jnp.tanh(0.5 * z) + 0.5


def _softplus(z):
    e = jnp.exp(-jnp.abs(z))
    u = 1.0 + e
    log1p = jnp.where(u == 1.0, e, jnp.log(u) * (e / (u - 1.0)))
    return jnp.maximum(z, 0.0) + log1p


def _neg_expm1(z):
    small = z * (1.0 + z * (0.5 + z * (1.0 / 6.0 + z * (1.0 / 24.0))))
    return -jnp.where(jnp.abs(z) < 0.03, small, jnp.exp(z) - 1.0)


_GELU_K = math.sqrt(2.0 / math.pi)
_GELU_C = 0.044715


def _gelu_and_grad(z):
    z2 = z * z
    th = jnp.tanh(_GELU_K * (z + _GELU_C * z2 * z))
    val = 0.5 * z * (1.0 + th)
    grad = 0.5 * (1.0 + th) + 0.5 * z * (1.0 - th * th) * (_GELU_K * (1.0 + 3.0 * _GELU_C * z2))
    return val, grad


def _rows_before(cat, k):
    if k == 0:
        return cat[SUBLANE:, :]
    return pltpu.roll(cat, k, 0)[SUBLANE:, :]


def _rows_after(cat, k):
    n = cat.shape[0]
    if k == 0:
        return cat[:n - SUBLANE, :]
    return pltpu.roll(cat, n - k, 0)[:n - SUBLANE, :]


def _conv_fwd(cat, w, width):
    y = _rows_before(cat, width - 1) * w[0:1, :]
    for k in range(1, width):
        y = y + _rows_before(cat, width - 1 - k) * w[k:k + 1, :]
    return y


def _conv_bwd_input(cat, w, width):
    dx = _rows_after(cat, width - 1) * w[0:1, :]
    for k in range(1, width):
        dx = dx + _rows_after(cat, width - 1 - k) * w[k:k + 1, :]
    return dx


def _conv_bwd_weight(dw_ref, dy, catx, width):
    for k in range(width):
        dw_ref[k:k + 1, :] += jnp.sum(dy * _rows_before(catx, width - 1 - k), axis=0, keepdims=True)


def _scan_tiles(a_ref, b_ref, out_ref, carry0, n_rows, reverse):
    cols = a_ref.shape[1]
    row = lax.broadcasted_iota(jnp.int32, (SUBLANE, cols), 0)
    n_tiles = n_rows // SUBLANE

    def step(j, carry):
        tile = (n_tiles - 1 - j) if reverse else j
        off = pl.multiple_of(tile * SUBLANE, SUBLANE)
        a = a_ref[pl.ds(off, SUBLANE), :]
        b = b_ref[pl.ds(off, SUBLANE), :]
        for s in (1, 2, 4):
            if reverse:
                keep = row < SUBLANE - s
                shift = SUBLANE - s
            else:
                keep = row >= s
                shift = s
            a_sh = jnp.where(keep, pltpu.roll(a, shift, 0), 1.0)
            b_sh = jnp.where(keep, pltpu.roll(b, shift, 0), 0.0)
            b = a * b_sh + b
            a = a * a_sh
        out = a * carry + b
        out_ref[pl.ds(off, SUBLANE), :] = out
        return out[0:1, :] if reverse else out[SUBLANE - 1:SUBLANE, :]

    return lax.fori_loop(0, n_tiles, step, carry0)


def _dep_args(body, in_specs, operands, *deps):
    deps = [d for d in deps if d is not None]
    if not deps:
        return body, in_specs, operands
    n = len(operands)

    def wrapped(*refs):
        return body(*refs[:n], *refs[n + len(deps):])

    return wrapped, list(in_specs) + [ANY] * len(deps), list(operands) + deps


def _mm(a, b, mode, out_dtype, res=None, name=None, dep=None, slabs=None):
    assert a.dtype == BF16 and b.dtype == BF16
    a_slabbed, b_slabbed = a.ndim == 3, b.ndim == 3
    assert not a_slabbed or (mode == "nt" and slabs is not None)
    assert not b_slabbed or (mode == "tn" and slabs is not None)
    if mode == "nn":
        (m, k), (k2, n) = a.shape, b.shape
        dims = (((1,), (0,)), ((), ()))
    elif mode == "nt":
        m, k = (a.shape[1], a.shape[0] * a.shape[2]) if a_slabbed else a.shape
        n, k2 = b.shape
        dims = (((1,), (1,)), ((), ()))
    else:
        k, m = a.shape
        k2, n = (b.shape[1], b.shape[0] * b.shape[2]) if b_slabbed else b.shape
        dims = (((0,), (0,)), ((), ()))
    assert k == k2
    n_unit = b.shape[2] if b_slabbed else n
    out_bytes = jnp.dtype(out_dtype).itemsize
    bm = _pick(m, 1024)
    bn = _pick(n_unit, 1024)
    bk = k

    a_buffers = 2

    def est(bm_, bn_, bk_):
        e = (a_buffers * bm_ * bk_ + 2 * bk_ * bn_) * 2 + 2 * bm_ * bn_ * out_bytes
        if k // bk_ > 1:
            e += bm_ * bn_ * 4
        if res is not None:
            e += 2 * bm_ * bn_ * 4
        return e

    def fit(bm_, bn_, order):
        for shrink_n, floor in order:
            while est(bm_, bn_, bk) > MM_VMEM_BUDGET:
                if shrink_n and bn_ > floor and bn_ % 2 == 0 and n_unit % (bn_ // 2) == 0:
                    bn_ //= 2
                elif not shrink_n and bm_ > floor and bm_ % 2 == 0 and m % (bm_ // 2) == 0:
                    bm_ //= 2
                else:
                    break
        return bm_, bn_

    bm0, bn0 = bm, bn
    bm, bn = fit(bm0, bn0, ((True, 512), (False, 512), (True, 256), (False, 256)))
    if mode == "nt" and bn == 256 and bm < bm0:
        a_buffers = 1
        bm, bn = fit(bm0, bn0, ((True, 256), (False, 512), (False, 256)))
    while (est(bm, bn, bk) > MM_VMEM_BUDGET and not a_slabbed and bk % (2 * LANE) == 0
           and k % (bk // 2) == 0):
        bk //= 2
    nk = k // bk
    per_slab = n_unit // bn

    def out_col(j):
        if not b_slabbed:
            return j
        s = j // per_slab
        where = sum(jnp.where(s == t, slabs[t], 0) for t in range(len(slabs)))
        return where * per_slab + j % per_slab

    if mode == "tn":
        a_spec = pl.BlockSpec((bk, bm), lambda i, j, kk: (kk, i))
    elif a_slabbed:
        a_spec = pl.BlockSpec((a.shape[0], bm, a.shape[2]), lambda i, j, kk: (0, i, 0),
                              pipeline_mode=pl.Buffered(1) if a_buffers == 1 else None)
    else:
        a_spec = pl.BlockSpec((bm, bk), lambda i, j, kk: (i, kk),
                              pipeline_mode=pl.Buffered(1) if a_buffers == 1 else None)
    if mode == "nt":
        b_spec = pl.BlockSpec((bn, bk), lambda i, j, kk: (j, kk))
    elif b_slabbed:
        b_spec = pl.BlockSpec((None, bk, bn), lambda i, j, kk: (j // per_slab, kk, j % per_slab))
    else:
        b_spec = pl.BlockSpec((bk, bn), lambda i, j, kk: (kk, j))
    o_spec = pl.BlockSpec((bm, bn), lambda i, j, kk: (i, out_col(j)))
    in_specs = [a_spec, b_spec]
    operands = [a, b]
    if res is not None:
        in_specs.append(o_spec)
        operands.append(res)
    has_res = res is not None

    def body(*refs):
        a_ref, b_ref = refs[0], refs[1]
        res_ref = refs[2] if has_res else None
        o_ref = refs[2 + has_res]
        if a_slabbed:
            width = a_ref.shape[2]
            part = None
            for s, col in enumerate(slabs):
                term = lax.dot_general(a_ref[s], b_ref[:, col * width:(col + 1) * width], dims,
                                       preferred_element_type=F32)
                part = term if part is None else part + term
        else:
            part = lax.dot_general(a_ref[...], b_ref[...], dims, preferred_element_type=F32)
        if nk == 1:
            if has_res:
                part = part + res_ref[...]
            o_ref[...] = part.astype(o_ref.dtype)
            return
        acc_ref = refs[-1]
        kk = pl.program_id(2)

        @pl.when(kk == 0)
        def _():
            acc_ref[...] = part

        @pl.when(kk > 0)
        def _():
            acc_ref[...] += part

        @pl.when(kk == nk - 1)
        def _():
            total = acc_ref[...]
            if has_res:
                total = total + res_ref[...]
            o_ref[...] = total.astype(o_ref.dtype)

    scratch = [pltpu.VMEM((bm, bn), F32)] if nk > 1 else []
    body, in_specs, operands = _dep_args(body, in_specs, operands, dep)
    return pl.pallas_call(
        body,
        out_shape=jax.ShapeDtypeStruct((m, n), out_dtype),
        grid=(m // bm, n // bn, nk),
        in_specs=in_specs,
        out_specs=o_spec,
        scratch_shapes=scratch,
        compiler_params=_cparams(("parallel", "parallel", "arbitrary"), est(bm, bn, bk)),
        name=name,
    )(*operands)


def _rms_fwd(x, g, name, dep=None):
    t, d = x.shape
    tb = _pick(t, 512, SUBLANE)

    def body(x_ref, g_ref, h_ref):
        xv = x_ref[...]
        r = lax.rsqrt(jnp.mean(xv * xv, axis=-1, keepdims=True) + EPS)
        h_ref[...] = ((xv * r) * g_ref[...]).astype(BF16)

    blk = pl.BlockSpec((tb, d), lambda i: (i, 0))
    body, in_specs, operands = _dep_args(
        body, [blk, pl.BlockSpec((1, d), lambda i: (0, 0))], [x, g.reshape(1, d)], dep)
    return pl.pallas_call(
        body,
        out_shape=jax.ShapeDtypeStruct((t, d), BF16),
        grid=(t // tb,),
        in_specs=in_specs,
        out_specs=blk,
        compiler_params=_cparams(("parallel",), 2 * tb * d * 6),
        name=name,
    )(*operands)


def _rms_bwd(x, g, dh, dres, name, want_bf16, dep=None):
    t, d = x.shape
    tb = _pick(t, 256, SUBLANE)

    def body(x_ref, g_ref, dh_ref, dres_ref, *outs):
        dx_ref, dg_ref = outs[0], outs[-1]
        xv = x_ref[...]
        r = lax.rsqrt(jnp.mean(xv * xv, axis=-1, keepdims=True) + EPS)
        xhat = xv * r
        dhv = dh_ref[...]
        dxhat = dhv * g_ref[...]
        dx = dres_ref[...] + r * (dxhat - xhat * jnp.mean(dxhat * xhat, axis=-1, keepdims=True))
        dx_ref[...] = dx
        if want_bf16:
            outs[1][...] = dx.astype(BF16)

        @pl.when(pl.program_id(0) == 0)
        def _():
            dg_ref[...] = jnp.zeros_like(dg_ref)

        dg_ref[...] += jnp.sum(dhv * xhat, axis=0, keepdims=True)

    blk = pl.BlockSpec((tb, d), lambda i: (i, 0))
    row = pl.BlockSpec((1, d), lambda i: (0, 0))
    out_shape = [jax.ShapeDtypeStruct((t, d), F32)]
    out_specs = [blk]
    if want_bf16:
        out_shape.append(jax.ShapeDtypeStruct((t, d), BF16))
        out_specs.append(blk)
    out_shape.append(jax.ShapeDtypeStruct((1, d), F32))
    out_specs.append(row)
    body, in_specs, operands = _dep_args(
        body, [blk, row, blk, blk], [x, g.reshape(1, d), dh, dres], dep)
    return pl.pallas_call(
        body,
        out_shape=out_shape,
        grid=(t // tb,),
        in_specs=in_specs,
        out_specs=out_specs,
        compiler_params=_cparams(("arbitrary",), 2 * tb * d * 18),
        name=name,
    )(*operands)


def _loss_head(x3, g, target):
    t, d = x3.shape
    tb = _pick(t, 256, SUBLANE)

    def body(x_ref, g_ref, t_ref, loss_ref, dx_ref, dxb_ref, dg_ref):
        xv = x_ref[...]
        gv = g_ref[...]
        r = lax.rsqrt(jnp.mean(xv * xv, axis=-1, keepdims=True) + EPS)
        xhat = xv * r
        err = xhat * gv - t_ref[...]
        dy = err * (1.0 / d)
        dxhat = dy * gv
        dx = r * (dxhat - xhat * jnp.mean(dxhat * xhat, axis=-1, keepdims=True))
        dx_ref[...] = dx
        dxb_ref[...] = dx.astype(BF16)

        @pl.when(pl.program_id(0) == 0)
        def _():
            dg_ref[...] = jnp.zeros_like(dg_ref)
            loss_ref[...] = jnp.zeros_like(loss_ref)

        dg_ref[...] += jnp.sum(dy * xhat, axis=0, keepdims=True)
        per_token = jnp.mean(err * err, axis=-1, keepdims=True)
        loss_ref[...] += 0.5 * jnp.sum(per_token, axis=0, keepdims=True)

    blk = pl.BlockSpec((tb, d), lambda i: (i, 0))
    row = pl.BlockSpec((1, d), lambda i: (0, 0))
    return pl.pallas_call(
        body,
        out_shape=[jax.ShapeDtypeStruct((1, 1), F32), jax.ShapeDtypeStruct((t, d), F32),
                   jax.ShapeDtypeStruct((t, d), BF16), jax.ShapeDtypeStruct((1, d), F32)],
        grid=(t // tb,),
        in_specs=[blk, row, blk],
        out_specs=[pl.BlockSpec((1, 1), lambda i: (0, 0)), blk, blk, row],
        compiler_params=_cparams(("arbitrary",), 2 * tb * d * 14),
        name="loss_head",
    )(x3, g.reshape(1, d), target)


def _lru_gates(xc, wa, ba, wx, bx, lam):
    nn = (((1,), (0,)), ((), ()))
    xcb = xc.astype(BF16)
    r = _sigmoid(lax.dot_general(xcb, wa, nn, preferred_element_type=F32) + ba)
    i = _sigmoid(lax.dot_general(xcb, wx, nn, preferred_element_type=F32) + bx)
    cl = -LRU_C * _softplus(-lam)
    log_a = cl * r
    a = jnp.exp(log_a)
    one_minus_a2 = _neg_expm1(2.0 * log_a)
    return xcb, r, i, a, one_minus_a2, cl


def _lru_fwd(p, conv_w, conv_b, wa_bd, ba, wx_bd, bx, lam, d_lru, gc, tc, dep=None):
    t = p.shape[0]
    ng = d_lru // gc
    nt = t // tc
    width = conv_w.shape[0]

    def body(lx_ref, gate_ref, cw_ref, cb_ref, wa_ref, ba_ref, wx_ref, bx_ref, lam_ref,
             y_ref, h_ref, halo, hcar, a_s, u_s):
        @pl.when(pl.program_id(1) == 0)
        def _():
            halo[...] = jnp.zeros_like(halo)
            hcar[...] = jnp.zeros_like(hcar)

        x = lx_ref[...]
        cat = jnp.concatenate([halo[...], x], axis=0)
        halo[...] = x[tc - SUBLANE:, :]
        xc = _conv_fwd(cat, cw_ref[...], width) + cb_ref[...]
        _, r, i, a, om, _ = _lru_gates(xc, wa_ref[...], ba_ref[...], wx_ref[...], bx_ref[...], lam_ref[...])
        a_s[...] = a
        u_s[...] = jnp.sqrt(om) * (i * xc)
        hcar[0:1, :] = _scan_tiles(a_s, u_s, h_ref, hcar[0:1, :], tc, reverse=False)
        gl, _ = _gelu_and_grad(gate_ref[...])
        y_ref[...] = (gl * h_ref[...]).astype(BF16)

    blk = lambda off: pl.BlockSpec((tc, gc), lambda g, s, off=off: (s, off + g))
    rowv = lambda rows: pl.BlockSpec((rows, gc), lambda g, s: (0, g))
    wspec = pl.BlockSpec((None, gc, gc), lambda g, s: (g, 0, 0))
    out_blk = pl.BlockSpec((tc, gc), lambda g, s: (s, g))
    body, in_specs, operands = _dep_args(
        body, [blk(0), blk(ng), rowv(width), rowv(1), wspec, rowv(1), wspec, rowv(1), rowv(1)],
        [p, p, conv_w, conv_b.reshape(1, -1), wa_bd, ba.reshape(1, -1), wx_bd, bx.reshape(1, -1),
         lam.reshape(1, -1)], dep)
    return pl.pallas_call(
        body,
        out_shape=[jax.ShapeDtypeStruct((t, d_lru), BF16), jax.ShapeDtypeStruct((t, d_lru), F32)],
        grid=(ng, nt),
        in_specs=in_specs,
        out_specs=[out_blk, out_blk],
        scratch_shapes=[pltpu.VMEM((SUBLANE, gc), F32), pltpu.VMEM((SUBLANE, gc), F32),
                        pltpu.VMEM((tc, gc), F32), pltpu.VMEM((tc, gc), F32)],
        compiler_params=_cparams(("parallel", "arbitrary"), 40 * tc * gc * 4),
        name="lru_fwd",
    )(*operands)


def _lru_bwd(p, hseq, dyp, conv_w, conv_b, wa_bd, ba, wx_bd, bx, lam, d_lru, gc, tc, dp, slab0, dep=None):
    t = p.shape[0]
    ng = d_lru // gc
    nt = t // tc
    width = conv_w.shape[0]
    halo_blocks = tc // SUBLANE
    nn = (((1,), (0,)), ((), ()))
    nt_dims = (((1,), (1,)), ((), ()))
    tn_dims = (((0,), (0,)), ((), ()))

    def body(lx_ref, lxh_ref, gate_ref, h_ref, hh_ref, dyp_ref,
             cw_ref, cb_ref, wa_ref, ba_ref, wx_ref, bx_ref, lam_ref,
             dp_ref, dcw_ref, dcb_ref, dwa_ref, dba_ref, dwx_ref, dbx_ref, dlam_ref,
             nxt_dxc, nxt_a, nxt_g, al_s, b_s, g_s):
        s = pl.program_id(1)
        first_chunk = s == nt - 1

        @pl.when(s == 0)
        def _():
            nxt_dxc[...] = jnp.zeros_like(nxt_dxc)
            nxt_a[...] = jnp.zeros_like(nxt_a)
            nxt_g[...] = jnp.zeros_like(nxt_g)
            for ref in (dcw_ref, dcb_ref, dwa_ref, dba_ref, dwx_ref, dbx_ref, dlam_ref):
                ref[...] = jnp.zeros_like(ref)

        keep = jnp.where(first_chunk, 0.0, 1.0)
        x = lx_ref[...]
        catx = jnp.concatenate([lxh_ref[...] * keep, x], axis=0)
        cw = cw_ref[...]
        xc = _conv_fwd(catx, cw, width) + cb_ref[...]
        wa = wa_ref[...]
        wx = wx_ref[...]
        lam_v = lam_ref[...]
        xcb, r, i, a, om, cl = _lru_gates(xc, wa, ba_ref[...], wx, bx_ref[...], lam_v)
        mult = jnp.sqrt(om)

        h = h_ref[...]
        hprev = _rows_before(jnp.concatenate([hh_ref[...] * keep, h], axis=0), 1)
        gl, dgl = _gelu_and_grad(gate_ref[...])
        dyp_v = dyp_ref[...]
        dp_ref[1] = (dyp_v * h * dgl).astype(BF16)

        al_s[...] = _rows_after(jnp.concatenate([a, nxt_a[...]], axis=0), 1)
        b_s[...] = dyp_v * gl
        nxt_g[0:1, :] = _scan_tiles(al_s, b_s, g_s, nxt_g[0:1, :], tc, reverse=True)
        nxt_a[...] = a[0:SUBLANE, :]
        du = g_s[...]

        da = du * hprev
        dmult = du * (i * xc)
        di = du * mult * xc
        dxc = du * mult * i
        dlog_a = da * a - dmult * (a * a / mult)
        dlam_ref[...] += jnp.sum(dlog_a * r, axis=0, keepdims=True) * (LRU_C * _sigmoid(-lam_v))
        dza = (dlog_a * cl) * r * (1.0 - r)
        dzx = di * i * (1.0 - i)
        dba_ref[...] += jnp.sum(dza, axis=0, keepdims=True)
        dbx_ref[...] += jnp.sum(dzx, axis=0, keepdims=True)
        dzab = dza.astype(BF16)
        dzxb = dzx.astype(BF16)
        dwa_ref[...] += lax.dot_general(xcb, dzab, tn_dims, preferred_element_type=F32)
        dwx_ref[...] += lax.dot_general(xcb, dzxb, tn_dims, preferred_element_type=F32)
        dxc = dxc + lax.dot_general(dzab, wa, nt_dims, preferred_element_type=F32)
        dxc = dxc + lax.dot_general(dzxb, wx, nt_dims, preferred_element_type=F32)
        dcb_ref[...] += jnp.sum(dxc, axis=0, keepdims=True)
        _conv_bwd_weight(dcw_ref, dxc, catx, width)
        catd = jnp.concatenate([dxc, nxt_dxc[...]], axis=0)
        dp_ref[0] = _conv_bwd_input(catd, cw, width).astype(BF16)
        nxt_dxc[...] = dxc[0:SUBLANE, :]

    rev = lambda s: nt - 1 - s
    blk = lambda off: pl.BlockSpec((tc, gc), lambda g, s, off=off: (rev(s), off + g))
    halo = lambda off: pl.BlockSpec(
        (SUBLANE, gc), lambda g, s, off=off: (jnp.maximum(rev(s) * halo_blocks - 1, 0), off + g))
    rowv = lambda rows: pl.BlockSpec((rows, gc), lambda g, s: (0, g))
    wspec = pl.BlockSpec((None, gc, gc), lambda g, s: (g, 0, 0))
    out_blk = pl.BlockSpec((tc, gc), lambda g, s: (rev(s), g))
    vec = lambda rows: jax.ShapeDtypeStruct((rows, d_lru), F32)
    wshape = jax.ShapeDtypeStruct((ng, gc, gc), F32)
    body, in_specs, operands = _dep_args(
        body,
        [blk(0), halo(0), blk(ng), blk(0), halo(0), blk(0),
         rowv(width), rowv(1), wspec, rowv(1), wspec, rowv(1), rowv(1)],
        [p, p, p, hseq, hseq, dyp,
         conv_w, conv_b.reshape(1, -1), wa_bd, ba.reshape(1, -1), wx_bd,
         bx.reshape(1, -1), lam.reshape(1, -1)], dp, dep)
    assert dp.shape[2] == d_lru and slab0 % 2 == 0
    return pl.pallas_call(
        body,
        out_shape=[jax.ShapeDtypeStruct(dp.shape, dp.dtype),
                   vec(width), vec(1), wshape, vec(1), wshape, vec(1), vec(1)],
        grid=(ng, nt),
        in_specs=in_specs,
        out_specs=[pl.BlockSpec((2, tc, gc), lambda g, s: (slab0 // 2, rev(s), g)),
                   rowv(width), rowv(1), wspec, rowv(1), wspec, rowv(1), rowv(1)],
        input_output_aliases={13: 0},
        scratch_shapes=[pltpu.VMEM((SUBLANE, gc), F32), pltpu.VMEM((SUBLANE, gc), F32),
                        pltpu.VMEM((SUBLANE, gc), F32),
                        pltpu.VMEM((tc, gc), F32), pltpu.VMEM((tc, gc), F32), pltpu.VMEM((tc, gc), F32)],
        compiler_params=_cparams(("parallel", "arbitrary"), 80 * tc * gc * 4),
        name="lru_bwd",
    )(*operands)


def _sc_fwd(p, conv_w, col0, d_sc, cb, tc):
    t = p.shape[0]
    nc = d_sc // cb
    nt = t // tc
    width = conv_w.shape[0]
    base = col0 // cb

    def body(b_ref, c_ref, v_ref, w_ref, y_ref, halo):
        @pl.when(pl.program_id(1) == 0)
        def _():
            halo[...] = jnp.zeros_like(halo)

        cv = c_ref[...] * v_ref[...]
        cat = jnp.concatenate([halo[...], cv], axis=0)
        halo[...] = cv[tc - SUBLANE:, :]
        y_ref[...] = (b_ref[...] * _conv_fwd(cat, w_ref[...], width)).astype(BF16)

    blk = lambda slab: pl.BlockSpec((tc, cb), lambda j, s, slab=slab: (s, base + slab * nc + j))
    return pl.pallas_call(
        body,
        out_shape=jax.ShapeDtypeStruct((t, d_sc), BF16),
        grid=(nc, nt),
        in_specs=[blk(0), blk(1), blk(2), pl.BlockSpec((width, cb), lambda j, s: (0, j))],
        out_specs=pl.BlockSpec((tc, cb), lambda j, s: (s, j)),
        scratch_shapes=[pltpu.VMEM((SUBLANE, cb), F32)],
        compiler_params=_cparams(("parallel", "arbitrary"), 20 * tc * cb * 4),
        name="sc_fwd",
    )(p, p, p, conv_w)


def _sc_bwd(p, dyp, conv_w, col0, d_sc, cb, tc, dp, slab0):
    t = p.shape[0]
    nc = d_sc // cb
    nt = t // tc
    width = conv_w.shape[0]
    base = col0 // cb
    halo_blocks = tc // SUBLANE

    def body(b_ref, c_ref, ch_ref, v_ref, vh_ref, dyp_ref, w_ref,
             dp_ref, dw_ref, nxt_dq):
        s = pl.program_id(1)

        @pl.when(s == 0)
        def _():
            nxt_dq[...] = jnp.zeros_like(nxt_dq)
            dw_ref[...] = jnp.zeros_like(dw_ref)

        keep = jnp.where(s == nt - 1, 0.0, 1.0)
        cvals = c_ref[...]
        vvals = v_ref[...]
        w = w_ref[...]
        catcv = jnp.concatenate([ch_ref[...] * vh_ref[...] * keep, cvals * vvals], axis=0)
        q = _conv_fwd(catcv, w, width)
        dyp_v = dyp_ref[...]
        dp_ref[0] = (dyp_v * q).astype(BF16)
        dq = dyp_v * b_ref[...]
        _conv_bwd_weight(dw_ref, dq, catcv, width)
        dcv = _conv_bwd_input(jnp.concatenate([dq, nxt_dq[...]], axis=0), w, width)
        nxt_dq[...] = dq[0:SUBLANE, :]
        dp_ref[1] = (dcv * vvals).astype(BF16)
        dp_ref[2] = (dcv * cvals).astype(BF16)

    rev = lambda s: nt - 1 - s
    blk = lambda slab: pl.BlockSpec((tc, cb), lambda j, s, slab=slab: (rev(s), base + slab * nc + j))
    halo = lambda slab: pl.BlockSpec(
        (SUBLANE, cb),
        lambda j, s, slab=slab: (jnp.maximum(rev(s) * halo_blocks - 1, 0), base + slab * nc + j))
    out_blk = pl.BlockSpec((tc, cb), lambda j, s: (rev(s), j))
    wblk = pl.BlockSpec((width, cb), lambda j, s: (0, j))
    assert dp.shape[2] == d_sc and slab0 % 3 == 0
    operands = [p, p, p, p, p, dyp, conv_w]
    body, in_specs, operands = _dep_args(
        body, [blk(0), blk(1), halo(1), blk(2), halo(2), out_blk, wblk], operands, dp)
    return pl.pallas_call(
        body,
        out_shape=[jax.ShapeDtypeStruct(dp.shape, dp.dtype), jax.ShapeDtypeStruct((width, d_sc), F32)],
        grid=(nc, nt),
        in_specs=in_specs,
        out_specs=[pl.BlockSpec((3, tc, cb), lambda j, s: (slab0 // 3, rev(s), j)), wblk],
        input_output_aliases={7: 0},
        scratch_shapes=[pltpu.VMEM((SUBLANE, cb), F32)],
        compiler_params=_cparams(("parallel", "arbitrary"), 30 * tc * cb * 4),
        name="sc_bwd",
    )(*operands)


def _merge_fwd(p, y_lru, y_sc, col0, tc):
    t, d = y_lru.shape
    cb = _pick(math.gcd(d, col0), 1024)
    nc = d // cb
    base = col0 // cb

    def body(gl_ref, gs_ref, yl_ref, ys_ref, o_ref):
        o_ref[...] = (_sigmoid(gl_ref[...]) * yl_ref[...] + _sigmoid(gs_ref[...]) * ys_ref[...]).astype(BF16)

    gate = lambda slab: pl.BlockSpec((tc, cb), lambda s, j, slab=slab: (s, base + slab * nc + j))
    blk = pl.BlockSpec((tc, cb), lambda s, j: (s, j))
    return pl.pallas_call(
        body,
        out_shape=jax.ShapeDtypeStruct((t, d), BF16),
        grid=(t // tc, nc),
        in_specs=[gate(0), gate(1), blk, blk],
        out_specs=blk,
        compiler_params=_cparams(("parallel", "parallel"), 2 * tc * cb * 20),
        name="merge_fwd",
    )(p, p, y_lru, y_sc)


def _merge_bwd(p, y_lru, y_sc, dmerged, col0, tc, n_slabs):
    t, d = y_lru.shape
    cb = _pick(math.gcd(d, col0), 1024)
    nc = d // cb
    base = col0 // cb

    def body(gl_ref, gs_ref, yl_ref, ys_ref, dm_ref, dp_ref, dyl_ref, dys_ref):
        dm = dm_ref[...]
        sl = _sigmoid(gl_ref[...])
        ss = _sigmoid(gs_ref[...])
        dp_ref[0] = (dm * yl_ref[...] * (sl * (1.0 - sl))).astype(BF16)
        dp_ref[1] = (dm * ys_ref[...] * (ss * (1.0 - ss))).astype(BF16)
        dyl_ref[...] = (dm * sl).astype(BF16)
        dys_ref[...] = (dm * ss).astype(BF16)

    gate = lambda slab: pl.BlockSpec((tc, cb), lambda s, j, slab=slab: (s, base + slab * nc + j))
    blk = pl.BlockSpec((tc, cb), lambda s, j: (s, j))
    act = jax.ShapeDtypeStruct((t, d), BF16)
    return pl.pallas_call(
        body,
        out_shape=[jax.ShapeDtypeStruct((n_slabs, t, cb), BF16), act, act],
        grid=(t // tc, nc),
        in_specs=[gate(0), gate(1), blk, blk, blk],
        out_specs=[pl.BlockSpec((2, tc, cb), lambda s, j: (j, s, 0)), blk, blk],
        compiler_params=_cparams(("parallel", "parallel"), 2 * tc * cb * 28),
        name="merge_bwd",
    )(p, p, y_lru, y_sc, dmerged)


def _ffn_act_fwd(up, conv_w, d_ff, cb, tc, dep=None):
    t = up.shape[0]
    nc = d_ff // cb
    nt = t // tc
    width = conv_w.shape[0]

    def body(g_ref, v_ref, wg_ref, wv_ref, o_ref, halo_g, halo_v):
        @pl.when(pl.program_id(1) == 0)
        def _():
            halo_g[...] = jnp.zeros_like(halo_g)
            halo_v[...] = jnp.zeros_like(halo_v)

        g = g_ref[...]
        v = v_ref[...]
        ug = _conv_fwd(jnp.concatenate([halo_g[...], g], axis=0), wg_ref[...], width)
        uv = _conv_fwd(jnp.concatenate([halo_v[...], v], axis=0), wv_ref[...], width)
        halo_g[...] = g[tc - SUBLANE:, :]
        halo_v[...] = v[tc - SUBLANE:, :]
        o_ref[...] = (ug * _sigmoid(ug) * uv).astype(BF16)

    blk = lambda half: pl.BlockSpec((tc, cb), lambda j, s, half=half: (s, half * nc + j))
    wblk = lambda half: pl.BlockSpec((width, cb), lambda j, s, half=half: (0, half * nc + j))
    body, in_specs, operands = _dep_args(
        body, [blk(0), blk(1), wblk(0), wblk(1)], [up, up, conv_w, conv_w], dep)
    return pl.pallas_call(
        body,
        out_shape=jax.ShapeDtypeStruct((t, d_ff), BF16),
        grid=(nc, nt),
        in_specs=in_specs,
        out_specs=pl.BlockSpec((tc, cb), lambda j, s: (s, j)),
        scratch_shapes=[pltpu.VMEM((SUBLANE, cb), F32), pltpu.VMEM((SUBLANE, cb), F32)],
        compiler_params=_cparams(("parallel", "arbitrary"), 24 * tc * cb * 4),
        name="ffn_act_fwd",
    )(*operands)


def _ffn_act_bwd(up, dact, conv_w, d_ff, cb, tc, dep=None):
    t = up.shape[0]
    nc = d_ff // cb
    nt = t // tc
    width = conv_w.shape[0]
    halo_blocks = tc // SUBLANE

    def body(g_ref, gh_ref, v_ref, vh_ref, da_ref, wg_ref, wv_ref,
             dup_ref, dwg_ref, dwv_ref, nxt_g, nxt_v):
        s = pl.program_id(1)

        @pl.when(s == 0)
        def _():
            nxt_g[...] = jnp.zeros_like(nxt_g)
            nxt_v[...] = jnp.zeros_like(nxt_v)
            dwg_ref[...] = jnp.zeros_like(dwg_ref)
            dwv_ref[...] = jnp.zeros_like(dwv_ref)

        keep = jnp.where(s == nt - 1, 0.0, 1.0)
        wg = wg_ref[...]
        wv = wv_ref[...]
        catg = jnp.concatenate([gh_ref[...] * keep, g_ref[...]], axis=0)
        catv = jnp.concatenate([vh_ref[...] * keep, v_ref[...]], axis=0)
        ug = _conv_fwd(catg, wg, width)
        uv = _conv_fwd(catv, wv, width)
        sg = _sigmoid(ug)
        da = da_ref[...]
        duv = da * (ug * sg)
        dup_ref[1] = _conv_bwd_input(jnp.concatenate([duv, nxt_v[...]], axis=0), wv, width).astype(BF16)
        nxt_v[...] = duv[0:SUBLANE, :]
        _conv_bwd_weight(dwv_ref, duv, catv, width)
        dug = da * uv * (sg * (1.0 + ug * (1.0 - sg)))
        dup_ref[0] = _conv_bwd_input(jnp.concatenate([dug, nxt_g[...]], axis=0), wg, width).astype(BF16)
        nxt_g[...] = dug[0:SUBLANE, :]
        _conv_bwd_weight(dwg_ref, dug, catg, width)

    rev = lambda s: nt - 1 - s
    blk = lambda half: pl.BlockSpec((tc, cb), lambda j, s, half=half: (rev(s), half * nc + j))
    halo = lambda half: pl.BlockSpec(
        (SUBLANE, cb), lambda j, s, half=half: (jnp.maximum(rev(s) * halo_blocks - 1, 0), half * nc + j))
    wblk = lambda half: pl.BlockSpec((width, cb), lambda j, s, half=half: (0, half * nc + j))
    out_blk = pl.BlockSpec((tc, cb), lambda j, s: (rev(s), j))
    wout = pl.BlockSpec((width, cb), lambda j, s: (0, j))
    act = jax.ShapeDtypeStruct((t, d_ff), BF16)
    wshape = jax.ShapeDtypeStruct((width, d_ff), F32)
    body, in_specs, operands = _dep_args(
        body, [blk(0), halo(0), blk(1), halo(1), out_blk, wblk(0), wblk(1)],
        [up, up, up, up, dact, conv_w, conv_w], dep)
    return pl.pallas_call(
        body,
        out_shape=[jax.ShapeDtypeStruct((2, t, d_ff), BF16), wshape, wshape],
        grid=(nc, nt),
        in_specs=in_specs,
        out_specs=[pl.BlockSpec((2, tc, cb), lambda j, s: (0, rev(s), j)), wout, wout],
        scratch_shapes=[pltpu.VMEM((SUBLANE, cb), F32), pltpu.VMEM((SUBLANE, cb), F32)],
        compiler_params=_cparams(("parallel", "arbitrary"), 40 * tc * cb * 4),
        name="ffn_act_bwd",
    )(*operands)


def _mesh_pos():
    x, y, c = lax.axis_index("x"), lax.axis_index("y"), lax.axis_index("c")
    return x, y, c


def _other_chips(x, y):
    return [(1 - x, y), (x, 1 - y), (1 - x, 1 - y)]


def _cast_place(w, chip, col_sharded, name, dep=None):
    r, cdim = w.shape
    full = (r, cdim * N_CHIPS) if col_sharded else (r * N_CHIPS, cdim)
    rb = _pick(r, max(BF16_ROWS, STREAM_BLOCK // cdim), BF16_ROWS)
    nb = r // rb

    def body(chip_ref, w_ref, o_ref):
        o_ref[...] = w_ref[...].astype(BF16)

    if col_sharded:
        out_map = lambda i, chip_ref: (i, chip_ref[0])
    else:
        out_map = lambda i, chip_ref: (chip_ref[0] * nb + i, 0)
    grid_spec = pltpu.PrefetchScalarGridSpec(
        num_scalar_prefetch=1,
        grid=(nb,),
        in_specs=[pl.BlockSpec((rb, cdim), lambda i, chip_ref: (i, 0))] + ([ANY] if dep is not None else []),
        out_specs=pl.BlockSpec((rb, cdim), out_map),
    )
    body, _, operands = _dep_args(body, [], [chip, w], dep)
    return pl.pallas_call(
        body,
        out_shape=jax.ShapeDtypeStruct(full, BF16),
        grid_spec=grid_spec,
        compiler_params=_cparams(("parallel",), 2 * rb * cdim * 6),
        name=name,
    )(*operands)


def _remote(src, dst, send_sems, recv_sems, idx, to):
    return pltpu.make_async_remote_copy(
        src_ref=src, dst_ref=dst, send_sem=send_sems.at[idx], recv_sem=recv_sems.at[idx],
        device_id=to, device_id_type=MESH)


def _exchange(name, arrays, n_sems, plan):
    n = len(arrays)

    def body(*refs):
        bufs = refs[n:2 * n]
        send_sems, recv_sems = refs[2 * n:]
        sends, arrivals = plan(bufs, send_sems, recv_sems)
        for cp in sends:
            cp.start()
        for cp in arrivals:
            cp.wait_recv()
        for cp in sends:
            cp.wait_send()

    outs = pl.pallas_call(
        body,
        out_shape=[jax.ShapeDtypeStruct(a.shape, a.dtype) for a in arrays],
        in_specs=[ANY] * n,
        out_specs=[ANY] * n,
        input_output_aliases={k: k for k in range(n)},
        scratch_shapes=[pltpu.SemaphoreType.DMA((n_sems,)), pltpu.SemaphoreType.DMA((n_sems,))],
        name=name,
    )(*arrays)
    return list(outs)


def _exchange_start(name, arrays, n_sems, plan, after=None):
    n = len(arrays)
    n_in = n + (after is not None)

    def body(*refs):
        bufs = refs[:n]
        send_sems, recv_sems = refs[n_in], refs[n_in + 1]
        token = refs[-1]
        sends, _ = plan(bufs, send_sems, recv_sems)
        for cp in sends:
            cp.start()
        token[...] = jnp.zeros_like(token)

    out = pl.pallas_call(
        body,
        out_shape=(pltpu.SemaphoreType.DMA((n_sems,)), pltpu.SemaphoreType.DMA((n_sems,)),
                   *[pltpu.HBM(a.shape, a.dtype) for a in arrays],
                   jax.ShapeDtypeStruct((SUBLANE, LANE), F32)),
        in_specs=[HBM_SPEC] * n + [ANY] * (n_in - n),
        out_specs=(SEM_SPEC, SEM_SPEC, *[HBM_SPEC] * n, VMEM_SPEC),
        input_output_aliases={k: 2 + k for k in range(n)},
        compiler_params=pltpu.CompilerParams(has_side_effects=DATAFLOW_EFFECT),
        name=name,
    )(*[pltpu.with_memory_space_constraint(a, pltpu.HBM) for a in arrays], *([after] if after is not None else []))
    return out[0], out[1], list(out[2:2 + n]), out[-1]


def _exchange_wait(name, arrays, send_sems, recv_sems, after, plan):
    n = len(arrays)

    def body(*refs):
        bufs = refs[:n]
        sends, arrivals = plan(bufs, refs[n], refs[n + 1])
        for cp in arrivals:
            cp.wait_recv()
        for cp in sends:
            cp.wait_send()

    outs = pl.pallas_call(
        body,
        out_shape=[pltpu.HBM(a.shape, a.dtype) for a in arrays],
        in_specs=[HBM_SPEC] * n + [SEM_SPEC, SEM_SPEC, ANY],
        out_specs=[HBM_SPEC] * n,
        input_output_aliases={k: k for k in range(n)},
        compiler_params=pltpu.CompilerParams(has_side_effects=DATAFLOW_EFFECT),
        name=name,
    )(*arrays, send_sems, recv_sems, after)
    return list(outs)


def _half_block(ref, shard_shape, col_sharded, chip, half):
    r, cdim = shard_shape
    h = r // 2
    if col_sharded:
        return ref.at[pl.ds(pl.multiple_of(half * h, BF16_ROWS), h),
                      pl.ds(pl.multiple_of(chip * cdim, LANE), cdim)]
    return ref.at[pl.ds(pl.multiple_of(chip * r + half * h, BF16_ROWS), h), :]


def _gather_plan(shard_shapes, col_sharded, ks):
    def plan(bufs, send_sems, recv_sems):
        x, y, c = _mesh_pos()
        sends, arrivals = [], []
        for ref, k in zip(bufs, ks):
            mine = _half_block(ref, shard_shapes[k], col_sharded[k], 2 * x + y, c)
            for j, (px, py) in enumerate(_other_chips(x, y)):
                landed = _half_block(ref, shard_shapes[k], col_sharded[k], 2 * px + py, c)
                sends.append(_remote(mine, mine, send_sems, recv_sems, 3 * k + j, (px, py, c)))
                arrivals.append(_remote(landed, landed, send_sems, recv_sems, 3 * k + j, (px, py, c)))
        return sends, arrivals
    return plan


def _forward_plan(shard_shapes, col_sharded, ks):
    def plan(bufs, send_sems, recv_sems):
        x, y, c = _mesh_pos()
        sends, arrivals = [], []
        for i, (ref, k) in enumerate(zip(bufs, ks)):
            for j, (px, py) in enumerate(_other_chips(x, y)):
                landed = _half_block(ref, shard_shapes[k], col_sharded[k], 2 * px + py, c)
                theirs = _half_block(ref, shard_shapes[k], col_sharded[k], 2 * px + py, 1 - c)
                sends.append(_remote(landed, landed, send_sems, recv_sems, 3 * i + j, (x, y, 1 - c)))
                arrivals.append(_remote(theirs, theirs, send_sems, recv_sems, 3 * i + j, (x, y, 1 - c)))
        return sends, arrivals
    return plan


def _small_gather(small):
    def body(small_ref, out_ref, send_sems, recv_sems):
        x, y, c = _mesh_pos()
        me = 2 * x + y
        out_ref[me] = small_ref[...]
        copies = []
        for j, (px, py) in enumerate(_other_chips(x, y)):
            cp = _remote(small_ref, out_ref.at[me], send_sems, recv_sems, j, (px, py, c))
            cp.start()
            copies.append(cp)
        for j, (px, py) in enumerate(_other_chips(x, y)):
            _remote(small_ref, out_ref.at[2 * px + py], send_sems, recv_sems, j, (px, py, c)).wait_recv()
        for cp in copies:
            cp.wait_send()

    return pl.pallas_call(
        body,
        out_shape=jax.ShapeDtypeStruct((N_CHIPS,) + small.shape, small.dtype),
        in_specs=[VMEM_SPEC],
        out_specs=VMEM_SPEC,
        scratch_shapes=[pltpu.SemaphoreType.DMA((N_CHIPS - 1,)), pltpu.SemaphoreType.DMA((N_CHIPS - 1,))],
        name="gather_small",
    )(small)


def _as3d(g, col_sharded):
    r, cdim = g.shape
    return g.reshape(1, r, cdim) if col_sharded else g.reshape(N_CHIPS, r // N_CHIPS, cdim)


def _pair_plan(m):
    def plan(bufs, send_sems, recv_sems):
        x, y, c = _mesh_pos()
        copies = []
        for i in range(m):
            h = bufs[i].shape[1] // 2
            src = bufs[i].at[:, pl.ds(pl.multiple_of((1 - c) * h, BF16_ROWS), h), :]
            copies.append(_remote(src, bufs[m + i], send_sems, recv_sems, i, (x, y, 1 - c)))
        return copies, copies
    return plan


def _chip_plan(col_flags):
    m = len(col_flags)

    def plan(bufs, send_sems, recv_sems):
        x, y, c = _mesh_pos()
        copies = []
        for i in range(m):
            land = bufs[m + i]
            width = land.shape[2]
            for j, (px, py) in enumerate(_other_chips(x, y)):
                q = 2 * px + py
                if col_flags[i]:
                    src = bufs[i].at[0, :, pl.ds(pl.multiple_of(q * width, LANE), width)]
                else:
                    src = bufs[i].at[q]
                copies.append(_remote(src, land.at[j], send_sems, recv_sems, 3 * i + j, (px, py, c)))
        return copies, copies
    return plan


def _share_plan(m):
    def plan(bufs, send_sems, recv_sems):
        x, y, c = _mesh_pos()
        sends, arrivals = [], []
        for i in range(m):
            h = bufs[i].shape[0] // 2
            mine = bufs[i].at[pl.ds(pl.multiple_of(c * h, SUBLANE), h), :]
            theirs = bufs[i].at[pl.ds(pl.multiple_of((1 - c) * h, SUBLANE), h), :]
            sends.append(_remote(mine, mine, send_sems, recv_sems, i, (x, y, 1 - c)))
            arrivals.append(_remote(theirs, theirs, send_sems, recv_sems, i, (x, y, 1 - c)))
        return sends, arrivals
    return plan


def _pair_add(g3, other, core):
    a, r, cdim = g3.shape
    h = r // 2
    rb = _pick(h, max(BF16_ROWS, STREAM_BLOCK // cdim), BF16_ROWS)
    nb = h // rb

    def body(core_ref, g_ref, o_ref, out_ref):
        out_ref[...] = (g_ref[...].astype(F32) + o_ref[...].astype(F32)).astype(BF16)

    grid_spec = pltpu.PrefetchScalarGridSpec(
        num_scalar_prefetch=1,
        grid=(a, nb),
        in_specs=[pl.BlockSpec((None, rb, cdim), lambda i, j, core_ref: (i, core_ref[0] * nb + j, 0)),
                  pl.BlockSpec((None, rb, cdim), lambda i, j, core_ref: (i, j, 0))],
        out_specs=pl.BlockSpec((None, rb, cdim), lambda i, j, core_ref: (i, j, 0)),
    )
    return pl.pallas_call(
        body,
        out_shape=jax.ShapeDtypeStruct((a, h, cdim), BF16),
        grid_spec=grid_spec,
        compiler_params=_cparams(("parallel", "parallel"), 2 * rb * cdim * 10),
        name="grad_pair_add",
    )(core, g3, other)


def _small_allreduce(small):
    rows = small.shape[0]
    pad = (-rows) % (2 * SUBLANE)
    if pad:
        small = jnp.pad(small, ((0, pad), (0, 0)))
    h = small.shape[0] // 2
    half_shape = (h, small.shape[1])

    def body(small_ref, out_ref, theirs, by_chip, send_sems, recv_sems):
        x, y, c = _mesh_pos()
        me = 2 * x + y
        sibling = (x, y, 1 - c)
        mine = pl.ds(pl.multiple_of(c * h, SUBLANE), h)
        other = pl.ds(pl.multiple_of((1 - c) * h, SUBLANE), h)
        swap = _remote(small_ref, theirs, send_sems, recv_sems, 0, sibling)
        swap.start()
        swap.wait()
        by_chip[me] = small_ref[mine, :] + theirs[mine, :]
        copies = []
        for j, (px, py) in enumerate(_other_chips(x, y)):
            cp = _remote(by_chip.at[me], by_chip.at[me], send_sems, recv_sems, 1 + j, (px, py, c))
            cp.start()
            copies.append(cp)
        for j, (px, py) in enumerate(_other_chips(x, y)):
            landed = by_chip.at[2 * px + py]
            _remote(landed, landed, send_sems, recv_sems, 1 + j, (px, py, c)).wait_recv()
        total = by_chip[0]
        for q in range(1, N_CHIPS):
            total = total + by_chip[q]
        out_ref[mine, :] = total
        for cp in copies:
            cp.wait_send()
        share = _remote(out_ref.at[mine, :], out_ref.at[mine, :], send_sems, recv_sems, 4, sibling)
        share.start()
        _remote(out_ref.at[other, :], out_ref.at[other, :], send_sems, recv_sems, 4, sibling).wait_recv()
        share.wait_send()

    out = pl.pallas_call(
        body,
        out_shape=jax.ShapeDtypeStruct(small.shape, F32),
        in_specs=[VMEM_SPEC],
        out_specs=VMEM_SPEC,
        scratch_shapes=[pltpu.VMEM(small.shape, F32), pltpu.VMEM((N_CHIPS,) + half_shape, F32),
                        pltpu.SemaphoreType.DMA((5,)), pltpu.SemaphoreType.DMA((5,))],
        compiler_params=pltpu.CompilerParams(
            vmem_limit_bytes=min(VMEM_BUDGET, 8 * _nbytes(small.shape, F32) + (8 << 20))),
        name="grad_small_allreduce",
    )(small)
    return out[:rows]


def _chip_sum(partial, land, where, col_sharded):
    _, h, cdim = land.shape
    rb = _pick(h, max(BF16_ROWS, STREAM_BLOCK // cdim), BF16_ROWS)
    nb = h // rb

    def body(where_ref, own_ref, l_ref, o_ref):
        total = own_ref[...].astype(F32)
        for j in range(N_CHIPS - 1):
            total = total + l_ref[j].astype(F32)
        o_ref[...] = total

    if col_sharded:
        own_map = lambda i, w: (0, i, w[0])
    else:
        own_map = lambda i, w: (w[0], i, 0)
    grid_spec = pltpu.PrefetchScalarGridSpec(
        num_scalar_prefetch=1,
        grid=(nb,),
        in_specs=[pl.BlockSpec((None, rb, cdim), own_map),
                  pl.BlockSpec((N_CHIPS - 1, rb, cdim), lambda i, w: (0, i, 0))],
        out_specs=pl.BlockSpec((rb, cdim), lambda i, w: (w[1] * nb + i, 0)),
    )
    return pl.pallas_call(
        body,
        out_shape=jax.ShapeDtypeStruct((2 * h, cdim), F32),
        grid_spec=grid_spec,
        compiler_params=_cparams(("parallel",), 2 * rb * cdim * 12),
        name="grad_chip_sum",
    )(where, partial, land)


def _adamw(w, g, m, v, name, dep=None):
    r, cdim = w.shape
    rb = _pick(r, max(SUBLANE, (STREAM_BLOCK // 8) // cdim), SUBLANE)
    c1 = 1.0 - ADAM_B1 ** ADAM_STEP
    c2 = 1.0 - ADAM_B2 ** ADAM_STEP

    def body(w_ref, g_ref, m_ref, v_ref, go_ref, d_ref, mo_ref, vo_ref):
        gv = g_ref[...]
        mn = ADAM_B1 * m_ref[...] + (1.0 - ADAM_B1) * gv
        vn = ADAM_B2 * v_ref[...] + (1.0 - ADAM_B2) * (gv * gv)
        m_hat = mn / c1
        v_hat = vn / c2
        d_ref[...] = -ADAM_LR * (m_hat / (jnp.sqrt(v_hat) + ADAM_EPS) + ADAM_WD * w_ref[...])
        go_ref[...] = gv
        mo_ref[...] = mn
        vo_ref[...] = vn

    blk = pl.BlockSpec((rb, cdim), lambda i: (i, 0))
    shape = jax.ShapeDtypeStruct((r, cdim), F32)
    body, in_specs, operands = _dep_args(body, [blk] * 4, [w, g, m, v], dep)
    return pl.pallas_call(
        body,
        out_shape=[shape] * 4,
        grid=(r // rb,),
        in_specs=in_specs,
        out_specs=[blk] * 4,
        compiler_params=_cparams(("parallel",), 2 * rb * cdim * 4 * 8),
        name=name,
    )(*operands)


def _pack(arrays):
    tile = SUBLANE * LANE
    pieces = []
    for arr in arrays:
        flat = arr.reshape(-1)
        pad = (-flat.shape[0]) % tile
        if pad:
            flat = jnp.concatenate([flat, jnp.zeros((pad,), flat.dtype)])
        pieces.append(flat)
    return jnp.concatenate(pieces).reshape(-1, LANE)


def _unpack(packed, shapes):
    tile = SUBLANE * LANE
    flat = packed.reshape(-1)
    out, off = [], 0
    for shp in shapes:
        size = math.prod(shp)
        out.append(flat[off:off + size].reshape(shp))
        off += size + ((-size) % tile)
    return out


def _block_diag_groups(w, per_group):
    hcount, hd, _ = w.shape
    ng = hcount // per_group
    w4 = w.reshape(ng, per_group, hd, hd)
    eye = jnp.eye(per_group, dtype=w.dtype)
    bd = w4[:, :, :, None, :] * eye[None, :, None, :, None]
    return bd.reshape(ng, per_group * hd, per_group * hd).astype(BF16)


def _diag_blocks(wbd, per_group, hd):
    ng = wbd.shape[0]
    w5 = wbd.reshape(ng, per_group, hd, per_group, hd)
    blocks = [w5[:, i, :, i, :] for i in range(per_group)]
    return jnp.stack(blocks, axis=1).reshape(ng * per_group, hd, hd)


def kernel(x, g_mix, w_in, lru_conv_w, lru_conv_b, lru_wa, lru_ba, lru_wx, lru_bx, lru_lambda, lru_w_out, sc_conv_w, sc_w_out, w_o, g_ffn, ffn_w_up, ffn_conv_w, ffn_w_down, g_final, loss_target, m_g_mix, m_w_in, m_lru_conv_w, m_lru_conv_b, m_lru_wa, m_lru_ba, m_lru_wx, m_lru_bx, m_lru_lambda, m_lru_w_out, m_sc_conv_w, m_sc_w_out, m_w_o, m_g_ffn, m_ffn_w_up, m_ffn_conv_w, m_ffn_w_down, m_g_final, v_g_mix, v_w_in, v_lru_conv_w, v_lru_conv_b, v_lru_wa, v_lru_ba, v_lru_wx, v_lru_bx, v_lru_lambda, v_lru_w_out, v_sc_conv_w, v_sc_w_out, v_w_o, v_g_ffn, v_ffn_w_up, v_ffn_conv_w, v_ffn_w_down, v_g_final):
    seq, d_model = x.shape[1], x.shape[2]
    heads, head_dim, _ = lru_wa.shape
    d_lru = heads * head_dim
    d_sc = sc_w_out.shape[0]
    d_ff = ffn_w_down.shape[0] * N_CHIPS
    assert x.shape[0] == 1 and w_in.shape[1] * N_CHIPS == 2 * d_lru + 3 * d_sc + 2 * d_model
    xs = x.reshape(seq, d_model)
    target = loss_target.reshape(seq, d_model)

    chip = 2 * lax.axis_index("x") + lax.axis_index("y")
    core = lax.axis_index("c").astype(jnp.int32).reshape(1)

    big_w = [w_in, lru_w_out, sc_w_out, w_o, ffn_w_up, ffn_w_down]
    big_m = [m_w_in, m_lru_w_out, m_sc_w_out, m_w_o, m_ffn_w_up, m_ffn_w_down]
    big_v = [v_w_in, v_lru_w_out, v_sc_w_out, v_w_o, v_ffn_w_up, v_ffn_w_down]
    col_sharded = [True, True, True, False, True, False]
    conv_shards = [lru_conv_w, sc_conv_w, ffn_conv_w]
    conv_pack = jnp.concatenate(
        [jnp.pad(w, ((0, SUBLANE - w.shape[0]), (0, 0))) for w in conv_shards], axis=1)
    big_names = ["w_in", "lru_w_out", "sc_w_out", "w_o", "ffn_w_up", "ffn_w_down"]
    chip_arr = chip.astype(jnp.int32).reshape(1)
    placed = [_cast_place(big_w[0], chip_arr, col_sharded[0], "cast_" + big_names[0])]
    conv_all = _small_gather(conv_pack)
    shard_shapes = [w.shape for w in big_w]
    n_big = len(big_w)

    def gather_start(ks, after, tag):
        send, recv, bufs, token = _exchange_start(
            "gather_start_" + tag, [placed[k] for k in ks], 3 * n_big,
            _gather_plan(shard_shapes, col_sharded, ks), after=after)
        return (send, recv, dict(zip(ks, bufs))), token

    def arrived(state, ks, after, tag):
        send, recv, bufs = state
        got = _exchange_wait("gather_wait_" + tag, [bufs[k] for k in ks], send, recv, after,
                             _gather_plan(shard_shapes, col_sharded, ks))
        return _exchange("gather_forward_" + tag, got, 3 * len(ks), _forward_plan(shard_shapes, col_sharded, ks))

    def arrived_behind(state, ks, after, tag):
        send, recv, bufs = state
        got = _exchange_wait("gather_wait_" + tag, [bufs[k] for k in ks], send, recv, after,
                             _gather_plan(shard_shapes, col_sharded, ks))
        plan = _forward_plan(shard_shapes, col_sharded, ks)
        send, recv, got, token = _exchange_start("gather_forward_start_" + tag, got, 3 * len(ks), plan)
        return (send, recv, got, plan, tag), token

    def forwarded(state, after):
        send, recv, got, plan, tag = state
        return _exchange_wait("gather_forward_wait_" + tag, got, send, recv, after, plan)

    conv_full, off = [], 0
    for w in conv_shards:
        kw, nq = w.shape
        piece = conv_all[:, :kw, off:off + nq]
        conv_full.append(piece.transpose(1, 0, 2).reshape(kw, N_CHIPS * nq))
        off += nq
    lcw, scw, fcw = conv_full

    per_group = max(1, min(heads, 256 // head_dim))
    gc = per_group * head_dim
    wa_bd = _block_diag_groups(lru_wa, per_group)
    wx_bd = _block_diag_groups(lru_wx, per_group)
    tc = _pick(seq, 256, SUBLANE)
    cb_sc = _pick(d_sc, 512)
    cb_ff = _pick(d_ff, 128)
    tc_ff = _pick(seq, 2048, SUBLANE)
    tc_lru = _pick(seq, 1024, SUBLANE)
    tc_merge = _pick(seq, 512, SUBLANE)
    col_sc = 2 * d_lru
    col_gates = 2 * d_lru + 3 * d_sc

    first, token = gather_start([0], conv_all, "in")
    for k in range(1, n_big):
        placed.append(_cast_place(big_w[k], chip_arr, col_sharded[k], "cast_" + big_names[k], dep=token))
        token = placed[-1]
    h1 = _rms_fwd(xs, g_mix, "rms_mix", dep=token)
    (win_b,) = arrived(first, [0], h1, "in")
    rest, token = gather_start([1, 2, 3, 4, 5], win_b, "rest")
    p = _mm(h1, win_b, "nn", F32, name="mm_in", dep=token)
    mix, token = arrived_behind(rest, [1, 2, 3], p, "mix")
    y_lru_pre, hseq = _lru_fwd(p, lcw, lru_conv_b, wa_bd, lru_ba, wx_bd, lru_bx, lru_lambda, d_lru, gc, tc_lru,
                               dep=token)
    y_sc_pre = _sc_fwd(p, scw, col_sc, d_sc, cb_sc, tc)
    wlo_b, wso_b, wo_b = forwarded(mix, y_sc_pre)
    y_lru = _mm(y_lru_pre, wlo_b, "nn", BF16, name="mm_lru_out")
    y_sc = _mm(y_sc_pre, wso_b, "nn", BF16, name="mm_sc_out")
    merged = _merge_fwd(p, y_lru, y_sc, col_gates, tc_merge)
    x2 = _mm(merged, wo_b, "nn", F32, res=xs, name="mm_o")
    (wup_b,) = arrived(rest, [4], x2, "up")
    h2 = _rms_fwd(x2, g_ffn, "rms_ffn")
    up = _mm(h2, wup_b, "nn", F32, name="mm_up")
    down, token = arrived_behind(rest, [5], up, "down")
    act = _ffn_act_fwd(up, fcw, d_ff, cb_ff, tc_ff, dep=token)
    (wdn_b,) = forwarded(down, act)
    x3 = _mm(act, wdn_b, "nn", F32, res=x2, name="mm_down")
    loss_part, dx3, dx3b, dg_final = _loss_head(x3, g_final, target)

    where = jnp.concatenate([chip_arr, core])

    def reduce_start(grads, flags, tag):
        views = [_as3d(g, cs) for g, cs in zip(grads, flags)]
        lands = [lax.empty((v.shape[0], v.shape[1] // 2, v.shape[2]), v.dtype) for v in views]
        send, recv, bufs, token = _exchange_start("grad_pair_start_" + tag, views + lands, len(views),
                                                  _pair_plan(len(views)))
        return (send, recv, bufs, flags, tag), token

    def reduce_mid(state, after):
        send, recv, bufs, flags, tag = state
        m = len(flags)
        bufs = _exchange_wait("grad_pair_wait_" + tag, bufs, send, recv, after, _pair_plan(m))
        partials = [_pair_add(bufs[i], bufs[m + i], core) for i in range(m)]
        lands = []
        for pz, cs in zip(partials, flags):
            _, h, cdim = pz.shape
            lands.append(lax.empty((N_CHIPS - 1, h, cdim // N_CHIPS if cs else cdim), BF16))
        send, recv, bufs, token = _exchange_start("grad_chip_start_" + tag, partials + lands, 3 * m,
                                                  _chip_plan(flags))
        return (send, recv, bufs, flags, tag), token

    def reduce_end(state, after):
        send, recv, bufs, flags, tag = state
        m = len(flags)
        bufs = _exchange_wait("grad_chip_wait_" + tag, bufs, send, recv, after, _chip_plan(flags))
        return [_chip_sum(bufs[i], bufs[m + i], where, flags[i]) for i in range(m)]

    g_wdn = _mm(act, dx3b, "tn", F32, name="mm_down_dw")
    red_down, token = reduce_start([g_wdn], [False], "down")
    dact = _mm(dx3b, wdn_b, "nt", F32, name="mm_down_dx", dep=token)
    red_down, token = reduce_mid(red_down, dact)
    dup, dfcw_g, dfcw_v = _ffn_act_bwd(up, dact, fcw, d_ff, cb_ff, tc_ff, dep=token)
    g_wup = _mm(h2, dup, "tn", BF16, name="mm_up_dw", slabs=[0, 1])
    red_up, token = reduce_start([g_wup], [True], "up")
    dh2 = _mm(dup, wup_b, "nt", F32, name="mm_up_dx", dep=token, slabs=[0, 1])
    red_up, token = reduce_mid(red_up, dh2)
    dx2, dx2b, dg_ffn = _rms_bwd(x2, g_ffn, dh2, dx3, "rms_ffn_bwd", True, dep=token)
    g_wo = _mm(merged, dx2b, "tn", BF16, name="mm_o_dw")
    dmerged = _mm(dx2b, wo_b, "nt", BF16, name="mm_o_dx")
    slab_w = d_lru
    assert d_sc == slab_w and d_model % slab_w == 0 and col_gates % slab_w == 0
    n_gate = d_model // slab_w
    gate0 = col_gates // slab_w
    dp_slabs = [gate0 + kind * n_gate + j for j in range(n_gate) for kind in (0, 1)] + [0, 1, 2, 3, 4]
    dp, dyl, dys = _merge_bwd(p, y_lru, y_sc, dmerged, col_gates, tc_merge, len(dp_slabs))
    assert dp.shape[2] == slab_w
    g_wlo = _mm(y_lru_pre, dyl, "tn", BF16, name="mm_lru_out_dw")
    g_wso = _mm(y_sc_pre, dys, "tn", BF16, name="mm_sc_out_dw")
    red_mix, token = reduce_start([g_wlo, g_wso, g_wo], [True, True, False], "mix")
    dylp = _mm(dyl, wlo_b, "nt", F32, name="mm_lru_out_dx", dep=token)
    dysp = _mm(dys, wso_b, "nt", F32, name="mm_sc_out_dx")
    red_mix, token = reduce_mid(red_mix, dysp)
    dp, dlcw, dlcb, dwa_bd, dba, dwx_bd, dbx, dlam = _lru_bwd(
        p, hseq, dylp, lcw, lru_conv_b, wa_bd, lru_ba, wx_bd, lru_bx, lru_lambda, d_lru, gc, tc_lru,
        dp, 2 * n_gate, dep=token)
    dp, dscw = _sc_bwd(p, dysp, scw, col_sc, d_sc, cb_sc, tc, dp, 2 * n_gate + 2)
    g_win = _mm(h1, dp, "tn", BF16, name="mm_in_dw", slabs=dp_slabs)
    red_in, token = reduce_start([g_win], [True], "in")
    dh1 = _mm(dp, win_b, "nt", F32, name="mm_in_dx", dep=token, slabs=dp_slabs)
    grad_x, dg_mix = _rms_bwd(xs, g_mix, dh1, dx2, "rms_mix_bwd", False)

    small_g = [dg_mix, dlcw, dlcb, _diag_blocks(dwa_bd, per_group, head_dim), dba,
               _diag_blocks(dwx_bd, per_group, head_dim), dbx, dlam, dscw, dg_ffn,
               jnp.concatenate([dfcw_g, dfcw_v], axis=1), dg_final]
    small_shapes = [a.shape for a in small_g]
    small_sum = _small_allreduce(_pack(small_g))
    red_in, token = reduce_mid(red_in, small_sum)
    (h_wdn,) = reduce_end(red_down, token)
    (h_wup,) = reduce_end(red_up, token)
    h_wlo, h_wso, h_wo = reduce_end(red_mix, token)
    s_wlo, s_wso, s_wo, s_wup, s_wdn = _exchange("grad_share_a", [h_wlo, h_wso, h_wo, h_wup, h_wdn], 5,
                                                 _share_plan(5))
    early = {1: s_wlo, 2: s_wso, 3: s_wo, 4: s_wup, 5: s_wdn}
    big_out = [None] * n_big
    last = None
    for k, g in early.items():
        big_out[k] = _adamw(big_w[k], g, big_m[k], big_v[k], "adamw_" + big_names[k], dep=last)
        last = big_out[k][1]
    (h_win,) = reduce_end(red_in, last)
    (s_win,) = _exchange("grad_share_b", [h_win], 1, _share_plan(1))
    big_out[0] = _adamw(big_w[0], s_win, big_m[0], big_v[0], "adamw_" + big_names[0])
    sg = _unpack(small_sum, small_shapes)
    for idx in (1, 8, 10):
        nq = sg[idx].shape[1] // N_CHIPS
        sg[idx] = lax.dynamic_slice_in_dim(sg[idx], chip * nq, nq, axis=1)
    small_w = [g_mix, lru_conv_w, lru_conv_b, lru_wa, lru_ba, lru_wx, lru_bx, lru_lambda, sc_conv_w,
               g_ffn, ffn_conv_w, g_final]
    small_m = [m_g_mix, m_lru_conv_w, m_lru_conv_b, m_lru_wa, m_lru_ba, m_lru_wx, m_lru_bx, m_lru_lambda,
               m_sc_conv_w, m_g_ffn, m_ffn_conv_w, m_g_final]
    small_v = [v_g_mix, v_lru_conv_w, v_lru_conv_b, v_lru_wa, v_lru_ba, v_lru_wx, v_lru_bx, v_lru_lambda,
               v_sc_conv_w, v_g_ffn, v_ffn_conv_w, v_g_final]
    sg = [g.reshape(w.shape) for g, w in zip(sg, small_w)]
    w_shapes = [w.shape for w in small_w]
    packed = _adamw(_pack(small_w), _pack(sg), _pack(small_m), _pack(small_v), "adamw_small")
    small_out = [_unpack(pk, w_shapes) for pk in packed]

    order = [(0, 0), (1, 0), (0, 1), (0, 2), (0, 3), (0, 4), (0, 5), (0, 6), (0, 7), (1, 1), (0, 8), (1, 2),
             (1, 3), (0, 9), (1, 4), (0, 10), (1, 5), (0, 11)]
    by_kind = []
    for kind in range(4):
        by_kind.append([big_out[i][kind] if is_big else small_out[kind][i] for is_big, i in order])
    loss = lax.psum(loss_part[0, 0], ("x", "y", "c"))
    return (loss, grad_x.reshape(x.shape), *by_kind[0], *by_kind[1], *by_kind[2], *by_kind[3])
```
